```python
import jax, jax.numpy as jnp
from jax import lax
import numpy as np

D_MODEL = 2048
BATCH = 8
SEQ = 2048
DEPTH = 1

N_META = 16
BLOCK_Q = 128
CONV_CH = 2048
CONV_WIDTH = 31
N_HEADS = 16
Q_LORA = 512
KV_LORA = 512
QK_NOPE = 128
QK_ROPE = 64
V_HEAD = 128
ROPE_THETA = 10000.0
D_FF = -(-8 * D_MODEL // (3 * 256)) * 256
EPS = 1e-6
SPLITS = (2 * CONV_CH, Q_LORA, KV_LORA, QK_ROPE, 2 * D_MODEL)
IN_COLS = sum(SPLITS)
SPLIT_IDX = tuple(int(v) for v in np.cumsum(SPLITS)[:-1])

kernel_name = "hybrid_conformer_conv_mla_gated_block"


def rmsnorm(x, g):
    xf = x.astype(jnp.float32)
    y = xf * lax.rsqrt(jnp.mean(xf * xf, axis=-1, keepdims=True) + EPS)
    return y.astype(x.dtype) * g


def layernorm(x, g, b):
    xf = x.astype(jnp.float32)
    mu = jnp.mean(xf, axis=-1, keepdims=True)
    var = jnp.mean(jnp.square(xf - mu), axis=-1, keepdims=True)
    return ((xf - mu) * lax.rsqrt(var + EPS)).astype(x.dtype) * g + b


def rope(x, cos, sin):
    x1, x2 = jnp.split(x, 2, axis=-1)
    return jnp.concatenate([x1 * cos - x2 * sin, x1 * sin + x2 * cos], axis=-1)


def causal_depthwise_conv(x, w):
    return lax.conv_general_dilated(
        x, w[:, None, :], window_strides=(1,), padding=[(CONV_WIDTH - 1, 0)],
        dimension_numbers=("NWC", "WIO", "NWC"), feature_group_count=x.shape[-1])


def mla_attention(q_nope, q_rope, k_nope, k_rope, v):
    b, lp = q_nope.shape[0], q_nope.shape[1]
    n_blocks = lp // BLOCK_Q
    scale = (QK_NOPE + QK_ROPE) ** -0.5
    kpos = jnp.arange(lp)

    def one_block(i):
        start = i * BLOCK_Q
        qn = lax.dynamic_slice_in_dim(q_nope, start, BLOCK_Q, axis=1)
        qr = lax.dynamic_slice_in_dim(q_rope, start, BLOCK_Q, axis=1)
        s = (jnp.einsum("bqhd,bkhd->bhqk", qn, k_nope)
             + jnp.einsum("bqhr,bkr->bhqk", qr, k_rope)).astype(jnp.float32) * scale
        qpos = start + jnp.arange(BLOCK_Q)
        s = jnp.where(kpos[None, :] <= qpos[:, None], s, -jnp.inf)
        p = jax.nn.softmax(s, axis=-1).astype(v.dtype)
        return jnp.einsum("bhqk,bkhd->bqhd", p, v)

    o = lax.map(one_block, jnp.arange(n_blocks))
    return jnp.moveaxis(o, 0, 1).reshape(b, lp, N_HEADS * V_HEAD)


def _fwd_setup_inputs(seed: int = 0) -> dict:
    key = jax.random.key(seed)
    ks = iter(jax.random.split(key, 32))
    f32 = jnp.float32

    def w(shape, fan_in):
        return jax.random.normal(next(ks), shape, f32) * (fan_in ** -0.5)

    def gain(shape):
        return 1.0 + 0.02 * jax.random.normal(next(ks), shape, f32)

    def bias(shape):
        return 0.02 * jax.random.normal(next(ks), shape, f32)

    Ld = DEPTH
    return {
        "x": jax.random.normal(next(ks), (BATCH, SEQ, D_MODEL), f32),
        "meta_tokens": jax.random.normal(next(ks), (N_META, D_MODEL), f32),
        "g_mix": gain((Ld, D_MODEL)),
        "w_in": w((Ld, D_MODEL, IN_COLS), D_MODEL),
        "b_glu": bias((Ld, 2 * CONV_CH)),
        "b_gate": bias((Ld, 2 * D_MODEL)),
        "w_dw": w((Ld, CONV_WIDTH, CONV_CH), CONV_WIDTH),
        "b_dw": bias((Ld, CONV_CH)),
        "g_conv_ln": gain((Ld, CONV_CH)),
        "b_conv_ln": bias((Ld, CONV_CH)),
        "w_conv_out": w((Ld, CONV_CH, D_MODEL), CONV_CH),
        "b_conv_out": bias((Ld, D_MODEL)),
        "g_q_lora": gain((Ld, Q_LORA)),
        "w_uq": w((Ld, Q_LORA, N_HEADS * (QK_NOPE + QK_ROPE)), Q_LORA),
        "g_kv_lora": gain((Ld, KV_LORA)),
        "w_uk": w((Ld, KV_LORA, N_HEADS * QK_NOPE), KV_LORA),
        "w_uv": w((Ld, KV_LORA, N_HEADS * V_HEAD), KV_LORA),
        "w_attn_out": w((Ld, N_HEADS * V_HEAD, D_MODEL), N_HEADS * V_HEAD),
        "w_out": w((Ld, D_MODEL, D_MODEL), D_MODEL),
        "g_ffn": gain((Ld, D_MODEL)),
        "w_ffn_gate": w((Ld, D_MODEL, D_FF), D_MODEL),
        "w_ffn_up": w((Ld, D_MODEL, D_FF), D_MODEL),
        "w_ffn_down": w((Ld, D_FF, D_MODEL), D_FF),
        "g_final": gain((D_MODEL,)),
    }


def _fwd_reference(x, meta_tokens, g_mix, w_in, b_glu, b_gate, w_dw, b_dw, g_conv_ln, b_conv_ln,
              w_conv_out, b_conv_out, g_q_lora, w_uq, g_kv_lora, w_uk, w_uv, w_attn_out,
              w_out, g_ffn, w_ffn_gate, w_ffn_up, w_ffn_down, g_final):
    b, seq, d = x.shape
    length = N_META + seq
    lp = -(-length // BLOCK_Q) * BLOCK_Q
    meta = jnp.broadcast_to(meta_tokens[None].astype(x.dtype), (b, N_META, d))
    h = jnp.concatenate([meta, x], axis=1)
    h = jnp.pad(h, ((0, 0), (0, lp - length), (0, 0)))

    pos = jnp.arange(lp, dtype=jnp.float32)
    inv_freq = ROPE_THETA ** (-jnp.arange(0, QK_ROPE, 2, dtype=jnp.float32) / QK_ROPE)
    ang = pos[:, None] * inv_freq[None, :]
    cos, sin = jnp.cos(ang).astype(h.dtype), jnp.sin(ang).astype(h.dtype)

    for l in range(DEPTH):
        u = rmsnorm(h, g_mix[l])
        z = u @ w_in[l]
        z_glu, z_cq, z_ckv, z_kr, z_gate = jnp.split(z, SPLIT_IDX, axis=-1)

        za, zb = jnp.split(z_glu + b_glu[l], 2, axis=-1)
        c = za * jax.nn.sigmoid(zb)
        c = causal_depthwise_conv(c, w_dw[l]) + b_dw[l]
        c = jax.nn.silu(layernorm(c, g_conv_ln[l], b_conv_ln[l]))
        y_conv = c @ w_conv_out[l] + b_conv_out[l]

        cq = rmsnorm(z_cq, g_q_lora[l])
        q = (cq @ w_uq[l]).reshape(b, lp, N_HEADS, QK_NOPE + QK_ROPE)
        q_nope = q[..., :QK_NOPE]
        q_rope = rope(q[..., QK_NOPE:], cos[:, None, :], sin[:, None, :])
        ckv = rmsnorm(z_ckv, g_kv_lora[l])
        k_nope = (ckv @ w_uk[l]).reshape(b, lp, N_HEADS, QK_NOPE)
        v = (ckv @ w_uv[l]).reshape(b, lp, N_HEADS, V_HEAD)
        k_rope = rope(z_kr, cos, sin)
        y_attn = mla_attention(q_nope, q_rope, k_nope, k_rope, v) @ w_attn_out[l]

        g_c, g_a = jnp.split(jax.nn.sigmoid(z_gate + b_gate[l]), 2, axis=-1)
        h = h + (g_c * y_conv + g_a * y_attn) @ w_out[l]

        hn = rmsnorm(h, g_ffn[l])
        h = h + (jax.nn.silu(hn @ w_ffn_gate[l]) * (hn @ w_ffn_up[l])) @ w_ffn_down[l]

    h = rmsnorm(h, g_final)
    return h[:, N_META:N_META + seq]


import jax as _jax
import jax.numpy as _jnp

TWIN_FORMAT = 'train_step'
FWD_PARAMS = ['x', 'meta_tokens', 'g_mix', 'w_in', 'b_glu', 'b_gate', 'w_dw', 'b_dw', 'g_conv_ln', 'b_conv_ln', 'w_conv_out', 'b_conv_out', 'g_q_lora', 'w_uq', 'g_kv_lora', 'w_uk', 'w_uv', 'w_attn_out', 'w_out', 'g_ffn', 'w_ffn_gate', 'w_ffn_up', 'w_ffn_down', 'g_final']
TWIN_WEIGHTS = ['meta_tokens', 'g_mix', 'w_in', 'b_glu', 'b_gate', 'w_dw', 'b_dw', 'g_conv_ln', 'b_conv_ln', 'w_conv_out', 'b_conv_out', 'g_q_lora', 'w_uq', 'g_kv_lora', 'w_uk', 'w_uv', 'w_attn_out', 'w_out', 'g_ffn', 'w_ffn_gate', 'w_ffn_up', 'w_ffn_down', 'g_final']
TWIN_DIFF_INPUT = 'x'
TWIN_INPUTS = ['x', 'meta_tokens', 'g_mix', 'w_in', 'b_glu', 'b_gate', 'w_dw', 'b_dw', 'g_conv_ln', 'b_conv_ln', 'w_conv_out', 'b_conv_out', 'g_q_lora', 'w_uq', 'g_kv_lora', 'w_uk', 'w_uv', 'w_attn_out', 'w_out', 'g_ffn', 'w_ffn_gate', 'w_ffn_up', 'w_ffn_down', 'g_final', 'loss_target', 'm_meta_tokens', 'm_g_mix', 'm_w_in', 'm_b_glu', 'm_b_gate', 'm_w_dw', 'm_b_dw', 'm_g_conv_ln', 'm_b_conv_ln', 'm_w_conv_out', 'm_b_conv_out', 'm_g_q_lora', 'm_w_uq', 'm_g_kv_lora', 'm_w_uk', 'm_w_uv', 'm_w_attn_out', 'm_w_out', 'm_g_ffn', 'm_w_ffn_gate', 'm_w_ffn_up', 'm_w_ffn_down', 'm_g_final', 'v_meta_tokens', 'v_g_mix', 'v_w_in', 'v_b_glu', 'v_b_gate', 'v_w_dw', 'v_b_dw', 'v_g_conv_ln', 'v_b_conv_ln', 'v_w_conv_out', 'v_b_conv_out', 'v_g_q_lora', 'v_w_uq', 'v_g_kv_lora', 'v_w_uk', 'v_w_uv', 'v_w_attn_out', 'v_w_out', 'v_g_ffn', 'v_w_ffn_gate', 'v_w_ffn_up', 'v_w_ffn_down', 'v_g_final']
TWIN_OUTPUTS = ['loss', 'grad_x', 'grad_meta_tokens', 'grad_g_mix', 'grad_w_in', 'grad_b_glu', 'grad_b_gate', 'grad_w_dw', 'grad_b_dw', 'grad_g_conv_ln', 'grad_b_conv_ln', 'grad_w_conv_out', 'grad_b_conv_out', 'grad_g_q_lora', 'grad_w_uq', 'grad_g_kv_lora', 'grad_w_uk', 'grad_w_uv', 'grad_w_attn_out', 'grad_w_out', 'grad_g_ffn', 'grad_w_ffn_gate', 'grad_w_ffn_up', 'grad_w_ffn_down', 'grad_g_final', 'delta_meta_tokens', 'delta_g_mix', 'delta_w_in', 'delta_b_glu', 'delta_b_gate', 'delta_w_dw', 'delta_b_dw', 'delta_g_conv_ln', 'delta_b_conv_ln', 'delta_w_conv_out', 'delta_b_conv_out', 'delta_g_q_lora', 'delta_w_uq', 'delta_g_kv_lora', 'delta_w_uk', 'delta_w_uv', 'delta_w_attn_out', 'delta_w_out', 'delta_g_ffn', 'delta_w_ffn_gate', 'delta_w_ffn_up', 'delta_w_ffn_down', 'delta_g_final', 'new_m_meta_tokens', 'new_m_g_mix', 'new_m_w_in', 'new_m_b_glu', 'new_m_b_gate', 'new_m_w_dw', 'new_m_b_dw', 'new_m_g_conv_ln', 'new_m_b_conv_ln', 'new_m_w_conv_out', 'new_m_b_conv_out', 'new_m_g_q_lora', 'new_m_w_uq', 'new_m_g_kv_lora', 'new_m_w_uk', 'new_m_w_uv', 'new_m_w_attn_out', 'new_m_w_out', 'new_m_g_ffn', 'new_m_w_ffn_gate', 'new_m_w_ffn_up', 'new_m_w_ffn_down', 'new_m_g_final', 'new_v_meta_tokens', 'new_v_g_mix', 'new_v_w_in', 'new_v_b_glu', 'new_v_b_gate', 'new_v_w_dw', 'new_v_b_dw', 'new_v_g_conv_ln', 'new_v_b_conv_ln', 'new_v_w_conv_out', 'new_v_b_conv_out', 'new_v_g_q_lora', 'new_v_w_uq', 'new_v_g_kv_lora', 'new_v_w_uk', 'new_v_w_uv', 'new_v_w_attn_out', 'new_v_w_out', 'new_v_g_ffn', 'new_v_w_ffn_gate', 'new_v_w_ffn_up', 'new_v_w_ffn_down', 'new_v_g_final']
TWIN_LEAF_KINDS = {'loss': 'loss', 'grad_x': 'grad_x', 'grad_meta_tokens': 'grad_w', 'grad_g_mix': 'grad_w', 'grad_w_in': 'grad_w', 'grad_b_glu': 'grad_w', 'grad_b_gate': 'grad_w', 'grad_w_dw': 'grad_w', 'grad_b_dw': 'grad_w', 'grad_g_conv_ln': 'grad_w', 'grad_b_conv_ln': 'grad_w', 'grad_w_conv_out': 'grad_w', 'grad_b_conv_out': 'grad_w', 'grad_g_q_lora': 'grad_w', 'grad_w_uq': 'grad_w', 'grad_g_kv_lora': 'grad_w', 'grad_w_uk': 'grad_w', 'grad_w_uv': 'grad_w', 'grad_w_attn_out': 'grad_w', 'grad_w_out': 'grad_w', 'grad_g_ffn': 'grad_w', 'grad_w_ffn_gate': 'grad_w', 'grad_w_ffn_up': 'grad_w', 'grad_w_ffn_down': 'grad_w', 'grad_g_final': 'grad_w', 'delta_meta_tokens': 'delta_w', 'delta_g_mix': 'delta_w', 'delta_w_in': 'delta_w', 'delta_b_glu': 'delta_w', 'delta_b_gate': 'delta_w', 'delta_w_dw': 'delta_w', 'delta_b_dw': 'delta_w', 'delta_g_conv_ln': 'delta_w', 'delta_b_conv_ln': 'delta_w', 'delta_w_conv_out': 'delta_w', 'delta_b_conv_out': 'delta_w', 'delta_g_q_lora': 'delta_w', 'delta_w_uq': 'delta_w', 'delta_g_kv_lora': 'delta_w', 'delta_w_uk': 'delta_w', 'delta_w_uv': 'delta_w', 'delta_w_attn_out': 'delta_w', 'delta_w_out': 'delta_w', 'delta_g_ffn': 'delta_w', 'delta_w_ffn_gate': 'delta_w', 'delta_w_ffn_up': 'delta_w', 'delta_w_ffn_down': 'delta_w', 'delta_g_final': 'delta_w', 'new_m_meta_tokens': 'new_m', 'new_m_g_mix': 'new_m', 'new_m_w_in': 'new_m', 'new_m_b_glu': 'new_m', 'new_m_b_gate': 'new_m', 'new_m_w_dw': 'new_m', 'new_m_b_dw': 'new_m', 'new_m_g_conv_ln': 'new_m', 'new_m_b_conv_ln': 'new_m', 'new_m_w_conv_out': 'new_m', 'new_m_b_conv_out': 'new_m', 'new_m_g_q_lora': 'new_m', 'new_m_w_uq': 'new_m', 'new_m_g_kv_lora': 'new_m', 'new_m_w_uk': 'new_m', 'new_m_w_uv': 'new_m', 'new_m_w_attn_out': 'new_m', 'new_m_w_out': 'new_m', 'new_m_g_ffn': 'new_m', 'new_m_w_ffn_gate': 'new_m', 'new_m_w_ffn_up': 'new_m', 'new_m_w_ffn_down': 'new_m', 'new_m_g_final': 'new_m', 'new_v_meta_tokens': 'new_v', 'new_v_g_mix': 'new_v', 'new_v_w_in': 'new_v', 'new_v_b_glu': 'new_v', 'new_v_b_gate': 'new_v', 'new_v_w_dw': 'new_v', 'new_v_b_dw': 'new_v', 'new_v_g_conv_ln': 'new_v', 'new_v_b_conv_ln': 'new_v', 'new_v_w_conv_out': 'new_v', 'new_v_b_conv_out': 'new_v', 'new_v_g_q_lora': 'new_v', 'new_v_w_uq': 'new_v', 'new_v_g_kv_lora': 'new_v', 'new_v_w_uk': 'new_v', 'new_v_w_uv': 'new_v', 'new_v_w_attn_out': 'new_v', 'new_v_w_out': 'new_v', 'new_v_g_ffn': 'new_v', 'new_v_w_ffn_gate': 'new_v', 'new_v_w_ffn_up': 'new_v', 'new_v_w_ffn_down': 'new_v', 'new_v_g_final': 'new_v'}


def _forward(args):
    return _fwd_reference(*[args[k] for k in FWD_PARAMS])


def _output_shape():
    out = _jax.eval_shape(lambda: _forward(_fwd_setup_inputs(0)))
    return out.shape, out.dtype

N_MICROBATCH = 1
ADAM_LR = 0.001
ADAM_B1 = 0.9
ADAM_B2 = 0.999
ADAM_EPS = 1e-08
ADAM_WD = 0.01
ADAM_STEP = 10
PER_EXAMPLE_BATCH_AXIS = {'x': 0, 'loss_target': 0}
SHARED_INPUTS = []
_WEIGHT_DTYPES = {'meta_tokens': _jnp.float32, 'g_mix': _jnp.float32, 'w_in': _jnp.float32, 'b_glu': _jnp.float32, 'b_gate': _jnp.float32, 'w_dw': _jnp.float32, 'b_dw': _jnp.float32, 'g_conv_ln': _jnp.float32, 'b_conv_ln': _jnp.float32, 'w_conv_out': _jnp.float32, 'b_conv_out': _jnp.float32, 'g_q_lora': _jnp.float32, 'w_uq': _jnp.float32, 'g_kv_lora': _jnp.float32, 'w_uk': _jnp.float32, 'w_uv': _jnp.float32, 'w_attn_out': _jnp.float32, 'w_out': _jnp.float32, 'g_ffn': _jnp.float32, 'w_ffn_gate': _jnp.float32, 'w_ffn_up': _jnp.float32, 'w_ffn_down': _jnp.float32, 'g_final': _jnp.float32}
MOMENT_SCALE = {'meta_tokens': 1.749795e-03, 'g_mix': 2.697226e-02, 'w_in': 1.286991e-02, 'b_glu': 1.814976e-02, 'b_gate': 6.435698e-03, 'w_dw': 2.263514e-02, 'b_dw': 4.371163e-02, 'g_conv_ln': 2.729394e-02, 'b_conv_ln': 2.374018e-02, 'w_conv_out': 2.194224e-02, 'b_conv_out': 3.884110e-02, 'g_q_lora': 1.183461e-02, 'w_uq': 4.641952e-03, 'g_kv_lora': 1.452242e-02, 'w_uk': 4.724387e-03, 'w_uv': 6.101195e-03, 'w_attn_out': 6.094241e-03, 'w_out': 2.248317e-02, 'g_ffn': 4.488443e-02, 'w_ffn_gate': 1.920014e-02, 'w_ffn_up': 1.864754e-02, 'w_ffn_down': 3.092376e-02, 'g_final': 7.990496e+00}


def _to_microbatches(a, axis):
    t = _jnp.moveaxis(a, axis, 0)
    t = t.reshape((N_MICROBATCH, t.shape[0] // N_MICROBATCH) + t.shape[1:])
    return _jnp.moveaxis(t, 1, axis + 1)


def setup_inputs(seed: int = 0) -> dict:
    inp = _fwd_setup_inputs(seed)
    key = _jax.random.fold_in(_jax.random.key(seed), 7919)
    shape, _ = _output_shape()
    out = dict(inp)
    out["loss_target"] = _jax.random.normal(_jax.random.fold_in(key, 0), shape, _jnp.float32)
    for i, name in enumerate(TWIN_WEIGHTS):
        w = inp[name].astype(_jnp.float32)
        if MOMENT_SCALE is None:
            s = _jnp.sqrt(_jnp.mean(_jnp.square(w)) + 1e-30)
        else:
            s = MOMENT_SCALE[name]
        km, kv = _jax.random.split(_jax.random.fold_in(key, i + 1))
        out[name] = w
        out["m_" + name] = s * _jax.random.normal(km, w.shape, _jnp.float32)
        out["v_" + name] = (s * s) * _jax.random.uniform(kv, w.shape, _jnp.float32, 0.5, 1.5)
    if N_MICROBATCH > 1:
        for name, axis in PER_EXAMPLE_BATCH_AXIS.items():
            out[name] = _to_microbatches(out[name], axis)
    return {'x': out['x'], 'meta_tokens': out['meta_tokens'], 'g_mix': out['g_mix'], 'w_in': out['w_in'], 'b_glu': out['b_glu'], 'b_gate': out['b_gate'], 'w_dw': out['w_dw'], 'b_dw': out['b_dw'], 'g_conv_ln': out['g_conv_ln'], 'b_conv_ln': out['b_conv_ln'], 'w_conv_out': out['w_conv_out'], 'b_conv_out': out['b_conv_out'], 'g_q_lora': out['g_q_lora'], 'w_uq': out['w_uq'], 'g_kv_lora': out['g_kv_lora'], 'w_uk': out['w_uk'], 'w_uv': out['w_uv'], 'w_attn_out': out['w_attn_out'], 'w_out': out['w_out'], 'g_ffn': out['g_ffn'], 'w_ffn_gate': out['w_ffn_gate'], 'w_ffn_up': out['w_ffn_up'], 'w_ffn_down': out['w_ffn_down'], 'g_final': out['g_final'], 'loss_target': out['loss_target'], 'm_meta_tokens': out['m_meta_tokens'], 'm_g_mix': out['m_g_mix'], 'm_w_in': out['m_w_in'], 'm_b_glu': out['m_b_glu'], 'm_b_gate': out['m_b_gate'], 'm_w_dw': out['m_w_dw'], 'm_b_dw': out['m_b_dw'], 'm_g_conv_ln': out['m_g_conv_ln'], 'm_b_conv_ln': out['m_b_conv_ln'], 'm_w_conv_out': out['m_w_conv_out'], 'm_b_conv_out': out['m_b_conv_out'], 'm_g_q_lora': out['m_g_q_lora'], 'm_w_uq': out['m_w_uq'], 'm_g_kv_lora': out['m_g_kv_lora'], 'm_w_uk': out['m_w_uk'], 'm_w_uv': out['m_w_uv'], 'm_w_attn_out': out['m_w_attn_out'], 'm_w_out': out['m_w_out'], 'm_g_ffn': out['m_g_ffn'], 'm_w_ffn_gate': out['m_w_ffn_gate'], 'm_w_ffn_up': out['m_w_ffn_up'], 'm_w_ffn_down': out['m_w_ffn_down'], 'm_g_final': out['m_g_final'], 'v_meta_tokens': out['v_meta_tokens'], 'v_g_mix': out['v_g_mix'], 'v_w_in': out['v_w_in'], 'v_b_glu': out['v_b_glu'], 'v_b_gate': out['v_b_gate'], 'v_w_dw': out['v_w_dw'], 'v_b_dw': out['v_b_dw'], 'v_g_conv_ln': out['v_g_conv_ln'], 'v_b_conv_ln': out['v_b_conv_ln'], 'v_w_conv_out': out['v_w_conv_out'], 'v_b_conv_out': out['v_b_conv_out'], 'v_g_q_lora': out['v_g_q_lora'], 'v_w_uq': out['v_w_uq'], 'v_g_kv_lora': out['v_g_kv_lora'], 'v_w_uk': out['v_w_uk'], 'v_w_uv': out['v_w_uv'], 'v_w_attn_out': out['v_w_attn_out'], 'v_w_out': out['v_w_out'], 'v_g_ffn': out['v_g_ffn'], 'v_w_ffn_gate': out['v_w_ffn_gate'], 'v_w_ffn_up': out['v_w_ffn_up'], 'v_w_ffn_down': out['v_w_ffn_down'], 'v_g_final': out['v_g_final']}


def _loss(weights, diff, rest, loss_target):
    with _jax.named_scope("forward"):
        args = {**rest, TWIN_DIFF_INPUT: diff, **{k: w.astype(_WEIGHT_DTYPES[k]) for k, w in weights.items()}}
        y = _forward(args)
    with _jax.named_scope("loss_head"):
        err = _jnp.square(y.astype(_jnp.float32) - loss_target)
        return 0.5 * _jnp.sum(_jnp.mean(err, axis=-1)) if err.ndim else 0.5 * err


def _adamw(w, g, m, v):
    m = ADAM_B1 * m + (1.0 - ADAM_B1) * g
    v = ADAM_B2 * v + (1.0 - ADAM_B2) * _jnp.square(g)
    m_hat = m / (1.0 - ADAM_B1 ** ADAM_STEP)
    v_hat = v / (1.0 - ADAM_B2 ** ADAM_STEP)
    delta = -ADAM_LR * (m_hat / (_jnp.sqrt(v_hat) + ADAM_EPS) + ADAM_WD * w)
    return delta, m, v


def reference(x, meta_tokens, g_mix, w_in, b_glu, b_gate, w_dw, b_dw, g_conv_ln, b_conv_ln, w_conv_out, b_conv_out, g_q_lora, w_uq, g_kv_lora, w_uk, w_uv, w_attn_out, w_out, g_ffn, w_ffn_gate, w_ffn_up, w_ffn_down, g_final, loss_target, m_meta_tokens, m_g_mix, m_w_in, m_b_glu, m_b_gate, m_w_dw, m_b_dw, m_g_conv_ln, m_b_conv_ln, m_w_conv_out, m_b_conv_out, m_g_q_lora, m_w_uq, m_g_kv_lora, m_w_uk, m_w_uv, m_w_attn_out, m_w_out, m_g_ffn, m_w_ffn_gate, m_w_ffn_up, m_w_ffn_down, m_g_final, v_meta_tokens, v_g_mix, v_w_in, v_b_glu, v_b_gate, v_w_dw, v_b_dw, v_g_conv_ln, v_b_conv_ln, v_w_conv_out, v_b_conv_out, v_g_q_lora, v_w_uq, v_g_kv_lora, v_w_uk, v_w_uv, v_w_attn_out, v_w_out, v_g_ffn, v_w_ffn_gate, v_w_ffn_up, v_w_ffn_down, v_g_final):
    given = dict(x=x, meta_tokens=meta_tokens, g_mix=g_mix, w_in=w_in, b_glu=b_glu, b_gate=b_gate, w_dw=w_dw, b_dw=b_dw, g_conv_ln=g_conv_ln, b_conv_ln=b_conv_ln, w_conv_out=w_conv_out, b_conv_out=b_conv_out, g_q_lora=g_q_lora, w_uq=w_uq, g_kv_lora=g_kv_lora, w_uk=w_uk, w_uv=w_uv, w_attn_out=w_attn_out, w_out=w_out, g_ffn=g_ffn, w_ffn_gate=w_ffn_gate, w_ffn_up=w_ffn_up, w_ffn_down=w_ffn_down, g_final=g_final, loss_target=loss_target, m_meta_tokens=m_meta_tokens, m_g_mix=m_g_mix, m_w_in=m_w_in, m_b_glu=m_b_glu, m_b_gate=m_b_gate, m_w_dw=m_w_dw, m_b_dw=m_b_dw, m_g_conv_ln=m_g_conv_ln, m_b_conv_ln=m_b_conv_ln, m_w_conv_out=m_w_conv_out, m_b_conv_out=m_b_conv_out, m_g_q_lora=m_g_q_lora, m_w_uq=m_w_uq, m_g_kv_lora=m_g_kv_lora, m_w_uk=m_w_uk, m_w_uv=m_w_uv, m_w_attn_out=m_w_attn_out, m_w_out=m_w_out, m_g_ffn=m_g_ffn, m_w_ffn_gate=m_w_ffn_gate, m_w_ffn_up=m_w_ffn_up, m_w_ffn_down=m_w_ffn_down, m_g_final=m_g_final, v_meta_tokens=v_meta_tokens, v_g_mix=v_g_mix, v_w_in=v_w_in, v_b_glu=v_b_glu, v_b_gate=v_b_gate, v_w_dw=v_w_dw, v_b_dw=v_b_dw, v_g_conv_ln=v_g_conv_ln, v_b_conv_ln=v_b_conv_ln, v_w_conv_out=v_w_conv_out, v_b_conv_out=v_b_conv_out, v_g_q_lora=v_g_q_lora, v_w_uq=v_w_uq, v_g_kv_lora=v_g_kv_lora, v_w_uk=v_w_uk, v_w_uv=v_w_uv, v_w_attn_out=v_w_attn_out, v_w_out=v_w_out, v_g_ffn=v_g_ffn, v_w_ffn_gate=v_w_ffn_gate, v_w_ffn_up=v_w_ffn_up, v_w_ffn_down=v_w_ffn_down, v_g_final=v_g_final)
    weights = {n: given[n] for n in TWIN_WEIGHTS}
    shared = {n: given[n] for n in SHARED_INPUTS}
    per_example = {n: given[n] for n in ['x']}
    grad_fn = _jax.value_and_grad(_loss, argnums=(0, 1))

    def one_microbatch(ex, loss_target):
        ex = dict(ex)
        diff = ex.pop(TWIN_DIFF_INPUT)
        return grad_fn(weights, diff, {**shared, **ex}, loss_target)

    if N_MICROBATCH == 1:
        loss, (grad_w, grad_x) = one_microbatch(per_example, given["loss_target"])
    else:
        def body(carry, xs):
            loss_sum, grad_sum = carry
            l_k, (gw_k, gx_k) = one_microbatch(xs[0], xs[1])
            with _jax.named_scope("update"):
                return (loss_sum + l_k, _jax.tree.map(_jnp.add, grad_sum, gw_k)), gx_k

        init = (_jnp.zeros((), _jnp.float32), _jax.tree.map(_jnp.zeros_like, weights))
        (loss, grad_w), grad_x = _jax.lax.scan(body, init, (per_example, given["loss_target"]))
    with _jax.named_scope("update"):
        delta_w, new_m, new_v = {}, {}, {}
        for n in TWIN_WEIGHTS:
            delta_w[n], new_m[n], new_v[n] = _adamw(weights[n], grad_w[n], given["m_" + n], given["v_" + n])
    return (loss, grad_x, *[grad_w[n] for n in TWIN_WEIGHTS], *[delta_w[n] for n in TWIN_WEIGHTS],
            *[new_m[n] for n in TWIN_WEIGHTS], *[new_v[n] for n in TWIN_WEIGHTS])
```

```python
import functools

import numpy as np
import jax
import jax.numpy as jnp
from jax import lax
from jax.experimental import pallas as pl
from jax.experimental.pallas import tpu as pltpu

F32 = jnp.float32
BF = jnp.bfloat16
MESH = pl.DeviceIdType.MESH

N_META = 16
BLOCK_Q = 128
CONV_WIDTH = 31
CONV_PAD = 32
QK_NOPE = 128
QK_ROPE = 64
V_HEAD = 128
HEAD_PAD = 256
ROPE_THETA = 10000.0
EPS = 1e-6
ADAM_LR = 0.001
ADAM_B1 = 0.9
ADAM_B2 = 0.999
ADAM_EPS = 1e-08
ADAM_WD = 0.01
ADAM_STEP = 10
VMEM_LIMIT = 56 * 1024 * 1024
N_CHIPS = 4
N_DEV = 8


def _cparams(sem):
    return pltpu.CompilerParams(dimension_semantics=sem, vmem_limit_bytes=VMEM_LIMIT)


def mm(a, b, *, mode, tm, tn, tk, out_dtype, name, res=None, out_blocks=None):
    b3 = b.ndim == 3
    if mode == "nn":
        M, K = a.shape
        N = b.shape[0] * b.shape[2] if b3 else b.shape[1]
        a_spec = pl.BlockSpec((tm, tk), lambda i, j, k: (i, k))
        if b3:
            per = b.shape[2] // tn
            b_spec = pl.BlockSpec((None, tk, tn), lambda i, j, k: (j // per, k, j % per))
        else:
            b_spec = pl.BlockSpec((tk, tn), lambda i, j, k: (k, j))
        dims = (((1,), (0,)), ((), ()))
    elif mode == "nt":
        M, K = a.shape
        N = b.shape[1] if b3 else b.shape[0]
        a_spec = pl.BlockSpec((tm, tk), lambda i, j, k: (i, k))
        if b3:
            per = b.shape[2] // tk
            b_spec = pl.BlockSpec((None, tn, tk), lambda i, j, k: (k // per, j, k % per))
        else:
            b_spec = pl.BlockSpec((tn, tk), lambda i, j, k: (j, k))
        dims = (((1,), (1,)), ((), ()))
    else:
        K, M = a.shape
        N = b.shape[1]
        a_spec = pl.BlockSpec((tk, tm), lambda i, j, k: (k, i))
        b_spec = pl.BlockSpec((tk, tn), lambda i, j, k: (k, j))
        dims = (((0,), (0,)), ((), ()))
    gm, gn, gk = M // tm, N // tn, K // tk
    assert gm * tm == M and gn * tn == N and gk * tk == K, (name, a.shape, b.shape, tm, tn, tk)
    if out_blocks is None:
        o_spec = pl.BlockSpec((tm, tn), lambda i, j, k: (i, j))
        o_shape = (M, N)
    else:
        nbw = N // out_blocks
        per_o = nbw // tn
        assert per_o * tn == nbw
        o_spec = pl.BlockSpec((None, tm, tn), lambda i, j, k: (j // per_o, i, j % per_o))
        o_shape = (out_blocks, M, nbw)
    has_res = res is not None

    def body(*refs):
        a_ref, b_ref = refs[0], refs[1]
        r_ref = refs[2] if has_res else None
        o_ref = refs[3] if has_res else refs[2]
        acc = refs[-1]
        p = lax.dot_general(a_ref[...], b_ref[...], dims, preferred_element_type=F32)

        def finish(v):
            if has_res:
                v = v + r_ref[...]
            o_ref[...] = v.astype(o_ref.dtype)

        if gk == 1:
            finish(p)
        else:
            k = pl.program_id(2)

            @pl.when(k == 0)
            def _():
                acc[...] = p

            @pl.when(k > 0)
            def _():
                acc[...] += p

            @pl.when(k == gk - 1)
            def _():
                finish(acc[...])

    in_specs = [a_spec, b_spec]
    args = [a, b]
    if has_res:
        in_specs.append(pl.BlockSpec((tm, tn), lambda i, j, k: (i, j)))
        args.append(res)
    return pl.pallas_call(
        body, name=name, grid=(gm, gn, gk), in_specs=in_specs, out_specs=o_spec,
        out_shape=jax.ShapeDtypeStruct(o_shape, out_dtype),
        scratch_shapes=[pltpu.VMEM((tm, tn), F32)] if gk > 1 else [],
        compiler_params=_cparams(("parallel", "parallel", "arbitrary")),
    )(*args)


def rowwise(body, row_ins, full_ins, row_outs, acc_outs, *, tr, name):
    T = row_ins[0].shape[0]
    assert T % tr == 0, (name, T, tr)
    n_ri, n_fi, n_ro = len(row_ins), len(full_ins), len(row_outs)

    def kern(*refs):
        body(pl.program_id(0), refs[:n_ri], refs[n_ri:n_ri + n_fi],
             refs[n_ri + n_fi:n_ri + n_fi + n_ro], refs[n_ri + n_fi + n_ro:])

    in_specs = [pl.BlockSpec((tr, a.shape[1]), lambda i: (i, 0)) for a in row_ins]
    in_specs += [pl.BlockSpec(a.shape, lambda i: (0, 0)) for a in full_ins]
    out_specs = [pl.BlockSpec((tr, c), lambda i: (i, 0)) for c, _ in row_outs]
    out_specs += [pl.BlockSpec(s, lambda i: (0, 0)) for s in acc_outs]
    out_shape = [jax.ShapeDtypeStruct((T, c), d) for c, d in row_outs]
    out_shape += [jax.ShapeDtypeStruct(s, F32) for s in acc_outs]
    return pl.pallas_call(
        kern, name=name, grid=(T // tr,), in_specs=in_specs, out_specs=out_specs, out_shape=out_shape,
        compiler_params=_cparams(("arbitrary",)),
    )(*row_ins, *full_ins)


def _acc(step, ref, val):
    @pl.when(step == 0)
    def _():
        ref[...] = val

    @pl.when(step > 0)
    def _():
        ref[...] += val


def _colsum(x):
    return jnp.sum(x, axis=0, keepdims=True)


def _sigmoid(x):
    return 1.0 / (1.0 + jnp.exp(-x))


def _rot_half(v):
    lane = lax.broadcasted_iota(jnp.int32, v.shape, 1)
    lo = -pltpu.roll(v, 96, 1)
    hi = pltpu.roll(v, 32, 1)
    return jnp.where(lane < 32, lo, jnp.where(lane < 64, hi, 0.0))


def rms_fwd(h, g, *, tr, name):
    def body(step, ri, fi, ro, ao):
        x = ri[0][...]
        r = lax.rsqrt(jnp.mean(x * x, axis=-1, keepdims=True) + EPS)
        ro[0][...] = ((x * r) * fi[0][...]).astype(BF)

    return rowwise(body, [h], [g], [(h.shape[1], BF)], [], tr=tr, name=name)[0]


def rms_bwd(dy, h, g, dres, *, tr, name):
    D = h.shape[1]

    def body(step, ri, fi, ro, ao):
        d, x, dr = ri[0][...], ri[1][...], ri[2][...]
        r = lax.rsqrt(jnp.mean(x * x, axis=-1, keepdims=True) + EPS)
        n = x * r
        _acc(step, ao[0], _colsum(d * n))
        dn = d * fi[0][...]
        dh = r * (dn - n * jnp.mean(dn * n, axis=-1, keepdims=True)) + dr
        ro[0][...] = dh
        ro[1][...] = dh.astype(BF)

    return rowwise(body, [dy, h, dres], [g], [(D, F32), (D, BF)], [(1, D)], tr=tr, name=name)


def glu_fwd(z, b, *, tr, name):
    C = z.shape[1] // 2

    def body(step, ri, fi, ro, ao):
        zz = ri[0][...] + fi[0][...]
        ro[0][...] = zz[:, :C] * _sigmoid(zz[:, C:])

    return rowwise(body, [z], [b], [(C, F32)], [], tr=tr, name=name)[0]


def glu_bwd(z, b, dc, *, tr, name):
    C = z.shape[1] // 2

    def body(step, ri, fi, ro, ao):
        zz = ri[0][...] + fi[0][...]
        d = ri[1][...]
        za, sg = zz[:, :C], _sigmoid(zz[:, C:])
        dza = d * sg
        dzb = d * za * sg * (1.0 - sg)
        ro[0][:, :C] = dza.astype(BF)
        ro[0][:, C:] = dzb.astype(BF)
        _acc(step, ao[0], _colsum(dza))
        _acc(step, ao[1], _colsum(dzb))

    dz, da, db = rowwise(body, [z, dc], [b], [(2 * C, BF)], [(1, C), (1, C)], tr=tr, name=name)
    return dz, jnp.concatenate([da, db], axis=1)


def ln_silu_fwd(c1, g, b, *, tr, name):
    def body(step, ri, fi, ro, ao):
        x = ri[0][...]
        mu = jnp.mean(x, axis=-1, keepdims=True)
        xc = x - mu
        rstd = lax.rsqrt(jnp.mean(xc * xc, axis=-1, keepdims=True) + EPS)
        y = (xc * rstd) * fi[0][...] + fi[1][...]
        ro[0][...] = (y * _sigmoid(y)).astype(BF)

    return rowwise(body, [c1], [g, b], [(c1.shape[1], BF)], [], tr=tr, name=name)[0]


def ln_silu_bwd(c1, dc3, g, b, *, tr, name):
    C = c1.shape[1]

    def body(step, ri, fi, ro, ao):
        x, d3 = ri[0][...], ri[1][...]
        mu = jnp.mean(x, axis=-1, keepdims=True)
        xc = x - mu
        rstd = lax.rsqrt(jnp.mean(xc * xc, axis=-1, keepdims=True) + EPS)
        n = xc * rstd
        y = n * fi[0][...] + fi[1][...]
        sg = _sigmoid(y)
        dy = d3 * (sg * (1.0 + y * (1.0 - sg)))
        _acc(step, ao[0], _colsum(dy * n))
        _acc(step, ao[1], _colsum(dy))
        dn = dy * fi[0][...]
        dx = rstd * (dn - jnp.mean(dn, axis=-1, keepdims=True) - n * jnp.mean(dn * n, axis=-1, keepdims=True))
        ro[0][...] = dx
        _acc(step, ao[2], _colsum(dx))

    return rowwise(body, [c1, dc3], [g, b], [(C, F32)], [(1, C), (1, C), (1, C)], tr=tr, name=name)


def lora_norm_fwd(zs, gq, gkv, cos, sin, *, tr, name):
    R = gq.shape[1]

    def body(step, ri, fi, ro, ao):
        z = ri[0][...]
        for o, gi in ((0, 0), (1, 1)):
            x = z[:, o * R:(o + 1) * R]
            r = lax.rsqrt(jnp.mean(x * x, axis=-1, keepdims=True) + EPS)
            ro[o][...] = ((x * r) * fi[gi][...]).astype(BF)
        kr = z[:, 2 * R:2 * R + 128]
        ro[2][...] = kr * ri[1][...] + _rot_half(kr) * ri[2][...]

    return rowwise(body, [zs, cos, sin], [gq, gkv], [(R, BF), (R, BF), (128, F32)], [], tr=tr, name=name)


def lora_norm_bwd(zs, dcq, dckv, dkr, cos, sin, gq, gkv, *, tr, name):
    R = gq.shape[1]

    def body(step, ri, fi, ro, ao):
        z = ri[0][...]
        for o in (0, 1):
            x = z[:, o * R:(o + 1) * R]
            d = ri[1 + o][...]
            r = lax.rsqrt(jnp.mean(x * x, axis=-1, keepdims=True) + EPS)
            n = x * r
            _acc(step, ao[o], _colsum(d * n))
            dn = d * fi[o][...]
            ro[0][:, o * R:(o + 1) * R] = (r * (dn - n * jnp.mean(dn * n, axis=-1, keepdims=True))).astype(BF)
        dk = ri[3][...]
        ro[0][:, 2 * R:2 * R + 128] = (dk * ri[4][...] - _rot_half(dk * ri[5][...])).astype(BF)

    return rowwise(body, [zs, dcq, dckv, dkr, cos, sin], [gq, gkv], [(2 * R + 128, BF)], [(1, R), (1, R)],
                   tr=tr, name=name)


def mix_fwd(zg, yc, ya, bg, bco, *, tr, name):
    D = yc.shape[1]

    def body(step, ri, fi, ro, ao):
        g = _sigmoid(ri[0][...] + fi[0][...])
        ro[0][...] = (g[:, :D] * (ri[1][...] + fi[1][...]) + g[:, D:] * ri[2][...]).astype(BF)

    return rowwise(body, [zg, yc, ya], [bg, bco], [(D, BF)], [], tr=tr, name=name)[0]


def mix_bwd(zg, yc, ya, dmix, bg, bco, *, tr, name):
    D = yc.shape[1]

    def body(step, ri, fi, ro, ao):
        g = _sigmoid(ri[0][...] + fi[0][...])
        gc, ga = g[:, :D], g[:, D:]
        ycv = ri[1][...] + fi[1][...]
        yav = ri[2][...]
        dm = ri[3][...]
        dyc = dm * gc
        ro[0][...] = dyc.astype(BF)
        ro[1][...] = (dm * ga).astype(BF)
        dzc = dm * ycv * gc * (1.0 - gc)
        dza = dm * yav * ga * (1.0 - ga)
        ro[2][:, :D] = dzc.astype(BF)
        ro[2][:, D:] = dza.astype(BF)
        _acc(step, ao[0], _colsum(dzc))
        _acc(step, ao[1], _colsum(dza))
        _acc(step, ao[2], _colsum(dyc))

    dyc, dya, dzg, dbc, dba, dbo = rowwise(body, [zg, yc, ya, dmix], [bg, bco], [(D, BF), (D, BF), (2 * D, BF)],
                                           [(1, D), (1, D), (1, D)], tr=tr, name=name)
    return dyc, dya, dzg, jnp.concatenate([dbc, dba], axis=1), dbo


def swiglu_fwd(a, b, *, tr, name):
    def body(step, ri, fi, ro, ao):
        x = ri[0][...]
        ro[0][...] = (x * _sigmoid(x) * ri[1][...]).astype(BF)

    return rowwise(body, [a, b], [], [(a.shape[1], BF)], [], tr=tr, name=name)[0]


def swiglu_bwd(a, b, df, *, tr, name):
    F = a.shape[1]

    def body(step, ri, fi, ro, ao):
        x, u, d = ri[0][...], ri[1][...], ri[2][...]
        sg = _sigmoid(x)
        ro[0][...] = (d * u * (sg * (1.0 + x * (1.0 - sg)))).astype(BF)
        ro[1][...] = (d * (x * sg)).astype(BF)

    return rowwise(body, [a, b, df], [], [(F, BF), (F, BF)], [], tr=tr, name=name)


def final_loss(h2, tgt, g, *, seq, tr, name):
    D = h2.shape[1]

    def body(step, ri, fi, ro, ao):
        x = ri[0][...]
        r = lax.rsqrt(jnp.mean(x * x, axis=-1, keepdims=True) + EPS)
        n = x * r
        gg = fi[0][...]
        row = lax.broadcasted_iota(jnp.int32, (tr, 1), 0) + step * tr
        live = jnp.logical_and(row >= N_META, row < N_META + seq)
        e = jnp.where(live, n * gg - ri[1][...], 0.0)
        _acc(step, ao[0], jnp.broadcast_to(0.5 * jnp.sum(jnp.mean(e * e, axis=-1, keepdims=True)), (1, 128)))
        dy = e * (1.0 / D)
        _acc(step, ao[1], _colsum(dy * n))
        dn = dy * gg
        dh = r * (dn - n * jnp.mean(dn * n, axis=-1, keepdims=True))
        ro[0][...] = dh
        ro[1][...] = dh.astype(BF)

    return rowwise(body, [h2, tgt], [g], [(D, F32), (D, BF)], [(1, 128), (1, D)], tr=tr, name=name)


def dwconv_fwd(x, w, b, *, tc, rc, name):
    T, C = x.shape
    nchunk = T // rc
    assert nchunk * rc == T and C % tc == 0

    def body(x_ref, w_ref, b_ref, y_ref, xp_ref):
        xp_ref[0:CONV_PAD, :] = jnp.zeros((CONV_PAD, tc), F32)
        xp_ref[CONV_PAD:CONV_PAD + T, :] = x_ref[...]
        off = CONV_PAD - (CONV_WIDTH - 1)
        for ci in range(nchunk):
            t0 = ci * rc
            acc = jnp.broadcast_to(b_ref[...], (rc, tc))
            for j in range(CONV_WIDTH):
                acc = acc + w_ref[j:j + 1, :] * xp_ref[t0 + off + j:t0 + off + j + rc, :]
            y_ref[t0:t0 + rc, :] = acc

    return pl.pallas_call(
        body, name=name, grid=(C // tc,),
        in_specs=[pl.BlockSpec((T, tc), lambda j: (0, j)), pl.BlockSpec((32, tc), lambda j: (0, j)),
                  pl.BlockSpec((1, tc), lambda j: (0, j))],
        out_specs=pl.BlockSpec((T, tc), lambda j: (0, j)),
        out_shape=jax.ShapeDtypeStruct((T, C), F32),
        scratch_shapes=[pltpu.VMEM((T + CONV_PAD, tc), F32)],
        compiler_params=_cparams(("parallel",)),
    )(x, w, b)


def dwconv_bwd(x, dy, w, *, tc, rc, name):
    T, C = x.shape
    nchunk = T // rc
    assert nchunk * rc == T and C % tc == 0 and rc % 8 == 0

    def body(x_ref, dy_ref, w_ref, dx_ref, dw_ref, xp_ref, dp_ref):
        off = CONV_PAD - (CONV_WIDTH - 1)
        xp_ref[0:CONV_PAD, :] = jnp.zeros((CONV_PAD, tc), F32)
        xp_ref[CONV_PAD:CONV_PAD + T, :] = x_ref[...]
        dp_ref[0:T, :] = dy_ref[...]
        dp_ref[T:T + CONV_PAD, :] = jnp.zeros((CONV_PAD, tc), F32)
        for ci in range(nchunk):
            t0 = ci * rc
            acc = jnp.zeros((rc, tc), F32)
            for j in range(CONV_WIDTH):
                s = t0 + (CONV_WIDTH - 1) - j
                acc = acc + w_ref[j:j + 1, :] * dp_ref[s:s + rc, :]
            dx_ref[t0:t0 + rc, :] = acc
        dw_ref[...] = jnp.zeros((32, tc), F32)
        for j in range(CONV_WIDTH):
            acc = jnp.zeros((8, tc), F32)
            for ci in range(nchunk):
                t0 = ci * rc
                pr = dp_ref[t0:t0 + rc, :] * xp_ref[t0 + off + j:t0 + off + j + rc, :]
                acc = acc + jnp.sum(pr.reshape(rc // 8, 8, tc), axis=0)
            dw_ref[j:j + 1, :] = jnp.sum(acc, axis=0, keepdims=True)

    return pl.pallas_call(
        body, name=name, grid=(C // tc,),
        in_specs=[pl.BlockSpec((T, tc), lambda j: (0, j)), pl.BlockSpec((T, tc), lambda j: (0, j)),
                  pl.BlockSpec((32, tc), lambda j: (0, j))],
        out_specs=[pl.BlockSpec((T, tc), lambda j: (0, j)), pl.BlockSpec((32, tc), lambda j: (0, j))],
        out_shape=[jax.ShapeDtypeStruct((T, C), F32), jax.ShapeDtypeStruct((32, C), F32)],
        scratch_shapes=[pltpu.VMEM((T + CONV_PAD, tc), F32), pltpu.VMEM((T + CONV_PAD, tc), F32)],
        compiler_params=_cparams(("parallel",)),
    )(x, dy, w)


def q_proj(cq, wq, cos, sin, *, scale, name):
    T, R = cq.shape
    H = wq.shape[1] // HEAD_PAD

    def body(a_ref, b_ref, c_ref, s_ref, o_ref):
        p = jnp.dot(a_ref[...], b_ref[...], preferred_element_type=F32)
        o_ref[:, :QK_NOPE] = (p[:, :QK_NOPE] * scale).astype(BF)
        x = p[:, QK_NOPE:]
        o_ref[:, QK_NOPE:] = ((x * c_ref[...] + _rot_half(x) * s_ref[...]) * scale).astype(BF)

    return pl.pallas_call(
        body, name=name, grid=(H,),
        in_specs=[pl.BlockSpec((T, R), lambda h: (0, 0)), pl.BlockSpec((R, HEAD_PAD), lambda h: (0, h)),
                  pl.BlockSpec((T, 128), lambda h: (0, 0)), pl.BlockSpec((T, 128), lambda h: (0, 0))],
        out_specs=pl.BlockSpec((T, HEAD_PAD), lambda h: (0, h)),
        out_shape=jax.ShapeDtypeStruct((T, H * HEAD_PAD), BF),
        compiler_params=_cparams(("parallel",)),
    )(cq, wq, cos, sin)


def k_proj(ckv, wk, kr, *, name):
    T, R = ckv.shape
    H = wk.shape[1] // QK_NOPE

    def body(a_ref, b_ref, kr_ref, o_ref):
        p = jnp.dot(a_ref[...], b_ref[...], preferred_element_type=F32)
        o_ref[:, :QK_NOPE] = p.astype(BF)
        o_ref[:, QK_NOPE:] = kr_ref[...].astype(BF)

    return pl.pallas_call(
        body, name=name, grid=(H,),
        in_specs=[pl.BlockSpec((T, R), lambda h: (0, 0)), pl.BlockSpec((R, QK_NOPE), lambda h: (0, h)),
                  pl.BlockSpec((T, 128), lambda h: (0, 0))],
        out_specs=pl.BlockSpec((T, HEAD_PAD), lambda h: (0, h)),
        out_shape=jax.ShapeDtypeStruct((T, H * HEAD_PAD), BF),
        compiler_params=_cparams(("parallel",)),
    )(ckv, wk, kr)


def _causal_scores(q, k, q0):
    s = lax.dot_general(q, k, (((1,), (1,)), ((), ())), preferred_element_type=F32)
    row = lax.broadcasted_iota(jnp.int32, s.shape, 0) + q0
    col = lax.broadcasted_iota(jnp.int32, s.shape, 1)
    return jnp.where(col <= row, s, -1e30)


def attn_fwd(q, k, v, *, nch, name):
    T = q.shape[0]
    H = q.shape[1] // HEAD_PAD
    CH = T // nch
    assert CH * nch == T and CH % 16 == 0

    def body(q_ref, k_ref, v_ref, o_ref, lse_ref):
        for qi in range(nch):
            L = (qi + 1) * CH
            s = _causal_scores(q_ref[qi * CH:L, :], k_ref[0:L, :], qi * CH)
            m = jnp.max(s, axis=1, keepdims=True)
            p = jnp.exp(s - m)
            l = jnp.sum(p, axis=1, keepdims=True)
            o = jnp.dot(p.astype(BF), v_ref[0:L, :], preferred_element_type=F32)
            o_ref[qi * CH:L, :] = (o / l).astype(BF)
            lse_ref[qi * CH:L, :] = jnp.broadcast_to(m + jnp.log(l), (CH, V_HEAD))

    return pl.pallas_call(
        body, name=name, grid=(H,),
        in_specs=[pl.BlockSpec((T, HEAD_PAD), lambda h: (0, h)), pl.BlockSpec((T, HEAD_PAD), lambda h: (0, h)),
                  pl.BlockSpec((T, V_HEAD), lambda h: (0, h))],
        out_specs=[pl.BlockSpec((T, V_HEAD), lambda h: (0, h)), pl.BlockSpec((T, V_HEAD), lambda h: (0, h))],
        out_shape=[jax.ShapeDtypeStruct((T, H * V_HEAD), BF), jax.ShapeDtypeStruct((T, H * V_HEAD), F32)],
        compiler_params=_cparams(("parallel",)),
    )(q, k, v)


def attn_bwd(q, k, v, o, do, lse, cos, sin, *, scale, nch, name):
    T = q.shape[0]
    H = q.shape[1] // HEAD_PAD
    CH = T // nch
    tn_dims = (((0,), (0,)), ((), ()))
    nt_dims = (((1,), (1,)), ((), ()))

    def body(q_ref, k_ref, v_ref, o_ref, do_ref, lse_ref, c_ref, s_ref,
             dq_ref, dk_ref, dv_ref, dkr_ref, dk_acc, dv_acc):
        h = pl.program_id(0)
        dk_acc[...] = jnp.zeros_like(dk_acc)
        dv_acc[...] = jnp.zeros_like(dv_acc)
        for qi in range(nch):
            L = (qi + 1) * CH
            rows = slice(qi * CH, L)
            qc, doc = q_ref[rows, :], do_ref[rows, :]
            kc, vc = k_ref[0:L, :], v_ref[0:L, :]
            s = _causal_scores(qc, kc, qi * CH)
            p = jnp.exp(s - lse_ref[rows, 0:1])
            dp = lax.dot_general(doc, vc, nt_dims, preferred_element_type=F32)
            dsum = jnp.sum(doc.astype(F32) * o_ref[rows, :].astype(F32), axis=1, keepdims=True)
            ds = (p * (dp - dsum)).astype(BF)
            dq = jnp.dot(ds, kc, preferred_element_type=F32)
            dq_ref[rows, :QK_NOPE] = (dq[:, :QK_NOPE] * scale).astype(BF)
            dyr = dq[:, QK_NOPE:]
            dq_ref[rows, QK_NOPE:] = ((dyr * c_ref[rows, :] - _rot_half(dyr * s_ref[rows, :])) * scale).astype(BF)
            dk_acc[0:L, :] += lax.dot_general(ds, qc, tn_dims, preferred_element_type=F32)
            dv_acc[0:L, :] += lax.dot_general(p.astype(BF), doc, tn_dims, preferred_element_type=F32)
        dk_ref[:, :QK_NOPE] = dk_acc[:, :QK_NOPE].astype(BF)
        dk_ref[:, QK_NOPE:] = jnp.zeros((T, HEAD_PAD - QK_NOPE), BF)
        dv_ref[...] = dv_acc[...].astype(BF)

        @pl.when(h == 0)
        def _():
            dkr_ref[...] = dk_acc[:, QK_NOPE:]

        @pl.when(h > 0)
        def _():
            dkr_ref[...] += dk_acc[:, QK_NOPE:]

    hp = pl.BlockSpec((T, HEAD_PAD), lambda h: (0, h))
    hv = pl.BlockSpec((T, V_HEAD), lambda h: (0, h))
    tab = pl.BlockSpec((T, 128), lambda h: (0, 0))
    return pl.pallas_call(
        body, name=name, grid=(H,),
        in_specs=[hp, hp, hv, hv, hv, hv, tab, tab],
        out_specs=[hp, hp, hv, tab],
        out_shape=[jax.ShapeDtypeStruct((T, H * HEAD_PAD), BF), jax.ShapeDtypeStruct((T, H * HEAD_PAD), BF),
                   jax.ShapeDtypeStruct((T, H * V_HEAD), BF), jax.ShapeDtypeStruct((T, 128), F32)],
        scratch_shapes=[pltpu.VMEM((T, HEAD_PAD), F32), pltpu.VMEM((T, V_HEAD), F32)],
        compiler_params=_cparams(("arbitrary",)),
    )(q, k, v, o, do, lse, cos, sin)


def _place():
    x, y, c = lax.axis_index("x"), lax.axis_index("y"), lax.axis_index("c")
    return x, y, c


def _other_chips(x, y):
    return [(1 - x, y), (x, 1 - y), (1 - x, 1 - y)]


def gather_devices(v, *, name):
    R, C = v.shape

    def body(v_ref, o_ref, send_sems, recv_sems):
        x, y, c = _place()
        me = 4 * x + 2 * y + c
        o_ref[me] = v_ref[...]
        copies = []
        for k in range(1, N_DEV):
            fx, fy, fc = (k >> 2) & 1, (k >> 1) & 1, k & 1
            to = (x ^ fx, y ^ fy, c ^ fc)
            cp = pltpu.make_async_remote_copy(src_ref=v_ref, dst_ref=o_ref.at[me], send_sem=send_sems.at[k - 1],
                                              recv_sem=recv_sems.at[k - 1], device_id=to, device_id_type=MESH)
            cp.start()
            copies.append(cp)
        for k in range(1, N_DEV):
            fx, fy, fc = (k >> 2) & 1, (k >> 1) & 1, k & 1
            frm = 4 * (x ^ fx) + 2 * (y ^ fy) + (c ^ fc)
            pltpu.make_async_remote_copy(src_ref=v_ref, dst_ref=o_ref.at[frm], send_sem=send_sems.at[k - 1],
                                         recv_sem=recv_sems.at[k - 1], device_id=(x, y, c),
                                         device_id_type=MESH).wait_recv()
        for cp in copies:
            cp.wait_send()

    return pl.pallas_call(
        body, name=name, out_shape=jax.ShapeDtypeStruct((N_DEV, R, C), F32),
        in_specs=[pl.BlockSpec(memory_space=pltpu.VMEM)], out_specs=pl.BlockSpec(memory_space=pltpu.VMEM),
        scratch_shapes=[pltpu.SemaphoreType.DMA((N_DEV - 1,)), pltpu.SemaphoreType.DMA((N_DEV - 1,))],
    )(v)


def gather_chips(shards, *, name):
    n = len(shards)

    def body(*refs):
        ins, outs = refs[:n], refs[n:2 * n]
        send_sems, recv_sems, local_sem = refs[2 * n:]
        x, y, c = _place()
        me = 2 * x + y
        sib = (x, y, 1 - c)
        chips = _other_chips(x, y)
        locals_, sends = [], []
        for i in range(n):
            hr = ins[i].shape[0] // 2
            mine = pl.ds(c * hr, hr)
            lc = pltpu.make_async_copy(ins[i], outs[i].at[me], local_sem.at[i])
            lc.start()
            locals_.append(lc)
            for j, (cx, cy) in enumerate(chips):
                cp = pltpu.make_async_remote_copy(
                    src_ref=ins[i].at[mine], dst_ref=outs[i].at[me, mine], send_sem=send_sems.at[i, j],
                    recv_sem=recv_sems.at[i, j], device_id=(cx, cy, c), device_id_type=MESH)
                cp.start()
                sends.append(cp)
        for i in range(n):
            hr = ins[i].shape[0] // 2
            mine = pl.ds(c * hr, hr)
            for j, (cx, cy) in enumerate(chips):
                blk = outs[i].at[2 * cx + cy, mine]
                pltpu.make_async_remote_copy(src_ref=blk, dst_ref=blk, send_sem=send_sems.at[i, j],
                                             recv_sem=recv_sems.at[i, j], device_id=sib,
                                             device_id_type=MESH).wait_recv()
                cp = pltpu.make_async_remote_copy(src_ref=blk, dst_ref=blk, send_sem=send_sems.at[i, 3 + j],
                                                  recv_sem=recv_sems.at[i, 3 + j], device_id=sib,
                                                  device_id_type=MESH)
                cp.start()
                sends.append(cp)
        for i in range(n):
            hr = ins[i].shape[0] // 2
            theirs = pl.ds((1 - c) * hr, hr)
            for j, (cx, cy) in enumerate(chips):
                blk = outs[i].at[2 * cx + cy, theirs]
                pltpu.make_async_remote_copy(src_ref=blk, dst_ref=blk, send_sem=send_sems.at[i, 3 + j],
                                             recv_sem=recv_sems.at[i, 3 + j], device_id=sib,
                                             device_id_type=MESH).wait_recv()
        for cp in sends:
            cp.wait_send()
        for lc in locals_:
            lc.wait()

    anyspec = pl.BlockSpec(memory_space=pl.ANY)
    return pl.pallas_call(
        body, name=name,
        out_shape=[jax.ShapeDtypeStruct((N_CHIPS,) + s.shape, s.dtype) for s in shards],
        in_specs=[anyspec] * n, out_specs=[anyspec] * n,
        scratch_shapes=[pltpu.SemaphoreType.DMA((n, 6)), pltpu.SemaphoreType.DMA((n, 6)),
                        pltpu.SemaphoreType.DMA((n,))],
    )(*shards)


def exchange_pair(gs, *, name):
    n = len(gs)

    def body(*refs):
        ins, outs = refs[:n], refs[n:2 * n]
        send_sems, recv_sems = refs[2 * n:]
        x, y, c = _place()
        sib = (x, y, 1 - c)
        cps = []
        for i in range(n):
            hr = ins[i].shape[1] // 2
            for k in range(N_CHIPS):
                cp = pltpu.make_async_remote_copy(
                    src_ref=ins[i].at[k, pl.ds((1 - c) * hr, hr)], dst_ref=outs[i].at[k],
                    send_sem=send_sems.at[i, k], recv_sem=recv_sems.at[i, k], device_id=sib, device_id_type=MESH)
                cp.start()
                cps.append(cp)
        for cp in cps:
            cp.wait()

    anyspec = pl.BlockSpec(memory_space=pl.ANY)
    return pl.pallas_call(
        body, name=name,
        out_shape=[jax.ShapeDtypeStruct((N_CHIPS, g.shape[1] // 2, g.shape[2]), g.dtype) for g in gs],
        in_specs=[anyspec] * n, out_specs=[anyspec] * n,
        scratch_shapes=[pltpu.SemaphoreType.DMA((n, N_CHIPS)), pltpu.SemaphoreType.DMA((n, N_CHIPS))],
    )(*gs)


def scatter_chips(ss, *, name):
    n = len(ss)

    def body(*refs):
        ins, outs = refs[:n], refs[n:2 * n]
        send_sems, recv_sems = refs[2 * n:]
        x, y, c = _place()
        chips = _other_chips(x, y)
        cps = []
        for i in range(n):
            for j, (cx, cy) in enumerate(chips):
                cp = pltpu.make_async_remote_copy(
                    src_ref=ins[i].at[2 * cx + cy], dst_ref=outs[i].at[j], send_sem=send_sems.at[i, j],
                    recv_sem=recv_sems.at[i, j], device_id=(cx, cy, c), device_id_type=MESH)
                cp.start()
                cps.append(cp)
        for cp in cps:
            cp.wait()

    anyspec = pl.BlockSpec(memory_space=pl.ANY)
    return pl.pallas_call(
        body, name=name,
        out_shape=[jax.ShapeDtypeStruct((3,) + s.shape[1:], s.dtype) for s in ss],
        in_specs=[anyspec] * n, out_specs=[anyspec] * n,
        scratch_shapes=[pltpu.SemaphoreType.DMA((n, 3)), pltpu.SemaphoreType.DMA((n, 3))],
    )(*ss)


def share_halves(hs, *, name):
    n = len(hs)

    def body(*refs):
        ins, outs = refs[:n], refs[n:2 * n]
        send_sems, recv_sems, local_sem = refs[2 * n:]
        x, y, c = _place()
        sib = (x, y, 1 - c)
        cps, lcs = [], []
        for i in range(n):
            hr = ins[i].shape[0]
            mine = pl.ds(c * hr, hr)
            lc = pltpu.make_async_copy(ins[i], outs[i].at[mine], local_sem.at[i])
            lc.start()
            lcs.append(lc)
            cp = pltpu.make_async_remote_copy(src_ref=ins[i], dst_ref=outs[i].at[mine], send_sem=send_sems.at[i],
                                              recv_sem=recv_sems.at[i], device_id=sib, device_id_type=MESH)
            cp.start()
            cps.append(cp)
        for i in range(n):
            hr = ins[i].shape[0]
            theirs = outs[i].at[pl.ds((1 - c) * hr, hr)]
            pltpu.make_async_remote_copy(src_ref=ins[i], dst_ref=theirs, send_sem=send_sems.at[i],
                                         recv_sem=recv_sems.at[i], device_id=sib, device_id_type=MESH).wait_recv()
        for cp in cps:
            cp.wait_send()
        for lc in lcs:
            lc.wait()

    anyspec = pl.BlockSpec(memory_space=pl.ANY)
    return pl.pallas_call(
        body, name=name,
        out_shape=[jax.ShapeDtypeStruct((2 * h.shape[0], h.shape[1]), h.dtype) for h in hs],
        in_specs=[anyspec] * n, out_specs=[anyspec] * n,
        scratch_shapes=[pltpu.SemaphoreType.DMA((n,)), pltpu.SemaphoreType.DMA((n,)), pltpu.SemaphoreType.DMA((n,))],
    )(*hs)


def _row_tile(r, cols, itemsize, target_bytes=2 * 1024 * 1024):
    best = None
    for t in range(16, r + 1, 16):
        if r % t == 0 and t * cols * itemsize <= target_bytes:
            best = t
    return best if best is not None else r


def pair_sum(g, recv, half, *, name):
    _, R, C = g.shape
    hr = R // 2
    tr = _row_tile(hr, C, 4)
    nb = hr // tr

    def body(half_ref, a_ref, b_ref, o_ref):
        o_ref[...] = (a_ref[...].astype(F32) + b_ref[...].astype(F32)).astype(BF)

    return pl.pallas_call(
        body, name=name,
        grid_spec=pltpu.PrefetchScalarGridSpec(
            num_scalar_prefetch=1, grid=(N_CHIPS, nb),
            in_specs=[pl.BlockSpec((None, tr, C), lambda k, r, hf: (k, hf[0] * nb + r, 0)),
                      pl.BlockSpec((None, tr, C), lambda k, r, hf: (k, r, 0))],
            out_specs=pl.BlockSpec((None, tr, C), lambda k, r, hf: (k, r, 0))),
        out_shape=jax.ShapeDtypeStruct((N_CHIPS, hr, C), BF),
        compiler_params=_cparams(("parallel", "parallel")),
    )(half, g, recv)


def chip_sum(s, recv, chip, *, name):
    _, hr, C = s.shape
    tr = _row_tile(hr, C, 4)
    nb = hr // tr

    def body(chip_ref, a_ref, b_ref, o_ref):
        acc = a_ref[...].astype(F32)
        for j in range(3):
            acc = acc + b_ref[j].astype(F32)
        o_ref[...] = acc

    return pl.pallas_call(
        body, name=name,
        grid_spec=pltpu.PrefetchScalarGridSpec(
            num_scalar_prefetch=1, grid=(nb,),
            in_specs=[pl.BlockSpec((None, tr, C), lambda r, ch: (ch[0], r, 0)),
                      pl.BlockSpec((3, tr, C), lambda r, ch: (0, r, 0))],
            out_specs=pl.BlockSpec((tr, C), lambda r, ch: (r, 0))),
        out_shape=jax.ShapeDtypeStruct((hr, C), F32),
        compiler_params=_cparams(("parallel",)),
    )(chip, s, recv)


def sum_devices(v, *, name):
    _, R, C = v.shape

    def body(v_ref, o_ref):
        acc = v_ref[0]
        for k in range(1, N_DEV):
            acc = acc + v_ref[k]
        o_ref[...] = acc

    return pl.pallas_call(body, name=name, out_shape=jax.ShapeDtypeStruct((R, C), F32))(v)


def adamw(w, g, m, v, *, name):
    R, C = w.shape
    tr = _row_tile(R, C, 4, target_bytes=1024 * 1024)
    c1 = 1.0 / (1.0 - ADAM_B1 ** ADAM_STEP)
    c2 = 1.0 / (1.0 - ADAM_B2 ** ADAM_STEP)

    def body(w_ref, g_ref, m_ref, v_ref, d_ref, nm_ref, nv_ref):
        gg = g_ref[...]
        nm = ADAM_B1 * m_ref[...] + (1.0 - ADAM_B1) * gg
        nv = ADAM_B2 * v_ref[...] + (1.0 - ADAM_B2) * (gg * gg)
        nm_ref[...] = nm
        nv_ref[...] = nv
        d_ref[...] = -ADAM_LR * ((nm * c1) / (jnp.sqrt(nv * c2) + ADAM_EPS) + ADAM_WD * w_ref[...])

    spec = pl.BlockSpec((tr, C), lambda i: (i, 0))
    return pl.pallas_call(
        body, name=name, grid=(R // tr,), in_specs=[spec] * 4, out_specs=[spec] * 3,
        out_shape=[jax.ShapeDtypeStruct((R, C), F32)] * 3, compiler_params=_cparams(("parallel",)),
    )(w, g, m, v)


def _rope_tables(T):
    pos = np.arange(T, dtype=np.float32)
    inv_freq = (ROPE_THETA ** (-np.arange(0, QK_ROPE, 2, dtype=np.float32) / QK_ROPE)).astype(np.float32)
    ang = pos[:, None] * inv_freq[None, :]
    z = np.zeros((T, 64), np.float32)
    cos = np.concatenate([np.cos(ang), np.cos(ang), z], axis=1).astype(np.float32)
    sin = np.concatenate([np.sin(ang), np.sin(ang), z], axis=1).astype(np.float32)
    return jnp.asarray(cos), jnp.asarray(sin)


def _pick(n, cands):
    for c in cands:
        if n % c == 0:
            return c
    return n


def kernel(x, meta_tokens, g_mix, w_in, b_glu, b_gate, w_dw, b_dw, g_conv_ln, b_conv_ln, w_conv_out, b_conv_out, g_q_lora, w_uq, g_kv_lora, w_uk, w_uv, w_attn_out, w_out, g_ffn, w_ffn_gate, w_ffn_up, w_ffn_down, g_final, loss_target, m_meta_tokens, m_g_mix, m_w_in, m_b_glu, m_b_gate, m_w_dw, m_b_dw, m_g_conv_ln, m_b_conv_ln, m_w_conv_out, m_b_conv_out, m_g_q_lora, m_w_uq, m_g_kv_lora, m_w_uk, m_w_uv, m_w_attn_out, m_w_out, m_g_ffn, m_w_ffn_gate, m_w_ffn_up, m_w_ffn_down, m_g_final, v_meta_tokens, v_g_mix, v_w_in, v_b_glu, v_b_gate, v_w_dw, v_b_dw, v_g_conv_ln, v_b_conv_ln, v_w_conv_out, v_b_conv_out, v_g_q_lora, v_w_uq, v_g_kv_lora, v_w_uk, v_w_uv, v_w_attn_out, v_w_out, v_g_ffn, v_w_ffn_gate, v_w_ffn_up, v_w_ffn_down, v_g_final):
    weights = dict(meta_tokens=meta_tokens, g_mix=g_mix, w_in=w_in, b_glu=b_glu, b_gate=b_gate, w_dw=w_dw, b_dw=b_dw,
                   g_conv_ln=g_conv_ln, b_conv_ln=b_conv_ln, w_conv_out=w_conv_out, b_conv_out=b_conv_out,
                   g_q_lora=g_q_lora, w_uq=w_uq, g_kv_lora=g_kv_lora, w_uk=w_uk, w_uv=w_uv, w_attn_out=w_attn_out,
                   w_out=w_out, g_ffn=g_ffn, w_ffn_gate=w_ffn_gate, w_ffn_up=w_ffn_up, w_ffn_down=w_ffn_down,
                   g_final=g_final)
    m_in = dict(meta_tokens=m_meta_tokens, g_mix=m_g_mix, w_in=m_w_in, b_glu=m_b_glu, b_gate=m_b_gate, w_dw=m_w_dw,
                b_dw=m_b_dw, g_conv_ln=m_g_conv_ln, b_conv_ln=m_b_conv_ln, w_conv_out=m_w_conv_out,
                b_conv_out=m_b_conv_out, g_q_lora=m_g_q_lora, w_uq=m_w_uq, g_kv_lora=m_g_kv_lora, w_uk=m_w_uk,
                w_uv=m_w_uv, w_attn_out=m_w_attn_out, w_out=m_w_out, g_ffn=m_g_ffn, w_ffn_gate=m_w_ffn_gate,
                w_ffn_up=m_w_ffn_up, w_ffn_down=m_w_ffn_down, g_final=m_g_final)
    v_in = dict(meta_tokens=v_meta_tokens, g_mix=v_g_mix, w_in=v_w_in, b_glu=v_b_glu, b_gate=v_b_gate, w_dw=v_w_dw,
                b_dw=v_b_dw, g_conv_ln=v_g_conv_ln, b_conv_ln=v_b_conv_ln, w_conv_out=v_w_conv_out,
                b_conv_out=v_b_conv_out, g_q_lora=v_g_q_lora, w_uq=v_w_uq, g_kv_lora=v_g_kv_lora, w_uk=v_w_uk,
                w_uv=v_w_uv, w_attn_out=v_w_attn_out, w_out=v_w_out, g_ffn=v_g_ffn, w_ffn_gate=v_w_ffn_gate,
                w_ffn_up=v_w_ffn_up, w_ffn_down=v_w_ffn_down, g_final=v_g_final)
    names = list(weights)

    seq, D = x.shape[1], x.shape[2]
    C = w_conv_out.shape[2]
    R = g_q_lora.shape[1]
    H = w_uv.shape[2] * N_CHIPS // V_HEAD
    FB = w_ffn_gate.shape[2]
    FF = FB * N_CHIPS
    length = N_META + seq
    T = -(-length // BLOCK_Q) * BLOCK_Q
    scale = (QK_NOPE + QK_ROPE) ** -0.5
    assert C == D and w_in.shape[2] * N_CHIPS == 2 * C + 2 * R + QK_ROPE + 2 * D

    xi, yi, ci = _place()
    chip = (2 * xi + yi).astype(jnp.int32)
    chip_arr = chip.reshape(1)
    half_arr = ci.astype(jnp.int32).reshape(1)

    tr = _pick(T, (272, 256, 128))
    trw = _pick(T, (136, 128))
    tmh = T // 2 if (T // 2) % 16 == 0 else T
    tmw = _pick(D, (1024, 512))
    nch = 4
    rc = _pick(T, (136, 128))
    tcv = 256

    def tn_of(n, pref=512):
        return _pick(n, (pref, 384, 256, 128))

    meta_all = gather_devices(meta_tokens, name="gather_meta")
    meta_full = jnp.concatenate([meta_all[2 * k] for k in range(N_CHIPS)], axis=1)

    big = ["w_in", "w_conv_out", "w_uq", "w_uk", "w_uv", "w_attn_out", "w_out", "w_ffn_gate", "w_ffn_up", "w_ffn_down"]
    gathered = gather_chips([weights[nm][0].astype(BF) for nm in big], name="gather_weights")
    G = dict(zip(big, gathered))

    def cols_full(a):
        return jnp.concatenate([a[k] for k in range(N_CHIPS)], axis=1)

    W_in = cols_full(G["w_in"])
    o1, o2 = 2 * C, 2 * C + 2 * R + QK_ROPE
    W_glu = W_in[:, :o1]
    W_small = jnp.pad(W_in[:, o1:o2], ((0, 0), (0, 128 - QK_ROPE)))
    W_gate = W_in[:, o2:]
    W_co = G["w_conv_out"].reshape(C, D)
    W_uq = cols_full(G["w_uq"]).reshape(R, H, QK_NOPE + QK_ROPE)
    Wq_p = jnp.pad(W_uq, ((0, 0), (0, 0), (0, HEAD_PAD - QK_NOPE - QK_ROPE))).reshape(R, H * HEAD_PAD)
    W_uk = cols_full(G["w_uk"])
    W_uv = cols_full(G["w_uv"])
    W_ao = G["w_attn_out"].reshape(H * V_HEAD, D)
    W_o = G["w_out"].reshape(D, D)
    W_fg, W_fu = G["w_ffn_gate"], G["w_ffn_up"]
    W_fd = G["w_ffn_down"].reshape(FF, D)

    cos, sin = _rope_tables(T)

    h0 = jnp.concatenate([meta_full, x[0], jnp.zeros((T - length, D), F32)], axis=0)
    tgt = jnp.pad(loss_target[0], ((N_META, T - length), (0, 0)))

    u = rms_fwd(h0, g_mix, tr=tr, name="rms_mix")
    z_glu = mm(u, W_glu, mode="nn", tm=T, tn=tn_of(2 * C), tk=D, out_dtype=F32, name="mm_z_glu")
    z_small = mm(u, W_small, mode="nn", tm=T, tn=tn_of(2 * R + 128, 384), tk=D, out_dtype=F32, name="mm_z_small")
    z_gate = mm(u, W_gate, mode="nn", tm=T, tn=tn_of(2 * D), tk=D, out_dtype=F32, name="mm_z_gate")

    c0 = glu_fwd(z_glu, b_glu, tr=tr, name="glu_fwd")
    dw_all = gather_devices(jnp.pad(w_dw[0], ((0, 1), (0, 0))), name="gather_w_dw")
    w_dw_full = jnp.concatenate([dw_all[2 * k] for k in range(N_CHIPS)], axis=1)
    c1 = dwconv_fwd(c0, w_dw_full, b_dw, tc=tcv, rc=rc, name="dwconv_fwd")
    c3 = ln_silu_fwd(c1, g_conv_ln, b_conv_ln, tr=tr, name="ln_silu_fwd")
    y_conv = mm(c3, W_co, mode="nn", tm=T, tn=tn_of(D), tk=C, out_dtype=F32, name="mm_y_conv")

    cq, ckv, kr = lora_norm_fwd(z_small, g_q_lora, g_kv_lora, cos, sin, tr=tr, name="lora_norm_fwd")
    q = q_proj(cq, Wq_p, cos, sin, scale=scale, name="q_proj")
    kk = k_proj(ckv, W_uk, kr, name="k_proj")
    vv = mm(ckv, W_uv, mode="nn", tm=T, tn=tn_of(H * V_HEAD), tk=R, out_dtype=BF, name="mm_v")
    o_attn, lse = attn_fwd(q, kk, vv, nch=nch, name="attn_fwd")
    y_attn = mm(o_attn, W_ao, mode="nn", tm=T, tn=tn_of(D), tk=H * V_HEAD, out_dtype=F32, name="mm_y_attn")

    mix = mix_fwd(z_gate, y_conv, y_attn, b_gate, b_conv_out, tr=tr, name="mix_fwd")
    h1 = mm(mix, W_o, mode="nn", tm=T, tn=tn_of(D), tk=D, out_dtype=F32, name="mm_h1", res=h0)

    hn = rms_fwd(h1, g_ffn, tr=tr, name="rms_ffn")
    fa = mm(hn, W_fg, mode="nn", tm=tmh, tn=FB, tk=D, out_dtype=F32, name="mm_ffn_gate")
    fb = mm(hn, W_fu, mode="nn", tm=tmh, tn=FB, tk=D, out_dtype=F32, name="mm_ffn_up")
    f = swiglu_fwd(fa, fb, tr=trw, name="swiglu_fwd")
    h2 = mm(f, W_fd, mode="nn", tm=tmh, tn=tn_of(D, 256), tk=FF, out_dtype=F32, name="mm_h2", res=h1)

    d_h2, d_h2b, loss_p, d_g_final = final_loss(h2, tgt, g_final.reshape(1, D), seq=seq, tr=tr, name="final_loss")
    loss = lax.psum(loss_p[0, 0], ("x", "y", "c"))

    d_f = mm(d_h2b, W_fd, mode="nt", tm=tmh, tn=FB, tk=D, out_dtype=F32, name="mm_d_f")
    dW_fd = mm(f, d_h2b, mode="tn", tm=FB, tn=tn_of(D), tk=T, out_dtype=BF, name="mm_dw_ffn_down")
    d_a, d_b = swiglu_bwd(fa, fb, d_f, tr=trw, name="swiglu_bwd")
    dW_fg = mm(hn, d_a, mode="tn", tm=tmw, tn=FB, tk=T, out_dtype=BF, name="mm_dw_ffn_gate", out_blocks=N_CHIPS)
    dW_fu = mm(hn, d_b, mode="tn", tm=tmw, tn=FB, tk=T, out_dtype=BF, name="mm_dw_ffn_up", out_blocks=N_CHIPS)
    d_hn = mm(d_a, W_fg, mode="nt", tm=tmh, tn=tn_of(D), tk=FB, out_dtype=F32, name="mm_d_hn_gate")
    d_hn = mm(d_b, W_fu, mode="nt", tm=tmh, tn=tn_of(D), tk=FB, out_dtype=F32, name="mm_d_hn_up", res=d_hn)
    d_h1, d_h1b, d_g_ffn = rms_bwd(d_hn, h1, g_ffn, d_h2, tr=tr, name="rms_ffn_bwd")

    d_mix = mm(d_h1b, W_o, mode="nt", tm=T, tn=tn_of(D), tk=D, out_dtype=F32, name="mm_d_mix")
    dW_o = mm(mix, d_h1b, mode="tn", tm=tmw, tn=tn_of(D), tk=T, out_dtype=BF, name="mm_dw_out")
    d_yc, d_ya, d_zgate, d_b_gate, d_b_co = mix_bwd(z_gate, y_conv, y_attn, d_mix, b_gate, b_conv_out, tr=tr,
                                                    name="mix_bwd")

    dW_co = mm(c3, d_yc, mode="tn", tm=tmw, tn=tn_of(D), tk=T, out_dtype=BF, name="mm_dw_conv_out")
    d_c3 = mm(d_yc, W_co, mode="nt", tm=T, tn=tn_of(C), tk=D, out_dtype=F32, name="mm_d_c3")
    d_c1, d_g_ln, d_b_ln, d_b_dw = ln_silu_bwd(c1, d_c3, g_conv_ln, b_conv_ln, tr=tr, name="ln_silu_bwd")
    d_c0, d_w_dw = dwconv_bwd(c0, d_c1, w_dw_full, tc=tcv, rc=rc, name="dwconv_bwd")
    d_zglu, d_b_glu = glu_bwd(z_glu, b_glu, d_c0, tr=tr, name="glu_bwd")

    dW_ao = mm(o_attn, d_ya, mode="tn", tm=_pick(H * V_HEAD, (1024, 512)), tn=tn_of(D), tk=T, out_dtype=BF, name="mm_dw_attn_out")
    d_o = mm(d_ya, W_ao, mode="nt", tm=T, tn=tn_of(H * V_HEAD), tk=D, out_dtype=BF, name="mm_d_o")
    d_q2, d_k2, d_v, d_kr = attn_bwd(q, kk, vv, o_attn, d_o, lse, cos, sin, scale=scale, nch=nch, name="attn_bwd")

    dWq_p = mm(cq, d_q2, mode="tn", tm=R, tn=tn_of(H * HEAD_PAD), tk=T, out_dtype=BF, name="mm_dw_uq")
    d_cq = mm(d_q2, Wq_p, mode="nt", tm=T, tn=R, tk=_pick(H * HEAD_PAD, (1024,)), out_dtype=F32, name="mm_d_cq")
    Wk_p = jnp.pad(W_uk.reshape(R, H, QK_NOPE), ((0, 0), (0, 0), (0, HEAD_PAD - QK_NOPE))).reshape(R, H * HEAD_PAD)
    dWk_p = mm(ckv, d_k2, mode="tn", tm=R, tn=tn_of(H * HEAD_PAD), tk=T, out_dtype=BF, name="mm_dw_uk")
    dW_uv = mm(ckv, d_v, mode="tn", tm=R, tn=tn_of(H * V_HEAD), tk=T, out_dtype=BF, name="mm_dw_uv")
    d_ckv = mm(d_k2, Wk_p, mode="nt", tm=T, tn=R, tk=_pick(H * HEAD_PAD, (1024,)), out_dtype=F32, name="mm_d_ckv_k")
    d_ckv = mm(d_v, W_uv, mode="nt", tm=T, tn=R, tk=_pick(H * V_HEAD, (1024,)), out_dtype=F32, name="mm_d_ckv_v",
               res=d_ckv)
    d_zsmall, d_g_q, d_g_kv = lora_norm_bwd(z_small, d_cq, d_ckv, d_kr, cos, sin, g_q_lora, g_kv_lora, tr=tr,
                                            name="lora_norm_bwd")

    dW_glu = mm(u, d_zglu, mode="tn", tm=tmw, tn=tn_of(2 * C), tk=T, out_dtype=BF, name="mm_dw_glu")
    dW_small = mm(u, d_zsmall, mode="tn", tm=tmw, tn=tn_of(2 * R + 128, 384), tk=T, out_dtype=BF, name="mm_dw_small")
    dW_gate = mm(u, d_zgate, mode="tn", tm=tmw, tn=tn_of(2 * D), tk=T, out_dtype=BF, name="mm_dw_gate")
    d_u = mm(d_zglu, W_glu, mode="nt", tm=T, tn=tn_of(D), tk=_pick(2 * C, (2048, 1024)), out_dtype=F32, name="mm_d_u_glu")
    d_u = mm(d_zsmall, W_small, mode="nt", tm=T, tn=tn_of(D), tk=2 * R + 128, out_dtype=F32, name="mm_d_u_small",
             res=d_u)
    d_u = mm(d_zgate, W_gate, mode="nt", tm=T, tn=tn_of(D), tk=_pick(2 * D, (2048, 1024)), out_dtype=F32,
             name="mm_d_u_gate", res=d_u)
    d_h0, _, d_g_mix = rms_bwd(d_u, h0, g_mix, d_h1, tr=tr, name="rms_mix_bwd")

    grad_x = d_h0[N_META:length][None]

    def col_blocks(a):
        r, cfull = a.shape
        return a.reshape(r, N_CHIPS, cfull // N_CHIPS).transpose(1, 0, 2)

    dW_in = jnp.concatenate([dW_glu, dW_small[:, :2 * R + QK_ROPE], dW_gate], axis=1)
    dW_uq = dWq_p.reshape(R, H, HEAD_PAD)[:, :, :QK_NOPE + QK_ROPE].reshape(R, H * (QK_NOPE + QK_ROPE))
    dW_uk = dWk_p.reshape(R, H, HEAD_PAD)[:, :, :QK_NOPE].reshape(R, H * QK_NOPE)
    gblocks = dict(
        w_in=col_blocks(dW_in), w_conv_out=dW_co.reshape(N_CHIPS, C // N_CHIPS, D), w_uq=col_blocks(dW_uq),
        w_uk=col_blocks(dW_uk), w_uv=col_blocks(dW_uv), w_attn_out=dW_ao.reshape(N_CHIPS, H * V_HEAD // N_CHIPS, D),
        w_out=dW_o.reshape(N_CHIPS, D // N_CHIPS, D), w_ffn_gate=dW_fg, w_ffn_up=dW_fu,
        w_ffn_down=dW_fd.reshape(N_CHIPS, FB, D))
    gl = [gblocks[nm] for nm in big]
    recv_a = exchange_pair(gl, name="rs_pair_exchange")
    pair = [pair_sum(g, r, half_arr, name="rs_pair_sum_" + nm) for nm, g, r in zip(big, gl, recv_a)]
    recv_b = scatter_chips(pair, name="rs_chip_scatter")
    halves = [chip_sum(s, r, chip_arr, name="rs_chip_sum_" + nm) for nm, s, r in zip(big, pair, recv_b)]
    reduced = dict(zip(big, share_halves(halves, name="rs_share_halves")))

    small = ["g_mix", "b_glu", "b_gate", "b_dw", "g_conv_ln", "b_conv_ln", "b_conv_out", "g_q_lora", "g_kv_lora",
             "g_ffn", "g_final", "w_dw", "meta_tokens"]
    sgrads = dict(g_mix=d_g_mix, b_glu=d_b_glu, b_gate=d_b_gate, b_dw=d_b_dw, g_conv_ln=d_g_ln, b_conv_ln=d_b_ln,
                  b_conv_out=d_b_co, g_q_lora=d_g_q, g_kv_lora=d_g_kv, g_ffn=d_g_ffn, g_final=d_g_final,
                  w_dw=d_w_dw[:CONV_WIDTH], meta_tokens=d_h0[:N_META])
    sizes = [int(np.prod(sgrads[nm].shape)) for nm in small]
    packed = jnp.concatenate([sgrads[nm].reshape(-1) for nm in small]).reshape(-1, 128)
    summed = sum_devices(gather_devices(packed, name="gather_small_grads"), name="sum_small_grads").reshape(-1)
    offs = np.concatenate([[0], np.cumsum(sizes)])
    sfull = {nm: summed[int(offs[i]):int(offs[i + 1])].reshape(sgrads[nm].shape) for i, nm in enumerate(small)}

    grads = {}
    for nm in big:
        grads[nm] = reduced[nm][None]
    for nm in small:
        gfull = sfull[nm]
        if nm == "w_dw":
            cb = C // N_CHIPS
            grads[nm] = lax.dynamic_slice(gfull, (0, chip * cb), (CONV_WIDTH, cb))[None]
        elif nm == "meta_tokens":
            cb = D // N_CHIPS
            grads[nm] = lax.dynamic_slice(gfull, (0, chip * cb), (N_META, cb))
        else:
            grads[nm] = gfull.reshape(weights[nm].shape)

    delta, new_m, new_v = {}, {}, {}
    rep = [nm for nm in small if nm not in ("w_dw", "meta_tokens")]

    def pack(d):
        return jnp.concatenate([d[nm].reshape(-1) for nm in rep]).reshape(-1, 128)

    pd, pm, pv = adamw(pack(weights), pack(grads), pack(m_in), pack(v_in), name="adamw_small")
    rsz = [int(np.prod(weights[nm].shape)) for nm in rep]
    roff = np.concatenate([[0], np.cumsum(rsz)])
    for i, nm in enumerate(rep):
        sl = slice(int(roff[i]), int(roff[i + 1]))
        delta[nm] = pd.reshape(-1)[sl].reshape(weights[nm].shape)
        new_m[nm] = pm.reshape(-1)[sl].reshape(weights[nm].shape)
        new_v[nm] = pv.reshape(-1)[sl].reshape(weights[nm].shape)
    for nm in big + ["w_dw", "meta_tokens"]:
        shp = weights[nm].shape
        two = (shp[-2], shp[-1])
        d_, m_, v_ = adamw(weights[nm].reshape(two), grads[nm].reshape(two), m_in[nm].reshape(two),
                           v_in[nm].reshape(two), name="adamw_" + nm)
        delta[nm], new_m[nm], new_v[nm] = d_.reshape(shp), m_.reshape(shp), v_.reshape(shp)

    return (loss, grad_x, *[grads[nm] for nm in names], *[delta[nm] for nm in names],
            *[new_m[nm] for nm in names], *[new_v[nm] for nm in names])
```

```python
import functools

import numpy as np
import jax
import jax.numpy as jnp
from jax import lax
from jax.experimental import pallas as pl
from jax.experimental.pallas import tpu as pltpu

F32 = jnp.float32
BF = jnp.bfloat16
MESH = pl.DeviceIdType.MESH

N_META = 16
BLOCK_Q = 128
CONV_WIDTH = 31
CONV_PAD = 32
QK_NOPE = 128
QK_ROPE = 64
V_HEAD = 128
HEAD_PAD = 256
ROPE_THETA = 10000.0
EPS = 1e-6
ADAM_LR = 0.001
ADAM_B1 = 0.9
ADAM_B2 = 0.999
ADAM_EPS = 1e-08
ADAM_WD = 0.01
ADAM_STEP = 10
VMEM_LIMIT = 56 * 1024 * 1024
N_CHIPS = 4
N_DEV = 8


def _cparams(sem):
    return pltpu.CompilerParams(dimension_semantics=sem, vmem_limit_bytes=VMEM_LIMIT)


def mm(a, b, *, mode, tm, tn, tk, out_dtype, name, res=None, out_blocks=None):
    b3 = b.ndim == 3
    if mode == "nn":
        M, K = a.shape
        N = b.shape[0] * b.shape[2] if b3 else b.shape[1]
        a_spec = pl.BlockSpec((tm, tk), lambda i, j, k: (i, k))
        if b3:
            per = b.shape[2] // tn
            b_spec = pl.BlockSpec((None, tk, tn), lambda i, j, k: (j // per, k, j % per))
        else:
            b_spec = pl.BlockSpec((tk, tn), lambda i, j, k: (k, j))
        dims = (((1,), (0,)), ((), ()))
    elif mode == "nt":
        M, K = a.shape
        N = b.shape[1] if b3 else b.shape[0]
        a_spec = pl.BlockSpec((tm, tk), lambda i, j, k: (i, k))
        if b3:
            per = b.shape[2] // tk
            b_spec = pl.BlockSpec((None, tn, tk), lambda i, j, k: (k // per, j, k % per))
        else:
            b_spec = pl.BlockSpec((tn, tk), lambda i, j, k: (j, k))
        dims = (((1,), (1,)), ((), ()))
    else:
        K, M = a.shape
        N = b.shape[1]
        a_spec = pl.BlockSpec((tk, tm), lambda i, j, k: (k, i))
        b_spec = pl.BlockSpec((tk, tn), lambda i, j, k: (k, j))
        dims = (((0,), (0,)), ((), ()))
    gm, gn, gk = M // tm, N // tn, K // tk
    assert gm * tm == M and gn * tn == N and gk * tk == K, (name, a.shape, b.shape, tm, tn, tk)
    if out_blocks is None:
        o_spec = pl.BlockSpec((tm, tn), lambda i, j, k: (i, j))
        o_shape = (M, N)
    else:
        nbw = N // out_blocks
        per_o = nbw // tn
        assert per_o * tn == nbw
        o_spec = pl.BlockSpec((None, tm, tn), lambda i, j, k: (j // per_o, i, j % per_o))
        o_shape = (out_blocks, M, nbw)
    has_res = res is not None

    def body(*refs):
        a_ref, b_ref = refs[0], refs[1]
        r_ref = refs[2] if has_res else None
        o_ref = refs[3] if has_res else refs[2]
        acc = refs[-1]
        p = lax.dot_general(a_ref[...], b_ref[...], dims, preferred_element_type=F32)

        def finish(v):
            if has_res:
                v = v + r_ref[...]
            o_ref[...] = v.astype(o_ref.dtype)

        if gk == 1:
            finish(p)
        else:
            k = pl.program_id(2)

            @pl.when(k == 0)
            def _():
                acc[...] = p

            @pl.when(k > 0)
            def _():
                acc[...] += p

            @pl.when(k == gk - 1)
            def _():
                finish(acc[...])

    in_specs = [a_spec, b_spec]
    args = [a, b]
    if has_res:
        in_specs.append(pl.BlockSpec((tm, tn), lambda i, j, k: (i, j)))
        args.append(res)
    return pl.pallas_call(
        body, name=name, grid=(gm, gn, gk), in_specs=in_specs, out_specs=o_spec,
        out_shape=jax.ShapeDtypeStruct(o_shape, out_dtype),
        scratch_shapes=[pltpu.VMEM((tm, tn), F32)] if gk > 1 else [],
        compiler_params=_cparams(("parallel", "parallel", "arbitrary")),
    )(*args)


def rowwise(body, row_ins, full_ins, row_outs, acc_outs, *, tr, name):
    T = row_ins[0].shape[0]
    assert T % tr == 0, (name, T, tr)
    n_ri, n_fi, n_ro = len(row_ins), len(full_ins), len(row_outs)

    def kern(*refs):
        body(pl.program_id(0), refs[:n_ri], refs[n_ri:n_ri + n_fi],
             refs[n_ri + n_fi:n_ri + n_fi + n_ro], refs[n_ri + n_fi + n_ro:])

    in_specs = [pl.BlockSpec((tr, a.shape[1]), lambda i: (i, 0)) for a in row_ins]
    in_specs += [pl.BlockSpec(a.shape, lambda i: (0, 0)) for a in full_ins]
    out_specs = [pl.BlockSpec((tr, c), lambda i: (i, 0)) for c, _ in row_outs]
    out_specs += [pl.BlockSpec(s, lambda i: (0, 0)) for s in acc_outs]
    out_shape = [jax.ShapeDtypeStruct((T, c), d) for c, d in row_outs]
    out_shape += [jax.ShapeDtypeStruct(s, F32) for s in acc_outs]
    return pl.pallas_call(
        kern, name=name, grid=(T // tr,), in_specs=in_specs, out_specs=out_specs, out_shape=out_shape,
        compiler_params=_cparams(("arbitrary",)),
    )(*row_ins, *full_ins)


def _acc(step, ref, val):
    @pl.when(step == 0)
    def _():
        ref[...] = val

    @pl.when(step > 0)
    def _():
        ref[...] += val


def _colsum(x):
    return jnp.sum(x, axis=0, keepdims=True)


def _sigmoid(x):
    return 1.0 / (1.0 + jnp.exp(-x))


def _rot_half(v):
    lane = lax.broadcasted_iota(jnp.int32, v.shape, 1)
    lo = -pltpu.roll(v, 96, 1)
    hi = pltpu.roll(v, 32, 1)
    return jnp.where(lane < 32, lo, jnp.where(lane < 64, hi, 0.0))


def rms_fwd(h, g, *, tr, name):
    def body(step, ri, fi, ro, ao):
        x = ri[0][...]
        r = lax.rsqrt(jnp.mean(x * x, axis=-1, keepdims=True) + EPS)
        ro[0][...] = ((x * r) * fi[0][...]).astype(BF)

    return rowwise(body, [h], [g], [(h.shape[1], BF)], [], tr=tr, name=name)[0]


def rms_bwd(dy, h, g, dres, *, tr, name):
    D = h.shape[1]

    def body(step, ri, fi, ro, ao):
        d, x, dr = ri[0][...], ri[1][...], ri[2][...]
        r = lax.rsqrt(jnp.mean(x * x, axis=-1, keepdims=True) + EPS)
        n = x * r
        _acc(step, ao[0], _colsum(d * n))
        dn = d * fi[0][...]
        dh = r * (dn - n * jnp.mean(dn * n, axis=-1, keepdims=True)) + dr
        ro[0][...] = dh
        ro[1][...] = dh.astype(BF)

    return rowwise(body, [dy, h, dres], [g], [(D, F32), (D, BF)], [(1, D)], tr=tr, name=name)


def glu_fwd(z, b, *, tr, name):
    C = z.shape[1] // 2

    def body(step, ri, fi, ro, ao):
        zz = ri[0][...] + fi[0][...]
        ro[0][...] = zz[:, :C] * _sigmoid(zz[:, C:])

    return rowwise(body, [z], [b], [(C, F32)], [], tr=tr, name=name)[0]


def glu_bwd(z, b, dc, *, tr, name):
    C = z.shape[1] // 2

    def body(step, ri, fi, ro, ao):
        zz = ri[0][...] + fi[0][...]
        d = ri[1][...]
        za, sg = zz[:, :C], _sigmoid(zz[:, C:])
        dza = d * sg
        dzb = d * za * sg * (1.0 - sg)
        ro[0][:, :C] = dza.astype(BF)
        ro[0][:, C:] = dzb.astype(BF)
        _acc(step, ao[0], _colsum(dza))
        _acc(step, ao[1], _colsum(dzb))

    dz, da, db = rowwise(body, [z, dc], [b], [(2 * C, BF)], [(1, C), (1, C)], tr=tr, name=name)
    return dz, jnp.concatenate([da, db], axis=1)


def ln_silu_fwd(c1, g, b, *, tr, name):
    def body(step, ri, fi, ro, ao):
        x = ri[0][...]
        mu = jnp.mean(x, axis=-1, keepdims=True)
        xc = x - mu
        rstd = lax.rsqrt(jnp.mean(xc * xc, axis=-1, keepdims=True) + EPS)
        y = (xc * rstd) * fi[0][...] + fi[1][...]
        ro[0][...] = (y * _sigmoid(y)).astype(BF)

    return rowwise(body, [c1], [g, b], [(c1.shape[1], BF)], [], tr=tr, name=name)[0]


def ln_silu_bwd(c1, dc3, g, b, *, tr, name):
    C = c1.shape[1]

    def body(step, ri, fi, ro, ao):
        x, d3 = ri[0][...], ri[1][...]
        mu = jnp.mean(x, axis=-1, keepdims=True)
        xc = x - mu
        rstd = lax.rsqrt(jnp.mean(xc * xc, axis=-1, keepdims=True) + EPS)
        n = xc * rstd
        y = n * fi[0][...] + fi[1][...]
        sg = _sigmoid(y)
        dy = d3 * (sg * (1.0 + y * (1.0 - sg)))
        _acc(step, ao[0], _colsum(dy * n))
        _acc(step, ao[1], _colsum(dy))
        dn = dy * fi[0][...]
        dx = rstd * (dn - jnp.mean(dn, axis=-1, keepdims=True) - n * jnp.mean(dn * n, axis=-1, keepdims=True))
        ro[0][...] = dx
        _acc(step, ao[2], _colsum(dx))

    return rowwise(body, [c1, dc3], [g, b], [(C, F32)], [(1, C), (1, C), (1, C)], tr=tr, name=name)


def lora_norm_fwd(zs, gq, gkv, cos, sin, *, tr, name):
    R = gq.shape[1]

    def body(step, ri, fi, ro, ao):
        z = ri[0][...]
        for o, gi in ((0, 0), (1, 1)):
            x = z[:, o * R:(o + 1) * R]
            r = lax.rsqrt(jnp.mean(x * x, axis=-1, keepdims=True) + EPS)
            ro[o][...] = ((x * r) * fi[gi][...]).astype(BF)
        kr = z[:, 2 * R:2 * R + 128]
        ro[2][...] = kr * ri[1][...] + _rot_half(kr) * ri[2][...]

    return rowwise(body, [zs, cos, sin], [gq, gkv], [(R, BF), (R, BF), (128, F32)], [], tr=tr, name=name)


def lora_norm_bwd(zs, dcq, dckv, dkr, cos, sin, gq, gkv, *, tr, name):
    R = gq.shape[1]

    def body(step, ri, fi, ro, ao):
        z = ri[0][...]
        for o in (0, 1):
            x = z[:, o * R:(o + 1) * R]
            d = ri[1 + o][...]
            r = lax.rsqrt(jnp.mean(x * x, axis=-1, keepdims=True) + EPS)
            n = x * r
            _acc(step, ao[o], _colsum(d * n))
            dn = d * fi[o][...]
            ro[0][:, o * R:(o + 1) * R] = (r * (dn - n * jnp.mean(dn * n, axis=-1, keepdims=True))).astype(BF)
        dk = ri[3][...]
        ro[0][:, 2 * R:2 * R + 128] = (dk * ri[4][...] - _rot_half(dk * ri[5][...])).astype(BF)

    return rowwise(body, [zs, dcq, dckv, dkr, cos, sin], [gq, gkv], [(2 * R + 128, BF)], [(1, R), (1, R)],
                   tr=tr, name=name)


def mix_fwd(zg, yc, ya, bg, bco, *, tr, name):
    D = yc.shape[1]

    def body(step, ri, fi, ro, ao):
        g = _sigmoid(ri[0][...] + fi[0][...])
        ro[0][...] = (g[:, :D] * (ri[1][...] + fi[1][...]) + g[:, D:] * ri[2][...]).astype(BF)

    return rowwise(body, [zg, yc, ya], [bg, bco], [(D, BF)], [], tr=tr, name=name)[0]


def mix_bwd(zg, yc, ya, dmix, bg, bco, *, tr, name):
    D = yc.shape[1]

    def body(step, ri, fi, ro, ao):
        g = _sigmoid(ri[0][...] + fi[0][...])
        gc, ga = g[:, :D], g[:, D:]
        ycv = ri[1][...] + fi[1][...]
        yav = ri[2][...]
        dm = ri[3][...]
        dyc = dm * gc
        ro[0][...] = dyc.astype(BF)
        ro[1][...] = (dm * ga).astype(BF)
        dzc = dm * ycv * gc * (1.0 - gc)
        dza = dm * yav * ga * (1.0 - ga)
        ro[2][:, :D] = dzc.astype(BF)
        ro[2][:, D:] = dza.astype(BF)
        _acc(step, ao[0], _colsum(dzc))
        _acc(step, ao[1], _colsum(dza))
        _acc(step, ao[2], _colsum(dyc))

    dyc, dya, dzg, dbc, dba, dbo = rowwise(body, [zg, yc, ya, dmix], [bg, bco], [(D, BF), (D, BF), (2 * D, BF)],
                                           [(1, D), (1, D), (1, D)], tr=tr, name=name)
    return dyc, dya, dzg, jnp.concatenate([dbc, dba], axis=1), dbo


def swiglu_fwd(a, b, *, tr, name):
    def body(step, ri, fi, ro, ao):
        x = ri[0][...]
        ro[0][...] = (x * _sigmoid(x) * ri[1][...]).astype(BF)

    return rowwise(body, [a, b], [], [(a.shape[1], BF)], [], tr=tr, name=name)[0]


def swiglu_bwd(a, b, df, *, tr, name):
    F = a.shape[1]

    def body(step, ri, fi, ro, ao):
        x, u, d = ri[0][...], ri[1][...], ri[2][...]
        sg = _sigmoid(x)
        ro[0][...] = (d * u * (sg * (1.0 + x * (1.0 - sg)))).astype(BF)
        ro[1][...] = (d * (x * sg)).astype(BF)

    return rowwise(body, [a, b, df], [], [(F, BF), (F, BF)], [], tr=tr, name=name)


def final_loss(h2, tgt, g, *, seq, tr, name):
    D = h2.shape[1]

    def body(step, ri, fi, ro, ao):
        x = ri[0][...]
        r = lax.rsqrt(jnp.mean(x * x, axis=-1, keepdims=True) + EPS)
        n = x * r
        gg = fi[0][...]
        row = lax.broadcasted_iota(jnp.int32, (tr, 1), 0) + step * tr
        live = jnp.logical_and(row >= N_META, row < N_META + seq)
        e = jnp.where(live, n * gg - ri[1][...], 0.0)
        _acc(step, ao[0], jnp.broadcast_to(0.5 * jnp.sum(jnp.mean(e * e, axis=-1, keepdims=True)), (1, 128)))
        dy = e * (1.0 / D)
        _acc(step, ao[1], _colsum(dy * n))
        dn = dy * gg
        dh = r * (dn - n * jnp.mean(dn * n, axis=-1, keepdims=True))
        ro[0][...] = dh
        ro[1][...] = dh.astype(BF)

    return rowwise(body, [h2, tgt], [g], [(D, F32), (D, BF)], [(1, 128), (1, D)], tr=tr, name=name)


def dwconv_fwd(x, w, b, *, tc, rc, name):
    T, C = x.shape
    nchunk = T // rc
    assert nchunk * rc == T and C % tc == 0

    def body(x_ref, w_ref, b_ref, y_ref, xp_ref):
        xp_ref[0:CONV_PAD, :] = jnp.zeros((CONV_PAD, tc), F32)
        xp_ref[CONV_PAD:CONV_PAD + T, :] = x_ref[...]
        off = CONV_PAD - (CONV_WIDTH - 1)
        for ci in range(nchunk):
            t0 = ci * rc
            acc = jnp.broadcast_to(b_ref[...], (rc, tc))
            for j in range(CONV_WIDTH):
                acc = acc + w_ref[j:j + 1, :] * xp_ref[t0 + off + j:t0 + off + j + rc, :]
            y_ref[t0:t0 + rc, :] = acc

    return pl.pallas_call(
        body, name=name, grid=(C // tc,),
        in_specs=[pl.BlockSpec((T, tc), lambda j: (0, j)), pl.BlockSpec((32, tc), lambda j: (0, j)),
                  pl.BlockSpec((1, tc), lambda j: (0, j))],
        out_specs=pl.BlockSpec((T, tc), lambda j: (0, j)),
        out_shape=jax.ShapeDtypeStruct((T, C), F32),
        scratch_shapes=[pltpu.VMEM((T + CONV_PAD, tc), F32)],
        compiler_params=_cparams(("parallel",)),
    )(x, w, b)


def dwconv_bwd(x, dy, w, *, tc, rc, name):
    T, C = x.shape
    nchunk = T // rc
    assert nchunk * rc == T and C % tc == 0 and rc % 8 == 0

    def body(x_ref, dy_ref, w_ref, dx_ref, dw_ref, xp_ref, dp_ref):
        off = CONV_PAD - (CONV_WIDTH - 1)
        xp_ref[0:CONV_PAD, :] = jnp.zeros((CONV_PAD, tc), F32)
        xp_ref[CONV_PAD:CONV_PAD + T, :] = x_ref[...]
        dp_ref[0:T, :] = dy_ref[...]
        dp_ref[T:T + CONV_PAD, :] = jnp.zeros((CONV_PAD, tc), F32)
        for ci in range(nchunk):
            t0 = ci * rc
            acc = jnp.zeros((rc, tc), F32)
            for j in range(CONV_WIDTH):
                s = t0 + (CONV_WIDTH - 1) - j
                acc = acc + w_ref[j:j + 1, :] * dp_ref[s:s + rc, :]
            dx_ref[t0:t0 + rc, :] = acc
        dw_ref[...] = jnp.zeros((32, tc), F32)
        for j in range(CONV_WIDTH):
            acc = jnp.zeros((8, tc), F32)
            for ci in range(nchunk):
                t0 = ci * rc
                pr = dp_ref[t0:t0 + rc, :] * xp_ref[t0 + off + j:t0 + off + j + rc, :]
                acc = acc + jnp.sum(pr.reshape(rc // 8, 8, tc), axis=0)
            dw_ref[j:j + 1, :] = jnp.sum(acc, axis=0, keepdims=True)

    return pl.pallas_call(
        body, name=name, grid=(C // tc,),
        in_specs=[pl.BlockSpec((T, tc), lambda j: (0, j)), pl.BlockSpec((T, tc), lambda j: (0, j)),
                  pl.BlockSpec((32, tc), lambda j: (0, j))],
        out_specs=[pl.BlockSpec((T, tc), lambda j: (0, j)), pl.BlockSpec((32, tc), lambda j: (0, j))],
        out_shape=[jax.ShapeDtypeStruct((T, C), F32), jax.ShapeDtypeStruct((32, C), F32)],
        scratch_shapes=[pltpu.VMEM((T + CONV_PAD, tc), F32), pltpu.VMEM((T + CONV_PAD, tc), F32)],
        compiler_params=_cparams(("parallel",)),
    )(x, dy, w)


def q_proj(cq, wq, cos, sin, *, scale, name):
    T, R = cq.shape
    H = wq.shape[1] // HEAD_PAD

    def body(a_ref, b_ref, c_ref, s_ref, o_ref):
        p = jnp.dot(a_ref[...], b_ref[...], preferred_element_type=F32)
        o_ref[:, :QK_NOPE] = (p[:, :QK_NOPE] * scale).astype(BF)
        x = p[:, QK_NOPE:]
        o_ref[:, QK_NOPE:] = ((x * c_ref[...] + _rot_half(x) * s_ref[...]) * scale).astype(BF)

    return pl.pallas_call(
        body, name=name, grid=(H,),
        in_specs=[pl.BlockSpec((T, R), lambda h: (0, 0)), pl.BlockSpec((R, HEAD_PAD), lambda h: (0, h)),
                  pl.BlockSpec((T, 128), lambda h: (0, 0)), pl.BlockSpec((T, 128), lambda h: (0, 0))],
        out_specs=pl.BlockSpec((T, HEAD_PAD), lambda h: (0, h)),
        out_shape=jax.ShapeDtypeStruct((T, H * HEAD_PAD), BF),
        compiler_params=_cparams(("parallel",)),
    )(cq, wq, cos, sin)


def k_proj(ckv, wk, kr, *, name):
    T, R = ckv.shape
    H = wk.shape[1] // QK_NOPE

    def body(a_ref, b_ref, kr_ref, o_ref):
        p = jnp.dot(a_ref[...], b_ref[...], preferred_element_type=F32)
        o_ref[:, :QK_NOPE] = p.astype(BF)
        o_ref[:, QK_NOPE:] = kr_ref[...].astype(BF)

    return pl.pallas_call(
        body, name=name, grid=(H,),
        in_specs=[pl.BlockSpec((T, R), lambda h: (0, 0)), pl.BlockSpec((R, QK_NOPE), lambda h: (0, h)),
                  pl.BlockSpec((T, 128), lambda h: (0, 0))],
        out_specs=pl.BlockSpec((T, HEAD_PAD), lambda h: (0, h)),
        out_shape=jax.ShapeDtypeStruct((T, H * HEAD_PAD), BF),
        compiler_params=_cparams(("parallel",)),
    )(ckv, wk, kr)


def _causal_scores(q, k, q0):
    s = lax.dot_general(q, k, (((1,), (1,)), ((), ())), preferred_element_type=F32)
    row = lax.broadcasted_iota(jnp.int32, s.shape, 0) + q0
    col = lax.broadcasted_iota(jnp.int32, s.shape, 1)
    return jnp.where(col <= row, s, -1e30)


def attn_fwd(q, k, v, *, nch, name):
    T = q.shape[0]
    H = q.shape[1] // HEAD_PAD
    CH = T // nch
    assert CH * nch == T and CH % 16 == 0

    def body(q_ref, k_ref, v_ref, o_ref, lse_ref):
        for qi in range(nch):
            L = (qi + 1) * CH
            s = _causal_scores(q_ref[qi * CH:L, :], k_ref[0:L, :], qi * CH)
            m = jnp.max(s, axis=1, keepdims=True)
            p = jnp.exp(s - m)
            l = jnp.sum(p, axis=1, keepdims=True)
            o = jnp.dot(p.astype(BF), v_ref[0:L, :], preferred_element_type=F32)
            o_ref[qi * CH:L, :] = (o / l).astype(BF)
            lse_ref[qi * CH:L, :] = jnp.broadcast_to(m + jnp.log(l), (CH, V_HEAD))

    return pl.pallas_call(
        body, name=name, grid=(H,),
        in_specs=[pl.BlockSpec((T, HEAD_PAD), lambda h: (0, h)), pl.BlockSpec((T, HEAD_PAD), lambda h: (0, h)),
                  pl.BlockSpec((T, V_HEAD), lambda h: (0, h))],
        out_specs=[pl.BlockSpec((T, V_HEAD), lambda h: (0, h)), pl.BlockSpec((T, V_HEAD), lambda h: (0, h))],
        out_shape=[jax.ShapeDtypeStruct((T, H * V_HEAD), BF), jax.ShapeDtypeStruct((T, H * V_HEAD), F32)],
        compiler_params=_cparams(("parallel",)),
    )(q, k, v)


def attn_bwd(q, k, v, o, do, lse, cos, sin, *, scale, nch, name):
    T = q.shape[0]
    H = q.shape[1] // HEAD_PAD
    CH = T // nch
    tn_dims = (((0,), (0,)), ((), ()))
    nt_dims = (((1,), (1,)), ((), ()))

    def body(q_ref, k_ref, v_ref, o_ref, do_ref, lse_ref, c_ref, s_ref,
             dq_ref, dk_ref, dv_ref, dkr_ref, dk_acc, dv_acc):
        h = pl.program_id(0)
        dk_acc[...] = jnp.zeros_like(dk_acc)
        dv_acc[...] = jnp.zeros_like(dv_acc)
        for qi in range(nch):
            L = (qi + 1) * CH
            rows = slice(qi * CH, L)
            qc, doc = q_ref[rows, :], do_ref[rows, :]
            kc, vc = k_ref[0:L, :], v_ref[0:L, :]
            s = _causal_scores(qc, kc, qi * CH)
            p = jnp.exp(s - lse_ref[rows, 0:1])
            dp = lax.dot_general(doc, vc, nt_dims, preferred_element_type=F32)
            dsum = jnp.sum(doc.astype(F32) * o_ref[rows, :].astype(F32), axis=1, keepdims=True)
            ds = (p * (dp - dsum)).astype(BF)
            dq = jnp.dot(ds, kc, preferred_element_type=F32)
            dq_ref[rows, :QK_NOPE] = (dq[:, :QK_NOPE] * scale).astype(BF)
            dyr = dq[:, QK_NOPE:]
            dq_ref[rows, QK_NOPE:] = ((dyr * c_ref[rows, :] - _rot_half(dyr * s_ref[rows, :])) * scale).astype(BF)
            dk_acc[0:L, :] += lax.dot_general(ds, qc, tn_dims, preferred_element_type=F32)
            dv_acc[0:L, :] += lax.dot_general(p.astype(BF), doc, tn_dims, preferred_element_type=F32)
        dk_ref[:, :QK_NOPE] = dk_acc[:, :QK_NOPE].astype(BF)
        dk_ref[:, QK_NOPE:] = jnp.zeros((T, HEAD_PAD - QK_NOPE), BF)
        dv_ref[...] = dv_acc[...].astype(BF)

        @pl.when(h == 0)
        def _():
            dkr_ref[...] = dk_acc[:, QK_NOPE:]

        @pl.when(h > 0)
        def _():
            dkr_ref[...] += dk_acc[:, QK_NOPE:]

    hp = pl.BlockSpec((T, HEAD_PAD), lambda h: (0, h))
    hv = pl.BlockSpec((T, V_HEAD), lambda h: (0, h))
    tab = pl.BlockSpec((T, 128), lambda h: (0, 0))
    return pl.pallas_call(
        body, name=name, grid=(H,),
        in_specs=[hp, hp, hv, hv, hv, hv, tab, tab],
        out_specs=[hp, hp, hv, tab],
        out_shape=[jax.ShapeDtypeStruct((T, H * HEAD_PAD), BF), jax.ShapeDtypeStruct((T, H * HEAD_PAD), BF),
                   jax.ShapeDtypeStruct((T, H * V_HEAD), BF), jax.ShapeDtypeStruct((T, 128), F32)],
        scratch_shapes=[pltpu.VMEM((T, HEAD_PAD), F32), pltpu.VMEM((T, V_HEAD), F32)],
        compiler_params=_cparams(("arbitrary",)),
    )(q, k, v, o, do, lse, cos, sin)


def _place():
    x, y, c = lax.axis_index("x"), lax.axis_index("y"), lax.axis_index("c")
    return x, y, c


def _other_chips(x, y):
    return [(1 - x, y), (x, 1 - y), (1 - x, 1 - y)]


def gather_devices(v, *, name):
    R, C = v.shape

    def body(v_ref, o_ref, send_sems, recv_sems):
        x, y, c = _place()
        me = 4 * x + 2 * y + c
        o_ref[me] = v_ref[...]
        copies = []
        for k in range(1, N_DEV):
            fx, fy, fc = (k >> 2) & 1, (k >> 1) & 1, k & 1
            to = (x ^ fx, y ^ fy, c ^ fc)
            cp = pltpu.make_async_remote_copy(src_ref=v_ref, dst_ref=o_ref.at[me], send_sem=send_sems.at[k - 1],
                                              recv_sem=recv_sems.at[k - 1], device_id=to, device_id_type=MESH)
            cp.start()
            copies.append(cp)
        for k in range(1, N_DEV):
            fx, fy, fc = (k >> 2) & 1, (k >> 1) & 1, k & 1
            frm = 4 * (x ^ fx) + 2 * (y ^ fy) + (c ^ fc)
            pltpu.make_async_remote_copy(src_ref=v_ref, dst_ref=o_ref.at[frm], send_sem=send_sems.at[k - 1],
                                         recv_sem=recv_sems.at[k - 1], device_id=(x, y, c),
                                         device_id_type=MESH).wait_recv()
        for cp in copies:
            cp.wait_send()

    return pl.pallas_call(
        body, name=name, out_shape=jax.ShapeDtypeStruct((N_DEV, R, C), F32),
        in_specs=[pl.BlockSpec(memory_space=pltpu.VMEM)], out_specs=pl.BlockSpec(memory_space=pltpu.VMEM),
        scratch_shapes=[pltpu.SemaphoreType.DMA((N_DEV - 1,)), pltpu.SemaphoreType.DMA((N_DEV - 1,))],
    )(v)


def gather_chips(bufs, *, name):
    n = len(bufs)

    def body(*refs):
        outs = refs[n:2 * n]
        send_sems, recv_sems = refs[2 * n:]
        x, y, c = _place()
        me = 2 * x + y
        sib = (x, y, 1 - c)
        chips = _other_chips(x, y)
        sends = []
        for i in range(n):
            hr = outs[i].shape[1] // 2
            mine = pl.ds(c * hr, hr)
            for j, (cx, cy) in enumerate(chips):
                cp = pltpu.make_async_remote_copy(
                    src_ref=outs[i].at[me, mine], dst_ref=outs[i].at[me, mine], send_sem=send_sems.at[i, j],
                    recv_sem=recv_sems.at[i, j], device_id=(cx, cy, c), device_id_type=MESH)
                cp.start()
                sends.append(cp)
        for i in range(n):
            hr = outs[i].shape[1] // 2
            mine = pl.ds(c * hr, hr)
            for j, (cx, cy) in enumerate(chips):
                blk = outs[i].at[2 * cx + cy, mine]
                pltpu.make_async_remote_copy(src_ref=blk, dst_ref=blk, send_sem=send_sems.at[i, j],
                                             recv_sem=recv_sems.at[i, j], device_id=sib,
                                             device_id_type=MESH).wait_recv()
                cp = pltpu.make_async_remote_copy(src_ref=blk, dst_ref=blk, send_sem=send_sems.at[i, 3 + j],
                                                  recv_sem=recv_sems.at[i, 3 + j], device_id=sib,
                                                  device_id_type=MESH)
                cp.start()
                sends.append(cp)
        for i in range(n):
            hr = outs[i].shape[1] // 2
            theirs = pl.ds((1 - c) * hr, hr)
            for j, (cx, cy) in enumerate(chips):
                blk = outs[i].at[2 * cx + cy, theirs]
                pltpu.make_async_remote_copy(src_ref=blk, dst_ref=blk, send_sem=send_sems.at[i, 3 + j],
                                             recv_sem=recv_sems.at[i, 3 + j], device_id=sib,
                                             device_id_type=MESH).wait_recv()
        for cp in sends:
            cp.wait_send()

    anyspec = pl.BlockSpec(memory_space=pl.ANY)
    return pl.pallas_call(
        body, name=name,
        out_shape=[jax.ShapeDtypeStruct(b.shape, b.dtype) for b in bufs],
        in_specs=[anyspec] * n, out_specs=[anyspec] * n,
        input_output_aliases={i: i for i in range(n)},
        scratch_shapes=[pltpu.SemaphoreType.DMA((n, 6)), pltpu.SemaphoreType.DMA((n, 6))],
    )(*bufs)


def exchange_pair(gs, *, name):
    n = len(gs)

    def body(*refs):
        ins, outs = refs[:n], refs[n:2 * n]
        send_sems, recv_sems = refs[2 * n:]
        x, y, c = _place()
        sib = (x, y, 1 - c)
        cps = []
        for i in range(n):
            hr = ins[i].shape[1] // 2
            for k in range(N_CHIPS):
                cp = pltpu.make_async_remote_copy(
                    src_ref=ins[i].at[k, pl.ds((1 - c) * hr, hr)], dst_ref=outs[i].at[k],
                    send_sem=send_sems.at[i, k], recv_sem=recv_sems.at[i, k], device_id=sib, device_id_type=MESH)
                cp.start()
                cps.append(cp)
        for cp in cps:
            cp.wait()

    anyspec = pl.BlockSpec(memory_space=pl.ANY)
    return pl.pallas_call(
        body, name=name,
        out_shape=[jax.ShapeDtypeStruct((N_CHIPS, g.shape[1] // 2, g.shape[2]), g.dtype) for g in gs],
        in_specs=[anyspec] * n, out_specs=[anyspec] * n,
        scratch_shapes=[pltpu.SemaphoreType.DMA((n, N_CHIPS)), pltpu.SemaphoreType.DMA((n, N_CHIPS))],
    )(*gs)


def scatter_chips(ss, *, name):
    n = len(ss)

    def body(*refs):
        ins, outs = refs[:n], refs[n:2 * n]
        send_sems, recv_sems = refs[2 * n:]
        x, y, c = _place()
        chips = _other_chips(x, y)
        cps = []
        for i in range(n):
            for j, (cx, cy) in enumerate(chips):
                cp = pltpu.make_async_remote_copy(
                    src_ref=ins[i].at[2 * cx + cy], dst_ref=outs[i].at[j], send_sem=send_sems.at[i, j],
                    recv_sem=recv_sems.at[i, j], device_id=(cx, cy, c), device_id_type=MESH)
                cp.start()
                cps.append(cp)
        for cp in cps:
            cp.wait()

    anyspec = pl.BlockSpec(memory_space=pl.ANY)
    return pl.pallas_call(
        body, name=name,
        out_shape=[jax.ShapeDtypeStruct((3,) + s.shape[1:], s.dtype) for s in ss],
        in_specs=[anyspec] * n, out_specs=[anyspec] * n,
        scratch_shapes=[pltpu.SemaphoreType.DMA((n, 3)), pltpu.SemaphoreType.DMA((n, 3))],
    )(*ss)


def share_halves(bufs, *, name):
    n = len(bufs)

    def body(*refs):
        outs = refs[n:2 * n]
        send_sems, recv_sems = refs[2 * n:]
        x, y, c = _place()
        sib = (x, y, 1 - c)
        cps = []
        for i in range(n):
            hr = outs[i].shape[0] // 2
            mine = outs[i].at[pl.ds(c * hr, hr)]
            cp = pltpu.make_async_remote_copy(src_ref=mine, dst_ref=mine, send_sem=send_sems.at[i],
                                              recv_sem=recv_sems.at[i], device_id=sib, device_id_type=MESH)
            cp.start()
            cps.append(cp)
        for i in range(n):
            hr = outs[i].shape[0] // 2
            theirs = outs[i].at[pl.ds((1 - c) * hr, hr)]
            pltpu.make_async_remote_copy(src_ref=theirs, dst_ref=theirs, send_sem=send_sems.at[i],
                                         recv_sem=recv_sems.at[i], device_id=sib, device_id_type=MESH).wait_recv()
        for cp in cps:
            cp.wait_send()

    anyspec = pl.BlockSpec(memory_space=pl.ANY)
    return pl.pallas_call(
        body, name=name,
        out_shape=[jax.ShapeDtypeStruct(b.shape, b.dtype) for b in bufs],
        in_specs=[anyspec] * n, out_specs=[anyspec] * n,
        input_output_aliases={i: i for i in range(n)},
        scratch_shapes=[pltpu.SemaphoreType.DMA((n,)), pltpu.SemaphoreType.DMA((n,))],
    )(*bufs)


def _row_tile(r, cols, itemsize, target_bytes=2 * 1024 * 1024):
    best = None
    for t in range(16, r + 1, 16):
        if r % t == 0 and t * cols * itemsize <= target_bytes:
            best = t
    return best if best is not None else r


def pair_sum(g, recv, half, *, name):
    _, R, C = g.shape
    hr = R // 2
    tr = _row_tile(hr, C, 4)
    nb = hr // tr

    def body(half_ref, a_ref, b_ref, o_ref):
        o_ref[...] = (a_ref[...].astype(F32) + b_ref[...].astype(F32)).astype(BF)

    return pl.pallas_call(
        body, name=name,
        grid_spec=pltpu.PrefetchScalarGridSpec(
            num_scalar_prefetch=1, grid=(N_CHIPS, nb),
            in_specs=[pl.BlockSpec((None, tr, C), lambda k, r, hf: (k, hf[0] * nb + r, 0)),
                      pl.BlockSpec((None, tr, C), lambda k, r, hf: (k, r, 0))],
            out_specs=pl.BlockSpec((None, tr, C), lambda k, r, hf: (k, r, 0))),
        out_shape=jax.ShapeDtypeStruct((N_CHIPS, hr, C), BF),
        compiler_params=_cparams(("parallel", "parallel")),
    )(half, g, recv)


def chip_sum(s, recv, place, *, name):
    _, hr, C = s.shape
    tr = _row_tile(hr, C, 4)
    nb = hr // tr

    def body(place_ref, a_ref, b_ref, o_ref):
        acc = a_ref[...].astype(F32)
        for j in range(3):
            acc = acc + b_ref[j].astype(F32)
        o_ref[...] = acc

    return pl.pallas_call(
        body, name=name,
        grid_spec=pltpu.PrefetchScalarGridSpec(
            num_scalar_prefetch=1, grid=(nb,),
            in_specs=[pl.BlockSpec((None, tr, C), lambda r, pc: (pc[0], r, 0)),
                      pl.BlockSpec((3, tr, C), lambda r, pc: (0, r, 0))],
            out_specs=pl.BlockSpec((tr, C), lambda r, pc: (pc[1] * nb + r, 0))),
        out_shape=jax.ShapeDtypeStruct((2 * hr, C), F32),
        compiler_params=_cparams(("parallel",)),
    )(place, s, recv)


def sum_devices(v, *, name):
    _, R, C = v.shape

    def body(v_ref, o_ref):
        acc = v_ref[0]
        for k in range(1, N_DEV):
            acc = acc + v_ref[k]
        o_ref[...] = acc

    return pl.pallas_call(body, name=name, out_shape=jax.ShapeDtypeStruct((R, C), F32))(v)


def adamw(w, g, m, v, *, name):
    R, C = w.shape
    tr = _row_tile(R, C, 4, target_bytes=1024 * 1024)
    c1 = 1.0 / (1.0 - ADAM_B1 ** ADAM_STEP)
    c2 = 1.0 / (1.0 - ADAM_B2 ** ADAM_STEP)

    def body(w_ref, g_ref, m_ref, v_ref, d_ref, nm_ref, nv_ref):
        gg = g_ref[...]
        nm = ADAM_B1 * m_ref[...] + (1.0 - ADAM_B1) * gg
        nv = ADAM_B2 * v_ref[...] + (1.0 - ADAM_B2) * (gg * gg)
        nm_ref[...] = nm
        nv_ref[...] = nv
        d_ref[...] = -ADAM_LR * ((nm * c1) / (jnp.sqrt(nv * c2) + ADAM_EPS) + ADAM_WD * w_ref[...])

    spec = pl.BlockSpec((tr, C), lambda i: (i, 0))
    return pl.pallas_call(
        body, name=name, grid=(R // tr,), in_specs=[spec] * 4, out_specs=[spec] * 3,
        out_shape=[jax.ShapeDtypeStruct((R, C), F32)] * 3, compiler_params=_cparams(("parallel",)),
    )(w, g, m, v)


def _rope_tables(T):
    pos = np.arange(T, dtype=np.float32)
    inv_freq = (ROPE_THETA ** (-np.arange(0, QK_ROPE, 2, dtype=np.float32) / QK_ROPE)).astype(np.float32)
    ang = pos[:, None] * inv_freq[None, :]
    z = np.zeros((T, 64), np.float32)
    cos = np.concatenate([np.cos(ang), np.cos(ang), z], axis=1).astype(np.float32)
    sin = np.concatenate([np.sin(ang), np.sin(ang), z], axis=1).astype(np.float32)
    return jnp.asarray(cos), jnp.asarray(sin)


def _pick(n, cands):
    for c in cands:
        if n % c == 0:
            return c
    return n


def kernel(x, meta_tokens, g_mix, w_in, b_glu, b_gate, w_dw, b_dw, g_conv_ln, b_conv_ln, w_conv_out, b_conv_out, g_q_lora, w_uq, g_kv_lora, w_uk, w_uv, w_attn_out, w_out, g_ffn, w_ffn_gate, w_ffn_up, w_ffn_down, g_final, loss_target, m_meta_tokens, m_g_mix, m_w_in, m_b_glu, m_b_gate, m_w_dw, m_b_dw, m_g_conv_ln, m_b_conv_ln, m_w_conv_out, m_b_conv_out, m_g_q_lora, m_w_uq, m_g_kv_lora, m_w_uk, m_w_uv, m_w_attn_out, m_w_out, m_g_ffn, m_w_ffn_gate, m_w_ffn_up, m_w_ffn_down, m_g_final, v_meta_tokens, v_g_mix, v_w_in, v_b_glu, v_b_gate, v_w_dw, v_b_dw, v_g_conv_ln, v_b_conv_ln, v_w_conv_out, v_b_conv_out, v_g_q_lora, v_w_uq, v_g_kv_lora, v_w_uk, v_w_uv, v_w_attn_out, v_w_out, v_g_ffn, v_w_ffn_gate, v_w_ffn_up, v_w_ffn_down, v_g_final):
    weights = dict(meta_tokens=meta_tokens, g_mix=g_mix, w_in=w_in, b_glu=b_glu, b_gate=b_gate, w_dw=w_dw, b_dw=b_dw,
                   g_conv_ln=g_conv_ln, b_conv_ln=b_conv_ln, w_conv_out=w_conv_out, b_conv_out=b_conv_out,
                   g_q_lora=g_q_lora, w_uq=w_uq, g_kv_lora=g_kv_lora, w_uk=w_uk, w_uv=w_uv, w_attn_out=w_attn_out,
                   w_out=w_out, g_ffn=g_ffn, w_ffn_gate=w_ffn_gate, w_ffn_up=w_ffn_up, w_ffn_down=w_ffn_down,
                   g_final=g_final)
    m_in = dict(meta_tokens=m_meta_tokens, g_mix=m_g_mix, w_in=m_w_in, b_glu=m_b_glu, b_gate=m_b_gate, w_dw=m_w_dw,
                b_dw=m_b_dw, g_conv_ln=m_g_conv_ln, b_conv_ln=m_b_conv_ln, w_conv_out=m_w_conv_out,
                b_conv_out=m_b_conv_out, g_q_lora=m_g_q_lora, w_uq=m_w_uq, g_kv_lora=m_g_kv_lora, w_uk=m_w_uk,
                w_uv=m_w_uv, w_attn_out=m_w_attn_out, w_out=m_w_out, g_ffn=m_g_ffn, w_ffn_gate=m_w_ffn_gate,
                w_ffn_up=m_w_ffn_up, w_ffn_down=m_w_ffn_down, g_final=m_g_final)
    v_in = dict(meta_tokens=v_meta_tokens, g_mix=v_g_mix, w_in=v_w_in, b_glu=v_b_glu, b_gate=v_b_gate, w_dw=v_w_dw,
                b_dw=v_b_dw, g_conv_ln=v_g_conv_ln, b_conv_ln=v_b_conv_ln, w_conv_out=v_w_conv_out,
                b_conv_out=v_b_conv_out, g_q_lora=v_g_q_lora, w_uq=v_w_uq, g_kv_lora=v_g_kv_lora, w_uk=v_w_uk,
                w_uv=v_w_uv, w_attn_out=v_w_attn_out, w_out=v_w_out, g_ffn=v_g_ffn, w_ffn_gate=v_w_ffn_gate,
                w_ffn_up=v_w_ffn_up, w_ffn_down=v_w_ffn_down, g_final=v_g_final)
    names = list(weights)

    seq, D = x.shape[1], x.shape[2]
    C = w_conv_out.shape[2]
    R = g_q_lora.shape[1]
    H = w_uv.shape[2] * N_CHIPS // V_HEAD
    FB = w_ffn_gate.shape[2]
    FF = FB * N_CHIPS
    length = N_META + seq
    T = -(-length // BLOCK_Q) * BLOCK_Q
    scale = (QK_NOPE + QK_ROPE) ** -0.5
    assert C == D and w_in.shape[2] * N_CHIPS == 2 * C + 2 * R + QK_ROPE + 2 * D

    xi, yi, ci = _place()
    chip = (2 * xi + yi).astype(jnp.int32)
    half_arr = ci.astype(jnp.int32).reshape(1)

    tr = _pick(T, (272, 256, 128))
    trw = _pick(T, (136, 128))
    tmh = T // 2 if (T // 2) % 16 == 0 else T
    tmw = _pick(D, (1024, 512))
    nch = 4
    rc = _pick(T, (136, 128))
    tcv = 256

    def tn_of(n, pref=512):
        return _pick(n, (pref, 384, 256, 128))

    meta_all = gather_devices(meta_tokens, name="gather_meta")
    meta_full = jnp.concatenate([meta_all[2 * k] for k in range(N_CHIPS)], axis=1)

    big = ["w_in", "w_conv_out", "w_uq", "w_uk", "w_uv", "w_attn_out", "w_out", "w_ffn_gate", "w_ffn_up", "w_ffn_down"]
    def own_block(nm):
        w = weights[nm][0].astype(BF)
        return lax.dynamic_update_slice(lax.empty((N_CHIPS,) + w.shape, BF), w[None], (chip, 0, 0))

    gathered = gather_chips([own_block(nm) for nm in big], name="gather_weights")
    G = dict(zip(big, gathered))

    def cols_full(a):
        return jnp.concatenate([a[k] for k in range(N_CHIPS)], axis=1)

    W_in = cols_full(G["w_in"])
    o1, o2 = 2 * C, 2 * C + 2 * R + QK_ROPE
    W_glu = W_in[:, :o1]
    W_small = jnp.pad(W_in[:, o1:o2], ((0, 0), (0, 128 - QK_ROPE)))
    W_gate = W_in[:, o2:]
    W_co = G["w_conv_out"].reshape(C, D)
    W_uq = cols_full(G["w_uq"]).reshape(R, H, QK_NOPE + QK_ROPE)
    Wq_p = jnp.pad(W_uq, ((0, 0), (0, 0), (0, HEAD_PAD - QK_NOPE - QK_ROPE))).reshape(R, H * HEAD_PAD)
    W_uk = cols_full(G["w_uk"])
    W_uv = cols_full(G["w_uv"])
    W_ao = G["w_attn_out"].reshape(H * V_HEAD, D)
    W_o = G["w_out"].reshape(D, D)
    W_fg, W_fu = G["w_ffn_gate"], G["w_ffn_up"]
    W_fd = G["w_ffn_down"].reshape(FF, D)

    cos, sin = _rope_tables(T)

    h0 = jnp.concatenate([meta_full, x[0], jnp.zeros((T - length, D), F32)], axis=0)
    tgt = jnp.pad(loss_target[0], ((N_META, T - length), (0, 0)))

    u = rms_fwd(h0, g_mix, tr=tr, name="rms_mix")
    z_glu = mm(u, W_glu, mode="nn", tm=T, tn=tn_of(2 * C), tk=D, out_dtype=F32, name="mm_z_glu")
    z_small = mm(u, W_small, mode="nn", tm=T, tn=tn_of(2 * R + 128, 384), tk=D, out_dtype=F32, name="mm_z_small")
    z_gate = mm(u, W_gate, mode="nn", tm=T, tn=tn_of(2 * D), tk=D, out_dtype=F32, name="mm_z_gate")

    c0 = glu_fwd(z_glu, b_glu, tr=tr, name="glu_fwd")
    dw_all = gather_devices(jnp.pad(w_dw[0], ((0, 1), (0, 0))), name="gather_w_dw")
    w_dw_full = jnp.concatenate([dw_all[2 * k] for k in range(N_CHIPS)], axis=1)
    c1 = dwconv_fwd(c0, w_dw_full, b_dw, tc=tcv, rc=rc, name="dwconv_fwd")
    c3 = ln_silu_fwd(c1, g_conv_ln, b_conv_ln, tr=tr, name="ln_silu_fwd")
    y_conv = mm(c3, W_co, mode="nn", tm=T, tn=tn_of(D), tk=C, out_dtype=F32, name="mm_y_conv")

    cq, ckv, kr = lora_norm_fwd(z_small, g_q_lora, g_kv_lora, cos, sin, tr=tr, name="lora_norm_fwd")
    q = q_proj(cq, Wq_p, cos, sin, scale=scale, name="q_proj")
    kk = k_proj(ckv, W_uk, kr, name="k_proj")
    vv = mm(ckv, W_uv, mode="nn", tm=T, tn=tn_of(H * V_HEAD), tk=R, out_dtype=BF, name="mm_v")
    o_attn, lse = attn_fwd(q, kk, vv, nch=nch, name="attn_fwd")
    y_attn = mm(o_attn, W_ao, mode="nn", tm=T, tn=tn_of(D), tk=H * V_HEAD, out_dtype=F32, name="mm_y_attn")

    mix = mix_fwd(z_gate, y_conv, y_attn, b_gate, b_conv_out, tr=tr, name="mix_fwd")
    h1 = mm(mix, W_o, mode="nn", tm=T, tn=tn_of(D), tk=D, out_dtype=F32, name="mm_h1", res=h0)

    hn = rms_fwd(h1, g_ffn, tr=tr, name="rms_ffn")
    fa = mm(hn, W_fg, mode="nn", tm=tmh, tn=FB, tk=D, out_dtype=F32, name="mm_ffn_gate")
    fb = mm(hn, W_fu, mode="nn", tm=tmh, tn=FB, tk=D, out_dtype=F32, name="mm_ffn_up")
    f = swiglu_fwd(fa, fb, tr=trw, name="swiglu_fwd")
    h2 = mm(f, W_fd, mode="nn", tm=tmh, tn=tn_of(D, 256), tk=FF, out_dtype=F32, name="mm_h2", res=h1)

    d_h2, d_h2b, loss_p, d_g_final = final_loss(h2, tgt, g_final.reshape(1, D), seq=seq, tr=tr, name="final_loss")
    loss = lax.psum(loss_p[0, 0], ("x", "y", "c"))

    d_f = mm(d_h2b, W_fd, mode="nt", tm=tmh, tn=FB, tk=D, out_dtype=F32, name="mm_d_f")
    dW_fd = mm(f, d_h2b, mode="tn", tm=FB, tn=tn_of(D), tk=T, out_dtype=BF, name="mm_dw_ffn_down")
    d_a, d_b = swiglu_bwd(fa, fb, d_f, tr=trw, name="swiglu_bwd")
    dW_fg = mm(hn, d_a, mode="tn", tm=tmw, tn=FB, tk=T, out_dtype=BF, name="mm_dw_ffn_gate", out_blocks=N_CHIPS)
    dW_fu = mm(hn, d_b, mode="tn", tm=tmw, tn=FB, tk=T, out_dtype=BF, name="mm_dw_ffn_up", out_blocks=N_CHIPS)
    d_hn = mm(d_a, W_fg, mode="nt", tm=tmh, tn=tn_of(D), tk=FB, out_dtype=F32, name="mm_d_hn_gate")
    d_hn = mm(d_b, W_fu, mode="nt", tm=tmh, tn=tn_of(D), tk=FB, out_dtype=F32, name="mm_d_hn_up", res=d_hn)
    d_h1, d_h1b, d_g_ffn = rms_bwd(d_hn, h1, g_ffn, d_h2, tr=tr, name="rms_ffn_bwd")

    d_mix = mm(d_h1b, W_o, mode="nt", tm=T, tn=tn_of(D), tk=D, out_dtype=F32, name="mm_d_mix")
    dW_o = mm(mix, d_h1b, mode="tn", tm=tmw, tn=tn_of(D), tk=T, out_dtype=BF, name="mm_dw_out")
    d_yc, d_ya, d_zgate, d_b_gate, d_b_co = mix_bwd(z_gate, y_conv, y_attn, d_mix, b_gate, b_conv_out, tr=tr,
                                                    name="mix_bwd")

    dW_co = mm(c3, d_yc, mode="tn", tm=tmw, tn=tn_of(D), tk=T, out_dtype=BF, name="mm_dw_conv_out")
    d_c3 = mm(d_yc, W_co, mode="nt", tm=T, tn=tn_of(C), tk=D, out_dtype=F32, name="mm_d_c3")
    d_c1, d_g_ln, d_b_ln, d_b_dw = ln_silu_bwd(c1, d_c3, g_conv_ln, b_conv_ln, tr=tr, name="ln_silu_bwd")
    d_c0, d_w_dw = dwconv_bwd(c0, d_c1, w_dw_full, tc=tcv, rc=rc, name="dwconv_bwd")
    d_zglu, d_b_glu = glu_bwd(z_glu, b_glu, d_c0, tr=tr, name="glu_bwd")

    dW_ao = mm(o_attn, d_ya, mode="tn", tm=_pick(H * V_HEAD, (1024, 512)), tn=tn_of(D), tk=T, out_dtype=BF, name="mm_dw_attn_out")
    d_o = mm(d_ya, W_ao, mode="nt", tm=T, tn=tn_of(H * V_HEAD), tk=D, out_dtype=BF, name="mm_d_o")
    d_q2, d_k2, d_v, d_kr = attn_bwd(q, kk, vv, o_attn, d_o, lse, cos, sin, scale=scale, nch=nch, name="attn_bwd")

    dWq_p = mm(cq, d_q2, mode="tn", tm=R, tn=tn_of(H * HEAD_PAD), tk=T, out_dtype=BF, name="mm_dw_uq")
    d_cq = mm(d_q2, Wq_p, mode="nt", tm=T, tn=R, tk=_pick(H * HEAD_PAD, (1024,)), out_dtype=F32, name="mm_d_cq")
    Wk_p = jnp.pad(W_uk.reshape(R, H, QK_NOPE), ((0, 0), (0, 0), (0, HEAD_PAD - QK_NOPE))).reshape(R, H * HEAD_PAD)
    dWk_p = mm(ckv, d_k2, mode="tn", tm=R, tn=tn_of(H * HEAD_PAD), tk=T, out_dtype=BF, name="mm_dw_uk")
    dW_uv = mm(ckv, d_v, mode="tn", tm=R, tn=tn_of(H * V_HEAD), tk=T, out_dtype=BF, name="mm_dw_uv")
    d_ckv = mm(d_k2, Wk_p, mode="nt", tm=T, tn=R, tk=_pick(H * HEAD_PAD, (1024,)), out_dtype=F32, name="mm_d_ckv_k")
    d_ckv = mm(d_v, W_uv, mode="nt", tm=T, tn=R, tk=_pick(H * V_HEAD, (1024,)), out_dtype=F32, name="mm_d_ckv_v",
               res=d_ckv)
    d_zsmall, d_g_q, d_g_kv = lora_norm_bwd(z_small, d_cq, d_ckv, d_kr, cos, sin, g_q_lora, g_kv_lora, tr=tr,
                                            name="lora_norm_bwd")

    dW_glu = mm(u, d_zglu, mode="tn", tm=tmw, tn=tn_of(2 * C), tk=T, out_dtype=BF, name="mm_dw_glu")
    dW_small = mm(u, d_zsmall, mode="tn", tm=tmw, tn=tn_of(2 * R + 128, 384), tk=T, out_dtype=BF, name="mm_dw_small")
    dW_gate = mm(u, d_zgate, mode="tn", tm=tmw, tn=tn_of(2 * D), tk=T, out_dtype=BF, name="mm_dw_gate")
    d_u = mm(d_zglu, W_glu, mode="nt", tm=T, tn=tn_of(D), tk=_pick(2 * C, (2048, 1024)), out_dtype=F32, name="mm_d_u_glu")
    d_u = mm(d_zsmall, W_small, mode="nt", tm=T, tn=tn_of(D), tk=2 * R + 128, out_dtype=F32, name="mm_d_u_small",
             res=d_u)
    d_u = mm(d_zgate, W_gate, mode="nt", tm=T, tn=tn_of(D), tk=_pick(2 * D, (2048, 1024)), out_dtype=F32,
             name="mm_d_u_gate", res=d_u)
    d_h0, _, d_g_mix = rms_bwd(d_u, h0, g_mix, d_h1, tr=tr, name="rms_mix_bwd")

    grad_x = d_h0[N_META:length][None]

    def col_blocks(a):
        r, cfull = a.shape
        return a.reshape(r, N_CHIPS, cfull // N_CHIPS).transpose(1, 0, 2)

    dW_in = jnp.concatenate([dW_glu, dW_small[:, :2 * R + QK_ROPE], dW_gate], axis=1)
    dW_uq = dWq_p.reshape(R, H, HEAD_PAD)[:, :, :QK_NOPE + QK_ROPE].reshape(R, H * (QK_NOPE + QK_ROPE))
    dW_uk = dWk_p.reshape(R, H, HEAD_PAD)[:, :, :QK_NOPE].reshape(R, H * QK_NOPE)
    gblocks = dict(
        w_in=col_blocks(dW_in), w_conv_out=dW_co.reshape(N_CHIPS, C // N_CHIPS, D), w_uq=col_blocks(dW_uq),
        w_uk=col_blocks(dW_uk), w_uv=col_blocks(dW_uv), w_attn_out=dW_ao.reshape(N_CHIPS, H * V_HEAD // N_CHIPS, D),
        w_out=dW_o.reshape(N_CHIPS, D // N_CHIPS, D), w_ffn_gate=dW_fg, w_ffn_up=dW_fu,
        w_ffn_down=dW_fd.reshape(N_CHIPS, FB, D))
    gl = [gblocks[nm] for nm in big]
    recv_a = exchange_pair(gl, name="rs_pair_exchange")
    pair = [pair_sum(g, r, half_arr, name="rs_pair_sum_" + nm) for nm, g, r in zip(big, gl, recv_a)]
    recv_b = scatter_chips(pair, name="rs_chip_scatter")
    place_arr = jnp.stack([chip, ci.astype(jnp.int32)])
    halves = [chip_sum(s, r, place_arr, name="rs_chip_sum_" + nm) for nm, s, r in zip(big, pair, recv_b)]
    reduced = dict(zip(big, share_halves(halves, name="rs_share_halves")))

    small = ["g_mix", "b_glu", "b_gate", "b_dw", "g_conv_ln", "b_conv_ln", "b_conv_out", "g_q_lora", "g_kv_lora",
             "g_ffn", "g_final", "w_dw", "meta_tokens"]
    sgrads = dict(g_mix=d_g_mix, b_glu=d_b_glu, b_gate=d_b_gate, b_dw=d_b_dw, g_conv_ln=d_g_ln, b_conv_ln=d_b_ln,
                  b_conv_out=d_b_co, g_q_lora=d_g_q, g_kv_lora=d_g_kv, g_ffn=d_g_ffn, g_final=d_g_final,
                  w_dw=d_w_dw[:CONV_WIDTH], meta_tokens=d_h0[:N_META])
    sizes = [int(np.prod(sgrads[nm].shape)) for nm in small]
    packed = jnp.concatenate([sgrads[nm].reshape(-1) for nm in small]).reshape(-1, 128)
    summed = sum_devices(gather_devices(packed, name="gather_small_grads"), name="sum_small_grads").reshape(-1)
    offs = np.concatenate([[0], np.cumsum(sizes)])
    sfull = {nm: summed[int(offs[i]):int(offs[i + 1])].reshape(sgrads[nm].shape) for i, nm in enumerate(small)}

    grads = {}
    for nm in big:
        grads[nm] = reduced[nm][None]
    for nm in small:
        gfull = sfull[nm]
        if nm == "w_dw":
            cb = C // N_CHIPS
            grads[nm] = lax.dynamic_slice(gfull, (0, chip * cb), (CONV_WIDTH, cb))[None]
        elif nm == "meta_tokens":
            cb = D // N_CHIPS
            grads[nm] = lax.dynamic_slice(gfull, (0, chip * cb), (N_META, cb))
        else:
            grads[nm] = gfull.reshape(weights[nm].shape)

    delta, new_m, new_v = {}, {}, {}
    rep = [nm for nm in small if nm not in ("w_dw", "meta_tokens")]

    def pack(d):
        return jnp.concatenate([d[nm].reshape(-1) for nm in rep]).reshape(-1, 128)

    pd, pm, pv = adamw(pack(weights), pack(grads), pack(m_in), pack(v_in), name="adamw_small")
    rsz = [int(np.prod(weights[nm].shape)) for nm in rep]
    roff = np.concatenate([[0], np.cumsum(rsz)])
    for i, nm in enumerate(rep):
        sl = slice(int(roff[i]), int(roff[i + 1]))
        delta[nm] = pd.reshape(-1)[sl].reshape(weights[nm].shape)
        new_m[nm] = pm.reshape(-1)[sl].reshape(weights[nm].shape)
        new_v[nm] = pv.reshape(-1)[sl].reshape(weights[nm].shape)
    for nm in big + ["w_dw", "meta_tokens"]:
        shp = weights[nm].shape
        two = (shp[-2], shp[-1])
        d_, m_, v_ = adamw(weights[nm].reshape(two), grads[nm].reshape(two), m_in[nm].reshape(two),
                           v_in[nm].reshape(two), name="adamw_" + nm)
        delta[nm], new_m[nm], new_v[nm] = d_.reshape(shp), m_.reshape(shp), v_.reshape(shp)

    return (loss, grad_x, *[grads[nm] for nm in names], *[delta[nm] for nm in names],
            *[new_m[nm] for nm in names], *[new_v[nm] for nm in names])
```

```python
import functools

import numpy as np
import jax
import jax.numpy as jnp
from jax import lax
from jax.experimental import pallas as pl
from jax.experimental.pallas import tpu as pltpu

F32 = jnp.float32
BF = jnp.bfloat16
MESH = pl.DeviceIdType.MESH

N_META = 16
BLOCK_Q = 128
CONV_WIDTH = 31
CONV_PAD = 32
QK_NOPE = 128
QK_ROPE = 64
V_HEAD = 128
HEAD_PAD = 256
ROPE_THETA = 10000.0
EPS = 1e-6
ADAM_LR = 0.001
ADAM_B1 = 0.9
ADAM_B2 = 0.999
ADAM_EPS = 1e-08
ADAM_WD = 0.01
ADAM_STEP = 10
VMEM_LIMIT = 56 * 1024 * 1024
N_CHIPS = 4
N_DEV = 8


def _cparams(sem):
    return pltpu.CompilerParams(dimension_semantics=sem, vmem_limit_bytes=VMEM_LIMIT)


def mm(a, b, *, mode, tm, tn, tk, out_dtype, name, res=None, out_blocks=None, deps=()):
    b3 = b.ndim == 3
    if mode == "nn":
        M, K = a.shape
        N = b.shape[0] * b.shape[2] if b3 else b.shape[1]
        a_spec = pl.BlockSpec((tm, tk), lambda i, j, k: (i, k))
        if b3:
            per = b.shape[2] // tn
            b_spec = pl.BlockSpec((None, tk, tn), lambda i, j, k: (j // per, k, j % per))
        else:
            b_spec = pl.BlockSpec((tk, tn), lambda i, j, k: (k, j))
        dims = (((1,), (0,)), ((), ()))
    elif mode == "nt":
        M, K = a.shape
        N = b.shape[1] if b3 else b.shape[0]
        a_spec = pl.BlockSpec((tm, tk), lambda i, j, k: (i, k))
        if b3:
            per = b.shape[2] // tk
            b_spec = pl.BlockSpec((None, tn, tk), lambda i, j, k: (k // per, j, k % per))
        else:
            b_spec = pl.BlockSpec((tn, tk), lambda i, j, k: (j, k))
        dims = (((1,), (1,)), ((), ()))
    else:
        K, M = a.shape
        N = b.shape[1]
        a_spec = pl.BlockSpec((tk, tm), lambda i, j, k: (k, i))
        b_spec = pl.BlockSpec((tk, tn), lambda i, j, k: (k, j))
        dims = (((0,), (0,)), ((), ()))
    gm, gn, gk = M // tm, N // tn, K // tk
    assert gm * tm == M and gn * tn == N and gk * tk == K, (name, a.shape, b.shape, tm, tn, tk)
    if out_blocks is None:
        o_spec = pl.BlockSpec((tm, tn), lambda i, j, k: (i, j))
        o_shape = (M, N)
    else:
        nbw = N // out_blocks
        per_o = nbw // tn
        assert per_o * tn == nbw
        o_spec = pl.BlockSpec((None, tm, tn), lambda i, j, k: (j // per_o, i, j % per_o))
        o_shape = (out_blocks, M, nbw)
    has_res = res is not None

    def body(*refs):
        a_ref, b_ref = refs[0], refs[1]
        r_ref = refs[2] if has_res else None
        o_ref = refs[2 + has_res + len(deps)]
        acc = refs[-1]
        p = lax.dot_general(a_ref[...], b_ref[...], dims, preferred_element_type=F32)

        def finish(v):
            if has_res:
                v = v + r_ref[...]
            o_ref[...] = v.astype(o_ref.dtype)

        if gk == 1:
            finish(p)
        else:
            k = pl.program_id(2)

            @pl.when(k == 0)
            def _():
                acc[...] = p

            @pl.when(k > 0)
            def _():
                acc[...] += p

            @pl.when(k == gk - 1)
            def _():
                finish(acc[...])

    in_specs = [a_spec, b_spec]
    args = [a, b]
    if has_res:
        in_specs.append(pl.BlockSpec((tm, tn), lambda i, j, k: (i, j)))
        args.append(res)
    in_specs += [pl.BlockSpec(memory_space=pl.ANY)] * len(deps)
    args += list(deps)
    return pl.pallas_call(
        body, name=name, grid=(gm, gn, gk), in_specs=in_specs, out_specs=o_spec,
        out_shape=jax.ShapeDtypeStruct(o_shape, out_dtype),
        scratch_shapes=[pltpu.VMEM((tm, tn), F32)] if gk > 1 else [],
        compiler_params=_cparams(("parallel", "parallel", "arbitrary")),
    )(*args)


def rowwise(body, row_ins, full_ins, row_outs, acc_outs, *, tr, name, deps=()):
    T = row_ins[0].shape[0]
    assert T % tr == 0, (name, T, tr)
    n_ri, n_fi, n_ro = len(row_ins), len(full_ins), len(row_outs)
    n_in = n_ri + n_fi + len(deps)

    def kern(*refs):
        body(pl.program_id(0), refs[:n_ri], refs[n_ri:n_ri + n_fi], refs[n_in:n_in + n_ro], refs[n_in + n_ro:])

    in_specs = [pl.BlockSpec((tr, a.shape[1]), lambda i: (i, 0)) for a in row_ins]
    in_specs += [pl.BlockSpec(a.shape, lambda i: (0, 0)) for a in full_ins]
    in_specs += [pl.BlockSpec(memory_space=pl.ANY)] * len(deps)
    out_specs = [pl.BlockSpec((tr, c), lambda i: (i, 0)) for c, _ in row_outs]
    out_specs += [pl.BlockSpec(s, lambda i: (0, 0)) for s in acc_outs]
    out_shape = [jax.ShapeDtypeStruct((T, c), d) for c, d in row_outs]
    out_shape += [jax.ShapeDtypeStruct(s, F32) for s in acc_outs]
    return pl.pallas_call(
        kern, name=name, grid=(T // tr,), in_specs=in_specs, out_specs=out_specs, out_shape=out_shape,
        compiler_params=_cparams(("arbitrary",)),
    )(*row_ins, *full_ins, *deps)


def _acc(step, ref, val):
    @pl.when(step == 0)
    def _():
        ref[...] = val

    @pl.when(step > 0)
    def _():
        ref[...] += val


def _colsum(x):
    return jnp.sum(x, axis=0, keepdims=True)


def _sigmoid(x):
    return 1.0 / (1.0 + jnp.exp(-x))


def _rot_half(v):
    lane = lax.broadcasted_iota(jnp.int32, v.shape, 1)
    lo = -pltpu.roll(v, 96, 1)
    hi = pltpu.roll(v, 32, 1)
    return jnp.where(lane < 32, lo, jnp.where(lane < 64, hi, 0.0))


def rms_fwd(h, g, *, tr, name, deps=()):
    def body(step, ri, fi, ro, ao):
        x = ri[0][...]
        r = lax.rsqrt(jnp.mean(x * x, axis=-1, keepdims=True) + EPS)
        ro[0][...] = ((x * r) * fi[0][...]).astype(BF)

    return rowwise(body, [h], [g], [(h.shape[1], BF)], [], tr=tr, name=name, deps=deps)[0]


def rms_bwd(dy, h, g, dres, *, tr, name):
    D = h.shape[1]

    def body(step, ri, fi, ro, ao):
        d, x, dr = ri[0][...], ri[1][...], ri[2][...]
        r = lax.rsqrt(jnp.mean(x * x, axis=-1, keepdims=True) + EPS)
        n = x * r
        _acc(step, ao[0], _colsum(d * n))
        dn = d * fi[0][...]
        dh = r * (dn - n * jnp.mean(dn * n, axis=-1, keepdims=True)) + dr
        ro[0][...] = dh
        ro[1][...] = dh.astype(BF)

    return rowwise(body, [dy, h, dres], [g], [(D, F32), (D, BF)], [(1, D)], tr=tr, name=name)


def glu_fwd(z, b, *, tr, name):
    C = z.shape[1] // 2

    def body(step, ri, fi, ro, ao):
        zz = ri[0][...] + fi[0][...]
        ro[0][...] = zz[:, :C] * _sigmoid(zz[:, C:])

    return rowwise(body, [z], [b], [(C, F32)], [], tr=tr, name=name)[0]


def glu_bwd(z, b, dc, *, tr, name):
    C = z.shape[1] // 2

    def body(step, ri, fi, ro, ao):
        zz = ri[0][...] + fi[0][...]
        d = ri[1][...]
        za, sg = zz[:, :C], _sigmoid(zz[:, C:])
        dza = d * sg
        dzb = d * za * sg * (1.0 - sg)
        ro[0][:, :C] = dza.astype(BF)
        ro[0][:, C:] = dzb.astype(BF)
        _acc(step, ao[0], _colsum(dza))
        _acc(step, ao[1], _colsum(dzb))

    dz, da, db = rowwise(body, [z, dc], [b], [(2 * C, BF)], [(1, C), (1, C)], tr=tr, name=name)
    return dz, jnp.concatenate([da, db], axis=1)


def ln_silu_fwd(c1, g, b, *, tr, name):
    def body(step, ri, fi, ro, ao):
        x = ri[0][...]
        mu = jnp.mean(x, axis=-1, keepdims=True)
        xc = x - mu
        rstd = lax.rsqrt(jnp.mean(xc * xc, axis=-1, keepdims=True) + EPS)
        y = (xc * rstd) * fi[0][...] + fi[1][...]
        ro[0][...] = (y * _sigmoid(y)).astype(BF)

    return rowwise(body, [c1], [g, b], [(c1.shape[1], BF)], [], tr=tr, name=name)[0]


def ln_silu_bwd(c1, dc3, g, b, *, tr, name):
    C = c1.shape[1]

    def body(step, ri, fi, ro, ao):
        x, d3 = ri[0][...], ri[1][...]
        mu = jnp.mean(x, axis=-1, keepdims=True)
        xc = x - mu
        rstd = lax.rsqrt(jnp.mean(xc * xc, axis=-1, keepdims=True) + EPS)
        n = xc * rstd
        y = n * fi[0][...] + fi[1][...]
        sg = _sigmoid(y)
        dy = d3 * (sg * (1.0 + y * (1.0 - sg)))
        _acc(step, ao[0], _colsum(dy * n))
        _acc(step, ao[1], _colsum(dy))
        dn = dy * fi[0][...]
        dx = rstd * (dn - jnp.mean(dn, axis=-1, keepdims=True) - n * jnp.mean(dn * n, axis=-1, keepdims=True))
        ro[0][...] = dx
        _acc(step, ao[2], _colsum(dx))

    return rowwise(body, [c1, dc3], [g, b], [(C, F32)], [(1, C), (1, C), (1, C)], tr=tr, name=name)


def lora_norm_fwd(zs, gq, gkv, cos, sin, *, tr, name):
    R = gq.shape[1]

    def body(step, ri, fi, ro, ao):
        z = ri[0][...]
        for o, gi in ((0, 0), (1, 1)):
            x = z[:, o * R:(o + 1) * R]
            r = lax.rsqrt(jnp.mean(x * x, axis=-1, keepdims=True) + EPS)
            ro[o][...] = ((x * r) * fi[gi][...]).astype(BF)
        kr = z[:, 2 * R:2 * R + 128]
        ro[2][...] = kr * ri[1][...] + _rot_half(kr) * ri[2][...]

    return rowwise(body, [zs, cos, sin], [gq, gkv], [(R, BF), (R, BF), (128, F32)], [], tr=tr, name=name)


def lora_norm_bwd(zs, dcq, dckv, dkr, cos, sin, gq, gkv, *, tr, name):
    R = gq.shape[1]

    def body(step, ri, fi, ro, ao):
        z = ri[0][...]
        for o in (0, 1):
            x = z[:, o * R:(o + 1) * R]
            d = ri[1 + o][...]
            r = lax.rsqrt(jnp.mean(x * x, axis=-1, keepdims=True) + EPS)
            n = x * r
            _acc(step, ao[o], _colsum(d * n))
            dn = d * fi[o][...]
            ro[0][:, o * R:(o + 1) * R] = (r * (dn - n * jnp.mean(dn * n, axis=-1, keepdims=True))).astype(BF)
        dk = ri[3][...]
        ro[0][:, 2 * R:2 * R + 128] = (dk * ri[4][...] - _rot_half(dk * ri[5][...])).astype(BF)

    return rowwise(body, [zs, dcq, dckv, dkr, cos, sin], [gq, gkv], [(2 * R + 128, BF)], [(1, R), (1, R)],
                   tr=tr, name=name)


def mix_fwd(zg, yc, ya, bg, bco, *, tr, name):
    D = yc.shape[1]

    def body(step, ri, fi, ro, ao):
        g = _sigmoid(ri[0][...] + fi[0][...])
        ro[0][...] = (g[:, :D] * (ri[1][...] + fi[1][...]) + g[:, D:] * ri[2][...]).astype(BF)

    return rowwise(body, [zg, yc, ya], [bg, bco], [(D, BF)], [], tr=tr, name=name)[0]


def mix_bwd(zg, yc, ya, dmix, bg, bco, *, tr, name):
    D = yc.shape[1]

    def body(step, ri, fi, ro, ao):
        g = _sigmoid(ri[0][...] + fi[0][...])
        gc, ga = g[:, :D], g[:, D:]
        ycv = ri[1][...] + fi[1][...]
        yav = ri[2][...]
        dm = ri[3][...]
        dyc = dm * gc
        ro[0][...] = dyc.astype(BF)
        ro[1][...] = (dm * ga).astype(BF)
        dzc = dm * ycv * gc * (1.0 - gc)
        dza = dm * yav * ga * (1.0 - ga)
        ro[2][:, :D] = dzc.astype(BF)
        ro[2][:, D:] = dza.astype(BF)
        _acc(step, ao[0], _colsum(dzc))
        _acc(step, ao[1], _colsum(dza))
        _acc(step, ao[2], _colsum(dyc))

    dyc, dya, dzg, dbc, dba, dbo = rowwise(body, [zg, yc, ya, dmix], [bg, bco], [(D, BF), (D, BF), (2 * D, BF)],
                                           [(1, D), (1, D), (1, D)], tr=tr, name=name)
    return dyc, dya, dzg, jnp.concatenate([dbc, dba], axis=1), dbo


def swiglu_fwd(a, b, *, tr, name):
    def body(step, ri, fi, ro, ao):
        x = ri[0][...]
        ro[0][...] = (x * _sigmoid(x) * ri[1][...]).astype(BF)

    return rowwise(body, [a, b], [], [(a.shape[1], BF)], [], tr=tr, name=name)[0]


def swiglu_bwd(a, b, df, *, tr, name):
    F = a.shape[1]

    def body(step, ri, fi, ro, ao):
        x, u, d = ri[0][...], ri[1][...], ri[2][...]
        sg = _sigmoid(x)
        ro[0][...] = (d * u * (sg * (1.0 + x * (1.0 - sg)))).astype(BF)
        ro[1][...] = (d * (x * sg)).astype(BF)

    return rowwise(body, [a, b, df], [], [(F, BF), (F, BF)], [], tr=tr, name=name)


def final_loss(h2, tgt, g, *, seq, tr, name):
    D = h2.shape[1]

    def body(step, ri, fi, ro, ao):
        x = ri[0][...]
        r = lax.rsqrt(jnp.mean(x * x, axis=-1, keepdims=True) + EPS)
        n = x * r
        gg = fi[0][...]
        row = lax.broadcasted_iota(jnp.int32, (tr, 1), 0) + step * tr
        live = jnp.logical_and(row >= N_META, row < N_META + seq)
        e = jnp.where(live, n * gg - ri[1][...], 0.0)
        _acc(step, ao[0], jnp.broadcast_to(0.5 * jnp.sum(jnp.mean(e * e, axis=-1, keepdims=True)), (1, 128)))
        dy = e * (1.0 / D)
        _acc(step, ao[1], _colsum(dy * n))
        dn = dy * gg
        dh = r * (dn - n * jnp.mean(dn * n, axis=-1, keepdims=True))
        ro[0][...] = dh
        ro[1][...] = dh.astype(BF)

    return rowwise(body, [h2, tgt], [g], [(D, F32), (D, BF)], [(1, 128), (1, D)], tr=tr, name=name)


def dwconv_fwd(x, w, b, *, tc, rc, name):
    T, C = x.shape
    nchunk = T // rc
    assert nchunk * rc == T and C % tc == 0

    def body(x_ref, w_ref, b_ref, y_ref, xp_ref):
        xp_ref[0:CONV_PAD, :] = jnp.zeros((CONV_PAD, tc), F32)
        xp_ref[CONV_PAD:CONV_PAD + T, :] = x_ref[...]
        off = CONV_PAD - (CONV_WIDTH - 1)
        for ci in range(nchunk):
            t0 = ci * rc
            acc = jnp.broadcast_to(b_ref[...], (rc, tc))
            for j in range(CONV_WIDTH):
                acc = acc + w_ref[j:j + 1, :] * xp_ref[t0 + off + j:t0 + off + j + rc, :]
            y_ref[t0:t0 + rc, :] = acc

    return pl.pallas_call(
        body, name=name, grid=(C // tc,),
        in_specs=[pl.BlockSpec((T, tc), lambda j: (0, j)), pl.BlockSpec((32, tc), lambda j: (0, j)),
                  pl.BlockSpec((1, tc), lambda j: (0, j))],
        out_specs=pl.BlockSpec((T, tc), lambda j: (0, j)),
        out_shape=jax.ShapeDtypeStruct((T, C), F32),
        scratch_shapes=[pltpu.VMEM((T + CONV_PAD, tc), F32)],
        compiler_params=_cparams(("parallel",)),
    )(x, w, b)


def dwconv_bwd(x, dy, w, *, tc, rc, name):
    T, C = x.shape
    nchunk = T // rc
    assert nchunk * rc == T and C % tc == 0 and rc % 8 == 0

    def body(x_ref, dy_ref, w_ref, dx_ref, dw_ref, xp_ref, dp_ref):
        off = CONV_PAD - (CONV_WIDTH - 1)
        xp_ref[0:CONV_PAD, :] = jnp.zeros((CONV_PAD, tc), F32)
        xp_ref[CONV_PAD:CONV_PAD + T, :] = x_ref[...]
        dp_ref[0:T, :] = dy_ref[...]
        dp_ref[T:T + CONV_PAD, :] = jnp.zeros((CONV_PAD, tc), F32)
        for ci in range(nchunk):
            t0 = ci * rc
            acc = jnp.zeros((rc, tc), F32)
            for j in range(CONV_WIDTH):
                s = t0 + (CONV_WIDTH - 1) - j
                acc = acc + w_ref[j:j + 1, :] * dp_ref[s:s + rc, :]
            dx_ref[t0:t0 + rc, :] = acc
        dw_ref[...] = jnp.zeros((32, tc), F32)
        for j in range(CONV_WIDTH):
            acc = jnp.zeros((8, tc), F32)
            for ci in range(nchunk):
                t0 = ci * rc
                pr = dp_ref[t0:t0 + rc, :] * xp_ref[t0 + off + j:t0 + off + j + rc, :]
                acc = acc + jnp.sum(pr.reshape(rc // 8, 8, tc), axis=0)
            dw_ref[j:j + 1, :] = jnp.sum(acc, axis=0, keepdims=True)

    return pl.pallas_call(
        body, name=name, grid=(C // tc,),
        in_specs=[pl.BlockSpec((T, tc), lambda j: (0, j)), pl.BlockSpec((T, tc), lambda j: (0, j)),
                  pl.BlockSpec((32, tc), lambda j: (0, j))],
        out_specs=[pl.BlockSpec((T, tc), lambda j: (0, j)), pl.BlockSpec((32, tc), lambda j: (0, j))],
        out_shape=[jax.ShapeDtypeStruct((T, C), F32), jax.ShapeDtypeStruct((32, C), F32)],
        scratch_shapes=[pltpu.VMEM((T + CONV_PAD, tc), F32), pltpu.VMEM((T + CONV_PAD, tc), F32)],
        compiler_params=_cparams(("parallel",)),
    )(x, dy, w)


def q_proj(cq, wq, cos, sin, *, scale, name):
    T, R = cq.shape
    H = wq.shape[1] // HEAD_PAD

    def body(a_ref, b_ref, c_ref, s_ref, o_ref):
        p = jnp.dot(a_ref[...], b_ref[...], preferred_element_type=F32)
        o_ref[:, :QK_NOPE] = (p[:, :QK_NOPE] * scale).astype(BF)
        x = p[:, QK_NOPE:]
        o_ref[:, QK_NOPE:] = ((x * c_ref[...] + _rot_half(x) * s_ref[...]) * scale).astype(BF)

    return pl.pallas_call(
        body, name=name, grid=(H,),
        in_specs=[pl.BlockSpec((T, R), lambda h: (0, 0)), pl.BlockSpec((R, HEAD_PAD), lambda h: (0, h)),
                  pl.BlockSpec((T, 128), lambda h: (0, 0)), pl.BlockSpec((T, 128), lambda h: (0, 0))],
        out_specs=pl.BlockSpec((T, HEAD_PAD), lambda h: (0, h)),
        out_shape=jax.ShapeDtypeStruct((T, H * HEAD_PAD), BF),
        compiler_params=_cparams(("parallel",)),
    )(cq, wq, cos, sin)


def k_proj(ckv, wk, kr, *, name):
    T, R = ckv.shape
    H = wk.shape[1] // QK_NOPE

    def body(a_ref, b_ref, kr_ref, o_ref):
        p = jnp.dot(a_ref[...], b_ref[...], preferred_element_type=F32)
        o_ref[:, :QK_NOPE] = p.astype(BF)
        o_ref[:, QK_NOPE:] = kr_ref[...].astype(BF)

    return pl.pallas_call(
        body, name=name, grid=(H,),
        in_specs=[pl.BlockSpec((T, R), lambda h: (0, 0)), pl.BlockSpec((R, QK_NOPE), lambda h: (0, h)),
                  pl.BlockSpec((T, 128), lambda h: (0, 0))],
        out_specs=pl.BlockSpec((T, HEAD_PAD), lambda h: (0, h)),
        out_shape=jax.ShapeDtypeStruct((T, H * HEAD_PAD), BF),
        compiler_params=_cparams(("parallel",)),
    )(ckv, wk, kr)


def _causal_scores(q, k, q0):
    s = lax.dot_general(q, k, (((1,), (1,)), ((), ())), preferred_element_type=F32)
    row = lax.broadcasted_iota(jnp.int32, s.shape, 0) + q0
    col = lax.broadcasted_iota(jnp.int32, s.shape, 1)
    return jnp.where(col <= row, s, -1e30)


def attn_fwd(q, k, v, *, nch, name):
    T = q.shape[0]
    H = q.shape[1] // HEAD_PAD
    CH = T // nch
    assert CH * nch == T and CH % 16 == 0

    def body(q_ref, k_ref, v_ref, o_ref, lse_ref):
        for qi in range(nch):
            L = (qi + 1) * CH
            s = _causal_scores(q_ref[qi * CH:L, :], k_ref[0:L, :], qi * CH)
            m = jnp.max(s, axis=1, keepdims=True)
            p = jnp.exp(s - m)
            l = jnp.sum(p, axis=1, keepdims=True)
            o = jnp.dot(p.astype(BF), v_ref[0:L, :], preferred_element_type=F32)
            o_ref[qi * CH:L, :] = (o / l).astype(BF)
            lse_ref[qi * CH:L, :] = jnp.broadcast_to(m + jnp.log(l), (CH, V_HEAD))

    return pl.pallas_call(
        body, name=name, grid=(H,),
        in_specs=[pl.BlockSpec((T, HEAD_PAD), lambda h: (0, h)), pl.BlockSpec((T, HEAD_PAD), lambda h: (0, h)),
                  pl.BlockSpec((T, V_HEAD), lambda h: (0, h))],
        out_specs=[pl.BlockSpec((T, V_HEAD), lambda h: (0, h)), pl.BlockSpec((T, V_HEAD), lambda h: (0, h))],
        out_shape=[jax.ShapeDtypeStruct((T, H * V_HEAD), BF), jax.ShapeDtypeStruct((T, H * V_HEAD), F32)],
        compiler_params=_cparams(("parallel",)),
    )(q, k, v)


def attn_bwd(q, k, v, o, do, lse, cos, sin, *, scale, nch, name):
    T = q.shape[0]
    H = q.shape[1] // HEAD_PAD
    CH = T // nch
    tn_dims = (((0,), (0,)), ((), ()))
    nt_dims = (((1,), (1,)), ((), ()))

    def body(q_ref, k_ref, v_ref, o_ref, do_ref, lse_ref, c_ref, s_ref,
             dq_ref, dk_ref, dv_ref, dkr_ref, dk_acc, dv_acc):
        h = pl.program_id(0)
        dk_acc[...] = jnp.zeros_like(dk_acc)
        dv_acc[...] = jnp.zeros_like(dv_acc)
        for qi in range(nch):
            L = (qi + 1) * CH
            rows = slice(qi * CH, L)
            qc, doc = q_ref[rows, :], do_ref[rows, :]
            kc, vc = k_ref[0:L, :], v_ref[0:L, :]
            s = _causal_scores(qc, kc, qi * CH)
            p = jnp.exp(s - lse_ref[rows, 0:1])
            dp = lax.dot_general(doc, vc, nt_dims, preferred_element_type=F32)
            dsum = jnp.sum(doc.astype(F32) * o_ref[rows, :].astype(F32), axis=1, keepdims=True)
            ds = (p * (dp - dsum)).astype(BF)
            dq = jnp.dot(ds, kc, preferred_element_type=F32)
            dq_ref[rows, :QK_NOPE] = (dq[:, :QK_NOPE] * scale).astype(BF)
            dyr = dq[:, QK_NOPE:]
            dq_ref[rows, QK_NOPE:] = ((dyr * c_ref[rows, :] - _rot_half(dyr * s_ref[rows, :])) * scale).astype(BF)
            dk_acc[0:L, :] += lax.dot_general(ds, qc, tn_dims, preferred_element_type=F32)
            dv_acc[0:L, :] += lax.dot_general(p.astype(BF), doc, tn_dims, preferred_element_type=F32)
        dk_ref[:, :QK_NOPE] = dk_acc[:, :QK_NOPE].astype(BF)
        dk_ref[:, QK_NOPE:] = jnp.zeros((T, HEAD_PAD - QK_NOPE), BF)
        dv_ref[...] = dv_acc[...].astype(BF)

        @pl.when(h == 0)
        def _():
            dkr_ref[...] = dk_acc[:, QK_NOPE:]

        @pl.when(h > 0)
        def _():
            dkr_ref[...] += dk_acc[:, QK_NOPE:]

    hp = pl.BlockSpec((T, HEAD_PAD), lambda h: (0, h))
    hv = pl.BlockSpec((T, V_HEAD), lambda h: (0, h))
    tab = pl.BlockSpec((T, 128), lambda h: (0, 0))
    return pl.pallas_call(
        body, name=name, grid=(H,),
        in_specs=[hp, hp, hv, hv, hv, hv, tab, tab],
        out_specs=[hp, hp, hv, tab],
        out_shape=[jax.ShapeDtypeStruct((T, H * HEAD_PAD), BF), jax.ShapeDtypeStruct((T, H * HEAD_PAD), BF),
                   jax.ShapeDtypeStruct((T, H * V_HEAD), BF), jax.ShapeDtypeStruct((T, 128), F32)],
        scratch_shapes=[pltpu.VMEM((T, HEAD_PAD), F32), pltpu.VMEM((T, V_HEAD), F32)],
        compiler_params=_cparams(("arbitrary",)),
    )(q, k, v, o, do, lse, cos, sin)


def _place():
    x, y, c = lax.axis_index("x"), lax.axis_index("y"), lax.axis_index("c")
    return x, y, c


def _other_chips(x, y):
    return [(1 - x, y), (x, 1 - y), (1 - x, 1 - y)]


def gather_devices(v, *, name):
    R, C = v.shape

    def body(v_ref, o_ref, send_sems, recv_sems):
        x, y, c = _place()
        me = 4 * x + 2 * y + c
        o_ref[me] = v_ref[...]
        copies = []
        for k in range(1, N_DEV):
            fx, fy, fc = (k >> 2) & 1, (k >> 1) & 1, k & 1
            to = (x ^ fx, y ^ fy, c ^ fc)
            cp = pltpu.make_async_remote_copy(src_ref=v_ref, dst_ref=o_ref.at[me], send_sem=send_sems.at[k - 1],
                                              recv_sem=recv_sems.at[k - 1], device_id=to, device_id_type=MESH)
            cp.start()
            copies.append(cp)
        for k in range(1, N_DEV):
            fx, fy, fc = (k >> 2) & 1, (k >> 1) & 1, k & 1
            frm = 4 * (x ^ fx) + 2 * (y ^ fy) + (c ^ fc)
            pltpu.make_async_remote_copy(src_ref=v_ref, dst_ref=o_ref.at[frm], send_sem=send_sems.at[k - 1],
                                         recv_sem=recv_sems.at[k - 1], device_id=(x, y, c),
                                         device_id_type=MESH).wait_recv()
        for cp in copies:
            cp.wait_send()

    return pl.pallas_call(
        body, name=name, out_shape=jax.ShapeDtypeStruct((N_DEV, R, C), F32),
        in_specs=[pl.BlockSpec(memory_space=pltpu.VMEM)], out_specs=pl.BlockSpec(memory_space=pltpu.VMEM),
        scratch_shapes=[pltpu.SemaphoreType.DMA((N_DEV - 1,)), pltpu.SemaphoreType.DMA((N_DEV - 1,))],
    )(v)


def gather_chips(bufs, *, name):
    n = len(bufs)

    def body(*refs):
        outs = refs[n:2 * n]
        send_sems, recv_sems = refs[2 * n:]
        x, y, c = _place()
        me = 2 * x + y
        sib = (x, y, 1 - c)
        chips = _other_chips(x, y)
        sends = []
        for i in range(n):
            hr = outs[i].shape[1] // 2
            mine = pl.ds(c * hr, hr)
            for j, (cx, cy) in enumerate(chips):
                cp = pltpu.make_async_remote_copy(
                    src_ref=outs[i].at[me, mine], dst_ref=outs[i].at[me, mine], send_sem=send_sems.at[i, j],
                    recv_sem=recv_sems.at[i, j], device_id=(cx, cy, c), device_id_type=MESH)
                cp.start()
                sends.append(cp)
        for i in range(n):
            hr = outs[i].shape[1] // 2
            mine = pl.ds(c * hr, hr)
            for j, (cx, cy) in enumerate(chips):
                blk = outs[i].at[2 * cx + cy, mine]
                pltpu.make_async_remote_copy(src_ref=blk, dst_ref=blk, send_sem=send_sems.at[i, j],
                                             recv_sem=recv_sems.at[i, j], device_id=sib,
                                             device_id_type=MESH).wait_recv()
                cp = pltpu.make_async_remote_copy(src_ref=blk, dst_ref=blk, send_sem=send_sems.at[i, 3 + j],
                                                  recv_sem=recv_sems.at[i, 3 + j], device_id=sib,
                                                  device_id_type=MESH)
                cp.start()
                sends.append(cp)
        for i in range(n):
            hr = outs[i].shape[1] // 2
            theirs = pl.ds((1 - c) * hr, hr)
            for j, (cx, cy) in enumerate(chips):
                blk = outs[i].at[2 * cx + cy, theirs]
                pltpu.make_async_remote_copy(src_ref=blk, dst_ref=blk, send_sem=send_sems.at[i, 3 + j],
                                             recv_sem=recv_sems.at[i, 3 + j], device_id=sib,
                                             device_id_type=MESH).wait_recv()
        for cp in sends:
            cp.wait_send()

    anyspec = pl.BlockSpec(memory_space=pl.ANY)
    return pl.pallas_call(
        body, name=name,
        out_shape=[jax.ShapeDtypeStruct(b.shape, b.dtype) for b in bufs],
        in_specs=[anyspec] * n, out_specs=[anyspec] * n,
        input_output_aliases={i: i for i in range(n)},
        scratch_shapes=[pltpu.SemaphoreType.DMA((n, 6)), pltpu.SemaphoreType.DMA((n, 6))],
    )(*bufs)


HBM_SPEC = pl.BlockSpec(memory_space=pltpu.HBM)
SEM_SPEC = pl.BlockSpec(memory_space=pltpu.SEMAPHORE)
DATAFLOW = pltpu.SideEffectType.DATAFLOW_SIDE_EFFECTING


def _hbm(a):
    return pltpu.with_memory_space_constraint(a, pltpu.HBM)


def _gather_ici_copies(bufs, send_sems, recv_sems, receiving):
    x, y, c = _place()
    me = 2 * x + y
    out = []
    for i, b in enumerate(bufs):
        hr = b.shape[1] // 2
        mine = pl.ds(c * hr, hr)
        for j, (cx, cy) in enumerate(_other_chips(x, y)):
            src = b.at[me, mine]
            dst = b.at[2 * cx + cy, mine] if receiving else src
            out.append(pltpu.make_async_remote_copy(
                src_ref=src, dst_ref=dst, send_sem=send_sems.at[3 * i + j], recv_sem=recv_sems.at[3 * i + j],
                device_id=(cx, cy, c), device_id_type=MESH))
    return out


def gather_start(bufs, after, *, name):
    n = len(bufs)

    def body(*refs):
        outs = refs[n + 1:2 * n + 1]
        send_sems, recv_sems, token = refs[2 * n + 1:]
        for cp in _gather_ici_copies(outs, send_sems, recv_sems, False):
            cp.start()
        token[...] = jnp.zeros_like(token)

    res = pl.pallas_call(
        body, name=name,
        out_shape=tuple(pltpu.HBM(b.shape, b.dtype) for b in bufs)
        + (pltpu.SemaphoreType.DMA((3 * n,)), pltpu.SemaphoreType.DMA((3 * n,)), jax.ShapeDtypeStruct((8, 128), F32)),
        in_specs=(HBM_SPEC,) * n + (pl.BlockSpec(memory_space=pl.ANY),),
        out_specs=(HBM_SPEC,) * n + (SEM_SPEC, SEM_SPEC, pl.BlockSpec(memory_space=pltpu.VMEM)),
        input_output_aliases={i: i for i in range(n)},
        compiler_params=pltpu.CompilerParams(has_side_effects=DATAFLOW),
    )(*[_hbm(b) for b in bufs], after)
    return list(res[:n]), res[n], res[n + 1], res[n + 2]


def gather_wait(bufs, send_sems, recv_sems, after, *, name):
    n = len(bufs)

    def body(*refs):
        for cp in _gather_ici_copies(refs[:n], refs[n], refs[n + 1], True):
            cp.wait_send()
            cp.wait_recv()

    res = pl.pallas_call(
        body, name=name,
        out_shape=tuple(pltpu.HBM(b.shape, b.dtype) for b in bufs),
        in_specs=(HBM_SPEC,) * n + (SEM_SPEC, SEM_SPEC, pl.BlockSpec(memory_space=pl.ANY)),
        out_specs=(HBM_SPEC,) * n,
        input_output_aliases={i: i for i in range(n)},
        compiler_params=pltpu.CompilerParams(has_side_effects=DATAFLOW),
    )(*bufs, send_sems, recv_sems, after)
    return list(res)


def forward_pair(bufs, *, name):
    n = len(bufs)

    def body(*refs):
        outs = refs[n:2 * n]
        send_sems, recv_sems = refs[2 * n:]
        x, y, c = _place()
        sib = (x, y, 1 - c)
        cps = []
        for i in range(n):
            hr = outs[i].shape[1] // 2
            for j, (cx, cy) in enumerate(_other_chips(x, y)):
                blk = outs[i].at[2 * cx + cy, pl.ds(c * hr, hr)]
                cp = pltpu.make_async_remote_copy(src_ref=blk, dst_ref=blk, send_sem=send_sems.at[i, j],
                                                  recv_sem=recv_sems.at[i, j], device_id=sib, device_id_type=MESH)
                cp.start()
                cps.append(cp)
        for i in range(n):
            hr = outs[i].shape[1] // 2
            for j, (cx, cy) in enumerate(_other_chips(x, y)):
                blk = outs[i].at[2 * cx + cy, pl.ds((1 - c) * hr, hr)]
                pltpu.make_async_remote_copy(src_ref=blk, dst_ref=blk, send_sem=send_sems.at[i, j],
                                             recv_sem=recv_sems.at[i, j], device_id=sib,
                                             device_id_type=MESH).wait_recv()
        for cp in cps:
            cp.wait_send()

    anyspec = pl.BlockSpec(memory_space=pl.ANY)
    return pl.pallas_call(
        body, name=name,
        out_shape=[jax.ShapeDtypeStruct(b.shape, b.dtype) for b in bufs],
        in_specs=[anyspec] * n, out_specs=[anyspec] * n,
        input_output_aliases={i: i for i in range(n)},
        scratch_shapes=[pltpu.SemaphoreType.DMA((n, 3)), pltpu.SemaphoreType.DMA((n, 3))],
    )(*bufs)


def _scatter_copies(srcs, lands, send_sems, recv_sems):
    x, y, c = _place()
    out = []
    for i in range(len(srcs)):
        for j, (cx, cy) in enumerate(_other_chips(x, y)):
            out.append(pltpu.make_async_remote_copy(
                src_ref=srcs[i].at[2 * cx + cy], dst_ref=lands[i].at[j], send_sem=send_sems.at[3 * i + j],
                recv_sem=recv_sems.at[3 * i + j], device_id=(cx, cy, c), device_id_type=MESH))
    return out


def scatter_start(ss, *, name):
    n = len(ss)
    lands = [lax.empty((3,) + s.shape[1:], s.dtype) for s in ss]

    def body(*refs):
        outs = refs[2 * n:4 * n]
        send_sems, recv_sems, token = refs[4 * n:]
        for cp in _scatter_copies(outs[:n], outs[n:], send_sems, recv_sems):
            cp.start()
        token[...] = jnp.zeros_like(token)

    res = pl.pallas_call(
        body, name=name,
        out_shape=tuple(pltpu.HBM(a.shape, a.dtype) for a in ss + lands)
        + (pltpu.SemaphoreType.DMA((3 * n,)), pltpu.SemaphoreType.DMA((3 * n,)), jax.ShapeDtypeStruct((8, 128), F32)),
        in_specs=(HBM_SPEC,) * (2 * n),
        out_specs=(HBM_SPEC,) * (2 * n) + (SEM_SPEC, SEM_SPEC, pl.BlockSpec(memory_space=pltpu.VMEM)),
        input_output_aliases={i: i for i in range(2 * n)},
        compiler_params=pltpu.CompilerParams(has_side_effects=DATAFLOW),
    )(*[_hbm(a) for a in ss + lands])
    return list(res[:n]), list(res[n:2 * n]), res[2 * n], res[2 * n + 1], res[2 * n + 2]


def scatter_wait(ss, lands, send_sems, recv_sems, after, *, name):
    n = len(ss)

    def body(*refs):
        for cp in _scatter_copies(refs[:n], refs[n:2 * n], refs[2 * n], refs[2 * n + 1]):
            cp.wait_send()
            cp.wait_recv()

    res = pl.pallas_call(
        body, name=name,
        out_shape=tuple(pltpu.HBM(a.shape, a.dtype) for a in ss + lands),
        in_specs=(HBM_SPEC,) * (2 * n) + (SEM_SPEC, SEM_SPEC, pl.BlockSpec(memory_space=pl.ANY)),
        out_specs=(HBM_SPEC,) * (2 * n),
        input_output_aliases={i: i for i in range(2 * n)},
        compiler_params=pltpu.CompilerParams(has_side_effects=DATAFLOW),
    )(*ss, *lands, send_sems, recv_sems, after)
    return list(res[:n]), list(res[n:])


def exchange_pair(gs, *, name):
    n = len(gs)

    def body(*refs):
        ins, outs = refs[:n], refs[n:2 * n]
        send_sems, recv_sems = refs[2 * n:]
        x, y, c = _place()
        sib = (x, y, 1 - c)
        cps = []
        for i in range(n):
            hr = ins[i].shape[1] // 2
            for k in range(N_CHIPS):
                cp = pltpu.make_async_remote_copy(
                    src_ref=ins[i].at[k, pl.ds((1 - c) * hr, hr)], dst_ref=outs[i].at[k],
                    send_sem=send_sems.at[i, k], recv_sem=recv_sems.at[i, k], device_id=sib, device_id_type=MESH)
                cp.start()
                cps.append(cp)
        for cp in cps:
            cp.wait()

    anyspec = pl.BlockSpec(memory_space=pl.ANY)
    return pl.pallas_call(
        body, name=name,
        out_shape=[jax.ShapeDtypeStruct((N_CHIPS, g.shape[1] // 2, g.shape[2]), g.dtype) for g in gs],
        in_specs=[anyspec] * n, out_specs=[anyspec] * n,
        scratch_shapes=[pltpu.SemaphoreType.DMA((n, N_CHIPS)), pltpu.SemaphoreType.DMA((n, N_CHIPS))],
    )(*gs)


def scatter_chips(ss, *, name):
    n = len(ss)

    def body(*refs):
        ins, outs = refs[:n], refs[n:2 * n]
        send_sems, recv_sems = refs[2 * n:]
        x, y, c = _place()
        chips = _other_chips(x, y)
        cps = []
        for i in range(n):
            for j, (cx, cy) in enumerate(chips):
                cp = pltpu.make_async_remote_copy(
                    src_ref=ins[i].at[2 * cx + cy], dst_ref=outs[i].at[j], send_sem=send_sems.at[i, j],
                    recv_sem=recv_sems.at[i, j], device_id=(cx, cy, c), device_id_type=MESH)
                cp.start()
                cps.append(cp)
        for cp in cps:
            cp.wait()

    anyspec = pl.BlockSpec(memory_space=pl.ANY)
    return pl.pallas_call(
        body, name=name,
        out_shape=[jax.ShapeDtypeStruct((3,) + s.shape[1:], s.dtype) for s in ss],
        in_specs=[anyspec] * n, out_specs=[anyspec] * n,
        scratch_shapes=[pltpu.SemaphoreType.DMA((n, 3)), pltpu.SemaphoreType.DMA((n, 3))],
    )(*ss)


def share_halves(bufs, *, name):
    n = len(bufs)

    def body(*refs):
        outs = refs[n:2 * n]
        send_sems, recv_sems = refs[2 * n:]
        x, y, c = _place()
        sib = (x, y, 1 - c)
        cps = []
        for i in range(n):
            hr = outs[i].shape[0] // 2
            mine = outs[i].at[pl.ds(c * hr, hr)]
            cp = pltpu.make_async_remote_copy(src_ref=mine, dst_ref=mine, send_sem=send_sems.at[i],
                                              recv_sem=recv_sems.at[i], device_id=sib, device_id_type=MESH)
            cp.start()
            cps.append(cp)
        for i in range(n):
            hr = outs[i].shape[0] // 2
            theirs = outs[i].at[pl.ds((1 - c) * hr, hr)]
            pltpu.make_async_remote_copy(src_ref=theirs, dst_ref=theirs, send_sem=send_sems.at[i],
                                         recv_sem=recv_sems.at[i], device_id=sib, device_id_type=MESH).wait_recv()
        for cp in cps:
            cp.wait_send()

    anyspec = pl.BlockSpec(memory_space=pl.ANY)
    return pl.pallas_call(
        body, name=name,
        out_shape=[jax.ShapeDtypeStruct(b.shape, b.dtype) for b in bufs],
        in_specs=[anyspec] * n, out_specs=[anyspec] * n,
        input_output_aliases={i: i for i in range(n)},
        scratch_shapes=[pltpu.SemaphoreType.DMA((n,)), pltpu.SemaphoreType.DMA((n,))],
    )(*bufs)


def _row_tile(r, cols, itemsize, target_bytes=2 * 1024 * 1024):
    best = None
    for t in range(16, r + 1, 16):
        if r % t == 0 and t * cols * itemsize <= target_bytes:
            best = t
    return best if best is not None else r


def pair_sum(g, recv, half, *, name):
    _, R, C = g.shape
    hr = R // 2
    tr = _row_tile(hr, C, 4)
    nb = hr // tr

    def body(half_ref, a_ref, b_ref, o_ref):
        o_ref[...] = (a_ref[...].astype(F32) + b_ref[...].astype(F32)).astype(BF)

    return pl.pallas_call(
        body, name=name,
        grid_spec=pltpu.PrefetchScalarGridSpec(
            num_scalar_prefetch=1, grid=(N_CHIPS, nb),
            in_specs=[pl.BlockSpec((None, tr, C), lambda k, r, hf: (k, hf[0] * nb + r, 0)),
                      pl.BlockSpec((None, tr, C), lambda k, r, hf: (k, r, 0))],
            out_specs=pl.BlockSpec((None, tr, C), lambda k, r, hf: (k, r, 0))),
        out_shape=jax.ShapeDtypeStruct((N_CHIPS, hr, C), BF),
        compiler_params=_cparams(("parallel", "parallel")),
    )(half, g, recv)


def chip_sum(s, recv, place, *, name):
    _, hr, C = s.shape
    tr = _row_tile(hr, C, 4)
    nb = hr // tr

    def body(place_ref, a_ref, b_ref, o_ref):
        acc = a_ref[...].astype(F32)
        for j in range(3):
            acc = acc + b_ref[j].astype(F32)
        o_ref[...] = acc

    return pl.pallas_call(
        body, name=name,
        grid_spec=pltpu.PrefetchScalarGridSpec(
            num_scalar_prefetch=1, grid=(nb,),
            in_specs=[pl.BlockSpec((None, tr, C), lambda r, pc: (pc[0], r, 0)),
                      pl.BlockSpec((3, tr, C), lambda r, pc: (0, r, 0))],
            out_specs=pl.BlockSpec((tr, C), lambda r, pc: (pc[1] * nb + r, 0))),
        out_shape=jax.ShapeDtypeStruct((2 * hr, C), F32),
        compiler_params=_cparams(("parallel",)),
    )(place, s, recv)


def sum_devices(v, *, name):
    _, R, C = v.shape

    def body(v_ref, o_ref):
        acc = v_ref[0]
        for k in range(1, N_DEV):
            acc = acc + v_ref[k]
        o_ref[...] = acc

    return pl.pallas_call(body, name=name, out_shape=jax.ShapeDtypeStruct((R, C), F32))(v)


def adamw(w, g, m, v, *, name):
    R, C = w.shape
    tr = _row_tile(R, C, 4, target_bytes=1024 * 1024)
    c1 = 1.0 / (1.0 - ADAM_B1 ** ADAM_STEP)
    c2 = 1.0 / (1.0 - ADAM_B2 ** ADAM_STEP)

    def body(w_ref, g_ref, m_ref, v_ref, d_ref, nm_ref, nv_ref):
        gg = g_ref[...]
        nm = ADAM_B1 * m_ref[...] + (1.0 - ADAM_B1) * gg
        nv = ADAM_B2 * v_ref[...] + (1.0 - ADAM_B2) * (gg * gg)
        nm_ref[...] = nm
        nv_ref[...] = nv
        d_ref[...] = -ADAM_LR * ((nm * c1) / (jnp.sqrt(nv * c2) + ADAM_EPS) + ADAM_WD * w_ref[...])

    spec = pl.BlockSpec((tr, C), lambda i: (i, 0))
    return pl.pallas_call(
        body, name=name, grid=(R // tr,), in_specs=[spec] * 4, out_specs=[spec] * 3,
        out_shape=[jax.ShapeDtypeStruct((R, C), F32)] * 3, compiler_params=_cparams(("parallel",)),
    )(w, g, m, v)


def _rope_tables(T):
    pos = np.arange(T, dtype=np.float32)
    inv_freq = (ROPE_THETA ** (-np.arange(0, QK_ROPE, 2, dtype=np.float32) / QK_ROPE)).astype(np.float32)
    ang = pos[:, None] * inv_freq[None, :]
    z = np.zeros((T, 64), np.float32)
    cos = np.concatenate([np.cos(ang), np.cos(ang), z], axis=1).astype(np.float32)
    sin = np.concatenate([np.sin(ang), np.sin(ang), z], axis=1).astype(np.float32)
    return jnp.asarray(cos), jnp.asarray(sin)


def _pick(n, cands):
    for c in cands:
        if n % c == 0:
            return c
    return n


def kernel(x, meta_tokens, g_mix, w_in, b_glu, b_gate, w_dw, b_dw, g_conv_ln, b_conv_ln, w_conv_out, b_conv_out, g_q_lora, w_uq, g_kv_lora, w_uk, w_uv, w_attn_out, w_out, g_ffn, w_ffn_gate, w_ffn_up, w_ffn_down, g_final, loss_target, m_meta_tokens, m_g_mix, m_w_in, m_b_glu, m_b_gate, m_w_dw, m_b_dw, m_g_conv_ln, m_b_conv_ln, m_w_conv_out, m_b_conv_out, m_g_q_lora, m_w_uq, m_g_kv_lora, m_w_uk, m_w_uv, m_w_attn_out, m_w_out, m_g_ffn, m_w_ffn_gate, m_w_ffn_up, m_w_ffn_down, m_g_final, v_meta_tokens, v_g_mix, v_w_in, v_b_glu, v_b_gate, v_w_dw, v_b_dw, v_g_conv_ln, v_b_conv_ln, v_w_conv_out, v_b_conv_out, v_g_q_lora, v_w_uq, v_g_kv_lora, v_w_uk, v_w_uv, v_w_attn_out, v_w_out, v_g_ffn, v_w_ffn_gate, v_w_ffn_up, v_w_ffn_down, v_g_final):
    weights = dict(meta_tokens=meta_tokens, g_mix=g_mix, w_in=w_in, b_glu=b_glu, b_gate=b_gate, w_dw=w_dw, b_dw=b_dw,
                   g_conv_ln=g_conv_ln, b_conv_ln=b_conv_ln, w_conv_out=w_conv_out, b_conv_out=b_conv_out,
                   g_q_lora=g_q_lora, w_uq=w_uq, g_kv_lora=g_kv_lora, w_uk=w_uk, w_uv=w_uv, w_attn_out=w_attn_out,
                   w_out=w_out, g_ffn=g_ffn, w_ffn_gate=w_ffn_gate, w_ffn_up=w_ffn_up, w_ffn_down=w_ffn_down,
                   g_final=g_final)
    m_in = dict(meta_tokens=m_meta_tokens, g_mix=m_g_mix, w_in=m_w_in, b_glu=m_b_glu, b_gate=m_b_gate, w_dw=m_w_dw,
                b_dw=m_b_dw, g_conv_ln=m_g_conv_ln, b_conv_ln=m_b_conv_ln, w_conv_out=m_w_conv_out,
                b_conv_out=m_b_conv_out, g_q_lora=m_g_q_lora, w_uq=m_w_uq, g_kv_lora=m_g_kv_lora, w_uk=m_w_uk,
                w_uv=m_w_uv, w_attn_out=m_w_attn_out, w_out=m_w_out, g_ffn=m_g_ffn, w_ffn_gate=m_w_ffn_gate,
                w_ffn_up=m_w_ffn_up, w_ffn_down=m_w_ffn_down, g_final=m_g_final)
    v_in = dict(meta_tokens=v_meta_tokens, g_mix=v_g_mix, w_in=v_w_in, b_glu=v_b_glu, b_gate=v_b_gate, w_dw=v_w_dw,
                b_dw=v_b_dw, g_conv_ln=v_g_conv_ln, b_conv_ln=v_b_conv_ln, w_conv_out=v_w_conv_out,
                b_conv_out=v_b_conv_out, g_q_lora=v_g_q_lora, w_uq=v_w_uq, g_kv_lora=v_g_kv_lora, w_uk=v_w_uk,
                w_uv=v_w_uv, w_attn_out=v_w_attn_out, w_out=v_w_out, g_ffn=v_g_ffn, w_ffn_gate=v_w_ffn_gate,
                w_ffn_up=v_w_ffn_up, w_ffn_down=v_w_ffn_down, g_final=v_g_final)
    names = list(weights)

    seq, D = x.shape[1], x.shape[2]
    C = w_conv_out.shape[2]
    R = g_q_lora.shape[1]
    H = w_uv.shape[2] * N_CHIPS // V_HEAD
    FB = w_ffn_gate.shape[2]
    FF = FB * N_CHIPS
    length = N_META + seq
    T = -(-length // BLOCK_Q) * BLOCK_Q
    scale = (QK_NOPE + QK_ROPE) ** -0.5
    assert C == D and w_in.shape[2] * N_CHIPS == 2 * C + 2 * R + QK_ROPE + 2 * D

    xi, yi, ci = _place()
    chip = (2 * xi + yi).astype(jnp.int32)
    half_arr = ci.astype(jnp.int32).reshape(1)

    tr = _pick(T, (272, 256, 128))
    trw = _pick(T, (136, 128))
    tmh = T // 2 if (T // 2) % 16 == 0 else T
    tmw = _pick(D, (1024, 512))
    nch = 4
    rc = _pick(T, (136, 128))
    tcv = 256

    def tn_of(n, pref=512):
        return _pick(n, (pref, 384, 256, 128))

    meta_all = gather_devices(meta_tokens, name="gather_meta")
    meta_full = jnp.concatenate([meta_all[2 * k] for k in range(N_CHIPS)], axis=1)

    grp_a = ["w_in"]
    grp_b = ["w_conv_out", "w_uq", "w_uk", "w_uv", "w_attn_out", "w_out"]
    grp_c = ["w_ffn_gate", "w_ffn_up", "w_ffn_down"]
    big = grp_a + grp_b + grp_c

    def own_block(nm):
        w = weights[nm][0].astype(BF)
        return lax.dynamic_update_slice(lax.empty((N_CHIPS,) + w.shape, BF), w[None], (chip, 0, 0))

    G = dict(zip(grp_a, gather_chips([own_block(nm) for nm in grp_a], name="gather_a")))
    bufs_b, ss_b, rs_b, tok_b = gather_start([own_block(nm) for nm in grp_b], G["w_in"], name="gather_b_start")
    bufs_c, ss_c, rs_c, tok_c = gather_start([own_block(nm) for nm in grp_c], tok_b, name="gather_c_start")

    def cols_full(a):
        return jnp.concatenate([a[k] for k in range(N_CHIPS)], axis=1)

    W_in = cols_full(G["w_in"])
    o1, o2 = 2 * C, 2 * C + 2 * R + QK_ROPE
    W_glu = W_in[:, :o1]
    W_small = jnp.pad(W_in[:, o1:o2], ((0, 0), (0, 128 - QK_ROPE)))
    W_gate = W_in[:, o2:]

    cos, sin = _rope_tables(T)

    h0 = jnp.concatenate([meta_full, x[0], jnp.zeros((T - length, D), F32)], axis=0)
    tgt = jnp.pad(loss_target[0], ((N_META, T - length), (0, 0)))

    u = rms_fwd(h0, g_mix, tr=tr, name="rms_mix", deps=[tok_b, tok_c])
    z_glu = mm(u, W_glu, mode="nn", tm=T, tn=tn_of(2 * C), tk=D, out_dtype=F32, name="mm_z_glu")
    z_small = mm(u, W_small, mode="nn", tm=T, tn=tn_of(2 * R + 128, 384), tk=D, out_dtype=F32, name="mm_z_small")
    z_gate = mm(u, W_gate, mode="nn", tm=T, tn=tn_of(2 * D), tk=D, out_dtype=F32, name="mm_z_gate")

    c0 = glu_fwd(z_glu, b_glu, tr=tr, name="glu_fwd")
    dw_all = gather_devices(jnp.pad(w_dw[0], ((0, 1), (0, 0))), name="gather_w_dw")
    w_dw_full = jnp.concatenate([dw_all[2 * k] for k in range(N_CHIPS)], axis=1)
    c1 = dwconv_fwd(c0, w_dw_full, b_dw, tc=tcv, rc=rc, name="dwconv_fwd")
    c3 = ln_silu_fwd(c1, g_conv_ln, b_conv_ln, tr=tr, name="ln_silu_fwd")

    bufs_b = gather_wait(bufs_b, ss_b, rs_b, c1, name="gather_b_wait")
    G.update(zip(grp_b, forward_pair(bufs_b, name="gather_b_forward")))
    W_co = G["w_conv_out"].reshape(C, D)
    W_uq = cols_full(G["w_uq"]).reshape(R, H, QK_NOPE + QK_ROPE)
    Wq_p = jnp.pad(W_uq, ((0, 0), (0, 0), (0, HEAD_PAD - QK_NOPE - QK_ROPE))).reshape(R, H * HEAD_PAD)
    W_uk = cols_full(G["w_uk"])
    W_uv = cols_full(G["w_uv"])
    W_ao = G["w_attn_out"].reshape(H * V_HEAD, D)
    W_o = G["w_out"].reshape(D, D)

    y_conv = mm(c3, W_co, mode="nn", tm=T, tn=tn_of(D), tk=C, out_dtype=F32, name="mm_y_conv")

    cq, ckv, kr = lora_norm_fwd(z_small, g_q_lora, g_kv_lora, cos, sin, tr=tr, name="lora_norm_fwd")
    q = q_proj(cq, Wq_p, cos, sin, scale=scale, name="q_proj")
    kk = k_proj(ckv, W_uk, kr, name="k_proj")
    vv = mm(ckv, W_uv, mode="nn", tm=T, tn=tn_of(H * V_HEAD), tk=R, out_dtype=BF, name="mm_v")
    o_attn, lse = attn_fwd(q, kk, vv, nch=nch, name="attn_fwd")
    y_attn = mm(o_attn, W_ao, mode="nn", tm=T, tn=tn_of(D), tk=H * V_HEAD, out_dtype=F32, name="mm_y_attn")

    mix = mix_fwd(z_gate, y_conv, y_attn, b_gate, b_conv_out, tr=tr, name="mix_fwd")
    h1 = mm(mix, W_o, mode="nn", tm=T, tn=tn_of(D), tk=D, out_dtype=F32, name="mm_h1", res=h0)

    hn = rms_fwd(h1, g_ffn, tr=tr, name="rms_ffn")

    bufs_c = gather_wait(bufs_c, ss_c, rs_c, hn, name="gather_c_wait")
    G.update(zip(grp_c, forward_pair(bufs_c, name="gather_c_forward")))
    W_fg, W_fu = G["w_ffn_gate"], G["w_ffn_up"]
    W_fd = G["w_ffn_down"].reshape(FF, D)

    fa = mm(hn, W_fg, mode="nn", tm=tmh, tn=FB, tk=D, out_dtype=F32, name="mm_ffn_gate")
    fb = mm(hn, W_fu, mode="nn", tm=tmh, tn=FB, tk=D, out_dtype=F32, name="mm_ffn_up")
    f = swiglu_fwd(fa, fb, tr=trw, name="swiglu_fwd")
    h2 = mm(f, W_fd, mode="nn", tm=tmh, tn=tn_of(D, 256), tk=FF, out_dtype=F32, name="mm_h2", res=h1)

    place_arr = jnp.stack([chip, ci.astype(jnp.int32)])

    def col_blocks(a):
        r, cfull = a.shape
        return a.reshape(r, N_CHIPS, cfull // N_CHIPS).transpose(1, 0, 2)

    def rs_begin(nms, gls, tag):
        recv = exchange_pair(gls, name=f"rs_{tag}_pair_exchange")
        pair = [pair_sum(g, r, half_arr, name="rs_pair_sum_" + nm) for nm, g, r in zip(nms, gls, recv)]
        pair, lands, ssem, rsem, tok = scatter_start(pair, name=f"rs_{tag}_scatter_start")
        return (nms, pair, lands, ssem, rsem), tok

    def rs_end(state, after, tag):
        nms, pair, lands, ssem, rsem = state
        pair, lands = scatter_wait(pair, lands, ssem, rsem, after, name=f"rs_{tag}_scatter_wait")
        return [chip_sum(s, r, place_arr, name="rs_chip_sum_" + nm) for nm, s, r in zip(nms, pair, lands)]

    d_h2, d_h2b, loss_p, d_g_final = final_loss(h2, tgt, g_final.reshape(1, D), seq=seq, tr=tr, name="final_loss")
    loss = lax.psum(loss_p[0, 0], ("x", "y", "c"))

    d_f = mm(d_h2b, W_fd, mode="nt", tm=tmh, tn=FB, tk=D, out_dtype=F32, name="mm_d_f")
    dW_fd = mm(f, d_h2b, mode="tn", tm=FB, tn=tn_of(D), tk=T, out_dtype=BF, name="mm_dw_ffn_down")
    d_a, d_b = swiglu_bwd(fa, fb, d_f, tr=trw, name="swiglu_bwd")
    dW_fg = mm(hn, d_a, mode="tn", tm=tmw, tn=FB, tk=T, out_dtype=BF, name="mm_dw_ffn_gate", out_blocks=N_CHIPS)
    dW_fu = mm(hn, d_b, mode="tn", tm=tmw, tn=FB, tk=T, out_dtype=BF, name="mm_dw_ffn_up", out_blocks=N_CHIPS)
    st_c, tok_rc = rs_begin(grp_c, [dW_fg, dW_fu, dW_fd.reshape(N_CHIPS, FB, D)], "c")
    d_hn = mm(d_a, W_fg, mode="nt", tm=tmh, tn=tn_of(D), tk=FB, out_dtype=F32, name="mm_d_hn_gate", deps=[tok_rc])
    d_hn = mm(d_b, W_fu, mode="nt", tm=tmh, tn=tn_of(D), tk=FB, out_dtype=F32, name="mm_d_hn_up", res=d_hn)
    d_h1, d_h1b, d_g_ffn = rms_bwd(d_hn, h1, g_ffn, d_h2, tr=tr, name="rms_ffn_bwd")

    d_mix = mm(d_h1b, W_o, mode="nt", tm=T, tn=tn_of(D), tk=D, out_dtype=F32, name="mm_d_mix")
    dW_o = mm(mix, d_h1b, mode="tn", tm=tmw, tn=tn_of(D), tk=T, out_dtype=BF, name="mm_dw_out")
    d_yc, d_ya, d_zgate, d_b_gate, d_b_co = mix_bwd(z_gate, y_conv, y_attn, d_mix, b_gate, b_conv_out, tr=tr,
                                                    name="mix_bwd")

    dW_co = mm(c3, d_yc, mode="tn", tm=tmw, tn=tn_of(D), tk=T, out_dtype=BF, name="mm_dw_conv_out")
    d_c3 = mm(d_yc, W_co, mode="nt", tm=T, tn=tn_of(C), tk=D, out_dtype=F32, name="mm_d_c3")
    d_c1, d_g_ln, d_b_ln, d_b_dw = ln_silu_bwd(c1, d_c3, g_conv_ln, b_conv_ln, tr=tr, name="ln_silu_bwd")
    d_c0, d_w_dw = dwconv_bwd(c0, d_c1, w_dw_full, tc=tcv, rc=rc, name="dwconv_bwd")
    d_zglu, d_b_glu = glu_bwd(z_glu, b_glu, d_c0, tr=tr, name="glu_bwd")

    dW_ao = mm(o_attn, d_ya, mode="tn", tm=_pick(H * V_HEAD, (1024, 512)), tn=tn_of(D), tk=T, out_dtype=BF, name="mm_dw_attn_out")
    d_o = mm(d_ya, W_ao, mode="nt", tm=T, tn=tn_of(H * V_HEAD), tk=D, out_dtype=BF, name="mm_d_o")
    d_q2, d_k2, d_v, d_kr = attn_bwd(q, kk, vv, o_attn, d_o, lse, cos, sin, scale=scale, nch=nch, name="attn_bwd")

    dWq_p = mm(cq, d_q2, mode="tn", tm=R, tn=tn_of(H * HEAD_PAD), tk=T, out_dtype=BF, name="mm_dw_uq")
    d_cq = mm(d_q2, Wq_p, mode="nt", tm=T, tn=R, tk=_pick(H * HEAD_PAD, (1024,)), out_dtype=F32, name="mm_d_cq")
    Wk_p = jnp.pad(W_uk.reshape(R, H, QK_NOPE), ((0, 0), (0, 0), (0, HEAD_PAD - QK_NOPE))).reshape(R, H * HEAD_PAD)
    dWk_p = mm(ckv, d_k2, mode="tn", tm=R, tn=tn_of(H * HEAD_PAD), tk=T, out_dtype=BF, name="mm_dw_uk")
    dW_uv = mm(ckv, d_v, mode="tn", tm=R, tn=tn_of(H * V_HEAD), tk=T, out_dtype=BF, name="mm_dw_uv")
    red_c = rs_end(st_c, d_v, "c")
    dW_uq = dWq_p.reshape(R, H, HEAD_PAD)[:, :, :QK_NOPE + QK_ROPE].reshape(R, H * (QK_NOPE + QK_ROPE))
    dW_uk = dWk_p.reshape(R, H, HEAD_PAD)[:, :, :QK_NOPE].reshape(R, H * QK_NOPE)
    st_b, tok_rb = rs_begin(grp_b, [dW_co.reshape(N_CHIPS, C // N_CHIPS, D), col_blocks(dW_uq), col_blocks(dW_uk),
                                    col_blocks(dW_uv), dW_ao.reshape(N_CHIPS, H * V_HEAD // N_CHIPS, D),
                                    dW_o.reshape(N_CHIPS, D // N_CHIPS, D)], "b")
    d_ckv = mm(d_k2, Wk_p, mode="nt", tm=T, tn=R, tk=_pick(H * HEAD_PAD, (1024,)), out_dtype=F32, name="mm_d_ckv_k",
               deps=[tok_rb])
    d_ckv = mm(d_v, W_uv, mode="nt", tm=T, tn=R, tk=_pick(H * V_HEAD, (1024,)), out_dtype=F32, name="mm_d_ckv_v",
               res=d_ckv)
    d_zsmall, d_g_q, d_g_kv = lora_norm_bwd(z_small, d_cq, d_ckv, d_kr, cos, sin, g_q_lora, g_kv_lora, tr=tr,
                                            name="lora_norm_bwd")

    dW_glu = mm(u, d_zglu, mode="tn", tm=tmw, tn=tn_of(2 * C), tk=T, out_dtype=BF, name="mm_dw_glu")
    dW_small = mm(u, d_zsmall, mode="tn", tm=tmw, tn=tn_of(2 * R + 128, 384), tk=T, out_dtype=BF, name="mm_dw_small")
    dW_gate = mm(u, d_zgate, mode="tn", tm=tmw, tn=tn_of(2 * D), tk=T, out_dtype=BF, name="mm_dw_gate")
    red_b = rs_end(st_b, dW_gate, "b")
    dW_in = jnp.concatenate([dW_glu, dW_small[:, :2 * R + QK_ROPE], dW_gate], axis=1)
    st_a, tok_ra = rs_begin(grp_a, [col_blocks(dW_in)], "a")
    d_u = mm(d_zglu, W_glu, mode="nt", tm=T, tn=tn_of(D), tk=_pick(2 * C, (2048, 1024)), out_dtype=F32, name="mm_d_u_glu",
             deps=[tok_ra])
    d_u = mm(d_zsmall, W_small, mode="nt", tm=T, tn=tn_of(D), tk=2 * R + 128, out_dtype=F32, name="mm_d_u_small",
             res=d_u)
    d_u = mm(d_zgate, W_gate, mode="nt", tm=T, tn=tn_of(D), tk=_pick(2 * D, (2048, 1024)), out_dtype=F32,
             name="mm_d_u_gate", res=d_u)
    d_h0, _, d_g_mix = rms_bwd(d_u, h0, g_mix, d_h1, tr=tr, name="rms_mix_bwd")

    grad_x = d_h0[N_META:length][None]

    red_a = rs_end(st_a, d_h0, "a")
    reduced = dict(zip(big, share_halves(red_a + red_b + red_c, name="rs_share_halves")))

    small = ["g_mix", "b_glu", "b_gate", "b_dw", "g_conv_ln", "b_conv_ln", "b_conv_out", "g_q_lora", "g_kv_lora",
             "g_ffn", "g_final", "w_dw", "meta_tokens"]
    sgrads = dict(g_mix=d_g_mix, b_glu=d_b_glu, b_gate=d_b_gate, b_dw=d_b_dw, g_conv_ln=d_g_ln, b_conv_ln=d_b_ln,
                  b_conv_out=d_b_co, g_q_lora=d_g_q, g_kv_lora=d_g_kv, g_ffn=d_g_ffn, g_final=d_g_final,
                  w_dw=d_w_dw[:CONV_WIDTH], meta_tokens=d_h0[:N_META])
    sizes = [int(np.prod(sgrads[nm].shape)) for nm in small]
    packed = jnp.concatenate([sgrads[nm].reshape(-1) for nm in small]).reshape(-1, 128)
    summed = sum_devices(gather_devices(packed, name="gather_small_grads"), name="sum_small_grads").reshape(-1)
    offs = np.concatenate([[0], np.cumsum(sizes)])
    sfull = {nm: summed[int(offs[i]):int(offs[i + 1])].reshape(sgrads[nm].shape) for i, nm in enumerate(small)}

    grads = {}
    for nm in big:
        grads[nm] = reduced[nm][None]
    for nm in small:
        gfull = sfull[nm]
        if nm == "w_dw":
            cb = C // N_CHIPS
            grads[nm] = lax.dynamic_slice(gfull, (0, chip * cb), (CONV_WIDTH, cb))[None]
        elif nm == "meta_tokens":
            cb = D // N_CHIPS
            grads[nm] = lax.dynamic_slice(gfull, (0, chip * cb), (N_META, cb))
        else:
            grads[nm] = gfull.reshape(weights[nm].shape)

    delta, new_m, new_v = {}, {}, {}
    rep = [nm for nm in small if nm not in ("w_dw", "meta_tokens")]

    def pack(d):
        return jnp.concatenate([d[nm].reshape(-1) for nm in rep]).reshape(-1, 128)

    pd, pm, pv = adamw(pack(weights), pack(grads), pack(m_in), pack(v_in), name="adamw_small")
    rsz = [int(np.prod(weights[nm].shape)) for nm in rep]
    roff = np.concatenate([[0], np.cumsum(rsz)])
    for i, nm in enumerate(rep):
        sl = slice(int(roff[i]), int(roff[i + 1]))
        delta[nm] = pd.reshape(-1)[sl].reshape(weights[nm].shape)
        new_m[nm] = pm.reshape(-1)[sl].reshape(weights[nm].shape)
        new_v[nm] = pv.reshape(-1)[sl].reshape(weights[nm].shape)
    for nm in big + ["w_dw", "meta_tokens"]:
        shp = weights[nm].shape
        two = (shp[-2], shp[-1])
        d_, m_, v_ = adamw(weights[nm].reshape(two), grads[nm].reshape(two), m_in[nm].reshape(two),
                           v_in[nm].reshape(two), name="adamw_" + nm)
        delta[nm], new_m[nm], new_v[nm] = d_.reshape(shp), m_.reshape(shp), v_.reshape(shp)

    return (loss, grad_x, *[grads[nm] for nm in names], *[delta[nm] for nm in names],
            *[new_m[nm] for nm in names], *[new_v[nm] for nm in names])
```

```python
import functools

import numpy as np
import jax
import jax.numpy as jnp
from jax import lax
from jax.experimental import pallas as pl
from jax.experimental.pallas import tpu as pltpu

F32 = jnp.float32
BF = jnp.bfloat16
MESH = pl.DeviceIdType.MESH

N_META = 16
BLOCK_Q = 128
CONV_WIDTH = 31
CONV_PAD = 32
QK_NOPE = 128
QK_ROPE = 64
V_HEAD = 128
HEAD_PAD = 256
ROPE_THETA = 10000.0
EPS = 1e-6
ADAM_LR = 0.001
ADAM_B1 = 0.9
ADAM_B2 = 0.999
ADAM_EPS = 1e-08
ADAM_WD = 0.01
ADAM_STEP = 10
VMEM_LIMIT = 56 * 1024 * 1024
N_CHIPS = 4
N_DEV = 8


def _cparams(sem):
    return pltpu.CompilerParams(dimension_semantics=sem, vmem_limit_bytes=VMEM_LIMIT)


def mm(a, b, *, mode, tm, tn, tk, out_dtype, name, res=None, out_blocks=None, deps=()):
    b3 = b.ndim == 3
    if mode == "nn":
        M, K = a.shape
        N = b.shape[0] * b.shape[2] if b3 else b.shape[1]
        a_spec = pl.BlockSpec((tm, tk), lambda i, j, k: (i, k))
        if b3:
            per = b.shape[2] // tn
            b_spec = pl.BlockSpec((None, tk, tn), lambda i, j, k: (j // per, k, j % per))
        else:
            b_spec = pl.BlockSpec((tk, tn), lambda i, j, k: (k, j))
        dims = (((1,), (0,)), ((), ()))
    elif mode == "nt":
        M, K = a.shape
        N = b.shape[1] if b3 else b.shape[0]
        a_spec = pl.BlockSpec((tm, tk), lambda i, j, k: (i, k))
        if b3:
            per = b.shape[2] // tk
            b_spec = pl.BlockSpec((None, tn, tk), lambda i, j, k: (k // per, j, k % per))
        else:
            b_spec = pl.BlockSpec((tn, tk), lambda i, j, k: (j, k))
        dims = (((1,), (1,)), ((), ()))
    else:
        K, M = a.shape
        N = b.shape[1]
        a_spec = pl.BlockSpec((tk, tm), lambda i, j, k: (k, i))
        b_spec = pl.BlockSpec((tk, tn), lambda i, j, k: (k, j))
        dims = (((0,), (0,)), ((), ()))
    gm, gn, gk = M // tm, N // tn, K // tk
    assert gm * tm == M and gn * tn == N and gk * tk == K, (name, a.shape, b.shape, tm, tn, tk)
    if out_blocks is None:
        o_spec = pl.BlockSpec((tm, tn), lambda i, j, k: (i, j))
        o_shape = (M, N)
    else:
        nbw = N // out_blocks
        per_o = nbw // tn
        assert per_o * tn == nbw
        o_spec = pl.BlockSpec((None, tm, tn), lambda i, j, k: (j // per_o, i, j % per_o))
        o_shape = (out_blocks, M, nbw)
    has_res = res is not None

    def body(*refs):
        a_ref, b_ref = refs[0], refs[1]
        r_ref = refs[2] if has_res else None
        o_ref = refs[2 + has_res + len(deps)]
        acc = refs[-1]
        p = lax.dot_general(a_ref[...], b_ref[...], dims, preferred_element_type=F32)

        def finish(v):
            if has_res:
                v = v + r_ref[...]
            o_ref[...] = v.astype(o_ref.dtype)

        if gk == 1:
            finish(p)
        else:
            k = pl.program_id(2)

            @pl.when(k == 0)
            def _():
                acc[...] = p

            @pl.when(k > 0)
            def _():
                acc[...] += p

            @pl.when(k == gk - 1)
            def _():
                finish(acc[...])

    in_specs = [a_spec, b_spec]
    args = [a, b]
    if has_res:
        in_specs.append(pl.BlockSpec((tm, tn), lambda i, j, k: (i, j)))
        args.append(res)
    in_specs += [pl.BlockSpec(memory_space=pl.ANY)] * len(deps)
    args += list(deps)
    return pl.pallas_call(
        body, name=name, grid=(gm, gn, gk), in_specs=in_specs, out_specs=o_spec,
        out_shape=jax.ShapeDtypeStruct(o_shape, out_dtype),
        scratch_shapes=[pltpu.VMEM((tm, tn), F32)] if gk > 1 else [],
        compiler_params=_cparams(("parallel", "parallel", "arbitrary")),
    )(*args)


def rowwise(body, row_ins, full_ins, row_outs, acc_outs, *, tr, name, deps=()):
    T = row_ins[0].shape[0]
    assert T % tr == 0, (name, T, tr)
    n_ri, n_fi, n_ro = len(row_ins), len(full_ins), len(row_outs)
    n_in = n_ri + n_fi + len(deps)

    def kern(*refs):
        body(pl.program_id(0), refs[:n_ri], refs[n_ri:n_ri + n_fi], refs[n_in:n_in + n_ro], refs[n_in + n_ro:])

    in_specs = [pl.BlockSpec((tr, a.shape[1]), lambda i: (i, 0)) for a in row_ins]
    in_specs += [pl.BlockSpec(a.shape, lambda i: (0, 0)) for a in full_ins]
    in_specs += [pl.BlockSpec(memory_space=pl.ANY)] * len(deps)
    out_specs = [pl.BlockSpec((tr, c), lambda i: (i, 0)) for c, _ in row_outs]
    out_specs += [pl.BlockSpec(s, lambda i: (0, 0)) for s in acc_outs]
    out_shape = [jax.ShapeDtypeStruct((T, c), d) for c, d in row_outs]
    out_shape += [jax.ShapeDtypeStruct(s, F32) for s in acc_outs]
    return pl.pallas_call(
        kern, name=name, grid=(T // tr,), in_specs=in_specs, out_specs=out_specs, out_shape=out_shape,
        compiler_params=_cparams(("arbitrary",)),
    )(*row_ins, *full_ins, *deps)


def _acc(step, ref, val):
    @pl.when(step == 0)
    def _():
        ref[...] = val

    @pl.when(step > 0)
    def _():
        ref[...] += val


def _colsum(x):
    return jnp.sum(x, axis=0, keepdims=True)


def _sigmoid(x):
    return 1.0 / (1.0 + jnp.exp(-x))


def _rot_half(v):
    lane = lax.broadcasted_iota(jnp.int32, v.shape, 1)
    lo = -pltpu.roll(v, 96, 1)
    hi = pltpu.roll(v, 32, 1)
    return jnp.where(lane < 32, lo, jnp.where(lane < 64, hi, 0.0))


def rms_fwd(h, g, *, tr, name, deps=()):
    def body(step, ri, fi, ro, ao):
        x = ri[0][...]
        r = lax.rsqrt(jnp.mean(x * x, axis=-1, keepdims=True) + EPS)
        ro[0][...] = ((x * r) * fi[0][...]).astype(BF)

    return rowwise(body, [h], [g], [(h.shape[1], BF)], [], tr=tr, name=name, deps=deps)[0]


def rms_bwd(dy, h, g, dres, *, tr, name):
    D = h.shape[1]

    def body(step, ri, fi, ro, ao):
        d, x, dr = ri[0][...], ri[1][...], ri[2][...]
        r = lax.rsqrt(jnp.mean(x * x, axis=-1, keepdims=True) + EPS)
        n = x * r
        _acc(step, ao[0], _colsum(d * n))
        dn = d * fi[0][...]
        dh = r * (dn - n * jnp.mean(dn * n, axis=-1, keepdims=True)) + dr
        ro[0][...] = dh
        ro[1][...] = dh.astype(BF)

    return rowwise(body, [dy, h, dres], [g], [(D, F32), (D, BF)], [(1, D)], tr=tr, name=name)


def glu_fwd(z, b, *, tr, name):
    C = z.shape[1] // 2

    def body(step, ri, fi, ro, ao):
        zz = ri[0][...] + fi[0][...]
        ro[0][...] = zz[:, :C] * _sigmoid(zz[:, C:])

    return rowwise(body, [z], [b], [(C, F32)], [], tr=tr, name=name)[0]


def glu_bwd(z, b, dc, *, tr, name):
    C = z.shape[1] // 2

    def body(step, ri, fi, ro, ao):
        zz = ri[0][...] + fi[0][...]
        d = ri[1][...]
        za, sg = zz[:, :C], _sigmoid(zz[:, C:])
        dza = d * sg
        dzb = d * za * sg * (1.0 - sg)
        ro[0][:, :C] = dza.astype(BF)
        ro[0][:, C:] = dzb.astype(BF)
        _acc(step, ao[0], _colsum(dza))
        _acc(step, ao[1], _colsum(dzb))

    dz, da, db = rowwise(body, [z, dc], [b], [(2 * C, BF)], [(1, C), (1, C)], tr=tr, name=name)
    return dz, jnp.concatenate([da, db], axis=1)


def ln_silu_fwd(c1, g, b, *, tr, name):
    def body(step, ri, fi, ro, ao):
        x = ri[0][...]
        mu = jnp.mean(x, axis=-1, keepdims=True)
        xc = x - mu
        rstd = lax.rsqrt(jnp.mean(xc * xc, axis=-1, keepdims=True) + EPS)
        y = (xc * rstd) * fi[0][...] + fi[1][...]
        ro[0][...] = (y * _sigmoid(y)).astype(BF)

    return rowwise(body, [c1], [g, b], [(c1.shape[1], BF)], [], tr=tr, name=name)[0]


def ln_silu_bwd(c1, dc3, g, b, *, tr, name):
    C = c1.shape[1]

    def body(step, ri, fi, ro, ao):
        x, d3 = ri[0][...], ri[1][...]
        mu = jnp.mean(x, axis=-1, keepdims=True)
        xc = x - mu
        rstd = lax.rsqrt(jnp.mean(xc * xc, axis=-1, keepdims=True) + EPS)
        n = xc * rstd
        y = n * fi[0][...] + fi[1][...]
        sg = _sigmoid(y)
        dy = d3 * (sg * (1.0 + y * (1.0 - sg)))
        _acc(step, ao[0], _colsum(dy * n))
        _acc(step, ao[1], _colsum(dy))
        dn = dy * fi[0][...]
        dx = rstd * (dn - jnp.mean(dn, axis=-1, keepdims=True) - n * jnp.mean(dn * n, axis=-1, keepdims=True))
        ro[0][...] = dx
        _acc(step, ao[2], _colsum(dx))

    return rowwise(body, [c1, dc3], [g, b], [(C, F32)], [(1, C), (1, C), (1, C)], tr=tr, name=name)


def lora_norm_fwd(zs, gq, gkv, cos, sin, *, tr, name):
    R = gq.shape[1]

    def body(step, ri, fi, ro, ao):
        z = ri[0][...]
        for o, gi in ((0, 0), (1, 1)):
            x = z[:, o * R:(o + 1) * R]
            r = lax.rsqrt(jnp.mean(x * x, axis=-1, keepdims=True) + EPS)
            ro[o][...] = ((x * r) * fi[gi][...]).astype(BF)
        kr = z[:, 2 * R:2 * R + 128]
        ro[2][...] = kr * ri[1][...] + _rot_half(kr) * ri[2][...]

    return rowwise(body, [zs, cos, sin], [gq, gkv], [(R, BF), (R, BF), (128, F32)], [], tr=tr, name=name)


def lora_norm_bwd(zs, dcq, dckv, dkr, cos, sin, gq, gkv, *, tr, name):
    R = gq.shape[1]

    def body(step, ri, fi, ro, ao):
        z = ri[0][...]
        for o in (0, 1):
            x = z[:, o * R:(o + 1) * R]
            d = ri[1 + o][...]
            r = lax.rsqrt(jnp.mean(x * x, axis=-1, keepdims=True) + EPS)
            n = x * r
            _acc(step, ao[o], _colsum(d * n))
            dn = d * fi[o][...]
            ro[0][:, o * R:(o + 1) * R] = (r * (dn - n * jnp.mean(dn * n, axis=-1, keepdims=True))).astype(BF)
        dk = ri[3][...]
        ro[0][:, 2 * R:2 * R + 128] = (dk * ri[4][...] - _rot_half(dk * ri[5][...])).astype(BF)

    return rowwise(body, [zs, dcq, dckv, dkr, cos, sin], [gq, gkv], [(2 * R + 128, BF)], [(1, R), (1, R)],
                   tr=tr, name=name)


def mix_fwd(zg, yc, ya, bg, bco, *, tr, name):
    D = yc.shape[1]

    def body(step, ri, fi, ro, ao):
        g = _sigmoid(ri[0][...] + fi[0][...])
        ro[0][...] = (g[:, :D] * (ri[1][...] + fi[1][...]) + g[:, D:] * ri[2][...]).astype(BF)

    return rowwise(body, [zg, yc, ya], [bg, bco], [(D, BF)], [], tr=tr, name=name)[0]


def mix_bwd(zg, yc, ya, dmix, bg, bco, *, tr, name):
    D = yc.shape[1]

    def body(step, ri, fi, ro, ao):
        g = _sigmoid(ri[0][...] + fi[0][...])
        gc, ga = g[:, :D], g[:, D:]
        ycv = ri[1][...] + fi[1][...]
        yav = ri[2][...]
        dm = ri[3][...]
        dyc = dm * gc
        ro[0][...] = dyc.astype(BF)
        ro[1][...] = (dm * ga).astype(BF)
        dzc = dm * ycv * gc * (1.0 - gc)
        dza = dm * yav * ga * (1.0 - ga)
        ro[2][:, :D] = dzc.astype(BF)
        ro[2][:, D:] = dza.astype(BF)
        _acc(step, ao[0], _colsum(dzc))
        _acc(step, ao[1], _colsum(dza))
        _acc(step, ao[2], _colsum(dyc))

    dyc, dya, dzg, dbc, dba, dbo = rowwise(body, [zg, yc, ya, dmix], [bg, bco], [(D, BF), (D, BF), (2 * D, BF)],
                                           [(1, D), (1, D), (1, D)], tr=tr, name=name)
    return dyc, dya, dzg, jnp.concatenate([dbc, dba], axis=1), dbo


def swiglu_fwd(a, b, *, tr, name):
    def body(step, ri, fi, ro, ao):
        x = ri[0][...]
        ro[0][...] = (x * _sigmoid(x) * ri[1][...]).astype(BF)

    return rowwise(body, [a, b], [], [(a.shape[1], BF)], [], tr=tr, name=name)[0]


def swiglu_bwd(a, b, df, *, tr, name):
    F = a.shape[1]

    def body(step, ri, fi, ro, ao):
        x, u, d = ri[0][...], ri[1][...], ri[2][...]
        sg = _sigmoid(x)
        ro[0][...] = (d * u * (sg * (1.0 + x * (1.0 - sg)))).astype(BF)
        ro[1][...] = (d * (x * sg)).astype(BF)

    return rowwise(body, [a, b, df], [], [(F, BF), (F, BF)], [], tr=tr, name=name)


def final_loss(h2, tgt, g, *, seq, tr, name):
    D = h2.shape[1]

    def body(step, ri, fi, ro, ao):
        x = ri[0][...]
        r = lax.rsqrt(jnp.mean(x * x, axis=-1, keepdims=True) + EPS)
        n = x * r
        gg = fi[0][...]
        row = lax.broadcasted_iota(jnp.int32, (tr, 1), 0) + step * tr
        live = jnp.logical_and(row >= N_META, row < N_META + seq)
        e = jnp.where(live, n * gg - ri[1][...], 0.0)
        _acc(step, ao[0], jnp.broadcast_to(0.5 * jnp.sum(jnp.mean(e * e, axis=-1, keepdims=True)), (1, 128)))
        dy = e * (1.0 / D)
        _acc(step, ao[1], _colsum(dy * n))
        dn = dy * gg
        dh = r * (dn - n * jnp.mean(dn * n, axis=-1, keepdims=True))
        ro[0][...] = dh
        ro[1][...] = dh.astype(BF)

    return rowwise(body, [h2, tgt], [g], [(D, F32), (D, BF)], [(1, 128), (1, D)], tr=tr, name=name)


def dwconv_fwd(x, w, b, *, tc, rc, name):
    T, C = x.shape
    nchunk = T // rc
    assert nchunk * rc == T and C % tc == 0

    def body(x_ref, w_ref, b_ref, y_ref, xp_ref):
        xp_ref[0:CONV_PAD, :] = jnp.zeros((CONV_PAD, tc), F32)
        xp_ref[CONV_PAD:CONV_PAD + T, :] = x_ref[...]
        off = CONV_PAD - (CONV_WIDTH - 1)
        for ci in range(nchunk):
            t0 = ci * rc
            acc = jnp.broadcast_to(b_ref[...], (rc, tc))
            for j in range(CONV_WIDTH):
                acc = acc + w_ref[j:j + 1, :] * xp_ref[t0 + off + j:t0 + off + j + rc, :]
            y_ref[t0:t0 + rc, :] = acc

    return pl.pallas_call(
        body, name=name, grid=(C // tc,),
        in_specs=[pl.BlockSpec((T, tc), lambda j: (0, j)), pl.BlockSpec((32, tc), lambda j: (0, j)),
                  pl.BlockSpec((1, tc), lambda j: (0, j))],
        out_specs=pl.BlockSpec((T, tc), lambda j: (0, j)),
        out_shape=jax.ShapeDtypeStruct((T, C), F32),
        scratch_shapes=[pltpu.VMEM((T + CONV_PAD, tc), F32)],
        compiler_params=_cparams(("parallel",)),
    )(x, w, b)


def dwconv_bwd(x, dy, w, *, tc, rc, name):
    T, C = x.shape
    nchunk = T // rc
    assert nchunk * rc == T and C % tc == 0 and rc % 8 == 0

    def body(x_ref, dy_ref, w_ref, dx_ref, dw_ref, xp_ref, dp_ref):
        off = CONV_PAD - (CONV_WIDTH - 1)
        xp_ref[0:CONV_PAD, :] = jnp.zeros((CONV_PAD, tc), F32)
        xp_ref[CONV_PAD:CONV_PAD + T, :] = x_ref[...]
        dp_ref[0:T, :] = dy_ref[...]
        dp_ref[T:T + CONV_PAD, :] = jnp.zeros((CONV_PAD, tc), F32)
        for ci in range(nchunk):
            t0 = ci * rc
            acc = jnp.zeros((rc, tc), F32)
            for j in range(CONV_WIDTH):
                s = t0 + (CONV_WIDTH - 1) - j
                acc = acc + w_ref[j:j + 1, :] * dp_ref[s:s + rc, :]
            dx_ref[t0:t0 + rc, :] = acc
        dw_ref[...] = jnp.zeros((32, tc), F32)
        for j in range(CONV_WIDTH):
            acc = jnp.zeros((8, tc), F32)
            for ci in range(nchunk):
                t0 = ci * rc
                pr = dp_ref[t0:t0 + rc, :] * xp_ref[t0 + off + j:t0 + off + j + rc, :]
                acc = acc + jnp.sum(pr.reshape(rc // 8, 8, tc), axis=0)
            dw_ref[j:j + 1, :] = jnp.sum(acc, axis=0, keepdims=True)

    return pl.pallas_call(
        body, name=name, grid=(C // tc,),
        in_specs=[pl.BlockSpec((T, tc), lambda j: (0, j)), pl.BlockSpec((T, tc), lambda j: (0, j)),
                  pl.BlockSpec((32, tc), lambda j: (0, j))],
        out_specs=[pl.BlockSpec((T, tc), lambda j: (0, j)), pl.BlockSpec((32, tc), lambda j: (0, j))],
        out_shape=[jax.ShapeDtypeStruct((T, C), F32), jax.ShapeDtypeStruct((32, C), F32)],
        scratch_shapes=[pltpu.VMEM((T + CONV_PAD, tc), F32), pltpu.VMEM((T + CONV_PAD, tc), F32)],
        compiler_params=_cparams(("parallel",)),
    )(x, dy, w)


def q_proj(cq, wq, cos, sin, *, scale, name):
    T, R = cq.shape
    H = wq.shape[1] // HEAD_PAD

    def body(a_ref, b_ref, c_ref, s_ref, o_ref):
        p = jnp.dot(a_ref[...], b_ref[...], preferred_element_type=F32)
        o_ref[:, :QK_NOPE] = (p[:, :QK_NOPE] * scale).astype(BF)
        x = p[:, QK_NOPE:]
        o_ref[:, QK_NOPE:] = ((x * c_ref[...] + _rot_half(x) * s_ref[...]) * scale).astype(BF)

    return pl.pallas_call(
        body, name=name, grid=(H,),
        in_specs=[pl.BlockSpec((T, R), lambda h: (0, 0)), pl.BlockSpec((R, HEAD_PAD), lambda h: (0, h)),
                  pl.BlockSpec((T, 128), lambda h: (0, 0)), pl.BlockSpec((T, 128), lambda h: (0, 0))],
        out_specs=pl.BlockSpec((T, HEAD_PAD), lambda h: (0, h)),
        out_shape=jax.ShapeDtypeStruct((T, H * HEAD_PAD), BF),
        compiler_params=_cparams(("parallel",)),
    )(cq, wq, cos, sin)


def k_proj(ckv, wk, kr, *, name):
    T, R = ckv.shape
    H = wk.shape[1] // QK_NOPE

    def body(a_ref, b_ref, kr_ref, o_ref):
        p = jnp.dot(a_ref[...], b_ref[...], preferred_element_type=F32)
        o_ref[:, :QK_NOPE] = p.astype(BF)
        o_ref[:, QK_NOPE:] = kr_ref[...].astype(BF)

    return pl.pallas_call(
        body, name=name, grid=(H,),
        in_specs=[pl.BlockSpec((T, R), lambda h: (0, 0)), pl.BlockSpec((R, QK_NOPE), lambda h: (0, h)),
                  pl.BlockSpec((T, 128), lambda h: (0, 0))],
        out_specs=pl.BlockSpec((T, HEAD_PAD), lambda h: (0, h)),
        out_shape=jax.ShapeDtypeStruct((T, H * HEAD_PAD), BF),
        compiler_params=_cparams(("parallel",)),
    )(ckv, wk, kr)


def _causal_scores(q, k, q0):
    s = lax.dot_general(q, k, (((1,), (1,)), ((), ())), preferred_element_type=F32)
    row = lax.broadcasted_iota(jnp.int32, s.shape, 0) + q0
    col = lax.broadcasted_iota(jnp.int32, s.shape, 1)
    return jnp.where(col <= row, s, -1e30)


def attn_fwd(q, k, v, *, nch, name):
    T = q.shape[0]
    H = q.shape[1] // HEAD_PAD
    CH = T // nch
    assert CH * nch == T and CH % 16 == 0

    def body(q_ref, k_ref, v_ref, o_ref, lse_ref):
        for qi in range(nch):
            L = (qi + 1) * CH
            s = _causal_scores(q_ref[qi * CH:L, :], k_ref[0:L, :], qi * CH)
            m = jnp.max(s, axis=1, keepdims=True)
            p = jnp.exp(s - m)
            l = jnp.sum(p, axis=1, keepdims=True)
            o = jnp.dot(p.astype(BF), v_ref[0:L, :], preferred_element_type=F32)
            o_ref[qi * CH:L, :] = (o / l).astype(BF)
            lse_ref[qi * CH:L, :] = jnp.broadcast_to(m + jnp.log(l), (CH, V_HEAD))

    return pl.pallas_call(
        body, name=name, grid=(H,),
        in_specs=[pl.BlockSpec((T, HEAD_PAD), lambda h: (0, h)), pl.BlockSpec((T, HEAD_PAD), lambda h: (0, h)),
                  pl.BlockSpec((T, V_HEAD), lambda h: (0, h))],
        out_specs=[pl.BlockSpec((T, V_HEAD), lambda h: (0, h)), pl.BlockSpec((T, V_HEAD), lambda h: (0, h))],
        out_shape=[jax.ShapeDtypeStruct((T, H * V_HEAD), BF), jax.ShapeDtypeStruct((T, H * V_HEAD), F32)],
        compiler_params=_cparams(("parallel",)),
    )(q, k, v)


def attn_bwd(q, k, v, o, do, lse, cos, sin, *, scale, nch, name):
    T = q.shape[0]
    H = q.shape[1] // HEAD_PAD
    CH = T // nch
    tn_dims = (((0,), (0,)), ((), ()))
    nt_dims = (((1,), (1,)), ((), ()))

    def body(q_ref, k_ref, v_ref, o_ref, do_ref, lse_ref, c_ref, s_ref,
             dq_ref, dk_ref, dv_ref, dkr_ref, dk_acc, dv_acc):
        h = pl.program_id(0)
        dk_acc[...] = jnp.zeros_like(dk_acc)
        dv_acc[...] = jnp.zeros_like(dv_acc)
        for qi in range(nch):
            L = (qi + 1) * CH
            rows = slice(qi * CH, L)
            qc, doc = q_ref[rows, :], do_ref[rows, :]
            kc, vc = k_ref[0:L, :], v_ref[0:L, :]
            s = _causal_scores(qc, kc, qi * CH)
            p = jnp.exp(s - lse_ref[rows, 0:1])
            dp = lax.dot_general(doc, vc, nt_dims, preferred_element_type=F32)
            dsum = jnp.sum(doc.astype(F32) * o_ref[rows, :].astype(F32), axis=1, keepdims=True)
            ds = (p * (dp - dsum)).astype(BF)
            dq = jnp.dot(ds, kc, preferred_element_type=F32)
            dq_ref[rows, :QK_NOPE] = (dq[:, :QK_NOPE] * scale).astype(BF)
            dyr = dq[:, QK_NOPE:]
            dq_ref[rows, QK_NOPE:] = ((dyr * c_ref[rows, :] - _rot_half(dyr * s_ref[rows, :])) * scale).astype(BF)
            dk_acc[0:L, :] += lax.dot_general(ds, qc, tn_dims, preferred_element_type=F32)
            dv_acc[0:L, :] += lax.dot_general(p.astype(BF), doc, tn_dims, preferred_element_type=F32)
        dk_ref[:, :QK_NOPE] = dk_acc[:, :QK_NOPE].astype(BF)
        dk_ref[:, QK_NOPE:] = jnp.zeros((T, HEAD_PAD - QK_NOPE), BF)
        dv_ref[...] = dv_acc[...].astype(BF)

        @pl.when(h == 0)
        def _():
            dkr_ref[...] = dk_acc[:, QK_NOPE:]

        @pl.when(h > 0)
        def _():
            dkr_ref[...] += dk_acc[:, QK_NOPE:]

    hp = pl.BlockSpec((T, HEAD_PAD), lambda h: (0, h))
    hv = pl.BlockSpec((T, V_HEAD), lambda h: (0, h))
    tab = pl.BlockSpec((T, 128), lambda h: (0, 0))
    return pl.pallas_call(
        body, name=name, grid=(H,),
        in_specs=[hp, hp, hv, hv, hv, hv, tab, tab],
        out_specs=[hp, hp, hv, tab],
        out_shape=[jax.ShapeDtypeStruct((T, H * HEAD_PAD), BF), jax.ShapeDtypeStruct((T, H * HEAD_PAD), BF),
                   jax.ShapeDtypeStruct((T, H * V_HEAD), BF), jax.ShapeDtypeStruct((T, 128), F32)],
        scratch_shapes=[pltpu.VMEM((T, HEAD_PAD), F32), pltpu.VMEM((T, V_HEAD), F32)],
        compiler_params=_cparams(("arbitrary",)),
    )(q, k, v, o, do, lse, cos, sin)


def _place():
    x, y, c = lax.axis_index("x"), lax.axis_index("y"), lax.axis_index("c")
    return x, y, c


def _other_chips(x, y):
    return [(1 - x, y), (x, 1 - y), (1 - x, 1 - y)]


def gather_devices(v, *, name):
    R, C = v.shape

    def body(v_ref, o_ref, send_sems, recv_sems):
        x, y, c = _place()
        me = 4 * x + 2 * y + c
        o_ref[me] = v_ref[...]
        copies = []
        for k in range(1, N_DEV):
            fx, fy, fc = (k >> 2) & 1, (k >> 1) & 1, k & 1
            to = (x ^ fx, y ^ fy, c ^ fc)
            cp = pltpu.make_async_remote_copy(src_ref=v_ref, dst_ref=o_ref.at[me], send_sem=send_sems.at[k - 1],
                                              recv_sem=recv_sems.at[k - 1], device_id=to, device_id_type=MESH)
            cp.start()
            copies.append(cp)
        for k in range(1, N_DEV):
            fx, fy, fc = (k >> 2) & 1, (k >> 1) & 1, k & 1
            frm = 4 * (x ^ fx) + 2 * (y ^ fy) + (c ^ fc)
            pltpu.make_async_remote_copy(src_ref=v_ref, dst_ref=o_ref.at[frm], send_sem=send_sems.at[k - 1],
                                         recv_sem=recv_sems.at[k - 1], device_id=(x, y, c),
                                         device_id_type=MESH).wait_recv()
        for cp in copies:
            cp.wait_send()

    return pl.pallas_call(
        body, name=name, out_shape=jax.ShapeDtypeStruct((N_DEV, R, C), F32),
        in_specs=[pl.BlockSpec(memory_space=pltpu.VMEM)], out_specs=pl.BlockSpec(memory_space=pltpu.VMEM),
        scratch_shapes=[pltpu.SemaphoreType.DMA((N_DEV - 1,)), pltpu.SemaphoreType.DMA((N_DEV - 1,))],
    )(v)


HBM_SPEC = pl.BlockSpec(memory_space=pltpu.HBM)
SEM_SPEC = pl.BlockSpec(memory_space=pltpu.SEMAPHORE)
DATAFLOW = pltpu.SideEffectType.DATAFLOW_SIDE_EFFECTING


def _hbm(a):
    return pltpu.with_memory_space_constraint(a, pltpu.HBM)


def forward_pair(bufs, axes, *, name):
    n = len(bufs)

    def body(*refs):
        outs = refs[n:2 * n]
        send_sems, recv_sems = refs[2 * n:]
        x, y, c = _place()
        sib = (x, y, 1 - c)
        cps = []
        for i in range(n):
            for j, (cx, cy) in enumerate(_other_chips(x, y)):
                blk = _half(outs[i], (2 * cx + cy,), c, axes[i])
                cp = pltpu.make_async_remote_copy(src_ref=blk, dst_ref=blk, send_sem=send_sems.at[i, j],
                                                  recv_sem=recv_sems.at[i, j], device_id=sib, device_id_type=MESH)
                cp.start()
                cps.append(cp)
        for i in range(n):
            for j, (cx, cy) in enumerate(_other_chips(x, y)):
                blk = _half(outs[i], (2 * cx + cy,), 1 - c, axes[i])
                pltpu.make_async_remote_copy(src_ref=blk, dst_ref=blk, send_sem=send_sems.at[i, j],
                                             recv_sem=recv_sems.at[i, j], device_id=sib,
                                             device_id_type=MESH).wait_recv()
        for cp in cps:
            cp.wait_send()

    anyspec = pl.BlockSpec(memory_space=pl.ANY)
    return pl.pallas_call(
        body, name=name,
        out_shape=[jax.ShapeDtypeStruct(b.shape, b.dtype) for b in bufs],
        in_specs=[anyspec] * n, out_specs=[anyspec] * n,
        input_output_aliases={i: i for i in range(n)},
        scratch_shapes=[pltpu.SemaphoreType.DMA((n, 3)), pltpu.SemaphoreType.DMA((n, 3))],
    )(*bufs)


def _half(ref, lead, c, axis):
    rows, cols = ref.shape[-2], ref.shape[-1]
    if axis == 0:
        return ref.at[(*lead, pl.ds(c * (rows // 2), rows // 2))]
    return ref.at[(*lead, slice(None), pl.ds(c * (cols // 2), cols // 2))]


def copies_start(arrs, plan, n_sems, after, *, name):
    n = len(arrs)

    def body(*refs):
        outs = refs[n + 1:2 * n + 1]
        send_sems, recv_sems, token = refs[2 * n + 1:]
        for cp in plan(outs, send_sems, recv_sems, False):
            cp.start()
        token[...] = jnp.zeros_like(token)

    res = pl.pallas_call(
        body, name=name,
        out_shape=tuple(pltpu.HBM(a.shape, a.dtype) for a in arrs)
        + (pltpu.SemaphoreType.DMA((n_sems,)), pltpu.SemaphoreType.DMA((n_sems,)), jax.ShapeDtypeStruct((8, 128), F32)),
        in_specs=(HBM_SPEC,) * n + (pl.BlockSpec(memory_space=pl.ANY),),
        out_specs=(HBM_SPEC,) * n + (SEM_SPEC, SEM_SPEC, pl.BlockSpec(memory_space=pltpu.VMEM)),
        input_output_aliases={i: i for i in range(n)},
        compiler_params=pltpu.CompilerParams(has_side_effects=DATAFLOW),
    )(*[_hbm(a) for a in arrs], after)
    return list(res[:n]), res[n], res[n + 1], res[n + 2]


def copies_wait(arrs, plan, send_sems, recv_sems, after, *, name):
    n = len(arrs)

    def body(*refs):
        for cp in plan(refs[:n], refs[n], refs[n + 1], True):
            cp.wait_send()
            cp.wait_recv()

    res = pl.pallas_call(
        body, name=name,
        out_shape=tuple(pltpu.HBM(a.shape, a.dtype) for a in arrs),
        in_specs=(HBM_SPEC,) * n + (SEM_SPEC, SEM_SPEC, pl.BlockSpec(memory_space=pl.ANY)),
        out_specs=(HBM_SPEC,) * n,
        input_output_aliases={i: i for i in range(n)},
        compiler_params=pltpu.CompilerParams(has_side_effects=DATAFLOW),
    )(*arrs, send_sems, recv_sems, after)
    return list(res)


def gather_plan(axes):
    def plan(bufs, send_sems, recv_sems, receiving):
        x, y, c = _place()
        me = 2 * x + y
        out = []
        for i, b in enumerate(bufs):
            for j, (cx, cy) in enumerate(_other_chips(x, y)):
                src = _half(b, (me,), c, axes[i])
                dst = _half(b, (2 * cx + cy,), c, axes[i]) if receiving else src
                out.append(pltpu.make_async_remote_copy(
                    src_ref=src, dst_ref=dst, send_sem=send_sems.at[3 * i + j], recv_sem=recv_sems.at[3 * i + j],
                    device_id=(cx, cy, c), device_id_type=MESH))
        return out
    return plan


def scatter_plan(n):
    def plan(arrs, send_sems, recv_sems, receiving):
        x, y, c = _place()
        out = []
        for i in range(n):
            for j, (cx, cy) in enumerate(_other_chips(x, y)):
                out.append(pltpu.make_async_remote_copy(
                    src_ref=arrs[i].at[2 * cx + cy], dst_ref=arrs[n + i].at[j], send_sem=send_sems.at[3 * i + j],
                    recv_sem=recv_sems.at[3 * i + j], device_id=(cx, cy, c), device_id_type=MESH))
        return out
    return plan


def exchange_plan(axes):
    n = len(axes)

    def plan(arrs, send_sems, recv_sems, receiving):
        x, y, c = _place()
        out = []
        for i in range(n):
            for k in range(N_CHIPS):
                out.append(pltpu.make_async_remote_copy(
                    src_ref=_half(arrs[i], (k,), 1 - c, axes[i]), dst_ref=arrs[n + i].at[k],
                    send_sem=send_sems.at[N_CHIPS * i + k], recv_sem=recv_sems.at[N_CHIPS * i + k],
                    device_id=(x, y, 1 - c), device_id_type=MESH))
        return out
    return plan


def devices_plan(arrs, send_sems, recv_sems, receiving):
    x, y, c = _place()
    out = []
    for k in range(1, N_DEV):
        px, py, pc = x ^ ((k >> 2) & 1), y ^ ((k >> 1) & 1), c ^ (k & 1)
        slot = (4 * px + 2 * py + pc) if receiving else (4 * x + 2 * y + c)
        out.append(pltpu.make_async_remote_copy(
            src_ref=arrs[0], dst_ref=arrs[1].at[slot], send_sem=send_sems.at[k - 1], recv_sem=recv_sems.at[k - 1],
            device_id=(px, py, pc), device_id_type=MESH))
    return out


def share_halves(bufs, axes, *, name):
    n = len(bufs)

    def body(*refs):
        outs = refs[n:2 * n]
        send_sems, recv_sems = refs[2 * n:]
        x, y, c = _place()
        sib = (x, y, 1 - c)
        cps = []
        for i in range(n):
            mine = _half(outs[i], (), c, axes[i])
            cp = pltpu.make_async_remote_copy(src_ref=mine, dst_ref=mine, send_sem=send_sems.at[i],
                                              recv_sem=recv_sems.at[i], device_id=sib, device_id_type=MESH)
            cp.start()
            cps.append(cp)
        for i in range(n):
            theirs = _half(outs[i], (), 1 - c, axes[i])
            pltpu.make_async_remote_copy(src_ref=theirs, dst_ref=theirs, send_sem=send_sems.at[i],
                                         recv_sem=recv_sems.at[i], device_id=sib, device_id_type=MESH).wait_recv()
        for cp in cps:
            cp.wait_send()

    anyspec = pl.BlockSpec(memory_space=pl.ANY)
    return pl.pallas_call(
        body, name=name,
        out_shape=[jax.ShapeDtypeStruct(b.shape, b.dtype) for b in bufs],
        in_specs=[anyspec] * n, out_specs=[anyspec] * n,
        input_output_aliases={i: i for i in range(n)},
        scratch_shapes=[pltpu.SemaphoreType.DMA((n,)), pltpu.SemaphoreType.DMA((n,))],
    )(*bufs)


def _row_tile(r, cols, itemsize, target_bytes=2 * 1024 * 1024):
    best = None
    for t in range(16, r + 1, 16):
        if r % t == 0 and t * cols * itemsize <= target_bytes:
            best = t
    return best if best is not None else r


def pair_sum(g, recv, half, axis, *, name):
    _, hr, hc = recv.shape
    tr = _row_tile(hr, hc, 4)
    nb = hr // tr
    if axis == 0:
        mine = pl.BlockSpec((None, tr, hc), lambda k, r, hf: (k, hf[0] * nb + r, 0))
    else:
        mine = pl.BlockSpec((None, tr, hc), lambda k, r, hf: (k, r, hf[0]))

    def body(half_ref, a_ref, b_ref, o_ref):
        o_ref[...] = (a_ref[...].astype(F32) + b_ref[...].astype(F32)).astype(BF)

    return pl.pallas_call(
        body, name=name,
        grid_spec=pltpu.PrefetchScalarGridSpec(
            num_scalar_prefetch=1, grid=(N_CHIPS, nb),
            in_specs=[mine, pl.BlockSpec((None, tr, hc), lambda k, r, hf: (k, r, 0))],
            out_specs=pl.BlockSpec((None, tr, hc), lambda k, r, hf: (k, r, 0))),
        out_shape=jax.ShapeDtypeStruct((N_CHIPS, hr, hc), BF),
        compiler_params=_cparams(("parallel", "parallel")),
    )(half, g, recv)


def chip_sum(s, recv, place, axis, *, name):
    _, hr, hc = s.shape
    tr = _row_tile(hr, hc, 4)
    nb = hr // tr
    if axis == 0:
        o_spec = pl.BlockSpec((tr, hc), lambda r, pc: (pc[1] * nb + r, 0))
        o_shape = (2 * hr, hc)
    else:
        o_spec = pl.BlockSpec((tr, hc), lambda r, pc: (r, pc[1]))
        o_shape = (hr, 2 * hc)

    def body(place_ref, a_ref, b_ref, o_ref):
        acc = a_ref[...].astype(F32)
        for j in range(3):
            acc = acc + b_ref[j].astype(F32)
        o_ref[...] = acc

    return pl.pallas_call(
        body, name=name,
        grid_spec=pltpu.PrefetchScalarGridSpec(
            num_scalar_prefetch=1, grid=(nb,),
            in_specs=[pl.BlockSpec((None, tr, hc), lambda r, pc: (pc[0], r, 0)),
                      pl.BlockSpec((3, tr, hc), lambda r, pc: (0, r, 0))],
            out_specs=o_spec),
        out_shape=jax.ShapeDtypeStruct(o_shape, F32),
        compiler_params=_cparams(("parallel",)),
    )(place, s, recv)


def sum_devices(v, *, name):
    _, R, C = v.shape

    def body(v_ref, o_ref):
        acc = v_ref[0]
        for k in range(1, N_DEV):
            acc = acc + v_ref[k]
        o_ref[...] = acc

    return pl.pallas_call(body, name=name, out_shape=jax.ShapeDtypeStruct((R, C), F32))(v)


def adamw(w, g, m, v, *, name):
    R, C = w.shape
    tr = _row_tile(R, C, 4, target_bytes=1024 * 1024)
    c1 = 1.0 / (1.0 - ADAM_B1 ** ADAM_STEP)
    c2 = 1.0 / (1.0 - ADAM_B2 ** ADAM_STEP)

    def body(w_ref, g_ref, m_ref, v_ref, d_ref, nm_ref, nv_ref):
        gg = g_ref[...]
        nm = ADAM_B1 * m_ref[...] + (1.0 - ADAM_B1) * gg
        nv = ADAM_B2 * v_ref[...] + (1.0 - ADAM_B2) * (gg * gg)
        nm_ref[...] = nm
        nv_ref[...] = nv
        d_ref[...] = -ADAM_LR * ((nm * c1) / (jnp.sqrt(nv * c2) + ADAM_EPS) + ADAM_WD * w_ref[...])

    spec = pl.BlockSpec((tr, C), lambda i: (i, 0))
    return pl.pallas_call(
        body, name=name, grid=(R // tr,), in_specs=[spec] * 4, out_specs=[spec] * 3,
        out_shape=[jax.ShapeDtypeStruct((R, C), F32)] * 3, compiler_params=_cparams(("parallel",)),
    )(w, g, m, v)


def _rope_tables(T):
    pos = np.arange(T, dtype=np.float32)
    inv_freq = (ROPE_THETA ** (-np.arange(0, QK_ROPE, 2, dtype=np.float32) / QK_ROPE)).astype(np.float32)
    ang = pos[:, None] * inv_freq[None, :]
    z = np.zeros((T, 64), np.float32)
    cos = np.concatenate([np.cos(ang), np.cos(ang), z], axis=1).astype(np.float32)
    sin = np.concatenate([np.sin(ang), np.sin(ang), z], axis=1).astype(np.float32)
    return jnp.asarray(cos), jnp.asarray(sin)


def _pick(n, cands):
    for c in cands:
        if n % c == 0:
            return c
    return n


def kernel(x, meta_tokens, g_mix, w_in, b_glu, b_gate, w_dw, b_dw, g_conv_ln, b_conv_ln, w_conv_out, b_conv_out, g_q_lora, w_uq, g_kv_lora, w_uk, w_uv, w_attn_out, w_out, g_ffn, w_ffn_gate, w_ffn_up, w_ffn_down, g_final, loss_target, m_meta_tokens, m_g_mix, m_w_in, m_b_glu, m_b_gate, m_w_dw, m_b_dw, m_g_conv_ln, m_b_conv_ln, m_w_conv_out, m_b_conv_out, m_g_q_lora, m_w_uq, m_g_kv_lora, m_w_uk, m_w_uv, m_w_attn_out, m_w_out, m_g_ffn, m_w_ffn_gate, m_w_ffn_up, m_w_ffn_down, m_g_final, v_meta_tokens, v_g_mix, v_w_in, v_b_glu, v_b_gate, v_w_dw, v_b_dw, v_g_conv_ln, v_b_conv_ln, v_w_conv_out, v_b_conv_out, v_g_q_lora, v_w_uq, v_g_kv_lora, v_w_uk, v_w_uv, v_w_attn_out, v_w_out, v_g_ffn, v_w_ffn_gate, v_w_ffn_up, v_w_ffn_down, v_g_final):
    weights = dict(meta_tokens=meta_tokens, g_mix=g_mix, w_in=w_in, b_glu=b_glu, b_gate=b_gate, w_dw=w_dw, b_dw=b_dw,
                   g_conv_ln=g_conv_ln, b_conv_ln=b_conv_ln, w_conv_out=w_conv_out, b_conv_out=b_conv_out,
                   g_q_lora=g_q_lora, w_uq=w_uq, g_kv_lora=g_kv_lora, w_uk=w_uk, w_uv=w_uv, w_attn_out=w_attn_out,
                   w_out=w_out, g_ffn=g_ffn, w_ffn_gate=w_ffn_gate, w_ffn_up=w_ffn_up, w_ffn_down=w_ffn_down,
                   g_final=g_final)
    m_in = dict(meta_tokens=m_meta_tokens, g_mix=m_g_mix, w_in=m_w_in, b_glu=m_b_glu, b_gate=m_b_gate, w_dw=m_w_dw,
                b_dw=m_b_dw, g_conv_ln=m_g_conv_ln, b_conv_ln=m_b_conv_ln, w_conv_out=m_w_conv_out,
                b_conv_out=m_b_conv_out, g_q_lora=m_g_q_lora, w_uq=m_w_uq, g_kv_lora=m_g_kv_lora, w_uk=m_w_uk,
                w_uv=m_w_uv, w_attn_out=m_w_attn_out, w_out=m_w_out, g_ffn=m_g_ffn, w_ffn_gate=m_w_ffn_gate,
                w_ffn_up=m_w_ffn_up, w_ffn_down=m_w_ffn_down, g_final=m_g_final)
    v_in = dict(meta_tokens=v_meta_tokens, g_mix=v_g_mix, w_in=v_w_in, b_glu=v_b_glu, b_gate=v_b_gate, w_dw=v_w_dw,
                b_dw=v_b_dw, g_conv_ln=v_g_conv_ln, b_conv_ln=v_b_conv_ln, w_conv_out=v_w_conv_out,
                b_conv_out=v_b_conv_out, g_q_lora=v_g_q_lora, w_uq=v_w_uq, g_kv_lora=v_g_kv_lora, w_uk=v_w_uk,
                w_uv=v_w_uv, w_attn_out=v_w_attn_out, w_out=v_w_out, g_ffn=v_g_ffn, w_ffn_gate=v_w_ffn_gate,
                w_ffn_up=v_w_ffn_up, w_ffn_down=v_w_ffn_down, g_final=v_g_final)
    names = list(weights)

    seq, D = x.shape[1], x.shape[2]
    C = w_conv_out.shape[2]
    R = g_q_lora.shape[1]
    H = w_uv.shape[2] * N_CHIPS // V_HEAD
    FB = w_ffn_gate.shape[2]
    FF = FB * N_CHIPS
    length = N_META + seq
    T = -(-length // BLOCK_Q) * BLOCK_Q
    scale = (QK_NOPE + QK_ROPE) ** -0.5
    assert C == D and w_in.shape[2] * N_CHIPS == 2 * C + 2 * R + QK_ROPE + 2 * D

    xi, yi, ci = _place()
    chip = (2 * xi + yi).astype(jnp.int32)
    half_arr = ci.astype(jnp.int32).reshape(1)

    tr = _pick(T, (272, 256, 128))
    trw = _pick(T, (136, 128))
    tmh = T // 2 if (T // 2) % 16 == 0 else T
    tmw = _pick(D, (1024, 512))
    nch = 4
    rc = _pick(T, (136, 128))
    tcv = 256

    def tn_of(n, pref=512):
        return _pick(n, (pref, 384, 256, 128))

    grp_a = ["w_in"]
    grp_b = ["w_conv_out", "w_uq", "w_uk", "w_uv", "w_attn_out", "w_out"]
    grp_c = ["w_ffn_gate", "w_ffn_up", "w_ffn_down"]
    big = grp_a + grp_b + grp_c
    axes = {nm: 0 for nm in big}
    axes["w_in"] = 1
    shard = {nm: weights[nm][0] for nm in big}
    shard["w_in"] = jnp.swapaxes(w_in[0], 0, 1)

    def gather_begin(nms, srcs, after, tag):
        bufs = []
        for w in srcs:
            wb = w.astype(BF)
            bufs.append(lax.dynamic_update_slice(lax.empty((N_CHIPS,) + wb.shape, BF), wb[None], (chip, 0, 0)))
        ax = [axes[nm] for nm in nms]
        bufs, ssem, rsem, tok = copies_start(bufs, gather_plan(ax), 3 * len(nms), after, name=f"gather_{tag}_start")
        return (nms, ax, bufs, ssem, rsem), tok

    def gather_end(state, after, tag):
        nms, ax, bufs, ssem, rsem = state
        bufs = copies_wait(bufs, gather_plan(ax), ssem, rsem, after, name=f"gather_{tag}_wait")
        return dict(zip(nms, forward_pair(bufs, ax, name=f"gather_{tag}_forward")))

    st_ga, tok_ga = gather_begin(grp_a, [shard["w_in"]], g_mix, "a")
    tok_ga, held = lax.optimization_barrier((tok_ga, (meta_tokens, w_dw, x, loss_target,
                                                      [shard[nm] for nm in grp_b + grp_c])))
    meta_l, w_dw_l, x_l, tgt_l, shard_bc = held
    meta_all = gather_devices(meta_l, name="gather_meta")
    meta_full = jnp.concatenate([meta_all[2 * k] for k in range(N_CHIPS)], axis=1)
    dw_all = gather_devices(jnp.pad(w_dw_l[0], ((0, 1), (0, 0))), name="gather_w_dw")
    w_dw_full = jnp.concatenate([dw_all[2 * k] for k in range(N_CHIPS)], axis=1)
    h0 = jnp.concatenate([meta_full, x_l[0], jnp.zeros((T - length, D), F32)], axis=0)
    tgt = jnp.pad(tgt_l[0], ((N_META, T - length), (0, 0)))
    cos, sin = _rope_tables(T)

    G = gather_end(st_ga, h0, "a")
    st_gb, tok_gb = gather_begin(grp_b, shard_bc[:len(grp_b)], G["w_in"], "b")
    st_gc, tok_gc = gather_begin(grp_c, shard_bc[len(grp_b):], tok_gb, "c")

    WT_in = G["w_in"].reshape(N_CHIPS * G["w_in"].shape[1], D)
    o1, o2 = 2 * C, 2 * C + 2 * R + QK_ROPE
    WT_glu = WT_in[:o1]
    WT_small = jnp.pad(WT_in[o1:o2], ((0, 128 - QK_ROPE), (0, 0)))
    WT_gate = WT_in[o2:]

    u = rms_fwd(h0, g_mix, tr=tr, name="rms_mix", deps=[tok_gb, tok_gc])
    z_glu = mm(u, WT_glu, mode="nt", tm=T, tn=tn_of(2 * C), tk=D, out_dtype=F32, name="mm_z_glu")
    z_small = mm(u, WT_small, mode="nt", tm=T, tn=tn_of(2 * R + 128, 384), tk=D, out_dtype=F32, name="mm_z_small")
    z_gate = mm(u, WT_gate, mode="nt", tm=T, tn=tn_of(2 * D), tk=D, out_dtype=F32, name="mm_z_gate")

    c0 = glu_fwd(z_glu, b_glu, tr=tr, name="glu_fwd")
    c1 = dwconv_fwd(c0, w_dw_full, b_dw, tc=tcv, rc=rc, name="dwconv_fwd")
    c3 = ln_silu_fwd(c1, g_conv_ln, b_conv_ln, tr=tr, name="ln_silu_fwd")

    def cols_full(a):
        return jnp.concatenate([a[k] for k in range(N_CHIPS)], axis=1)

    G.update(gather_end(st_gb, c1, "b"))
    W_co = G["w_conv_out"].reshape(C, D)
    W_uq = cols_full(G["w_uq"]).reshape(R, H, QK_NOPE + QK_ROPE)
    Wq_p = jnp.pad(W_uq, ((0, 0), (0, 0), (0, HEAD_PAD - QK_NOPE - QK_ROPE))).reshape(R, H * HEAD_PAD)
    W_uk = cols_full(G["w_uk"])
    W_uv = cols_full(G["w_uv"])
    W_ao = G["w_attn_out"].reshape(H * V_HEAD, D)
    W_o = G["w_out"].reshape(D, D)

    y_conv = mm(c3, W_co, mode="nn", tm=T, tn=tn_of(D), tk=C, out_dtype=F32, name="mm_y_conv")

    cq, ckv, kr = lora_norm_fwd(z_small, g_q_lora, g_kv_lora, cos, sin, tr=tr, name="lora_norm_fwd")
    q = q_proj(cq, Wq_p, cos, sin, scale=scale, name="q_proj")
    kk = k_proj(ckv, W_uk, kr, name="k_proj")
    vv = mm(ckv, W_uv, mode="nn", tm=T, tn=tn_of(H * V_HEAD), tk=R, out_dtype=BF, name="mm_v")
    o_attn, lse = attn_fwd(q, kk, vv, nch=nch, name="attn_fwd")
    y_attn = mm(o_attn, W_ao, mode="nn", tm=T, tn=tn_of(D), tk=H * V_HEAD, out_dtype=F32, name="mm_y_attn")

    mix = mix_fwd(z_gate, y_conv, y_attn, b_gate, b_conv_out, tr=tr, name="mix_fwd")
    h1 = mm(mix, W_o, mode="nn", tm=T, tn=tn_of(D), tk=D, out_dtype=F32, name="mm_h1", res=h0)

    hn = rms_fwd(h1, g_ffn, tr=tr, name="rms_ffn")

    G.update(gather_end(st_gc, hn, "c"))
    W_fg, W_fu = G["w_ffn_gate"], G["w_ffn_up"]
    W_fd = G["w_ffn_down"].reshape(FF, D)

    fa = mm(hn, W_fg, mode="nn", tm=tmh, tn=FB, tk=D, out_dtype=F32, name="mm_ffn_gate")
    fb = mm(hn, W_fu, mode="nn", tm=tmh, tn=FB, tk=D, out_dtype=F32, name="mm_ffn_up")
    f = swiglu_fwd(fa, fb, tr=trw, name="swiglu_fwd")
    h2 = mm(f, W_fd, mode="nn", tm=tmh, tn=tn_of(D, 256), tk=FF, out_dtype=F32, name="mm_h2", res=h1)

    place_arr = jnp.stack([chip, ci.astype(jnp.int32)])

    def col_blocks(a):
        r, cfull = a.shape
        return a.reshape(r, N_CHIPS, cfull // N_CHIPS).transpose(1, 0, 2)

    def rs_begin(nms, gls, after, tag):
        ax = [axes[nm] for nm in nms]
        lands = []
        for g, a in zip(gls, ax):
            _, r, cfull = g.shape
            lands.append(lax.empty((N_CHIPS, r // 2, cfull) if a == 0 else (N_CHIPS, r, cfull // 2), BF))
        arrs, ssem, rsem, tok = copies_start(list(gls) + lands, exchange_plan(ax), N_CHIPS * len(nms), after,
                                             name=f"rs_{tag}_exchange_start")
        return (nms, ax, arrs, ssem, rsem), tok

    def rs_middle(state, after, tag):
        nms, ax, arrs, ssem, rsem = state
        n = len(nms)
        arrs = copies_wait(arrs, exchange_plan(ax), ssem, rsem, after, name=f"rs_{tag}_exchange_wait")
        pair = [pair_sum(g, r, half_arr, a, name="rs_pair_sum_" + nm)
                for nm, g, r, a in zip(nms, arrs[:n], arrs[n:], ax)]
        lands = [lax.empty((3,) + p.shape[1:], BF) for p in pair]
        arrs, ssem, rsem, tok = copies_start(pair + lands, scatter_plan(n), 3 * n, pair[-1],
                                             name=f"rs_{tag}_scatter_start")
        return (nms, ax, arrs, ssem, rsem), tok

    def rs_end(state, after, tag):
        nms, ax, arrs, ssem, rsem = state
        n = len(nms)
        arrs = copies_wait(arrs, scatter_plan(n), ssem, rsem, after, name=f"rs_{tag}_scatter_wait")
        return [chip_sum(s, r, place_arr, a, name="rs_chip_sum_" + nm)
                for nm, s, r, a in zip(nms, arrs[:n], arrs[n:], ax)]

    d_h2, d_h2b, loss_p, d_g_final = final_loss(h2, tgt, g_final.reshape(1, D), seq=seq, tr=tr, name="final_loss")
    loss = lax.psum(loss_p[0, 0], ("x", "y", "c"))

    d_f = mm(d_h2b, W_fd, mode="nt", tm=tmh, tn=FB, tk=D, out_dtype=F32, name="mm_d_f")
    dW_fd = mm(f, d_h2b, mode="tn", tm=FB, tn=tn_of(D), tk=T, out_dtype=BF, name="mm_dw_ffn_down")
    d_a, d_b = swiglu_bwd(fa, fb, d_f, tr=trw, name="swiglu_bwd")
    dW_fg = mm(hn, d_a, mode="tn", tm=tmw, tn=FB, tk=T, out_dtype=BF, name="mm_dw_ffn_gate", out_blocks=N_CHIPS)
    dW_fu = mm(hn, d_b, mode="tn", tm=tmw, tn=FB, tk=T, out_dtype=BF, name="mm_dw_ffn_up", out_blocks=N_CHIPS)
    st_c, tok_rc = rs_begin(grp_c, [dW_fg, dW_fu, dW_fd.reshape(N_CHIPS, FB, D)], dW_fu, "c")
    d_hn = mm(d_a, W_fg, mode="nt", tm=tmh, tn=tn_of(D), tk=FB, out_dtype=F32, name="mm_d_hn_gate", deps=[tok_rc])
    st_c, tok_rc = rs_middle(st_c, d_hn, "c")
    d_hn = mm(d_b, W_fu, mode="nt", tm=tmh, tn=tn_of(D), tk=FB, out_dtype=F32, name="mm_d_hn_up", res=d_hn,
              deps=[tok_rc])
    d_h1, d_h1b, d_g_ffn = rms_bwd(d_hn, h1, g_ffn, d_h2, tr=tr, name="rms_ffn_bwd")

    d_mix = mm(d_h1b, W_o, mode="nt", tm=T, tn=tn_of(D), tk=D, out_dtype=F32, name="mm_d_mix")
    dW_o = mm(mix, d_h1b, mode="tn", tm=tmw, tn=tn_of(D), tk=T, out_dtype=BF, name="mm_dw_out")
    d_yc, d_ya, d_zgate, d_b_gate, d_b_co = mix_bwd(z_gate, y_conv, y_attn, d_mix, b_gate, b_conv_out, tr=tr,
                                                    name="mix_bwd")

    dW_co = mm(c3, d_yc, mode="tn", tm=tmw, tn=tn_of(D), tk=T, out_dtype=BF, name="mm_dw_conv_out")
    d_c3 = mm(d_yc, W_co, mode="nt", tm=T, tn=tn_of(C), tk=D, out_dtype=F32, name="mm_d_c3")
    d_c1, d_g_ln, d_b_ln, d_b_dw = ln_silu_bwd(c1, d_c3, g_conv_ln, b_conv_ln, tr=tr, name="ln_silu_bwd")
    d_c0, d_w_dw = dwconv_bwd(c0, d_c1, w_dw_full, tc=tcv, rc=rc, name="dwconv_bwd")
    d_zglu, d_b_glu = glu_bwd(z_glu, b_glu, d_c0, tr=tr, name="glu_bwd")

    dW_ao = mm(o_attn, d_ya, mode="tn", tm=_pick(H * V_HEAD, (1024, 512)), tn=tn_of(D), tk=T, out_dtype=BF, name="mm_dw_attn_out")
    d_o = mm(d_ya, W_ao, mode="nt", tm=T, tn=tn_of(H * V_HEAD), tk=D, out_dtype=BF, name="mm_d_o")
    d_q2, d_k2, d_v, d_kr = attn_bwd(q, kk, vv, o_attn, d_o, lse, cos, sin, scale=scale, nch=nch, name="attn_bwd")

    dWq_p = mm(cq, d_q2, mode="tn", tm=R, tn=tn_of(H * HEAD_PAD), tk=T, out_dtype=BF, name="mm_dw_uq")
    d_cq = mm(d_q2, Wq_p, mode="nt", tm=T, tn=R, tk=_pick(H * HEAD_PAD, (1024,)), out_dtype=F32, name="mm_d_cq")
    Wk_p = jnp.pad(W_uk.reshape(R, H, QK_NOPE), ((0, 0), (0, 0), (0, HEAD_PAD - QK_NOPE))).reshape(R, H * HEAD_PAD)
    dWk_p = mm(ckv, d_k2, mode="tn", tm=R, tn=tn_of(H * HEAD_PAD), tk=T, out_dtype=BF, name="mm_dw_uk")
    dW_uv = mm(ckv, d_v, mode="tn", tm=R, tn=tn_of(H * V_HEAD), tk=T, out_dtype=BF, name="mm_dw_uv")
    red_c = rs_end(st_c, d_v, "c")
    dW_uq = dWq_p.reshape(R, H, HEAD_PAD)[:, :, :QK_NOPE + QK_ROPE].reshape(R, H * (QK_NOPE + QK_ROPE))
    dW_uk = dWk_p.reshape(R, H, HEAD_PAD)[:, :, :QK_NOPE].reshape(R, H * QK_NOPE)
    st_b, tok_rb = rs_begin(grp_b, [dW_co.reshape(N_CHIPS, C // N_CHIPS, D), col_blocks(dW_uq), col_blocks(dW_uk),
                                    col_blocks(dW_uv), dW_ao.reshape(N_CHIPS, H * V_HEAD // N_CHIPS, D),
                                    dW_o.reshape(N_CHIPS, D // N_CHIPS, D)], red_c[-1], "b")
    d_ckv = mm(d_k2, Wk_p, mode="nt", tm=T, tn=R, tk=_pick(H * HEAD_PAD, (1024,)), out_dtype=F32, name="mm_d_ckv_k",
               deps=[tok_rb])
    d_ckv = mm(d_v, W_uv, mode="nt", tm=T, tn=R, tk=_pick(H * V_HEAD, (1024,)), out_dtype=F32, name="mm_d_ckv_v",
               res=d_ckv)
    d_zsmall, d_g_q, d_g_kv = lora_norm_bwd(z_small, d_cq, d_ckv, d_kr, cos, sin, g_q_lora, g_kv_lora, tr=tr,
                                            name="lora_norm_bwd")
    st_b, tok_rb = rs_middle(st_b, d_zsmall, "b")

    tmi = _pick(2 * C, (1024, 512))
    dWT_glu = mm(d_zglu, u, mode="tn", tm=tmi, tn=tn_of(D), tk=T, out_dtype=BF, name="mm_dw_glu", deps=[tok_rb])
    dWT_small = mm(d_zsmall, u, mode="tn", tm=2 * R + 128, tn=tn_of(D), tk=T, out_dtype=BF, name="mm_dw_small")
    dWT_gate = mm(d_zgate, u, mode="tn", tm=tmi, tn=tn_of(D), tk=T, out_dtype=BF, name="mm_dw_gate")
    red_b = rs_end(st_b, dWT_gate, "b")
    dWT_in = jnp.concatenate([dWT_glu, dWT_small[:2 * R + QK_ROPE], dWT_gate], axis=0)
    st_a, tok_ra = rs_begin(grp_a, [dWT_in.reshape(N_CHIPS, dWT_in.shape[0] // N_CHIPS, D)], red_b[-1], "a")
    d_u = mm(d_zglu, WT_glu, mode="nn", tm=T, tn=tn_of(D), tk=_pick(2 * C, (2048, 1024)), out_dtype=F32, name="mm_d_u_glu",
             deps=[tok_ra])
    st_a, tok_ra = rs_middle(st_a, d_u, "a")
    d_u = mm(d_zsmall, WT_small, mode="nn", tm=T, tn=tn_of(D), tk=2 * R + 128, out_dtype=F32, name="mm_d_u_small",
             res=d_u, deps=[tok_ra])
    d_u = mm(d_zgate, WT_gate, mode="nn", tm=T, tn=tn_of(D), tk=_pick(2 * D, (2048, 1024)), out_dtype=F32,
             name="mm_d_u_gate", res=d_u)
    d_h0, _, d_g_mix = rms_bwd(d_u, h0, g_mix, d_h1, tr=tr, name="rms_mix_bwd")

    grad_x = d_h0[N_META:length][None]

    small = ["g_mix", "b_glu", "b_gate", "b_dw", "g_conv_ln", "b_conv_ln", "b_conv_out", "g_q_lora", "g_kv_lora",
             "g_ffn", "g_final", "w_dw", "meta_tokens"]
    sgrads = dict(g_mix=d_g_mix, b_glu=d_b_glu, b_gate=d_b_gate, b_dw=d_b_dw, g_conv_ln=d_g_ln, b_conv_ln=d_b_ln,
                  b_conv_out=d_b_co, g_q_lora=d_g_q, g_kv_lora=d_g_kv, g_ffn=d_g_ffn, g_final=d_g_final,
                  w_dw=d_w_dw[:CONV_WIDTH], meta_tokens=d_h0[:N_META])
    sizes = [int(np.prod(sgrads[nm].shape)) for nm in small]
    packed = jnp.concatenate([sgrads[nm].reshape(-1) for nm in small]).reshape(-1, 128)
    sm_arrs, sm_ss, sm_rs, tok_sm = copies_start([packed, lax.empty((N_DEV,) + packed.shape, F32)], devices_plan,
                                                 N_DEV - 1, packed, name="gather_small_start")

    red_a = rs_end(st_a, tok_sm, "a")
    reduced = dict(zip(big, share_halves(red_a + red_b + red_c, [axes[nm] for nm in big], name="rs_share_halves")))

    grads, delta, new_m, new_v = {}, {}, {}, {}

    def adamw_2d(nm, w2, g2, m2, v2, back):
        d_, m_, v_ = adamw(w2, g2, m2, v2, name="adamw_" + nm)
        grads[nm], delta[nm], new_m[nm], new_v[nm] = back(g2), back(d_), back(m_), back(v_)

    for nm in big:
        shp = weights[nm].shape
        if axes[nm] == 1:
            adamw_2d(nm, shard[nm], reduced[nm], jnp.swapaxes(m_in[nm][0], 0, 1), jnp.swapaxes(v_in[nm][0], 0, 1),
                     lambda a: jnp.swapaxes(a, 0, 1)[None])
        else:
            two = (shp[-2], shp[-1])
            adamw_2d(nm, weights[nm].reshape(two), reduced[nm], m_in[nm].reshape(two), v_in[nm].reshape(two),
                     lambda a, shp=shp: a.reshape(shp))

    sm_arrs = copies_wait(sm_arrs, devices_plan, sm_ss, sm_rs, delta[big[-1]], name="gather_small_wait")
    me = (4 * xi + 2 * yi + ci).astype(jnp.int32)
    parts = lax.dynamic_update_slice(sm_arrs[1], sm_arrs[0][None], (me, 0, 0))
    summed = sum_devices(parts, name="sum_small_grads").reshape(-1)
    offs = np.concatenate([[0], np.cumsum(sizes)])
    sfull = {nm: summed[int(offs[i]):int(offs[i + 1])].reshape(sgrads[nm].shape) for i, nm in enumerate(small)}
    for nm in small:
        gfull = sfull[nm]
        if nm == "w_dw":
            cb = C // N_CHIPS
            grads[nm] = lax.dynamic_slice(gfull, (0, chip * cb), (CONV_WIDTH, cb))[None]
        elif nm == "meta_tokens":
            cb = D // N_CHIPS
            grads[nm] = lax.dynamic_slice(gfull, (0, chip * cb), (N_META, cb))
        else:
            grads[nm] = gfull.reshape(weights[nm].shape)

    rep = [nm for nm in small if nm not in ("w_dw", "meta_tokens")]

    def pack(d):
        return jnp.concatenate([d[nm].reshape(-1) for nm in rep]).reshape(-1, 128)

    pd, pm, pv = adamw(pack(weights), pack(grads), pack(m_in), pack(v_in), name="adamw_small")
    rsz = [int(np.prod(weights[nm].shape)) for nm in rep]
    roff = np.concatenate([[0], np.cumsum(rsz)])
    for i, nm in enumerate(rep):
        sl = slice(int(roff[i]), int(roff[i + 1]))
        delta[nm] = pd.reshape(-1)[sl].reshape(weights[nm].shape)
        new_m[nm] = pm.reshape(-1)[sl].reshape(weights[nm].shape)
        new_v[nm] = pv.reshape(-1)[sl].reshape(weights[nm].shape)
    for nm in ["w_dw", "meta_tokens"]:
        shp = weights[nm].shape
        two = (shp[-2], shp[-1])
        adamw_2d(nm, weights[nm].reshape(two), grads[nm].reshape(two), m_in[nm].reshape(two), v_in[nm].reshape(two),
                 lambda a, shp=shp: a.reshape(shp))

    return (loss, grad_x, *[grads[nm] for nm in names], *[delta[nm] for nm in names],
            *[new_m[nm] for nm in names], *[new_v[nm] for nm in names])
```

```python
import functools

import numpy as np
import jax
import jax.numpy as jnp
from jax import lax
from jax.experimental import pallas as pl
from jax.experimental.pallas import tpu as pltpu

F32 = jnp.float32
BF = jnp.bfloat16
MESH = pl.DeviceIdType.MESH

N_META = 16
BLOCK_Q = 128
CONV_WIDTH = 31
CONV_PAD = 32
QK_NOPE = 128
QK_ROPE = 64
V_HEAD = 128
HEAD_PAD = 256
ROPE_THETA = 10000.0
EPS = 1e-6
ADAM_LR = 0.001
ADAM_B1 = 0.9
ADAM_B2 = 0.999
ADAM_EPS = 1e-08
ADAM_WD = 0.01
ADAM_STEP = 10
VMEM_LIMIT = 56 * 1024 * 1024
N_CHIPS = 4
N_DEV = 8


def _cparams(sem):
    return pltpu.CompilerParams(dimension_semantics=sem, vmem_limit_bytes=VMEM_LIMIT)


def mm(a, b, *, mode, tm, tn, tk, out_dtype, name, res=None, out_blocks=None, deps=()):
    b3 = b.ndim == 3
    if mode == "nn":
        M, K = a.shape
        N = b.shape[0] * b.shape[2] if b3 else b.shape[1]
        a_spec = pl.BlockSpec((tm, tk), lambda i, j, k: (i, k))
        if b3:
            per = b.shape[2] // tn
            b_spec = pl.BlockSpec((None, tk, tn), lambda i, j, k: (j // per, k, j % per))
        else:
            b_spec = pl.BlockSpec((tk, tn), lambda i, j, k: (k, j))
        dims = (((1,), (0,)), ((), ()))
    elif mode == "nt":
        M, K = a.shape
        N = b.shape[1] if b3 else b.shape[0]
        a_spec = pl.BlockSpec((tm, tk), lambda i, j, k: (i, k))
        if b3:
            per = b.shape[2] // tk
            b_spec = pl.BlockSpec((None, tn, tk), lambda i, j, k: (k // per, j, k % per))
        else:
            b_spec = pl.BlockSpec((tn, tk), lambda i, j, k: (j, k))
        dims = (((1,), (1,)), ((), ()))
    else:
        K, M = a.shape
        N = b.shape[1]
        a_spec = pl.BlockSpec((tk, tm), lambda i, j, k: (k, i))
        b_spec = pl.BlockSpec((tk, tn), lambda i, j, k: (k, j))
        dims = (((0,), (0,)), ((), ()))
    gm, gn, gk = M // tm, N // tn, K // tk
    assert gm * tm == M and gn * tn == N and gk * tk == K, (name, a.shape, b.shape, tm, tn, tk)
    if out_blocks is None:
        o_spec = pl.BlockSpec((tm, tn), lambda i, j, k: (i, j))
        o_shape = (M, N)
    else:
        nbw = N // out_blocks
        per_o = nbw // tn
        assert per_o * tn == nbw
        o_spec = pl.BlockSpec((None, tm, tn), lambda i, j, k: (j // per_o, i, j % per_o))
        o_shape = (out_blocks, M, nbw)
    has_res = res is not None

    def body(*refs):
        a_ref, b_ref = refs[0], refs[1]
        r_ref = refs[2] if has_res else None
        o_ref = refs[2 + has_res + len(deps)]
        acc = refs[-1]
        p = lax.dot_general(a_ref[...], b_ref[...], dims, preferred_element_type=F32)

        def finish(v):
            if has_res:
                v = v + r_ref[...]
            o_ref[...] = v.astype(o_ref.dtype)

        if gk == 1:
            finish(p)
        else:
            k = pl.program_id(2)

            @pl.when(k == 0)
            def _():
                acc[...] = p

            @pl.when(k > 0)
            def _():
                acc[...] += p

            @pl.when(k == gk - 1)
            def _():
                finish(acc[...])

    in_specs = [a_spec, b_spec]
    args = [a, b]
    if has_res:
        in_specs.append(pl.BlockSpec((tm, tn), lambda i, j, k: (i, j)))
        args.append(res)
    in_specs += [pl.BlockSpec(memory_space=pl.ANY)] * len(deps)
    args += list(deps)
    return pl.pallas_call(
        body, name=name, grid=(gm, gn, gk), in_specs=in_specs, out_specs=o_spec,
        out_shape=jax.ShapeDtypeStruct(o_shape, out_dtype),
        scratch_shapes=[pltpu.VMEM((tm, tn), F32)] if gk > 1 else [],
        compiler_params=_cparams(("parallel", "parallel", "arbitrary")),
    )(*args)


def rowwise(body, row_ins, full_ins, row_outs, acc_outs, *, tr, name, deps=()):
    T = row_ins[0].shape[0]
    assert T % tr == 0, (name, T, tr)
    n_ri, n_fi, n_ro = len(row_ins), len(full_ins), len(row_outs)
    n_in = n_ri + n_fi + len(deps)

    def kern(*refs):
        body(pl.program_id(0), refs[:n_ri], refs[n_ri:n_ri + n_fi], refs[n_in:n_in + n_ro], refs[n_in + n_ro:])

    in_specs = [pl.BlockSpec((tr, a.shape[1]), lambda i: (i, 0)) for a in row_ins]
    in_specs += [pl.BlockSpec(a.shape, lambda i: (0, 0)) for a in full_ins]
    in_specs += [pl.BlockSpec(memory_space=pl.ANY)] * len(deps)
    out_specs = [pl.BlockSpec((tr, c), lambda i: (i, 0)) for c, _ in row_outs]
    out_specs += [pl.BlockSpec(s, lambda i: (0, 0)) for s in acc_outs]
    out_shape = [jax.ShapeDtypeStruct((T, c), d) for c, d in row_outs]
    out_shape += [jax.ShapeDtypeStruct(s, F32) for s in acc_outs]
    return pl.pallas_call(
        kern, name=name, grid=(T // tr,), in_specs=in_specs, out_specs=out_specs, out_shape=out_shape,
        compiler_params=_cparams(("arbitrary",)),
    )(*row_ins, *full_ins, *deps)


def _acc(step, ref, val):
    @pl.when(step == 0)
    def _():
        ref[...] = val

    @pl.when(step > 0)
    def _():
        ref[...] += val


def _colsum(x):
    return jnp.sum(x, axis=0, keepdims=True)


def _sigmoid(x):
    return 1.0 / (1.0 + jnp.exp(-x))


def _rot_half(v):
    lane = lax.broadcasted_iota(jnp.int32, v.shape, 1)
    lo = -pltpu.roll(v, 96, 1)
    hi = pltpu.roll(v, 32, 1)
    return jnp.where(lane < 32, lo, jnp.where(lane < 64, hi, 0.0))


def rms_fwd(h, g, *, tr, name, deps=()):
    def body(step, ri, fi, ro, ao):
        x = ri[0][...]
        r = lax.rsqrt(jnp.mean(x * x, axis=-1, keepdims=True) + EPS)
        ro[0][...] = ((x * r) * fi[0][...]).astype(BF)

    return rowwise(body, [h], [g], [(h.shape[1], BF)], [], tr=tr, name=name, deps=deps)[0]


def rms_bwd(dy, h, g, dres, *, tr, name):
    D = h.shape[1]

    def body(step, ri, fi, ro, ao):
        d, x, dr = ri[0][...], ri[1][...], ri[2][...]
        r = lax.rsqrt(jnp.mean(x * x, axis=-1, keepdims=True) + EPS)
        n = x * r
        _acc(step, ao[0], _colsum(d * n))
        dn = d * fi[0][...]
        dh = r * (dn - n * jnp.mean(dn * n, axis=-1, keepdims=True)) + dr
        ro[0][...] = dh
        ro[1][...] = dh.astype(BF)

    return rowwise(body, [dy, h, dres], [g], [(D, F32), (D, BF)], [(1, D)], tr=tr, name=name)


def glu_fwd(z, b, *, tr, name):
    C = z.shape[1] // 2

    def body(step, ri, fi, ro, ao):
        zz = ri[0][...] + fi[0][...]
        ro[0][...] = zz[:, :C] * _sigmoid(zz[:, C:])

    return rowwise(body, [z], [b], [(C, F32)], [], tr=tr, name=name)[0]


def glu_bwd(z, b, dc, *, tr, name):
    C = z.shape[1] // 2

    def body(step, ri, fi, ro, ao):
        zz = ri[0][...] + fi[0][...]
        d = ri[1][...]
        za, sg = zz[:, :C], _sigmoid(zz[:, C:])
        dza = d * sg
        dzb = d * za * sg * (1.0 - sg)
        ro[0][:, :C] = dza.astype(BF)
        ro[0][:, C:] = dzb.astype(BF)
        _acc(step, ao[0], _colsum(dza))
        _acc(step, ao[1], _colsum(dzb))

    dz, da, db = rowwise(body, [z, dc], [b], [(2 * C, BF)], [(1, C), (1, C)], tr=tr, name=name)
    return dz, jnp.concatenate([da, db], axis=1)


def ln_silu_fwd(c1, g, b, *, tr, name):
    def body(step, ri, fi, ro, ao):
        x = ri[0][...]
        mu = jnp.mean(x, axis=-1, keepdims=True)
        xc = x - mu
        rstd = lax.rsqrt(jnp.mean(xc * xc, axis=-1, keepdims=True) + EPS)
        y = (xc * rstd) * fi[0][...] + fi[1][...]
        ro[0][...] = (y * _sigmoid(y)).astype(BF)

    return rowwise(body, [c1], [g, b], [(c1.shape[1], BF)], [], tr=tr, name=name)[0]


def ln_silu_bwd(c1, dc3, g, b, *, tr, name):
    C = c1.shape[1]

    def body(step, ri, fi, ro, ao):
        x, d3 = ri[0][...], ri[1][...]
        mu = jnp.mean(x, axis=-1, keepdims=True)
        xc = x - mu
        rstd = lax.rsqrt(jnp.mean(xc * xc, axis=-1, keepdims=True) + EPS)
        n = xc * rstd
        y = n * fi[0][...] + fi[1][...]
        sg = _sigmoid(y)
        dy = d3 * (sg * (1.0 + y * (1.0 - sg)))
        _acc(step, ao[0], _colsum(dy * n))
        _acc(step, ao[1], _colsum(dy))
        dn = dy * fi[0][...]
        dx = rstd * (dn - jnp.mean(dn, axis=-1, keepdims=True) - n * jnp.mean(dn * n, axis=-1, keepdims=True))
        ro[0][...] = dx
        _acc(step, ao[2], _colsum(dx))

    return rowwise(body, [c1, dc3], [g, b], [(C, F32)], [(1, C), (1, C), (1, C)], tr=tr, name=name)


def lora_norm_fwd(zs, gq, gkv, cos, sin, *, tr, name):
    R = gq.shape[1]

    def body(step, ri, fi, ro, ao):
        z = ri[0][...]
        for o, gi in ((0, 0), (1, 1)):
            x = z[:, o * R:(o + 1) * R]
            r = lax.rsqrt(jnp.mean(x * x, axis=-1, keepdims=True) + EPS)
            ro[o][...] = ((x * r) * fi[gi][...]).astype(BF)
        kr = z[:, 2 * R:2 * R + 128]
        ro[2][...] = kr * ri[1][...] + _rot_half(kr) * ri[2][...]

    return rowwise(body, [zs, cos, sin], [gq, gkv], [(R, BF), (R, BF), (128, F32)], [], tr=tr, name=name)


def lora_norm_bwd(zs, dcq, dckv, dkr, cos, sin, gq, gkv, *, tr, name):
    R = gq.shape[1]

    def body(step, ri, fi, ro, ao):
        z = ri[0][...]
        for o in (0, 1):
            x = z[:, o * R:(o + 1) * R]
            d = ri[1 + o][...]
            r = lax.rsqrt(jnp.mean(x * x, axis=-1, keepdims=True) + EPS)
            n = x * r
            _acc(step, ao[o], _colsum(d * n))
            dn = d * fi[o][...]
            ro[0][:, o * R:(o + 1) * R] = (r * (dn - n * jnp.mean(dn * n, axis=-1, keepdims=True))).astype(BF)
        dk = ri[3][...]
        ro[0][:, 2 * R:2 * R + 128] = (dk * ri[4][...] - _rot_half(dk * ri[5][...])).astype(BF)

    return rowwise(body, [zs, dcq, dckv, dkr, cos, sin], [gq, gkv], [(2 * R + 128, BF)], [(1, R), (1, R)],
                   tr=tr, name=name)


def mix_fwd(zg, yc, ya, bg, bco, *, tr, name):
    D = yc.shape[1]

    def body(step, ri, fi, ro, ao):
        g = _sigmoid(ri[0][...] + fi[0][...])
        ro[0][...] = (g[:, :D] * (ri[1][...] + fi[1][...]) + g[:, D:] * ri[2][...]).astype(BF)

    return rowwise(body, [zg, yc, ya], [bg, bco], [(D, BF)], [], tr=tr, name=name)[0]


def mix_bwd(zg, yc, ya, dmix, bg, bco, *, tr, name):
    D = yc.shape[1]

    def body(step, ri, fi, ro, ao):
        g = _sigmoid(ri[0][...] + fi[0][...])
        gc, ga = g[:, :D], g[:, D:]
        ycv = ri[1][...] + fi[1][...]
        yav = ri[2][...]
        dm = ri[3][...]
        dyc = dm * gc
        ro[0][...] = dyc.astype(BF)
        ro[1][...] = (dm * ga).astype(BF)
        dzc = dm * ycv * gc * (1.0 - gc)
        dza = dm * yav * ga * (1.0 - ga)
        ro[2][:, :D] = dzc.astype(BF)
        ro[2][:, D:] = dza.astype(BF)
        _acc(step, ao[0], _colsum(dzc))
        _acc(step, ao[1], _colsum(dza))
        _acc(step, ao[2], _colsum(dyc))

    dyc, dya, dzg, dbc, dba, dbo = rowwise(body, [zg, yc, ya, dmix], [bg, bco], [(D, BF), (D, BF), (2 * D, BF)],
                                           [(1, D), (1, D), (1, D)], tr=tr, name=name)
    return dyc, dya, dzg, jnp.concatenate([dbc, dba], axis=1), dbo


def swiglu_fwd(a, b, *, tr, name):
    def body(step, ri, fi, ro, ao):
        x = ri[0][...]
        ro[0][...] = (x * _sigmoid(x) * ri[1][...]).astype(BF)

    return rowwise(body, [a, b], [], [(a.shape[1], BF)], [], tr=tr, name=name)[0]


def swiglu_bwd(a, b, df, *, tr, name):
    F = a.shape[1]

    def body(step, ri, fi, ro, ao):
        x, u, d = ri[0][...], ri[1][...], ri[2][...]
        sg = _sigmoid(x)
        ro[0][...] = (d * u * (sg * (1.0 + x * (1.0 - sg)))).astype(BF)
        ro[1][...] = (d * (x * sg)).astype(BF)

    return rowwise(body, [a, b, df], [], [(F, BF), (F, BF)], [], tr=tr, name=name)


def final_loss(h2, tgt, g, *, seq, tr, name):
    D = h2.shape[1]

    def body(step, ri, fi, ro, ao):
        x = ri[0][...]
        r = lax.rsqrt(jnp.mean(x * x, axis=-1, keepdims=True) + EPS)
        n = x * r
        gg = fi[0][...]
        row = lax.broadcasted_iota(jnp.int32, (tr, 1), 0) + step * tr
        live = jnp.logical_and(row >= N_META, row < N_META + seq)
        e = jnp.where(live, n * gg - ri[1][...], 0.0)
        _acc(step, ao[0], jnp.broadcast_to(0.5 * jnp.sum(jnp.mean(e * e, axis=-1, keepdims=True)), (1, 128)))
        dy = e * (1.0 / D)
        _acc(step, ao[1], _colsum(dy * n))
        dn = dy * gg
        dh = r * (dn - n * jnp.mean(dn * n, axis=-1, keepdims=True))
        ro[0][...] = dh
        ro[1][...] = dh.astype(BF)

    return rowwise(body, [h2, tgt], [g], [(D, F32), (D, BF)], [(1, 128), (1, D)], tr=tr, name=name)


def dwconv_fwd(x, w, b, *, tc, rc, name):
    T, C = x.shape
    nchunk = T // rc
    assert nchunk * rc == T and C % tc == 0

    def body(x_ref, w_ref, b_ref, y_ref, xp_ref):
        xp_ref[0:CONV_PAD, :] = jnp.zeros((CONV_PAD, tc), F32)
        xp_ref[CONV_PAD:CONV_PAD + T, :] = x_ref[...]
        off = CONV_PAD - (CONV_WIDTH - 1)
        for ci in range(nchunk):
            t0 = ci * rc
            acc = jnp.broadcast_to(b_ref[...], (rc, tc))
            for j in range(CONV_WIDTH):
                acc = acc + w_ref[j:j + 1, :] * xp_ref[t0 + off + j:t0 + off + j + rc, :]
            y_ref[t0:t0 + rc, :] = acc

    return pl.pallas_call(
        body, name=name, grid=(C // tc,),
        in_specs=[pl.BlockSpec((T, tc), lambda j: (0, j)), pl.BlockSpec((32, tc), lambda j: (0, j)),
                  pl.BlockSpec((1, tc), lambda j: (0, j))],
        out_specs=pl.BlockSpec((T, tc), lambda j: (0, j)),
        out_shape=jax.ShapeDtypeStruct((T, C), F32),
        scratch_shapes=[pltpu.VMEM((T + CONV_PAD, tc), F32)],
        compiler_params=_cparams(("parallel",)),
    )(x, w, b)


def dwconv_bwd(x, dy, w, *, tc, rc, name):
    T, C = x.shape
    nchunk = T // rc
    assert nchunk * rc == T and C % tc == 0 and rc % 8 == 0

    def body(x_ref, dy_ref, w_ref, dx_ref, dw_ref, xp_ref, dp_ref):
        off = CONV_PAD - (CONV_WIDTH - 1)
        xp_ref[0:CONV_PAD, :] = jnp.zeros((CONV_PAD, tc), F32)
        xp_ref[CONV_PAD:CONV_PAD + T, :] = x_ref[...]
        dp_ref[0:T, :] = dy_ref[...]
        dp_ref[T:T + CONV_PAD, :] = jnp.zeros((CONV_PAD, tc), F32)
        for ci in range(nchunk):
            t0 = ci * rc
            acc = jnp.zeros((rc, tc), F32)
            for j in range(CONV_WIDTH):
                s = t0 + (CONV_WIDTH - 1) - j
                acc = acc + w_ref[j:j + 1, :] * dp_ref[s:s + rc, :]
            dx_ref[t0:t0 + rc, :] = acc
        dw_ref[...] = jnp.zeros((32, tc), F32)
        for j in range(CONV_WIDTH):
            acc = jnp.zeros((8, tc), F32)
            for ci in range(nchunk):
                t0 = ci * rc
                pr = dp_ref[t0:t0 + rc, :] * xp_ref[t0 + off + j:t0 + off + j + rc, :]
                acc = acc + jnp.sum(pr.reshape(rc // 8, 8, tc), axis=0)
            dw_ref[j:j + 1, :] = jnp.sum(acc, axis=0, keepdims=True)

    return pl.pallas_call(
        body, name=name, grid=(C // tc,),
        in_specs=[pl.BlockSpec((T, tc), lambda j: (0, j)), pl.BlockSpec((T, tc), lambda j: (0, j)),
                  pl.BlockSpec((32, tc), lambda j: (0, j))],
        out_specs=[pl.BlockSpec((T, tc), lambda j: (0, j)), pl.BlockSpec((32, tc), lambda j: (0, j))],
        out_shape=[jax.ShapeDtypeStruct((T, C), F32), jax.ShapeDtypeStruct((32, C), F32)],
        scratch_shapes=[pltpu.VMEM((T + CONV_PAD, tc), F32), pltpu.VMEM((T + CONV_PAD, tc), F32)],
        compiler_params=_cparams(("parallel",)),
    )(x, dy, w)


def q_proj(cq, wq, cos, sin, *, scale, name):
    T, R = cq.shape
    H = wq.shape[1] // HEAD_PAD

    def body(a_ref, b_ref, c_ref, s_ref, o_ref):
        p = jnp.dot(a_ref[...], b_ref[...], preferred_element_type=F32)
        o_ref[:, :QK_NOPE] = (p[:, :QK_NOPE] * scale).astype(BF)
        x = p[:, QK_NOPE:]
        o_ref[:, QK_NOPE:] = ((x * c_ref[...] + _rot_half(x) * s_ref[...]) * scale).astype(BF)

    return pl.pallas_call(
        body, name=name, grid=(H,),
        in_specs=[pl.BlockSpec((T, R), lambda h: (0, 0)), pl.BlockSpec((R, HEAD_PAD), lambda h: (0, h)),
                  pl.BlockSpec((T, 128), lambda h: (0, 0)), pl.BlockSpec((T, 128), lambda h: (0, 0))],
        out_specs=pl.BlockSpec((T, HEAD_PAD), lambda h: (0, h)),
        out_shape=jax.ShapeDtypeStruct((T, H * HEAD_PAD), BF),
        compiler_params=_cparams(("parallel",)),
    )(cq, wq, cos, sin)


def k_proj(ckv, wk, kr, *, name):
    T, R = ckv.shape
    H = wk.shape[1] // QK_NOPE

    def body(a_ref, b_ref, kr_ref, o_ref):
        p = jnp.dot(a_ref[...], b_ref[...], preferred_element_type=F32)
        o_ref[:, :QK_NOPE] = p.astype(BF)
        o_ref[:, QK_NOPE:] = kr_ref[...].astype(BF)

    return pl.pallas_call(
        body, name=name, grid=(H,),
        in_specs=[pl.BlockSpec((T, R), lambda h: (0, 0)), pl.BlockSpec((R, QK_NOPE), lambda h: (0, h)),
                  pl.BlockSpec((T, 128), lambda h: (0, 0))],
        out_specs=pl.BlockSpec((T, HEAD_PAD), lambda h: (0, h)),
        out_shape=jax.ShapeDtypeStruct((T, H * HEAD_PAD), BF),
        compiler_params=_cparams(("parallel",)),
    )(ckv, wk, kr)


def _causal_scores(q, k, q0):
    s = lax.dot_general(q, k, (((1,), (1,)), ((), ())), preferred_element_type=F32)
    row = lax.broadcasted_iota(jnp.int32, s.shape, 0) + q0
    col = lax.broadcasted_iota(jnp.int32, s.shape, 1)
    return jnp.where(col <= row, s, -1e30)


def attn_fwd(q, k, v, *, nch, name):
    T = q.shape[0]
    H = q.shape[1] // HEAD_PAD
    CH = T // nch
    assert CH * nch == T and CH % 16 == 0

    def body(q_ref, k_ref, v_ref, o_ref, lse_ref):
        for qi in range(nch):
            L = (qi + 1) * CH
            s = _causal_scores(q_ref[qi * CH:L, :], k_ref[0:L, :], qi * CH)
            m = jnp.max(s, axis=1, keepdims=True)
            p = jnp.exp(s - m)
            l = jnp.sum(p, axis=1, keepdims=True)
            o = jnp.dot(p.astype(BF), v_ref[0:L, :], preferred_element_type=F32)
            o_ref[qi * CH:L, :] = (o / l).astype(BF)
            lse_ref[qi * CH:L, :] = jnp.broadcast_to(m + jnp.log(l), (CH, V_HEAD))

    return pl.pallas_call(
        body, name=name, grid=(H,),
        in_specs=[pl.BlockSpec((T, HEAD_PAD), lambda h: (0, h)), pl.BlockSpec((T, HEAD_PAD), lambda h: (0, h)),
                  pl.BlockSpec((T, V_HEAD), lambda h: (0, h))],
        out_specs=[pl.BlockSpec((T, V_HEAD), lambda h: (0, h)), pl.BlockSpec((T, V_HEAD), lambda h: (0, h))],
        out_shape=[jax.ShapeDtypeStruct((T, H * V_HEAD), BF), jax.ShapeDtypeStruct((T, H * V_HEAD), F32)],
        compiler_params=_cparams(("parallel",)),
    )(q, k, v)


def attn_bwd(q, k, v, o, do, lse, cos, sin, *, scale, nch, name):
    T = q.shape[0]
    H = q.shape[1] // HEAD_PAD
    CH = T // nch
    tn_dims = (((0,), (0,)), ((), ()))
    nt_dims = (((1,), (1,)), ((), ()))

    def body(q_ref, k_ref, v_ref, o_ref, do_ref, lse_ref, c_ref, s_ref,
             dq_ref, dk_ref, dv_ref, dkr_ref, dk_acc, dv_acc):
        h = pl.program_id(0)
        dk_acc[...] = jnp.zeros_like(dk_acc)
        dv_acc[...] = jnp.zeros_like(dv_acc)
        for qi in range(nch):
            L = (qi + 1) * CH
            rows = slice(qi * CH, L)
            qc, doc = q_ref[rows, :], do_ref[rows, :]
            kc, vc = k_ref[0:L, :], v_ref[0:L, :]
            s = _causal_scores(qc, kc, qi * CH)
            p = jnp.exp(s - lse_ref[rows, 0:1])
            dp = lax.dot_general(doc, vc, nt_dims, preferred_element_type=F32)
            dsum = jnp.sum(doc.astype(F32) * o_ref[rows, :].astype(F32), axis=1, keepdims=True)
            ds = (p * (dp - dsum)).astype(BF)
            dq = jnp.dot(ds, kc, preferred_element_type=F32)
            dq_ref[rows, :QK_NOPE] = (dq[:, :QK_NOPE] * scale).astype(BF)
            dyr = dq[:, QK_NOPE:]
            dq_ref[rows, QK_NOPE:] = ((dyr * c_ref[rows, :] - _rot_half(dyr * s_ref[rows, :])) * scale).astype(BF)
            dk_acc[0:L, :] += lax.dot_general(ds, qc, tn_dims, preferred_element_type=F32)
            dv_acc[0:L, :] += lax.dot_general(p.astype(BF), doc, tn_dims, preferred_element_type=F32)
        dk_ref[:, :QK_NOPE] = dk_acc[:, :QK_NOPE].astype(BF)
        dk_ref[:, QK_NOPE:] = jnp.zeros((T, HEAD_PAD - QK_NOPE), BF)
        dv_ref[...] = dv_acc[...].astype(BF)

        @pl.when(h == 0)
        def _():
            dkr_ref[...] = dk_acc[:, QK_NOPE:]

        @pl.when(h > 0)
        def _():
            dkr_ref[...] += dk_acc[:, QK_NOPE:]

    hp = pl.BlockSpec((T, HEAD_PAD), lambda h: (0, h))
    hv = pl.BlockSpec((T, V_HEAD), lambda h: (0, h))
    tab = pl.BlockSpec((T, 128), lambda h: (0, 0))
    return pl.pallas_call(
        body, name=name, grid=(H,),
        in_specs=[hp, hp, hv, hv, hv, hv, tab, tab],
        out_specs=[hp, hp, hv, tab],
        out_shape=[jax.ShapeDtypeStruct((T, H * HEAD_PAD), BF), jax.ShapeDtypeStruct((T, H * HEAD_PAD), BF),
                   jax.ShapeDtypeStruct((T, H * V_HEAD), BF), jax.ShapeDtypeStruct((T, 128), F32)],
        scratch_shapes=[pltpu.VMEM((T, HEAD_PAD), F32), pltpu.VMEM((T, V_HEAD), F32)],
        compiler_params=_cparams(("arbitrary",)),
    )(q, k, v, o, do, lse, cos, sin)


def _place():
    x, y, c = lax.axis_index("x"), lax.axis_index("y"), lax.axis_index("c")
    return x, y, c


def _other_chips(x, y):
    return [(1 - x, y), (x, 1 - y), (1 - x, 1 - y)]


def gather_devices(v, *, name):
    R, C = v.shape

    def body(v_ref, o_ref, send_sems, recv_sems):
        x, y, c = _place()
        me = 4 * x + 2 * y + c
        o_ref[me] = v_ref[...]
        copies = []
        for k in range(1, N_DEV):
            fx, fy, fc = (k >> 2) & 1, (k >> 1) & 1, k & 1
            to = (x ^ fx, y ^ fy, c ^ fc)
            cp = pltpu.make_async_remote_copy(src_ref=v_ref, dst_ref=o_ref.at[me], send_sem=send_sems.at[k - 1],
                                              recv_sem=recv_sems.at[k - 1], device_id=to, device_id_type=MESH)
            cp.start()
            copies.append(cp)
        for k in range(1, N_DEV):
            fx, fy, fc = (k >> 2) & 1, (k >> 1) & 1, k & 1
            frm = 4 * (x ^ fx) + 2 * (y ^ fy) + (c ^ fc)
            pltpu.make_async_remote_copy(src_ref=v_ref, dst_ref=o_ref.at[frm], send_sem=send_sems.at[k - 1],
                                         recv_sem=recv_sems.at[k - 1], device_id=(x, y, c),
                                         device_id_type=MESH).wait_recv()
        for cp in copies:
            cp.wait_send()

    return pl.pallas_call(
        body, name=name, out_shape=jax.ShapeDtypeStruct((N_DEV, R, C), F32),
        in_specs=[pl.BlockSpec(memory_space=pltpu.VMEM)], out_specs=pl.BlockSpec(memory_space=pltpu.VMEM),
        scratch_shapes=[pltpu.SemaphoreType.DMA((N_DEV - 1,)), pltpu.SemaphoreType.DMA((N_DEV - 1,))],
    )(v)


HBM_SPEC = pl.BlockSpec(memory_space=pltpu.HBM)
SEM_SPEC = pl.BlockSpec(memory_space=pltpu.SEMAPHORE)
DATAFLOW = pltpu.SideEffectType.DATAFLOW_SIDE_EFFECTING


def _hbm(a):
    return pltpu.with_memory_space_constraint(a, pltpu.HBM)


def forward_pair(bufs, axes, *, name):
    n = len(bufs)

    def body(*refs):
        outs = refs[n:2 * n]
        send_sems, recv_sems = refs[2 * n:]
        x, y, c = _place()
        sib = (x, y, 1 - c)
        cps = []
        for i in range(n):
            for j, (cx, cy) in enumerate(_other_chips(x, y)):
                blk = _half(outs[i], (2 * cx + cy,), c, axes[i])
                cp = pltpu.make_async_remote_copy(src_ref=blk, dst_ref=blk, send_sem=send_sems.at[i, j],
                                                  recv_sem=recv_sems.at[i, j], device_id=sib, device_id_type=MESH)
                cp.start()
                cps.append(cp)
        for i in range(n):
            for j, (cx, cy) in enumerate(_other_chips(x, y)):
                blk = _half(outs[i], (2 * cx + cy,), 1 - c, axes[i])
                pltpu.make_async_remote_copy(src_ref=blk, dst_ref=blk, send_sem=send_sems.at[i, j],
                                             recv_sem=recv_sems.at[i, j], device_id=sib,
                                             device_id_type=MESH).wait_recv()
        for cp in cps:
            cp.wait_send()

    anyspec = pl.BlockSpec(memory_space=pl.ANY)
    return pl.pallas_call(
        body, name=name,
        out_shape=[jax.ShapeDtypeStruct(b.shape, b.dtype) for b in bufs],
        in_specs=[anyspec] * n, out_specs=[anyspec] * n,
        input_output_aliases={i: i for i in range(n)},
        scratch_shapes=[pltpu.SemaphoreType.DMA((n, 3)), pltpu.SemaphoreType.DMA((n, 3))],
    )(*bufs)


def _half(ref, lead, c, axis):
    rows, cols = ref.shape[-2], ref.shape[-1]
    if axis == 0:
        return ref.at[(*lead, pl.ds(c * (rows // 2), rows // 2))]
    return ref.at[(*lead, slice(None), pl.ds(c * (cols // 2), cols // 2))]


def copies_start(arrs, plan, n_sems, after, *, name):
    n = len(arrs)

    def body(*refs):
        outs = refs[n + 1:2 * n + 1]
        send_sems, recv_sems, token = refs[2 * n + 1:]
        for cp in plan(outs, send_sems, recv_sems, False):
            cp.start()
        token[...] = jnp.zeros_like(token)

    res = pl.pallas_call(
        body, name=name,
        out_shape=tuple(pltpu.HBM(a.shape, a.dtype) for a in arrs)
        + (pltpu.SemaphoreType.DMA((n_sems,)), pltpu.SemaphoreType.DMA((n_sems,)), jax.ShapeDtypeStruct((8, 128), F32)),
        in_specs=(HBM_SPEC,) * n + (pl.BlockSpec(memory_space=pl.ANY),),
        out_specs=(HBM_SPEC,) * n + (SEM_SPEC, SEM_SPEC, pl.BlockSpec(memory_space=pltpu.VMEM)),
        input_output_aliases={i: i for i in range(n)},
        compiler_params=pltpu.CompilerParams(has_side_effects=DATAFLOW),
    )(*[_hbm(a) for a in arrs], after)
    return list(res[:n]), res[n], res[n + 1], res[n + 2]


def copies_wait(arrs, plan, send_sems, recv_sems, after, *, name):
    n = len(arrs)

    def body(*refs):
        for cp in plan(refs[:n], refs[n], refs[n + 1], True):
            cp.wait_send()
            cp.wait_recv()

    res = pl.pallas_call(
        body, name=name,
        out_shape=tuple(pltpu.HBM(a.shape, a.dtype) for a in arrs),
        in_specs=(HBM_SPEC,) * n + (SEM_SPEC, SEM_SPEC, pl.BlockSpec(memory_space=pl.ANY)),
        out_specs=(HBM_SPEC,) * n,
        input_output_aliases={i: i for i in range(n)},
        compiler_params=pltpu.CompilerParams(has_side_effects=DATAFLOW),
    )(*arrs, send_sems, recv_sems, after)
    return list(res)


def gather_plan(axes):
    def plan(bufs, send_sems, recv_sems, receiving):
        x, y, c = _place()
        me = 2 * x + y
        out = []
        for i, b in enumerate(bufs):
            for j, (cx, cy) in enumerate(_other_chips(x, y)):
                src = _half(b, (me,), c, axes[i])
                dst = _half(b, (2 * cx + cy,), c, axes[i]) if receiving else src
                out.append(pltpu.make_async_remote_copy(
                    src_ref=src, dst_ref=dst, send_sem=send_sems.at[3 * i + j], recv_sem=recv_sems.at[3 * i + j],
                    device_id=(cx, cy, c), device_id_type=MESH))
        return out
    return plan


def scatter_plan(n):
    def plan(arrs, send_sems, recv_sems, receiving):
        x, y, c = _place()
        out = []
        for i in range(n):
            for j, (cx, cy) in enumerate(_other_chips(x, y)):
                out.append(pltpu.make_async_remote_copy(
                    src_ref=arrs[i].at[2 * cx + cy], dst_ref=arrs[n + i].at[j], send_sem=send_sems.at[3 * i + j],
                    recv_sem=recv_sems.at[3 * i + j], device_id=(cx, cy, c), device_id_type=MESH))
        return out
    return plan


def exchange_plan(axes):
    n = len(axes)

    def plan(arrs, send_sems, recv_sems, receiving):
        x, y, c = _place()
        out = []
        for i in range(n):
            for k in range(N_CHIPS):
                out.append(pltpu.make_async_remote_copy(
                    src_ref=_half(arrs[i], (k,), 1 - c, axes[i]), dst_ref=arrs[n + i].at[k],
                    send_sem=send_sems.at[N_CHIPS * i + k], recv_sem=recv_sems.at[N_CHIPS * i + k],
                    device_id=(x, y, 1 - c), device_id_type=MESH))
        return out
    return plan


def devices_plan(arrs, send_sems, recv_sems, receiving):
    x, y, c = _place()
    out = []
    for k in range(1, N_DEV):
        px, py, pc = x ^ ((k >> 2) & 1), y ^ ((k >> 1) & 1), c ^ (k & 1)
        slot = (4 * px + 2 * py + pc) if receiving else (4 * x + 2 * y + c)
        out.append(pltpu.make_async_remote_copy(
            src_ref=arrs[0], dst_ref=arrs[1].at[slot], send_sem=send_sems.at[k - 1], recv_sem=recv_sems.at[k - 1],
            device_id=(px, py, pc), device_id_type=MESH))
    return out


def share_halves(bufs, axes, after, *, name):
    n = len(bufs)

    def body(*refs):
        outs = refs[n + 1:2 * n + 1]
        send_sems, recv_sems = refs[2 * n + 1:]
        x, y, c = _place()
        sib = (x, y, 1 - c)
        cps = []
        for i in range(n):
            mine = _half(outs[i], (), c, axes[i])
            cp = pltpu.make_async_remote_copy(src_ref=mine, dst_ref=mine, send_sem=send_sems.at[i],
                                              recv_sem=recv_sems.at[i], device_id=sib, device_id_type=MESH)
            cp.start()
            cps.append(cp)
        for i in range(n):
            theirs = _half(outs[i], (), 1 - c, axes[i])
            pltpu.make_async_remote_copy(src_ref=theirs, dst_ref=theirs, send_sem=send_sems.at[i],
                                         recv_sem=recv_sems.at[i], device_id=sib, device_id_type=MESH).wait_recv()
        for cp in cps:
            cp.wait_send()

    anyspec = pl.BlockSpec(memory_space=pl.ANY)
    return pl.pallas_call(
        body, name=name,
        out_shape=[jax.ShapeDtypeStruct(b.shape, b.dtype) for b in bufs],
        in_specs=[anyspec] * (n + 1), out_specs=[anyspec] * n,
        input_output_aliases={i: i for i in range(n)},
        scratch_shapes=[pltpu.SemaphoreType.DMA((n,)), pltpu.SemaphoreType.DMA((n,))],
    )(*bufs, after)


def _row_tile(r, cols, itemsize, target_bytes=2 * 1024 * 1024):
    best = None
    for t in range(16, r + 1, 16):
        if r % t == 0 and t * cols * itemsize <= target_bytes:
            best = t
    return best if best is not None else r


def pair_sum(g, recv, half, axis, *, name):
    _, hr, hc = recv.shape
    tr = _row_tile(hr, hc, 4)
    nb = hr // tr
    if axis == 0:
        mine = pl.BlockSpec((None, tr, hc), lambda k, r, hf: (k, hf[0] * nb + r, 0))
    else:
        mine = pl.BlockSpec((None, tr, hc), lambda k, r, hf: (k, r, hf[0]))

    def body(half_ref, a_ref, b_ref, o_ref):
        o_ref[...] = (a_ref[...].astype(F32) + b_ref[...].astype(F32)).astype(BF)

    return pl.pallas_call(
        body, name=name,
        grid_spec=pltpu.PrefetchScalarGridSpec(
            num_scalar_prefetch=1, grid=(N_CHIPS, nb),
            in_specs=[mine, pl.BlockSpec((None, tr, hc), lambda k, r, hf: (k, r, 0))],
            out_specs=pl.BlockSpec((None, tr, hc), lambda k, r, hf: (k, r, 0))),
        out_shape=jax.ShapeDtypeStruct((N_CHIPS, hr, hc), BF),
        compiler_params=_cparams(("parallel", "parallel")),
    )(half, g, recv)


def chip_sum(s, recv, place, axis, *, name):
    _, hr, hc = s.shape
    tr = _row_tile(hr, hc, 4)
    nb = hr // tr
    if axis == 0:
        o_spec = pl.BlockSpec((tr, hc), lambda r, pc: (pc[1] * nb + r, 0))
        o_shape = (2 * hr, hc)
    else:
        o_spec = pl.BlockSpec((tr, hc), lambda r, pc: (r, pc[1]))
        o_shape = (hr, 2 * hc)

    def body(place_ref, a_ref, b_ref, o_ref):
        acc = a_ref[...].astype(F32)
        for j in range(3):
            acc = acc + b_ref[j].astype(F32)
        o_ref[...] = acc

    return pl.pallas_call(
        body, name=name,
        grid_spec=pltpu.PrefetchScalarGridSpec(
            num_scalar_prefetch=1, grid=(nb,),
            in_specs=[pl.BlockSpec((None, tr, hc), lambda r, pc: (pc[0], r, 0)),
                      pl.BlockSpec((3, tr, hc), lambda r, pc: (0, r, 0))],
            out_specs=o_spec),
        out_shape=jax.ShapeDtypeStruct(o_shape, F32),
        compiler_params=_cparams(("parallel",)),
    )(place, s, recv)


def sum_devices(v, *, name):
    _, R, C = v.shape

    def body(v_ref, o_ref):
        acc = v_ref[0]
        for k in range(1, N_DEV):
            acc = acc + v_ref[k]
        o_ref[...] = acc

    return pl.pallas_call(body, name=name, out_shape=jax.ShapeDtypeStruct((R, C), F32))(v)


def adamw(w, g, m, v, *, name):
    R, C = w.shape
    tr = _row_tile(R, C, 4, target_bytes=1024 * 1024)
    c1 = 1.0 / (1.0 - ADAM_B1 ** ADAM_STEP)
    c2 = 1.0 / (1.0 - ADAM_B2 ** ADAM_STEP)

    def body(w_ref, g_ref, m_ref, v_ref, d_ref, nm_ref, nv_ref):
        gg = g_ref[...]
        nm = ADAM_B1 * m_ref[...] + (1.0 - ADAM_B1) * gg
        nv = ADAM_B2 * v_ref[...] + (1.0 - ADAM_B2) * (gg * gg)
        nm_ref[...] = nm
        nv_ref[...] = nv
        d_ref[...] = -ADAM_LR * ((nm * c1) / (jnp.sqrt(nv * c2) + ADAM_EPS) + ADAM_WD * w_ref[...])

    spec = pl.BlockSpec((tr, C), lambda i: (i, 0))
    return pl.pallas_call(
        body, name=name, grid=(R // tr,), in_specs=[spec] * 4, out_specs=[spec] * 3,
        out_shape=[jax.ShapeDtypeStruct((R, C), F32)] * 3, compiler_params=_cparams(("parallel",)),
    )(w, g, m, v)


def _rope_tables(T):
    pos = np.arange(T, dtype=np.float32)
    inv_freq = (ROPE_THETA ** (-np.arange(0, QK_ROPE, 2, dtype=np.float32) / QK_ROPE)).astype(np.float32)
    ang = pos[:, None] * inv_freq[None, :]
    z = np.zeros((T, 64), np.float32)
    cos = np.concatenate([np.cos(ang), np.cos(ang), z], axis=1).astype(np.float32)
    sin = np.concatenate([np.sin(ang), np.sin(ang), z], axis=1).astype(np.float32)
    return jnp.asarray(cos), jnp.asarray(sin)


def _pick(n, cands):
    for c in cands:
        if n % c == 0:
            return c
    return n


def kernel(x, meta_tokens, g_mix, w_in, b_glu, b_gate, w_dw, b_dw, g_conv_ln, b_conv_ln, w_conv_out, b_conv_out, g_q_lora, w_uq, g_kv_lora, w_uk, w_uv, w_attn_out, w_out, g_ffn, w_ffn_gate, w_ffn_up, w_ffn_down, g_final, loss_target, m_meta_tokens, m_g_mix, m_w_in, m_b_glu, m_b_gate, m_w_dw, m_b_dw, m_g_conv_ln, m_b_conv_ln, m_w_conv_out, m_b_conv_out, m_g_q_lora, m_w_uq, m_g_kv_lora, m_w_uk, m_w_uv, m_w_attn_out, m_w_out, m_g_ffn, m_w_ffn_gate, m_w_ffn_up, m_w_ffn_down, m_g_final, v_meta_tokens, v_g_mix, v_w_in, v_b_glu, v_b_gate, v_w_dw, v_b_dw, v_g_conv_ln, v_b_conv_ln, v_w_conv_out, v_b_conv_out, v_g_q_lora, v_w_uq, v_g_kv_lora, v_w_uk, v_w_uv, v_w_attn_out, v_w_out, v_g_ffn, v_w_ffn_gate, v_w_ffn_up, v_w_ffn_down, v_g_final):
    weights = dict(meta_tokens=meta_tokens, g_mix=g_mix, w_in=w_in, b_glu=b_glu, b_gate=b_gate, w_dw=w_dw, b_dw=b_dw,
                   g_conv_ln=g_conv_ln, b_conv_ln=b_conv_ln, w_conv_out=w_conv_out, b_conv_out=b_conv_out,
                   g_q_lora=g_q_lora, w_uq=w_uq, g_kv_lora=g_kv_lora, w_uk=w_uk, w_uv=w_uv, w_attn_out=w_attn_out,
                   w_out=w_out, g_ffn=g_ffn, w_ffn_gate=w_ffn_gate, w_ffn_up=w_ffn_up, w_ffn_down=w_ffn_down,
                   g_final=g_final)
    m_in = dict(meta_tokens=m_meta_tokens, g_mix=m_g_mix, w_in=m_w_in, b_glu=m_b_glu, b_gate=m_b_gate, w_dw=m_w_dw,
                b_dw=m_b_dw, g_conv_ln=m_g_conv_ln, b_conv_ln=m_b_conv_ln, w_conv_out=m_w_conv_out,
                b_conv_out=m_b_conv_out, g_q_lora=m_g_q_lora, w_uq=m_w_uq, g_kv_lora=m_g_kv_lora, w_uk=m_w_uk,
                w_uv=m_w_uv, w_attn_out=m_w_attn_out, w_out=m_w_out, g_ffn=m_g_ffn, w_ffn_gate=m_w_ffn_gate,
                w_ffn_up=m_w_ffn_up, w_ffn_down=m_w_ffn_down, g_final=m_g_final)
    v_in = dict(meta_tokens=v_meta_tokens, g_mix=v_g_mix, w_in=v_w_in, b_glu=v_b_glu, b_gate=v_b_gate, w_dw=v_w_dw,
                b_dw=v_b_dw, g_conv_ln=v_g_conv_ln, b_conv_ln=v_b_conv_ln, w_conv_out=v_w_conv_out,
                b_conv_out=v_b_conv_out, g_q_lora=v_g_q_lora, w_uq=v_w_uq, g_kv_lora=v_g_kv_lora, w_uk=v_w_uk,
                w_uv=v_w_uv, w_attn_out=v_w_attn_out, w_out=v_w_out, g_ffn=v_g_ffn, w_ffn_gate=v_w_ffn_gate,
                w_ffn_up=v_w_ffn_up, w_ffn_down=v_w_ffn_down, g_final=v_g_final)
    names = list(weights)

    seq, D = x.shape[1], x.shape[2]
    C = w_conv_out.shape[2]
    R = g_q_lora.shape[1]
    H = w_uv.shape[2] * N_CHIPS // V_HEAD
    FB = w_ffn_gate.shape[2]
    FF = FB * N_CHIPS
    length = N_META + seq
    T = -(-length // BLOCK_Q) * BLOCK_Q
    scale = (QK_NOPE + QK_ROPE) ** -0.5
    assert C == D and w_in.shape[2] * N_CHIPS == 2 * C + 2 * R + QK_ROPE + 2 * D

    xi, yi, ci = _place()
    chip = (2 * xi + yi).astype(jnp.int32)
    half_arr = ci.astype(jnp.int32).reshape(1)

    tr = _pick(T, (272, 256, 128))
    trw = _pick(T, (136, 128))
    tmh = T // 2 if (T // 2) % 16 == 0 else T
    tmw = _pick(D, (1024, 512))
    nch = 4
    rc = _pick(T, (136, 128))
    tcv = 256

    def tn_of(n, pref=512):
        return _pick(n, (pref, 384, 256, 128))

    grp_a = ["w_in"]
    grp_b = ["w_conv_out", "w_uq", "w_uk", "w_uv", "w_attn_out", "w_out"]
    grp_c = ["w_ffn_gate", "w_ffn_up", "w_ffn_down"]
    big = grp_a + grp_b + grp_c
    axes = {nm: 0 for nm in big}
    axes["w_in"] = 1
    shard = {nm: weights[nm][0] for nm in big}
    shard["w_in"] = jnp.swapaxes(w_in[0], 0, 1)

    def gather_begin(nms, srcs, after, tag):
        bufs = []
        for w in srcs:
            wb = w.astype(BF)
            bufs.append(lax.dynamic_update_slice(lax.empty((N_CHIPS,) + wb.shape, BF), wb[None], (chip, 0, 0)))
        ax = [axes[nm] for nm in nms]
        bufs, ssem, rsem, tok = copies_start(bufs, gather_plan(ax), 3 * len(nms), after, name=f"gather_{tag}_start")
        return (nms, ax, bufs, ssem, rsem), tok

    def gather_end(state, after, tag):
        nms, ax, bufs, ssem, rsem = state
        bufs = copies_wait(bufs, gather_plan(ax), ssem, rsem, after, name=f"gather_{tag}_wait")
        return dict(zip(nms, forward_pair(bufs, ax, name=f"gather_{tag}_forward")))

    tiny = gather_devices(jnp.concatenate([meta_tokens, jnp.pad(w_dw[0], ((0, 1), (0, 0)))], axis=0),
                          name="gather_meta_w_dw")
    meta_full = jnp.concatenate([tiny[2 * k, :N_META] for k in range(N_CHIPS)], axis=1)
    w_dw_full = jnp.concatenate([tiny[2 * k, N_META:] for k in range(N_CHIPS)], axis=1)

    st_ga, tok_ga = gather_begin(grp_a, [shard["w_in"]], tiny, "a")
    tok_ga, held = lax.optimization_barrier((tok_ga, (x, loss_target, [shard[nm] for nm in grp_b + grp_c])))
    x_l, tgt_l, shard_bc = held
    h0 = jnp.concatenate([meta_full, x_l[0], jnp.zeros((T - length, D), F32)], axis=0)
    tgt = jnp.pad(tgt_l[0], ((N_META, T - length), (0, 0)))
    cos, sin = _rope_tables(T)

    G = gather_end(st_ga, h0, "a")
    st_gb, tok_gb = gather_begin(grp_b, shard_bc[:len(grp_b)], G["w_in"], "b")
    st_gc, tok_gc = gather_begin(grp_c, shard_bc[len(grp_b):], tok_gb, "c")

    WT_in = G["w_in"].reshape(N_CHIPS * G["w_in"].shape[1], D)
    o1, o2 = 2 * C, 2 * C + 2 * R + QK_ROPE
    WT_glu = WT_in[:o1]
    WT_small = jnp.pad(WT_in[o1:o2], ((0, 128 - QK_ROPE), (0, 0)))
    WT_gate = WT_in[o2:]

    u = rms_fwd(h0, g_mix, tr=tr, name="rms_mix", deps=[tok_gb, tok_gc])
    z_glu = mm(u, WT_glu, mode="nt", tm=T, tn=tn_of(2 * C), tk=D, out_dtype=F32, name="mm_z_glu")
    z_small = mm(u, WT_small, mode="nt", tm=T, tn=tn_of(2 * R + 128, 384), tk=D, out_dtype=F32, name="mm_z_small")
    z_gate = mm(u, WT_gate, mode="nt", tm=T, tn=tn_of(2 * D), tk=D, out_dtype=F32, name="mm_z_gate")

    c0 = glu_fwd(z_glu, b_glu, tr=tr, name="glu_fwd")
    c1 = dwconv_fwd(c0, w_dw_full, b_dw, tc=tcv, rc=rc, name="dwconv_fwd")
    c3 = ln_silu_fwd(c1, g_conv_ln, b_conv_ln, tr=tr, name="ln_silu_fwd")

    def cols_full(a):
        return jnp.concatenate([a[k] for k in range(N_CHIPS)], axis=1)

    G.update(gather_end(st_gb, c1, "b"))
    W_co = G["w_conv_out"].reshape(C, D)
    W_uq = cols_full(G["w_uq"]).reshape(R, H, QK_NOPE + QK_ROPE)
    Wq_p = jnp.pad(W_uq, ((0, 0), (0, 0), (0, HEAD_PAD - QK_NOPE - QK_ROPE))).reshape(R, H * HEAD_PAD)
    W_uk = cols_full(G["w_uk"])
    W_uv = cols_full(G["w_uv"])
    W_ao = G["w_attn_out"].reshape(H * V_HEAD, D)
    W_o = G["w_out"].reshape(D, D)

    y_conv = mm(c3, W_co, mode="nn", tm=T, tn=tn_of(D), tk=C, out_dtype=F32, name="mm_y_conv")

    cq, ckv, kr = lora_norm_fwd(z_small, g_q_lora, g_kv_lora, cos, sin, tr=tr, name="lora_norm_fwd")
    q = q_proj(cq, Wq_p, cos, sin, scale=scale, name="q_proj")
    kk = k_proj(ckv, W_uk, kr, name="k_proj")
    vv = mm(ckv, W_uv, mode="nn", tm=T, tn=tn_of(H * V_HEAD), tk=R, out_dtype=BF, name="mm_v")
    o_attn, lse = attn_fwd(q, kk, vv, nch=nch, name="attn_fwd")
    y_attn = mm(o_attn, W_ao, mode="nn", tm=T, tn=tn_of(D), tk=H * V_HEAD, out_dtype=F32, name="mm_y_attn")

    mix = mix_fwd(z_gate, y_conv, y_attn, b_gate, b_conv_out, tr=tr, name="mix_fwd")
    h1 = mm(mix, W_o, mode="nn", tm=T, tn=tn_of(D), tk=D, out_dtype=F32, name="mm_h1", res=h0)

    hn = rms_fwd(h1, g_ffn, tr=tr, name="rms_ffn")

    G.update(gather_end(st_gc, hn, "c"))
    W_fg, W_fu = G["w_ffn_gate"], G["w_ffn_up"]
    W_fd = G["w_ffn_down"].reshape(FF, D)

    fa = mm(hn, W_fg, mode="nn", tm=tmh, tn=FB, tk=D, out_dtype=F32, name="mm_ffn_gate")
    fb = mm(hn, W_fu, mode="nn", tm=tmh, tn=FB, tk=D, out_dtype=F32, name="mm_ffn_up")
    f = swiglu_fwd(fa, fb, tr=trw, name="swiglu_fwd")
    h2 = mm(f, W_fd, mode="nn", tm=tmh, tn=tn_of(D, 256), tk=FF, out_dtype=F32, name="mm_h2", res=h1)

    place_arr = jnp.stack([chip, ci.astype(jnp.int32)])

    def col_blocks(a):
        r, cfull = a.shape
        return a.reshape(r, N_CHIPS, cfull // N_CHIPS).transpose(1, 0, 2)

    def rs_begin(nms, gls, after, tag):
        ax = [axes[nm] for nm in nms]
        lands = []
        for g, a in zip(gls, ax):
            _, r, cfull = g.shape
            lands.append(lax.empty((N_CHIPS, r // 2, cfull) if a == 0 else (N_CHIPS, r, cfull // 2), BF))
        arrs, ssem, rsem, tok = copies_start(list(gls) + lands, exchange_plan(ax), N_CHIPS * len(nms), after,
                                             name=f"rs_{tag}_exchange_start")
        return (nms, ax, arrs, ssem, rsem), tok

    def rs_middle(state, after, tag):
        nms, ax, arrs, ssem, rsem = state
        n = len(nms)
        arrs = copies_wait(arrs, exchange_plan(ax), ssem, rsem, after, name=f"rs_{tag}_exchange_wait")
        pair = [pair_sum(g, r, half_arr, a, name="rs_pair_sum_" + nm)
                for nm, g, r, a in zip(nms, arrs[:n], arrs[n:], ax)]
        lands = [lax.empty((3,) + p.shape[1:], BF) for p in pair]
        arrs, ssem, rsem, tok = copies_start(pair + lands, scatter_plan(n), 3 * n, pair[-1],
                                             name=f"rs_{tag}_scatter_start")
        return (nms, ax, arrs, ssem, rsem), tok

    def rs_end(state, after, tag):
        nms, ax, arrs, ssem, rsem = state
        n = len(nms)
        arrs = copies_wait(arrs, scatter_plan(n), ssem, rsem, after, name=f"rs_{tag}_scatter_wait")
        return [chip_sum(s, r, place_arr, a, name="rs_chip_sum_" + nm)
                for nm, s, r, a in zip(nms, arrs[:n], arrs[n:], ax)]

    d_h2, d_h2b, loss_p, d_g_final = final_loss(h2, tgt, g_final.reshape(1, D), seq=seq, tr=tr, name="final_loss")
    loss = lax.psum(loss_p[0, 0], ("x", "y", "c"))

    d_f = mm(d_h2b, W_fd, mode="nt", tm=tmh, tn=FB, tk=D, out_dtype=F32, name="mm_d_f")
    dW_fd = mm(f, d_h2b, mode="tn", tm=FB, tn=tn_of(D), tk=T, out_dtype=BF, name="mm_dw_ffn_down")
    d_a, d_b = swiglu_bwd(fa, fb, d_f, tr=trw, name="swiglu_bwd")
    dW_fg = mm(hn, d_a, mode="tn", tm=tmw, tn=FB, tk=T, out_dtype=BF, name="mm_dw_ffn_gate", out_blocks=N_CHIPS)
    dW_fu = mm(hn, d_b, mode="tn", tm=tmw, tn=FB, tk=T, out_dtype=BF, name="mm_dw_ffn_up", out_blocks=N_CHIPS)
    st_c, tok_rc = rs_begin(grp_c, [dW_fg, dW_fu, dW_fd.reshape(N_CHIPS, FB, D)], dW_fu, "c")
    d_hn = mm(d_a, W_fg, mode="nt", tm=tmh, tn=tn_of(D), tk=FB, out_dtype=F32, name="mm_d_hn_gate", deps=[tok_rc])
    st_c, tok_rc = rs_middle(st_c, d_hn, "c")
    d_hn = mm(d_b, W_fu, mode="nt", tm=tmh, tn=tn_of(D), tk=FB, out_dtype=F32, name="mm_d_hn_up", res=d_hn,
              deps=[tok_rc])
    d_h1, d_h1b, d_g_ffn = rms_bwd(d_hn, h1, g_ffn, d_h2, tr=tr, name="rms_ffn_bwd")

    d_mix = mm(d_h1b, W_o, mode="nt", tm=T, tn=tn_of(D), tk=D, out_dtype=F32, name="mm_d_mix")
    dW_o = mm(mix, d_h1b, mode="tn", tm=tmw, tn=tn_of(D), tk=T, out_dtype=BF, name="mm_dw_out")
    d_yc, d_ya, d_zgate, d_b_gate, d_b_co = mix_bwd(z_gate, y_conv, y_attn, d_mix, b_gate, b_conv_out, tr=tr,
                                                    name="mix_bwd")

    dW_co = mm(c3, d_yc, mode="tn", tm=tmw, tn=tn_of(D), tk=T, out_dtype=BF, name="mm_dw_conv_out")
    d_c3 = mm(d_yc, W_co, mode="nt", tm=T, tn=tn_of(C), tk=D, out_dtype=F32, name="mm_d_c3")
    d_c1, d_g_ln, d_b_ln, d_b_dw = ln_silu_bwd(c1, d_c3, g_conv_ln, b_conv_ln, tr=tr, name="ln_silu_bwd")
    d_c0, d_w_dw = dwconv_bwd(c0, d_c1, w_dw_full, tc=tcv, rc=rc, name="dwconv_bwd")
    d_zglu, d_b_glu = glu_bwd(z_glu, b_glu, d_c0, tr=tr, name="glu_bwd")

    dW_ao = mm(o_attn, d_ya, mode="tn", tm=_pick(H * V_HEAD, (1024, 512)), tn=tn_of(D), tk=T, out_dtype=BF, name="mm_dw_attn_out")
    d_o = mm(d_ya, W_ao, mode="nt", tm=T, tn=tn_of(H * V_HEAD), tk=D, out_dtype=BF, name="mm_d_o")
    d_q2, d_k2, d_v, d_kr = attn_bwd(q, kk, vv, o_attn, d_o, lse, cos, sin, scale=scale, nch=nch, name="attn_bwd")

    dWq_p = mm(cq, d_q2, mode="tn", tm=R, tn=tn_of(H * HEAD_PAD), tk=T, out_dtype=BF, name="mm_dw_uq")
    d_cq = mm(d_q2, Wq_p, mode="nt", tm=T, tn=R, tk=_pick(H * HEAD_PAD, (1024,)), out_dtype=F32, name="mm_d_cq")
    Wk_p = jnp.pad(W_uk.reshape(R, H, QK_NOPE), ((0, 0), (0, 0), (0, HEAD_PAD - QK_NOPE))).reshape(R, H * HEAD_PAD)
    dWk_p = mm(ckv, d_k2, mode="tn", tm=R, tn=tn_of(H * HEAD_PAD), tk=T, out_dtype=BF, name="mm_dw_uk")
    dW_uv = mm(ckv, d_v, mode="tn", tm=R, tn=tn_of(H * V_HEAD), tk=T, out_dtype=BF, name="mm_dw_uv")
    red_c = rs_end(st_c, d_v, "c")
    dW_uq = dWq_p.reshape(R, H, HEAD_PAD)[:, :, :QK_NOPE + QK_ROPE].reshape(R, H * (QK_NOPE + QK_ROPE))
    dW_uk = dWk_p.reshape(R, H, HEAD_PAD)[:, :, :QK_NOPE].reshape(R, H * QK_NOPE)
    st_b, tok_rb = rs_begin(grp_b, [dW_co.reshape(N_CHIPS, C // N_CHIPS, D), col_blocks(dW_uq), col_blocks(dW_uk),
                                    col_blocks(dW_uv), dW_ao.reshape(N_CHIPS, H * V_HEAD // N_CHIPS, D),
                                    dW_o.reshape(N_CHIPS, D // N_CHIPS, D)], red_c[-1], "b")
    d_ckv = mm(d_k2, Wk_p, mode="nt", tm=T, tn=R, tk=_pick(H * HEAD_PAD, (1024,)), out_dtype=F32, name="mm_d_ckv_k",
               deps=[tok_rb])
    d_ckv = mm(d_v, W_uv, mode="nt", tm=T, tn=R, tk=_pick(H * V_HEAD, (1024,)), out_dtype=F32, name="mm_d_ckv_v",
               res=d_ckv)
    d_zsmall, d_g_q, d_g_kv = lora_norm_bwd(z_small, d_cq, d_ckv, d_kr, cos, sin, g_q_lora, g_kv_lora, tr=tr,
                                            name="lora_norm_bwd")
    st_b, tok_rb = rs_middle(st_b, d_zsmall, "b")

    tmi = _pick(2 * C, (1024, 512))
    dWT_glu = mm(d_zglu, u, mode="tn", tm=tmi, tn=tn_of(D), tk=T, out_dtype=BF, name="mm_dw_glu", deps=[tok_rb])
    dWT_small = mm(d_zsmall, u, mode="tn", tm=2 * R + 128, tn=tn_of(D), tk=T, out_dtype=BF, name="mm_dw_small")
    dWT_gate = mm(d_zgate, u, mode="tn", tm=tmi, tn=tn_of(D), tk=T, out_dtype=BF, name="mm_dw_gate")
    dWT_in = jnp.concatenate([dWT_glu, dWT_small[:2 * R + QK_ROPE], dWT_gate], axis=0)
    red_b = rs_end(st_b, dWT_in, "b")
    st_a, tok_ra = rs_begin(grp_a, [dWT_in.reshape(N_CHIPS, dWT_in.shape[0] // N_CHIPS, D)], red_b[-1], "a")
    d_u = mm(d_zglu, WT_glu, mode="nn", tm=T, tn=tn_of(D), tk=_pick(2 * C, (2048, 1024)), out_dtype=F32, name="mm_d_u_glu",
             deps=[tok_ra])
    st_a, tok_ra = rs_middle(st_a, d_u, "a")
    d_u = mm(d_zsmall, WT_small, mode="nn", tm=T, tn=tn_of(D), tk=2 * R + 128, out_dtype=F32, name="mm_d_u_small",
             res=d_u, deps=[tok_ra])
    d_u = mm(d_zgate, WT_gate, mode="nn", tm=T, tn=tn_of(D), tk=_pick(2 * D, (2048, 1024)), out_dtype=F32,
             name="mm_d_u_gate", res=d_u)
    d_h0, _, d_g_mix = rms_bwd(d_u, h0, g_mix, d_h1, tr=tr, name="rms_mix_bwd")

    grad_x = d_h0[N_META:length][None]

    small = ["g_mix", "b_glu", "b_gate", "b_dw", "g_conv_ln", "b_conv_ln", "b_conv_out", "g_q_lora", "g_kv_lora",
             "g_ffn", "g_final", "w_dw", "meta_tokens"]
    sgrads = dict(g_mix=d_g_mix, b_glu=d_b_glu, b_gate=d_b_gate, b_dw=d_b_dw, g_conv_ln=d_g_ln, b_conv_ln=d_b_ln,
                  b_conv_out=d_b_co, g_q_lora=d_g_q, g_kv_lora=d_g_kv, g_ffn=d_g_ffn, g_final=d_g_final,
                  w_dw=d_w_dw[:CONV_WIDTH], meta_tokens=d_h0[:N_META])
    sizes = [int(np.prod(sgrads[nm].shape)) for nm in small]
    packed = jnp.concatenate([sgrads[nm].reshape(-1) for nm in small]).reshape(-1, 128)
    sm_arrs, sm_ss, sm_rs, tok_sm = copies_start([packed, lax.empty((N_DEV,) + packed.shape, F32)], devices_plan,
                                                 N_DEV - 1, packed, name="gather_small_start")

    grads, delta, new_m, new_v, delta_2d = {}, {}, {}, {}, {}

    def adamw_2d(nm, w2, g2, m2, v2, back):
        d_, m_, v_ = adamw(w2, g2, m2, v2, name="adamw_" + nm)
        delta_2d[nm] = d_
        grads[nm], delta[nm], new_m[nm], new_v[nm] = back(g2), back(d_), back(m_), back(v_)

    reduced = dict(zip(grp_b + grp_c, share_halves(red_b + red_c, [axes[nm] for nm in grp_b + grp_c], tok_sm,
                                                   name="rs_share_halves_bc")))
    for nm in grp_b + grp_c + grp_a:
        if nm == grp_a[0]:
            red_a = rs_end(st_a, delta_2d[grp_c[-1]], "a")
            reduced.update(zip(grp_a, share_halves(red_a, [axes[nm] for nm in grp_a], tok_sm,
                                                   name="rs_share_halves_a")))
        shp = weights[nm].shape
        if axes[nm] == 1:
            adamw_2d(nm, shard[nm], reduced[nm], jnp.swapaxes(m_in[nm][0], 0, 1), jnp.swapaxes(v_in[nm][0], 0, 1),
                     lambda a: jnp.swapaxes(a, 0, 1)[None])
        else:
            two = (shp[-2], shp[-1])
            adamw_2d(nm, weights[nm].reshape(two), reduced[nm], m_in[nm].reshape(two), v_in[nm].reshape(two),
                     lambda a, shp=shp: a.reshape(shp))

    sm_arrs = copies_wait(sm_arrs, devices_plan, sm_ss, sm_rs, delta_2d[grp_a[0]], name="gather_small_wait")
    me = (4 * xi + 2 * yi + ci).astype(jnp.int32)
    parts = lax.dynamic_update_slice(sm_arrs[1], sm_arrs[0][None], (me, 0, 0))
    summed = sum_devices(parts, name="sum_small_grads").reshape(-1)
    offs = np.concatenate([[0], np.cumsum(sizes)])
    sfull = {nm: summed[int(offs[i]):int(offs[i + 1])].reshape(sgrads[nm].shape) for i, nm in enumerate(small)}
    for nm in small:
        gfull = sfull[nm]
        if nm == "w_dw":
            cb = C // N_CHIPS
            grads[nm] = lax.dynamic_slice(gfull, (0, chip * cb), (CONV_WIDTH, cb))[None]
        elif nm == "meta_tokens":
            cb = D // N_CHIPS
            grads[nm] = lax.dynamic_slice(gfull, (0, chip * cb), (N_META, cb))
        else:
            grads[nm] = gfull.reshape(weights[nm].shape)

    rep = [nm for nm in small if nm not in ("w_dw", "meta_tokens")]

    def pack(d):
        return jnp.concatenate([d[nm].reshape(-1) for nm in rep]).reshape(-1, 128)

    pd, pm, pv = adamw(pack(weights), pack(grads), pack(m_in), pack(v_in), name="adamw_small")
    rsz = [int(np.prod(weights[nm].shape)) for nm in rep]
    roff = np.concatenate([[0], np.cumsum(rsz)])
    for i, nm in enumerate(rep):
        sl = slice(int(roff[i]), int(roff[i + 1]))
        delta[nm] = pd.reshape(-1)[sl].reshape(weights[nm].shape)
        new_m[nm] = pm.reshape(-1)[sl].reshape(weights[nm].shape)
        new_v[nm] = pv.reshape(-1)[sl].reshape(weights[nm].shape)
    for nm in ["w_dw", "meta_tokens"]:
        shp = weights[nm].shape
        two = (shp[-2], shp[-1])
        adamw_2d(nm, weights[nm].reshape(two), grads[nm].reshape(two), m_in[nm].reshape(two), v_in[nm].reshape(two),
                 lambda a, shp=shp: a.reshape(shp))

    return (loss, grad_x, *[grads[nm] for nm in names], *[delta[nm] for nm in names],
            *[new_m[nm] for nm in names], *[new_v[nm] for nm in names])
```

```python
import functools

import numpy as np
import jax
import jax.numpy as jnp
from jax import lax
from jax.experimental import pallas as pl
from jax.experimental.pallas import tpu as pltpu

F32 = jnp.float32
BF = jnp.bfloat16
MESH = pl.DeviceIdType.MESH

N_META = 16
BLOCK_Q = 128
CONV_WIDTH = 31
CONV_PAD = 32
QK_NOPE = 128
QK_ROPE = 64
V_HEAD = 128
HEAD_PAD = 256
ROPE_THETA = 10000.0
EPS = 1e-6
ADAM_LR = 0.001
ADAM_B1 = 0.9
ADAM_B2 = 0.999
ADAM_EPS = 1e-08
ADAM_WD = 0.01
ADAM_STEP = 10
VMEM_LIMIT = 56 * 1024 * 1024
N_CHIPS = 4
N_DEV = 8


def _cparams(sem):
    return pltpu.CompilerParams(dimension_semantics=sem, vmem_limit_bytes=VMEM_LIMIT)


def mm(a, b, *, mode, tm, tn, tk, out_dtype, name, res=None, out_blocks=None, deps=()):
    b3 = b.ndim == 3
    if mode == "nn":
        M, K = a.shape
        N = b.shape[0] * b.shape[2] if b3 else b.shape[1]
        a_spec = pl.BlockSpec((tm, tk), lambda i, j, k: (i, k))
        if b3:
            per = b.shape[2] // tn
            b_spec = pl.BlockSpec((None, tk, tn), lambda i, j, k: (j // per, k, j % per))
        else:
            b_spec = pl.BlockSpec((tk, tn), lambda i, j, k: (k, j))
        dims = (((1,), (0,)), ((), ()))
    elif mode == "nt":
        M, K = a.shape
        N = b.shape[1] if b3 else b.shape[0]
        a_spec = pl.BlockSpec((tm, tk), lambda i, j, k: (i, k))
        if b3:
            per = b.shape[2] // tk
            b_spec = pl.BlockSpec((None, tn, tk), lambda i, j, k: (k // per, j, k % per))
        else:
            b_spec = pl.BlockSpec((tn, tk), lambda i, j, k: (j, k))
        dims = (((1,), (1,)), ((), ()))
    else:
        K, M = a.shape
        N = b.shape[1]
        a_spec = pl.BlockSpec((tk, tm), lambda i, j, k: (k, i))
        b_spec = pl.BlockSpec((tk, tn), lambda i, j, k: (k, j))
        dims = (((0,), (0,)), ((), ()))
    gm, gn, gk = M // tm, N // tn, K // tk
    assert gm * tm == M and gn * tn == N and gk * tk == K, (name, a.shape, b.shape, tm, tn, tk)
    if out_blocks is None:
        o_spec = pl.BlockSpec((tm, tn), lambda i, j, k: (i, j))
        o_shape = (M, N)
    else:
        nbw = N // out_blocks
        per_o = nbw // tn
        assert per_o * tn == nbw
        o_spec = pl.BlockSpec((None, tm, tn), lambda i, j, k: (j // per_o, i, j % per_o))
        o_shape = (out_blocks, M, nbw)
    has_res = res is not None

    def body(*refs):
        a_ref, b_ref = refs[0], refs[1]
        r_ref = refs[2] if has_res else None
        o_ref = refs[2 + has_res + len(deps)]
        acc = refs[-1]
        p = lax.dot_general(a_ref[...], b_ref[...], dims, preferred_element_type=F32)

        def finish(v):
            if has_res:
                v = v + r_ref[...]
            o_ref[...] = v.astype(o_ref.dtype)

        if gk == 1:
            finish(p)
        else:
            k = pl.program_id(2)

            @pl.when(k == 0)
            def _():
                acc[...] = p

            @pl.when(k > 0)
            def _():
                acc[...] += p

            @pl.when(k == gk - 1)
            def _():
                finish(acc[...])

    in_specs = [a_spec, b_spec]
    args = [a, b]
    if has_res:
        in_specs.append(pl.BlockSpec((tm, tn), lambda i, j, k: (i, j)))
        args.append(res)
    in_specs += [pl.BlockSpec(memory_space=pl.ANY)] * len(deps)
    args += list(deps)
    return pl.pallas_call(
        body, name=name, grid=(gm, gn, gk), in_specs=in_specs, out_specs=o_spec,
        out_shape=jax.ShapeDtypeStruct(o_shape, out_dtype),
        scratch_shapes=[pltpu.VMEM((tm, tn), F32)] if gk > 1 else [],
        compiler_params=_cparams(("parallel", "parallel", "arbitrary")),
    )(*args)


def rowwise(body, row_ins, full_ins, row_outs, acc_outs, *, tr, name, deps=()):
    T = row_ins[0].shape[0]
    assert T % tr == 0, (name, T, tr)
    n_ri, n_fi, n_ro = len(row_ins), len(full_ins), len(row_outs)
    n_in = n_ri + n_fi + len(deps)

    def kern(*refs):
        body(pl.program_id(0), refs[:n_ri], refs[n_ri:n_ri + n_fi], refs[n_in:n_in + n_ro], refs[n_in + n_ro:])

    in_specs = [pl.BlockSpec((tr, a.shape[1]), lambda i: (i, 0)) for a in row_ins]
    in_specs += [pl.BlockSpec(a.shape, lambda i: (0, 0)) for a in full_ins]
    in_specs += [pl.BlockSpec(memory_space=pl.ANY)] * len(deps)
    out_specs = [pl.BlockSpec((tr, c), lambda i: (i, 0)) for c, _ in row_outs]
    out_specs += [pl.BlockSpec(s, lambda i: (0, 0)) for s in acc_outs]
    out_shape = [jax.ShapeDtypeStruct((T, c), d) for c, d in row_outs]
    out_shape += [jax.ShapeDtypeStruct(s, F32) for s in acc_outs]
    return pl.pallas_call(
        kern, name=name, grid=(T // tr,), in_specs=in_specs, out_specs=out_specs, out_shape=out_shape,
        compiler_params=_cparams(("arbitrary",)),
    )(*row_ins, *full_ins, *deps)


def _acc(step, ref, val):
    @pl.when(step == 0)
    def _():
        ref[...] = val

    @pl.when(step > 0)
    def _():
        ref[...] += val


def _colsum(x):
    return jnp.sum(x, axis=0, keepdims=True)


def _sigmoid(x):
    return 1.0 / (1.0 + jnp.exp(-x))


def _rot_half(v):
    lane = lax.broadcasted_iota(jnp.int32, v.shape, 1)
    lo = -pltpu.roll(v, 96, 1)
    hi = pltpu.roll(v, 32, 1)
    return jnp.where(lane < 32, lo, jnp.where(lane < 64, hi, 0.0))


def rms_fwd(h, g, *, tr, name, deps=()):
    def body(step, ri, fi, ro, ao):
        x = ri[0][...]
        r = lax.rsqrt(jnp.mean(x * x, axis=-1, keepdims=True) + EPS)
        ro[0][...] = ((x * r) * fi[0][...]).astype(BF)

    return rowwise(body, [h], [g], [(h.shape[1], BF)], [], tr=tr, name=name, deps=deps)[0]


def rms_bwd(dy, h, g, dres, *, tr, name, deps=()):
    D = h.shape[1]

    def body(step, ri, fi, ro, ao):
        d, x, dr = ri[0][...], ri[1][...], ri[2][...]
        r = lax.rsqrt(jnp.mean(x * x, axis=-1, keepdims=True) + EPS)
        n = x * r
        _acc(step, ao[0], _colsum(d * n))
        dn = d * fi[0][...]
        dh = r * (dn - n * jnp.mean(dn * n, axis=-1, keepdims=True)) + dr
        ro[0][...] = dh
        ro[1][...] = dh.astype(BF)

    return rowwise(body, [dy, h, dres], [g], [(D, F32), (D, BF)], [(1, D)], tr=tr, name=name, deps=deps)


def glu_fwd(z, b, *, tr, name):
    C = z.shape[1] // 2

    def body(step, ri, fi, ro, ao):
        zz = ri[0][...] + fi[0][...]
        ro[0][...] = zz[:, :C] * _sigmoid(zz[:, C:])

    return rowwise(body, [z], [b], [(C, F32)], [], tr=tr, name=name)[0]


def glu_bwd(z, b, dc, *, tr, name):
    C = z.shape[1] // 2

    def body(step, ri, fi, ro, ao):
        zz = ri[0][...] + fi[0][...]
        d = ri[1][...]
        za, sg = zz[:, :C], _sigmoid(zz[:, C:])
        dza = d * sg
        dzb = d * za * sg * (1.0 - sg)
        ro[0][:, :C] = dza.astype(BF)
        ro[0][:, C:] = dzb.astype(BF)
        _acc(step, ao[0], _colsum(dza))
        _acc(step, ao[1], _colsum(dzb))

    dz, da, db = rowwise(body, [z, dc], [b], [(2 * C, BF)], [(1, C), (1, C)], tr=tr, name=name)
    return dz, jnp.concatenate([da, db], axis=1)


def ln_silu_fwd(c1, g, b, *, tr, name):
    def body(step, ri, fi, ro, ao):
        x = ri[0][...]
        mu = jnp.mean(x, axis=-1, keepdims=True)
        xc = x - mu
        rstd = lax.rsqrt(jnp.mean(xc * xc, axis=-1, keepdims=True) + EPS)
        y = (xc * rstd) * fi[0][...] + fi[1][...]
        ro[0][...] = (y * _sigmoid(y)).astype(BF)

    return rowwise(body, [c1], [g, b], [(c1.shape[1], BF)], [], tr=tr, name=name)[0]


def ln_silu_bwd(c1, dc3, g, b, *, tr, name):
    C = c1.shape[1]

    def body(step, ri, fi, ro, ao):
        x, d3 = ri[0][...], ri[1][...]
        mu = jnp.mean(x, axis=-1, keepdims=True)
        xc = x - mu
        rstd = lax.rsqrt(jnp.mean(xc * xc, axis=-1, keepdims=True) + EPS)
        n = xc * rstd
        y = n * fi[0][...] + fi[1][...]
        sg = _sigmoid(y)
        dy = d3 * (sg * (1.0 + y * (1.0 - sg)))
        _acc(step, ao[0], _colsum(dy * n))
        _acc(step, ao[1], _colsum(dy))
        dn = dy * fi[0][...]
        dx = rstd * (dn - jnp.mean(dn, axis=-1, keepdims=True) - n * jnp.mean(dn * n, axis=-1, keepdims=True))
        ro[0][...] = dx
        _acc(step, ao[2], _colsum(dx))

    return rowwise(body, [c1, dc3], [g, b], [(C, F32)], [(1, C), (1, C), (1, C)], tr=tr, name=name)


def lora_norm_fwd(zs, gq, gkv, cos, sin, *, tr, name):
    R = gq.shape[1]

    def body(step, ri, fi, ro, ao):
        z = ri[0][...]
        for o, gi in ((0, 0), (1, 1)):
            x = z[:, o * R:(o + 1) * R]
            r = lax.rsqrt(jnp.mean(x * x, axis=-1, keepdims=True) + EPS)
            ro[o][...] = ((x * r) * fi[gi][...]).astype(BF)
        kr = z[:, 2 * R:2 * R + 128]
        ro[2][...] = kr * ri[1][...] + _rot_half(kr) * ri[2][...]

    return rowwise(body, [zs, cos, sin], [gq, gkv], [(R, BF), (R, BF), (128, F32)], [], tr=tr, name=name)


def lora_norm_bwd(zs, dcq, dckv, dkr, cos, sin, gq, gkv, *, tr, name):
    R = gq.shape[1]

    def body(step, ri, fi, ro, ao):
        z = ri[0][...]
        for o in (0, 1):
            x = z[:, o * R:(o + 1) * R]
            d = ri[1 + o][...]
            r = lax.rsqrt(jnp.mean(x * x, axis=-1, keepdims=True) + EPS)
            n = x * r
            _acc(step, ao[o], _colsum(d * n))
            dn = d * fi[o][...]
            ro[0][:, o * R:(o + 1) * R] = (r * (dn - n * jnp.mean(dn * n, axis=-1, keepdims=True))).astype(BF)
        dk = ri[3][...]
        ro[0][:, 2 * R:2 * R + 128] = (dk * ri[4][...] - _rot_half(dk * ri[5][...])).astype(BF)

    return rowwise(body, [zs, dcq, dckv, dkr, cos, sin], [gq, gkv], [(2 * R + 128, BF)], [(1, R), (1, R)],
                   tr=tr, name=name)


def mix_fwd(zg, yc, ya, bg, bco, *, tr, name):
    D = yc.shape[1]

    def body(step, ri, fi, ro, ao):
        g = _sigmoid(ri[0][...] + fi[0][...])
        ro[0][...] = (g[:, :D] * (ri[1][...] + fi[1][...]) + g[:, D:] * ri[2][...]).astype(BF)

    return rowwise(body, [zg, yc, ya], [bg, bco], [(D, BF)], [], tr=tr, name=name)[0]


def mix_bwd(zg, yc, ya, dmix, bg, bco, *, tr, name):
    D = yc.shape[1]

    def body(step, ri, fi, ro, ao):
        g = _sigmoid(ri[0][...] + fi[0][...])
        gc, ga = g[:, :D], g[:, D:]
        ycv = ri[1][...] + fi[1][...]
        yav = ri[2][...]
        dm = ri[3][...]
        dyc = dm * gc
        ro[0][...] = dyc.astype(BF)
        ro[1][...] = (dm * ga).astype(BF)
        dzc = dm * ycv * gc * (1.0 - gc)
        dza = dm * yav * ga * (1.0 - ga)
        ro[2][:, :D] = dzc.astype(BF)
        ro[2][:, D:] = dza.astype(BF)
        _acc(step, ao[0], _colsum(dzc))
        _acc(step, ao[1], _colsum(dza))
        _acc(step, ao[2], _colsum(dyc))

    dyc, dya, dzg, dbc, dba, dbo = rowwise(body, [zg, yc, ya, dmix], [bg, bco], [(D, BF), (D, BF), (2 * D, BF)],
                                           [(1, D), (1, D), (1, D)], tr=tr, name=name)
    return dyc, dya, dzg, jnp.concatenate([dbc, dba], axis=1), dbo


def ffn_fwd(hn, wg, wu, *, tm, name):
    T, D = hn.shape
    nb, _, FB = wg.shape
    assert T % tm == 0

    def body(h_ref, g_ref, u_ref, a_ref, b_ref, f_ref):
        h = h_ref[...]
        a = jnp.dot(h, g_ref[...], preferred_element_type=F32)
        b = jnp.dot(h, u_ref[...], preferred_element_type=F32)
        a_ref[...] = a
        b_ref[...] = b
        f_ref[...] = (a * _sigmoid(a) * b).astype(BF)

    wspec = pl.BlockSpec((None, D, FB), lambda j, i: (j, 0, 0))
    ospec = pl.BlockSpec((tm, FB), lambda j, i: (i, j))
    return pl.pallas_call(
        body, name=name, grid=(nb, T // tm),
        in_specs=[pl.BlockSpec((tm, D), lambda j, i: (i, 0)), wspec, wspec], out_specs=[ospec, ospec, ospec],
        out_shape=[jax.ShapeDtypeStruct((T, nb * FB), F32), jax.ShapeDtypeStruct((T, nb * FB), F32),
                   jax.ShapeDtypeStruct((T, nb * FB), BF)],
        compiler_params=_cparams(("parallel", "parallel")),
    )(hn, wg, wu)


def ffn_dact(dh, wd, a, b, *, tm, name):
    T, D = dh.shape
    FF = wd.shape[0]
    FB = FF // N_CHIPS

    def body(dh_ref, w_ref, a_ref, b_ref, da_ref, db_ref):
        d = lax.dot_general(dh_ref[...], w_ref[...], (((1,), (1,)), ((), ())), preferred_element_type=F32)
        x, u = a_ref[...], b_ref[...]
        sg = _sigmoid(x)
        da_ref[...] = (d * u * (sg * (1.0 + x * (1.0 - sg)))).astype(BF)
        db_ref[...] = (d * (x * sg)).astype(BF)

    blk = pl.BlockSpec((tm, FB), lambda j, i: (i, j))
    return pl.pallas_call(
        body, name=name, grid=(N_CHIPS, T // tm),
        in_specs=[pl.BlockSpec((tm, D), lambda j, i: (i, 0)), pl.BlockSpec((FB, D), lambda j, i: (j, 0)), blk, blk],
        out_specs=[blk, blk],
        out_shape=[jax.ShapeDtypeStruct((T, FF), BF), jax.ShapeDtypeStruct((T, FF), BF)],
        compiler_params=_cparams(("parallel", "parallel")),
    )(dh, wd, a, b)


def ffn_dhn(da, db, wg, wu, *, tm, tn, name, deps=()):
    T, FF = da.shape
    nb, D, FB = wg.shape
    dims = (((1,), (1,)), ((), ()))

    def body(da_ref, db_ref, g_ref, u_ref, *rest):
        o_ref = rest[len(deps)]
        acc = None
        for k in range(nb):
            cols = slice(k * FB, (k + 1) * FB)
            p = lax.dot_general(da_ref[:, cols], g_ref[k], dims, preferred_element_type=F32)
            p = p + lax.dot_general(db_ref[:, cols], u_ref[k], dims, preferred_element_type=F32)
            acc = p if acc is None else acc + p
        o_ref[...] = acc

    aspec = pl.BlockSpec((tm, FF), lambda i, j: (i, 0))
    wspec = pl.BlockSpec((nb, tn, FB), lambda i, j: (0, j, 0))
    return pl.pallas_call(
        body, name=name, grid=(T // tm, D // tn),
        in_specs=[aspec, aspec, wspec, wspec] + [pl.BlockSpec(memory_space=pl.ANY)] * len(deps),
        out_specs=pl.BlockSpec((tm, tn), lambda i, j: (i, j)),
        out_shape=jax.ShapeDtypeStruct((T, D), F32),
        compiler_params=_cparams(("parallel", "parallel")),
    )(da, db, wg, wu, *deps)


def final_loss(h2, tgt, g, *, seq, tr, name):
    D = h2.shape[1]

    def body(step, ri, fi, ro, ao):
        x = ri[0][...]
        r = lax.rsqrt(jnp.mean(x * x, axis=-1, keepdims=True) + EPS)
        n = x * r
        gg = fi[0][...]
        row = lax.broadcasted_iota(jnp.int32, (tr, 1), 0) + step * tr
        live = jnp.logical_and(row >= N_META, row < N_META + seq)
        e = jnp.where(live, n * gg - ri[1][...], 0.0)
        _acc(step, ao[0], jnp.broadcast_to(0.5 * jnp.sum(jnp.mean(e * e, axis=-1, keepdims=True)), (1, 128)))
        dy = e * (1.0 / D)
        _acc(step, ao[1], _colsum(dy * n))
        dn = dy * gg
        dh = r * (dn - n * jnp.mean(dn * n, axis=-1, keepdims=True))
        ro[0][...] = dh
        ro[1][...] = dh.astype(BF)

    return rowwise(body, [h2, tgt], [g], [(D, F32), (D, BF)], [(1, 128), (1, D)], tr=tr, name=name)


def dwconv_fwd(x, w, b, *, tc, rc, name, deps=()):
    T, C = x.shape
    nchunk = T // rc
    assert nchunk * rc == T and C % tc == 0

    def body(x_ref, w_ref, b_ref, *rest):
        y_ref, xp_ref = rest[len(deps):]
        xp_ref[0:CONV_PAD, :] = jnp.zeros((CONV_PAD, tc), F32)
        xp_ref[CONV_PAD:CONV_PAD + T, :] = x_ref[...]
        off = CONV_PAD - (CONV_WIDTH - 1)
        for ci in range(nchunk):
            t0 = ci * rc
            acc = jnp.broadcast_to(b_ref[...], (rc, tc))
            for j in range(CONV_WIDTH):
                acc = acc + w_ref[j:j + 1, :] * xp_ref[t0 + off + j:t0 + off + j + rc, :]
            y_ref[t0:t0 + rc, :] = acc

    return pl.pallas_call(
        body, name=name, grid=(C // tc,),
        in_specs=[pl.BlockSpec((T, tc), lambda j: (0, j)), pl.BlockSpec((32, tc), lambda j: (0, j)),
                  pl.BlockSpec((1, tc), lambda j: (0, j))] + [pl.BlockSpec(memory_space=pl.ANY)] * len(deps),
        out_specs=pl.BlockSpec((T, tc), lambda j: (0, j)),
        out_shape=jax.ShapeDtypeStruct((T, C), F32),
        scratch_shapes=[pltpu.VMEM((T + CONV_PAD, tc), F32)],
        compiler_params=_cparams(("parallel",)),
    )(x, w, b, *deps)


def dwconv_bwd(x, dy, w, *, tc, rc, name):
    T, C = x.shape
    nchunk = T // rc
    assert nchunk * rc == T and C % tc == 0 and rc % 8 == 0

    def body(x_ref, dy_ref, w_ref, dx_ref, dw_ref, xp_ref, dp_ref):
        off = CONV_PAD - (CONV_WIDTH - 1)
        xp_ref[0:CONV_PAD, :] = jnp.zeros((CONV_PAD, tc), F32)
        xp_ref[CONV_PAD:CONV_PAD + T, :] = x_ref[...]
        dp_ref[0:T, :] = dy_ref[...]
        dp_ref[T:T + CONV_PAD, :] = jnp.zeros((CONV_PAD, tc), F32)
        for ci in range(nchunk):
            t0 = ci * rc
            acc = jnp.zeros((rc, tc), F32)
            for j in range(CONV_WIDTH):
                s = t0 + (CONV_WIDTH - 1) - j
                acc = acc + w_ref[j:j + 1, :] * dp_ref[s:s + rc, :]
            dx_ref[t0:t0 + rc, :] = acc
        dw_ref[...] = jnp.zeros((32, tc), F32)
        for j in range(CONV_WIDTH):
            acc = jnp.zeros((8, tc), F32)
            for ci in range(nchunk):
                t0 = ci * rc
                pr = dp_ref[t0:t0 + rc, :] * xp_ref[t0 + off + j:t0 + off + j + rc, :]
                acc = acc + jnp.sum(pr.reshape(rc // 8, 8, tc), axis=0)
            dw_ref[j:j + 1, :] = jnp.sum(acc, axis=0, keepdims=True)

    return pl.pallas_call(
        body, name=name, grid=(C // tc,),
        in_specs=[pl.BlockSpec((T, tc), lambda j: (0, j)), pl.BlockSpec((T, tc), lambda j: (0, j)),
                  pl.BlockSpec((32, tc), lambda j: (0, j))],
        out_specs=[pl.BlockSpec((T, tc), lambda j: (0, j)), pl.BlockSpec((32, tc), lambda j: (0, j))],
        out_shape=[jax.ShapeDtypeStruct((T, C), F32), jax.ShapeDtypeStruct((32, C), F32)],
        scratch_shapes=[pltpu.VMEM((T + CONV_PAD, tc), F32), pltpu.VMEM((T + CONV_PAD, tc), F32)],
        compiler_params=_cparams(("parallel",)),
    )(x, dy, w)


def q_proj(cq, wq, cos, sin, *, scale, name):
    T, R = cq.shape
    H = wq.shape[1] // HEAD_PAD

    def body(a_ref, b_ref, c_ref, s_ref, o_ref):
        p = jnp.dot(a_ref[...], b_ref[...], preferred_element_type=F32)
        o_ref[:, :QK_NOPE] = (p[:, :QK_NOPE] * scale).astype(BF)
        x = p[:, QK_NOPE:]
        o_ref[:, QK_NOPE:] = ((x * c_ref[...] + _rot_half(x) * s_ref[...]) * scale).astype(BF)

    return pl.pallas_call(
        body, name=name, grid=(H,),
        in_specs=[pl.BlockSpec((T, R), lambda h: (0, 0)), pl.BlockSpec((R, HEAD_PAD), lambda h: (0, h)),
                  pl.BlockSpec((T, 128), lambda h: (0, 0)), pl.BlockSpec((T, 128), lambda h: (0, 0))],
        out_specs=pl.BlockSpec((T, HEAD_PAD), lambda h: (0, h)),
        out_shape=jax.ShapeDtypeStruct((T, H * HEAD_PAD), BF),
        compiler_params=_cparams(("parallel",)),
    )(cq, wq, cos, sin)


def k_proj(ckv, wk, kr, *, name):
    T, R = ckv.shape
    H = wk.shape[1] // QK_NOPE

    def body(a_ref, b_ref, kr_ref, o_ref):
        p = jnp.dot(a_ref[...], b_ref[...], preferred_element_type=F32)
        o_ref[:, :QK_NOPE] = p.astype(BF)
        o_ref[:, QK_NOPE:] = kr_ref[...].astype(BF)

    return pl.pallas_call(
        body, name=name, grid=(H,),
        in_specs=[pl.BlockSpec((T, R), lambda h: (0, 0)), pl.BlockSpec((R, QK_NOPE), lambda h: (0, h)),
                  pl.BlockSpec((T, 128), lambda h: (0, 0))],
        out_specs=pl.BlockSpec((T, HEAD_PAD), lambda h: (0, h)),
        out_shape=jax.ShapeDtypeStruct((T, H * HEAD_PAD), BF),
        compiler_params=_cparams(("parallel",)),
    )(ckv, wk, kr)


def _causal_scores(q, k, q0):
    s = lax.dot_general(q, k, (((1,), (1,)), ((), ())), preferred_element_type=F32)
    row = lax.broadcasted_iota(jnp.int32, s.shape, 0) + q0
    col = lax.broadcasted_iota(jnp.int32, s.shape, 1)
    return jnp.where(col <= row, s, -1e30)


def attn_fwd(q, k, v, *, nch, name):
    T = q.shape[0]
    H = q.shape[1] // HEAD_PAD
    CH = T // nch
    assert CH * nch == T and CH % 16 == 0

    def body(q_ref, k_ref, v_ref, o_ref, lse_ref):
        for qi in range(nch):
            L = (qi + 1) * CH
            s = _causal_scores(q_ref[qi * CH:L, :], k_ref[0:L, :], qi * CH)
            m = jnp.max(s, axis=1, keepdims=True)
            p = jnp.exp(s - m)
            l = jnp.sum(p, axis=1, keepdims=True)
            o = jnp.dot(p.astype(BF), v_ref[0:L, :], preferred_element_type=F32)
            o_ref[qi * CH:L, :] = (o / l).astype(BF)
            lse_ref[qi * CH:L, :] = jnp.broadcast_to(m + jnp.log(l), (CH, V_HEAD))

    return pl.pallas_call(
        body, name=name, grid=(H,),
        in_specs=[pl.BlockSpec((T, HEAD_PAD), lambda h: (0, h)), pl.BlockSpec((T, HEAD_PAD), lambda h: (0, h)),
                  pl.BlockSpec((T, V_HEAD), lambda h: (0, h))],
        out_specs=[pl.BlockSpec((T, V_HEAD), lambda h: (0, h)), pl.BlockSpec((T, V_HEAD), lambda h: (0, h))],
        out_shape=[jax.ShapeDtypeStruct((T, H * V_HEAD), BF), jax.ShapeDtypeStruct((T, H * V_HEAD), F32)],
        compiler_params=_cparams(("parallel",)),
    )(q, k, v)


def attn_bwd(q, k, v, o, do, lse, cos, sin, *, scale, nch, name):
    T = q.shape[0]
    H = q.shape[1] // HEAD_PAD
    CH = T // nch
    tn_dims = (((0,), (0,)), ((), ()))
    nt_dims = (((1,), (1,)), ((), ()))

    def body(q_ref, k_ref, v_ref, o_ref, do_ref, lse_ref, c_ref, s_ref,
             dq_ref, dk_ref, dv_ref, dkr_ref, dk_acc, dv_acc):
        h = pl.program_id(0)
        dk_acc[...] = jnp.zeros_like(dk_acc)
        dv_acc[...] = jnp.zeros_like(dv_acc)
        for qi in range(nch):
            L = (qi + 1) * CH
            rows = slice(qi * CH, L)
            qc, doc = q_ref[rows, :], do_ref[rows, :]
            kc, vc = k_ref[0:L, :], v_ref[0:L, :]
            s = _causal_scores(qc, kc, qi * CH)
            p = jnp.exp(s - lse_ref[rows, 0:1])
            dp = lax.dot_general(doc, vc, nt_dims, preferred_element_type=F32)
            dsum = jnp.sum(doc.astype(F32) * o_ref[rows, :].astype(F32), axis=1, keepdims=True)
            ds = (p * (dp - dsum)).astype(BF)
            dq = jnp.dot(ds, kc, preferred_element_type=F32)
            dq_ref[rows, :QK_NOPE] = (dq[:, :QK_NOPE] * scale).astype(BF)
            dyr = dq[:, QK_NOPE:]
            dq_ref[rows, QK_NOPE:] = ((dyr * c_ref[rows, :] - _rot_half(dyr * s_ref[rows, :])) * scale).astype(BF)
            dk_acc[0:L, :] += lax.dot_general(ds, qc, tn_dims, preferred_element_type=F32)
            dv_acc[0:L, :] += lax.dot_general(p.astype(BF), doc, tn_dims, preferred_element_type=F32)
        dk_ref[:, :QK_NOPE] = dk_acc[:, :QK_NOPE].astype(BF)
        dk_ref[:, QK_NOPE:] = jnp.zeros((T, HEAD_PAD - QK_NOPE), BF)
        dv_ref[...] = dv_acc[...].astype(BF)

        @pl.when(h == 0)
        def _():
            dkr_ref[...] = dk_acc[:, QK_NOPE:]

        @pl.when(h > 0)
        def _():
            dkr_ref[...] += dk_acc[:, QK_NOPE:]

    hp = pl.BlockSpec((T, HEAD_PAD), lambda h: (0, h))
    hv = pl.BlockSpec((T, V_HEAD), lambda h: (0, h))
    tab = pl.BlockSpec((T, 128), lambda h: (0, 0))
    return pl.pallas_call(
        body, name=name, grid=(H,),
        in_specs=[hp, hp, hv, hv, hv, hv, tab, tab],
        out_specs=[hp, hp, hv, tab],
        out_shape=[jax.ShapeDtypeStruct((T, H * HEAD_PAD), BF), jax.ShapeDtypeStruct((T, H * HEAD_PAD), BF),
                   jax.ShapeDtypeStruct((T, H * V_HEAD), BF), jax.ShapeDtypeStruct((T, 128), F32)],
        scratch_shapes=[pltpu.VMEM((T, HEAD_PAD), F32), pltpu.VMEM((T, V_HEAD), F32)],
        compiler_params=_cparams(("arbitrary",)),
    )(q, k, v, o, do, lse, cos, sin)


def _place():
    x, y, c = lax.axis_index("x"), lax.axis_index("y"), lax.axis_index("c")
    return x, y, c


def _other_chips(x, y):
    return [(1 - x, y), (x, 1 - y), (1 - x, 1 - y)]


def gather_devices(v, *, name):
    R, C = v.shape

    def body(v_ref, o_ref, send_sems, recv_sems):
        x, y, c = _place()
        me = 4 * x + 2 * y + c
        o_ref[me] = v_ref[...]
        copies = []
        for k in range(1, N_DEV):
            fx, fy, fc = (k >> 2) & 1, (k >> 1) & 1, k & 1
            to = (x ^ fx, y ^ fy, c ^ fc)
            cp = pltpu.make_async_remote_copy(src_ref=v_ref, dst_ref=o_ref.at[me], send_sem=send_sems.at[k - 1],
                                              recv_sem=recv_sems.at[k - 1], device_id=to, device_id_type=MESH)
            cp.start()
            copies.append(cp)
        for k in range(1, N_DEV):
            fx, fy, fc = (k >> 2) & 1, (k >> 1) & 1, k & 1
            frm = 4 * (x ^ fx) + 2 * (y ^ fy) + (c ^ fc)
            pltpu.make_async_remote_copy(src_ref=v_ref, dst_ref=o_ref.at[frm], send_sem=send_sems.at[k - 1],
                                         recv_sem=recv_sems.at[k - 1], device_id=(x, y, c),
                                         device_id_type=MESH).wait_recv()
        for cp in copies:
            cp.wait_send()

    return pl.pallas_call(
        body, name=name, out_shape=jax.ShapeDtypeStruct((N_DEV, R, C), F32),
        in_specs=[pl.BlockSpec(memory_space=pltpu.VMEM)], out_specs=pl.BlockSpec(memory_space=pltpu.VMEM),
        scratch_shapes=[pltpu.SemaphoreType.DMA((N_DEV - 1,)), pltpu.SemaphoreType.DMA((N_DEV - 1,))],
    )(v)


HBM_SPEC = pl.BlockSpec(memory_space=pltpu.HBM)
SEM_SPEC = pl.BlockSpec(memory_space=pltpu.SEMAPHORE)
DATAFLOW = pltpu.SideEffectType.DATAFLOW_SIDE_EFFECTING


def _hbm(a):
    return pltpu.with_memory_space_constraint(a, pltpu.HBM)


def _half(ref, lead, c, axis):
    rows, cols = ref.shape[-2], ref.shape[-1]
    if axis == 0:
        return ref.at[(*lead, pl.ds(c * (rows // 2), rows // 2))]
    return ref.at[(*lead, slice(None), pl.ds(c * (cols // 2), cols // 2))]


def copies_start(arrs, plan, n_sems, after, *, name):
    n = len(arrs)

    def body(*refs):
        outs = refs[n + 1:2 * n + 1]
        send_sems, recv_sems, token = refs[2 * n + 1:]
        for cp in plan(outs, send_sems, recv_sems, False):
            cp.start()
        token[...] = jnp.zeros_like(token)

    res = pl.pallas_call(
        body, name=name,
        out_shape=tuple(pltpu.HBM(a.shape, a.dtype) for a in arrs)
        + (pltpu.SemaphoreType.DMA((n_sems,)), pltpu.SemaphoreType.DMA((n_sems,)), jax.ShapeDtypeStruct((8, 128), F32)),
        in_specs=(HBM_SPEC,) * n + (pl.BlockSpec(memory_space=pl.ANY),),
        out_specs=(HBM_SPEC,) * n + (SEM_SPEC, SEM_SPEC, pl.BlockSpec(memory_space=pltpu.VMEM)),
        input_output_aliases={i: i for i in range(n)},
        compiler_params=pltpu.CompilerParams(has_side_effects=DATAFLOW),
    )(*[_hbm(a) for a in arrs], after)
    return list(res[:n]), res[n], res[n + 1], res[n + 2]


def copies_wait(arrs, plan, send_sems, recv_sems, after, *, name):
    n = len(arrs)

    def body(*refs):
        for cp in plan(refs[:n], refs[n], refs[n + 1], True):
            cp.wait_send()
            cp.wait_recv()

    res = pl.pallas_call(
        body, name=name,
        out_shape=tuple(pltpu.HBM(a.shape, a.dtype) for a in arrs),
        in_specs=(HBM_SPEC,) * n + (SEM_SPEC, SEM_SPEC, pl.BlockSpec(memory_space=pl.ANY)),
        out_specs=(HBM_SPEC,) * n,
        input_output_aliases={i: i for i in range(n)},
        compiler_params=pltpu.CompilerParams(has_side_effects=DATAFLOW),
    )(*arrs, send_sems, recv_sems, after)
    return list(res)


def gather_plan(axes):
    def plan(bufs, send_sems, recv_sems, receiving):
        x, y, c = _place()
        me = 2 * x + y
        out = []
        for i, b in enumerate(bufs):
            for j, (cx, cy) in enumerate(_other_chips(x, y)):
                src = _half(b, (me,), c, axes[i])
                dst = _half(b, (2 * cx + cy,), c, axes[i]) if receiving else src
                out.append(pltpu.make_async_remote_copy(
                    src_ref=src, dst_ref=dst, send_sem=send_sems.at[3 * i + j], recv_sem=recv_sems.at[3 * i + j],
                    device_id=(cx, cy, c), device_id_type=MESH))
        return out
    return plan


def forward_plan(axes):
    def plan(bufs, send_sems, recv_sems, receiving):
        x, y, c = _place()
        out = []
        for i, b in enumerate(bufs):
            for j, (cx, cy) in enumerate(_other_chips(x, y)):
                blk = _half(b, (2 * cx + cy,), (1 - c) if receiving else c, axes[i])
                out.append(pltpu.make_async_remote_copy(
                    src_ref=blk, dst_ref=blk, send_sem=send_sems.at[3 * i + j], recv_sem=recv_sems.at[3 * i + j],
                    device_id=(x, y, 1 - c), device_id_type=MESH))
        return out
    return plan


def scatter_plan(n):
    def plan(arrs, send_sems, recv_sems, receiving):
        x, y, c = _place()
        out = []
        for i in range(n):
            for j, (cx, cy) in enumerate(_other_chips(x, y)):
                out.append(pltpu.make_async_remote_copy(
                    src_ref=arrs[i].at[2 * cx + cy], dst_ref=arrs[n + i].at[j], send_sem=send_sems.at[3 * i + j],
                    recv_sem=recv_sems.at[3 * i + j], device_id=(cx, cy, c), device_id_type=MESH))
        return out
    return plan


def exchange_plan(axes):
    n = len(axes)

    def plan(arrs, send_sems, recv_sems, receiving):
        x, y, c = _place()
        out = []
        for i in range(n):
            for k in range(N_CHIPS):
                out.append(pltpu.make_async_remote_copy(
                    src_ref=_half(arrs[i], (k,), 1 - c, axes[i]), dst_ref=arrs[n + i].at[k],
                    send_sem=send_sems.at[N_CHIPS * i + k], recv_sem=recv_sems.at[N_CHIPS * i + k],
                    device_id=(x, y, 1 - c), device_id_type=MESH))
        return out
    return plan


def devices_plan(arrs, send_sems, recv_sems, receiving):
    x, y, c = _place()
    out = []
    for k in range(1, N_DEV):
        px, py, pc = x ^ ((k >> 2) & 1), y ^ ((k >> 1) & 1), c ^ (k & 1)
        slot = (4 * px + 2 * py + pc) if receiving else (4 * x + 2 * y + c)
        out.append(pltpu.make_async_remote_copy(
            src_ref=arrs[0], dst_ref=arrs[1].at[slot], send_sem=send_sems.at[k - 1], recv_sem=recv_sems.at[k - 1],
            device_id=(px, py, pc), device_id_type=MESH))
    return out


def share_halves(bufs, axes, after, *, name):
    n = len(bufs)

    def body(*refs):
        outs = refs[n + 1:2 * n + 1]
        send_sems, recv_sems = refs[2 * n + 1:]
        x, y, c = _place()
        sib = (x, y, 1 - c)
        cps = []
        for i in range(n):
            mine = _half(outs[i], (), c, axes[i])
            cp = pltpu.make_async_remote_copy(src_ref=mine, dst_ref=mine, send_sem=send_sems.at[i],
                                              recv_sem=recv_sems.at[i], device_id=sib, device_id_type=MESH)
            cp.start()
            cps.append(cp)
        for i in range(n):
            theirs = _half(outs[i], (), 1 - c, axes[i])
            pltpu.make_async_remote_copy(src_ref=theirs, dst_ref=theirs, send_sem=send_sems.at[i],
                                         recv_sem=recv_sems.at[i], device_id=sib, device_id_type=MESH).wait_recv()
        for cp in cps:
            cp.wait_send()

    anyspec = pl.BlockSpec(memory_space=pl.ANY)
    return pl.pallas_call(
        body, name=name,
        out_shape=[jax.ShapeDtypeStruct(b.shape, b.dtype) for b in bufs],
        in_specs=[anyspec] * (n + 1), out_specs=[anyspec] * n,
        input_output_aliases={i: i for i in range(n)},
        scratch_shapes=[pltpu.SemaphoreType.DMA((n,)), pltpu.SemaphoreType.DMA((n,))],
    )(*bufs, after)


def _row_tile(r, cols, itemsize, target_bytes=2 * 1024 * 1024):
    best = None
    for t in range(16, r + 1, 16):
        if r % t == 0 and t * cols * itemsize <= target_bytes:
            best = t
    return best if best is not None else r


def pair_sum(g, recv, half, axis, *, name):
    _, hr, hc = recv.shape
    tr = _row_tile(hr, hc, 4)
    nb = hr // tr
    if axis == 0:
        mine = pl.BlockSpec((None, tr, hc), lambda k, r, hf: (k, hf[0] * nb + r, 0))
    else:
        mine = pl.BlockSpec((None, tr, hc), lambda k, r, hf: (k, r, hf[0]))

    def body(half_ref, a_ref, b_ref, o_ref):
        o_ref[...] = (a_ref[...].astype(F32) + b_ref[...].astype(F32)).astype(BF)

    return pl.pallas_call(
        body, name=name,
        grid_spec=pltpu.PrefetchScalarGridSpec(
            num_scalar_prefetch=1, grid=(N_CHIPS, nb),
            in_specs=[mine, pl.BlockSpec((None, tr, hc), lambda k, r, hf: (k, r, 0))],
            out_specs=pl.BlockSpec((None, tr, hc), lambda k, r, hf: (k, r, 0))),
        out_shape=jax.ShapeDtypeStruct((N_CHIPS, hr, hc), BF),
        compiler_params=_cparams(("parallel", "parallel")),
    )(half, g, recv)


def chip_sum(s, recv, place, axis, *, name):
    _, hr, hc = s.shape
    tr = _row_tile(hr, hc, 4)
    nb = hr // tr
    if axis == 0:
        o_spec = pl.BlockSpec((tr, hc), lambda r, pc: (pc[1] * nb + r, 0))
        o_shape = (2 * hr, hc)
    else:
        o_spec = pl.BlockSpec((tr, hc), lambda r, pc: (r, pc[1]))
        o_shape = (hr, 2 * hc)

    def body(place_ref, a_ref, b_ref, o_ref):
        acc = a_ref[...].astype(F32)
        for j in range(3):
            acc = acc + b_ref[j].astype(F32)
        o_ref[...] = acc

    return pl.pallas_call(
        body, name=name,
        grid_spec=pltpu.PrefetchScalarGridSpec(
            num_scalar_prefetch=1, grid=(nb,),
            in_specs=[pl.BlockSpec((None, tr, hc), lambda r, pc: (pc[0], r, 0)),
                      pl.BlockSpec((3, tr, hc), lambda r, pc: (0, r, 0))],
            out_specs=o_spec),
        out_shape=jax.ShapeDtypeStruct(o_shape, F32),
        compiler_params=_cparams(("parallel",)),
    )(place, s, recv)


def sum_devices(v, *, name):
    _, R, C = v.shape

    def body(v_ref, o_ref):
        acc = v_ref[0]
        for k in range(1, N_DEV):
            acc = acc + v_ref[k]
        o_ref[...] = acc

    return pl.pallas_call(body, name=name, out_shape=jax.ShapeDtypeStruct((R, C), F32))(v)


def adamw(w, g, m, v, *, name):
    R, C = w.shape
    tr = _row_tile(R, C, 4, target_bytes=1024 * 1024)
    c1 = 1.0 / (1.0 - ADAM_B1 ** ADAM_STEP)
    c2 = 1.0 / (1.0 - ADAM_B2 ** ADAM_STEP)

    def body(w_ref, g_ref, m_ref, v_ref, go_ref, d_ref, nm_ref, nv_ref):
        gg = g_ref[...]
        go_ref[...] = gg
        nm = ADAM_B1 * m_ref[...] + (1.0 - ADAM_B1) * gg
        nv = ADAM_B2 * v_ref[...] + (1.0 - ADAM_B2) * (gg * gg)
        nm_ref[...] = nm
        nv_ref[...] = nv
        d_ref[...] = -ADAM_LR * ((nm * c1) / (jnp.sqrt(nv * c2) + ADAM_EPS) + ADAM_WD * w_ref[...])

    spec = pl.BlockSpec((tr, C), lambda i: (i, 0))
    return pl.pallas_call(
        body, name=name, grid=(R // tr,), in_specs=[spec] * 4, out_specs=[spec] * 4,
        out_shape=[jax.ShapeDtypeStruct((R, C), F32)] * 4, compiler_params=_cparams(("parallel",)),
    )(w, g, m, v)


def _rope_tables(T):
    pos = np.arange(T, dtype=np.float32)
    inv_freq = (ROPE_THETA ** (-np.arange(0, QK_ROPE, 2, dtype=np.float32) / QK_ROPE)).astype(np.float32)
    ang = pos[:, None] * inv_freq[None, :]
    z = np.zeros((T, 64), np.float32)
    cos = np.concatenate([np.cos(ang), np.cos(ang), z], axis=1).astype(np.float32)
    sin = np.concatenate([np.sin(ang), np.sin(ang), z], axis=1).astype(np.float32)
    return jnp.asarray(cos), jnp.asarray(sin)


def _pick(n, cands):
    for c in cands:
        if n % c == 0:
            return c
    return n


def kernel(x, meta_tokens, g_mix, w_in, b_glu, b_gate, w_dw, b_dw, g_conv_ln, b_conv_ln, w_conv_out, b_conv_out, g_q_lora, w_uq, g_kv_lora, w_uk, w_uv, w_attn_out, w_out, g_ffn, w_ffn_gate, w_ffn_up, w_ffn_down, g_final, loss_target, m_meta_tokens, m_g_mix, m_w_in, m_b_glu, m_b_gate, m_w_dw, m_b_dw, m_g_conv_ln, m_b_conv_ln, m_w_conv_out, m_b_conv_out, m_g_q_lora, m_w_uq, m_g_kv_lora, m_w_uk, m_w_uv, m_w_attn_out, m_w_out, m_g_ffn, m_w_ffn_gate, m_w_ffn_up, m_w_ffn_down, m_g_final, v_meta_tokens, v_g_mix, v_w_in, v_b_glu, v_b_gate, v_w_dw, v_b_dw, v_g_conv_ln, v_b_conv_ln, v_w_conv_out, v_b_conv_out, v_g_q_lora, v_w_uq, v_g_kv_lora, v_w_uk, v_w_uv, v_w_attn_out, v_w_out, v_g_ffn, v_w_ffn_gate, v_w_ffn_up, v_w_ffn_down, v_g_final):
    weights = dict(meta_tokens=meta_tokens, g_mix=g_mix, w_in=w_in, b_glu=b_glu, b_gate=b_gate, w_dw=w_dw, b_dw=b_dw,
                   g_conv_ln=g_conv_ln, b_conv_ln=b_conv_ln, w_conv_out=w_conv_out, b_conv_out=b_conv_out,
                   g_q_lora=g_q_lora, w_uq=w_uq, g_kv_lora=g_kv_lora, w_uk=w_uk, w_uv=w_uv, w_attn_out=w_attn_out,
                   w_out=w_out, g_ffn=g_ffn, w_ffn_gate=w_ffn_gate, w_ffn_up=w_ffn_up, w_ffn_down=w_ffn_down,
                   g_final=g_final)
    m_in = dict(meta_tokens=m_meta_tokens, g_mix=m_g_mix, w_in=m_w_in, b_glu=m_b_glu, b_gate=m_b_gate, w_dw=m_w_dw,
                b_dw=m_b_dw, g_conv_ln=m_g_conv_ln, b_conv_ln=m_b_conv_ln, w_conv_out=m_w_conv_out,
                b_conv_out=m_b_conv_out, g_q_lora=m_g_q_lora, w_uq=m_w_uq, g_kv_lora=m_g_kv_lora, w_uk=m_w_uk,
                w_uv=m_w_uv, w_attn_out=m_w_attn_out, w_out=m_w_out, g_ffn=m_g_ffn, w_ffn_gate=m_w_ffn_gate,
                w_ffn_up=m_w_ffn_up, w_ffn_down=m_w_ffn_down, g_final=m_g_final)
    v_in = dict(meta_tokens=v_meta_tokens, g_mix=v_g_mix, w_in=v_w_in, b_glu=v_b_glu, b_gate=v_b_gate, w_dw=v_w_dw,
                b_dw=v_b_dw, g_conv_ln=v_g_conv_ln, b_conv_ln=v_b_conv_ln, w_conv_out=v_w_conv_out,
                b_conv_out=v_b_conv_out, g_q_lora=v_g_q_lora, w_uq=v_w_uq, g_kv_lora=v_g_kv_lora, w_uk=v_w_uk,
                w_uv=v_w_uv, w_attn_out=v_w_attn_out, w_out=v_w_out, g_ffn=v_g_ffn, w_ffn_gate=v_w_ffn_gate,
                w_ffn_up=v_w_ffn_up, w_ffn_down=v_w_ffn_down, g_final=v_g_final)
    names = list(weights)

    seq, D = x.shape[1], x.shape[2]
    C = w_conv_out.shape[2]
    R = g_q_lora.shape[1]
    H = w_uv.shape[2] * N_CHIPS // V_HEAD
    FB = w_ffn_gate.shape[2]
    FF = FB * N_CHIPS
    length = N_META + seq
    T = -(-length // BLOCK_Q) * BLOCK_Q
    scale = (QK_NOPE + QK_ROPE) ** -0.5
    assert C == D and w_in.shape[2] * N_CHIPS == 2 * C + 2 * R + QK_ROPE + 2 * D

    xi, yi, ci = _place()
    chip = (2 * xi + yi).astype(jnp.int32)
    half_arr = ci.astype(jnp.int32).reshape(1)

    tr = _pick(T, (272, 256, 128))
    tmh = T // 2 if (T // 2) % 16 == 0 else T
    tmq = T // 4 if (T // 4) % 16 == 0 else T
    tmw = _pick(D, (1024, 512))
    nch = 4
    rc = _pick(T, (136, 128))
    tcv = 256

    def tn_of(n, pref=512):
        return _pick(n, (pref, 384, 256, 128))

    grp_a = ["w_in"]
    grp_b = ["w_conv_out", "w_uq", "w_uk", "w_uv", "w_attn_out", "w_out"]
    grp_c = ["w_ffn_gate", "w_ffn_up", "w_ffn_down"]
    big = grp_a + grp_b + grp_c
    axes = {nm: 0 for nm in big}
    axes["w_in"] = 1
    shard = {nm: weights[nm][0] for nm in big}
    shard["w_in"] = jnp.swapaxes(w_in[0], 0, 1)

    def gather_begin(nms, srcs, after, tag):
        bufs = []
        for w in srcs:
            wb = w.astype(BF)
            bufs.append(lax.dynamic_update_slice(lax.empty((N_CHIPS,) + wb.shape, BF), wb[None], (chip, 0, 0)))
        ax = [axes[nm] for nm in nms]
        bufs, ssem, rsem, tok = copies_start(bufs, gather_plan(ax), 3 * len(nms), after, name=f"gather_{tag}_start")
        return (nms, ax, bufs, ssem, rsem), tok

    def gather_middle(state, after, tag):
        nms, ax, bufs, ssem, rsem = state
        bufs = copies_wait(bufs, gather_plan(ax), ssem, rsem, after, name=f"gather_{tag}_wait")
        bufs, ssem, rsem, tok = copies_start(bufs, forward_plan(ax), 3 * len(nms), half_arr,
                                             name=f"gather_{tag}_forward_start")
        return (nms, ax, bufs, ssem, rsem), tok

    def gather_end(state, after, tag):
        nms, ax, bufs, ssem, rsem = state
        return dict(zip(nms, copies_wait(bufs, forward_plan(ax), ssem, rsem, after, name=f"gather_{tag}_forward_wait")))

    tiny = gather_devices(jnp.concatenate([meta_tokens, jnp.pad(w_dw[0], ((0, 1), (0, 0)))], axis=0),
                          name="gather_meta_w_dw")
    meta_full = jnp.concatenate([tiny[2 * k, :N_META] for k in range(N_CHIPS)], axis=1)
    w_dw_full = jnp.concatenate([tiny[2 * k, N_META:] for k in range(N_CHIPS)], axis=1)

    st_ga, tok_ga = gather_begin(grp_a, [shard["w_in"]], tiny, "a")
    tok_ga, held = lax.optimization_barrier((tok_ga, (x, loss_target, [shard[nm] for nm in grp_b + grp_c])))
    x_l, tgt_l, shard_bc = held
    h0 = jnp.concatenate([meta_full, x_l[0], jnp.zeros((T - length, D), F32)], axis=0)
    tgt = jnp.pad(tgt_l[0], ((N_META, T - length), (0, 0)))
    cos, sin = _rope_tables(T)

    st_ga, tok_ga = gather_middle(st_ga, h0, "a")
    G = gather_end(st_ga, tok_ga, "a")
    st_gb, tok_gb = gather_begin(grp_b, shard_bc[:len(grp_b)], G["w_in"], "b")
    st_gc, tok_gc = gather_begin(grp_c, shard_bc[len(grp_b):], tok_gb, "c")

    WT_in = G["w_in"].reshape(N_CHIPS * G["w_in"].shape[1], D)
    o1, o2 = 2 * C, 2 * C + 2 * R + QK_ROPE
    WT_glu = WT_in[:o1]
    WT_small = jnp.pad(WT_in[o1:o2], ((0, 128 - QK_ROPE), (0, 0)))
    WT_gate = WT_in[o2:]

    u = rms_fwd(h0, g_mix, tr=tr, name="rms_mix", deps=[tok_gb, tok_gc])
    z_glu = mm(u, WT_glu, mode="nt", tm=T, tn=tn_of(2 * C), tk=D, out_dtype=F32, name="mm_z_glu")
    z_small = mm(u, WT_small, mode="nt", tm=T, tn=tn_of(2 * R + 128, 384), tk=D, out_dtype=F32, name="mm_z_small")
    z_gate = mm(u, WT_gate, mode="nt", tm=T, tn=tn_of(2 * D), tk=D, out_dtype=F32, name="mm_z_gate")

    c0 = glu_fwd(z_glu, b_glu, tr=tr, name="glu_fwd")
    st_gb, tok_gb = gather_middle(st_gb, c0, "b")
    c1 = dwconv_fwd(c0, w_dw_full, b_dw, tc=tcv, rc=rc, name="dwconv_fwd", deps=[tok_gb])
    c3 = ln_silu_fwd(c1, g_conv_ln, b_conv_ln, tr=tr, name="ln_silu_fwd")

    def cols_full(a):
        return jnp.concatenate([a[k] for k in range(N_CHIPS)], axis=1)

    G.update(gather_end(st_gb, c1, "b"))
    W_co = G["w_conv_out"].reshape(C, D)
    W_uq = cols_full(G["w_uq"]).reshape(R, H, QK_NOPE + QK_ROPE)
    Wq_p = jnp.pad(W_uq, ((0, 0), (0, 0), (0, HEAD_PAD - QK_NOPE - QK_ROPE))).reshape(R, H * HEAD_PAD)
    W_uk = cols_full(G["w_uk"])
    W_uv = cols_full(G["w_uv"])
    W_ao = G["w_attn_out"].reshape(H * V_HEAD, D)
    W_o = G["w_out"].reshape(D, D)

    y_conv = mm(c3, W_co, mode="nn", tm=T, tn=tn_of(D), tk=C, out_dtype=F32, name="mm_y_conv")

    cq, ckv, kr = lora_norm_fwd(z_small, g_q_lora, g_kv_lora, cos, sin, tr=tr, name="lora_norm_fwd")
    q = q_proj(cq, Wq_p, cos, sin, scale=scale, name="q_proj")
    kk = k_proj(ckv, W_uk, kr, name="k_proj")
    vv = mm(ckv, W_uv, mode="nn", tm=T, tn=tn_of(H * V_HEAD), tk=R, out_dtype=BF, name="mm_v")
    o_attn, lse = attn_fwd(q, kk, vv, nch=nch, name="attn_fwd")
    st_gc, tok_gc = gather_middle(st_gc, o_attn, "c")
    y_attn = mm(o_attn, W_ao, mode="nn", tm=T, tn=tn_of(D), tk=H * V_HEAD, out_dtype=F32, name="mm_y_attn",
                deps=[tok_gc])

    mix = mix_fwd(z_gate, y_conv, y_attn, b_gate, b_conv_out, tr=tr, name="mix_fwd")
    h1 = mm(mix, W_o, mode="nn", tm=T, tn=tn_of(D), tk=D, out_dtype=F32, name="mm_h1", res=h0)

    hn = rms_fwd(h1, g_ffn, tr=tr, name="rms_ffn")

    G.update(gather_end(st_gc, hn, "c"))
    W_fg, W_fu = G["w_ffn_gate"], G["w_ffn_up"]
    W_fd = G["w_ffn_down"].reshape(FF, D)

    fa, fb, f = ffn_fwd(hn, W_fg, W_fu, tm=tmq, name="ffn_fwd")
    h2 = mm(f, W_fd, mode="nn", tm=tmh, tn=tn_of(D, 256), tk=FF, out_dtype=F32, name="mm_h2", res=h1)

    place_arr = jnp.stack([chip, ci.astype(jnp.int32)])

    def col_blocks(a):
        r, cfull = a.shape
        return a.reshape(r, N_CHIPS, cfull // N_CHIPS).transpose(1, 0, 2)

    def rs_begin(nms, gls, after, tag):
        ax = [axes[nm] for nm in nms]
        lands = []
        for g, a in zip(gls, ax):
            _, r, cfull = g.shape
            lands.append(lax.empty((N_CHIPS, r // 2, cfull) if a == 0 else (N_CHIPS, r, cfull // 2), BF))
        arrs, ssem, rsem, tok = copies_start(list(gls) + lands, exchange_plan(ax), N_CHIPS * len(nms), after,
                                             name=f"rs_{tag}_exchange_start")
        return (nms, ax, arrs, ssem, rsem), tok

    def rs_middle(state, after, tag):
        nms, ax, arrs, ssem, rsem = state
        n = len(nms)
        arrs = copies_wait(arrs, exchange_plan(ax), ssem, rsem, after, name=f"rs_{tag}_exchange_wait")
        pair = [pair_sum(g, r, half_arr, a, name="rs_pair_sum_" + nm)
                for nm, g, r, a in zip(nms, arrs[:n], arrs[n:], ax)]
        lands = [lax.empty((3,) + p.shape[1:], BF) for p in pair]
        arrs, ssem, rsem, tok = copies_start(pair + lands, scatter_plan(n), 3 * n, half_arr,
                                             name=f"rs_{tag}_scatter_start")
        return (nms, ax, arrs, ssem, rsem), tok

    def rs_end(state, after, tag):
        nms, ax, arrs, ssem, rsem = state
        n = len(nms)
        arrs = copies_wait(arrs, scatter_plan(n), ssem, rsem, after, name=f"rs_{tag}_scatter_wait")
        return [chip_sum(s, r, place_arr, a, name="rs_chip_sum_" + nm)
                for nm, s, r, a in zip(nms, arrs[:n], arrs[n:], ax)]

    d_h2, d_h2b, loss_p, d_g_final = final_loss(h2, tgt, g_final.reshape(1, D), seq=seq, tr=tr, name="final_loss")
    loss = lax.psum(loss_p[0, 0], ("x", "y", "c"))

    dW_fd = mm(f, d_h2b, mode="tn", tm=FB, tn=tn_of(D), tk=T, out_dtype=BF, name="mm_dw_ffn_down")
    d_a, d_b = ffn_dact(d_h2b, W_fd, fa, fb, tm=tmq, name="ffn_dact")
    dW_fg = mm(hn, d_a, mode="tn", tm=tmw, tn=FB, tk=T, out_dtype=BF, name="mm_dw_ffn_gate", out_blocks=N_CHIPS)
    dW_fu = mm(hn, d_b, mode="tn", tm=tmw, tn=FB, tk=T, out_dtype=BF, name="mm_dw_ffn_up", out_blocks=N_CHIPS)
    st_c, tok_rc = rs_begin(grp_c, [dW_fg, dW_fu, dW_fd.reshape(N_CHIPS, FB, D)], half_arr, "c")
    d_hn = ffn_dhn(d_a, d_b, W_fg, W_fu, tm=tmq, tn=_pick(D, (256,)), name="ffn_dhn", deps=[tok_rc])
    st_c, tok_rc = rs_middle(st_c, d_hn, "c")
    d_h1, d_h1b, d_g_ffn = rms_bwd(d_hn, h1, g_ffn, d_h2, tr=tr, name="rms_ffn_bwd", deps=[tok_rc])

    d_mix = mm(d_h1b, W_o, mode="nt", tm=T, tn=tn_of(D), tk=D, out_dtype=F32, name="mm_d_mix")
    dW_o = mm(mix, d_h1b, mode="tn", tm=tmw, tn=tn_of(D), tk=T, out_dtype=BF, name="mm_dw_out")
    d_yc, d_ya, d_zgate, d_b_gate, d_b_co = mix_bwd(z_gate, y_conv, y_attn, d_mix, b_gate, b_conv_out, tr=tr,
                                                    name="mix_bwd")

    dW_co = mm(c3, d_yc, mode="tn", tm=tmw, tn=tn_of(D), tk=T, out_dtype=BF, name="mm_dw_conv_out")
    d_c3 = mm(d_yc, W_co, mode="nt", tm=T, tn=tn_of(C), tk=D, out_dtype=F32, name="mm_d_c3")
    d_c1, d_g_ln, d_b_ln, d_b_dw = ln_silu_bwd(c1, d_c3, g_conv_ln, b_conv_ln, tr=tr, name="ln_silu_bwd")
    d_c0, d_w_dw = dwconv_bwd(c0, d_c1, w_dw_full, tc=tcv, rc=rc, name="dwconv_bwd")
    d_zglu, d_b_glu = glu_bwd(z_glu, b_glu, d_c0, tr=tr, name="glu_bwd")

    dW_ao = mm(o_attn, d_ya, mode="tn", tm=_pick(H * V_HEAD, (1024, 512)), tn=tn_of(D), tk=T, out_dtype=BF, name="mm_dw_attn_out")
    d_o = mm(d_ya, W_ao, mode="nt", tm=T, tn=tn_of(H * V_HEAD), tk=D, out_dtype=BF, name="mm_d_o")
    d_q2, d_k2, d_v, d_kr = attn_bwd(q, kk, vv, o_attn, d_o, lse, cos, sin, scale=scale, nch=nch, name="attn_bwd")

    dWq_p = mm(cq, d_q2, mode="tn", tm=R, tn=tn_of(H * HEAD_PAD), tk=T, out_dtype=BF, name="mm_dw_uq")
    d_cq = mm(d_q2, Wq_p, mode="nt", tm=T, tn=R, tk=_pick(H * HEAD_PAD, (1024,)), out_dtype=F32, name="mm_d_cq")
    Wk_p = jnp.pad(W_uk.reshape(R, H, QK_NOPE), ((0, 0), (0, 0), (0, HEAD_PAD - QK_NOPE))).reshape(R, H * HEAD_PAD)
    dWk_p = mm(ckv, d_k2, mode="tn", tm=R, tn=tn_of(H * HEAD_PAD), tk=T, out_dtype=BF, name="mm_dw_uk")
    dW_uv = mm(ckv, d_v, mode="tn", tm=R, tn=tn_of(H * V_HEAD), tk=T, out_dtype=BF, name="mm_dw_uv")
    red_c = rs_end(st_c, d_v, "c")
    dW_uq = dWq_p.reshape(R, H, HEAD_PAD)[:, :, :QK_NOPE + QK_ROPE].reshape(R, H * (QK_NOPE + QK_ROPE))
    dW_uk = dWk_p.reshape(R, H, HEAD_PAD)[:, :, :QK_NOPE].reshape(R, H * QK_NOPE)
    st_b, tok_rb = rs_begin(grp_b, [dW_co.reshape(N_CHIPS, C // N_CHIPS, D), col_blocks(dW_uq), col_blocks(dW_uk),
                                    col_blocks(dW_uv), dW_ao.reshape(N_CHIPS, H * V_HEAD // N_CHIPS, D),
                                    dW_o.reshape(N_CHIPS, D // N_CHIPS, D)], red_c[-1], "b")
    d_ckv = mm(d_k2, Wk_p, mode="nt", tm=T, tn=R, tk=_pick(H * HEAD_PAD, (1024,)), out_dtype=F32, name="mm_d_ckv_k",
               deps=[tok_rb])
    d_ckv = mm(d_v, W_uv, mode="nt", tm=T, tn=R, tk=_pick(H * V_HEAD, (1024,)), out_dtype=F32, name="mm_d_ckv_v",
               res=d_ckv)
    d_zsmall, d_g_q, d_g_kv = lora_norm_bwd(z_small, d_cq, d_ckv, d_kr, cos, sin, g_q_lora, g_kv_lora, tr=tr,
                                            name="lora_norm_bwd")
    st_b, tok_rb = rs_middle(st_b, d_zsmall, "b")

    tmi = _pick(2 * C, (1024, 512))
    dWT_glu = mm(d_zglu, u, mode="tn", tm=tmi, tn=tn_of(D), tk=T, out_dtype=BF, name="mm_dw_glu", deps=[tok_rb])
    dWT_small = mm(d_zsmall, u, mode="tn", tm=2 * R + 128, tn=tn_of(D), tk=T, out_dtype=BF, name="mm_dw_small")
    dWT_gate = mm(d_zgate, u, mode="tn", tm=tmi, tn=tn_of(D), tk=T, out_dtype=BF, name="mm_dw_gate")
    dWT_in = jnp.concatenate([dWT_glu, dWT_small[:2 * R + QK_ROPE], dWT_gate], axis=0)
    red_b = rs_end(st_b, dWT_in, "b")
    st_a, tok_ra = rs_begin(grp_a, [dWT_in.reshape(N_CHIPS, dWT_in.shape[0] // N_CHIPS, D)], red_b[-1], "a")
    d_u = mm(d_zglu, WT_glu, mode="nn", tm=T, tn=tn_of(D), tk=_pick(2 * C, (2048, 1024)), out_dtype=F32, name="mm_d_u_glu",
             deps=[tok_ra])
    st_a, tok_ra = rs_middle(st_a, d_u, "a")
    d_u = mm(d_zsmall, WT_small, mode="nn", tm=T, tn=tn_of(D), tk=2 * R + 128, out_dtype=F32, name="mm_d_u_small",
             res=d_u, deps=[tok_ra])
    d_u = mm(d_zgate, WT_gate, mode="nn", tm=T, tn=tn_of(D), tk=_pick(2 * D, (2048, 1024)), out_dtype=F32,
             name="mm_d_u_gate", res=d_u)
    d_h0, _, d_g_mix = rms_bwd(d_u, h0, g_mix, d_h1, tr=tr, name="rms_mix_bwd")

    grad_x = d_h0[N_META:length][None]

    small = ["g_mix", "b_glu", "b_gate", "b_dw", "g_conv_ln", "b_conv_ln", "b_conv_out", "g_q_lora", "g_kv_lora",
             "g_ffn", "g_final", "w_dw", "meta_tokens"]
    sgrads = dict(g_mix=d_g_mix, b_glu=d_b_glu, b_gate=d_b_gate, b_dw=d_b_dw, g_conv_ln=d_g_ln, b_conv_ln=d_b_ln,
                  b_conv_out=d_b_co, g_q_lora=d_g_q, g_kv_lora=d_g_kv, g_ffn=d_g_ffn, g_final=d_g_final,
                  w_dw=d_w_dw[:CONV_WIDTH], meta_tokens=d_h0[:N_META])
    sizes = [int(np.prod(sgrads[nm].shape)) for nm in small]
    packed = jnp.concatenate([sgrads[nm].reshape(-1) for nm in small]).reshape(-1, 128)
    sm_arrs, sm_ss, sm_rs, tok_sm = copies_start([packed, lax.empty((N_DEV,) + packed.shape, F32)], devices_plan,
                                                 N_DEV - 1, packed, name="gather_small_start")

    grads, delta, new_m, new_v, delta_2d = {}, {}, {}, {}, {}

    def adamw_2d(nm, w2, g2, m2, v2, back):
        g_, d_, m_, v_ = adamw(w2, g2, m2, v2, name="adamw_" + nm)
        delta_2d[nm] = d_
        grads[nm], delta[nm], new_m[nm], new_v[nm] = back(g_), back(d_), back(m_), back(v_)

    reduced = dict(zip(grp_b + grp_c, share_halves(red_b + red_c, [axes[nm] for nm in grp_b + grp_c], tok_sm,
                                                   name="rs_share_halves_bc")))
    for nm in grp_b + grp_c + grp_a:
        if nm == grp_a[0]:
            red_a = rs_end(st_a, delta_2d[grp_c[-1]], "a")
            reduced.update(zip(grp_a, share_halves(red_a, [axes[nm] for nm in grp_a], tok_sm,
                                                   name="rs_share_halves_a")))
        shp = weights[nm].shape
        if axes[nm] == 1:
            adamw_2d(nm, shard[nm], reduced[nm], jnp.swapaxes(m_in[nm][0], 0, 1), jnp.swapaxes(v_in[nm][0], 0, 1),
                     lambda a: jnp.swapaxes(a, 0, 1)[None])
        else:
            two = (shp[-2], shp[-1])
            adamw_2d(nm, weights[nm].reshape(two), reduced[nm], m_in[nm].reshape(two), v_in[nm].reshape(two),
                     lambda a, shp=shp: a.reshape(shp))

    sm_arrs = copies_wait(sm_arrs, devices_plan, sm_ss, sm_rs, delta_2d[grp_a[0]], name="gather_small_wait")
    me = (4 * xi + 2 * yi + ci).astype(jnp.int32)
    parts = lax.dynamic_update_slice(sm_arrs[1], sm_arrs[0][None], (me, 0, 0))
    summed = sum_devices(parts, name="sum_small_grads").reshape(-1)
    offs = np.concatenate([[0], np.cumsum(sizes)])
    sfull = {nm: summed[int(offs[i]):int(offs[i + 1])].reshape(sgrads[nm].shape) for i, nm in enumerate(small)}
    for nm in small:
        gfull = sfull[nm]
        if nm == "w_dw":
            cb = C // N_CHIPS
            grads[nm] = lax.dynamic_slice(gfull, (0, chip * cb), (CONV_WIDTH, cb))[None]
        elif nm == "meta_tokens":
            cb = D // N_CHIPS
            grads[nm] = lax.dynamic_slice(gfull, (0, chip * cb), (N_META, cb))
        else:
            grads[nm] = gfull.reshape(weights[nm].shape)

    rep = [nm for nm in small if nm not in ("w_dw", "meta_tokens")]

    def pack(d):
        return jnp.concatenate([d[nm].reshape(-1) for nm in rep]).reshape(-1, 128)

    _, pd, pm, pv = adamw(pack(weights), pack(grads), pack(m_in), pack(v_in), name="adamw_small")
    rsz = [int(np.prod(weights[nm].shape)) for nm in rep]
    roff = np.concatenate([[0], np.cumsum(rsz)])
    for i, nm in enumerate(rep):
        sl = slice(int(roff[i]), int(roff[i + 1]))
        delta[nm] = pd.reshape(-1)[sl].reshape(weights[nm].shape)
        new_m[nm] = pm.reshape(-1)[sl].reshape(weights[nm].shape)
        new_v[nm] = pv.reshape(-1)[sl].reshape(weights[nm].shape)
    for nm in ["w_dw", "meta_tokens"]:
        shp = weights[nm].shape
        two = (shp[-2], shp[-1])
        adamw_2d(nm, weights[nm].reshape(two), grads[nm].reshape(two), m_in[nm].reshape(two), v_in[nm].reshape(two),
                 lambda a, shp=shp: a.reshape(shp))

    return (loss, grad_x, *[grads[nm] for nm in names], *[delta[nm] for nm in names],
            *[new_m[nm] for nm in names], *[new_v[nm] for nm in names])
```

```python
import functools

import numpy as np
import jax
import jax.numpy as jnp
from jax import lax
from jax.experimental import pallas as pl
from jax.experimental.pallas import tpu as pltpu

F32 = jnp.float32
BF = jnp.bfloat16
MESH = pl.DeviceIdType.MESH

N_META = 16
BLOCK_Q = 128
CONV_WIDTH = 31
CONV_PAD = 32
QK_NOPE = 128
QK_ROPE = 64
V_HEAD = 128
HEAD_PAD = 256
ROPE_THETA = 10000.0
EPS = 1e-6
ADAM_LR = 0.001
ADAM_B1 = 0.9
ADAM_B2 = 0.999
ADAM_EPS = 1e-08
ADAM_WD = 0.01
ADAM_STEP = 10
VMEM_LIMIT = 56 * 1024 * 1024
N_CHIPS = 4
N_DEV = 8


def _cparams(sem):
    return pltpu.CompilerParams(dimension_semantics=sem, vmem_limit_bytes=VMEM_LIMIT)


def mm(a, b, *, mode, tm, tn, tk, out_dtype, name, res=None, out_blocks=None, deps=()):
    b3 = b.ndim == 3
    if mode == "nn":
        M, K = a.shape
        N = b.shape[0] * b.shape[2] if b3 else b.shape[1]
        a_spec = pl.BlockSpec((tm, tk), lambda i, j, k: (i, k))
        if b3:
            per = b.shape[2] // tn
            b_spec = pl.BlockSpec((None, tk, tn), lambda i, j, k: (j // per, k, j % per))
        else:
            b_spec = pl.BlockSpec((tk, tn), lambda i, j, k: (k, j))
        dims = (((1,), (0,)), ((), ()))
    elif mode == "nt":
        M, K = a.shape
        N = b.shape[1] if b3 else b.shape[0]
        a_spec = pl.BlockSpec((tm, tk), lambda i, j, k: (i, k))
        if b3:
            per = b.shape[2] // tk
            b_spec = pl.BlockSpec((None, tn, tk), lambda i, j, k: (k // per, j, k % per))
        else:
            b_spec = pl.BlockSpec((tn, tk), lambda i, j, k: (j, k))
        dims = (((1,), (1,)), ((), ()))
    else:
        K, M = a.shape
        N = b.shape[1]
        a_spec = pl.BlockSpec((tk, tm), lambda i, j, k: (k, i))
        b_spec = pl.BlockSpec((tk, tn), lambda i, j, k: (k, j))
        dims = (((0,), (0,)), ((), ()))
    gm, gn, gk = M // tm, N // tn, K // tk
    assert gm * tm == M and gn * tn == N and gk * tk == K, (name, a.shape, b.shape, tm, tn, tk)
    if out_blocks is None:
        o_spec = pl.BlockSpec((tm, tn), lambda i, j, k: (i, j))
        o_shape = (M, N)
    else:
        nbw = N // out_blocks
        per_o = nbw // tn
        assert per_o * tn == nbw
        o_spec = pl.BlockSpec((None, tm, tn), lambda i, j, k: (j // per_o, i, j % per_o))
        o_shape = (out_blocks, M, nbw)
    has_res = res is not None

    def body(*refs):
        a_ref, b_ref = refs[0], refs[1]
        r_ref = refs[2] if has_res else None
        o_ref = refs[2 + has_res + len(deps)]
        acc = refs[-1]
        p = lax.dot_general(a_ref[...], b_ref[...], dims, preferred_element_type=F32)

        def finish(v):
            if has_res:
                v = v + r_ref[...]
            o_ref[...] = v.astype(o_ref.dtype)

        if gk == 1:
            finish(p)
        else:
            k = pl.program_id(2)

            @pl.when(k == 0)
            def _():
                acc[...] = p

            @pl.when(k > 0)
            def _():
                acc[...] += p

            @pl.when(k == gk - 1)
            def _():
                finish(acc[...])

    in_specs = [a_spec, b_spec]
    args = [a, b]
    if has_res:
        in_specs.append(pl.BlockSpec((tm, tn), lambda i, j, k: (i, j)))
        args.append(res)
    in_specs += [pl.BlockSpec(memory_space=pl.ANY)] * len(deps)
    args += list(deps)
    return pl.pallas_call(
        body, name=name, grid=(gm, gn, gk), in_specs=in_specs, out_specs=o_spec,
        out_shape=jax.ShapeDtypeStruct(o_shape, out_dtype),
        scratch_shapes=[pltpu.VMEM((tm, tn), F32)] if gk > 1 else [],
        compiler_params=_cparams(("parallel", "parallel", "arbitrary")),
    )(*args)


def rowwise(body, row_ins, full_ins, row_outs, acc_outs, *, tr, name, deps=()):
    T = row_ins[0].shape[0]
    assert T % tr == 0, (name, T, tr)
    n_ri, n_fi, n_ro = len(row_ins), len(full_ins), len(row_outs)
    n_in = n_ri + n_fi + len(deps)

    def kern(*refs):
        body(pl.program_id(0), refs[:n_ri], refs[n_ri:n_ri + n_fi], refs[n_in:n_in + n_ro], refs[n_in + n_ro:])

    in_specs = [pl.BlockSpec((tr, a.shape[1]), lambda i: (i, 0)) for a in row_ins]
    in_specs += [pl.BlockSpec(a.shape, lambda i: (0, 0)) for a in full_ins]
    in_specs += [pl.BlockSpec(memory_space=pl.ANY)] * len(deps)
    out_specs = [pl.BlockSpec((tr, c), lambda i: (i, 0)) for c, _ in row_outs]
    out_specs += [pl.BlockSpec(s, lambda i: (0, 0)) for s in acc_outs]
    out_shape = [jax.ShapeDtypeStruct((T, c), d) for c, d in row_outs]
    out_shape += [jax.ShapeDtypeStruct(s, F32) for s in acc_outs]
    return pl.pallas_call(
        kern, name=name, grid=(T // tr,), in_specs=in_specs, out_specs=out_specs, out_shape=out_shape,
        compiler_params=_cparams(("arbitrary",)),
    )(*row_ins, *full_ins, *deps)


def _acc(step, ref, val):
    @pl.when(step == 0)
    def _():
        ref[...] = val

    @pl.when(step > 0)
    def _():
        ref[...] += val


def _colsum(x):
    return jnp.sum(x, axis=0, keepdims=True)


def _sigmoid(x):
    return 1.0 / (1.0 + jnp.exp(-x))


def _rot_half(v):
    lane = lax.broadcasted_iota(jnp.int32, v.shape, 1)
    lo = -pltpu.roll(v, 96, 1)
    hi = pltpu.roll(v, 32, 1)
    return jnp.where(lane < 32, lo, jnp.where(lane < 64, hi, 0.0))


def rms_fwd(h, g, *, tr, name, deps=()):
    def body(step, ri, fi, ro, ao):
        x = ri[0][...]
        r = lax.rsqrt(jnp.mean(x * x, axis=-1, keepdims=True) + EPS)
        ro[0][...] = ((x * r) * fi[0][...]).astype(BF)

    return rowwise(body, [h], [g], [(h.shape[1], BF)], [], tr=tr, name=name, deps=deps)[0]


def rms_bwd(dy, h, g, dres, *, tr, name, deps=()):
    D = h.shape[1]

    def body(step, ri, fi, ro, ao):
        d, x, dr = ri[0][...], ri[1][...], ri[2][...]
        r = lax.rsqrt(jnp.mean(x * x, axis=-1, keepdims=True) + EPS)
        n = x * r
        _acc(step, ao[0], _colsum(d * n))
        dn = d * fi[0][...]
        dh = r * (dn - n * jnp.mean(dn * n, axis=-1, keepdims=True)) + dr
        ro[0][...] = dh
        ro[1][...] = dh.astype(BF)

    return rowwise(body, [dy, h, dres], [g], [(D, F32), (D, BF)], [(1, D)], tr=tr, name=name, deps=deps)


def glu_fwd(z, b, *, tr, name):
    C = z.shape[1] // 2

    def body(step, ri, fi, ro, ao):
        zz = ri[0][...] + fi[0][...]
        ro[0][...] = zz[:, :C] * _sigmoid(zz[:, C:])

    return rowwise(body, [z], [b], [(C, F32)], [], tr=tr, name=name)[0]


def glu_bwd(z, b, dc, *, tr, name):
    C = z.shape[1] // 2

    def body(step, ri, fi, ro, ao):
        zz = ri[0][...] + fi[0][...]
        d = ri[1][...]
        za, sg = zz[:, :C], _sigmoid(zz[:, C:])
        dza = d * sg
        dzb = d * za * sg * (1.0 - sg)
        ro[0][:, :C] = dza.astype(BF)
        ro[0][:, C:] = dzb.astype(BF)
        _acc(step, ao[0], _colsum(dza))
        _acc(step, ao[1], _colsum(dzb))

    dz, da, db = rowwise(body, [z, dc], [b], [(2 * C, BF)], [(1, C), (1, C)], tr=tr, name=name)
    return dz, jnp.concatenate([da, db], axis=1)


def ln_silu_fwd(c1, g, b, *, tr, name):
    def body(step, ri, fi, ro, ao):
        x = ri[0][...]
        mu = jnp.mean(x, axis=-1, keepdims=True)
        xc = x - mu
        rstd = lax.rsqrt(jnp.mean(xc * xc, axis=-1, keepdims=True) + EPS)
        y = (xc * rstd) * fi[0][...] + fi[1][...]
        ro[0][...] = (y * _sigmoid(y)).astype(BF)

    return rowwise(body, [c1], [g, b], [(c1.shape[1], BF)], [], tr=tr, name=name)[0]


def ln_silu_bwd(c1, dc3, g, b, *, tr, name):
    C = c1.shape[1]

    def body(step, ri, fi, ro, ao):
        x, d3 = ri[0][...], ri[1][...]
        mu = jnp.mean(x, axis=-1, keepdims=True)
        xc = x - mu
        rstd = lax.rsqrt(jnp.mean(xc * xc, axis=-1, keepdims=True) + EPS)
        n = xc * rstd
        y = n * fi[0][...] + fi[1][...]
        sg = _sigmoid(y)
        dy = d3 * (sg * (1.0 + y * (1.0 - sg)))
        _acc(step, ao[0], _colsum(dy * n))
        _acc(step, ao[1], _colsum(dy))
        dn = dy * fi[0][...]
        dx = rstd * (dn - jnp.mean(dn, axis=-1, keepdims=True) - n * jnp.mean(dn * n, axis=-1, keepdims=True))
        ro[0][...] = dx
        _acc(step, ao[2], _colsum(dx))

    return rowwise(body, [c1, dc3], [g, b], [(C, F32)], [(1, C), (1, C), (1, C)], tr=tr, name=name)


def lora_norm_fwd(zs, gq, gkv, cos, sin, *, tr, name, deps=()):
    R = gq.shape[1]

    def body(step, ri, fi, ro, ao):
        z = ri[0][...]
        for o, gi in ((0, 0), (1, 1)):
            x = z[:, o * R:(o + 1) * R]
            r = lax.rsqrt(jnp.mean(x * x, axis=-1, keepdims=True) + EPS)
            ro[o][...] = ((x * r) * fi[gi][...]).astype(BF)
        kr = z[:, 2 * R:2 * R + 128]
        ro[2][...] = kr * ri[1][...] + _rot_half(kr) * ri[2][...]

    return rowwise(body, [zs, cos, sin], [gq, gkv], [(R, BF), (R, BF), (128, F32)], [], tr=tr, name=name, deps=deps)


def lora_norm_bwd(zs, dcq, dckv, dkr, cos, sin, gq, gkv, *, tr, name):
    R = gq.shape[1]

    def body(step, ri, fi, ro, ao):
        z = ri[0][...]
        for o in (0, 1):
            x = z[:, o * R:(o + 1) * R]
            d = ri[1 + o][...]
            r = lax.rsqrt(jnp.mean(x * x, axis=-1, keepdims=True) + EPS)
            n = x * r
            _acc(step, ao[o], _colsum(d * n))
            dn = d * fi[o][...]
            ro[0][:, o * R:(o + 1) * R] = (r * (dn - n * jnp.mean(dn * n, axis=-1, keepdims=True))).astype(BF)
        dk = ri[3][...]
        ro[0][:, 2 * R:2 * R + 128] = (dk * ri[4][...] - _rot_half(dk * ri[5][...])).astype(BF)

    return rowwise(body, [zs, dcq, dckv, dkr, cos, sin], [gq, gkv], [(2 * R + 128, BF)], [(1, R), (1, R)],
                   tr=tr, name=name)


def mix_fwd(zg, yc, ya, bg, bco, *, tr, name):
    D = yc.shape[1]

    def body(step, ri, fi, ro, ao):
        g = _sigmoid(ri[0][...] + fi[0][...])
        ro[0][...] = (g[:, :D] * (ri[1][...] + fi[1][...]) + g[:, D:] * ri[2][...]).astype(BF)

    return rowwise(body, [zg, yc, ya], [bg, bco], [(D, BF)], [], tr=tr, name=name)[0]


def mix_bwd(zg, yc, ya, dmix, bg, bco, *, tr, name):
    D = yc.shape[1]

    def body(step, ri, fi, ro, ao):
        g = _sigmoid(ri[0][...] + fi[0][...])
        gc, ga = g[:, :D], g[:, D:]
        ycv = ri[1][...] + fi[1][...]
        yav = ri[2][...]
        dm = ri[3][...]
        dyc = dm * gc
        ro[0][...] = dyc.astype(BF)
        ro[1][...] = (dm * ga).astype(BF)
        dzc = dm * ycv * gc * (1.0 - gc)
        dza = dm * yav * ga * (1.0 - ga)
        ro[2][:, :D] = dzc.astype(BF)
        ro[2][:, D:] = dza.astype(BF)
        _acc(step, ao[0], _colsum(dzc))
        _acc(step, ao[1], _colsum(dza))
        _acc(step, ao[2], _colsum(dyc))

    dyc, dya, dzg, dbc, dba, dbo = rowwise(body, [zg, yc, ya, dmix], [bg, bco], [(D, BF), (D, BF), (2 * D, BF)],
                                           [(1, D), (1, D), (1, D)], tr=tr, name=name)
    return dyc, dya, dzg, jnp.concatenate([dbc, dba], axis=1), dbo


def ffn_fwd(hn, wg, wu, *, tm, name):
    T, D = hn.shape
    nb, _, FB = wg.shape
    assert T % tm == 0

    def body(h_ref, g_ref, u_ref, a_ref, b_ref, f_ref):
        h = h_ref[...]
        a = jnp.dot(h, g_ref[...], preferred_element_type=F32)
        b = jnp.dot(h, u_ref[...], preferred_element_type=F32)
        a_ref[...] = a
        b_ref[...] = b
        f_ref[...] = (a * _sigmoid(a) * b).astype(BF)

    wspec = pl.BlockSpec((None, D, FB), lambda j, i: (j, 0, 0))
    ospec = pl.BlockSpec((tm, FB), lambda j, i: (i, j))
    return pl.pallas_call(
        body, name=name, grid=(nb, T // tm),
        in_specs=[pl.BlockSpec((tm, D), lambda j, i: (i, 0)), wspec, wspec], out_specs=[ospec, ospec, ospec],
        out_shape=[jax.ShapeDtypeStruct((T, nb * FB), F32), jax.ShapeDtypeStruct((T, nb * FB), F32),
                   jax.ShapeDtypeStruct((T, nb * FB), BF)],
        compiler_params=_cparams(("parallel", "parallel")),
    )(hn, wg, wu)


def ffn_dact(dh, wd, a, b, *, tm, name):
    T, D = dh.shape
    FF = wd.shape[0]
    FB = FF // N_CHIPS

    def body(dh_ref, w_ref, a_ref, b_ref, da_ref, db_ref):
        d = lax.dot_general(dh_ref[...], w_ref[...], (((1,), (1,)), ((), ())), preferred_element_type=F32)
        x, u = a_ref[...], b_ref[...]
        sg = _sigmoid(x)
        da_ref[...] = (d * u * (sg * (1.0 + x * (1.0 - sg)))).astype(BF)
        db_ref[...] = (d * (x * sg)).astype(BF)

    blk = pl.BlockSpec((tm, FB), lambda j, i: (i, j))
    return pl.pallas_call(
        body, name=name, grid=(N_CHIPS, T // tm),
        in_specs=[pl.BlockSpec((tm, D), lambda j, i: (i, 0)), pl.BlockSpec((FB, D), lambda j, i: (j, 0)), blk, blk],
        out_specs=[blk, blk],
        out_shape=[jax.ShapeDtypeStruct((T, FF), BF), jax.ShapeDtypeStruct((T, FF), BF)],
        compiler_params=_cparams(("parallel", "parallel")),
    )(dh, wd, a, b)


def ffn_dhn(da, db, wg, wu, *, tm, tn, name, deps=()):
    T, FF = da.shape
    nb, D, FB = wg.shape
    dims = (((1,), (1,)), ((), ()))

    def body(da_ref, db_ref, g_ref, u_ref, *rest):
        o_ref = rest[len(deps)]
        acc = None
        for k in range(nb):
            cols = slice(k * FB, (k + 1) * FB)
            p = lax.dot_general(da_ref[:, cols], g_ref[k], dims, preferred_element_type=F32)
            p = p + lax.dot_general(db_ref[:, cols], u_ref[k], dims, preferred_element_type=F32)
            acc = p if acc is None else acc + p
        o_ref[...] = acc

    aspec = pl.BlockSpec((tm, FF), lambda i, j: (i, 0))
    wspec = pl.BlockSpec((nb, tn, FB), lambda i, j: (0, j, 0))
    return pl.pallas_call(
        body, name=name, grid=(T // tm, D // tn),
        in_specs=[aspec, aspec, wspec, wspec] + [pl.BlockSpec(memory_space=pl.ANY)] * len(deps),
        out_specs=pl.BlockSpec((tm, tn), lambda i, j: (i, j)),
        out_shape=jax.ShapeDtypeStruct((T, D), F32),
        compiler_params=_cparams(("parallel", "parallel")),
    )(da, db, wg, wu, *deps)


def final_loss(h2, tgt, g, *, seq, tr, name):
    D = h2.shape[1]

    def body(step, ri, fi, ro, ao):
        x = ri[0][...]
        r = lax.rsqrt(jnp.mean(x * x, axis=-1, keepdims=True) + EPS)
        n = x * r
        gg = fi[0][...]
        row = lax.broadcasted_iota(jnp.int32, (tr, 1), 0) + step * tr
        live = jnp.logical_and(row >= N_META, row < N_META + seq)
        e = jnp.where(live, n * gg - ri[1][...], 0.0)
        _acc(step, ao[0], jnp.broadcast_to(0.5 * jnp.sum(jnp.mean(e * e, axis=-1, keepdims=True)), (1, 128)))
        dy = e * (1.0 / D)
        _acc(step, ao[1], _colsum(dy * n))
        dn = dy * gg
        dh = r * (dn - n * jnp.mean(dn * n, axis=-1, keepdims=True))
        ro[0][...] = dh
        ro[1][...] = dh.astype(BF)

    return rowwise(body, [h2, tgt], [g], [(D, F32), (D, BF)], [(1, 128), (1, D)], tr=tr, name=name)


def _shifted_copies(s_ref, T, rc):
    for r in range(1, 8):
        for t0 in range(0, T, rc):
            s_ref[r, t0:t0 + rc, :] = s_ref[0, t0 + r:t0 + r + rc, :]
        s_ref[r, T:T + CONV_PAD - 8, :] = s_ref[0, T + r:T + r + CONV_PAD - 8, :]


def _shifted(s_ref, start, rows):
    return s_ref[start % 8, start - start % 8:start - start % 8 + rows, :]


def dwconv_fwd(x, w, b, *, tc, rc, name, deps=()):
    T, C = x.shape
    nchunk = T // rc
    assert nchunk * rc == T and C % tc == 0

    def body(x_ref, w_ref, b_ref, *rest):
        y_ref, xs_ref = rest[len(deps):]
        xs_ref[0, 0:CONV_PAD, :] = jnp.zeros((CONV_PAD, tc), F32)
        xs_ref[0, CONV_PAD:CONV_PAD + T, :] = x_ref[...]
        _shifted_copies(xs_ref, T, rc)
        off = CONV_PAD - (CONV_WIDTH - 1)
        for ci in range(nchunk):
            t0 = ci * rc
            acc = jnp.broadcast_to(b_ref[...], (rc, tc))
            for j in range(CONV_WIDTH):
                acc = acc + w_ref[j:j + 1, :] * _shifted(xs_ref, t0 + off + j, rc)
            y_ref[t0:t0 + rc, :] = acc

    return pl.pallas_call(
        body, name=name, grid=(C // tc,),
        in_specs=[pl.BlockSpec((T, tc), lambda j: (0, j)), pl.BlockSpec((32, tc), lambda j: (0, j)),
                  pl.BlockSpec((1, tc), lambda j: (0, j))] + [pl.BlockSpec(memory_space=pl.ANY)] * len(deps),
        out_specs=pl.BlockSpec((T, tc), lambda j: (0, j)),
        out_shape=jax.ShapeDtypeStruct((T, C), F32),
        scratch_shapes=[pltpu.VMEM((8, T + CONV_PAD, tc), F32)],
        compiler_params=_cparams(("parallel",)),
    )(x, w, b, *deps)


def dwconv_bwd(x, dy, w, *, tc, rc, name):
    T, C = x.shape
    nchunk = T // rc
    assert nchunk * rc == T and C % tc == 0 and rc % 8 == 0

    def body(x_ref, dy_ref, w_ref, dx_ref, dw_ref, xs_ref, ds_ref):
        off = CONV_PAD - (CONV_WIDTH - 1)
        xs_ref[0, 0:CONV_PAD, :] = jnp.zeros((CONV_PAD, tc), F32)
        xs_ref[0, CONV_PAD:CONV_PAD + T, :] = x_ref[...]
        _shifted_copies(xs_ref, T, rc)
        ds_ref[0, 0:T, :] = dy_ref[...]
        ds_ref[0, T:T + CONV_PAD, :] = jnp.zeros((CONV_PAD, tc), F32)
        _shifted_copies(ds_ref, T, rc)
        for ci in range(nchunk):
            t0 = ci * rc
            acc = jnp.zeros((rc, tc), F32)
            for j in range(CONV_WIDTH):
                acc = acc + w_ref[j:j + 1, :] * _shifted(ds_ref, t0 + (CONV_WIDTH - 1) - j, rc)
            dx_ref[t0:t0 + rc, :] = acc
        dw_ref[...] = jnp.zeros((32, tc), F32)
        for j in range(CONV_WIDTH):
            acc = jnp.zeros((8, tc), F32)
            for ci in range(nchunk):
                t0 = ci * rc
                pr = dy_ref[t0:t0 + rc, :] * _shifted(xs_ref, t0 + off + j, rc)
                acc = acc + jnp.sum(pr.reshape(rc // 8, 8, tc), axis=0)
            dw_ref[j:j + 1, :] = jnp.sum(acc, axis=0, keepdims=True)

    return pl.pallas_call(
        body, name=name, grid=(C // tc,),
        in_specs=[pl.BlockSpec((T, tc), lambda j: (0, j)), pl.BlockSpec((T, tc), lambda j: (0, j)),
                  pl.BlockSpec((32, tc), lambda j: (0, j))],
        out_specs=[pl.BlockSpec((T, tc), lambda j: (0, j)), pl.BlockSpec((32, tc), lambda j: (0, j))],
        out_shape=[jax.ShapeDtypeStruct((T, C), F32), jax.ShapeDtypeStruct((32, C), F32)],
        scratch_shapes=[pltpu.VMEM((8, T + CONV_PAD, tc), F32), pltpu.VMEM((8, T + CONV_PAD, tc), F32)],
        compiler_params=_cparams(("parallel",)),
    )(x, dy, w)


def q_proj(cq, wq, cos, sin, *, scale, name):
    T, R = cq.shape
    H = wq.shape[1] // HEAD_PAD

    def body(a_ref, b_ref, c_ref, s_ref, o_ref):
        p = jnp.dot(a_ref[...], b_ref[...], preferred_element_type=F32)
        o_ref[:, :QK_NOPE] = (p[:, :QK_NOPE] * scale).astype(BF)
        x = p[:, QK_NOPE:]
        o_ref[:, QK_NOPE:] = ((x * c_ref[...] + _rot_half(x) * s_ref[...]) * scale).astype(BF)

    return pl.pallas_call(
        body, name=name, grid=(H,),
        in_specs=[pl.BlockSpec((T, R), lambda h: (0, 0)), pl.BlockSpec((R, HEAD_PAD), lambda h: (0, h)),
                  pl.BlockSpec((T, 128), lambda h: (0, 0)), pl.BlockSpec((T, 128), lambda h: (0, 0))],
        out_specs=pl.BlockSpec((T, HEAD_PAD), lambda h: (0, h)),
        out_shape=jax.ShapeDtypeStruct((T, H * HEAD_PAD), BF),
        compiler_params=_cparams(("parallel",)),
    )(cq, wq, cos, sin)


def k_proj(ckv, wk, kr, *, name):
    T, R = ckv.shape
    H = wk.shape[1] // QK_NOPE

    def body(a_ref, b_ref, kr_ref, o_ref):
        p = jnp.dot(a_ref[...], b_ref[...], preferred_element_type=F32)
        o_ref[:, :QK_NOPE] = p.astype(BF)
        o_ref[:, QK_NOPE:] = kr_ref[...].astype(BF)

    return pl.pallas_call(
        body, name=name, grid=(H,),
        in_specs=[pl.BlockSpec((T, R), lambda h: (0, 0)), pl.BlockSpec((R, QK_NOPE), lambda h: (0, h)),
                  pl.BlockSpec((T, 128), lambda h: (0, 0))],
        out_specs=pl.BlockSpec((T, HEAD_PAD), lambda h: (0, h)),
        out_shape=jax.ShapeDtypeStruct((T, H * HEAD_PAD), BF),
        compiler_params=_cparams(("parallel",)),
    )(ckv, wk, kr)


def _causal_scores(q, k_ref, qi, CH):
    nt = (((1,), (1,)), ((), ()))
    pieces = []
    if qi > 0:
        rows = slice(0, qi * CH)
        pieces.append((rows, lax.dot_general(q, k_ref[rows, :], nt, preferred_element_type=F32)))
    rows = slice(qi * CH, (qi + 1) * CH)
    s = lax.dot_general(q, k_ref[rows, :], nt, preferred_element_type=F32)
    keep = lax.broadcasted_iota(jnp.int32, s.shape, 1) <= lax.broadcasted_iota(jnp.int32, s.shape, 0)
    pieces.append((rows, jnp.where(keep, s, -1e30)))
    return pieces


def attn_fwd(q, k, v, *, nch, name):
    T = q.shape[0]
    H = q.shape[1] // HEAD_PAD
    CH = T // nch
    assert CH * nch == T and CH % 16 == 0

    def body(q_ref, k_ref, v_ref, o_ref, lse_ref):
        for qi in range(nch):
            L = (qi + 1) * CH
            pieces = _causal_scores(q_ref[qi * CH:L, :], k_ref, qi, CH)
            m = functools.reduce(jnp.maximum, [jnp.max(s, axis=1, keepdims=True) for _, s in pieces])
            l, o = 0.0, 0.0
            for rows, s in pieces:
                p = jnp.exp(s - m)
                l = l + jnp.sum(p, axis=1, keepdims=True)
                o = o + jnp.dot(p.astype(BF), v_ref[rows, :], preferred_element_type=F32)
            o_ref[qi * CH:L, :] = (o / l).astype(BF)
            lse_ref[qi * CH:L, :] = jnp.broadcast_to(m + jnp.log(l), (CH, V_HEAD))

    return pl.pallas_call(
        body, name=name, grid=(H,),
        in_specs=[pl.BlockSpec((T, HEAD_PAD), lambda h: (0, h)), pl.BlockSpec((T, HEAD_PAD), lambda h: (0, h)),
                  pl.BlockSpec((T, V_HEAD), lambda h: (0, h))],
        out_specs=[pl.BlockSpec((T, V_HEAD), lambda h: (0, h)), pl.BlockSpec((T, V_HEAD), lambda h: (0, h))],
        out_shape=[jax.ShapeDtypeStruct((T, H * V_HEAD), BF), jax.ShapeDtypeStruct((T, H * V_HEAD), F32)],
        compiler_params=_cparams(("parallel",)),
    )(q, k, v)


def attn_bwd(q, k, v, o, do, lse, cos, sin, *, scale, nch, name):
    T = q.shape[0]
    H = q.shape[1] // HEAD_PAD
    CH = T // nch
    tn_dims = (((0,), (0,)), ((), ()))
    nt_dims = (((1,), (1,)), ((), ()))

    def body(q_ref, k_ref, v_ref, o_ref, do_ref, lse_ref, c_ref, s_ref,
             dq_ref, dk_ref, dv_ref, dkr_ref, dk_acc, dv_acc):
        h = pl.program_id(0)
        dk_acc[...] = jnp.zeros_like(dk_acc)
        dv_acc[...] = jnp.zeros_like(dv_acc)
        for qi in range(nch):
            L = (qi + 1) * CH
            rows = slice(qi * CH, L)
            qc, doc = q_ref[rows, :], do_ref[rows, :]
            lse_c = lse_ref[rows, 0:1]
            dsum = jnp.sum(doc.astype(F32) * o_ref[rows, :].astype(F32), axis=1, keepdims=True)
            dq = 0.0
            for krows, s in _causal_scores(qc, k_ref, qi, CH):
                p = jnp.exp(s - lse_c)
                dp = lax.dot_general(doc, v_ref[krows, :], nt_dims, preferred_element_type=F32)
                ds = (p * (dp - dsum)).astype(BF)
                dq = dq + jnp.dot(ds, k_ref[krows, :], preferred_element_type=F32)
                dk_acc[krows, :] += lax.dot_general(ds, qc, tn_dims, preferred_element_type=F32)
                dv_acc[krows, :] += lax.dot_general(p.astype(BF), doc, tn_dims, preferred_element_type=F32)
            dq_ref[rows, :QK_NOPE] = (dq[:, :QK_NOPE] * scale).astype(BF)
            dyr = dq[:, QK_NOPE:]
            dq_ref[rows, QK_NOPE:] = ((dyr * c_ref[rows, :] - _rot_half(dyr * s_ref[rows, :])) * scale).astype(BF)
        dk_ref[:, :QK_NOPE] = dk_acc[:, :QK_NOPE].astype(BF)
        dk_ref[:, QK_NOPE:] = jnp.zeros((T, HEAD_PAD - QK_NOPE), BF)
        dv_ref[...] = dv_acc[...].astype(BF)

        @pl.when(h == 0)
        def _():
            dkr_ref[...] = dk_acc[:, QK_NOPE:]

        @pl.when(h > 0)
        def _():
            dkr_ref[...] += dk_acc[:, QK_NOPE:]

    hp = pl.BlockSpec((T, HEAD_PAD), lambda h: (0, h))
    hv = pl.BlockSpec((T, V_HEAD), lambda h: (0, h))
    tab = pl.BlockSpec((T, 128), lambda h: (0, 0))
    return pl.pallas_call(
        body, name=name, grid=(H,),
        in_specs=[hp, hp, hv, hv, hv, hv, tab, tab],
        out_specs=[hp, hp, hv, tab],
        out_shape=[jax.ShapeDtypeStruct((T, H * HEAD_PAD), BF), jax.ShapeDtypeStruct((T, H * HEAD_PAD), BF),
                   jax.ShapeDtypeStruct((T, H * V_HEAD), BF), jax.ShapeDtypeStruct((T, 128), F32)],
        scratch_shapes=[pltpu.VMEM((T, HEAD_PAD), F32), pltpu.VMEM((T, V_HEAD), F32)],
        compiler_params=_cparams(("arbitrary",)),
    )(q, k, v, o, do, lse, cos, sin)


def _place():
    x, y, c = lax.axis_index("x"), lax.axis_index("y"), lax.axis_index("c")
    return x, y, c


def _other_chips(x, y):
    return [(1 - x, y), (x, 1 - y), (1 - x, 1 - y)]


def gather_devices(v, *, name):
    R, C = v.shape

    def body(v_ref, o_ref, send_sems, recv_sems):
        x, y, c = _place()
        me = 4 * x + 2 * y + c
        o_ref[me] = v_ref[...]
        copies = []
        for k in range(1, N_DEV):
            fx, fy, fc = (k >> 2) & 1, (k >> 1) & 1, k & 1
            to = (x ^ fx, y ^ fy, c ^ fc)
            cp = pltpu.make_async_remote_copy(src_ref=v_ref, dst_ref=o_ref.at[me], send_sem=send_sems.at[k - 1],
                                              recv_sem=recv_sems.at[k - 1], device_id=to, device_id_type=MESH)
            cp.start()
            copies.append(cp)
        for k in range(1, N_DEV):
            fx, fy, fc = (k >> 2) & 1, (k >> 1) & 1, k & 1
            frm = 4 * (x ^ fx) + 2 * (y ^ fy) + (c ^ fc)
            pltpu.make_async_remote_copy(src_ref=v_ref, dst_ref=o_ref.at[frm], send_sem=send_sems.at[k - 1],
                                         recv_sem=recv_sems.at[k - 1], device_id=(x, y, c),
                                         device_id_type=MESH).wait_recv()
        for cp in copies:
            cp.wait_send()

    return pl.pallas_call(
        body, name=name, out_shape=jax.ShapeDtypeStruct((N_DEV, R, C), F32),
        in_specs=[pl.BlockSpec(memory_space=pltpu.VMEM)], out_specs=pl.BlockSpec(memory_space=pltpu.VMEM),
        scratch_shapes=[pltpu.SemaphoreType.DMA((N_DEV - 1,)), pltpu.SemaphoreType.DMA((N_DEV - 1,))],
    )(v)


HBM_SPEC = pl.BlockSpec(memory_space=pltpu.HBM)
SEM_SPEC = pl.BlockSpec(memory_space=pltpu.SEMAPHORE)
DATAFLOW = pltpu.SideEffectType.DATAFLOW_SIDE_EFFECTING


def _hbm(a):
    return pltpu.with_memory_space_constraint(a, pltpu.HBM)


def _half(ref, lead, c, axis):
    rows, cols = ref.shape[-2], ref.shape[-1]
    if axis == 0:
        return ref.at[(*lead, pl.ds(c * (rows // 2), rows // 2))]
    return ref.at[(*lead, slice(None), pl.ds(c * (cols // 2), cols // 2))]


def copies_start(arrs, plan, n_sems, after, *, name):
    n = len(arrs)

    def body(*refs):
        outs = refs[n + 1:2 * n + 1]
        send_sems, recv_sems, token = refs[2 * n + 1:]
        for cp in plan(outs, send_sems, recv_sems, False):
            cp.start()
        token[...] = jnp.zeros_like(token)

    res = pl.pallas_call(
        body, name=name,
        out_shape=tuple(pltpu.HBM(a.shape, a.dtype) for a in arrs)
        + (pltpu.SemaphoreType.DMA((n_sems,)), pltpu.SemaphoreType.DMA((n_sems,)), jax.ShapeDtypeStruct((8, 128), F32)),
        in_specs=(HBM_SPEC,) * n + (pl.BlockSpec(memory_space=pl.ANY),),
        out_specs=(HBM_SPEC,) * n + (SEM_SPEC, SEM_SPEC, pl.BlockSpec(memory_space=pltpu.VMEM)),
        input_output_aliases={i: i for i in range(n)},
        compiler_params=pltpu.CompilerParams(has_side_effects=DATAFLOW),
    )(*[_hbm(a) for a in arrs], after)
    return list(res[:n]), res[n], res[n + 1], res[n + 2]


def copies_wait(arrs, plan, send_sems, recv_sems, after, *, name):
    n = len(arrs)
    afters = list(after) if isinstance(after, (list, tuple)) else [after]

    def body(*refs):
        for cp in plan(refs[:n], refs[n], refs[n + 1], True):
            cp.wait_send()
            cp.wait_recv()

    res = pl.pallas_call(
        body, name=name,
        out_shape=tuple(pltpu.HBM(a.shape, a.dtype) for a in arrs),
        in_specs=(HBM_SPEC,) * n + (SEM_SPEC, SEM_SPEC) + (pl.BlockSpec(memory_space=pl.ANY),) * len(afters),
        out_specs=(HBM_SPEC,) * n,
        input_output_aliases={i: i for i in range(n)},
        compiler_params=pltpu.CompilerParams(has_side_effects=DATAFLOW),
    )(*arrs, send_sems, recv_sems, *afters)
    return list(res)


def gather_plan(axes):
    def plan(bufs, send_sems, recv_sems, receiving):
        x, y, c = _place()
        me = 2 * x + y
        out = []
        for i, b in enumerate(bufs):
            for j, (cx, cy) in enumerate(_other_chips(x, y)):
                src = _half(b, (me,), c, axes[i])
                dst = _half(b, (2 * cx + cy,), c, axes[i]) if receiving else src
                out.append(pltpu.make_async_remote_copy(
                    src_ref=src, dst_ref=dst, send_sem=send_sems.at[3 * i + j], recv_sem=recv_sems.at[3 * i + j],
                    device_id=(cx, cy, c), device_id_type=MESH))
        return out
    return plan


def forward_plan(axes):
    def plan(bufs, send_sems, recv_sems, receiving):
        x, y, c = _place()
        out = []
        for i, b in enumerate(bufs):
            for j, (cx, cy) in enumerate(_other_chips(x, y)):
                blk = _half(b, (2 * cx + cy,), (1 - c) if receiving else c, axes[i])
                out.append(pltpu.make_async_remote_copy(
                    src_ref=blk, dst_ref=blk, send_sem=send_sems.at[3 * i + j], recv_sem=recv_sems.at[3 * i + j],
                    device_id=(x, y, 1 - c), device_id_type=MESH))
        return out
    return plan


def scatter_plan(n):
    def plan(arrs, send_sems, recv_sems, receiving):
        x, y, c = _place()
        out = []
        for i in range(n):
            for j, (cx, cy) in enumerate(_other_chips(x, y)):
                out.append(pltpu.make_async_remote_copy(
                    src_ref=arrs[i].at[2 * cx + cy], dst_ref=arrs[n + i].at[j], send_sem=send_sems.at[3 * i + j],
                    recv_sem=recv_sems.at[3 * i + j], device_id=(cx, cy, c), device_id_type=MESH))
        return out
    return plan


def exchange_plan(axes):
    n = len(axes)

    def plan(arrs, send_sems, recv_sems, receiving):
        x, y, c = _place()
        out = []
        for i in range(n):
            for k in range(N_CHIPS):
                out.append(pltpu.make_async_remote_copy(
                    src_ref=_half(arrs[i], (k,), 1 - c, axes[i]), dst_ref=arrs[n + i].at[k],
                    send_sem=send_sems.at[N_CHIPS * i + k], recv_sem=recv_sems.at[N_CHIPS * i + k],
                    device_id=(x, y, 1 - c), device_id_type=MESH))
        return out
    return plan


def devices_plan(arrs, send_sems, recv_sems, receiving):
    x, y, c = _place()
    out = []
    for k in range(1, N_DEV):
        px, py, pc = x ^ ((k >> 2) & 1), y ^ ((k >> 1) & 1), c ^ (k & 1)
        slot = (4 * px + 2 * py + pc) if receiving else (4 * x + 2 * y + c)
        out.append(pltpu.make_async_remote_copy(
            src_ref=arrs[0], dst_ref=arrs[1].at[slot], send_sem=send_sems.at[k - 1], recv_sem=recv_sems.at[k - 1],
            device_id=(px, py, pc), device_id_type=MESH))
    return out


def share_halves(bufs, axes, after, *, name):
    n = len(bufs)

    def body(*refs):
        outs = refs[n + 1:2 * n + 1]
        send_sems, recv_sems = refs[2 * n + 1:]
        x, y, c = _place()
        sib = (x, y, 1 - c)
        cps = []
        for i in range(n):
            mine = _half(outs[i], (), c, axes[i])
            cp = pltpu.make_async_remote_copy(src_ref=mine, dst_ref=mine, send_sem=send_sems.at[i],
                                              recv_sem=recv_sems.at[i], device_id=sib, device_id_type=MESH)
            cp.start()
            cps.append(cp)
        for i in range(n):
            theirs = _half(outs[i], (), 1 - c, axes[i])
            pltpu.make_async_remote_copy(src_ref=theirs, dst_ref=theirs, send_sem=send_sems.at[i],
                                         recv_sem=recv_sems.at[i], device_id=sib, device_id_type=MESH).wait_recv()
        for cp in cps:
            cp.wait_send()

    anyspec = pl.BlockSpec(memory_space=pl.ANY)
    return pl.pallas_call(
        body, name=name,
        out_shape=[jax.ShapeDtypeStruct(b.shape, b.dtype) for b in bufs],
        in_specs=[anyspec] * (n + 1), out_specs=[anyspec] * n,
        input_output_aliases={i: i for i in range(n)},
        scratch_shapes=[pltpu.SemaphoreType.DMA((n,)), pltpu.SemaphoreType.DMA((n,))],
    )(*bufs, after)


def _row_tile(r, cols, itemsize, target_bytes=2 * 1024 * 1024):
    best = None
    for t in range(16, r + 1, 16):
        if r % t == 0 and t * cols * itemsize <= target_bytes:
            best = t
    return best if best is not None else r


def pair_sum(g, recv, half, axis, *, name):
    _, hr, hc = recv.shape
    tr = _row_tile(hr, hc, 4)
    nb = hr // tr
    if axis == 0:
        mine = pl.BlockSpec((None, tr, hc), lambda k, r, hf: (k, hf[0] * nb + r, 0))
    else:
        mine = pl.BlockSpec((None, tr, hc), lambda k, r, hf: (k, r, hf[0]))

    def body(half_ref, a_ref, b_ref, o_ref):
        o_ref[...] = (a_ref[...].astype(F32) + b_ref[...].astype(F32)).astype(BF)

    return pl.pallas_call(
        body, name=name,
        grid_spec=pltpu.PrefetchScalarGridSpec(
            num_scalar_prefetch=1, grid=(N_CHIPS, nb),
            in_specs=[mine, pl.BlockSpec((None, tr, hc), lambda k, r, hf: (k, r, 0))],
            out_specs=pl.BlockSpec((None, tr, hc), lambda k, r, hf: (k, r, 0))),
        out_shape=jax.ShapeDtypeStruct((N_CHIPS, hr, hc), BF),
        compiler_params=_cparams(("parallel", "parallel")),
    )(half, g, recv)


def chip_sum(s, recv, place, axis, *, name):
    _, hr, hc = s.shape
    tr = _row_tile(hr, hc, 4)
    nb = hr // tr
    if axis == 0:
        o_spec = pl.BlockSpec((tr, hc), lambda r, pc: (pc[1] * nb + r, 0))
        o_shape = (2 * hr, hc)
    else:
        o_spec = pl.BlockSpec((tr, hc), lambda r, pc: (r, pc[1]))
        o_shape = (hr, 2 * hc)

    def body(place_ref, a_ref, b_ref, o_ref):
        acc = a_ref[...].astype(F32)
        for j in range(3):
            acc = acc + b_ref[j].astype(F32)
        o_ref[...] = acc

    return pl.pallas_call(
        body, name=name,
        grid_spec=pltpu.PrefetchScalarGridSpec(
            num_scalar_prefetch=1, grid=(nb,),
            in_specs=[pl.BlockSpec((None, tr, hc), lambda r, pc: (pc[0], r, 0)),
                      pl.BlockSpec((3, tr, hc), lambda r, pc: (0, r, 0))],
            out_specs=o_spec),
        out_shape=jax.ShapeDtypeStruct(o_shape, F32),
        compiler_params=_cparams(("parallel",)),
    )(place, s, recv)


def sum_devices(v, *, name):
    _, R, C = v.shape

    def body(v_ref, o_ref):
        acc = v_ref[0]
        for k in range(1, N_DEV):
            acc = acc + v_ref[k]
        o_ref[...] = acc

    return pl.pallas_call(body, name=name, out_shape=jax.ShapeDtypeStruct((R, C), F32))(v)


def adamw(w, g, m, v, *, name):
    R, C = w.shape
    tr = _row_tile(R, C, 4, target_bytes=1024 * 1024)
    c1 = 1.0 / (1.0 - ADAM_B1 ** ADAM_STEP)
    c2 = 1.0 / (1.0 - ADAM_B2 ** ADAM_STEP)

    def body(w_ref, g_ref, m_ref, v_ref, go_ref, d_ref, nm_ref, nv_ref):
        gg = g_ref[...]
        go_ref[...] = gg
        nm = ADAM_B1 * m_ref[...] + (1.0 - ADAM_B1) * gg
        nv = ADAM_B2 * v_ref[...] + (1.0 - ADAM_B2) * (gg * gg)
        nm_ref[...] = nm
        nv_ref[...] = nv
        d_ref[...] = -ADAM_LR * ((nm * c1) / (jnp.sqrt(nv * c2) + ADAM_EPS) + ADAM_WD * w_ref[...])

    spec = pl.BlockSpec((tr, C), lambda i: (i, 0))
    return pl.pallas_call(
        body, name=name, grid=(R // tr,), in_specs=[spec] * 4, out_specs=[spec] * 4,
        out_shape=[jax.ShapeDtypeStruct((R, C), F32)] * 4, compiler_params=_cparams(("parallel",)),
    )(w, g, m, v)


def _rope_tables(T):
    pos = np.arange(T, dtype=np.float32)
    inv_freq = (ROPE_THETA ** (-np.arange(0, QK_ROPE, 2, dtype=np.float32) / QK_ROPE)).astype(np.float32)
    ang = pos[:, None] * inv_freq[None, :]
    z = np.zeros((T, 64), np.float32)
    cos = np.concatenate([np.cos(ang), np.cos(ang), z], axis=1).astype(np.float32)
    sin = np.concatenate([np.sin(ang), np.sin(ang), z], axis=1).astype(np.float32)
    return jnp.asarray(cos), jnp.asarray(sin)


def _pick(n, cands):
    for c in cands:
        if n % c == 0:
            return c
    return n


def kernel(x, meta_tokens, g_mix, w_in, b_glu, b_gate, w_dw, b_dw, g_conv_ln, b_conv_ln, w_conv_out, b_conv_out, g_q_lora, w_uq, g_kv_lora, w_uk, w_uv, w_attn_out, w_out, g_ffn, w_ffn_gate, w_ffn_up, w_ffn_down, g_final, loss_target, m_meta_tokens, m_g_mix, m_w_in, m_b_glu, m_b_gate, m_w_dw, m_b_dw, m_g_conv_ln, m_b_conv_ln, m_w_conv_out, m_b_conv_out, m_g_q_lora, m_w_uq, m_g_kv_lora, m_w_uk, m_w_uv, m_w_attn_out, m_w_out, m_g_ffn, m_w_ffn_gate, m_w_ffn_up, m_w_ffn_down, m_g_final, v_meta_tokens, v_g_mix, v_w_in, v_b_glu, v_b_gate, v_w_dw, v_b_dw, v_g_conv_ln, v_b_conv_ln, v_w_conv_out, v_b_conv_out, v_g_q_lora, v_w_uq, v_g_kv_lora, v_w_uk, v_w_uv, v_w_attn_out, v_w_out, v_g_ffn, v_w_ffn_gate, v_w_ffn_up, v_w_ffn_down, v_g_final):
    weights = dict(meta_tokens=meta_tokens, g_mix=g_mix, w_in=w_in, b_glu=b_glu, b_gate=b_gate, w_dw=w_dw, b_dw=b_dw,
                   g_conv_ln=g_conv_ln, b_conv_ln=b_conv_ln, w_conv_out=w_conv_out, b_conv_out=b_conv_out,
                   g_q_lora=g_q_lora, w_uq=w_uq, g_kv_lora=g_kv_lora, w_uk=w_uk, w_uv=w_uv, w_attn_out=w_attn_out,
                   w_out=w_out, g_ffn=g_ffn, w_ffn_gate=w_ffn_gate, w_ffn_up=w_ffn_up, w_ffn_down=w_ffn_down,
                   g_final=g_final)
    m_in = dict(meta_tokens=m_meta_tokens, g_mix=m_g_mix, w_in=m_w_in, b_glu=m_b_glu, b_gate=m_b_gate, w_dw=m_w_dw,
                b_dw=m_b_dw, g_conv_ln=m_g_conv_ln, b_conv_ln=m_b_conv_ln, w_conv_out=m_w_conv_out,
                b_conv_out=m_b_conv_out, g_q_lora=m_g_q_lora, w_uq=m_w_uq, g_kv_lora=m_g_kv_lora, w_uk=m_w_uk,
                w_uv=m_w_uv, w_attn_out=m_w_attn_out, w_out=m_w_out, g_ffn=m_g_ffn, w_ffn_gate=m_w_ffn_gate,
                w_ffn_up=m_w_ffn_up, w_ffn_down=m_w_ffn_down, g_final=m_g_final)
    v_in = dict(meta_tokens=v_meta_tokens, g_mix=v_g_mix, w_in=v_w_in, b_glu=v_b_glu, b_gate=v_b_gate, w_dw=v_w_dw,
                b_dw=v_b_dw, g_conv_ln=v_g_conv_ln, b_conv_ln=v_b_conv_ln, w_conv_out=v_w_conv_out,
                b_conv_out=v_b_conv_out, g_q_lora=v_g_q_lora, w_uq=v_w_uq, g_kv_lora=v_g_kv_lora, w_uk=v_w_uk,
                w_uv=v_w_uv, w_attn_out=v_w_attn_out, w_out=v_w_out, g_ffn=v_g_ffn, w_ffn_gate=v_w_ffn_gate,
                w_ffn_up=v_w_ffn_up, w_ffn_down=v_w_ffn_down, g_final=v_g_final)
    names = list(weights)

    seq, D = x.shape[1], x.shape[2]
    C = w_conv_out.shape[2]
    R = g_q_lora.shape[1]
    H = w_uv.shape[2] * N_CHIPS // V_HEAD
    FB = w_ffn_gate.shape[2]
    FF = FB * N_CHIPS
    length = N_META + seq
    T = -(-length // BLOCK_Q) * BLOCK_Q
    scale = (QK_NOPE + QK_ROPE) ** -0.5
    assert C == D and w_in.shape[2] * N_CHIPS == 2 * C + 2 * R + QK_ROPE + 2 * D

    xi, yi, ci = _place()
    chip = (2 * xi + yi).astype(jnp.int32)
    half_arr = ci.astype(jnp.int32).reshape(1)

    tr = _pick(T, (272, 256, 128))
    tmh = T // 2 if (T // 2) % 16 == 0 else T
    tmq = T // 4 if (T // 4) % 16 == 0 else T
    tmw = _pick(D, (1024, 512))
    nch = 4
    rc = _pick(T, (272, 256, 128))
    tcv = 128

    def tn_of(n, pref=512):
        return _pick(n, (pref, 384, 256, 128))

    grp_a = ["w_in"]
    grp_b = ["w_conv_out", "w_uq", "w_uk", "w_uv", "w_attn_out", "w_out"]
    grp_c = ["w_ffn_gate", "w_ffn_up", "w_ffn_down"]
    big = grp_a + grp_b + grp_c
    axes = {nm: 0 for nm in big}
    axes["w_in"] = 1
    shard = {nm: weights[nm][0] for nm in big}
    shard["w_in"] = jnp.swapaxes(w_in[0], 0, 1)

    def gather_begin(nms, srcs, after, tag):
        bufs = []
        for w in srcs:
            wb = w.astype(BF)
            bufs.append(lax.dynamic_update_slice(lax.empty((N_CHIPS,) + wb.shape, BF), wb[None], (chip, 0, 0)))
        ax = [axes[nm] for nm in nms]
        bufs, ssem, rsem, tok = copies_start(bufs, gather_plan(ax), 3 * len(nms), after, name=f"gather_{tag}_start")
        return (nms, ax, bufs, ssem, rsem), tok

    def gather_middle(state, after, tag):
        nms, ax, bufs, ssem, rsem = state
        bufs = copies_wait(bufs, gather_plan(ax), ssem, rsem, after, name=f"gather_{tag}_wait")
        bufs, ssem, rsem, tok = copies_start(bufs, forward_plan(ax), 3 * len(nms), half_arr,
                                             name=f"gather_{tag}_forward_start")
        return (nms, ax, bufs, ssem, rsem), tok

    def gather_end(state, after, tag):
        nms, ax, bufs, ssem, rsem = state
        return dict(zip(nms, copies_wait(bufs, forward_plan(ax), ssem, rsem, after, name=f"gather_{tag}_forward_wait")))

    tiny = gather_devices(jnp.concatenate([meta_tokens, jnp.pad(w_dw[0], ((0, 1), (0, 0)))], axis=0),
                          name="gather_meta_w_dw")
    meta_full = jnp.concatenate([tiny[2 * k, :N_META] for k in range(N_CHIPS)], axis=1)
    w_dw_full = jnp.concatenate([tiny[2 * k, N_META:] for k in range(N_CHIPS)], axis=1)

    st_ga, tok_ga = gather_begin(grp_a, [shard["w_in"]], tiny, "a")
    tok_ga, held = lax.optimization_barrier((tok_ga, (x, loss_target, [shard[nm] for nm in grp_b + grp_c])))
    x_l, tgt_l, shard_bc = held
    h0 = jnp.concatenate([meta_full, x_l[0], jnp.zeros((T - length, D), F32)], axis=0)
    tgt = jnp.pad(tgt_l[0], ((N_META, T - length), (0, 0)))
    cos, sin = _rope_tables(T)

    st_ga, tok_ga = gather_middle(st_ga, h0, "a")
    G = gather_end(st_ga, tok_ga, "a")
    st_gb, tok_gb = gather_begin(grp_b, shard_bc[:len(grp_b)], G["w_in"], "b")
    st_gc, tok_gc = gather_begin(grp_c, shard_bc[len(grp_b):], tok_gb, "c")

    WT_in = G["w_in"].reshape(N_CHIPS * G["w_in"].shape[1], D)
    o1, o2 = 2 * C, 2 * C + 2 * R + QK_ROPE
    WT_glu = WT_in[:o1]
    WT_small = jnp.pad(WT_in[o1:o2], ((0, 128 - QK_ROPE), (0, 0)))
    WT_gate = WT_in[o2:]

    u = rms_fwd(h0, g_mix, tr=tr, name="rms_mix", deps=[tok_gb, tok_gc])
    z_glu = mm(u, WT_glu, mode="nt", tm=T, tn=tn_of(2 * C), tk=D, out_dtype=F32, name="mm_z_glu")
    z_small = mm(u, WT_small, mode="nt", tm=T, tn=tn_of(2 * R + 128, 384), tk=D, out_dtype=F32, name="mm_z_small")
    z_gate = mm(u, WT_gate, mode="nt", tm=T, tn=tn_of(2 * D), tk=D, out_dtype=F32, name="mm_z_gate")

    c0 = glu_fwd(z_glu, b_glu, tr=tr, name="glu_fwd")
    c1 = dwconv_fwd(c0, w_dw_full, b_dw, tc=tcv, rc=rc, name="dwconv_fwd")
    st_gb, tok_gb = gather_middle(st_gb, c1, "b")
    c3 = ln_silu_fwd(c1, g_conv_ln, b_conv_ln, tr=tr, name="ln_silu_fwd")
    cq, ckv, kr = lora_norm_fwd(z_small, g_q_lora, g_kv_lora, cos, sin, tr=tr, name="lora_norm_fwd", deps=[tok_gb])

    def cols_full(a):
        return jnp.concatenate([a[k] for k in range(N_CHIPS)], axis=1)

    G.update(gather_end(st_gb, cq, "b"))
    W_co = G["w_conv_out"].reshape(C, D)
    W_uq = cols_full(G["w_uq"]).reshape(R, H, QK_NOPE + QK_ROPE)
    Wq_p = jnp.pad(W_uq, ((0, 0), (0, 0), (0, HEAD_PAD - QK_NOPE - QK_ROPE))).reshape(R, H * HEAD_PAD)
    W_uk = cols_full(G["w_uk"])
    W_uv = cols_full(G["w_uv"])
    W_ao = G["w_attn_out"].reshape(H * V_HEAD, D)
    W_o = G["w_out"].reshape(D, D)

    y_conv = mm(c3, W_co, mode="nn", tm=T, tn=tn_of(D), tk=C, out_dtype=F32, name="mm_y_conv")

    q = q_proj(cq, Wq_p, cos, sin, scale=scale, name="q_proj")
    kk = k_proj(ckv, W_uk, kr, name="k_proj")
    vv = mm(ckv, W_uv, mode="nn", tm=T, tn=tn_of(H * V_HEAD), tk=R, out_dtype=BF, name="mm_v")
    o_attn, lse = attn_fwd(q, kk, vv, nch=nch, name="attn_fwd")
    y_attn = mm(o_attn, W_ao, mode="nn", tm=T, tn=tn_of(D), tk=H * V_HEAD, out_dtype=F32, name="mm_y_attn")

    mix = mix_fwd(z_gate, y_conv, y_attn, b_gate, b_conv_out, tr=tr, name="mix_fwd")
    st_gc, tok_gc = gather_middle(st_gc, mix, "c")
    h1 = mm(mix, W_o, mode="nn", tm=T, tn=tn_of(D), tk=D, out_dtype=F32, name="mm_h1", res=h0, deps=[tok_gc])

    hn = rms_fwd(h1, g_ffn, tr=tr, name="rms_ffn")

    G.update(gather_end(st_gc, hn, "c"))
    W_fg, W_fu = G["w_ffn_gate"], G["w_ffn_up"]
    W_fd = G["w_ffn_down"].reshape(FF, D)

    fa, fb, f = ffn_fwd(hn, W_fg, W_fu, tm=tmq, name="ffn_fwd")
    h2 = mm(f, W_fd, mode="nn", tm=tmh, tn=tn_of(D, 256), tk=FF, out_dtype=F32, name="mm_h2", res=h1)

    place_arr = jnp.stack([chip, ci.astype(jnp.int32)])

    def col_blocks(a):
        r, cfull = a.shape
        return a.reshape(r, N_CHIPS, cfull // N_CHIPS).transpose(1, 0, 2)

    def rs_begin(nms, gls, after, tag):
        ax = [axes[nm] for nm in nms]
        lands = []
        for g, a in zip(gls, ax):
            _, r, cfull = g.shape
            lands.append(lax.empty((N_CHIPS, r // 2, cfull) if a == 0 else (N_CHIPS, r, cfull // 2), BF))
        arrs, ssem, rsem, tok = copies_start(list(gls) + lands, exchange_plan(ax), N_CHIPS * len(nms), after,
                                             name=f"rs_{tag}_exchange_start")
        return (nms, ax, arrs, ssem, rsem), tok

    def rs_middle(state, after, tag):
        nms, ax, arrs, ssem, rsem = state
        n = len(nms)
        arrs = copies_wait(arrs, exchange_plan(ax), ssem, rsem, after, name=f"rs_{tag}_exchange_wait")
        pair = [pair_sum(g, r, half_arr, a, name="rs_pair_sum_" + nm)
                for nm, g, r, a in zip(nms, arrs[:n], arrs[n:], ax)]
        lands = [lax.empty((3,) + p.shape[1:], BF) for p in pair]
        arrs, ssem, rsem, tok = copies_start(pair + lands, scatter_plan(n), 3 * n, half_arr,
                                             name=f"rs_{tag}_scatter_start")
        return (nms, ax, arrs, ssem, rsem), tok

    def rs_end(state, after, tag):
        nms, ax, arrs, ssem, rsem = state
        n = len(nms)
        arrs = copies_wait(arrs, scatter_plan(n), ssem, rsem, after, name=f"rs_{tag}_scatter_wait")
        return [chip_sum(s, r, place_arr, a, name="rs_chip_sum_" + nm)
                for nm, s, r, a in zip(nms, arrs[:n], arrs[n:], ax)]

    d_h2, d_h2b, loss_p, d_g_final = final_loss(h2, tgt, g_final.reshape(1, D), seq=seq, tr=tr, name="final_loss")
    loss = lax.psum(loss_p[0, 0], ("x", "y", "c"))

    dW_fd = mm(f, d_h2b, mode="tn", tm=FB, tn=tn_of(D), tk=T, out_dtype=BF, name="mm_dw_ffn_down")
    d_a, d_b = ffn_dact(d_h2b, W_fd, fa, fb, tm=tmq, name="ffn_dact")
    dW_fg = mm(hn, d_a, mode="tn", tm=tmw, tn=FB, tk=T, out_dtype=BF, name="mm_dw_ffn_gate", out_blocks=N_CHIPS)
    dW_fu = mm(hn, d_b, mode="tn", tm=tmw, tn=FB, tk=T, out_dtype=BF, name="mm_dw_ffn_up", out_blocks=N_CHIPS)
    st_c, tok_rc = rs_begin(grp_c, [dW_fg, dW_fu, dW_fd.reshape(N_CHIPS, FB, D)], half_arr, "c")
    d_hn = ffn_dhn(d_a, d_b, W_fg, W_fu, tm=tmq, tn=_pick(D, (256,)), name="ffn_dhn", deps=[tok_rc])
    st_c, tok_rc = rs_middle(st_c, d_hn, "c")
    d_h1, d_h1b, d_g_ffn = rms_bwd(d_hn, h1, g_ffn, d_h2, tr=tr, name="rms_ffn_bwd", deps=[tok_rc])

    d_mix = mm(d_h1b, W_o, mode="nt", tm=T, tn=tn_of(D), tk=D, out_dtype=F32, name="mm_d_mix")
    dW_o = mm(mix, d_h1b, mode="tn", tm=tmw, tn=tn_of(D), tk=T, out_dtype=BF, name="mm_dw_out")
    d_yc, d_ya, d_zgate, d_b_gate, d_b_co = mix_bwd(z_gate, y_conv, y_attn, d_mix, b_gate, b_conv_out, tr=tr,
                                                    name="mix_bwd")

    dW_co = mm(c3, d_yc, mode="tn", tm=tmw, tn=tn_of(D), tk=T, out_dtype=BF, name="mm_dw_conv_out")
    d_c3 = mm(d_yc, W_co, mode="nt", tm=T, tn=tn_of(C), tk=D, out_dtype=F32, name="mm_d_c3")
    d_c1, d_g_ln, d_b_ln, d_b_dw = ln_silu_bwd(c1, d_c3, g_conv_ln, b_conv_ln, tr=tr, name="ln_silu_bwd")
    d_c0, d_w_dw = dwconv_bwd(c0, d_c1, w_dw_full, tc=tcv, rc=rc, name="dwconv_bwd")
    d_zglu, d_b_glu = glu_bwd(z_glu, b_glu, d_c0, tr=tr, name="glu_bwd")

    dW_ao = mm(o_attn, d_ya, mode="tn", tm=_pick(H * V_HEAD, (1024, 512)), tn=tn_of(D), tk=T, out_dtype=BF, name="mm_dw_attn_out")
    d_o = mm(d_ya, W_ao, mode="nt", tm=T, tn=tn_of(H * V_HEAD), tk=D, out_dtype=BF, name="mm_d_o")
    d_q2, d_k2, d_v, d_kr = attn_bwd(q, kk, vv, o_attn, d_o, lse, cos, sin, scale=scale, nch=nch, name="attn_bwd")

    dWq_p = mm(cq, d_q2, mode="tn", tm=R, tn=tn_of(H * HEAD_PAD), tk=T, out_dtype=BF, name="mm_dw_uq")
    d_cq = mm(d_q2, Wq_p, mode="nt", tm=T, tn=R, tk=_pick(H * HEAD_PAD, (1024,)), out_dtype=F32, name="mm_d_cq")
    Wk_p = jnp.pad(W_uk.reshape(R, H, QK_NOPE), ((0, 0), (0, 0), (0, HEAD_PAD - QK_NOPE))).reshape(R, H * HEAD_PAD)
    dWk_p = mm(ckv, d_k2, mode="tn", tm=R, tn=tn_of(H * HEAD_PAD), tk=T, out_dtype=BF, name="mm_dw_uk")
    dW_uv = mm(ckv, d_v, mode="tn", tm=R, tn=tn_of(H * V_HEAD), tk=T, out_dtype=BF, name="mm_dw_uv")
    red_c = rs_end(st_c, [dWq_p, dWk_p, dW_uv, dW_co, dW_ao, dW_o], "c")
    dW_uq = dWq_p.reshape(R, H, HEAD_PAD)[:, :, :QK_NOPE + QK_ROPE].reshape(R, H * (QK_NOPE + QK_ROPE))
    dW_uk = dWk_p.reshape(R, H, HEAD_PAD)[:, :, :QK_NOPE].reshape(R, H * QK_NOPE)
    st_b, tok_rb = rs_begin(grp_b, [dW_co.reshape(N_CHIPS, C // N_CHIPS, D), col_blocks(dW_uq), col_blocks(dW_uk),
                                    col_blocks(dW_uv), dW_ao.reshape(N_CHIPS, H * V_HEAD // N_CHIPS, D),
                                    dW_o.reshape(N_CHIPS, D // N_CHIPS, D)], red_c[-1], "b")
    d_ckv = mm(d_k2, Wk_p, mode="nt", tm=T, tn=R, tk=_pick(H * HEAD_PAD, (1024,)), out_dtype=F32, name="mm_d_ckv_k",
               deps=[tok_rb])
    d_ckv = mm(d_v, W_uv, mode="nt", tm=T, tn=R, tk=_pick(H * V_HEAD, (1024,)), out_dtype=F32, name="mm_d_ckv_v",
               res=d_ckv)
    d_zsmall, d_g_q, d_g_kv = lora_norm_bwd(z_small, d_cq, d_ckv, d_kr, cos, sin, g_q_lora, g_kv_lora, tr=tr,
                                            name="lora_norm_bwd")
    st_b, tok_rb = rs_middle(st_b, d_zsmall, "b")

    tmi = _pick(2 * C, (1024, 512))
    dWT_glu = mm(d_zglu, u, mode="tn", tm=tmi, tn=tn_of(D), tk=T, out_dtype=BF, name="mm_dw_glu", deps=[tok_rb])
    dWT_small = mm(d_zsmall, u, mode="tn", tm=2 * R + 128, tn=tn_of(D), tk=T, out_dtype=BF, name="mm_dw_small")
    dWT_gate = mm(d_zgate, u, mode="tn", tm=tmi, tn=tn_of(D), tk=T, out_dtype=BF, name="mm_dw_gate")
    dWT_in = jnp.concatenate([dWT_glu, dWT_small[:2 * R + QK_ROPE], dWT_gate], axis=0)
    red_b = rs_end(st_b, dWT_in, "b")
    st_a, tok_ra = rs_begin(grp_a, [dWT_in.reshape(N_CHIPS, dWT_in.shape[0] // N_CHIPS, D)], red_b[-1], "a")
    d_u = mm(d_zglu, WT_glu, mode="nn", tm=T, tn=tn_of(D), tk=_pick(2 * C, (2048, 1024)), out_dtype=F32, name="mm_d_u_glu",
             deps=[tok_ra])
    st_a, tok_ra = rs_middle(st_a, d_u, "a")
    d_u = mm(d_zsmall, WT_small, mode="nn", tm=T, tn=tn_of(D), tk=2 * R + 128, out_dtype=F32, name="mm_d_u_small",
             res=d_u, deps=[tok_ra])
    d_u = mm(d_zgate, WT_gate, mode="nn", tm=T, tn=tn_of(D), tk=_pick(2 * D, (2048, 1024)), out_dtype=F32,
             name="mm_d_u_gate", res=d_u)
    d_h0, _, d_g_mix = rms_bwd(d_u, h0, g_mix, d_h1, tr=tr, name="rms_mix_bwd")

    grad_x = d_h0[N_META:length][None]

    small = ["g_mix", "b_glu", "b_gate", "b_dw", "g_conv_ln", "b_conv_ln", "b_conv_out", "g_q_lora", "g_kv_lora",
             "g_ffn", "g_final", "w_dw", "meta_tokens"]
    sgrads = dict(g_mix=d_g_mix, b_glu=d_b_glu, b_gate=d_b_gate, b_dw=d_b_dw, g_conv_ln=d_g_ln, b_conv_ln=d_b_ln,
                  b_conv_out=d_b_co, g_q_lora=d_g_q, g_kv_lora=d_g_kv, g_ffn=d_g_ffn, g_final=d_g_final,
                  w_dw=d_w_dw[:CONV_WIDTH], meta_tokens=d_h0[:N_META])
    sizes = [int(np.prod(sgrads[nm].shape)) for nm in small]
    packed = jnp.concatenate([sgrads[nm].reshape(-1) for nm in small]).reshape(-1, 128)
    sm_arrs, sm_ss, sm_rs, tok_sm = copies_start([packed, lax.empty((N_DEV,) + packed.shape, F32)], devices_plan,
                                                 N_DEV - 1, packed, name="gather_small_start")

    grads, delta, new_m, new_v, delta_2d = {}, {}, {}, {}, {}

    def adamw_2d(nm, w2, g2, m2, v2, back):
        g_, d_, m_, v_ = adamw(w2, g2, m2, v2, name="adamw_" + nm)
        delta_2d[nm] = d_
        grads[nm], delta[nm], new_m[nm], new_v[nm] = back(g_), back(d_), back(m_), back(v_)

    reduced = dict(zip(grp_b + grp_c, share_halves(red_b + red_c, [axes[nm] for nm in grp_b + grp_c], tok_sm,
                                                   name="rs_share_halves_bc")))
    for nm in grp_b + grp_c + grp_a:
        if nm == grp_a[0]:
            red_a = rs_end(st_a, delta_2d[grp_c[-1]], "a")
            reduced.update(zip(grp_a, share_halves(red_a, [axes[nm] for nm in grp_a], tok_sm,
                                                   name="rs_share_halves_a")))
        shp = weights[nm].shape
        if axes[nm] == 1:
            adamw_2d(nm, shard[nm], reduced[nm], jnp.swapaxes(m_in[nm][0], 0, 1), jnp.swapaxes(v_in[nm][0], 0, 1),
                     lambda a: jnp.swapaxes(a, 0, 1)[None])
        else:
            two = (shp[-2], shp[-1])
            adamw_2d(nm, weights[nm].reshape(two), reduced[nm], m_in[nm].reshape(two), v_in[nm].reshape(two),
                     lambda a, shp=shp: a.reshape(shp))

    sm_arrs = copies_wait(sm_arrs, devices_plan, sm_ss, sm_rs, delta_2d[grp_a[0]], name="gather_small_wait")
    me = (4 * xi + 2 * yi + ci).astype(jnp.int32)
    parts = lax.dynamic_update_slice(sm_arrs[1], sm_arrs[0][None], (me, 0, 0))
    summed = sum_devices(parts, name="sum_small_grads").reshape(-1)
    offs = np.concatenate([[0], np.cumsum(sizes)])
    sfull = {nm: summed[int(offs[i]):int(offs[i + 1])].reshape(sgrads[nm].shape) for i, nm in enumerate(small)}
    for nm in small:
        gfull = sfull[nm]
        if nm == "w_dw":
            cb = C // N_CHIPS
            grads[nm] = lax.dynamic_slice(gfull, (0, chip * cb), (CONV_WIDTH, cb))[None]
        elif nm == "meta_tokens":
            cb = D // N_CHIPS
            grads[nm] = lax.dynamic_slice(gfull, (0, chip * cb), (N_META, cb))
        else:
            grads[nm] = gfull.reshape(weights[nm].shape)

    rep = [nm for nm in small if nm not in ("w_dw", "meta_tokens")]

    def pack(d):
        return jnp.concatenate([d[nm].reshape(-1) for nm in rep]).reshape(-1, 128)

    _, pd, pm, pv = adamw(pack(weights), pack(grads), pack(m_in), pack(v_in), name="adamw_small")
    rsz = [int(np.prod(weights[nm].shape)) for nm in rep]
    roff = np.concatenate([[0], np.cumsum(rsz)])
    for i, nm in enumerate(rep):
        sl = slice(int(roff[i]), int(roff[i + 1]))
        delta[nm] = pd.reshape(-1)[sl].reshape(weights[nm].shape)
        new_m[nm] = pm.reshape(-1)[sl].reshape(weights[nm].shape)
        new_v[nm] = pv.reshape(-1)[sl].reshape(weights[nm].shape)
    for nm in ["w_dw", "meta_tokens"]:
        shp = weights[nm].shape
        two = (shp[-2], shp[-1])
        adamw_2d(nm, weights[nm].reshape(two), grads[nm].reshape(two), m_in[nm].reshape(two), v_in[nm].reshape(two),
                 lambda a, shp=shp: a.reshape(shp))

    return (loss, grad_x, *[grads[nm] for nm in names], *[delta[nm] for nm in names],
            *[new_m[nm] for nm in names], *[new_v[nm] for nm in names])
```

```python
import functools

import numpy as np
import jax
import jax.numpy as jnp
from jax import lax
from jax.experimental import pallas as pl
from jax.experimental.pallas import tpu as pltpu

F32 = jnp.float32
BF = jnp.bfloat16
MESH = pl.DeviceIdType.MESH

N_META = 16
BLOCK_Q = 128
CONV_WIDTH = 31
CONV_PAD = 32
QK_NOPE = 128
QK_ROPE = 64
V_HEAD = 128
HEAD_PAD = 256
ROPE_THETA = 10000.0
EPS = 1e-6
ADAM_LR = 0.001
ADAM_B1 = 0.9
ADAM_B2 = 0.999
ADAM_EPS = 1e-08
ADAM_WD = 0.01
ADAM_STEP = 10
VMEM_LIMIT = 56 * 1024 * 1024
N_CHIPS = 4
N_DEV = 8


def _cparams(sem):
    return pltpu.CompilerParams(dimension_semantics=sem, vmem_limit_bytes=VMEM_LIMIT)


def mm(a, b, *, mode, tm, tn, tk, out_dtype, name, res=None, out_blocks=None, deps=()):
    b3 = b.ndim == 3
    if mode == "nn":
        M, K = a.shape
        N = b.shape[0] * b.shape[2] if b3 else b.shape[1]
        a_spec = pl.BlockSpec((tm, tk), lambda i, j, k: (i, k))
        if b3:
            per = b.shape[2] // tn
            b_spec = pl.BlockSpec((None, tk, tn), lambda i, j, k: (j // per, k, j % per))
        else:
            b_spec = pl.BlockSpec((tk, tn), lambda i, j, k: (k, j))
        dims = (((1,), (0,)), ((), ()))
    elif mode == "nt":
        M, K = a.shape
        N = b.shape[1] if b3 else b.shape[0]
        a_spec = pl.BlockSpec((tm, tk), lambda i, j, k: (i, k))
        if b3:
            per = b.shape[2] // tk
            b_spec = pl.BlockSpec((None, tn, tk), lambda i, j, k: (k // per, j, k % per))
        else:
            b_spec = pl.BlockSpec((tn, tk), lambda i, j, k: (j, k))
        dims = (((1,), (1,)), ((), ()))
    else:
        K, M = a.shape
        N = b.shape[1]
        a_spec = pl.BlockSpec((tk, tm), lambda i, j, k: (k, i))
        b_spec = pl.BlockSpec((tk, tn), lambda i, j, k: (k, j))
        dims = (((0,), (0,)), ((), ()))
    gm, gn, gk = M // tm, N // tn, K // tk
    assert gm * tm == M and gn * tn == N and gk * tk == K, (name, a.shape, b.shape, tm, tn, tk)
    if out_blocks is None:
        o_spec = pl.BlockSpec((tm, tn), lambda i, j, k: (i, j))
        o_shape = (M, N)
    else:
        nbw = N // out_blocks
        per_o = nbw // tn
        assert per_o * tn == nbw
        o_spec = pl.BlockSpec((None, tm, tn), lambda i, j, k: (j // per_o, i, j % per_o))
        o_shape = (out_blocks, M, nbw)
    has_res = res is not None

    def body(*refs):
        a_ref, b_ref = refs[0], refs[1]
        r_ref = refs[2] if has_res else None
        o_ref = refs[2 + has_res + len(deps)]
        acc = refs[-1]
        p = lax.dot_general(a_ref[...], b_ref[...], dims, preferred_element_type=F32)

        def finish(v):
            if has_res:
                v = v + r_ref[...]
            o_ref[...] = v.astype(o_ref.dtype)

        if gk == 1:
            finish(p)
        else:
            k = pl.program_id(2)

            @pl.when(k == 0)
            def _():
                acc[...] = p

            @pl.when(k > 0)
            def _():
                acc[...] += p

            @pl.when(k == gk - 1)
            def _():
                finish(acc[...])

    in_specs = [a_spec, b_spec]
    args = [a, b]
    if has_res:
        in_specs.append(pl.BlockSpec((tm, tn), lambda i, j, k: (i, j)))
        args.append(res)
    in_specs += [pl.BlockSpec(memory_space=pl.ANY)] * len(deps)
    args += list(deps)
    return pl.pallas_call(
        body, name=name, grid=(gm, gn, gk), in_specs=in_specs, out_specs=o_spec,
        out_shape=jax.ShapeDtypeStruct(o_shape, out_dtype),
        scratch_shapes=[pltpu.VMEM((tm, tn), F32)] if gk > 1 else [],
        compiler_params=_cparams(("parallel", "parallel", "arbitrary")),
    )(*args)


def mm_tn_into(a, b, buf, row0, *, tm, tn, name, deps=()):
    K, M = a.shape
    N = b.shape[1]
    gm, gn = M // tm, N // tn
    assert gm * tm == M and gn * tn == N and buf.shape[1] == N and row0 + M <= buf.shape[0]
    steps = gm * gn

    def body(a_ref, b_ref, buf_ref, *rest):
        o_ref, tile, sem = rest[len(deps):]
        i, j = pl.program_id(0), pl.program_id(1)
        s = i * gn + j
        slot = s % 2

        def out_copy(sl):
            return pltpu.make_async_copy(tile.at[sl], o_ref.at[pl.ds(row0 + i * tm, tm), pl.ds(j * tn, tn)],
                                         sem.at[sl])

        @pl.when(s >= 2)
        def _():
            out_copy(slot).wait()

        tile[slot] = lax.dot_general(a_ref[...], b_ref[...], (((0,), (0,)), ((), ())),
                                     preferred_element_type=F32).astype(tile.dtype)
        out_copy(slot).start()

        @pl.when(s == steps - 1)
        def _():
            out_copy(slot).wait()
            if steps >= 2:
                out_copy(1 - slot).wait()

    anyspec = pl.BlockSpec(memory_space=pl.ANY)
    return pl.pallas_call(
        body, name=name, grid=(gm, gn),
        in_specs=[pl.BlockSpec((K, tm), lambda i, j: (0, i)), pl.BlockSpec((K, tn), lambda i, j: (0, j)), anyspec]
        + [anyspec] * len(deps),
        out_specs=anyspec, out_shape=jax.ShapeDtypeStruct(buf.shape, buf.dtype),
        input_output_aliases={2: 0},
        scratch_shapes=[pltpu.VMEM((2, tm, tn), buf.dtype), pltpu.SemaphoreType.DMA((2,))],
        compiler_params=_cparams(("arbitrary", "arbitrary")),
    )(a, b, buf, *deps)


def rowwise(body, row_ins, full_ins, row_outs, acc_outs, *, tr, name, deps=()):
    T = row_ins[0].shape[0]
    assert T % tr == 0, (name, T, tr)
    n_ri, n_fi, n_ro = len(row_ins), len(full_ins), len(row_outs)
    n_in = n_ri + n_fi + len(deps)

    def kern(*refs):
        body(pl.program_id(0), refs[:n_ri], refs[n_ri:n_ri + n_fi], refs[n_in:n_in + n_ro], refs[n_in + n_ro:])

    in_specs = [pl.BlockSpec((tr, a.shape[1]), lambda i: (i, 0)) for a in row_ins]
    in_specs += [pl.BlockSpec(a.shape, lambda i: (0, 0)) for a in full_ins]
    in_specs += [pl.BlockSpec(memory_space=pl.ANY)] * len(deps)
    out_specs = [pl.BlockSpec((tr, c), lambda i: (i, 0)) for c, _ in row_outs]
    out_specs += [pl.BlockSpec(s, lambda i: (0, 0)) for s in acc_outs]
    out_shape = [jax.ShapeDtypeStruct((T, c), d) for c, d in row_outs]
    out_shape += [jax.ShapeDtypeStruct(s, F32) for s in acc_outs]
    return pl.pallas_call(
        kern, name=name, grid=(T // tr,), in_specs=in_specs, out_specs=out_specs, out_shape=out_shape,
        compiler_params=_cparams(("arbitrary",)),
    )(*row_ins, *full_ins, *deps)


def _acc(step, ref, val):
    @pl.when(step == 0)
    def _():
        ref[...] = val

    @pl.when(step > 0)
    def _():
        ref[...] += val


def _colsum(x):
    return jnp.sum(x, axis=0, keepdims=True)


def _sigmoid(x):
    return 1.0 / (1.0 + jnp.exp(-x))


def _rot_half(v):
    lane = lax.broadcasted_iota(jnp.int32, v.shape, 1)
    lo = -pltpu.roll(v, 96, 1)
    hi = pltpu.roll(v, 32, 1)
    return jnp.where(lane < 32, lo, jnp.where(lane < 64, hi, 0.0))


def rms_fwd(h, g, *, tr, name, deps=()):
    def body(step, ri, fi, ro, ao):
        x = ri[0][...]
        r = lax.rsqrt(jnp.mean(x * x, axis=-1, keepdims=True) + EPS)
        ro[0][...] = ((x * r) * fi[0][...]).astype(BF)

    return rowwise(body, [h], [g], [(h.shape[1], BF)], [], tr=tr, name=name, deps=deps)[0]


def rms_bwd(dy, h, g, dres, *, tr, name, deps=()):
    D = h.shape[1]

    def body(step, ri, fi, ro, ao):
        d, x, dr = ri[0][...].astype(F32), ri[1][...], ri[2][...]
        r = lax.rsqrt(jnp.mean(x * x, axis=-1, keepdims=True) + EPS)
        n = x * r
        _acc(step, ao[0], _colsum(d * n))
        dn = d * fi[0][...]
        dh = r * (dn - n * jnp.mean(dn * n, axis=-1, keepdims=True)) + dr
        ro[0][...] = dh
        ro[1][...] = dh.astype(BF)

    return rowwise(body, [dy, h, dres], [g], [(D, F32), (D, BF)], [(1, D)], tr=tr, name=name, deps=deps)


def glu_fwd(z, b, *, tr, name):
    C = z.shape[1] // 2

    def body(step, ri, fi, ro, ao):
        zz = ri[0][...].astype(F32) + fi[0][...]
        ro[0][...] = zz[:, :C] * _sigmoid(zz[:, C:])

    return rowwise(body, [z], [b], [(C, F32)], [], tr=tr, name=name)[0]


def glu_bwd(z, b, dc, *, tr, name):
    C = z.shape[1] // 2

    def body(step, ri, fi, ro, ao):
        zz = ri[0][...].astype(F32) + fi[0][...]
        d = ri[1][...]
        za, sg = zz[:, :C], _sigmoid(zz[:, C:])
        dza = d * sg
        dzb = d * za * sg * (1.0 - sg)
        ro[0][:, :C] = dza.astype(BF)
        ro[0][:, C:] = dzb.astype(BF)
        _acc(step, ao[0], _colsum(dza))
        _acc(step, ao[1], _colsum(dzb))

    dz, da, db = rowwise(body, [z, dc], [b], [(2 * C, BF)], [(1, C), (1, C)], tr=tr, name=name)
    return dz, jnp.concatenate([da, db], axis=1)


def ln_silu_fwd(c1, g, b, *, tr, name):
    def body(step, ri, fi, ro, ao):
        x = ri[0][...]
        mu = jnp.mean(x, axis=-1, keepdims=True)
        xc = x - mu
        rstd = lax.rsqrt(jnp.mean(xc * xc, axis=-1, keepdims=True) + EPS)
        y = (xc * rstd) * fi[0][...] + fi[1][...]
        ro[0][...] = (y * _sigmoid(y)).astype(BF)

    return rowwise(body, [c1], [g, b], [(c1.shape[1], BF)], [], tr=tr, name=name)[0]


def ln_silu_bwd(c1, dc3, g, b, *, tr, name):
    C = c1.shape[1]

    def body(step, ri, fi, ro, ao):
        x, d3 = ri[0][...], ri[1][...].astype(F32)
        mu = jnp.mean(x, axis=-1, keepdims=True)
        xc = x - mu
        rstd = lax.rsqrt(jnp.mean(xc * xc, axis=-1, keepdims=True) + EPS)
        n = xc * rstd
        y = n * fi[0][...] + fi[1][...]
        sg = _sigmoid(y)
        dy = d3 * (sg * (1.0 + y * (1.0 - sg)))
        _acc(step, ao[0], _colsum(dy * n))
        _acc(step, ao[1], _colsum(dy))
        dn = dy * fi[0][...]
        dx = rstd * (dn - jnp.mean(dn, axis=-1, keepdims=True) - n * jnp.mean(dn * n, axis=-1, keepdims=True))
        ro[0][...] = dx
        _acc(step, ao[2], _colsum(dx))

    return rowwise(body, [c1, dc3], [g, b], [(C, F32)], [(1, C), (1, C), (1, C)], tr=tr, name=name)


def lora_norm_fwd(zs, gq, gkv, cos, sin, *, tr, name, deps=()):
    R = gq.shape[1]

    def body(step, ri, fi, ro, ao):
        z = ri[0][...]
        for o, gi in ((0, 0), (1, 1)):
            x = z[:, o * R:(o + 1) * R]
            r = lax.rsqrt(jnp.mean(x * x, axis=-1, keepdims=True) + EPS)
            ro[o][...] = ((x * r) * fi[gi][...]).astype(BF)
        kr = z[:, 2 * R:2 * R + 128]
        ro[2][...] = kr * ri[1][...] + _rot_half(kr) * ri[2][...]

    return rowwise(body, [zs, cos, sin], [gq, gkv], [(R, BF), (R, BF), (128, F32)], [], tr=tr, name=name, deps=deps)


def lora_norm_bwd(zs, dcq, dckv, dkr, cos, sin, gq, gkv, *, tr, name):
    R = gq.shape[1]

    def body(step, ri, fi, ro, ao):
        z = ri[0][...]
        for o in (0, 1):
            x = z[:, o * R:(o + 1) * R]
            d = ri[1 + o][...]
            r = lax.rsqrt(jnp.mean(x * x, axis=-1, keepdims=True) + EPS)
            n = x * r
            _acc(step, ao[o], _colsum(d * n))
            dn = d * fi[o][...]
            ro[0][:, o * R:(o + 1) * R] = (r * (dn - n * jnp.mean(dn * n, axis=-1, keepdims=True))).astype(BF)
        dk = ri[3][...]
        ro[0][:, 2 * R:2 * R + 128] = (dk * ri[4][...] - _rot_half(dk * ri[5][...])).astype(BF)

    return rowwise(body, [zs, dcq, dckv, dkr, cos, sin], [gq, gkv], [(2 * R + 128, BF)], [(1, R), (1, R)],
                   tr=tr, name=name)


def mix_fwd(zg, yc, ya, bg, bco, *, tr, name):
    D = yc.shape[1]

    def body(step, ri, fi, ro, ao):
        g = _sigmoid(ri[0][...].astype(F32) + fi[0][...])
        ro[0][...] = (g[:, :D] * (ri[1][...].astype(F32) + fi[1][...]) + g[:, D:] * ri[2][...].astype(F32)).astype(BF)

    return rowwise(body, [zg, yc, ya], [bg, bco], [(D, BF)], [], tr=tr, name=name)[0]


def mix_bwd(zg, yc, ya, dmix, bg, bco, *, tr, name):
    D = yc.shape[1]

    def body(step, ri, fi, ro, ao):
        g = _sigmoid(ri[0][...].astype(F32) + fi[0][...])
        gc, ga = g[:, :D], g[:, D:]
        ycv = ri[1][...].astype(F32) + fi[1][...]
        yav = ri[2][...].astype(F32)
        dm = ri[3][...].astype(F32)
        dyc = dm * gc
        ro[0][...] = dyc.astype(BF)
        ro[1][...] = (dm * ga).astype(BF)
        dzc = dm * ycv * gc * (1.0 - gc)
        dza = dm * yav * ga * (1.0 - ga)
        ro[2][:, :D] = dzc.astype(BF)
        ro[2][:, D:] = dza.astype(BF)
        _acc(step, ao[0], _colsum(dzc))
        _acc(step, ao[1], _colsum(dza))
        _acc(step, ao[2], _colsum(dyc))

    dyc, dya, dzg, dbc, dba, dbo = rowwise(body, [zg, yc, ya, dmix], [bg, bco], [(D, BF), (D, BF), (2 * D, BF)],
                                           [(1, D), (1, D), (1, D)], tr=tr, name=name)
    return dyc, dya, dzg, jnp.concatenate([dbc, dba], axis=1), dbo


def ffn_fwd(hn, wg, wu, *, tm, name):
    T, D = hn.shape
    nb, _, FB = wg.shape
    assert T % tm == 0

    def body(h_ref, g_ref, u_ref, a_ref, b_ref, f_ref):
        h = h_ref[...]
        a = jnp.dot(h, g_ref[...], preferred_element_type=F32)
        b = jnp.dot(h, u_ref[...], preferred_element_type=F32)
        a_ref[...] = a
        b_ref[...] = b
        f_ref[...] = (a * _sigmoid(a) * b).astype(BF)

    wspec = pl.BlockSpec((None, D, FB), lambda j, i: (j, 0, 0))
    ospec = pl.BlockSpec((tm, FB), lambda j, i: (i, j))
    return pl.pallas_call(
        body, name=name, grid=(nb, T // tm),
        in_specs=[pl.BlockSpec((tm, D), lambda j, i: (i, 0)), wspec, wspec], out_specs=[ospec, ospec, ospec],
        out_shape=[jax.ShapeDtypeStruct((T, nb * FB), F32), jax.ShapeDtypeStruct((T, nb * FB), F32),
                   jax.ShapeDtypeStruct((T, nb * FB), BF)],
        compiler_params=_cparams(("parallel", "parallel")),
    )(hn, wg, wu)


def ffn_dact(dh, wd, a, b, *, tm, name):
    T, D = dh.shape
    FF = wd.shape[0]
    FB = FF // N_CHIPS

    def body(dh_ref, w_ref, a_ref, b_ref, da_ref, db_ref):
        d = lax.dot_general(dh_ref[...], w_ref[...], (((1,), (1,)), ((), ())), preferred_element_type=F32)
        x, u = a_ref[...], b_ref[...]
        sg = _sigmoid(x)
        da_ref[...] = (d * u * (sg * (1.0 + x * (1.0 - sg)))).astype(BF)
        db_ref[...] = (d * (x * sg)).astype(BF)

    blk = pl.BlockSpec((tm, FB), lambda j, i: (i, j))
    return pl.pallas_call(
        body, name=name, grid=(N_CHIPS, T // tm),
        in_specs=[pl.BlockSpec((tm, D), lambda j, i: (i, 0)), pl.BlockSpec((FB, D), lambda j, i: (j, 0)), blk, blk],
        out_specs=[blk, blk],
        out_shape=[jax.ShapeDtypeStruct((T, FF), BF), jax.ShapeDtypeStruct((T, FF), BF)],
        compiler_params=_cparams(("parallel", "parallel")),
    )(dh, wd, a, b)


def ffn_dhn(da, db, wg, wu, *, tm, tn, name, deps=()):
    T, FF = da.shape
    nb, D, FB = wg.shape
    dims = (((1,), (1,)), ((), ()))

    def body(da_ref, db_ref, g_ref, u_ref, *rest):
        o_ref = rest[len(deps)]
        acc = None
        for k in range(nb):
            cols = slice(k * FB, (k + 1) * FB)
            p = lax.dot_general(da_ref[:, cols], g_ref[k], dims, preferred_element_type=F32)
            p = p + lax.dot_general(db_ref[:, cols], u_ref[k], dims, preferred_element_type=F32)
            acc = p if acc is None else acc + p
        o_ref[...] = acc.astype(BF)

    aspec = pl.BlockSpec((tm, FF), lambda i, j: (i, 0))
    wspec = pl.BlockSpec((nb, tn, FB), lambda i, j: (0, j, 0))
    return pl.pallas_call(
        body, name=name, grid=(T // tm, D // tn),
        in_specs=[aspec, aspec, wspec, wspec] + [pl.BlockSpec(memory_space=pl.ANY)] * len(deps),
        out_specs=pl.BlockSpec((tm, tn), lambda i, j: (i, j)),
        out_shape=jax.ShapeDtypeStruct((T, D), BF),
        compiler_params=_cparams(("parallel", "parallel")),
    )(da, db, wg, wu, *deps)


def final_loss(h2, tgt, g, *, seq, tr, name):
    D = h2.shape[1]

    def body(step, ri, fi, ro, ao):
        x = ri[0][...]
        r = lax.rsqrt(jnp.mean(x * x, axis=-1, keepdims=True) + EPS)
        n = x * r
        gg = fi[0][...]
        row = lax.broadcasted_iota(jnp.int32, (tr, 1), 0) + step * tr
        live = jnp.logical_and(row >= N_META, row < N_META + seq)
        e = jnp.where(live, n * gg - ri[1][...], 0.0)
        _acc(step, ao[0], jnp.broadcast_to(0.5 * jnp.sum(jnp.mean(e * e, axis=-1, keepdims=True)), (1, 128)))
        dy = e * (1.0 / D)
        _acc(step, ao[1], _colsum(dy * n))
        dn = dy * gg
        dh = r * (dn - n * jnp.mean(dn * n, axis=-1, keepdims=True))
        ro[0][...] = dh
        ro[1][...] = dh.astype(BF)

    return rowwise(body, [h2, tgt], [g], [(D, F32), (D, BF)], [(1, 128), (1, D)], tr=tr, name=name)


def _shifted_copies(s_ref, T, rc):
    for r in range(1, 8):
        for t0 in range(0, T, rc):
            s_ref[r, t0:t0 + rc, :] = s_ref[0, t0 + r:t0 + r + rc, :]
        s_ref[r, T:T + CONV_PAD - 8, :] = s_ref[0, T + r:T + r + CONV_PAD - 8, :]


def _shifted(s_ref, start, rows):
    return s_ref[start % 8, start - start % 8:start - start % 8 + rows, :]


def dwconv_fwd(x, w, b, *, tc, rc, name, deps=()):
    T, C = x.shape
    nchunk = T // rc
    assert nchunk * rc == T and C % tc == 0

    def body(x_ref, w_ref, b_ref, *rest):
        y_ref, xs_ref = rest[len(deps):]
        xs_ref[0, 0:CONV_PAD, :] = jnp.zeros((CONV_PAD, tc), F32)
        xs_ref[0, CONV_PAD:CONV_PAD + T, :] = x_ref[...]
        _shifted_copies(xs_ref, T, rc)
        off = CONV_PAD - (CONV_WIDTH - 1)
        for ci in range(nchunk):
            t0 = ci * rc
            acc = jnp.broadcast_to(b_ref[...], (rc, tc))
            for j in range(CONV_WIDTH):
                acc = acc + w_ref[j:j + 1, :] * _shifted(xs_ref, t0 + off + j, rc)
            y_ref[t0:t0 + rc, :] = acc

    return pl.pallas_call(
        body, name=name, grid=(C // tc,),
        in_specs=[pl.BlockSpec((T, tc), lambda j: (0, j)), pl.BlockSpec((32, tc), lambda j: (0, j)),
                  pl.BlockSpec((1, tc), lambda j: (0, j))] + [pl.BlockSpec(memory_space=pl.ANY)] * len(deps),
        out_specs=pl.BlockSpec((T, tc), lambda j: (0, j)),
        out_shape=jax.ShapeDtypeStruct((T, C), F32),
        scratch_shapes=[pltpu.VMEM((8, T + CONV_PAD, tc), F32)],
        compiler_params=_cparams(("parallel",)),
    )(x, w, b, *deps)


def dwconv_bwd(x, dy, w, *, tc, rc, name):
    T, C = x.shape
    nchunk = T // rc
    assert nchunk * rc == T and C % tc == 0 and rc % 8 == 0

    def body(x_ref, dy_ref, w_ref, dx_ref, dw_ref, xs_ref, ds_ref):
        off = CONV_PAD - (CONV_WIDTH - 1)
        xs_ref[0, 0:CONV_PAD, :] = jnp.zeros((CONV_PAD, tc), F32)
        xs_ref[0, CONV_PAD:CONV_PAD + T, :] = x_ref[...]
        _shifted_copies(xs_ref, T, rc)
        ds_ref[0, 0:T, :] = dy_ref[...]
        ds_ref[0, T:T + CONV_PAD, :] = jnp.zeros((CONV_PAD, tc), F32)
        _shifted_copies(ds_ref, T, rc)
        for ci in range(nchunk):
            t0 = ci * rc
            acc = jnp.zeros((rc, tc), F32)
            for j in range(CONV_WIDTH):
                acc = acc + w_ref[j:j + 1, :] * _shifted(ds_ref, t0 + (CONV_WIDTH - 1) - j, rc)
            dx_ref[t0:t0 + rc, :] = acc
        dw_ref[...] = jnp.zeros((32, tc), F32)
        for j in range(CONV_WIDTH):
            acc = jnp.zeros((8, tc), F32)
            for ci in range(nchunk):
                t0 = ci * rc
                pr = dy_ref[t0:t0 + rc, :] * _shifted(xs_ref, t0 + off + j, rc)
                acc = acc + jnp.sum(pr.reshape(rc // 8, 8, tc), axis=0)
            dw_ref[j:j + 1, :] = jnp.sum(acc, axis=0, keepdims=True)

    return pl.pallas_call(
        body, name=name, grid=(C // tc,),
        in_specs=[pl.BlockSpec((T, tc), lambda j: (0, j)), pl.BlockSpec((T, tc), lambda j: (0, j)),
                  pl.BlockSpec((32, tc), lambda j: (0, j))],
        out_specs=[pl.BlockSpec((T, tc), lambda j: (0, j)), pl.BlockSpec((32, tc), lambda j: (0, j))],
        out_shape=[jax.ShapeDtypeStruct((T, C), F32), jax.ShapeDtypeStruct((32, C), F32)],
        scratch_shapes=[pltpu.VMEM((8, T + CONV_PAD, tc), F32), pltpu.VMEM((8, T + CONV_PAD, tc), F32)],
        compiler_params=_cparams(("parallel",)),
    )(x, dy, w)


def q_proj(cq, wq, cos, sin, *, scale, name):
    T, R = cq.shape
    H = wq.shape[1] // HEAD_PAD

    def body(a_ref, b_ref, c_ref, s_ref, o_ref):
        p = jnp.dot(a_ref[...], b_ref[...], preferred_element_type=F32)
        o_ref[:, :QK_NOPE] = (p[:, :QK_NOPE] * scale).astype(BF)
        x = p[:, QK_NOPE:]
        o_ref[:, QK_NOPE:] = ((x * c_ref[...] + _rot_half(x) * s_ref[...]) * scale).astype(BF)

    return pl.pallas_call(
        body, name=name, grid=(H,),
        in_specs=[pl.BlockSpec((T, R), lambda h: (0, 0)), pl.BlockSpec((R, HEAD_PAD), lambda h: (0, h)),
                  pl.BlockSpec((T, 128), lambda h: (0, 0)), pl.BlockSpec((T, 128), lambda h: (0, 0))],
        out_specs=pl.BlockSpec((T, HEAD_PAD), lambda h: (0, h)),
        out_shape=jax.ShapeDtypeStruct((T, H * HEAD_PAD), BF),
        compiler_params=_cparams(("parallel",)),
    )(cq, wq, cos, sin)


def k_proj(ckv, wk, kr, *, name):
    T, R = ckv.shape
    H = wk.shape[1] // QK_NOPE

    def body(a_ref, b_ref, kr_ref, o_ref):
        p = jnp.dot(a_ref[...], b_ref[...], preferred_element_type=F32)
        o_ref[:, :QK_NOPE] = p.astype(BF)
        o_ref[:, QK_NOPE:] = kr_ref[...].astype(BF)

    return pl.pallas_call(
        body, name=name, grid=(H,),
        in_specs=[pl.BlockSpec((T, R), lambda h: (0, 0)), pl.BlockSpec((R, QK_NOPE), lambda h: (0, h)),
                  pl.BlockSpec((T, 128), lambda h: (0, 0))],
        out_specs=pl.BlockSpec((T, HEAD_PAD), lambda h: (0, h)),
        out_shape=jax.ShapeDtypeStruct((T, H * HEAD_PAD), BF),
        compiler_params=_cparams(("parallel",)),
    )(ckv, wk, kr)


def _causal_scores(q, k_ref, qi, CH, split):
    nt = (((1,), (1,)), ((), ()))
    pieces = []
    if not split:
        rows = slice(0, (qi + 1) * CH)
        s = lax.dot_general(q, k_ref[rows, :], nt, preferred_element_type=F32)
        keep = lax.broadcasted_iota(jnp.int32, s.shape, 1) <= lax.broadcasted_iota(jnp.int32, s.shape, 0) + qi * CH
        return [(rows, jnp.where(keep, s, -1e30))]
    if qi > 0:
        rows = slice(0, qi * CH)
        pieces.append((rows, lax.dot_general(q, k_ref[rows, :], nt, preferred_element_type=F32)))
    rows = slice(qi * CH, (qi + 1) * CH)
    s = lax.dot_general(q, k_ref[rows, :], nt, preferred_element_type=F32)
    keep = lax.broadcasted_iota(jnp.int32, s.shape, 1) <= lax.broadcasted_iota(jnp.int32, s.shape, 0)
    pieces.append((rows, jnp.where(keep, s, -1e30)))
    return pieces


def attn_fwd(q, k, v, *, nch, name):
    T = q.shape[0]
    H = q.shape[1] // HEAD_PAD
    CH = T // nch
    assert CH * nch == T and CH % 16 == 0

    def body(q_ref, k_ref, v_ref, o_ref, lse_ref):
        for qi in range(nch):
            L = (qi + 1) * CH
            pieces = _causal_scores(q_ref[qi * CH:L, :], k_ref, qi, CH, True)
            m = functools.reduce(jnp.maximum, [jnp.max(s, axis=1, keepdims=True) for _, s in pieces])
            l, o = 0.0, 0.0
            for rows, s in pieces:
                p = jnp.exp(s - m)
                l = l + jnp.sum(p, axis=1, keepdims=True)
                o = o + jnp.dot(p.astype(BF), v_ref[rows, :], preferred_element_type=F32)
            o_ref[qi * CH:L, :] = (o / l).astype(BF)
            lse_ref[qi * CH:L, :] = jnp.broadcast_to(m + jnp.log(l), (CH, V_HEAD))

    return pl.pallas_call(
        body, name=name, grid=(H,),
        in_specs=[pl.BlockSpec((T, HEAD_PAD), lambda h: (0, h)), pl.BlockSpec((T, HEAD_PAD), lambda h: (0, h)),
                  pl.BlockSpec((T, V_HEAD), lambda h: (0, h))],
        out_specs=[pl.BlockSpec((T, V_HEAD), lambda h: (0, h)), pl.BlockSpec((T, V_HEAD), lambda h: (0, h))],
        out_shape=[jax.ShapeDtypeStruct((T, H * V_HEAD), BF), jax.ShapeDtypeStruct((T, H * V_HEAD), F32)],
        compiler_params=_cparams(("parallel",)),
    )(q, k, v)


def attn_bwd(q, k, v, o, do, lse, cos, sin, *, scale, nch, name):
    T = q.shape[0]
    H = q.shape[1] // HEAD_PAD
    CH = T // nch
    tn_dims = (((0,), (0,)), ((), ()))
    nt_dims = (((1,), (1,)), ((), ()))

    def body(q_ref, k_ref, v_ref, o_ref, do_ref, lse_ref, c_ref, s_ref,
             dq_ref, dk_ref, dv_ref, dkr_ref, dk_acc, dv_acc):
        h = pl.program_id(0)
        dk_acc[...] = jnp.zeros_like(dk_acc)
        dv_acc[...] = jnp.zeros_like(dv_acc)
        for qi in range(nch):
            L = (qi + 1) * CH
            rows = slice(qi * CH, L)
            qc, doc = q_ref[rows, :], do_ref[rows, :]
            lse_c = lse_ref[rows, 0:1]
            dsum = jnp.sum(doc.astype(F32) * o_ref[rows, :].astype(F32), axis=1, keepdims=True)
            dq = 0.0
            for krows, s in _causal_scores(qc, k_ref, qi, CH, False):
                p = jnp.exp(s - lse_c)
                dp = lax.dot_general(doc, v_ref[krows, :], nt_dims, preferred_element_type=F32)
                ds = (p * (dp - dsum)).astype(BF)
                dq = dq + jnp.dot(ds, k_ref[krows, :], preferred_element_type=F32)
                dk_acc[krows, :] += lax.dot_general(ds, qc, tn_dims, preferred_element_type=F32)
                dv_acc[krows, :] += lax.dot_general(p.astype(BF), doc, tn_dims, preferred_element_type=F32)
            dq_ref[rows, :QK_NOPE] = (dq[:, :QK_NOPE] * scale).astype(BF)
            dyr = dq[:, QK_NOPE:]
            dq_ref[rows, QK_NOPE:] = ((dyr * c_ref[rows, :] - _rot_half(dyr * s_ref[rows, :])) * scale).astype(BF)
        dk_ref[:, :QK_NOPE] = dk_acc[:, :QK_NOPE].astype(BF)
        dk_ref[:, QK_NOPE:] = jnp.zeros((T, HEAD_PAD - QK_NOPE), BF)
        dv_ref[...] = dv_acc[...].astype(BF)

        @pl.when(h == 0)
        def _():
            dkr_ref[...] = dk_acc[:, QK_NOPE:]

        @pl.when(h > 0)
        def _():
            dkr_ref[...] += dk_acc[:, QK_NOPE:]

    hp = pl.BlockSpec((T, HEAD_PAD), lambda h: (0, h))
    hv = pl.BlockSpec((T, V_HEAD), lambda h: (0, h))
    tab = pl.BlockSpec((T, 128), lambda h: (0, 0))
    return pl.pallas_call(
        body, name=name, grid=(H,),
        in_specs=[hp, hp, hv, hv, hv, hv, tab, tab],
        out_specs=[hp, hp, hv, tab],
        out_shape=[jax.ShapeDtypeStruct((T, H * HEAD_PAD), BF), jax.ShapeDtypeStruct((T, H * HEAD_PAD), BF),
                   jax.ShapeDtypeStruct((T, H * V_HEAD), BF), jax.ShapeDtypeStruct((T, 128), F32)],
        scratch_shapes=[pltpu.VMEM((T, HEAD_PAD), F32), pltpu.VMEM((T, V_HEAD), F32)],
        compiler_params=_cparams(("arbitrary",)),
    )(q, k, v, o, do, lse, cos, sin)


def _place():
    x, y, c = lax.axis_index("x"), lax.axis_index("y"), lax.axis_index("c")
    return x, y, c


def _other_chips(x, y):
    return [(1 - x, y), (x, 1 - y), (1 - x, 1 - y)]


def gather_devices(v, *, name):
    R, C = v.shape

    def body(v_ref, o_ref, send_sems, recv_sems):
        x, y, c = _place()
        me = 4 * x + 2 * y + c
        o_ref[me] = v_ref[...]
        copies = []
        for k in range(1, N_DEV):
            fx, fy, fc = (k >> 2) & 1, (k >> 1) & 1, k & 1
            to = (x ^ fx, y ^ fy, c ^ fc)
            cp = pltpu.make_async_remote_copy(src_ref=v_ref, dst_ref=o_ref.at[me], send_sem=send_sems.at[k - 1],
                                              recv_sem=recv_sems.at[k - 1], device_id=to, device_id_type=MESH)
            cp.start()
            copies.append(cp)
        for k in range(1, N_DEV):
            fx, fy, fc = (k >> 2) & 1, (k >> 1) & 1, k & 1
            frm = 4 * (x ^ fx) + 2 * (y ^ fy) + (c ^ fc)
            pltpu.make_async_remote_copy(src_ref=v_ref, dst_ref=o_ref.at[frm], send_sem=send_sems.at[k - 1],
                                         recv_sem=recv_sems.at[k - 1], device_id=(x, y, c),
                                         device_id_type=MESH).wait_recv()
        for cp in copies:
            cp.wait_send()

    return pl.pallas_call(
        body, name=name, out_shape=jax.ShapeDtypeStruct((N_DEV, R, C), F32),
        in_specs=[pl.BlockSpec(memory_space=pltpu.VMEM)], out_specs=pl.BlockSpec(memory_space=pltpu.VMEM),
        scratch_shapes=[pltpu.SemaphoreType.DMA((N_DEV - 1,)), pltpu.SemaphoreType.DMA((N_DEV - 1,))],
    )(v)


HBM_SPEC = pl.BlockSpec(memory_space=pltpu.HBM)
SEM_SPEC = pl.BlockSpec(memory_space=pltpu.SEMAPHORE)
DATAFLOW = pltpu.SideEffectType.DATAFLOW_SIDE_EFFECTING


def _hbm(a):
    return pltpu.with_memory_space_constraint(a, pltpu.HBM)


def _half(ref, lead, c, axis):
    rows, cols = ref.shape[-2], ref.shape[-1]
    if axis == 0:
        return ref.at[(*lead, pl.ds(c * (rows // 2), rows // 2))]
    return ref.at[(*lead, slice(None), pl.ds(c * (cols // 2), cols // 2))]


def copies_start(arrs, plan, n_sems, after, *, name):
    n = len(arrs)

    def body(*refs):
        outs = refs[n + 1:2 * n + 1]
        send_sems, recv_sems, token = refs[2 * n + 1:]
        for cp in plan(outs, send_sems, recv_sems, False):
            cp.start()
        token[...] = jnp.zeros_like(token)

    res = pl.pallas_call(
        body, name=name,
        out_shape=tuple(pltpu.HBM(a.shape, a.dtype) for a in arrs)
        + (pltpu.SemaphoreType.DMA((n_sems,)), pltpu.SemaphoreType.DMA((n_sems,)), jax.ShapeDtypeStruct((8, 128), F32)),
        in_specs=(HBM_SPEC,) * n + (pl.BlockSpec(memory_space=pl.ANY),),
        out_specs=(HBM_SPEC,) * n + (SEM_SPEC, SEM_SPEC, pl.BlockSpec(memory_space=pltpu.VMEM)),
        input_output_aliases={i: i for i in range(n)},
        compiler_params=pltpu.CompilerParams(has_side_effects=DATAFLOW),
    )(*[_hbm(a) for a in arrs], after)
    return list(res[:n]), res[n], res[n + 1], res[n + 2]


def copies_wait(arrs, plan, send_sems, recv_sems, after, *, name):
    n = len(arrs)
    afters = list(after) if isinstance(after, (list, tuple)) else [after]

    def body(*refs):
        for cp in plan(refs[:n], refs[n], refs[n + 1], True):
            cp.wait_send()
            cp.wait_recv()

    res = pl.pallas_call(
        body, name=name,
        out_shape=tuple(pltpu.HBM(a.shape, a.dtype) for a in arrs),
        in_specs=(HBM_SPEC,) * n + (SEM_SPEC, SEM_SPEC) + (pl.BlockSpec(memory_space=pl.ANY),) * len(afters),
        out_specs=(HBM_SPEC,) * n,
        input_output_aliases={i: i for i in range(n)},
        compiler_params=pltpu.CompilerParams(has_side_effects=DATAFLOW),
    )(*arrs, send_sems, recv_sems, *afters)
    return list(res)


def gather_plan(axes):
    def plan(bufs, send_sems, recv_sems, receiving):
        x, y, c = _place()
        me = 2 * x + y
        out = []
        for i, b in enumerate(bufs):
            for j, (cx, cy) in enumerate(_other_chips(x, y)):
                src = _half(b, (me,), c, axes[i])
                dst = _half(b, (2 * cx + cy,), c, axes[i]) if receiving else src
                out.append(pltpu.make_async_remote_copy(
                    src_ref=src, dst_ref=dst, send_sem=send_sems.at[3 * i + j], recv_sem=recv_sems.at[3 * i + j],
                    device_id=(cx, cy, c), device_id_type=MESH))
        return out
    return plan


def forward_plan(axes):
    def plan(bufs, send_sems, recv_sems, receiving):
        x, y, c = _place()
        out = []
        for i, b in enumerate(bufs):
            for j, (cx, cy) in enumerate(_other_chips(x, y)):
                blk = _half(b, (2 * cx + cy,), (1 - c) if receiving else c, axes[i])
                out.append(pltpu.make_async_remote_copy(
                    src_ref=blk, dst_ref=blk, send_sem=send_sems.at[3 * i + j], recv_sem=recv_sems.at[3 * i + j],
                    device_id=(x, y, 1 - c), device_id_type=MESH))
        return out
    return plan


def scatter_plan(n):
    def plan(arrs, send_sems, recv_sems, receiving):
        x, y, c = _place()
        out = []
        for i in range(n):
            for j, (cx, cy) in enumerate(_other_chips(x, y)):
                out.append(pltpu.make_async_remote_copy(
                    src_ref=arrs[i].at[2 * cx + cy], dst_ref=arrs[n + i].at[j], send_sem=send_sems.at[3 * i + j],
                    recv_sem=recv_sems.at[3 * i + j], device_id=(cx, cy, c), device_id_type=MESH))
        return out
    return plan


def exchange_plan(axes):
    n = len(axes)

    def plan(arrs, send_sems, recv_sems, receiving):
        x, y, c = _place()
        out = []
        for i in range(n):
            for k in range(N_CHIPS):
                out.append(pltpu.make_async_remote_copy(
                    src_ref=_half(arrs[i], (k,), 1 - c, axes[i]), dst_ref=arrs[n + i].at[k],
                    send_sem=send_sems.at[N_CHIPS * i + k], recv_sem=recv_sems.at[N_CHIPS * i + k],
                    device_id=(x, y, 1 - c), device_id_type=MESH))
        return out
    return plan


def devices_plan(arrs, send_sems, recv_sems, receiving):
    x, y, c = _place()
    out = []
    for k in range(1, N_DEV):
        px, py, pc = x ^ ((k >> 2) & 1), y ^ ((k >> 1) & 1), c ^ (k & 1)
        slot = (4 * px + 2 * py + pc) if receiving else (4 * x + 2 * y + c)
        out.append(pltpu.make_async_remote_copy(
            src_ref=arrs[0], dst_ref=arrs[1].at[slot], send_sem=send_sems.at[k - 1], recv_sem=recv_sems.at[k - 1],
            device_id=(px, py, pc), device_id_type=MESH))
    return out


def share_halves(bufs, axes, after, *, name):
    n = len(bufs)

    def body(*refs):
        outs = refs[n + 1:2 * n + 1]
        send_sems, recv_sems = refs[2 * n + 1:]
        x, y, c = _place()
        sib = (x, y, 1 - c)
        cps = []
        for i in range(n):
            mine = _half(outs[i], (), c, axes[i])
            cp = pltpu.make_async_remote_copy(src_ref=mine, dst_ref=mine, send_sem=send_sems.at[i],
                                              recv_sem=recv_sems.at[i], device_id=sib, device_id_type=MESH)
            cp.start()
            cps.append(cp)
        for i in range(n):
            theirs = _half(outs[i], (), 1 - c, axes[i])
            pltpu.make_async_remote_copy(src_ref=theirs, dst_ref=theirs, send_sem=send_sems.at[i],
                                         recv_sem=recv_sems.at[i], device_id=sib, device_id_type=MESH).wait_recv()
        for cp in cps:
            cp.wait_send()

    anyspec = pl.BlockSpec(memory_space=pl.ANY)
    return pl.pallas_call(
        body, name=name,
        out_shape=[jax.ShapeDtypeStruct(b.shape, b.dtype) for b in bufs],
        in_specs=[anyspec] * (n + 1), out_specs=[anyspec] * n,
        input_output_aliases={i: i for i in range(n)},
        scratch_shapes=[pltpu.SemaphoreType.DMA((n,)), pltpu.SemaphoreType.DMA((n,))],
    )(*bufs, after)


def _row_tile(r, cols, itemsize, target_bytes=2 * 1024 * 1024):
    best = None
    for t in range(16, r + 1, 16):
        if r % t == 0 and t * cols * itemsize <= target_bytes:
            best = t
    return best if best is not None else r


def pair_sum(g, recv, half, axis, *, name):
    _, hr, hc = recv.shape
    tr = _row_tile(hr, hc, 4)
    nb = hr // tr
    if axis == 0:
        mine = pl.BlockSpec((None, tr, hc), lambda k, r, hf: (k, hf[0] * nb + r, 0))
    else:
        mine = pl.BlockSpec((None, tr, hc), lambda k, r, hf: (k, r, hf[0]))

    def body(half_ref, a_ref, b_ref, o_ref):
        o_ref[...] = (a_ref[...].astype(F32) + b_ref[...].astype(F32)).astype(BF)

    return pl.pallas_call(
        body, name=name,
        grid_spec=pltpu.PrefetchScalarGridSpec(
            num_scalar_prefetch=1, grid=(N_CHIPS, nb),
            in_specs=[mine, pl.BlockSpec((None, tr, hc), lambda k, r, hf: (k, r, 0))],
            out_specs=pl.BlockSpec((None, tr, hc), lambda k, r, hf: (k, r, 0))),
        out_shape=jax.ShapeDtypeStruct((N_CHIPS, hr, hc), BF),
        compiler_params=_cparams(("parallel", "parallel")),
    )(half, g, recv)


def chip_sum(s, recv, place, axis, *, name):
    _, hr, hc = s.shape
    tr = _row_tile(hr, hc, 4)
    nb = hr // tr
    if axis == 0:
        o_spec = pl.BlockSpec((tr, hc), lambda r, pc: (pc[1] * nb + r, 0))
        o_shape = (2 * hr, hc)
    else:
        o_spec = pl.BlockSpec((tr, hc), lambda r, pc: (r, pc[1]))
        o_shape = (hr, 2 * hc)

    def body(place_ref, a_ref, b_ref, o_ref):
        acc = a_ref[...].astype(F32)
        for j in range(3):
            acc = acc + b_ref[j].astype(F32)
        o_ref[...] = acc

    return pl.pallas_call(
        body, name=name,
        grid_spec=pltpu.PrefetchScalarGridSpec(
            num_scalar_prefetch=1, grid=(nb,),
            in_specs=[pl.BlockSpec((None, tr, hc), lambda r, pc: (pc[0], r, 0)),
                      pl.BlockSpec((3, tr, hc), lambda r, pc: (0, r, 0))],
            out_specs=o_spec),
        out_shape=jax.ShapeDtypeStruct(o_shape, F32),
        compiler_params=_cparams(("parallel",)),
    )(place, s, recv)


def sum_devices(v, *, name):
    _, R, C = v.shape

    def body(v_ref, o_ref):
        acc = v_ref[0]
        for k in range(1, N_DEV):
            acc = acc + v_ref[k]
        o_ref[...] = acc

    return pl.pallas_call(body, name=name, out_shape=jax.ShapeDtypeStruct((R, C), F32))(v)


def adamw(w, g, m, v, *, name):
    R, C = w.shape
    tr = _row_tile(R, C, 4, target_bytes=1024 * 1024)
    c1 = 1.0 / (1.0 - ADAM_B1 ** ADAM_STEP)
    c2 = 1.0 / (1.0 - ADAM_B2 ** ADAM_STEP)

    def body(w_ref, g_ref, m_ref, v_ref, go_ref, d_ref, nm_ref, nv_ref):
        gg = g_ref[...]
        go_ref[...] = gg
        nm = ADAM_B1 * m_ref[...] + (1.0 - ADAM_B1) * gg
        nv = ADAM_B2 * v_ref[...] + (1.0 - ADAM_B2) * (gg * gg)
        nm_ref[...] = nm
        nv_ref[...] = nv
        d_ref[...] = -ADAM_LR * ((nm * c1) / (jnp.sqrt(nv * c2) + ADAM_EPS) + ADAM_WD * w_ref[...])

    spec = pl.BlockSpec((tr, C), lambda i: (i, 0))
    return pl.pallas_call(
        body, name=name, grid=(R // tr,), in_specs=[spec] * 4, out_specs=[spec] * 4,
        out_shape=[jax.ShapeDtypeStruct((R, C), F32)] * 4, compiler_params=_cparams(("parallel",)),
    )(w, g, m, v)


def _rope_tables(T):
    pos = np.arange(T, dtype=np.float32)
    inv_freq = (ROPE_THETA ** (-np.arange(0, QK_ROPE, 2, dtype=np.float32) / QK_ROPE)).astype(np.float32)
    ang = pos[:, None] * inv_freq[None, :]
    z = np.zeros((T, 64), np.float32)
    cos = np.concatenate([np.cos(ang), np.cos(ang), z], axis=1).astype(np.float32)
    sin = np.concatenate([np.sin(ang), np.sin(ang), z], axis=1).astype(np.float32)
    return jnp.asarray(cos), jnp.asarray(sin)


def _pick(n, cands):
    for c in cands:
        if n % c == 0:
            return c
    return n


def kernel(x, meta_tokens, g_mix, w_in, b_glu, b_gate, w_dw, b_dw, g_conv_ln, b_conv_ln, w_conv_out, b_conv_out, g_q_lora, w_uq, g_kv_lora, w_uk, w_uv, w_attn_out, w_out, g_ffn, w_ffn_gate, w_ffn_up, w_ffn_down, g_final, loss_target, m_meta_tokens, m_g_mix, m_w_in, m_b_glu, m_b_gate, m_w_dw, m_b_dw, m_g_conv_ln, m_b_conv_ln, m_w_conv_out, m_b_conv_out, m_g_q_lora, m_w_uq, m_g_kv_lora, m_w_uk, m_w_uv, m_w_attn_out, m_w_out, m_g_ffn, m_w_ffn_gate, m_w_ffn_up, m_w_ffn_down, m_g_final, v_meta_tokens, v_g_mix, v_w_in, v_b_glu, v_b_gate, v_w_dw, v_b_dw, v_g_conv_ln, v_b_conv_ln, v_w_conv_out, v_b_conv_out, v_g_q_lora, v_w_uq, v_g_kv_lora, v_w_uk, v_w_uv, v_w_attn_out, v_w_out, v_g_ffn, v_w_ffn_gate, v_w_ffn_up, v_w_ffn_down, v_g_final):
    weights = dict(meta_tokens=meta_tokens, g_mix=g_mix, w_in=w_in, b_glu=b_glu, b_gate=b_gate, w_dw=w_dw, b_dw=b_dw,
                   g_conv_ln=g_conv_ln, b_conv_ln=b_conv_ln, w_conv_out=w_conv_out, b_conv_out=b_conv_out,
                   g_q_lora=g_q_lora, w_uq=w_uq, g_kv_lora=g_kv_lora, w_uk=w_uk, w_uv=w_uv, w_attn_out=w_attn_out,
                   w_out=w_out, g_ffn=g_ffn, w_ffn_gate=w_ffn_gate, w_ffn_up=w_ffn_up, w_ffn_down=w_ffn_down,
                   g_final=g_final)
    m_in = dict(meta_tokens=m_meta_tokens, g_mix=m_g_mix, w_in=m_w_in, b_glu=m_b_glu, b_gate=m_b_gate, w_dw=m_w_dw,
                b_dw=m_b_dw, g_conv_ln=m_g_conv_ln, b_conv_ln=m_b_conv_ln, w_conv_out=m_w_conv_out,
                b_conv_out=m_b_conv_out, g_q_lora=m_g_q_lora, w_uq=m_w_uq, g_kv_lora=m_g_kv_lora, w_uk=m_w_uk,
                w_uv=m_w_uv, w_attn_out=m_w_attn_out, w_out=m_w_out, g_ffn=m_g_ffn, w_ffn_gate=m_w_ffn_gate,
                w_ffn_up=m_w_ffn_up, w_ffn_down=m_w_ffn_down, g_final=m_g_final)
    v_in = dict(meta_tokens=v_meta_tokens, g_mix=v_g_mix, w_in=v_w_in, b_glu=v_b_glu, b_gate=v_b_gate, w_dw=v_w_dw,
                b_dw=v_b_dw, g_conv_ln=v_g_conv_ln, b_conv_ln=v_b_conv_ln, w_conv_out=v_w_conv_out,
                b_conv_out=v_b_conv_out, g_q_lora=v_g_q_lora, w_uq=v_w_uq, g_kv_lora=v_g_kv_lora, w_uk=v_w_uk,
                w_uv=v_w_uv, w_attn_out=v_w_attn_out, w_out=v_w_out, g_ffn=v_g_ffn, w_ffn_gate=v_w_ffn_gate,
                w_ffn_up=v_w_ffn_up, w_ffn_down=v_w_ffn_down, g_final=v_g_final)
    names = list(weights)

    seq, D = x.shape[1], x.shape[2]
    C = w_conv_out.shape[2]
    R = g_q_lora.shape[1]
    H = w_uv.shape[2] * N_CHIPS // V_HEAD
    FB = w_ffn_gate.shape[2]
    FF = FB * N_CHIPS
    length = N_META + seq
    T = -(-length // BLOCK_Q) * BLOCK_Q
    scale = (QK_NOPE + QK_ROPE) ** -0.5
    assert C == D and w_in.shape[2] * N_CHIPS == 2 * C + 2 * R + QK_ROPE + 2 * D

    xi, yi, ci = _place()
    chip = (2 * xi + yi).astype(jnp.int32)
    half_arr = ci.astype(jnp.int32).reshape(1)

    tr = _pick(T, (272, 256, 128))
    tmh = T // 2 if (T // 2) % 16 == 0 else T
    tmq = T // 4 if (T // 4) % 16 == 0 else T
    tmw = _pick(D, (1024, 512))
    nch = 4
    rc = _pick(T, (272, 256, 128))
    tcv = 128

    def tn_of(n, pref=512):
        return _pick(n, (pref, 384, 256, 128))

    grp_a = ["w_in"]
    grp_b = ["w_conv_out", "w_uq", "w_uk", "w_uv", "w_attn_out", "w_out"]
    grp_c = ["w_ffn_gate", "w_ffn_up", "w_ffn_down"]
    big = grp_a + grp_b + grp_c
    axes = {nm: 0 for nm in big}
    axes["w_in"] = 1
    shard = {nm: weights[nm][0] for nm in big}
    shard["w_in"] = jnp.swapaxes(w_in[0], 0, 1)

    def gather_begin(nms, srcs, after, tag):
        bufs = []
        for w in srcs:
            wb = w.astype(BF)
            bufs.append(lax.dynamic_update_slice(lax.empty((N_CHIPS,) + wb.shape, BF), wb[None], (chip, 0, 0)))
        ax = [axes[nm] for nm in nms]
        bufs, ssem, rsem, tok = copies_start(bufs, gather_plan(ax), 3 * len(nms), after, name=f"gather_{tag}_start")
        return (nms, ax, bufs, ssem, rsem), tok

    def gather_middle(state, after, tag):
        nms, ax, bufs, ssem, rsem = state
        bufs = copies_wait(bufs, gather_plan(ax), ssem, rsem, after, name=f"gather_{tag}_wait")
        bufs, ssem, rsem, tok = copies_start(bufs, forward_plan(ax), 3 * len(nms), half_arr,
                                             name=f"gather_{tag}_forward_start")
        return (nms, ax, bufs, ssem, rsem), tok

    def gather_end(state, after, tag):
        nms, ax, bufs, ssem, rsem = state
        return dict(zip(nms, copies_wait(bufs, forward_plan(ax), ssem, rsem, after, name=f"gather_{tag}_forward_wait")))

    tiny = gather_devices(jnp.concatenate([meta_tokens, jnp.pad(w_dw[0], ((0, 1), (0, 0)))], axis=0),
                          name="gather_meta_w_dw")
    meta_full = jnp.concatenate([tiny[2 * k, :N_META] for k in range(N_CHIPS)], axis=1)
    w_dw_full = jnp.concatenate([tiny[2 * k, N_META:] for k in range(N_CHIPS)], axis=1)

    st_ga, tok_ga = gather_begin(grp_a, [shard["w_in"]], tiny, "a")
    tok_ga, held = lax.optimization_barrier((tok_ga, (x, loss_target, [shard[nm] for nm in grp_b + grp_c])))
    x_l, tgt_l, shard_bc = held
    h0 = jnp.concatenate([meta_full, x_l[0], jnp.zeros((T - length, D), F32)], axis=0)
    tgt = jnp.pad(tgt_l[0], ((N_META, T - length), (0, 0)))
    cos, sin = _rope_tables(T)

    st_ga, tok_ga = gather_middle(st_ga, h0, "a")
    G = gather_end(st_ga, tok_ga, "a")
    st_gb, tok_gb = gather_begin(grp_b, shard_bc[:len(grp_b)], G["w_in"], "b")
    st_gc, tok_gc = gather_begin(grp_c, shard_bc[len(grp_b):], tok_gb, "c")

    WT_in = G["w_in"].reshape(N_CHIPS * G["w_in"].shape[1], D)
    o1, o2 = 2 * C, 2 * C + 2 * R + QK_ROPE
    WT_glu = WT_in[:o1]
    WT_small = jnp.pad(WT_in[o1:o2], ((0, 128 - QK_ROPE), (0, 0)))
    WT_gate = WT_in[o2:]

    u = rms_fwd(h0, g_mix, tr=tr, name="rms_mix", deps=[tok_gb, tok_gc])
    z_glu = mm(u, WT_glu, mode="nt", tm=T, tn=tn_of(2 * C), tk=D, out_dtype=BF, name="mm_z_glu")
    z_small = mm(u, WT_small, mode="nt", tm=T, tn=tn_of(2 * R + 128, 384), tk=D, out_dtype=F32, name="mm_z_small")
    z_gate = mm(u, WT_gate, mode="nt", tm=T, tn=tn_of(2 * D), tk=D, out_dtype=BF, name="mm_z_gate")

    c0 = glu_fwd(z_glu, b_glu, tr=tr, name="glu_fwd")
    c1 = dwconv_fwd(c0, w_dw_full, b_dw, tc=tcv, rc=rc, name="dwconv_fwd")
    st_gb, tok_gb = gather_middle(st_gb, c1, "b")
    c3 = ln_silu_fwd(c1, g_conv_ln, b_conv_ln, tr=tr, name="ln_silu_fwd")
    cq, ckv, kr = lora_norm_fwd(z_small, g_q_lora, g_kv_lora, cos, sin, tr=tr, name="lora_norm_fwd", deps=[tok_gb])

    def cols_full(a):
        return jnp.concatenate([a[k] for k in range(N_CHIPS)], axis=1)

    G.update(gather_end(st_gb, cq, "b"))
    W_co = G["w_conv_out"].reshape(C, D)
    W_uq = cols_full(G["w_uq"]).reshape(R, H, QK_NOPE + QK_ROPE)
    Wq_p = jnp.pad(W_uq, ((0, 0), (0, 0), (0, HEAD_PAD - QK_NOPE - QK_ROPE))).reshape(R, H * HEAD_PAD)
    W_uk = cols_full(G["w_uk"])
    W_uv = cols_full(G["w_uv"])
    W_ao = G["w_attn_out"].reshape(H * V_HEAD, D)
    W_o = G["w_out"].reshape(D, D)

    y_conv = mm(c3, W_co, mode="nn", tm=T, tn=tn_of(D), tk=C, out_dtype=BF, name="mm_y_conv")

    q = q_proj(cq, Wq_p, cos, sin, scale=scale, name="q_proj")
    kk = k_proj(ckv, W_uk, kr, name="k_proj")
    vv = mm(ckv, W_uv, mode="nn", tm=T, tn=tn_of(H * V_HEAD), tk=R, out_dtype=BF, name="mm_v")
    o_attn, lse = attn_fwd(q, kk, vv, nch=nch, name="attn_fwd")
    y_attn = mm(o_attn, W_ao, mode="nn", tm=T, tn=tn_of(D), tk=H * V_HEAD, out_dtype=BF, name="mm_y_attn")

    mix = mix_fwd(z_gate, y_conv, y_attn, b_gate, b_conv_out, tr=tr, name="mix_fwd")
    st_gc, tok_gc = gather_middle(st_gc, mix, "c")
    h1 = mm(mix, W_o, mode="nn", tm=T, tn=tn_of(D), tk=D, out_dtype=F32, name="mm_h1", res=h0, deps=[tok_gc])

    hn = rms_fwd(h1, g_ffn, tr=tr, name="rms_ffn")

    G.update(gather_end(st_gc, hn, "c"))
    W_fg, W_fu = G["w_ffn_gate"], G["w_ffn_up"]
    W_fd = G["w_ffn_down"].reshape(FF, D)

    fa, fb, f = ffn_fwd(hn, W_fg, W_fu, tm=tmq, name="ffn_fwd")
    h2 = mm(f, W_fd, mode="nn", tm=tmh, tn=tn_of(D, 256), tk=FF, out_dtype=F32, name="mm_h2", res=h1)

    place_arr = jnp.stack([chip, ci.astype(jnp.int32)])

    def col_blocks(a):
        r, cfull = a.shape
        return a.reshape(r, N_CHIPS, cfull // N_CHIPS).transpose(1, 0, 2)

    def rs_begin(nms, gls, after, tag):
        ax = [axes[nm] for nm in nms]
        lands = []
        for g, a in zip(gls, ax):
            _, r, cfull = g.shape
            lands.append(lax.empty((N_CHIPS, r // 2, cfull) if a == 0 else (N_CHIPS, r, cfull // 2), BF))
        arrs, ssem, rsem, tok = copies_start(list(gls) + lands, exchange_plan(ax), N_CHIPS * len(nms), after,
                                             name=f"rs_{tag}_exchange_start")
        return (nms, ax, arrs, ssem, rsem), tok

    def rs_middle(state, after, tag):
        nms, ax, arrs, ssem, rsem = state
        n = len(nms)
        arrs = copies_wait(arrs, exchange_plan(ax), ssem, rsem, after, name=f"rs_{tag}_exchange_wait")
        pair = [pair_sum(g, r, half_arr, a, name="rs_pair_sum_" + nm)
                for nm, g, r, a in zip(nms, arrs[:n], arrs[n:], ax)]
        lands = [lax.empty((3,) + p.shape[1:], BF) for p in pair]
        arrs, ssem, rsem, tok = copies_start(pair + lands, scatter_plan(n), 3 * n, half_arr,
                                             name=f"rs_{tag}_scatter_start")
        return (nms, ax, arrs, ssem, rsem), tok

    def rs_end(state, after, tag):
        nms, ax, arrs, ssem, rsem = state
        n = len(nms)
        arrs = copies_wait(arrs, scatter_plan(n), ssem, rsem, after, name=f"rs_{tag}_scatter_wait")
        return [chip_sum(s, r, place_arr, a, name="rs_chip_sum_" + nm)
                for nm, s, r, a in zip(nms, arrs[:n], arrs[n:], ax)]

    d_h2, d_h2b, loss_p, d_g_final = final_loss(h2, tgt, g_final.reshape(1, D), seq=seq, tr=tr, name="final_loss")
    loss = lax.psum(loss_p[0, 0], ("x", "y", "c"))

    dW_fd = mm(f, d_h2b, mode="tn", tm=FB, tn=tn_of(D), tk=T, out_dtype=BF, name="mm_dw_ffn_down")
    d_a, d_b = ffn_dact(d_h2b, W_fd, fa, fb, tm=tmq, name="ffn_dact")
    dW_fg = mm(hn, d_a, mode="tn", tm=tmw, tn=FB, tk=T, out_dtype=BF, name="mm_dw_ffn_gate", out_blocks=N_CHIPS)
    dW_fu = mm(hn, d_b, mode="tn", tm=tmw, tn=FB, tk=T, out_dtype=BF, name="mm_dw_ffn_up", out_blocks=N_CHIPS)
    st_c, tok_rc = rs_begin(grp_c, [dW_fg, dW_fu, dW_fd.reshape(N_CHIPS, FB, D)], half_arr, "c")
    d_hn = ffn_dhn(d_a, d_b, W_fg, W_fu, tm=tmq, tn=_pick(D, (256,)), name="ffn_dhn", deps=[tok_rc])
    st_c, tok_rc = rs_middle(st_c, d_hn, "c")
    d_h1, d_h1b, d_g_ffn = rms_bwd(d_hn, h1, g_ffn, d_h2, tr=tr, name="rms_ffn_bwd", deps=[tok_rc])

    d_mix = mm(d_h1b, W_o, mode="nt", tm=T, tn=tn_of(D), tk=D, out_dtype=BF, name="mm_d_mix")
    dW_o = mm(mix, d_h1b, mode="tn", tm=tmw, tn=tn_of(D), tk=T, out_dtype=BF, name="mm_dw_out")
    d_yc, d_ya, d_zgate, d_b_gate, d_b_co = mix_bwd(z_gate, y_conv, y_attn, d_mix, b_gate, b_conv_out, tr=tr,
                                                    name="mix_bwd")

    dW_co = mm(c3, d_yc, mode="tn", tm=tmw, tn=tn_of(D), tk=T, out_dtype=BF, name="mm_dw_conv_out")
    d_c3 = mm(d_yc, W_co, mode="nt", tm=T, tn=tn_of(C), tk=D, out_dtype=BF, name="mm_d_c3")
    d_c1, d_g_ln, d_b_ln, d_b_dw = ln_silu_bwd(c1, d_c3, g_conv_ln, b_conv_ln, tr=tr, name="ln_silu_bwd")
    d_c0, d_w_dw = dwconv_bwd(c0, d_c1, w_dw_full, tc=tcv, rc=rc, name="dwconv_bwd")
    d_zglu, d_b_glu = glu_bwd(z_glu, b_glu, d_c0, tr=tr, name="glu_bwd")

    dW_ao = mm(o_attn, d_ya, mode="tn", tm=_pick(H * V_HEAD, (1024, 512)), tn=tn_of(D), tk=T, out_dtype=BF, name="mm_dw_attn_out")
    d_o = mm(d_ya, W_ao, mode="nt", tm=T, tn=tn_of(H * V_HEAD), tk=D, out_dtype=BF, name="mm_d_o")
    d_q2, d_k2, d_v, d_kr = attn_bwd(q, kk, vv, o_attn, d_o, lse, cos, sin, scale=scale, nch=nch, name="attn_bwd")

    dWq_p = mm(cq, d_q2, mode="tn", tm=R, tn=tn_of(H * HEAD_PAD), tk=T, out_dtype=BF, name="mm_dw_uq")
    d_cq = mm(d_q2, Wq_p, mode="nt", tm=T, tn=R, tk=_pick(H * HEAD_PAD, (1024,)), out_dtype=F32, name="mm_d_cq")
    Wk_p = jnp.pad(W_uk.reshape(R, H, QK_NOPE), ((0, 0), (0, 0), (0, HEAD_PAD - QK_NOPE))).reshape(R, H * HEAD_PAD)
    dWk_p = mm(ckv, d_k2, mode="tn", tm=R, tn=tn_of(H * HEAD_PAD), tk=T, out_dtype=BF, name="mm_dw_uk")
    dW_uv = mm(ckv, d_v, mode="tn", tm=R, tn=tn_of(H * V_HEAD), tk=T, out_dtype=BF, name="mm_dw_uv")
    red_c = rs_end(st_c, [dWq_p, dWk_p, dW_uv, dW_co, dW_ao, dW_o], "c")
    dW_uq = dWq_p.reshape(R, H, HEAD_PAD)[:, :, :QK_NOPE + QK_ROPE].reshape(R, H * (QK_NOPE + QK_ROPE))
    dW_uk = dWk_p.reshape(R, H, HEAD_PAD)[:, :, :QK_NOPE].reshape(R, H * QK_NOPE)
    st_b, tok_rb = rs_begin(grp_b, [dW_co.reshape(N_CHIPS, C // N_CHIPS, D), col_blocks(dW_uq), col_blocks(dW_uk),
                                    col_blocks(dW_uv), dW_ao.reshape(N_CHIPS, H * V_HEAD // N_CHIPS, D),
                                    dW_o.reshape(N_CHIPS, D // N_CHIPS, D)], red_c[-1], "b")
    d_ckv = mm(d_k2, Wk_p, mode="nt", tm=T, tn=R, tk=_pick(H * HEAD_PAD, (1024,)), out_dtype=F32, name="mm_d_ckv_k",
               deps=[tok_rb])
    d_ckv = mm(d_v, W_uv, mode="nt", tm=T, tn=R, tk=_pick(H * V_HEAD, (1024,)), out_dtype=F32, name="mm_d_ckv_v",
               res=d_ckv)
    d_zsmall, d_g_q, d_g_kv = lora_norm_bwd(z_small, d_cq, d_ckv, d_kr, cos, sin, g_q_lora, g_kv_lora, tr=tr,
                                            name="lora_norm_bwd")
    st_b, tok_rb = rs_middle(st_b, d_zsmall, "b")

    tmi = _pick(2 * C, (1024, 512))
    dWT_in = lax.empty((WT_in.shape[0], D), BF)
    dWT_in = mm_tn_into(d_zglu, u, dWT_in, 0, tm=tmi, tn=tn_of(D), name="mm_dw_glu", deps=[tok_rb])
    dWT_in = mm_tn_into(d_zsmall, u, dWT_in, o1, tm=2 * R + 128, tn=tn_of(D), name="mm_dw_small")
    dWT_in = mm_tn_into(d_zgate, u, dWT_in, o2, tm=tmi, tn=tn_of(D), name="mm_dw_gate")
    red_b = rs_end(st_b, dWT_in, "b")
    st_a, tok_ra = rs_begin(grp_a, [dWT_in.reshape(N_CHIPS, dWT_in.shape[0] // N_CHIPS, D)], red_b[-1], "a")
    d_u = mm(d_zglu, WT_glu, mode="nn", tm=T, tn=tn_of(D), tk=_pick(2 * C, (2048, 1024)), out_dtype=F32, name="mm_d_u_glu",
             deps=[tok_ra])
    st_a, tok_ra = rs_middle(st_a, d_u, "a")
    d_u = mm(d_zsmall, WT_small, mode="nn", tm=T, tn=tn_of(D), tk=2 * R + 128, out_dtype=F32, name="mm_d_u_small",
             res=d_u, deps=[tok_ra])
    d_u = mm(d_zgate, WT_gate, mode="nn", tm=T, tn=tn_of(D), tk=_pick(2 * D, (2048, 1024)), out_dtype=F32,
             name="mm_d_u_gate", res=d_u)
    d_h0, _, d_g_mix = rms_bwd(d_u, h0, g_mix, d_h1, tr=tr, name="rms_mix_bwd")

    grad_x = d_h0[N_META:length][None]

    small = ["g_mix", "b_glu", "b_gate", "b_dw", "g_conv_ln", "b_conv_ln", "b_conv_out", "g_q_lora", "g_kv_lora",
             "g_ffn", "g_final", "w_dw", "meta_tokens"]
    sgrads = dict(g_mix=d_g_mix, b_glu=d_b_glu, b_gate=d_b_gate, b_dw=d_b_dw, g_conv_ln=d_g_ln, b_conv_ln=d_b_ln,
                  b_conv_out=d_b_co, g_q_lora=d_g_q, g_kv_lora=d_g_kv, g_ffn=d_g_ffn, g_final=d_g_final,
                  w_dw=d_w_dw[:CONV_WIDTH], meta_tokens=d_h0[:N_META])
    sizes = [int(np.prod(sgrads[nm].shape)) for nm in small]
    packed = jnp.concatenate([sgrads[nm].reshape(-1) for nm in small]).reshape(-1, 128)
    sm_arrs, sm_ss, sm_rs, tok_sm = copies_start([packed, lax.empty((N_DEV,) + packed.shape, F32)], devices_plan,
                                                 N_DEV - 1, packed, name="gather_small_start")

    grads, delta, new_m, new_v, delta_2d = {}, {}, {}, {}, {}

    def adamw_2d(nm, w2, g2, m2, v2, back):
        g_, d_, m_, v_ = adamw(w2, g2, m2, v2, name="adamw_" + nm)
        delta_2d[nm] = d_
        grads[nm], delta[nm], new_m[nm], new_v[nm] = back(g_), back(d_), back(m_), back(v_)

    reduced = dict(zip(grp_b + grp_c, share_halves(red_b + red_c, [axes[nm] for nm in grp_b + grp_c], tok_sm,
                                                   name="rs_share_halves_bc")))
    for nm in grp_b + grp_c + grp_a:
        if nm == grp_a[0]:
            red_a = rs_end(st_a, delta_2d[grp_c[-1]], "a")
            reduced.update(zip(grp_a, share_halves(red_a, [axes[nm] for nm in grp_a], tok_sm,
                                                   name="rs_share_halves_a")))
        shp = weights[nm].shape
        if axes[nm] == 1:
            adamw_2d(nm, shard[nm], reduced[nm], jnp.swapaxes(m_in[nm][0], 0, 1), jnp.swapaxes(v_in[nm][0], 0, 1),
                     lambda a: jnp.swapaxes(a, 0, 1)[None])
        else:
            two = (shp[-2], shp[-1])
            adamw_2d(nm, weights[nm].reshape(two), reduced[nm], m_in[nm].reshape(two), v_in[nm].reshape(two),
                     lambda a, shp=shp: a.reshape(shp))

    sm_arrs = copies_wait(sm_arrs, devices_plan, sm_ss, sm_rs, delta_2d[grp_a[0]], name="gather_small_wait")
    me = (4 * xi + 2 * yi + ci).astype(jnp.int32)
    parts = lax.dynamic_update_slice(sm_arrs[1], sm_arrs[0][None], (me, 0, 0))
    summed = sum_devices(parts, name="sum_small_grads").reshape(-1)
    offs = np.concatenate([[0], np.cumsum(sizes)])
    sfull = {nm: summed[int(offs[i]):int(offs[i + 1])].reshape(sgrads[nm].shape) for i, nm in enumerate(small)}
    for nm in small:
        gfull = sfull[nm]
        if nm == "w_dw":
            cb = C // N_CHIPS
            grads[nm] = lax.dynamic_slice(gfull, (0, chip * cb), (CONV_WIDTH, cb))[None]
        elif nm == "meta_tokens":
            cb = D // N_CHIPS
            grads[nm] = lax.dynamic_slice(gfull, (0, chip * cb), (N_META, cb))
        else:
            grads[nm] = gfull.reshape(weights[nm].shape)

    rep = [nm for nm in small if nm not in ("w_dw", "meta_tokens")]

    def pack(d):
        return jnp.concatenate([d[nm].reshape(-1) for nm in rep]).reshape(-1, 128)

    _, pd, pm, pv = adamw(pack(weights), pack(grads), pack(m_in), pack(v_in), name="adamw_small")
    rsz = [int(np.prod(weights[nm].shape)) for nm in rep]
    roff = np.concatenate([[0], np.cumsum(rsz)])
    for i, nm in enumerate(rep):
        sl = slice(int(roff[i]), int(roff[i + 1]))
        delta[nm] = pd.reshape(-1)[sl].reshape(weights[nm].shape)
        new_m[nm] = pm.reshape(-1)[sl].reshape(weights[nm].shape)
        new_v[nm] = pv.reshape(-1)[sl].reshape(weights[nm].shape)
    for nm in ["w_dw", "meta_tokens"]:
        shp = weights[nm].shape
        two = (shp[-2], shp[-1])
        adamw_2d(nm, weights[nm].reshape(two), grads[nm].reshape(two), m_in[nm].reshape(two), v_in[nm].reshape(two),
                 lambda a, shp=shp: a.reshape(shp))

    return (loss, grad_x, *[grads[nm] for nm in names], *[delta[nm] for nm in names],
            *[new_m[nm] for nm in names], *[new_v[nm] for nm in names])
```

```python
import functools

import numpy as np
import jax
import jax.numpy as jnp
from jax import lax
from jax.experimental import pallas as pl
from jax.experimental.pallas import tpu as pltpu

F32 = jnp.float32
BF = jnp.bfloat16
MESH = pl.DeviceIdType.MESH

N_META = 16
BLOCK_Q = 128
CONV_WIDTH = 31
CONV_PAD = 32
QK_NOPE = 128
QK_ROPE = 64
V_HEAD = 128
HEAD_PAD = 256
ROPE_THETA = 10000.0
EPS = 1e-6
ADAM_LR = 0.001
ADAM_B1 = 0.9
ADAM_B2 = 0.999
ADAM_EPS = 1e-08
ADAM_WD = 0.01
ADAM_STEP = 10
VMEM_LIMIT = 56 * 1024 * 1024
N_CHIPS = 4
N_DEV = 8


def _cparams(sem):
    return pltpu.CompilerParams(dimension_semantics=sem, vmem_limit_bytes=VMEM_LIMIT)


def mm(a, b, *, mode, tm, tn, tk, out_dtype, name, res=None, out_blocks=None, deps=()):
    b3 = b.ndim == 3
    if mode == "nn":
        M, K = a.shape
        N = b.shape[0] * b.shape[2] if b3 else b.shape[1]
        a_spec = pl.BlockSpec((tm, tk), lambda i, j, k: (i, k))
        if b3:
            per = b.shape[2] // tn
            b_spec = pl.BlockSpec((None, tk, tn), lambda i, j, k: (j // per, k, j % per))
        else:
            b_spec = pl.BlockSpec((tk, tn), lambda i, j, k: (k, j))
        dims = (((1,), (0,)), ((), ()))
    elif mode == "nt":
        M, K = a.shape
        N = b.shape[1] if b3 else b.shape[0]
        a_spec = pl.BlockSpec((tm, tk), lambda i, j, k: (i, k))
        if b3:
            per = b.shape[2] // tk
            b_spec = pl.BlockSpec((None, tn, tk), lambda i, j, k: (k // per, j, k % per))
        else:
            b_spec = pl.BlockSpec((tn, tk), lambda i, j, k: (j, k))
        dims = (((1,), (1,)), ((), ()))
    else:
        K, M = a.shape
        N = b.shape[1]
        a_spec = pl.BlockSpec((tk, tm), lambda i, j, k: (k, i))
        b_spec = pl.BlockSpec((tk, tn), lambda i, j, k: (k, j))
        dims = (((0,), (0,)), ((), ()))
    gm, gn, gk = M // tm, N // tn, K // tk
    assert gm * tm == M and gn * tn == N and gk * tk == K, (name, a.shape, b.shape, tm, tn, tk)
    if out_blocks is None:
        o_spec = pl.BlockSpec((tm, tn), lambda i, j, k: (i, j))
        o_shape = (M, N)
    else:
        nbw = N // out_blocks
        per_o = nbw // tn
        assert per_o * tn == nbw
        o_spec = pl.BlockSpec((None, tm, tn), lambda i, j, k: (j // per_o, i, j % per_o))
        o_shape = (out_blocks, M, nbw)
    has_res = res is not None

    def body(*refs):
        a_ref, b_ref = refs[0], refs[1]
        r_ref = refs[2] if has_res else None
        o_ref = refs[2 + has_res + len(deps)]
        acc = refs[-1]
        p = lax.dot_general(a_ref[...], b_ref[...], dims, preferred_element_type=F32)

        def finish(v):
            if has_res:
                v = v + r_ref[...]
            o_ref[...] = v.astype(o_ref.dtype)

        if gk == 1:
            finish(p)
        else:
            k = pl.program_id(2)

            @pl.when(k == 0)
            def _():
                acc[...] = p

            @pl.when(k > 0)
            def _():
                acc[...] += p

            @pl.when(k == gk - 1)
            def _():
                finish(acc[...])

    in_specs = [a_spec, b_spec]
    args = [a, b]
    if has_res:
        in_specs.append(pl.BlockSpec((tm, tn), lambda i, j, k: (i, j)))
        args.append(res)
    in_specs += [pl.BlockSpec(memory_space=pl.ANY)] * len(deps)
    args += list(deps)
    return pl.pallas_call(
        body, name=name, grid=(gm, gn, gk), in_specs=in_specs, out_specs=o_spec,
        out_shape=jax.ShapeDtypeStruct(o_shape, out_dtype),
        scratch_shapes=[pltpu.VMEM((tm, tn), F32)] if gk > 1 else [],
        compiler_params=_cparams(("parallel", "parallel", "arbitrary")),
    )(*args)


def mm_tn_into(a, b, buf, row0, *, tm, tn, name, deps=()):
    K, M = a.shape
    N = b.shape[1]
    gm, gn = M // tm, N // tn
    assert gm * tm == M and gn * tn == N and buf.shape[1] == N and row0 + M <= buf.shape[0]
    steps = gm * gn

    def body(a_ref, b_ref, buf_ref, *rest):
        o_ref, tile, sem = rest[len(deps):]
        i, j = pl.program_id(0), pl.program_id(1)
        s = i * gn + j
        slot = s % 2

        def out_copy(sl):
            return pltpu.make_async_copy(tile.at[sl], o_ref.at[pl.ds(row0 + i * tm, tm), pl.ds(j * tn, tn)],
                                         sem.at[sl])

        @pl.when(s >= 2)
        def _():
            out_copy(slot).wait()

        tile[slot] = lax.dot_general(a_ref[...], b_ref[...], (((0,), (0,)), ((), ())),
                                     preferred_element_type=F32).astype(tile.dtype)
        out_copy(slot).start()

        @pl.when(s == steps - 1)
        def _():
            out_copy(slot).wait()
            if steps >= 2:
                out_copy(1 - slot).wait()

    anyspec = pl.BlockSpec(memory_space=pl.ANY)
    return pl.pallas_call(
        body, name=name, grid=(gm, gn),
        in_specs=[pl.BlockSpec((K, tm), lambda i, j: (0, i)), pl.BlockSpec((K, tn), lambda i, j: (0, j)), anyspec]
        + [anyspec] * len(deps),
        out_specs=anyspec, out_shape=jax.ShapeDtypeStruct(buf.shape, buf.dtype),
        input_output_aliases={2: 0},
        scratch_shapes=[pltpu.VMEM((2, tm, tn), buf.dtype), pltpu.SemaphoreType.DMA((2,))],
        compiler_params=_cparams(("arbitrary", "arbitrary")),
    )(a, b, buf, *deps)


def rowwise(body, row_ins, full_ins, row_outs, acc_outs, *, tr, name, deps=()):
    T = row_ins[0].shape[0]
    assert T % tr == 0, (name, T, tr)
    n_ri, n_fi, n_ro = len(row_ins), len(full_ins), len(row_outs)
    n_in = n_ri + n_fi + len(deps)

    def kern(*refs):
        body(pl.program_id(0), refs[:n_ri], refs[n_ri:n_ri + n_fi], refs[n_in:n_in + n_ro], refs[n_in + n_ro:])

    in_specs = [pl.BlockSpec((tr, a.shape[1]), lambda i: (i, 0)) for a in row_ins]
    in_specs += [pl.BlockSpec(a.shape, lambda i: (0, 0)) for a in full_ins]
    in_specs += [pl.BlockSpec(memory_space=pl.ANY)] * len(deps)
    out_specs = [pl.BlockSpec((tr, c), lambda i: (i, 0)) for c, _ in row_outs]
    out_specs += [pl.BlockSpec(s, lambda i: (0, 0)) for s in acc_outs]
    out_shape = [jax.ShapeDtypeStruct((T, c), d) for c, d in row_outs]
    out_shape += [jax.ShapeDtypeStruct(s, F32) for s in acc_outs]
    return pl.pallas_call(
        kern, name=name, grid=(T // tr,), in_specs=in_specs, out_specs=out_specs, out_shape=out_shape,
        compiler_params=_cparams(("arbitrary",)),
    )(*row_ins, *full_ins, *deps)


def _acc(step, ref, val):
    @pl.when(step == 0)
    def _():
        ref[...] = val

    @pl.when(step > 0)
    def _():
        ref[...] += val


def _colsum(x):
    return jnp.sum(x, axis=0, keepdims=True)


def _sigmoid(x):
    return 1.0 / (1.0 + jnp.exp(-x))


def _rot_half(v):
    lane = lax.broadcasted_iota(jnp.int32, v.shape, 1)
    lo = -pltpu.roll(v, 96, 1)
    hi = pltpu.roll(v, 32, 1)
    return jnp.where(lane < 32, lo, jnp.where(lane < 64, hi, 0.0))


def rms_fwd(h, g, *, tr, name, deps=()):
    def body(step, ri, fi, ro, ao):
        x = ri[0][...]
        r = lax.rsqrt(jnp.mean(x * x, axis=-1, keepdims=True) + EPS)
        ro[0][...] = ((x * r) * fi[0][...]).astype(BF)

    return rowwise(body, [h], [g], [(h.shape[1], BF)], [], tr=tr, name=name, deps=deps)[0]


def rms_bwd(dy, h, g, dres, *, tr, name, deps=()):
    D = h.shape[1]

    def body(step, ri, fi, ro, ao):
        d, x, dr = ri[0][...].astype(F32), ri[1][...], ri[2][...]
        r = lax.rsqrt(jnp.mean(x * x, axis=-1, keepdims=True) + EPS)
        n = x * r
        _acc(step, ao[0], _colsum(d * n))
        dn = d * fi[0][...]
        dh = r * (dn - n * jnp.mean(dn * n, axis=-1, keepdims=True)) + dr
        ro[0][...] = dh
        ro[1][...] = dh.astype(BF)

    return rowwise(body, [dy, h, dres], [g], [(D, F32), (D, BF)], [(1, D)], tr=tr, name=name, deps=deps)


def glu_fwd(z, b, *, tr, name):
    C = z.shape[1] // 2

    def body(step, ri, fi, ro, ao):
        zz = ri[0][...].astype(F32) + fi[0][...]
        ro[0][...] = zz[:, :C] * _sigmoid(zz[:, C:])

    return rowwise(body, [z], [b], [(C, F32)], [], tr=tr, name=name)[0]


def glu_bwd(z, b, dc, *, tr, name):
    C = z.shape[1] // 2

    def body(step, ri, fi, ro, ao):
        zz = ri[0][...].astype(F32) + fi[0][...]
        d = ri[1][...]
        za, sg = zz[:, :C], _sigmoid(zz[:, C:])
        dza = d * sg
        dzb = d * za * sg * (1.0 - sg)
        ro[0][:, :C] = dza.astype(BF)
        ro[0][:, C:] = dzb.astype(BF)
        _acc(step, ao[0], _colsum(dza))
        _acc(step, ao[1], _colsum(dzb))

    dz, da, db = rowwise(body, [z, dc], [b], [(2 * C, BF)], [(1, C), (1, C)], tr=tr, name=name)
    return dz, jnp.concatenate([da, db], axis=1)


def ln_silu_fwd(c1, g, b, *, tr, name):
    def body(step, ri, fi, ro, ao):
        x = ri[0][...]
        mu = jnp.mean(x, axis=-1, keepdims=True)
        xc = x - mu
        rstd = lax.rsqrt(jnp.mean(xc * xc, axis=-1, keepdims=True) + EPS)
        y = (xc * rstd) * fi[0][...] + fi[1][...]
        ro[0][...] = (y * _sigmoid(y)).astype(BF)

    return rowwise(body, [c1], [g, b], [(c1.shape[1], BF)], [], tr=tr, name=name)[0]


def ln_silu_bwd(c1, dc3, g, b, *, tr, name):
    C = c1.shape[1]

    def body(step, ri, fi, ro, ao):
        x, d3 = ri[0][...], ri[1][...].astype(F32)
        mu = jnp.mean(x, axis=-1, keepdims=True)
        xc = x - mu
        rstd = lax.rsqrt(jnp.mean(xc * xc, axis=-1, keepdims=True) + EPS)
        n = xc * rstd
        y = n * fi[0][...] + fi[1][...]
        sg = _sigmoid(y)
        dy = d3 * (sg * (1.0 + y * (1.0 - sg)))
        _acc(step, ao[0], _colsum(dy * n))
        _acc(step, ao[1], _colsum(dy))
        dn = dy * fi[0][...]
        dx = rstd * (dn - jnp.mean(dn, axis=-1, keepdims=True) - n * jnp.mean(dn * n, axis=-1, keepdims=True))
        ro[0][...] = dx
        _acc(step, ao[2], _colsum(dx))

    return rowwise(body, [c1, dc3], [g, b], [(C, F32)], [(1, C), (1, C), (1, C)], tr=tr, name=name)


def lora_norm_fwd(zs, gq, gkv, cos, sin, *, tr, name, deps=()):
    R = gq.shape[1]

    def body(step, ri, fi, ro, ao):
        z = ri[0][...]
        for o, gi in ((0, 0), (1, 1)):
            x = z[:, o * R:(o + 1) * R]
            r = lax.rsqrt(jnp.mean(x * x, axis=-1, keepdims=True) + EPS)
            ro[o][...] = ((x * r) * fi[gi][...]).astype(BF)
        kr = z[:, 2 * R:2 * R + 128]
        ro[2][...] = kr * ri[1][...] + _rot_half(kr) * ri[2][...]

    return rowwise(body, [zs, cos, sin], [gq, gkv], [(R, BF), (R, BF), (128, F32)], [], tr=tr, name=name, deps=deps)


def lora_norm_bwd(zs, dcq, dckv, dkr, cos, sin, gq, gkv, *, tr, name):
    R = gq.shape[1]

    def body(step, ri, fi, ro, ao):
        z = ri[0][...]
        for o in (0, 1):
            x = z[:, o * R:(o + 1) * R]
            d = ri[1 + o][...]
            r = lax.rsqrt(jnp.mean(x * x, axis=-1, keepdims=True) + EPS)
            n = x * r
            _acc(step, ao[o], _colsum(d * n))
            dn = d * fi[o][...]
            ro[0][:, o * R:(o + 1) * R] = (r * (dn - n * jnp.mean(dn * n, axis=-1, keepdims=True))).astype(BF)
        dk = ri[3][...]
        ro[0][:, 2 * R:2 * R + 128] = (dk * ri[4][...] - _rot_half(dk * ri[5][...])).astype(BF)

    return rowwise(body, [zs, dcq, dckv, dkr, cos, sin], [gq, gkv], [(2 * R + 128, BF)], [(1, R), (1, R)],
                   tr=tr, name=name)


def mix_fwd(zg, yc, ya, bg, bco, *, tr, name):
    D = yc.shape[1]

    def body(step, ri, fi, ro, ao):
        g = _sigmoid(ri[0][...].astype(F32) + fi[0][...])
        ro[0][...] = (g[:, :D] * (ri[1][...].astype(F32) + fi[1][...]) + g[:, D:] * ri[2][...].astype(F32)).astype(BF)

    return rowwise(body, [zg, yc, ya], [bg, bco], [(D, BF)], [], tr=tr, name=name)[0]


def mix_bwd(zg, yc, ya, dmix, bg, bco, *, tr, name):
    D = yc.shape[1]

    def body(step, ri, fi, ro, ao):
        g = _sigmoid(ri[0][...].astype(F32) + fi[0][...])
        gc, ga = g[:, :D], g[:, D:]
        ycv = ri[1][...].astype(F32) + fi[1][...]
        yav = ri[2][...].astype(F32)
        dm = ri[3][...].astype(F32)
        dyc = dm * gc
        ro[0][...] = dyc.astype(BF)
        ro[1][...] = (dm * ga).astype(BF)
        dzc = dm * ycv * gc * (1.0 - gc)
        dza = dm * yav * ga * (1.0 - ga)
        ro[2][:, :D] = dzc.astype(BF)
        ro[2][:, D:] = dza.astype(BF)
        _acc(step, ao[0], _colsum(dzc))
        _acc(step, ao[1], _colsum(dza))
        _acc(step, ao[2], _colsum(dyc))

    dyc, dya, dzg, dbc, dba, dbo = rowwise(body, [zg, yc, ya, dmix], [bg, bco], [(D, BF), (D, BF), (2 * D, BF)],
                                           [(1, D), (1, D), (1, D)], tr=tr, name=name)
    return dyc, dya, dzg, jnp.concatenate([dbc, dba], axis=1), dbo


def ffn_fwd(hn, wg, wu, *, tm, name):
    T, D = hn.shape
    nb, _, FB = wg.shape
    assert T % tm == 0

    def body(h_ref, g_ref, u_ref, a_ref, b_ref, f_ref):
        h = h_ref[...]
        a = jnp.dot(h, g_ref[...], preferred_element_type=F32)
        b = jnp.dot(h, u_ref[...], preferred_element_type=F32)
        a_ref[...] = a
        b_ref[...] = b
        f_ref[...] = (a * _sigmoid(a) * b).astype(BF)

    wspec = pl.BlockSpec((None, D, FB), lambda j, i: (j, 0, 0))
    ospec = pl.BlockSpec((tm, FB), lambda j, i: (i, j))
    return pl.pallas_call(
        body, name=name, grid=(nb, T // tm),
        in_specs=[pl.BlockSpec((tm, D), lambda j, i: (i, 0)), wspec, wspec], out_specs=[ospec, ospec, ospec],
        out_shape=[jax.ShapeDtypeStruct((T, nb * FB), F32), jax.ShapeDtypeStruct((T, nb * FB), F32),
                   jax.ShapeDtypeStruct((T, nb * FB), BF)],
        compiler_params=_cparams(("parallel", "parallel")),
    )(hn, wg, wu)


def ffn_dact(dh, wd, a, b, *, tm, name):
    T, D = dh.shape
    FF = wd.shape[0]
    FB = FF // N_CHIPS

    def body(dh_ref, w_ref, a_ref, b_ref, da_ref, db_ref):
        d = lax.dot_general(dh_ref[...], w_ref[...], (((1,), (1,)), ((), ())), preferred_element_type=F32)
        x, u = a_ref[...], b_ref[...]
        sg = _sigmoid(x)
        da_ref[...] = (d * u * (sg * (1.0 + x * (1.0 - sg)))).astype(BF)
        db_ref[...] = (d * (x * sg)).astype(BF)

    blk = pl.BlockSpec((tm, FB), lambda j, i: (i, j))
    return pl.pallas_call(
        body, name=name, grid=(N_CHIPS, T // tm),
        in_specs=[pl.BlockSpec((tm, D), lambda j, i: (i, 0)), pl.BlockSpec((FB, D), lambda j, i: (j, 0)), blk, blk],
        out_specs=[blk, blk],
        out_shape=[jax.ShapeDtypeStruct((T, FF), BF), jax.ShapeDtypeStruct((T, FF), BF)],
        compiler_params=_cparams(("parallel", "parallel")),
    )(dh, wd, a, b)


def ffn_dhn(da, db, wg, wu, *, tm, tn, name, deps=()):
    T, FF = da.shape
    nb, D, FB = wg.shape
    dims = (((1,), (1,)), ((), ()))

    def body(da_ref, db_ref, g_ref, u_ref, *rest):
        o_ref = rest[len(deps)]
        acc = None
        for k in range(nb):
            cols = slice(k * FB, (k + 1) * FB)
            p = lax.dot_general(da_ref[:, cols], g_ref[k], dims, preferred_element_type=F32)
            p = p + lax.dot_general(db_ref[:, cols], u_ref[k], dims, preferred_element_type=F32)
            acc = p if acc is None else acc + p
        o_ref[...] = acc.astype(BF)

    aspec = pl.BlockSpec((tm, FF), lambda i, j: (i, 0))
    wspec = pl.BlockSpec((nb, tn, FB), lambda i, j: (0, j, 0))
    return pl.pallas_call(
        body, name=name, grid=(T // tm, D // tn),
        in_specs=[aspec, aspec, wspec, wspec] + [pl.BlockSpec(memory_space=pl.ANY)] * len(deps),
        out_specs=pl.BlockSpec((tm, tn), lambda i, j: (i, j)),
        out_shape=jax.ShapeDtypeStruct((T, D), BF),
        compiler_params=_cparams(("parallel", "parallel")),
    )(da, db, wg, wu, *deps)


def final_loss(h2, tgt, g, *, seq, tr, name):
    D = h2.shape[1]

    def body(step, ri, fi, ro, ao):
        x = ri[0][...]
        r = lax.rsqrt(jnp.mean(x * x, axis=-1, keepdims=True) + EPS)
        n = x * r
        gg = fi[0][...]
        row = lax.broadcasted_iota(jnp.int32, (tr, 1), 0) + step * tr
        live = jnp.logical_and(row >= N_META, row < N_META + seq)
        e = jnp.where(live, n * gg - ri[1][...], 0.0)
        _acc(step, ao[0], jnp.broadcast_to(0.5 * jnp.sum(jnp.mean(e * e, axis=-1, keepdims=True)), (1, 128)))
        dy = e * (1.0 / D)
        _acc(step, ao[1], _colsum(dy * n))
        dn = dy * gg
        dh = r * (dn - n * jnp.mean(dn * n, axis=-1, keepdims=True))
        ro[0][...] = dh
        ro[1][...] = dh.astype(BF)

    return rowwise(body, [h2, tgt], [g], [(D, F32), (D, BF)], [(1, 128), (1, D)], tr=tr, name=name)


def _shifted_copies(s_ref, T, rc):
    for r in range(1, 8):
        for t0 in range(0, T, rc):
            s_ref[r, t0:t0 + rc, :] = s_ref[0, t0 + r:t0 + r + rc, :]
        s_ref[r, T:T + CONV_PAD - 8, :] = s_ref[0, T + r:T + r + CONV_PAD - 8, :]


def _shifted(s_ref, start, rows):
    return s_ref[start % 8, start - start % 8:start - start % 8 + rows, :]


def dwconv_fwd(x, w, b, *, tc, rc, name, deps=()):
    T, C = x.shape
    nchunk = T // rc
    assert nchunk * rc == T and C % tc == 0

    def body(x_ref, w_ref, b_ref, *rest):
        y_ref, xs_ref = rest[len(deps):]
        xs_ref[0, 0:CONV_PAD, :] = jnp.zeros((CONV_PAD, tc), F32)
        xs_ref[0, CONV_PAD:CONV_PAD + T, :] = x_ref[...]
        _shifted_copies(xs_ref, T, rc)
        off = CONV_PAD - (CONV_WIDTH - 1)
        for ci in range(nchunk):
            t0 = ci * rc
            acc = jnp.broadcast_to(b_ref[...], (rc, tc))
            for j in range(CONV_WIDTH):
                acc = acc + w_ref[j:j + 1, :] * _shifted(xs_ref, t0 + off + j, rc)
            y_ref[t0:t0 + rc, :] = acc

    return pl.pallas_call(
        body, name=name, grid=(C // tc,),
        in_specs=[pl.BlockSpec((T, tc), lambda j: (0, j)), pl.BlockSpec((32, tc), lambda j: (0, j)),
                  pl.BlockSpec((1, tc), lambda j: (0, j))] + [pl.BlockSpec(memory_space=pl.ANY)] * len(deps),
        out_specs=pl.BlockSpec((T, tc), lambda j: (0, j)),
        out_shape=jax.ShapeDtypeStruct((T, C), F32),
        scratch_shapes=[pltpu.VMEM((8, T + CONV_PAD, tc), F32)],
        compiler_params=_cparams(("parallel",)),
    )(x, w, b, *deps)


def dwconv_bwd(x, dy, w, *, tc, rc, name):
    T, C = x.shape
    nchunk = T // rc
    assert nchunk * rc == T and C % tc == 0 and rc % 8 == 0

    def body(x_ref, dy_ref, w_ref, dx_ref, dw_ref, xs_ref, ds_ref):
        off = CONV_PAD - (CONV_WIDTH - 1)
        xs_ref[0, 0:CONV_PAD, :] = jnp.zeros((CONV_PAD, tc), F32)
        xs_ref[0, CONV_PAD:CONV_PAD + T, :] = x_ref[...]
        _shifted_copies(xs_ref, T, rc)
        ds_ref[0, 0:T, :] = dy_ref[...]
        ds_ref[0, T:T + CONV_PAD, :] = jnp.zeros((CONV_PAD, tc), F32)
        _shifted_copies(ds_ref, T, rc)
        for ci in range(nchunk):
            t0 = ci * rc
            acc = jnp.zeros((rc, tc), F32)
            for j in range(CONV_WIDTH):
                acc = acc + w_ref[j:j + 1, :] * _shifted(ds_ref, t0 + (CONV_WIDTH - 1) - j, rc)
            dx_ref[t0:t0 + rc, :] = acc
        dw_ref[...] = jnp.zeros((32, tc), F32)
        for j in range(CONV_WIDTH):
            acc = jnp.zeros((8, tc), F32)
            for ci in range(nchunk):
                t0 = ci * rc
                pr = dy_ref[t0:t0 + rc, :] * _shifted(xs_ref, t0 + off + j, rc)
                acc = acc + jnp.sum(pr.reshape(rc // 8, 8, tc), axis=0)
            dw_ref[j:j + 1, :] = jnp.sum(acc, axis=0, keepdims=True)

    return pl.pallas_call(
        body, name=name, grid=(C // tc,),
        in_specs=[pl.BlockSpec((T, tc), lambda j: (0, j)), pl.BlockSpec((T, tc), lambda j: (0, j)),
                  pl.BlockSpec((32, tc), lambda j: (0, j))],
        out_specs=[pl.BlockSpec((T, tc), lambda j: (0, j)), pl.BlockSpec((32, tc), lambda j: (0, j))],
        out_shape=[jax.ShapeDtypeStruct((T, C), F32), jax.ShapeDtypeStruct((32, C), F32)],
        scratch_shapes=[pltpu.VMEM((8, T + CONV_PAD, tc), F32), pltpu.VMEM((8, T + CONV_PAD, tc), F32)],
        compiler_params=_cparams(("parallel",)),
    )(x, dy, w)


def q_proj(cq, wq, cos, sin, *, scale, name):
    T, R = cq.shape
    H = wq.shape[1] // HEAD_PAD

    def body(a_ref, b_ref, c_ref, s_ref, o_ref):
        p = jnp.dot(a_ref[...], b_ref[...], preferred_element_type=F32)
        o_ref[:, :QK_NOPE] = (p[:, :QK_NOPE] * scale).astype(BF)
        x = p[:, QK_NOPE:]
        o_ref[:, QK_NOPE:] = ((x * c_ref[...] + _rot_half(x) * s_ref[...]) * scale).astype(BF)

    return pl.pallas_call(
        body, name=name, grid=(H,),
        in_specs=[pl.BlockSpec((T, R), lambda h: (0, 0)), pl.BlockSpec((R, HEAD_PAD), lambda h: (0, h)),
                  pl.BlockSpec((T, 128), lambda h: (0, 0)), pl.BlockSpec((T, 128), lambda h: (0, 0))],
        out_specs=pl.BlockSpec((T, HEAD_PAD), lambda h: (0, h)),
        out_shape=jax.ShapeDtypeStruct((T, H * HEAD_PAD), BF),
        compiler_params=_cparams(("parallel",)),
    )(cq, wq, cos, sin)


def k_proj(ckv, wk, kr, *, name):
    T, R = ckv.shape
    H = wk.shape[1] // QK_NOPE

    def body(a_ref, b_ref, kr_ref, o_ref):
        p = jnp.dot(a_ref[...], b_ref[...], preferred_element_type=F32)
        o_ref[:, :QK_NOPE] = p.astype(BF)
        o_ref[:, QK_NOPE:] = kr_ref[...].astype(BF)

    return pl.pallas_call(
        body, name=name, grid=(H,),
        in_specs=[pl.BlockSpec((T, R), lambda h: (0, 0)), pl.BlockSpec((R, QK_NOPE), lambda h: (0, h)),
                  pl.BlockSpec((T, 128), lambda h: (0, 0))],
        out_specs=pl.BlockSpec((T, HEAD_PAD), lambda h: (0, h)),
        out_shape=jax.ShapeDtypeStruct((T, H * HEAD_PAD), BF),
        compiler_params=_cparams(("parallel",)),
    )(ckv, wk, kr)


def _causal_scores(q, k_ref, qi, CH, split):
    nt = (((1,), (1,)), ((), ()))
    pieces = []
    if not split:
        rows = slice(0, (qi + 1) * CH)
        s = lax.dot_general(q, k_ref[rows, :], nt, preferred_element_type=F32)
        keep = lax.broadcasted_iota(jnp.int32, s.shape, 1) <= lax.broadcasted_iota(jnp.int32, s.shape, 0) + qi * CH
        return [(rows, jnp.where(keep, s, -1e30))]
    if qi > 0:
        rows = slice(0, qi * CH)
        pieces.append((rows, lax.dot_general(q, k_ref[rows, :], nt, preferred_element_type=F32)))
    rows = slice(qi * CH, (qi + 1) * CH)
    s = lax.dot_general(q, k_ref[rows, :], nt, preferred_element_type=F32)
    keep = lax.broadcasted_iota(jnp.int32, s.shape, 1) <= lax.broadcasted_iota(jnp.int32, s.shape, 0)
    pieces.append((rows, jnp.where(keep, s, -1e30)))
    return pieces


def attn_fwd(q, k, v, *, nch, name):
    T = q.shape[0]
    H = q.shape[1] // HEAD_PAD
    CH = T // nch
    assert CH * nch == T and CH % 16 == 0

    def body(q_ref, k_ref, v_ref, o_ref, lse_ref):
        for qi in range(nch):
            L = (qi + 1) * CH
            pieces = _causal_scores(q_ref[qi * CH:L, :], k_ref, qi, CH, True)
            m = functools.reduce(jnp.maximum, [jnp.max(s, axis=1, keepdims=True) for _, s in pieces])
            l, o = 0.0, 0.0
            for rows, s in pieces:
                p = jnp.exp(s - m)
                l = l + jnp.sum(p, axis=1, keepdims=True)
                o = o + jnp.dot(p.astype(BF), v_ref[rows, :], preferred_element_type=F32)
            o_ref[qi * CH:L, :] = (o / l).astype(BF)
            lse_ref[qi * CH:L, :] = jnp.broadcast_to(m + jnp.log(l), (CH, V_HEAD))

    return pl.pallas_call(
        body, name=name, grid=(H,),
        in_specs=[pl.BlockSpec((T, HEAD_PAD), lambda h: (0, h)), pl.BlockSpec((T, HEAD_PAD), lambda h: (0, h)),
                  pl.BlockSpec((T, V_HEAD), lambda h: (0, h))],
        out_specs=[pl.BlockSpec((T, V_HEAD), lambda h: (0, h)), pl.BlockSpec((T, V_HEAD), lambda h: (0, h))],
        out_shape=[jax.ShapeDtypeStruct((T, H * V_HEAD), BF), jax.ShapeDtypeStruct((T, H * V_HEAD), F32)],
        compiler_params=_cparams(("parallel",)),
    )(q, k, v)


def attn_bwd(q, k, v, o, do, lse, cos, sin, *, scale, nch, name):
    T = q.shape[0]
    H = q.shape[1] // HEAD_PAD
    CH = T // nch
    tn_dims = (((0,), (0,)), ((), ()))
    nt_dims = (((1,), (1,)), ((), ()))

    def body(q_ref, k_ref, v_ref, o_ref, do_ref, lse_ref, c_ref, s_ref,
             dq_ref, dk_ref, dv_ref, dkr_ref, dk_acc, dv_acc):
        h = pl.program_id(0)
        dk_acc[...] = jnp.zeros_like(dk_acc)
        dv_acc[...] = jnp.zeros_like(dv_acc)
        for qi in range(nch):
            L = (qi + 1) * CH
            rows = slice(qi * CH, L)
            qc, doc = q_ref[rows, :], do_ref[rows, :]
            lse_c = lse_ref[rows, 0:1]
            dsum = jnp.sum(doc.astype(F32) * o_ref[rows, :].astype(F32), axis=1, keepdims=True)
            dq = 0.0
            for krows, s in _causal_scores(qc, k_ref, qi, CH, False):
                p = jnp.exp(s - lse_c)
                dp = lax.dot_general(doc, v_ref[krows, :], nt_dims, preferred_element_type=F32)
                ds = (p * (dp - dsum)).astype(BF)
                dq = dq + jnp.dot(ds, k_ref[krows, :], preferred_element_type=F32)
                dk_acc[krows, :] += lax.dot_general(ds, qc, tn_dims, preferred_element_type=F32)
                dv_acc[krows, :] += lax.dot_general(p.astype(BF), doc, tn_dims, preferred_element_type=F32)
            dq_ref[rows, :QK_NOPE] = (dq[:, :QK_NOPE] * scale).astype(BF)
            dyr = dq[:, QK_NOPE:]
            dq_ref[rows, QK_NOPE:] = ((dyr * c_ref[rows, :] - _rot_half(dyr * s_ref[rows, :])) * scale).astype(BF)
        dk_ref[:, :QK_NOPE] = dk_acc[:, :QK_NOPE].astype(BF)
        dk_ref[:, QK_NOPE:] = jnp.zeros((T, HEAD_PAD - QK_NOPE), BF)
        dv_ref[...] = dv_acc[...].astype(BF)

        @pl.when(h == 0)
        def _():
            dkr_ref[...] = dk_acc[:, QK_NOPE:]

        @pl.when(h > 0)
        def _():
            dkr_ref[...] += dk_acc[:, QK_NOPE:]

    hp = pl.BlockSpec((T, HEAD_PAD), lambda h: (0, h))
    hv = pl.BlockSpec((T, V_HEAD), lambda h: (0, h))
    tab = pl.BlockSpec((T, 128), lambda h: (0, 0))
    return pl.pallas_call(
        body, name=name, grid=(H,),
        in_specs=[hp, hp, hv, hv, hv, hv, tab, tab],
        out_specs=[hp, hp, hv, tab],
        out_shape=[jax.ShapeDtypeStruct((T, H * HEAD_PAD), BF), jax.ShapeDtypeStruct((T, H * HEAD_PAD), BF),
                   jax.ShapeDtypeStruct((T, H * V_HEAD), BF), jax.ShapeDtypeStruct((T, 128), F32)],
        scratch_shapes=[pltpu.VMEM((T, HEAD_PAD), F32), pltpu.VMEM((T, V_HEAD), F32)],
        compiler_params=_cparams(("arbitrary",)),
    )(q, k, v, o, do, lse, cos, sin)


def _place():
    x, y, c = lax.axis_index("x"), lax.axis_index("y"), lax.axis_index("c")
    return x, y, c


def _other_chips(x, y):
    return [(1 - x, y), (x, 1 - y), (1 - x, 1 - y)]


def gather_devices(v, *, name):
    R, C = v.shape

    def body(v_ref, o_ref, send_sems, recv_sems):
        x, y, c = _place()
        me = 4 * x + 2 * y + c
        o_ref[me] = v_ref[...]
        copies = []
        for k in range(1, N_DEV):
            fx, fy, fc = (k >> 2) & 1, (k >> 1) & 1, k & 1
            to = (x ^ fx, y ^ fy, c ^ fc)
            cp = pltpu.make_async_remote_copy(src_ref=v_ref, dst_ref=o_ref.at[me], send_sem=send_sems.at[k - 1],
                                              recv_sem=recv_sems.at[k - 1], device_id=to, device_id_type=MESH)
            cp.start()
            copies.append(cp)
        for k in range(1, N_DEV):
            fx, fy, fc = (k >> 2) & 1, (k >> 1) & 1, k & 1
            frm = 4 * (x ^ fx) + 2 * (y ^ fy) + (c ^ fc)
            pltpu.make_async_remote_copy(src_ref=v_ref, dst_ref=o_ref.at[frm], send_sem=send_sems.at[k - 1],
                                         recv_sem=recv_sems.at[k - 1], device_id=(x, y, c),
                                         device_id_type=MESH).wait_recv()
        for cp in copies:
            cp.wait_send()

    return pl.pallas_call(
        body, name=name, out_shape=jax.ShapeDtypeStruct((N_DEV, R, C), F32),
        in_specs=[pl.BlockSpec(memory_space=pltpu.VMEM)], out_specs=pl.BlockSpec(memory_space=pltpu.VMEM),
        scratch_shapes=[pltpu.SemaphoreType.DMA((N_DEV - 1,)), pltpu.SemaphoreType.DMA((N_DEV - 1,))],
    )(v)


HBM_SPEC = pl.BlockSpec(memory_space=pltpu.HBM)
SEM_SPEC = pl.BlockSpec(memory_space=pltpu.SEMAPHORE)
DATAFLOW = pltpu.SideEffectType.DATAFLOW_SIDE_EFFECTING


def _hbm(a):
    return pltpu.with_memory_space_constraint(a, pltpu.HBM)


def _half(ref, lead, c, axis):
    rows, cols = ref.shape[-2], ref.shape[-1]
    if axis == 0:
        return ref.at[(*lead, pl.ds(c * (rows // 2), rows // 2))]
    return ref.at[(*lead, slice(None), pl.ds(c * (cols // 2), cols // 2))]


def copies_start(arrs, plan, n_sems, after, *, name):
    n = len(arrs)

    def body(*refs):
        outs = refs[n + 1:2 * n + 1]
        send_sems, recv_sems, token = refs[2 * n + 1:]
        for cp in plan(outs, send_sems, recv_sems, False):
            cp.start()
        token[...] = jnp.zeros_like(token)

    res = pl.pallas_call(
        body, name=name,
        out_shape=tuple(pltpu.HBM(a.shape, a.dtype) for a in arrs)
        + (pltpu.SemaphoreType.DMA((n_sems,)), pltpu.SemaphoreType.DMA((n_sems,)), jax.ShapeDtypeStruct((8, 128), F32)),
        in_specs=(HBM_SPEC,) * n + (pl.BlockSpec(memory_space=pl.ANY),),
        out_specs=(HBM_SPEC,) * n + (SEM_SPEC, SEM_SPEC, pl.BlockSpec(memory_space=pltpu.VMEM)),
        input_output_aliases={i: i for i in range(n)},
        compiler_params=pltpu.CompilerParams(has_side_effects=DATAFLOW),
    )(*[_hbm(a) for a in arrs], after)
    return list(res[:n]), res[n], res[n + 1], res[n + 2]


def copies_wait(arrs, plan, send_sems, recv_sems, after, *, name):
    n = len(arrs)
    afters = list(after) if isinstance(after, (list, tuple)) else [after]

    def body(*refs):
        for cp in plan(refs[:n], refs[n], refs[n + 1], True):
            cp.wait_send()
            cp.wait_recv()

    res = pl.pallas_call(
        body, name=name,
        out_shape=tuple(pltpu.HBM(a.shape, a.dtype) for a in arrs),
        in_specs=(HBM_SPEC,) * n + (SEM_SPEC, SEM_SPEC) + (pl.BlockSpec(memory_space=pl.ANY),) * len(afters),
        out_specs=(HBM_SPEC,) * n,
        input_output_aliases={i: i for i in range(n)},
        compiler_params=pltpu.CompilerParams(has_side_effects=DATAFLOW),
    )(*arrs, send_sems, recv_sems, *afters)
    return list(res)


def gather_plan(axes):
    def plan(bufs, send_sems, recv_sems, receiving):
        x, y, c = _place()
        me = 2 * x + y
        out = []
        for i, b in enumerate(bufs):
            for j, (cx, cy) in enumerate(_other_chips(x, y)):
                src = _half(b, (me,), c, axes[i])
                dst = _half(b, (2 * cx + cy,), c, axes[i]) if receiving else src
                out.append(pltpu.make_async_remote_copy(
                    src_ref=src, dst_ref=dst, send_sem=send_sems.at[3 * i + j], recv_sem=recv_sems.at[3 * i + j],
                    device_id=(cx, cy, c), device_id_type=MESH))
        return out
    return plan


def _half_part(ref, lead, c, axis, part):
    rows, cols = ref.shape[-2], ref.shape[-1]
    span = rows // 2 if axis == 0 else rows
    p1 = -(-(span // 2) // 16) * 16
    lo, n = (0, p1) if part == 0 else (p1, span - p1)
    if axis == 0:
        return ref.at[(*lead, pl.ds(c * span + lo, n))]
    return ref.at[(*lead, pl.ds(lo, n), pl.ds(c * (cols // 2), cols // 2))]


def near_plan(axes):
    def plan(bufs, send_sems, recv_sems, receiving):
        x, y, c = _place()
        me = 2 * x + y
        out = []
        for i, b in enumerate(bufs):
            for j, (cx, cy) in enumerate(_other_chips(x, y)[:2]):
                src = _half(b, (me,), c, axes[i])
                dst = _half(b, (2 * cx + cy,), c, axes[i]) if receiving else src
                out.append(pltpu.make_async_remote_copy(
                    src_ref=src, dst_ref=dst, send_sem=send_sems.at[2 * i + j], recv_sem=recv_sems.at[2 * i + j],
                    device_id=(cx, cy, c), device_id_type=MESH))
        return out
    return plan


def relay_plan(axes):
    def plan(bufs, send_sems, recv_sems, receiving):
        x, y, c = _place()
        (xn, yn, dg) = [2 * cx + cy for cx, cy in _other_chips(x, y)]
        to = [(x, 1 - y, c), (1 - x, y, c), (x, y, 1 - c), (x, y, 1 - c)]
        out = []
        for i, b in enumerate(bufs):
            if receiving:
                blks = [_half_part(b, (dg,), c, axes[i], 0), _half_part(b, (dg,), c, axes[i], 1),
                        _half(b, (xn,), 1 - c, axes[i]), _half(b, (yn,), 1 - c, axes[i])]
            else:
                blks = [_half_part(b, (xn,), c, axes[i], 0), _half_part(b, (yn,), c, axes[i], 1),
                        _half(b, (xn,), c, axes[i]), _half(b, (yn,), c, axes[i])]
            for k in range(4):
                out.append(pltpu.make_async_remote_copy(
                    src_ref=blks[k], dst_ref=blks[k], send_sem=send_sems.at[4 * i + k],
                    recv_sem=recv_sems.at[4 * i + k], device_id=to[k], device_id_type=MESH))
        return out
    return plan


def far_forward_plan(axes):
    def plan(bufs, send_sems, recv_sems, receiving):
        x, y, c = _place()
        dg = 2 * (1 - x) + (1 - y)
        out = []
        for i, b in enumerate(bufs):
            blk = _half(b, (dg,), (1 - c) if receiving else c, axes[i])
            out.append(pltpu.make_async_remote_copy(
                src_ref=blk, dst_ref=blk, send_sem=send_sems.at[i], recv_sem=recv_sems.at[i],
                device_id=(x, y, 1 - c), device_id_type=MESH))
        return out
    return plan


def forward_plan(axes):
    def plan(bufs, send_sems, recv_sems, receiving):
        x, y, c = _place()
        out = []
        for i, b in enumerate(bufs):
            for j, (cx, cy) in enumerate(_other_chips(x, y)):
                blk = _half(b, (2 * cx + cy,), (1 - c) if receiving else c, axes[i])
                out.append(pltpu.make_async_remote_copy(
                    src_ref=blk, dst_ref=blk, send_sem=send_sems.at[3 * i + j], recv_sem=recv_sems.at[3 * i + j],
                    device_id=(x, y, 1 - c), device_id_type=MESH))
        return out
    return plan


def scatter_plan(n):
    def plan(arrs, send_sems, recv_sems, receiving):
        x, y, c = _place()
        out = []
        for i in range(n):
            for j, (cx, cy) in enumerate(_other_chips(x, y)):
                out.append(pltpu.make_async_remote_copy(
                    src_ref=arrs[i].at[2 * cx + cy], dst_ref=arrs[n + i].at[j], send_sem=send_sems.at[3 * i + j],
                    recv_sem=recv_sems.at[3 * i + j], device_id=(cx, cy, c), device_id_type=MESH))
        return out
    return plan


def exchange_plan(axes):
    n = len(axes)

    def plan(arrs, send_sems, recv_sems, receiving):
        x, y, c = _place()
        out = []
        for i in range(n):
            for k in range(N_CHIPS):
                out.append(pltpu.make_async_remote_copy(
                    src_ref=_half(arrs[i], (k,), 1 - c, axes[i]), dst_ref=arrs[n + i].at[k],
                    send_sem=send_sems.at[N_CHIPS * i + k], recv_sem=recv_sems.at[N_CHIPS * i + k],
                    device_id=(x, y, 1 - c), device_id_type=MESH))
        return out
    return plan


def devices_plan(arrs, send_sems, recv_sems, receiving):
    x, y, c = _place()
    out = []
    for k in range(1, N_DEV):
        px, py, pc = x ^ ((k >> 2) & 1), y ^ ((k >> 1) & 1), c ^ (k & 1)
        slot = (4 * px + 2 * py + pc) if receiving else (4 * x + 2 * y + c)
        out.append(pltpu.make_async_remote_copy(
            src_ref=arrs[0], dst_ref=arrs[1].at[slot], send_sem=send_sems.at[k - 1], recv_sem=recv_sems.at[k - 1],
            device_id=(px, py, pc), device_id_type=MESH))
    return out


def share_halves(bufs, axes, after, *, name):
    n = len(bufs)

    def body(*refs):
        outs = refs[n + 1:2 * n + 1]
        send_sems, recv_sems = refs[2 * n + 1:]
        x, y, c = _place()
        sib = (x, y, 1 - c)
        cps = []
        for i in range(n):
            mine = _half(outs[i], (), c, axes[i])
            cp = pltpu.make_async_remote_copy(src_ref=mine, dst_ref=mine, send_sem=send_sems.at[i],
                                              recv_sem=recv_sems.at[i], device_id=sib, device_id_type=MESH)
            cp.start()
            cps.append(cp)
        for i in range(n):
            theirs = _half(outs[i], (), 1 - c, axes[i])
            pltpu.make_async_remote_copy(src_ref=theirs, dst_ref=theirs, send_sem=send_sems.at[i],
                                         recv_sem=recv_sems.at[i], device_id=sib, device_id_type=MESH).wait_recv()
        for cp in cps:
            cp.wait_send()

    anyspec = pl.BlockSpec(memory_space=pl.ANY)
    return pl.pallas_call(
        body, name=name,
        out_shape=[jax.ShapeDtypeStruct(b.shape, b.dtype) for b in bufs],
        in_specs=[anyspec] * (n + 1), out_specs=[anyspec] * n,
        input_output_aliases={i: i for i in range(n)},
        scratch_shapes=[pltpu.SemaphoreType.DMA((n,)), pltpu.SemaphoreType.DMA((n,))],
    )(*bufs, after)


def _row_tile(r, cols, itemsize, target_bytes=2 * 1024 * 1024):
    best = None
    for t in range(16, r + 1, 16):
        if r % t == 0 and t * cols * itemsize <= target_bytes:
            best = t
    return best if best is not None else r


def pair_sum(g, recv, half, axis, *, name):
    _, hr, hc = recv.shape
    tr = _row_tile(hr, hc, 4)
    nb = hr // tr
    if axis == 0:
        mine = pl.BlockSpec((None, tr, hc), lambda k, r, hf: (k, hf[0] * nb + r, 0))
    else:
        mine = pl.BlockSpec((None, tr, hc), lambda k, r, hf: (k, r, hf[0]))

    def body(half_ref, a_ref, b_ref, o_ref):
        o_ref[...] = (a_ref[...].astype(F32) + b_ref[...].astype(F32)).astype(BF)

    return pl.pallas_call(
        body, name=name,
        grid_spec=pltpu.PrefetchScalarGridSpec(
            num_scalar_prefetch=1, grid=(N_CHIPS, nb),
            in_specs=[mine, pl.BlockSpec((None, tr, hc), lambda k, r, hf: (k, r, 0))],
            out_specs=pl.BlockSpec((None, tr, hc), lambda k, r, hf: (k, r, 0))),
        out_shape=jax.ShapeDtypeStruct((N_CHIPS, hr, hc), BF),
        compiler_params=_cparams(("parallel", "parallel")),
    )(half, g, recv)


def chip_sum(s, recv, place, axis, *, name):
    _, hr, hc = s.shape
    tr = _row_tile(hr, hc, 4)
    nb = hr // tr
    if axis == 0:
        o_spec = pl.BlockSpec((tr, hc), lambda r, pc: (pc[1] * nb + r, 0))
        o_shape = (2 * hr, hc)
    else:
        o_spec = pl.BlockSpec((tr, hc), lambda r, pc: (r, pc[1]))
        o_shape = (hr, 2 * hc)

    def body(place_ref, a_ref, b_ref, o_ref):
        acc = a_ref[...].astype(F32)
        for j in range(3):
            acc = acc + b_ref[j].astype(F32)
        o_ref[...] = acc

    return pl.pallas_call(
        body, name=name,
        grid_spec=pltpu.PrefetchScalarGridSpec(
            num_scalar_prefetch=1, grid=(nb,),
            in_specs=[pl.BlockSpec((None, tr, hc), lambda r, pc: (pc[0], r, 0)),
                      pl.BlockSpec((3, tr, hc), lambda r, pc: (0, r, 0))],
            out_specs=o_spec),
        out_shape=jax.ShapeDtypeStruct(o_shape, F32),
        compiler_params=_cparams(("parallel",)),
    )(place, s, recv)


def sum_devices(v, *, name):
    _, R, C = v.shape

    def body(v_ref, o_ref):
        acc = v_ref[0]
        for k in range(1, N_DEV):
            acc = acc + v_ref[k]
        o_ref[...] = acc

    return pl.pallas_call(body, name=name, out_shape=jax.ShapeDtypeStruct((R, C), F32))(v)


def adamw(w, g, m, v, *, name):
    R, C = w.shape
    tr = _row_tile(R, C, 4, target_bytes=1024 * 1024)
    c1 = 1.0 / (1.0 - ADAM_B1 ** ADAM_STEP)
    c2 = 1.0 / (1.0 - ADAM_B2 ** ADAM_STEP)

    def body(w_ref, g_ref, m_ref, v_ref, go_ref, d_ref, nm_ref, nv_ref):
        gg = g_ref[...]
        go_ref[...] = gg
        nm = ADAM_B1 * m_ref[...] + (1.0 - ADAM_B1) * gg
        nv = ADAM_B2 * v_ref[...] + (1.0 - ADAM_B2) * (gg * gg)
        nm_ref[...] = nm
        nv_ref[...] = nv
        d_ref[...] = -ADAM_LR * ((nm * c1) / (jnp.sqrt(nv * c2) + ADAM_EPS) + ADAM_WD * w_ref[...])

    spec = pl.BlockSpec((tr, C), lambda i: (i, 0))
    return pl.pallas_call(
        body, name=name, grid=(R // tr,), in_specs=[spec] * 4, out_specs=[spec] * 4,
        out_shape=[jax.ShapeDtypeStruct((R, C), F32)] * 4, compiler_params=_cparams(("parallel",)),
    )(w, g, m, v)


def _rope_tables(T):
    pos = np.arange(T, dtype=np.float32)
    inv_freq = (ROPE_THETA ** (-np.arange(0, QK_ROPE, 2, dtype=np.float32) / QK_ROPE)).astype(np.float32)
    ang = pos[:, None] * inv_freq[None, :]
    z = np.zeros((T, 64), np.float32)
    cos = np.concatenate([np.cos(ang), np.cos(ang), z], axis=1).astype(np.float32)
    sin = np.concatenate([np.sin(ang), np.sin(ang), z], axis=1).astype(np.float32)
    return jnp.asarray(cos), jnp.asarray(sin)


def _pick(n, cands):
    for c in cands:
        if n % c == 0:
            return c
    return n


def kernel(x, meta_tokens, g_mix, w_in, b_glu, b_gate, w_dw, b_dw, g_conv_ln, b_conv_ln, w_conv_out, b_conv_out, g_q_lora, w_uq, g_kv_lora, w_uk, w_uv, w_attn_out, w_out, g_ffn, w_ffn_gate, w_ffn_up, w_ffn_down, g_final, loss_target, m_meta_tokens, m_g_mix, m_w_in, m_b_glu, m_b_gate, m_w_dw, m_b_dw, m_g_conv_ln, m_b_conv_ln, m_w_conv_out, m_b_conv_out, m_g_q_lora, m_w_uq, m_g_kv_lora, m_w_uk, m_w_uv, m_w_attn_out, m_w_out, m_g_ffn, m_w_ffn_gate, m_w_ffn_up, m_w_ffn_down, m_g_final, v_meta_tokens, v_g_mix, v_w_in, v_b_glu, v_b_gate, v_w_dw, v_b_dw, v_g_conv_ln, v_b_conv_ln, v_w_conv_out, v_b_conv_out, v_g_q_lora, v_w_uq, v_g_kv_lora, v_w_uk, v_w_uv, v_w_attn_out, v_w_out, v_g_ffn, v_w_ffn_gate, v_w_ffn_up, v_w_ffn_down, v_g_final):
    weights = dict(meta_tokens=meta_tokens, g_mix=g_mix, w_in=w_in, b_glu=b_glu, b_gate=b_gate, w_dw=w_dw, b_dw=b_dw,
                   g_conv_ln=g_conv_ln, b_conv_ln=b_conv_ln, w_conv_out=w_conv_out, b_conv_out=b_conv_out,
                   g_q_lora=g_q_lora, w_uq=w_uq, g_kv_lora=g_kv_lora, w_uk=w_uk, w_uv=w_uv, w_attn_out=w_attn_out,
                   w_out=w_out, g_ffn=g_ffn, w_ffn_gate=w_ffn_gate, w_ffn_up=w_ffn_up, w_ffn_down=w_ffn_down,
                   g_final=g_final)
    m_in = dict(meta_tokens=m_meta_tokens, g_mix=m_g_mix, w_in=m_w_in, b_glu=m_b_glu, b_gate=m_b_gate, w_dw=m_w_dw,
                b_dw=m_b_dw, g_conv_ln=m_g_conv_ln, b_conv_ln=m_b_conv_ln, w_conv_out=m_w_conv_out,
                b_conv_out=m_b_conv_out, g_q_lora=m_g_q_lora, w_uq=m_w_uq, g_kv_lora=m_g_kv_lora, w_uk=m_w_uk,
                w_uv=m_w_uv, w_attn_out=m_w_attn_out, w_out=m_w_out, g_ffn=m_g_ffn, w_ffn_gate=m_w_ffn_gate,
                w_ffn_up=m_w_ffn_up, w_ffn_down=m_w_ffn_down, g_final=m_g_final)
    v_in = dict(meta_tokens=v_meta_tokens, g_mix=v_g_mix, w_in=v_w_in, b_glu=v_b_glu, b_gate=v_b_gate, w_dw=v_w_dw,
                b_dw=v_b_dw, g_conv_ln=v_g_conv_ln, b_conv_ln=v_b_conv_ln, w_conv_out=v_w_conv_out,
                b_conv_out=v_b_conv_out, g_q_lora=v_g_q_lora, w_uq=v_w_uq, g_kv_lora=v_g_kv_lora, w_uk=v_w_uk,
                w_uv=v_w_uv, w_attn_out=v_w_attn_out, w_out=v_w_out, g_ffn=v_g_ffn, w_ffn_gate=v_w_ffn_gate,
                w_ffn_up=v_w_ffn_up, w_ffn_down=v_w_ffn_down, g_final=v_g_final)
    names = list(weights)

    seq, D = x.shape[1], x.shape[2]
    C = w_conv_out.shape[2]
    R = g_q_lora.shape[1]
    H = w_uv.shape[2] * N_CHIPS // V_HEAD
    FB = w_ffn_gate.shape[2]
    FF = FB * N_CHIPS
    length = N_META + seq
    T = -(-length // BLOCK_Q) * BLOCK_Q
    scale = (QK_NOPE + QK_ROPE) ** -0.5
    assert C == D and w_in.shape[2] * N_CHIPS == 2 * C + 2 * R + QK_ROPE + 2 * D

    xi, yi, ci = _place()
    chip = (2 * xi + yi).astype(jnp.int32)
    half_arr = ci.astype(jnp.int32).reshape(1)

    tr = _pick(T, (272, 256, 128))
    tmh = T // 2 if (T // 2) % 16 == 0 else T
    tmq = T // 4 if (T // 4) % 16 == 0 else T
    tmw = _pick(D, (1024, 512))
    nch = 4
    rc = _pick(T, (272, 256, 128))
    tcv = 128

    def tn_of(n, pref=512):
        return _pick(n, (pref, 384, 256, 128))

    grp_a = ["w_in"]
    grp_b = ["w_conv_out", "w_uq", "w_uk", "w_uv", "w_attn_out", "w_out"]
    grp_c = ["w_ffn_gate", "w_ffn_up", "w_ffn_down"]
    big = grp_a + grp_b + grp_c
    axes = {nm: 0 for nm in big}
    axes["w_in"] = 1
    shard = {nm: weights[nm][0] for nm in big}
    shard["w_in"] = jnp.swapaxes(w_in[0], 0, 1)

    def own_blocks(srcs):
        bufs = []
        for w in srcs:
            wb = w.astype(BF)
            bufs.append(lax.dynamic_update_slice(lax.empty((N_CHIPS,) + wb.shape, BF), wb[None], (chip, 0, 0)))
        return bufs

    def gather_begin(nms, bufs, after, tag):
        ax = [axes[nm] for nm in nms]
        bufs, ssem, rsem, tok = copies_start(bufs, gather_plan(ax), 3 * len(nms), after, name=f"gather_{tag}_start")
        return (nms, ax, bufs, ssem, rsem), tok

    def gather_middle(state, after, tag):
        nms, ax, bufs, ssem, rsem = state
        bufs = copies_wait(bufs, gather_plan(ax), ssem, rsem, after, name=f"gather_{tag}_wait")
        bufs, ssem, rsem, tok = copies_start(bufs, forward_plan(ax), 3 * len(nms), half_arr,
                                             name=f"gather_{tag}_forward_start")
        return (nms, ax, bufs, ssem, rsem), tok

    def gather_end(state, after, tag):
        nms, ax, bufs, ssem, rsem = state
        return dict(zip(nms, copies_wait(bufs, forward_plan(ax), ssem, rsem, after, name=f"gather_{tag}_forward_wait")))

    tiny = gather_devices(jnp.concatenate([meta_tokens, jnp.pad(w_dw[0], ((0, 1), (0, 0)))], axis=0),
                          name="gather_meta_w_dw")
    meta_full = jnp.concatenate([tiny[2 * k, :N_META] for k in range(N_CHIPS)], axis=1)
    w_dw_full = jnp.concatenate([tiny[2 * k, N_META:] for k in range(N_CHIPS)], axis=1)

    ax_a = [axes[nm] for nm in grp_a]
    bufs_a, ss_a, rs_a, tok_ga = copies_start(own_blocks([shard[nm] for nm in grp_a]), near_plan(ax_a),
                                              2 * len(grp_a), tiny, name="gather_a_near_start")
    tok_ga, held = lax.optimization_barrier((tok_ga, (x, loss_target, [shard[nm] for nm in grp_b + grp_c])))
    x_l, tgt_l, shard_bc = held
    h0 = jnp.concatenate([meta_full, x_l[0], jnp.zeros((T - length, D), F32)], axis=0)
    tgt = jnp.pad(tgt_l[0], ((N_META, T - length), (0, 0)))
    bufs_b, bufs_c = own_blocks(shard_bc[:len(grp_b)]), own_blocks(shard_bc[len(grp_b):])
    cos, sin = _rope_tables(T)

    bufs_a = copies_wait(bufs_a, near_plan(ax_a), ss_a, rs_a, [h0, tgt] + bufs_b + bufs_c, name="gather_a_near_wait")
    bufs_a, ss_a, rs_a, _ = copies_start(bufs_a, relay_plan(ax_a), 4 * len(grp_a), half_arr,
                                         name="gather_a_relay_start")
    bufs_a = copies_wait(bufs_a, relay_plan(ax_a), ss_a, rs_a, half_arr, name="gather_a_relay_wait")
    bufs_a, ss_a, rs_a, _ = copies_start(bufs_a, far_forward_plan(ax_a), len(grp_a), half_arr,
                                         name="gather_a_far_start")
    G = dict(zip(grp_a, copies_wait(bufs_a, far_forward_plan(ax_a), ss_a, rs_a, half_arr, name="gather_a_far_wait")))
    st_gb, tok_gb = gather_begin(grp_b, bufs_b, G["w_in"], "b")
    st_gc, tok_gc = gather_begin(grp_c, bufs_c, tok_gb, "c")

    WT_in = G["w_in"].reshape(N_CHIPS * G["w_in"].shape[1], D)
    o1, o2 = 2 * C, 2 * C + 2 * R + QK_ROPE
    WT_glu = WT_in[:o1]
    WT_small = jnp.pad(WT_in[o1:o2], ((0, 128 - QK_ROPE), (0, 0)))
    WT_gate = WT_in[o2:]

    u = rms_fwd(h0, g_mix, tr=tr, name="rms_mix", deps=[tok_gb, tok_gc])
    z_glu = mm(u, WT_glu, mode="nt", tm=T, tn=tn_of(2 * C), tk=D, out_dtype=BF, name="mm_z_glu")
    z_small = mm(u, WT_small, mode="nt", tm=T, tn=tn_of(2 * R + 128, 384), tk=D, out_dtype=F32, name="mm_z_small")
    z_gate = mm(u, WT_gate, mode="nt", tm=T, tn=tn_of(2 * D), tk=D, out_dtype=BF, name="mm_z_gate")

    c0 = glu_fwd(z_glu, b_glu, tr=tr, name="glu_fwd")
    c1 = dwconv_fwd(c0, w_dw_full, b_dw, tc=tcv, rc=rc, name="dwconv_fwd")
    st_gb, tok_gb = gather_middle(st_gb, c1, "b")
    c3 = ln_silu_fwd(c1, g_conv_ln, b_conv_ln, tr=tr, name="ln_silu_fwd")
    cq, ckv, kr = lora_norm_fwd(z_small, g_q_lora, g_kv_lora, cos, sin, tr=tr, name="lora_norm_fwd", deps=[tok_gb])

    def cols_full(a):
        return jnp.concatenate([a[k] for k in range(N_CHIPS)], axis=1)

    G.update(gather_end(st_gb, cq, "b"))
    W_co = G["w_conv_out"].reshape(C, D)
    W_uq = cols_full(G["w_uq"]).reshape(R, H, QK_NOPE + QK_ROPE)
    Wq_p = jnp.pad(W_uq, ((0, 0), (0, 0), (0, HEAD_PAD - QK_NOPE - QK_ROPE))).reshape(R, H * HEAD_PAD)
    W_uk = cols_full(G["w_uk"])
    W_uv = cols_full(G["w_uv"])
    W_ao = G["w_attn_out"].reshape(H * V_HEAD, D)
    W_o = G["w_out"].reshape(D, D)

    y_conv = mm(c3, W_co, mode="nn", tm=T, tn=tn_of(D), tk=C, out_dtype=BF, name="mm_y_conv")

    q = q_proj(cq, Wq_p, cos, sin, scale=scale, name="q_proj")
    kk = k_proj(ckv, W_uk, kr, name="k_proj")
    vv = mm(ckv, W_uv, mode="nn", tm=T, tn=tn_of(H * V_HEAD), tk=R, out_dtype=BF, name="mm_v")
    o_attn, lse = attn_fwd(q, kk, vv, nch=nch, name="attn_fwd")
    y_attn = mm(o_attn, W_ao, mode="nn", tm=T, tn=tn_of(D), tk=H * V_HEAD, out_dtype=BF, name="mm_y_attn")

    mix = mix_fwd(z_gate, y_conv, y_attn, b_gate, b_conv_out, tr=tr, name="mix_fwd")
    st_gc, tok_gc = gather_middle(st_gc, mix, "c")
    h1 = mm(mix, W_o, mode="nn", tm=T, tn=tn_of(D), tk=D, out_dtype=F32, name="mm_h1", res=h0, deps=[tok_gc])

    hn = rms_fwd(h1, g_ffn, tr=tr, name="rms_ffn")

    G.update(gather_end(st_gc, hn, "c"))
    W_fg, W_fu = G["w_ffn_gate"], G["w_ffn_up"]
    W_fd = G["w_ffn_down"].reshape(FF, D)

    fa, fb, f = ffn_fwd(hn, W_fg, W_fu, tm=tmq, name="ffn_fwd")
    h2 = mm(f, W_fd, mode="nn", tm=tmh, tn=tn_of(D, 256), tk=FF, out_dtype=F32, name="mm_h2", res=h1)

    place_arr = jnp.stack([chip, ci.astype(jnp.int32)])

    def col_blocks(a):
        r, cfull = a.shape
        return a.reshape(r, N_CHIPS, cfull // N_CHIPS).transpose(1, 0, 2)

    def rs_begin(nms, gls, after, tag):
        ax = [axes[nm] for nm in nms]
        lands = []
        for g, a in zip(gls, ax):
            _, r, cfull = g.shape
            lands.append(lax.empty((N_CHIPS, r // 2, cfull) if a == 0 else (N_CHIPS, r, cfull // 2), BF))
        arrs, ssem, rsem, tok = copies_start(list(gls) + lands, exchange_plan(ax), N_CHIPS * len(nms), after,
                                             name=f"rs_{tag}_exchange_start")
        return (nms, ax, arrs, ssem, rsem), tok

    def rs_middle(state, after, tag):
        nms, ax, arrs, ssem, rsem = state
        n = len(nms)
        arrs = copies_wait(arrs, exchange_plan(ax), ssem, rsem, after, name=f"rs_{tag}_exchange_wait")
        pair = [pair_sum(g, r, half_arr, a, name="rs_pair_sum_" + nm)
                for nm, g, r, a in zip(nms, arrs[:n], arrs[n:], ax)]
        lands = [lax.empty((3,) + p.shape[1:], BF) for p in pair]
        arrs, ssem, rsem, tok = copies_start(pair + lands, scatter_plan(n), 3 * n, half_arr,
                                             name=f"rs_{tag}_scatter_start")
        return (nms, ax, arrs, ssem, rsem), tok

    def rs_end(state, after, tag):
        nms, ax, arrs, ssem, rsem = state
        n = len(nms)
        arrs = copies_wait(arrs, scatter_plan(n), ssem, rsem, after, name=f"rs_{tag}_scatter_wait")
        return [chip_sum(s, r, place_arr, a, name="rs_chip_sum_" + nm)
                for nm, s, r, a in zip(nms, arrs[:n], arrs[n:], ax)]

    d_h2, d_h2b, loss_p, d_g_final = final_loss(h2, tgt, g_final.reshape(1, D), seq=seq, tr=tr, name="final_loss")
    loss = lax.psum(loss_p[0, 0], ("x", "y", "c"))

    dW_fd = mm(f, d_h2b, mode="tn", tm=FB, tn=tn_of(D), tk=T, out_dtype=BF, name="mm_dw_ffn_down")
    d_a, d_b = ffn_dact(d_h2b, W_fd, fa, fb, tm=tmq, name="ffn_dact")
    dW_fg = mm(hn, d_a, mode="tn", tm=tmw, tn=FB, tk=T, out_dtype=BF, name="mm_dw_ffn_gate", out_blocks=N_CHIPS)
    dW_fu = mm(hn, d_b, mode="tn", tm=tmw, tn=FB, tk=T, out_dtype=BF, name="mm_dw_ffn_up", out_blocks=N_CHIPS)
    st_c, tok_rc = rs_begin(grp_c, [dW_fg, dW_fu, dW_fd.reshape(N_CHIPS, FB, D)], half_arr, "c")
    d_hn = ffn_dhn(d_a, d_b, W_fg, W_fu, tm=tmq, tn=_pick(D, (256,)), name="ffn_dhn", deps=[tok_rc])
    st_c, tok_rc = rs_middle(st_c, d_hn, "c")
    d_h1, d_h1b, d_g_ffn = rms_bwd(d_hn, h1, g_ffn, d_h2, tr=tr, name="rms_ffn_bwd", deps=[tok_rc])

    d_mix = mm(d_h1b, W_o, mode="nt", tm=T, tn=tn_of(D), tk=D, out_dtype=BF, name="mm_d_mix")
    dW_o = mm(mix, d_h1b, mode="tn", tm=tmw, tn=tn_of(D), tk=T, out_dtype=BF, name="mm_dw_out")
    d_yc, d_ya, d_zgate, d_b_gate, d_b_co = mix_bwd(z_gate, y_conv, y_attn, d_mix, b_gate, b_conv_out, tr=tr,
                                                    name="mix_bwd")

    dW_co = mm(c3, d_yc, mode="tn", tm=tmw, tn=tn_of(D), tk=T, out_dtype=BF, name="mm_dw_conv_out")
    d_c3 = mm(d_yc, W_co, mode="nt", tm=T, tn=tn_of(C), tk=D, out_dtype=BF, name="mm_d_c3")
    d_c1, d_g_ln, d_b_ln, d_b_dw = ln_silu_bwd(c1, d_c3, g_conv_ln, b_conv_ln, tr=tr, name="ln_silu_bwd")
    d_c0, d_w_dw = dwconv_bwd(c0, d_c1, w_dw_full, tc=tcv, rc=rc, name="dwconv_bwd")
    d_zglu, d_b_glu = glu_bwd(z_glu, b_glu, d_c0, tr=tr, name="glu_bwd")

    dW_ao = mm(o_attn, d_ya, mode="tn", tm=_pick(H * V_HEAD, (1024, 512)), tn=tn_of(D), tk=T, out_dtype=BF, name="mm_dw_attn_out")
    d_o = mm(d_ya, W_ao, mode="nt", tm=T, tn=tn_of(H * V_HEAD), tk=D, out_dtype=BF, name="mm_d_o")
    d_q2, d_k2, d_v, d_kr = attn_bwd(q, kk, vv, o_attn, d_o, lse, cos, sin, scale=scale, nch=nch, name="attn_bwd")

    dWq_p = mm(cq, d_q2, mode="tn", tm=R, tn=tn_of(H * HEAD_PAD), tk=T, out_dtype=BF, name="mm_dw_uq")
    d_cq = mm(d_q2, Wq_p, mode="nt", tm=T, tn=R, tk=_pick(H * HEAD_PAD, (1024,)), out_dtype=F32, name="mm_d_cq")
    Wk_p = jnp.pad(W_uk.reshape(R, H, QK_NOPE), ((0, 0), (0, 0), (0, HEAD_PAD - QK_NOPE))).reshape(R, H * HEAD_PAD)
    dWk_p = mm(ckv, d_k2, mode="tn", tm=R, tn=tn_of(H * HEAD_PAD), tk=T, out_dtype=BF, name="mm_dw_uk")
    dW_uv = mm(ckv, d_v, mode="tn", tm=R, tn=tn_of(H * V_HEAD), tk=T, out_dtype=BF, name="mm_dw_uv")
    red_c = rs_end(st_c, [dWq_p, dWk_p, dW_uv, dW_co, dW_ao, dW_o], "c")
    dW_uq = dWq_p.reshape(R, H, HEAD_PAD)[:, :, :QK_NOPE + QK_ROPE].reshape(R, H * (QK_NOPE + QK_ROPE))
    dW_uk = dWk_p.reshape(R, H, HEAD_PAD)[:, :, :QK_NOPE].reshape(R, H * QK_NOPE)
    st_b, tok_rb = rs_begin(grp_b, [dW_co.reshape(N_CHIPS, C // N_CHIPS, D), col_blocks(dW_uq), col_blocks(dW_uk),
                                    col_blocks(dW_uv), dW_ao.reshape(N_CHIPS, H * V_HEAD // N_CHIPS, D),
                                    dW_o.reshape(N_CHIPS, D // N_CHIPS, D)], red_c[-1], "b")
    d_ckv = mm(d_k2, Wk_p, mode="nt", tm=T, tn=R, tk=_pick(H * HEAD_PAD, (1024,)), out_dtype=F32, name="mm_d_ckv_k",
               deps=[tok_rb])
    d_ckv = mm(d_v, W_uv, mode="nt", tm=T, tn=R, tk=_pick(H * V_HEAD, (1024,)), out_dtype=F32, name="mm_d_ckv_v",
               res=d_ckv)
    d_zsmall, d_g_q, d_g_kv = lora_norm_bwd(z_small, d_cq, d_ckv, d_kr, cos, sin, g_q_lora, g_kv_lora, tr=tr,
                                            name="lora_norm_bwd")
    st_b, tok_rb = rs_middle(st_b, d_zsmall, "b")

    tmi = _pick(2 * C, (1024, 512))
    dWT_in = lax.empty((WT_in.shape[0], D), BF)
    dWT_in = mm_tn_into(d_zglu, u, dWT_in, 0, tm=tmi, tn=tn_of(D), name="mm_dw_glu", deps=[tok_rb])
    dWT_in = mm_tn_into(d_zsmall, u, dWT_in, o1, tm=2 * R + 128, tn=tn_of(D), name="mm_dw_small")
    dWT_in = mm_tn_into(d_zgate, u, dWT_in, o2, tm=tmi, tn=tn_of(D), name="mm_dw_gate")
    red_b = rs_end(st_b, dWT_in, "b")
    st_a, tok_ra = rs_begin(grp_a, [dWT_in.reshape(N_CHIPS, dWT_in.shape[0] // N_CHIPS, D)], red_b[-1], "a")
    d_u = mm(d_zglu, WT_glu, mode="nn", tm=T, tn=tn_of(D), tk=_pick(2 * C, (2048, 1024)), out_dtype=F32, name="mm_d_u_glu",
             deps=[tok_ra])
    st_a, tok_ra = rs_middle(st_a, d_u, "a")
    d_u = mm(d_zsmall, WT_small, mode="nn", tm=T, tn=tn_of(D), tk=2 * R + 128, out_dtype=F32, name="mm_d_u_small",
             res=d_u, deps=[tok_ra])
    d_u = mm(d_zgate, WT_gate, mode="nn", tm=T, tn=tn_of(D), tk=_pick(2 * D, (2048, 1024)), out_dtype=F32,
             name="mm_d_u_gate", res=d_u)
    d_h0, _, d_g_mix = rms_bwd(d_u, h0, g_mix, d_h1, tr=tr, name="rms_mix_bwd")

    grad_x = d_h0[N_META:length][None]

    small = ["g_mix", "b_glu", "b_gate", "b_dw", "g_conv_ln", "b_conv_ln", "b_conv_out", "g_q_lora", "g_kv_lora",
             "g_ffn", "g_final", "w_dw", "meta_tokens"]
    sgrads = dict(g_mix=d_g_mix, b_glu=d_b_glu, b_gate=d_b_gate, b_dw=d_b_dw, g_conv_ln=d_g_ln, b_conv_ln=d_b_ln,
                  b_conv_out=d_b_co, g_q_lora=d_g_q, g_kv_lora=d_g_kv, g_ffn=d_g_ffn, g_final=d_g_final,
                  w_dw=d_w_dw[:CONV_WIDTH], meta_tokens=d_h0[:N_META])
    sizes = [int(np.prod(sgrads[nm].shape)) for nm in small]
    packed = jnp.concatenate([sgrads[nm].reshape(-1) for nm in small]).reshape(-1, 128)
    sm_arrs, sm_ss, sm_rs, tok_sm = copies_start([packed, lax.empty((N_DEV,) + packed.shape, F32)], devices_plan,
                                                 N_DEV - 1, packed, name="gather_small_start")

    grads, delta, new_m, new_v, delta_2d = {}, {}, {}, {}, {}

    def adamw_2d(nm, w2, g2, m2, v2, back):
        g_, d_, m_, v_ = adamw(w2, g2, m2, v2, name="adamw_" + nm)
        delta_2d[nm] = d_
        grads[nm], delta[nm], new_m[nm], new_v[nm] = back(g_), back(d_), back(m_), back(v_)

    reduced = dict(zip(grp_b + grp_c, share_halves(red_b + red_c, [axes[nm] for nm in grp_b + grp_c], tok_sm,
                                                   name="rs_share_halves_bc")))
    for nm in grp_b + grp_c + grp_a:
        if nm == grp_a[0]:
            red_a = rs_end(st_a, delta_2d[grp_c[-1]], "a")
            reduced.update(zip(grp_a, share_halves(red_a, [axes[nm] for nm in grp_a], tok_sm,
                                                   name="rs_share_halves_a")))
        shp = weights[nm].shape
        if axes[nm] == 1:
            adamw_2d(nm, shard[nm], reduced[nm], jnp.swapaxes(m_in[nm][0], 0, 1), jnp.swapaxes(v_in[nm][0], 0, 1),
                     lambda a: jnp.swapaxes(a, 0, 1)[None])
        else:
            two = (shp[-2], shp[-1])
            adamw_2d(nm, weights[nm].reshape(two), reduced[nm], m_in[nm].reshape(two), v_in[nm].reshape(two),
                     lambda a, shp=shp: a.reshape(shp))

    sm_arrs = copies_wait(sm_arrs, devices_plan, sm_ss, sm_rs, delta_2d[grp_a[0]], name="gather_small_wait")
    me = (4 * xi + 2 * yi + ci).astype(jnp.int32)
    parts = lax.dynamic_update_slice(sm_arrs[1], sm_arrs[0][None], (me, 0, 0))
    summed = sum_devices(parts, name="sum_small_grads").reshape(-1)
    offs = np.concatenate([[0], np.cumsum(sizes)])
    sfull = {nm: summed[int(offs[i]):int(offs[i + 1])].reshape(sgrads[nm].shape) for i, nm in enumerate(small)}
    for nm in small:
        gfull = sfull[nm]
        if nm == "w_dw":
            cb = C // N_CHIPS
            grads[nm] = lax.dynamic_slice(gfull, (0, chip * cb), (CONV_WIDTH, cb))[None]
        elif nm == "meta_tokens":
            cb = D // N_CHIPS
            grads[nm] = lax.dynamic_slice(gfull, (0, chip * cb), (N_META, cb))
        else:
            grads[nm] = gfull.reshape(weights[nm].shape)

    rep = [nm for nm in small if nm not in ("w_dw", "meta_tokens")]

    def pack(d):
        return jnp.concatenate([d[nm].reshape(-1) for nm in rep]).reshape(-1, 128)

    _, pd, pm, pv = adamw(pack(weights), pack(grads), pack(m_in), pack(v_in), name="adamw_small")
    rsz = [int(np.prod(weights[nm].shape)) for nm in rep]
    roff = np.concatenate([[0], np.cumsum(rsz)])
    for i, nm in enumerate(rep):
        sl = slice(int(roff[i]), int(roff[i + 1]))
        delta[nm] = pd.reshape(-1)[sl].reshape(weights[nm].shape)
        new_m[nm] = pm.reshape(-1)[sl].reshape(weights[nm].shape)
        new_v[nm] = pv.reshape(-1)[sl].reshape(weights[nm].shape)
    for nm in ["w_dw", "meta_tokens"]:
        shp = weights[nm].shape
        two = (shp[-2], shp[-1])
        adamw_2d(nm, weights[nm].reshape(two), grads[nm].reshape(two), m_in[nm].reshape(two), v_in[nm].reshape(two),
                 lambda a, shp=shp: a.reshape(shp))

    return (loss, grad_x, *[grads[nm] for nm in names], *[delta[nm] for nm in names],
            *[new_m[nm] for nm in names], *[new_v[nm] for nm in names])
```

```python
import functools

import numpy as np
import jax
import jax.numpy as jnp
from jax import lax
from jax.experimental import pallas as pl
from jax.experimental.pallas import tpu as pltpu

F32 = jnp.float32
BF = jnp.bfloat16
MESH = pl.DeviceIdType.MESH

N_META = 16
BLOCK_Q = 128
CONV_WIDTH = 31
CONV_PAD = 32
QK_NOPE = 128
QK_ROPE = 64
V_HEAD = 128
HEAD_PAD = 256
ROPE_THETA = 10000.0
EPS = 1e-6
ADAM_LR = 0.001
ADAM_B1 = 0.9
ADAM_B2 = 0.999
ADAM_EPS = 1e-08
ADAM_WD = 0.01
ADAM_STEP = 10
VMEM_LIMIT = 56 * 1024 * 1024
N_CHIPS = 4
N_DEV = 8


def _cparams(sem):
    return pltpu.CompilerParams(dimension_semantics=sem, vmem_limit_bytes=VMEM_LIMIT)


def mm(a, b, *, mode, tm, tn, tk, out_dtype, name, res=None, out_blocks=None, deps=(), n_out=None):
    b3 = b.ndim == 3
    if mode == "nn":
        M, K = a.shape
        N = b.shape[0] * b.shape[2] if b3 else b.shape[1]
        a_spec = pl.BlockSpec((tm, tk), lambda i, j, k: (i, k))
        if b3:
            per = b.shape[2] // tn
            b_spec = pl.BlockSpec((None, tk, tn), lambda i, j, k: (j // per, k, j % per))
        else:
            b_spec = pl.BlockSpec((tk, tn), lambda i, j, k: (k, j))
        dims = (((1,), (0,)), ((), ()))
    elif mode == "nt":
        M, K = a.shape
        N = n_out if n_out is not None else (b.shape[1] if b3 else b.shape[0])
        a_spec = pl.BlockSpec((tm, tk), lambda i, j, k: (i, k))
        if b3:
            per = b.shape[2] // tk
            b_spec = pl.BlockSpec((None, tn, tk), lambda i, j, k: (k // per, j, k % per))
        else:
            b_spec = pl.BlockSpec((tn, tk), lambda i, j, k: (j, k))
        dims = (((1,), (1,)), ((), ()))
    else:
        K, M = a.shape
        N = b.shape[1]
        a_spec = pl.BlockSpec((tk, tm), lambda i, j, k: (k, i))
        b_spec = pl.BlockSpec((tk, tn), lambda i, j, k: (k, j))
        dims = (((0,), (0,)), ((), ()))
    gm, gn, gk = M // tm, N // tn, K // tk
    assert gm * tm == M and gn * tn == N and gk * tk == K, (name, a.shape, b.shape, tm, tn, tk)
    if out_blocks is None:
        o_spec = pl.BlockSpec((tm, tn), lambda i, j, k: (i, j))
        o_shape = (M, N)
    else:
        nbw = N // out_blocks
        per_o = nbw // tn
        assert per_o * tn == nbw
        o_spec = pl.BlockSpec((None, tm, tn), lambda i, j, k: (j // per_o, i, j % per_o))
        o_shape = (out_blocks, M, nbw)
    has_res = res is not None

    def body(*refs):
        a_ref, b_ref = refs[0], refs[1]
        r_ref = refs[2] if has_res else None
        o_ref = refs[2 + has_res + len(deps)]
        acc = refs[-1]
        p = lax.dot_general(a_ref[...], b_ref[...], dims, preferred_element_type=F32)

        def finish(v):
            if has_res:
                v = v + r_ref[...]
            o_ref[...] = v.astype(o_ref.dtype)

        if gk == 1:
            finish(p)
        else:
            k = pl.program_id(2)

            @pl.when(k == 0)
            def _():
                acc[...] = p

            @pl.when(k > 0)
            def _():
                acc[...] += p

            @pl.when(k == gk - 1)
            def _():
                finish(acc[...])

    in_specs = [a_spec, b_spec]
    args = [a, b]
    if has_res:
        in_specs.append(pl.BlockSpec((tm, tn), lambda i, j, k: (i, j)))
        args.append(res)
    in_specs += [pl.BlockSpec(memory_space=pl.ANY)] * len(deps)
    args += list(deps)
    return pl.pallas_call(
        body, name=name, grid=(gm, gn, gk), in_specs=in_specs, out_specs=o_spec,
        out_shape=jax.ShapeDtypeStruct(o_shape, out_dtype),
        scratch_shapes=[pltpu.VMEM((tm, tn), F32)] if gk > 1 else [],
        compiler_params=_cparams(("parallel", "parallel", "arbitrary")),
    )(*args)


def mm_tn_into(a, b, buf, row0, *, tm, tn, name, deps=()):
    K, M = a.shape
    N = b.shape[1]
    gm, gn = M // tm, N // tn
    assert gm * tm == M and gn * tn == N and buf.shape[1] == N and row0 + M <= buf.shape[0]
    steps = gm * gn

    def body(a_ref, b_ref, buf_ref, *rest):
        o_ref, tile, sem = rest[len(deps):]
        i, j = pl.program_id(0), pl.program_id(1)
        s = i * gn + j
        slot = s % 2

        def out_copy(sl):
            return pltpu.make_async_copy(tile.at[sl], o_ref.at[pl.ds(row0 + i * tm, tm), pl.ds(j * tn, tn)],
                                         sem.at[sl])

        @pl.when(s >= 2)
        def _():
            out_copy(slot).wait()

        tile[slot] = lax.dot_general(a_ref[...], b_ref[...], (((0,), (0,)), ((), ())),
                                     preferred_element_type=F32).astype(tile.dtype)
        out_copy(slot).start()

        @pl.when(s == steps - 1)
        def _():
            out_copy(slot).wait()
            if steps >= 2:
                out_copy(1 - slot).wait()

    anyspec = pl.BlockSpec(memory_space=pl.ANY)
    return pl.pallas_call(
        body, name=name, grid=(gm, gn),
        in_specs=[pl.BlockSpec((K, tm), lambda i, j: (0, i)), pl.BlockSpec((K, tn), lambda i, j: (0, j)), anyspec]
        + [anyspec] * len(deps),
        out_specs=anyspec, out_shape=jax.ShapeDtypeStruct(buf.shape, buf.dtype),
        input_output_aliases={2: 0},
        scratch_shapes=[pltpu.VMEM((2, tm, tn), buf.dtype), pltpu.SemaphoreType.DMA((2,))],
        compiler_params=_cparams(("arbitrary", "arbitrary")),
    )(a, b, buf, *deps)


def rowwise(body, row_ins, full_ins, row_outs, acc_outs, *, tr, name, deps=()):
    T = row_ins[0].shape[0]
    assert T % tr == 0, (name, T, tr)
    n_ri, n_fi, n_ro = len(row_ins), len(full_ins), len(row_outs)
    n_in = n_ri + n_fi + len(deps)

    def kern(*refs):
        body(pl.program_id(0), refs[:n_ri], refs[n_ri:n_ri + n_fi], refs[n_in:n_in + n_ro], refs[n_in + n_ro:])

    in_specs = [pl.BlockSpec((tr, a.shape[1]), lambda i: (i, 0)) for a in row_ins]
    in_specs += [pl.BlockSpec(a.shape, lambda i: (0, 0)) for a in full_ins]
    in_specs += [pl.BlockSpec(memory_space=pl.ANY)] * len(deps)
    out_specs = [pl.BlockSpec((tr, c), lambda i: (i, 0)) for c, _ in row_outs]
    out_specs += [pl.BlockSpec(s, lambda i: (0, 0)) for s in acc_outs]
    out_shape = [jax.ShapeDtypeStruct((T, c), d) for c, d in row_outs]
    out_shape += [jax.ShapeDtypeStruct(s, F32) for s in acc_outs]
    return pl.pallas_call(
        kern, name=name, grid=(T // tr,), in_specs=in_specs, out_specs=out_specs, out_shape=out_shape,
        compiler_params=_cparams(("arbitrary",)),
    )(*row_ins, *full_ins, *deps)


def _acc(step, ref, val):
    @pl.when(step == 0)
    def _():
        ref[...] = val

    @pl.when(step > 0)
    def _():
        ref[...] += val


def _colsum(x):
    return jnp.sum(x, axis=0, keepdims=True)


def _sigmoid(x):
    return 1.0 / (1.0 + jnp.exp(-x))


def _rot_half(v):
    lane = lax.broadcasted_iota(jnp.int32, v.shape, 1)
    lo = -pltpu.roll(v, 96, 1)
    hi = pltpu.roll(v, 32, 1)
    return jnp.where(lane < 32, lo, jnp.where(lane < 64, hi, 0.0))


def rms_fwd(h, g, *, tr, name, deps=()):
    def body(step, ri, fi, ro, ao):
        x = ri[0][...]
        r = lax.rsqrt(jnp.mean(x * x, axis=-1, keepdims=True) + EPS)
        ro[0][...] = ((x * r) * fi[0][...]).astype(BF)

    return rowwise(body, [h], [g], [(h.shape[1], BF)], [], tr=tr, name=name, deps=deps)[0]


def rms_bwd(dy, h, g, dres, *, tr, name, deps=()):
    D = h.shape[1]

    def body(step, ri, fi, ro, ao):
        d, x, dr = ri[0][...].astype(F32), ri[1][...], ri[2][...]
        r = lax.rsqrt(jnp.mean(x * x, axis=-1, keepdims=True) + EPS)
        n = x * r
        _acc(step, ao[0], _colsum(d * n))
        dn = d * fi[0][...]
        dh = r * (dn - n * jnp.mean(dn * n, axis=-1, keepdims=True)) + dr
        ro[0][...] = dh
        ro[1][...] = dh.astype(BF)

    return rowwise(body, [dy, h, dres], [g], [(D, F32), (D, BF)], [(1, D)], tr=tr, name=name, deps=deps)


def glu_fwd(z, b, *, tr, name):
    C = z.shape[1] // 2

    def body(step, ri, fi, ro, ao):
        zz = ri[0][...].astype(F32) + fi[0][...]
        ro[0][...] = zz[:, :C] * _sigmoid(zz[:, C:])

    return rowwise(body, [z], [b], [(C, F32)], [], tr=tr, name=name)[0]


def glu_bwd(z, b, dc, *, tr, name):
    C = z.shape[1] // 2

    def body(step, ri, fi, ro, ao):
        zz = ri[0][...].astype(F32) + fi[0][...]
        d = ri[1][...]
        za, sg = zz[:, :C], _sigmoid(zz[:, C:])
        dza = d * sg
        dzb = d * za * sg * (1.0 - sg)
        ro[0][:, :C] = dza.astype(BF)
        ro[0][:, C:] = dzb.astype(BF)
        _acc(step, ao[0], _colsum(dza))
        _acc(step, ao[1], _colsum(dzb))

    dz, da, db = rowwise(body, [z, dc], [b], [(2 * C, BF)], [(1, C), (1, C)], tr=tr, name=name)
    return dz, jnp.concatenate([da, db], axis=1)


def ln_silu_fwd(c1, g, b, *, tr, name):
    def body(step, ri, fi, ro, ao):
        x = ri[0][...]
        mu = jnp.mean(x, axis=-1, keepdims=True)
        xc = x - mu
        rstd = lax.rsqrt(jnp.mean(xc * xc, axis=-1, keepdims=True) + EPS)
        y = (xc * rstd) * fi[0][...] + fi[1][...]
        ro[0][...] = (y * _sigmoid(y)).astype(BF)

    return rowwise(body, [c1], [g, b], [(c1.shape[1], BF)], [], tr=tr, name=name)[0]


def ln_silu_bwd(c1, dc3, g, b, *, tr, name):
    C = c1.shape[1]

    def body(step, ri, fi, ro, ao):
        x, d3 = ri[0][...], ri[1][...].astype(F32)
        mu = jnp.mean(x, axis=-1, keepdims=True)
        xc = x - mu
        rstd = lax.rsqrt(jnp.mean(xc * xc, axis=-1, keepdims=True) + EPS)
        n = xc * rstd
        y = n * fi[0][...] + fi[1][...]
        sg = _sigmoid(y)
        dy = d3 * (sg * (1.0 + y * (1.0 - sg)))
        _acc(step, ao[0], _colsum(dy * n))
        _acc(step, ao[1], _colsum(dy))
        dn = dy * fi[0][...]
        dx = rstd * (dn - jnp.mean(dn, axis=-1, keepdims=True) - n * jnp.mean(dn * n, axis=-1, keepdims=True))
        ro[0][...] = dx
        _acc(step, ao[2], _colsum(dx))

    return rowwise(body, [c1, dc3], [g, b], [(C, F32)], [(1, C), (1, C), (1, C)], tr=tr, name=name)


def lora_norm_fwd(zs, gq, gkv, cos, sin, *, tr, name, deps=()):
    R = gq.shape[1]

    def body(step, ri, fi, ro, ao):
        z = ri[0][...]
        for o, gi in ((0, 0), (1, 1)):
            x = z[:, o * R:(o + 1) * R]
            r = lax.rsqrt(jnp.mean(x * x, axis=-1, keepdims=True) + EPS)
            ro[o][...] = ((x * r) * fi[gi][...]).astype(BF)
        kr = z[:, 2 * R:2 * R + 128]
        ro[2][...] = kr * ri[1][...] + _rot_half(kr) * ri[2][...]

    return rowwise(body, [zs, cos, sin], [gq, gkv], [(R, BF), (R, BF), (128, F32)], [], tr=tr, name=name, deps=deps)


def lora_norm_bwd(zs, dcq, dckv, dkr, cos, sin, gq, gkv, *, tr, name):
    R = gq.shape[1]

    def body(step, ri, fi, ro, ao):
        z = ri[0][...]
        for o in (0, 1):
            x = z[:, o * R:(o + 1) * R]
            d = ri[1 + o][...]
            r = lax.rsqrt(jnp.mean(x * x, axis=-1, keepdims=True) + EPS)
            n = x * r
            _acc(step, ao[o], _colsum(d * n))
            dn = d * fi[o][...]
            ro[0][:, o * R:(o + 1) * R] = (r * (dn - n * jnp.mean(dn * n, axis=-1, keepdims=True))).astype(BF)
        dk = ri[3][...]
        ro[0][:, 2 * R:2 * R + 128] = (dk * ri[4][...] - _rot_half(dk * ri[5][...])).astype(BF)

    return rowwise(body, [zs, dcq, dckv, dkr, cos, sin], [gq, gkv], [(2 * R + 128, BF)], [(1, R), (1, R)],
                   tr=tr, name=name)


def mix_fwd(zg, yc, ya, bg, bco, *, tr, name):
    D = yc.shape[1]

    def body(step, ri, fi, ro, ao):
        g = _sigmoid(ri[0][...].astype(F32) + fi[0][...])
        ro[0][...] = (g[:, :D] * (ri[1][...].astype(F32) + fi[1][...]) + g[:, D:] * ri[2][...].astype(F32)).astype(BF)

    return rowwise(body, [zg, yc, ya], [bg, bco], [(D, BF)], [], tr=tr, name=name)[0]


def mix_bwd(zg, yc, ya, dmix, bg, bco, *, tr, name):
    D = yc.shape[1]

    def body(step, ri, fi, ro, ao):
        g = _sigmoid(ri[0][...].astype(F32) + fi[0][...])
        gc, ga = g[:, :D], g[:, D:]
        ycv = ri[1][...].astype(F32) + fi[1][...]
        yav = ri[2][...].astype(F32)
        dm = ri[3][...].astype(F32)
        dyc = dm * gc
        ro[0][...] = dyc.astype(BF)
        ro[1][...] = (dm * ga).astype(BF)
        dzc = dm * ycv * gc * (1.0 - gc)
        dza = dm * yav * ga * (1.0 - ga)
        ro[2][:, :D] = dzc.astype(BF)
        ro[2][:, D:] = dza.astype(BF)
        _acc(step, ao[0], _colsum(dzc))
        _acc(step, ao[1], _colsum(dza))
        _acc(step, ao[2], _colsum(dyc))

    dyc, dya, dzg, dbc, dba, dbo = rowwise(body, [zg, yc, ya, dmix], [bg, bco], [(D, BF), (D, BF), (2 * D, BF)],
                                           [(1, D), (1, D), (1, D)], tr=tr, name=name)
    return dyc, dya, dzg, jnp.concatenate([dbc, dba], axis=1), dbo


def ffn_fwd(hn, wg, wu, *, tm, name):
    T, D = hn.shape
    nb, _, FB = wg.shape
    assert T % tm == 0

    def body(h_ref, g_ref, u_ref, a_ref, b_ref, f_ref):
        h = h_ref[...]
        a = jnp.dot(h, g_ref[...], preferred_element_type=F32)
        b = jnp.dot(h, u_ref[...], preferred_element_type=F32)
        a_ref[...] = a
        b_ref[...] = b
        f_ref[...] = (a * _sigmoid(a) * b).astype(BF)

    wspec = pl.BlockSpec((None, D, FB), lambda j, i: (j, 0, 0))
    ospec = pl.BlockSpec((tm, FB), lambda j, i: (i, j))
    return pl.pallas_call(
        body, name=name, grid=(nb, T // tm),
        in_specs=[pl.BlockSpec((tm, D), lambda j, i: (i, 0)), wspec, wspec], out_specs=[ospec, ospec, ospec],
        out_shape=[jax.ShapeDtypeStruct((T, nb * FB), F32), jax.ShapeDtypeStruct((T, nb * FB), F32),
                   jax.ShapeDtypeStruct((T, nb * FB), BF)],
        compiler_params=_cparams(("parallel", "parallel")),
    )(hn, wg, wu)


def ffn_dact(dh, wd, a, b, *, tm, name):
    T, D = dh.shape
    FF = wd.shape[0]
    FB = FF // N_CHIPS

    def body(dh_ref, w_ref, a_ref, b_ref, da_ref, db_ref):
        d = lax.dot_general(dh_ref[...], w_ref[...], (((1,), (1,)), ((), ())), preferred_element_type=F32)
        x, u = a_ref[...], b_ref[...]
        sg = _sigmoid(x)
        da_ref[...] = (d * u * (sg * (1.0 + x * (1.0 - sg)))).astype(BF)
        db_ref[...] = (d * (x * sg)).astype(BF)

    blk = pl.BlockSpec((tm, FB), lambda j, i: (i, j))
    return pl.pallas_call(
        body, name=name, grid=(N_CHIPS, T // tm),
        in_specs=[pl.BlockSpec((tm, D), lambda j, i: (i, 0)), pl.BlockSpec((FB, D), lambda j, i: (j, 0)), blk, blk],
        out_specs=[blk, blk],
        out_shape=[jax.ShapeDtypeStruct((T, FF), BF), jax.ShapeDtypeStruct((T, FF), BF)],
        compiler_params=_cparams(("parallel", "parallel")),
    )(dh, wd, a, b)


def ffn_dhn(da, db, wg, wu, *, tm, tn, name, deps=()):
    T, FF = da.shape
    nb, D, FB = wg.shape
    dims = (((1,), (1,)), ((), ()))

    def body(da_ref, db_ref, g_ref, u_ref, *rest):
        o_ref = rest[len(deps)]
        acc = None
        for k in range(nb):
            cols = slice(k * FB, (k + 1) * FB)
            p = lax.dot_general(da_ref[:, cols], g_ref[k], dims, preferred_element_type=F32)
            p = p + lax.dot_general(db_ref[:, cols], u_ref[k], dims, preferred_element_type=F32)
            acc = p if acc is None else acc + p
        o_ref[...] = acc.astype(BF)

    aspec = pl.BlockSpec((tm, FF), lambda i, j: (i, 0))
    wspec = pl.BlockSpec((nb, tn, FB), lambda i, j: (0, j, 0))
    return pl.pallas_call(
        body, name=name, grid=(T // tm, D // tn),
        in_specs=[aspec, aspec, wspec, wspec] + [pl.BlockSpec(memory_space=pl.ANY)] * len(deps),
        out_specs=pl.BlockSpec((tm, tn), lambda i, j: (i, j)),
        out_shape=jax.ShapeDtypeStruct((T, D), BF),
        compiler_params=_cparams(("parallel", "parallel")),
    )(da, db, wg, wu, *deps)


def final_loss(h2, tgt, g, *, seq, tr, name):
    D = h2.shape[1]

    def body(step, ri, fi, ro, ao):
        x = ri[0][...]
        r = lax.rsqrt(jnp.mean(x * x, axis=-1, keepdims=True) + EPS)
        n = x * r
        gg = fi[0][...]
        row = lax.broadcasted_iota(jnp.int32, (tr, 1), 0) + step * tr
        live = jnp.logical_and(row >= N_META, row < N_META + seq)
        e = jnp.where(live, n * gg - ri[1][...], 0.0)
        _acc(step, ao[0], jnp.broadcast_to(0.5 * jnp.sum(jnp.mean(e * e, axis=-1, keepdims=True)), (1, 128)))
        dy = e * (1.0 / D)
        _acc(step, ao[1], _colsum(dy * n))
        dn = dy * gg
        dh = r * (dn - n * jnp.mean(dn * n, axis=-1, keepdims=True))
        ro[0][...] = dh
        ro[1][...] = dh.astype(BF)

    return rowwise(body, [h2, tgt], [g], [(D, F32), (D, BF)], [(1, 128), (1, D)], tr=tr, name=name)


def _shifted_copies(s_ref, T, rc):
    for r in range(1, 8):
        for t0 in range(0, T, rc):
            s_ref[r, t0:t0 + rc, :] = s_ref[0, t0 + r:t0 + r + rc, :]
        s_ref[r, T:T + CONV_PAD - 8, :] = s_ref[0, T + r:T + r + CONV_PAD - 8, :]


def _shifted(s_ref, start, rows):
    return s_ref[start % 8, start - start % 8:start - start % 8 + rows, :]


def dwconv_fwd(x, w, b, *, tc, rc, name, deps=()):
    T, C = x.shape
    nchunk = T // rc
    assert nchunk * rc == T and C % tc == 0

    def body(x_ref, w_ref, b_ref, *rest):
        y_ref, xs_ref = rest[len(deps):]
        xs_ref[0, 0:CONV_PAD, :] = jnp.zeros((CONV_PAD, tc), F32)
        xs_ref[0, CONV_PAD:CONV_PAD + T, :] = x_ref[...]
        _shifted_copies(xs_ref, T, rc)
        off = CONV_PAD - (CONV_WIDTH - 1)
        for ci in range(nchunk):
            t0 = ci * rc
            acc = jnp.broadcast_to(b_ref[...], (rc, tc))
            for j in range(CONV_WIDTH):
                acc = acc + w_ref[j:j + 1, :] * _shifted(xs_ref, t0 + off + j, rc)
            y_ref[t0:t0 + rc, :] = acc

    return pl.pallas_call(
        body, name=name, grid=(C // tc,),
        in_specs=[pl.BlockSpec((T, tc), lambda j: (0, j)), pl.BlockSpec((32, tc), lambda j: (0, j)),
                  pl.BlockSpec((1, tc), lambda j: (0, j))] + [pl.BlockSpec(memory_space=pl.ANY)] * len(deps),
        out_specs=pl.BlockSpec((T, tc), lambda j: (0, j)),
        out_shape=jax.ShapeDtypeStruct((T, C), F32),
        scratch_shapes=[pltpu.VMEM((8, T + CONV_PAD, tc), F32)],
        compiler_params=_cparams(("parallel",)),
    )(x, w, b, *deps)


def dwconv_bwd(x, dy, w, *, tc, rc, name):
    T, C = x.shape
    nchunk = T // rc
    assert nchunk * rc == T and C % tc == 0 and rc % 8 == 0

    def body(x_ref, dy_ref, w_ref, dx_ref, dw_ref, xs_ref, ds_ref):
        off = CONV_PAD - (CONV_WIDTH - 1)
        xs_ref[0, 0:CONV_PAD, :] = jnp.zeros((CONV_PAD, tc), F32)
        xs_ref[0, CONV_PAD:CONV_PAD + T, :] = x_ref[...]
        _shifted_copies(xs_ref, T, rc)
        ds_ref[0, 0:T, :] = dy_ref[...]
        ds_ref[0, T:T + CONV_PAD, :] = jnp.zeros((CONV_PAD, tc), F32)
        _shifted_copies(ds_ref, T, rc)
        for ci in range(nchunk):
            t0 = ci * rc
            acc = jnp.zeros((rc, tc), F32)
            for j in range(CONV_WIDTH):
                acc = acc + w_ref[j:j + 1, :] * _shifted(ds_ref, t0 + (CONV_WIDTH - 1) - j, rc)
            dx_ref[t0:t0 + rc, :] = acc
        dw_ref[...] = jnp.zeros((32, tc), F32)
        for j in range(CONV_WIDTH):
            acc = jnp.zeros((8, tc), F32)
            for ci in range(nchunk):
                t0 = ci * rc
                pr = dy_ref[t0:t0 + rc, :] * _shifted(xs_ref, t0 + off + j, rc)
                acc = acc + jnp.sum(pr.reshape(rc // 8, 8, tc), axis=0)
            dw_ref[j:j + 1, :] = jnp.sum(acc, axis=0, keepdims=True)

    return pl.pallas_call(
        body, name=name, grid=(C // tc,),
        in_specs=[pl.BlockSpec((T, tc), lambda j: (0, j)), pl.BlockSpec((T, tc), lambda j: (0, j)),
                  pl.BlockSpec((32, tc), lambda j: (0, j))],
        out_specs=[pl.BlockSpec((T, tc), lambda j: (0, j)), pl.BlockSpec((32, tc), lambda j: (0, j))],
        out_shape=[jax.ShapeDtypeStruct((T, C), F32), jax.ShapeDtypeStruct((32, C), F32)],
        scratch_shapes=[pltpu.VMEM((8, T + CONV_PAD, tc), F32), pltpu.VMEM((8, T + CONV_PAD, tc), F32)],
        compiler_params=_cparams(("parallel",)),
    )(x, dy, w)


def q_proj(cq, wq, cos, sin, *, scale, name):
    T, R = cq.shape
    H = wq.shape[1] // HEAD_PAD

    def body(a_ref, b_ref, c_ref, s_ref, o_ref):
        p = jnp.dot(a_ref[...], b_ref[...], preferred_element_type=F32)
        o_ref[:, :QK_NOPE] = (p[:, :QK_NOPE] * scale).astype(BF)
        x = p[:, QK_NOPE:]
        o_ref[:, QK_NOPE:] = ((x * c_ref[...] + _rot_half(x) * s_ref[...]) * scale).astype(BF)

    return pl.pallas_call(
        body, name=name, grid=(H,),
        in_specs=[pl.BlockSpec((T, R), lambda h: (0, 0)), pl.BlockSpec((R, HEAD_PAD), lambda h: (0, h)),
                  pl.BlockSpec((T, 128), lambda h: (0, 0)), pl.BlockSpec((T, 128), lambda h: (0, 0))],
        out_specs=pl.BlockSpec((T, HEAD_PAD), lambda h: (0, h)),
        out_shape=jax.ShapeDtypeStruct((T, H * HEAD_PAD), BF),
        compiler_params=_cparams(("parallel",)),
    )(cq, wq, cos, sin)


def k_proj(ckv, wk, kr, *, name):
    T, R = ckv.shape
    H = wk.shape[1] // QK_NOPE

    def body(a_ref, b_ref, kr_ref, o_ref):
        p = jnp.dot(a_ref[...], b_ref[...], preferred_element_type=F32)
        o_ref[:, :QK_NOPE] = p.astype(BF)
        o_ref[:, QK_NOPE:] = kr_ref[...].astype(BF)

    return pl.pallas_call(
        body, name=name, grid=(H,),
        in_specs=[pl.BlockSpec((T, R), lambda h: (0, 0)), pl.BlockSpec((R, QK_NOPE), lambda h: (0, h)),
                  pl.BlockSpec((T, 128), lambda h: (0, 0))],
        out_specs=pl.BlockSpec((T, HEAD_PAD), lambda h: (0, h)),
        out_shape=jax.ShapeDtypeStruct((T, H * HEAD_PAD), BF),
        compiler_params=_cparams(("parallel",)),
    )(ckv, wk, kr)


def _causal_scores(q, k_ref, qi, CH, split):
    nt = (((1,), (1,)), ((), ()))
    pieces = []
    if not split:
        rows = slice(0, (qi + 1) * CH)
        s = lax.dot_general(q, k_ref[rows, :], nt, preferred_element_type=F32)
        keep = lax.broadcasted_iota(jnp.int32, s.shape, 1) <= lax.broadcasted_iota(jnp.int32, s.shape, 0) + qi * CH
        return [(rows, jnp.where(keep, s, -1e30))]
    if qi > 0:
        rows = slice(0, qi * CH)
        pieces.append((rows, lax.dot_general(q, k_ref[rows, :], nt, preferred_element_type=F32)))
    rows = slice(qi * CH, (qi + 1) * CH)
    s = lax.dot_general(q, k_ref[rows, :], nt, preferred_element_type=F32)
    keep = lax.broadcasted_iota(jnp.int32, s.shape, 1) <= lax.broadcasted_iota(jnp.int32, s.shape, 0)
    pieces.append((rows, jnp.where(keep, s, -1e30)))
    return pieces


def attn_fwd(q, k, v, *, nch, name):
    T = q.shape[0]
    H = q.shape[1] // HEAD_PAD
    CH = T // nch
    assert CH * nch == T and CH % 16 == 0

    def body(q_ref, k_ref, v_ref, o_ref, lse_ref):
        for qi in range(nch):
            L = (qi + 1) * CH
            pieces = _causal_scores(q_ref[qi * CH:L, :], k_ref, qi, CH, True)
            m = functools.reduce(jnp.maximum, [jnp.max(s, axis=1, keepdims=True) for _, s in pieces])
            l, o = 0.0, 0.0
            for rows, s in pieces:
                p = jnp.exp(s - m)
                l = l + jnp.sum(p, axis=1, keepdims=True)
                o = o + jnp.dot(p.astype(BF), v_ref[rows, :], preferred_element_type=F32)
            o_ref[qi * CH:L, :] = (o / l).astype(BF)
            lse_ref[qi * CH:L, :] = jnp.broadcast_to(m + jnp.log(l), (CH, V_HEAD))

    return pl.pallas_call(
        body, name=name, grid=(H,),
        in_specs=[pl.BlockSpec((T, HEAD_PAD), lambda h: (0, h)), pl.BlockSpec((T, HEAD_PAD), lambda h: (0, h)),
                  pl.BlockSpec((T, V_HEAD), lambda h: (0, h))],
        out_specs=[pl.BlockSpec((T, V_HEAD), lambda h: (0, h)), pl.BlockSpec((T, V_HEAD), lambda h: (0, h))],
        out_shape=[jax.ShapeDtypeStruct((T, H * V_HEAD), BF), jax.ShapeDtypeStruct((T, H * V_HEAD), F32)],
        compiler_params=_cparams(("parallel",)),
    )(q, k, v)


def attn_bwd(q, k, v, o, do, lse, cos, sin, *, scale, nch, name):
    T = q.shape[0]
    H = q.shape[1] // HEAD_PAD
    CH = T // nch
    tn_dims = (((0,), (0,)), ((), ()))
    nt_dims = (((1,), (1,)), ((), ()))

    def body(q_ref, k_ref, v_ref, o_ref, do_ref, lse_ref, c_ref, s_ref,
             dq_ref, dk_ref, dv_ref, dkr_ref, dk_acc, dv_acc):
        h = pl.program_id(0)
        dk_acc[...] = jnp.zeros_like(dk_acc)
        dv_acc[...] = jnp.zeros_like(dv_acc)
        for qi in range(nch):
            L = (qi + 1) * CH
            rows = slice(qi * CH, L)
            qc, doc = q_ref[rows, :], do_ref[rows, :]
            lse_c = lse_ref[rows, 0:1]
            dsum = jnp.sum(doc.astype(F32) * o_ref[rows, :].astype(F32), axis=1, keepdims=True)
            dq = 0.0
            for krows, s in _causal_scores(qc, k_ref, qi, CH, False):
                p = jnp.exp(s - lse_c)
                dp = lax.dot_general(doc, v_ref[krows, :], nt_dims, preferred_element_type=F32)
                ds = (p * (dp - dsum)).astype(BF)
                dq = dq + jnp.dot(ds, k_ref[krows, :], preferred_element_type=F32)
                dk_acc[krows, :] += lax.dot_general(ds, qc, tn_dims, preferred_element_type=F32)
                dv_acc[krows, :] += lax.dot_general(p.astype(BF), doc, tn_dims, preferred_element_type=F32)
            dq_ref[rows, :QK_NOPE] = (dq[:, :QK_NOPE] * scale).astype(BF)
            dyr = dq[:, QK_NOPE:]
            dq_ref[rows, QK_NOPE:] = ((dyr * c_ref[rows, :] - _rot_half(dyr * s_ref[rows, :])) * scale).astype(BF)
        dk_ref[:, :QK_NOPE] = dk_acc[:, :QK_NOPE].astype(BF)
        dk_ref[:, QK_NOPE:] = jnp.zeros((T, HEAD_PAD - QK_NOPE), BF)
        dv_ref[...] = dv_acc[...].astype(BF)

        @pl.when(h == 0)
        def _():
            dkr_ref[...] = dk_acc[:, QK_NOPE:]

        @pl.when(h > 0)
        def _():
            dkr_ref[...] += dk_acc[:, QK_NOPE:]

    hp = pl.BlockSpec((T, HEAD_PAD), lambda h: (0, h))
    hv = pl.BlockSpec((T, V_HEAD), lambda h: (0, h))
    tab = pl.BlockSpec((T, 128), lambda h: (0, 0))
    return pl.pallas_call(
        body, name=name, grid=(H,),
        in_specs=[hp, hp, hv, hv, hv, hv, tab, tab],
        out_specs=[hp, hp, hv, tab],
        out_shape=[jax.ShapeDtypeStruct((T, H * HEAD_PAD), BF), jax.ShapeDtypeStruct((T, H * HEAD_PAD), BF),
                   jax.ShapeDtypeStruct((T, H * V_HEAD), BF), jax.ShapeDtypeStruct((T, 128), F32)],
        scratch_shapes=[pltpu.VMEM((T, HEAD_PAD), F32), pltpu.VMEM((T, V_HEAD), F32)],
        compiler_params=_cparams(("arbitrary",)),
    )(q, k, v, o, do, lse, cos, sin)


def _place():
    x, y, c = lax.axis_index("x"), lax.axis_index("y"), lax.axis_index("c")
    return x, y, c


def _other_chips(x, y):
    return [(1 - x, y), (x, 1 - y), (1 - x, 1 - y)]


def gather_devices(v, *, name):
    R, C = v.shape

    def body(v_ref, o_ref, send_sems, recv_sems):
        x, y, c = _place()
        me = 4 * x + 2 * y + c
        o_ref[me] = v_ref[...]
        copies = []
        for k in range(1, N_DEV):
            fx, fy, fc = (k >> 2) & 1, (k >> 1) & 1, k & 1
            to = (x ^ fx, y ^ fy, c ^ fc)
            cp = pltpu.make_async_remote_copy(src_ref=v_ref, dst_ref=o_ref.at[me], send_sem=send_sems.at[k - 1],
                                              recv_sem=recv_sems.at[k - 1], device_id=to, device_id_type=MESH)
            cp.start()
            copies.append(cp)
        for k in range(1, N_DEV):
            fx, fy, fc = (k >> 2) & 1, (k >> 1) & 1, k & 1
            frm = 4 * (x ^ fx) + 2 * (y ^ fy) + (c ^ fc)
            pltpu.make_async_remote_copy(src_ref=v_ref, dst_ref=o_ref.at[frm], send_sem=send_sems.at[k - 1],
                                         recv_sem=recv_sems.at[k - 1], device_id=(x, y, c),
                                         device_id_type=MESH).wait_recv()
        for cp in copies:
            cp.wait_send()

    return pl.pallas_call(
        body, name=name, out_shape=jax.ShapeDtypeStruct((N_DEV, R, C), F32),
        in_specs=[pl.BlockSpec(memory_space=pltpu.VMEM)], out_specs=pl.BlockSpec(memory_space=pltpu.VMEM),
        scratch_shapes=[pltpu.SemaphoreType.DMA((N_DEV - 1,)), pltpu.SemaphoreType.DMA((N_DEV - 1,))],
    )(v)


HBM_SPEC = pl.BlockSpec(memory_space=pltpu.HBM)
SEM_SPEC = pl.BlockSpec(memory_space=pltpu.SEMAPHORE)
DATAFLOW = pltpu.SideEffectType.DATAFLOW_SIDE_EFFECTING


def _hbm(a):
    return pltpu.with_memory_space_constraint(a, pltpu.HBM)


def _half(ref, lead, c, axis):
    rows, cols = ref.shape[-2], ref.shape[-1]
    if axis == 0:
        return ref.at[(*lead, pl.ds(c * (rows // 2), rows // 2))]
    return ref.at[(*lead, slice(None), pl.ds(c * (cols // 2), cols // 2))]


def copies_start(arrs, plan, n_sems, after, *, name):
    n = len(arrs)

    def body(*refs):
        outs = refs[n + 1:2 * n + 1]
        send_sems, recv_sems, token = refs[2 * n + 1:]
        for cp in plan(outs, send_sems, recv_sems, False):
            cp.start()
        token[...] = jnp.zeros_like(token)

    res = pl.pallas_call(
        body, name=name,
        out_shape=tuple(pltpu.HBM(a.shape, a.dtype) for a in arrs)
        + (pltpu.SemaphoreType.DMA((n_sems,)), pltpu.SemaphoreType.DMA((n_sems,)), jax.ShapeDtypeStruct((8, 128), F32)),
        in_specs=(HBM_SPEC,) * n + (pl.BlockSpec(memory_space=pl.ANY),),
        out_specs=(HBM_SPEC,) * n + (SEM_SPEC, SEM_SPEC, pl.BlockSpec(memory_space=pltpu.VMEM)),
        input_output_aliases={i: i for i in range(n)},
        compiler_params=pltpu.CompilerParams(has_side_effects=DATAFLOW),
    )(*[_hbm(a) for a in arrs], after)
    return list(res[:n]), res[n], res[n + 1], res[n + 2]


def copies_wait(arrs, plan, send_sems, recv_sems, after, *, name):
    n = len(arrs)
    afters = list(after) if isinstance(after, (list, tuple)) else [after]

    def body(*refs):
        for cp in plan(refs[:n], refs[n], refs[n + 1], True):
            cp.wait_send()
            cp.wait_recv()

    res = pl.pallas_call(
        body, name=name,
        out_shape=tuple(pltpu.HBM(a.shape, a.dtype) for a in arrs),
        in_specs=(HBM_SPEC,) * n + (SEM_SPEC, SEM_SPEC) + (pl.BlockSpec(memory_space=pl.ANY),) * len(afters),
        out_specs=(HBM_SPEC,) * n,
        input_output_aliases={i: i for i in range(n)},
        compiler_params=pltpu.CompilerParams(has_side_effects=DATAFLOW),
    )(*arrs, send_sems, recv_sems, *afters)
    return list(res)


def gather_plan(axes):
    def plan(bufs, send_sems, recv_sems, receiving):
        x, y, c = _place()
        me = 2 * x + y
        out = []
        for i, b in enumerate(bufs):
            for j, (cx, cy) in enumerate(_other_chips(x, y)):
                src = _half(b, (me,), c, axes[i])
                dst = _half(b, (2 * cx + cy,), c, axes[i]) if receiving else src
                out.append(pltpu.make_async_remote_copy(
                    src_ref=src, dst_ref=dst, send_sem=send_sems.at[3 * i + j], recv_sem=recv_sems.at[3 * i + j],
                    device_id=(cx, cy, c), device_id_type=MESH))
        return out
    return plan


def _half_part(ref, lead, c, axis, part):
    rows, cols = ref.shape[-2], ref.shape[-1]
    span = rows // 2 if axis == 0 else rows
    p1 = -(-(span // 2) // 16) * 16
    lo, n = (0, p1) if part == 0 else (p1, span - p1)
    if axis == 0:
        return ref.at[(*lead, pl.ds(c * span + lo, n))]
    return ref.at[(*lead, pl.ds(lo, n), pl.ds(c * (cols // 2), cols // 2))]


def near_plan(axes):
    def plan(bufs, send_sems, recv_sems, receiving):
        x, y, c = _place()
        me = 2 * x + y
        out = []
        for i, b in enumerate(bufs):
            for j, (cx, cy) in enumerate(_other_chips(x, y)[:2]):
                src = _half(b, (me,), c, axes[i])
                dst = _half(b, (2 * cx + cy,), c, axes[i]) if receiving else src
                out.append(pltpu.make_async_remote_copy(
                    src_ref=src, dst_ref=dst, send_sem=send_sems.at[2 * i + j], recv_sem=recv_sems.at[2 * i + j],
                    device_id=(cx, cy, c), device_id_type=MESH))
        return out
    return plan


def relay_plan(axes):
    def plan(bufs, send_sems, recv_sems, receiving):
        x, y, c = _place()
        (xn, yn, dg) = [2 * cx + cy for cx, cy in _other_chips(x, y)]
        to = [(x, 1 - y, c), (1 - x, y, c), (x, y, 1 - c), (x, y, 1 - c)]
        out = []
        for i, b in enumerate(bufs):
            if receiving:
                blks = [_half_part(b, (dg,), c, axes[i], 0), _half_part(b, (dg,), c, axes[i], 1),
                        _half(b, (xn,), 1 - c, axes[i]), _half(b, (yn,), 1 - c, axes[i])]
            else:
                blks = [_half_part(b, (xn,), c, axes[i], 0), _half_part(b, (yn,), c, axes[i], 1),
                        _half(b, (xn,), c, axes[i]), _half(b, (yn,), c, axes[i])]
            for k in range(4):
                out.append(pltpu.make_async_remote_copy(
                    src_ref=blks[k], dst_ref=blks[k], send_sem=send_sems.at[4 * i + k],
                    recv_sem=recv_sems.at[4 * i + k], device_id=to[k], device_id_type=MESH))
        return out
    return plan


def far_forward_plan(axes):
    def plan(bufs, send_sems, recv_sems, receiving):
        x, y, c = _place()
        dg = 2 * (1 - x) + (1 - y)
        out = []
        for i, b in enumerate(bufs):
            blk = _half(b, (dg,), (1 - c) if receiving else c, axes[i])
            out.append(pltpu.make_async_remote_copy(
                src_ref=blk, dst_ref=blk, send_sem=send_sems.at[i], recv_sem=recv_sems.at[i],
                device_id=(x, y, 1 - c), device_id_type=MESH))
        return out
    return plan


def forward_plan(axes):
    def plan(bufs, send_sems, recv_sems, receiving):
        x, y, c = _place()
        out = []
        for i, b in enumerate(bufs):
            for j, (cx, cy) in enumerate(_other_chips(x, y)):
                blk = _half(b, (2 * cx + cy,), (1 - c) if receiving else c, axes[i])
                out.append(pltpu.make_async_remote_copy(
                    src_ref=blk, dst_ref=blk, send_sem=send_sems.at[3 * i + j], recv_sem=recv_sems.at[3 * i + j],
                    device_id=(x, y, 1 - c), device_id_type=MESH))
        return out
    return plan


def scatter_plan(n):
    def plan(arrs, send_sems, recv_sems, receiving):
        x, y, c = _place()
        out = []
        for i in range(n):
            for j, (cx, cy) in enumerate(_other_chips(x, y)):
                out.append(pltpu.make_async_remote_copy(
                    src_ref=arrs[i].at[2 * cx + cy], dst_ref=arrs[n + i].at[j], send_sem=send_sems.at[3 * i + j],
                    recv_sem=recv_sems.at[3 * i + j], device_id=(cx, cy, c), device_id_type=MESH))
        return out
    return plan


def exchange_plan(axes):
    n = len(axes)

    def plan(arrs, send_sems, recv_sems, receiving):
        x, y, c = _place()
        out = []
        for i in range(n):
            for k in range(N_CHIPS):
                out.append(pltpu.make_async_remote_copy(
                    src_ref=_half(arrs[i], (k,), 1 - c, axes[i]), dst_ref=arrs[n + i].at[k],
                    send_sem=send_sems.at[N_CHIPS * i + k], recv_sem=recv_sems.at[N_CHIPS * i + k],
                    device_id=(x, y, 1 - c), device_id_type=MESH))
        return out
    return plan


def devices_plan(arrs, send_sems, recv_sems, receiving):
    x, y, c = _place()
    out = []
    for k in range(1, N_DEV):
        px, py, pc = x ^ ((k >> 2) & 1), y ^ ((k >> 1) & 1), c ^ (k & 1)
        slot = (4 * px + 2 * py + pc) if receiving else (4 * x + 2 * y + c)
        out.append(pltpu.make_async_remote_copy(
            src_ref=arrs[0], dst_ref=arrs[1].at[slot], send_sem=send_sems.at[k - 1], recv_sem=recv_sems.at[k - 1],
            device_id=(px, py, pc), device_id_type=MESH))
    return out


def share_halves(bufs, axes, after, *, name):
    n = len(bufs)

    def body(*refs):
        outs = refs[n + 1:2 * n + 1]
        send_sems, recv_sems = refs[2 * n + 1:]
        x, y, c = _place()
        sib = (x, y, 1 - c)
        cps = []
        for i in range(n):
            mine = _half(outs[i], (), c, axes[i])
            cp = pltpu.make_async_remote_copy(src_ref=mine, dst_ref=mine, send_sem=send_sems.at[i],
                                              recv_sem=recv_sems.at[i], device_id=sib, device_id_type=MESH)
            cp.start()
            cps.append(cp)
        for i in range(n):
            theirs = _half(outs[i], (), 1 - c, axes[i])
            pltpu.make_async_remote_copy(src_ref=theirs, dst_ref=theirs, send_sem=send_sems.at[i],
                                         recv_sem=recv_sems.at[i], device_id=sib, device_id_type=MESH).wait_recv()
        for cp in cps:
            cp.wait_send()

    anyspec = pl.BlockSpec(memory_space=pl.ANY)
    return pl.pallas_call(
        body, name=name,
        out_shape=[jax.ShapeDtypeStruct(b.shape, b.dtype) for b in bufs],
        in_specs=[anyspec] * (n + 1), out_specs=[anyspec] * n,
        input_output_aliases={i: i for i in range(n)},
        scratch_shapes=[pltpu.SemaphoreType.DMA((n,)), pltpu.SemaphoreType.DMA((n,))],
    )(*bufs, after)


def _row_tile(r, cols, itemsize, target_bytes=2 * 1024 * 1024):
    best = None
    for t in range(16, r + 1, 16):
        if r % t == 0 and t * cols * itemsize <= target_bytes:
            best = t
    return best if best is not None else r


def pair_sum(g, recv, half, axis, *, name):
    _, hr, hc = recv.shape
    tr = _row_tile(hr, hc, 4)
    nb = hr // tr
    if axis == 0:
        mine = pl.BlockSpec((None, tr, hc), lambda k, r, hf: (k, hf[0] * nb + r, 0))
    else:
        mine = pl.BlockSpec((None, tr, hc), lambda k, r, hf: (k, r, hf[0]))

    def body(half_ref, a_ref, b_ref, o_ref):
        o_ref[...] = (a_ref[...].astype(F32) + b_ref[...].astype(F32)).astype(BF)

    return pl.pallas_call(
        body, name=name,
        grid_spec=pltpu.PrefetchScalarGridSpec(
            num_scalar_prefetch=1, grid=(N_CHIPS, nb),
            in_specs=[mine, pl.BlockSpec((None, tr, hc), lambda k, r, hf: (k, r, 0))],
            out_specs=pl.BlockSpec((None, tr, hc), lambda k, r, hf: (k, r, 0))),
        out_shape=jax.ShapeDtypeStruct((N_CHIPS, hr, hc), BF),
        compiler_params=_cparams(("parallel", "parallel")),
    )(half, g, recv)


def chip_sum(s, recv, place, axis, *, name):
    _, hr, hc = s.shape
    tr = _row_tile(hr, hc, 4)
    nb = hr // tr
    if axis == 0:
        o_spec = pl.BlockSpec((tr, hc), lambda r, pc: (pc[1] * nb + r, 0))
        o_shape = (2 * hr, hc)
    else:
        o_spec = pl.BlockSpec((tr, hc), lambda r, pc: (r, pc[1]))
        o_shape = (hr, 2 * hc)

    def body(place_ref, a_ref, b_ref, o_ref):
        acc = a_ref[...].astype(F32)
        for j in range(3):
            acc = acc + b_ref[j].astype(F32)
        o_ref[...] = acc

    return pl.pallas_call(
        body, name=name,
        grid_spec=pltpu.PrefetchScalarGridSpec(
            num_scalar_prefetch=1, grid=(nb,),
            in_specs=[pl.BlockSpec((None, tr, hc), lambda r, pc: (pc[0], r, 0)),
                      pl.BlockSpec((3, tr, hc), lambda r, pc: (0, r, 0))],
            out_specs=o_spec),
        out_shape=jax.ShapeDtypeStruct(o_shape, F32),
        compiler_params=_cparams(("parallel",)),
    )(place, s, recv)


def sum_devices(v, *, name):
    _, R, C = v.shape

    def body(v_ref, o_ref):
        acc = v_ref[0]
        for k in range(1, N_DEV):
            acc = acc + v_ref[k]
        o_ref[...] = acc

    return pl.pallas_call(body, name=name, out_shape=jax.ShapeDtypeStruct((R, C), F32))(v)


def adamw(w, g, m, v, *, name):
    R, C = w.shape
    tr = _row_tile(R, C, 4, target_bytes=1024 * 1024)
    c1 = 1.0 / (1.0 - ADAM_B1 ** ADAM_STEP)
    c2 = 1.0 / (1.0 - ADAM_B2 ** ADAM_STEP)

    def body(w_ref, g_ref, m_ref, v_ref, go_ref, d_ref, nm_ref, nv_ref):
        gg = g_ref[...]
        go_ref[...] = gg
        nm = ADAM_B1 * m_ref[...] + (1.0 - ADAM_B1) * gg
        nv = ADAM_B2 * v_ref[...] + (1.0 - ADAM_B2) * (gg * gg)
        nm_ref[...] = nm
        nv_ref[...] = nv
        d_ref[...] = -ADAM_LR * ((nm * c1) / (jnp.sqrt(nv * c2) + ADAM_EPS) + ADAM_WD * w_ref[...])

    spec = pl.BlockSpec((tr, C), lambda i: (i, 0))
    return pl.pallas_call(
        body, name=name, grid=(R // tr,), in_specs=[spec] * 4, out_specs=[spec] * 4,
        out_shape=[jax.ShapeDtypeStruct((R, C), F32)] * 4, compiler_params=_cparams(("parallel",)),
    )(w, g, m, v)


def _rope_tables(T):
    pos = np.arange(T, dtype=np.float32)
    inv_freq = (ROPE_THETA ** (-np.arange(0, QK_ROPE, 2, dtype=np.float32) / QK_ROPE)).astype(np.float32)
    ang = pos[:, None] * inv_freq[None, :]
    z = np.zeros((T, 64), np.float32)
    cos = np.concatenate([np.cos(ang), np.cos(ang), z], axis=1).astype(np.float32)
    sin = np.concatenate([np.sin(ang), np.sin(ang), z], axis=1).astype(np.float32)
    return jnp.asarray(cos), jnp.asarray(sin)


def _pick(n, cands):
    for c in cands:
        if n % c == 0:
            return c
    return n


def kernel(x, meta_tokens, g_mix, w_in, b_glu, b_gate, w_dw, b_dw, g_conv_ln, b_conv_ln, w_conv_out, b_conv_out, g_q_lora, w_uq, g_kv_lora, w_uk, w_uv, w_attn_out, w_out, g_ffn, w_ffn_gate, w_ffn_up, w_ffn_down, g_final, loss_target, m_meta_tokens, m_g_mix, m_w_in, m_b_glu, m_b_gate, m_w_dw, m_b_dw, m_g_conv_ln, m_b_conv_ln, m_w_conv_out, m_b_conv_out, m_g_q_lora, m_w_uq, m_g_kv_lora, m_w_uk, m_w_uv, m_w_attn_out, m_w_out, m_g_ffn, m_w_ffn_gate, m_w_ffn_up, m_w_ffn_down, m_g_final, v_meta_tokens, v_g_mix, v_w_in, v_b_glu, v_b_gate, v_w_dw, v_b_dw, v_g_conv_ln, v_b_conv_ln, v_w_conv_out, v_b_conv_out, v_g_q_lora, v_w_uq, v_g_kv_lora, v_w_uk, v_w_uv, v_w_attn_out, v_w_out, v_g_ffn, v_w_ffn_gate, v_w_ffn_up, v_w_ffn_down, v_g_final):
    weights = dict(meta_tokens=meta_tokens, g_mix=g_mix, w_in=w_in, b_glu=b_glu, b_gate=b_gate, w_dw=w_dw, b_dw=b_dw,
                   g_conv_ln=g_conv_ln, b_conv_ln=b_conv_ln, w_conv_out=w_conv_out, b_conv_out=b_conv_out,
                   g_q_lora=g_q_lora, w_uq=w_uq, g_kv_lora=g_kv_lora, w_uk=w_uk, w_uv=w_uv, w_attn_out=w_attn_out,
                   w_out=w_out, g_ffn=g_ffn, w_ffn_gate=w_ffn_gate, w_ffn_up=w_ffn_up, w_ffn_down=w_ffn_down,
                   g_final=g_final)
    m_in = dict(meta_tokens=m_meta_tokens, g_mix=m_g_mix, w_in=m_w_in, b_glu=m_b_glu, b_gate=m_b_gate, w_dw=m_w_dw,
                b_dw=m_b_dw, g_conv_ln=m_g_conv_ln, b_conv_ln=m_b_conv_ln, w_conv_out=m_w_conv_out,
                b_conv_out=m_b_conv_out, g_q_lora=m_g_q_lora, w_uq=m_w_uq, g_kv_lora=m_g_kv_lora, w_uk=m_w_uk,
                w_uv=m_w_uv, w_attn_out=m_w_attn_out, w_out=m_w_out, g_ffn=m_g_ffn, w_ffn_gate=m_w_ffn_gate,
                w_ffn_up=m_w_ffn_up, w_ffn_down=m_w_ffn_down, g_final=m_g_final)
    v_in = dict(meta_tokens=v_meta_tokens, g_mix=v_g_mix, w_in=v_w_in, b_glu=v_b_glu, b_gate=v_b_gate, w_dw=v_w_dw,
                b_dw=v_b_dw, g_conv_ln=v_g_conv_ln, b_conv_ln=v_b_conv_ln, w_conv_out=v_w_conv_out,
                b_conv_out=v_b_conv_out, g_q_lora=v_g_q_lora, w_uq=v_w_uq, g_kv_lora=v_g_kv_lora, w_uk=v_w_uk,
                w_uv=v_w_uv, w_attn_out=v_w_attn_out, w_out=v_w_out, g_ffn=v_g_ffn, w_ffn_gate=v_w_ffn_gate,
                w_ffn_up=v_w_ffn_up, w_ffn_down=v_w_ffn_down, g_final=v_g_final)
    names = list(weights)

    seq, D = x.shape[1], x.shape[2]
    C = w_conv_out.shape[2]
    R = g_q_lora.shape[1]
    H = w_uv.shape[2] * N_CHIPS // V_HEAD
    FB = w_ffn_gate.shape[2]
    FF = FB * N_CHIPS
    length = N_META + seq
    T = -(-length // BLOCK_Q) * BLOCK_Q
    scale = (QK_NOPE + QK_ROPE) ** -0.5
    assert C == D and w_in.shape[2] * N_CHIPS == 2 * C + 2 * R + QK_ROPE + 2 * D

    xi, yi, ci = _place()
    chip = (2 * xi + yi).astype(jnp.int32)
    half_arr = ci.astype(jnp.int32).reshape(1)

    tr = _pick(T, (272, 256, 128))
    tmh = T // 2 if (T // 2) % 16 == 0 else T
    tmq = T // 4 if (T // 4) % 16 == 0 else T
    tmw = _pick(D, (1024, 512))
    nch = 4
    rc = _pick(T, (272, 256, 128))
    tcv = 128

    def tn_of(n, pref=512):
        return _pick(n, (pref, 384, 256, 128))

    grp_a = ["w_in"]
    grp_b = ["w_conv_out", "w_uq", "w_uk", "w_uv", "w_attn_out", "w_out"]
    grp_c = ["w_ffn_gate", "w_ffn_up", "w_ffn_down"]
    big = grp_a + grp_b + grp_c
    axes = {nm: 0 for nm in big}
    axes["w_in"] = 1
    shard = {nm: weights[nm][0] for nm in big}
    shard["w_in"] = jnp.swapaxes(w_in[0], 0, 1)

    def own_blocks(srcs):
        bufs = []
        for w in srcs:
            wb = w.astype(BF)
            bufs.append(lax.dynamic_update_slice(lax.empty((N_CHIPS,) + wb.shape, BF), wb[None], (chip, 0, 0)))
        return bufs

    def gather_begin(nms, bufs, after, tag):
        ax = [axes[nm] for nm in nms]
        bufs, ssem, rsem, tok = copies_start(bufs, gather_plan(ax), 3 * len(nms), after, name=f"gather_{tag}_start")
        return (nms, ax, bufs, ssem, rsem), tok

    def gather_middle(state, after, tag):
        nms, ax, bufs, ssem, rsem = state
        bufs = copies_wait(bufs, gather_plan(ax), ssem, rsem, after, name=f"gather_{tag}_wait")
        bufs, ssem, rsem, tok = copies_start(bufs, forward_plan(ax), 3 * len(nms), half_arr,
                                             name=f"gather_{tag}_forward_start")
        return (nms, ax, bufs, ssem, rsem), tok

    def gather_end(state, after, tag):
        nms, ax, bufs, ssem, rsem = state
        return dict(zip(nms, copies_wait(bufs, forward_plan(ax), ssem, rsem, after, name=f"gather_{tag}_forward_wait")))

    tiny = gather_devices(jnp.concatenate([meta_tokens, jnp.pad(w_dw[0], ((0, 1), (0, 0)))], axis=0),
                          name="gather_meta_w_dw")
    meta_full = jnp.concatenate([tiny[2 * k, :N_META] for k in range(N_CHIPS)], axis=1)
    w_dw_full = jnp.concatenate([tiny[2 * k, N_META:] for k in range(N_CHIPS)], axis=1)

    ax_a = [axes[nm] for nm in grp_a]
    bufs_a, ss_a, rs_a, tok_ga = copies_start(own_blocks([shard[nm] for nm in grp_a]), near_plan(ax_a),
                                              2 * len(grp_a), tiny, name="gather_a_near_start")
    tok_ga, held = lax.optimization_barrier((tok_ga, (x, loss_target, [shard[nm] for nm in grp_b + grp_c])))
    x_l, tgt_l, shard_bc = held
    h0 = jnp.concatenate([meta_full, x_l[0], jnp.zeros((T - length, D), F32)], axis=0)
    tgt = jnp.pad(tgt_l[0], ((N_META, T - length), (0, 0)))
    bufs_b, bufs_c = own_blocks(shard_bc[:len(grp_b)]), own_blocks(shard_bc[len(grp_b):])
    cos, sin = _rope_tables(T)

    bufs_a = copies_wait(bufs_a, near_plan(ax_a), ss_a, rs_a, [h0, tgt] + bufs_b + bufs_c, name="gather_a_near_wait")
    bufs_a, ss_a, rs_a, _ = copies_start(bufs_a, relay_plan(ax_a), 4 * len(grp_a), half_arr,
                                         name="gather_a_relay_start")
    bufs_a = copies_wait(bufs_a, relay_plan(ax_a), ss_a, rs_a, half_arr, name="gather_a_relay_wait")
    bufs_a, ss_a, rs_a, _ = copies_start(bufs_a, far_forward_plan(ax_a), len(grp_a), half_arr,
                                         name="gather_a_far_start")
    G = dict(zip(grp_a, copies_wait(bufs_a, far_forward_plan(ax_a), ss_a, rs_a, half_arr, name="gather_a_far_wait")))
    st_gb, tok_gb = gather_begin(grp_b, bufs_b, G["w_in"], "b")
    ax_c = [axes[nm] for nm in grp_c]
    bufs_c, ss_c, rs_c, tok_gc = copies_start(bufs_c, near_plan(ax_c), 2 * len(grp_c), tok_gb,
                                              name="gather_c_near_start")

    WT_in = G["w_in"].reshape(N_CHIPS * G["w_in"].shape[1], D)
    o1, o2 = 2 * C, 2 * C + 2 * R + QK_ROPE
    WT_small = jnp.pad(WT_in[o1:o2], ((0, 128 - QK_ROPE), (0, 0)))
    WT_gate = WT_in[o2:]

    u = rms_fwd(h0, g_mix, tr=tr, name="rms_mix", deps=[tok_gb, tok_gc])
    z_glu = mm(u, WT_in, mode="nt", tm=T, tn=tn_of(2 * C), tk=D, out_dtype=BF, name="mm_z_glu", n_out=o1)
    z_small = mm(u, WT_small, mode="nt", tm=T, tn=tn_of(2 * R + 128, 384), tk=D, out_dtype=F32, name="mm_z_small")
    z_gate = mm(u, WT_gate, mode="nt", tm=T, tn=tn_of(2 * D), tk=D, out_dtype=BF, name="mm_z_gate")

    c0 = glu_fwd(z_glu, b_glu, tr=tr, name="glu_fwd")
    c1 = dwconv_fwd(c0, w_dw_full, b_dw, tc=tcv, rc=rc, name="dwconv_fwd")
    st_gb, tok_gb = gather_middle(st_gb, c1, "b")
    c3 = ln_silu_fwd(c1, g_conv_ln, b_conv_ln, tr=tr, name="ln_silu_fwd")
    cq, ckv, kr = lora_norm_fwd(z_small, g_q_lora, g_kv_lora, cos, sin, tr=tr, name="lora_norm_fwd", deps=[tok_gb])

    def cols_full(a):
        return jnp.concatenate([a[k] for k in range(N_CHIPS)], axis=1)

    G.update(gather_end(st_gb, cq, "b"))
    W_co = G["w_conv_out"].reshape(C, D)
    W_uq = cols_full(G["w_uq"]).reshape(R, H, QK_NOPE + QK_ROPE)
    Wq_p = jnp.pad(W_uq, ((0, 0), (0, 0), (0, HEAD_PAD - QK_NOPE - QK_ROPE))).reshape(R, H * HEAD_PAD)
    W_uk = cols_full(G["w_uk"])
    W_uv = cols_full(G["w_uv"])
    W_ao = G["w_attn_out"].reshape(H * V_HEAD, D)
    W_o = G["w_out"].reshape(D, D)

    y_conv = mm(c3, W_co, mode="nn", tm=T, tn=tn_of(D), tk=C, out_dtype=BF, name="mm_y_conv")

    q = q_proj(cq, Wq_p, cos, sin, scale=scale, name="q_proj")
    kk = k_proj(ckv, W_uk, kr, name="k_proj")
    vv = mm(ckv, W_uv, mode="nn", tm=T, tn=tn_of(H * V_HEAD), tk=R, out_dtype=BF, name="mm_v")
    o_attn, lse = attn_fwd(q, kk, vv, nch=nch, name="attn_fwd")
    bufs_c = copies_wait(bufs_c, near_plan(ax_c), ss_c, rs_c, o_attn, name="gather_c_near_wait")
    bufs_c, ss_c, rs_c, tok_gc = copies_start(bufs_c, relay_plan(ax_c), 4 * len(grp_c), half_arr,
                                              name="gather_c_relay_start")
    y_attn = mm(o_attn, W_ao, mode="nn", tm=T, tn=tn_of(D), tk=H * V_HEAD, out_dtype=BF, name="mm_y_attn",
                deps=[tok_gc])

    mix = mix_fwd(z_gate, y_conv, y_attn, b_gate, b_conv_out, tr=tr, name="mix_fwd")
    bufs_c = copies_wait(bufs_c, relay_plan(ax_c), ss_c, rs_c, mix, name="gather_c_relay_wait")
    bufs_c, ss_c, rs_c, tok_gc = copies_start(bufs_c, far_forward_plan(ax_c), len(grp_c), half_arr,
                                              name="gather_c_far_start")
    h1 = mm(mix, W_o, mode="nn", tm=T, tn=tn_of(D), tk=D, out_dtype=F32, name="mm_h1", res=h0, deps=[tok_gc])

    hn = rms_fwd(h1, g_ffn, tr=tr, name="rms_ffn")

    G.update(zip(grp_c, copies_wait(bufs_c, far_forward_plan(ax_c), ss_c, rs_c, hn, name="gather_c_far_wait")))
    W_fg, W_fu = G["w_ffn_gate"], G["w_ffn_up"]
    W_fd = G["w_ffn_down"].reshape(FF, D)

    fa, fb, f = ffn_fwd(hn, W_fg, W_fu, tm=tmq, name="ffn_fwd")
    h2 = mm(f, W_fd, mode="nn", tm=tmh, tn=tn_of(D, 256), tk=FF, out_dtype=F32, name="mm_h2", res=h1)

    place_arr = jnp.stack([chip, ci.astype(jnp.int32)])

    def col_blocks(a):
        r, cfull = a.shape
        return a.reshape(r, N_CHIPS, cfull // N_CHIPS).transpose(1, 0, 2)

    def rs_begin(nms, gls, after, tag):
        ax = [axes[nm] for nm in nms]
        lands = []
        for g, a in zip(gls, ax):
            _, r, cfull = g.shape
            lands.append(lax.empty((N_CHIPS, r // 2, cfull) if a == 0 else (N_CHIPS, r, cfull // 2), BF))
        arrs, ssem, rsem, tok = copies_start(list(gls) + lands, exchange_plan(ax), N_CHIPS * len(nms), after,
                                             name=f"rs_{tag}_exchange_start")
        return (nms, ax, arrs, ssem, rsem), tok

    def rs_middle(state, after, tag):
        nms, ax, arrs, ssem, rsem = state
        n = len(nms)
        arrs = copies_wait(arrs, exchange_plan(ax), ssem, rsem, after, name=f"rs_{tag}_exchange_wait")
        pair = [pair_sum(g, r, half_arr, a, name="rs_pair_sum_" + nm)
                for nm, g, r, a in zip(nms, arrs[:n], arrs[n:], ax)]
        lands = [lax.empty((3,) + p.shape[1:], BF) for p in pair]
        arrs, ssem, rsem, tok = copies_start(pair + lands, scatter_plan(n), 3 * n, half_arr,
                                             name=f"rs_{tag}_scatter_start")
        return (nms, ax, arrs, ssem, rsem), tok

    def rs_end(state, after, tag):
        nms, ax, arrs, ssem, rsem = state
        n = len(nms)
        arrs = copies_wait(arrs, scatter_plan(n), ssem, rsem, after, name=f"rs_{tag}_scatter_wait")
        return [chip_sum(s, r, place_arr, a, name="rs_chip_sum_" + nm)
                for nm, s, r, a in zip(nms, arrs[:n], arrs[n:], ax)]

    d_h2, d_h2b, loss_p, d_g_final = final_loss(h2, tgt, g_final.reshape(1, D), seq=seq, tr=tr, name="final_loss")
    loss = lax.psum(loss_p[0, 0], ("x", "y", "c"))

    dW_fd = mm(f, d_h2b, mode="tn", tm=FB, tn=tn_of(D), tk=T, out_dtype=BF, name="mm_dw_ffn_down")
    d_a, d_b = ffn_dact(d_h2b, W_fd, fa, fb, tm=tmq, name="ffn_dact")
    dW_fg = mm(hn, d_a, mode="tn", tm=tmw, tn=FB, tk=T, out_dtype=BF, name="mm_dw_ffn_gate", out_blocks=N_CHIPS)
    dW_fu = mm(hn, d_b, mode="tn", tm=tmw, tn=FB, tk=T, out_dtype=BF, name="mm_dw_ffn_up", out_blocks=N_CHIPS)
    st_c, tok_rc = rs_begin(grp_c, [dW_fg, dW_fu, dW_fd.reshape(N_CHIPS, FB, D)], half_arr, "c")
    d_hn = ffn_dhn(d_a, d_b, W_fg, W_fu, tm=tmq, tn=_pick(D, (256,)), name="ffn_dhn", deps=[tok_rc])
    st_c, tok_rc = rs_middle(st_c, d_hn, "c")
    d_h1, d_h1b, d_g_ffn = rms_bwd(d_hn, h1, g_ffn, d_h2, tr=tr, name="rms_ffn_bwd", deps=[tok_rc])

    d_mix = mm(d_h1b, W_o, mode="nt", tm=T, tn=tn_of(D), tk=D, out_dtype=BF, name="mm_d_mix")
    dW_o = mm(mix, d_h1b, mode="tn", tm=tmw, tn=tn_of(D), tk=T, out_dtype=BF, name="mm_dw_out")
    d_yc, d_ya, d_zgate, d_b_gate, d_b_co = mix_bwd(z_gate, y_conv, y_attn, d_mix, b_gate, b_conv_out, tr=tr,
                                                    name="mix_bwd")

    dW_co = mm(c3, d_yc, mode="tn", tm=tmw, tn=tn_of(D), tk=T, out_dtype=BF, name="mm_dw_conv_out")
    d_c3 = mm(d_yc, W_co, mode="nt", tm=T, tn=tn_of(C), tk=D, out_dtype=BF, name="mm_d_c3")
    d_c1, d_g_ln, d_b_ln, d_b_dw = ln_silu_bwd(c1, d_c3, g_conv_ln, b_conv_ln, tr=tr, name="ln_silu_bwd")
    d_c0, d_w_dw = dwconv_bwd(c0, d_c1, w_dw_full, tc=tcv, rc=rc, name="dwconv_bwd")
    d_zglu, d_b_glu = glu_bwd(z_glu, b_glu, d_c0, tr=tr, name="glu_bwd")

    dW_ao = mm(o_attn, d_ya, mode="tn", tm=_pick(H * V_HEAD, (1024, 512)), tn=tn_of(D), tk=T, out_dtype=BF, name="mm_dw_attn_out")
    d_o = mm(d_ya, W_ao, mode="nt", tm=T, tn=tn_of(H * V_HEAD), tk=D, out_dtype=BF, name="mm_d_o")
    d_q2, d_k2, d_v, d_kr = attn_bwd(q, kk, vv, o_attn, d_o, lse, cos, sin, scale=scale, nch=nch, name="attn_bwd")

    dWq_p = mm(cq, d_q2, mode="tn", tm=R, tn=tn_of(H * HEAD_PAD), tk=T, out_dtype=BF, name="mm_dw_uq")
    d_cq = mm(d_q2, Wq_p, mode="nt", tm=T, tn=R, tk=_pick(H * HEAD_PAD, (1024,)), out_dtype=F32, name="mm_d_cq")
    Wk_p = jnp.pad(W_uk.reshape(R, H, QK_NOPE), ((0, 0), (0, 0), (0, HEAD_PAD - QK_NOPE))).reshape(R, H * HEAD_PAD)
    dWk_p = mm(ckv, d_k2, mode="tn", tm=R, tn=tn_of(H * HEAD_PAD), tk=T, out_dtype=BF, name="mm_dw_uk")
    dW_uv = mm(ckv, d_v, mode="tn", tm=R, tn=tn_of(H * V_HEAD), tk=T, out_dtype=BF, name="mm_dw_uv")
    red_c = rs_end(st_c, [dWq_p, dWk_p, dW_uv, dW_co, dW_ao, dW_o], "c")
    dW_uq = dWq_p.reshape(R, H, HEAD_PAD)[:, :, :QK_NOPE + QK_ROPE].reshape(R, H * (QK_NOPE + QK_ROPE))
    dW_uk = dWk_p.reshape(R, H, HEAD_PAD)[:, :, :QK_NOPE].reshape(R, H * QK_NOPE)
    st_b, tok_rb = rs_begin(grp_b, [dW_co.reshape(N_CHIPS, C // N_CHIPS, D), col_blocks(dW_uq), col_blocks(dW_uk),
                                    col_blocks(dW_uv), dW_ao.reshape(N_CHIPS, H * V_HEAD // N_CHIPS, D),
                                    dW_o.reshape(N_CHIPS, D // N_CHIPS, D)], red_c[-1], "b")
    d_ckv = mm(d_k2, Wk_p, mode="nt", tm=T, tn=R, tk=_pick(H * HEAD_PAD, (1024,)), out_dtype=F32, name="mm_d_ckv_k",
               deps=[tok_rb])
    d_ckv = mm(d_v, W_uv, mode="nt", tm=T, tn=R, tk=_pick(H * V_HEAD, (1024,)), out_dtype=F32, name="mm_d_ckv_v",
               res=d_ckv)
    d_zsmall, d_g_q, d_g_kv = lora_norm_bwd(z_small, d_cq, d_ckv, d_kr, cos, sin, g_q_lora, g_kv_lora, tr=tr,
                                            name="lora_norm_bwd")
    st_b, tok_rb = rs_middle(st_b, d_zsmall, "b")

    tmi = _pick(2 * C, (1024, 512))
    dWT_in = lax.empty((WT_in.shape[0], D), BF)
    dWT_in = mm_tn_into(d_zglu, u, dWT_in, 0, tm=tmi, tn=tn_of(D), name="mm_dw_glu", deps=[tok_rb])
    dWT_in = mm_tn_into(d_zsmall, u, dWT_in, o1, tm=2 * R + 128, tn=tn_of(D), name="mm_dw_small")
    dWT_in = mm_tn_into(d_zgate, u, dWT_in, o2, tm=tmi, tn=tn_of(D), name="mm_dw_gate")
    red_b = rs_end(st_b, dWT_in, "b")
    st_a, tok_ra = rs_begin(grp_a, [dWT_in.reshape(N_CHIPS, dWT_in.shape[0] // N_CHIPS, D)], red_b[-1], "a")
    d_u = mm(d_zglu, WT_in, mode="nn", tm=T, tn=tn_of(D), tk=_pick(2 * C, (2048, 1024)), out_dtype=F32, name="mm_d_u_glu",
             deps=[tok_ra])
    st_a, tok_ra = rs_middle(st_a, d_u, "a")
    d_u = mm(d_zsmall, WT_small, mode="nn", tm=T, tn=tn_of(D), tk=2 * R + 128, out_dtype=F32, name="mm_d_u_small",
             res=d_u, deps=[tok_ra])
    d_u = mm(d_zgate, WT_gate, mode="nn", tm=T, tn=tn_of(D), tk=_pick(2 * D, (2048, 1024)), out_dtype=F32,
             name="mm_d_u_gate", res=d_u)
    d_h0, _, d_g_mix = rms_bwd(d_u, h0, g_mix, d_h1, tr=tr, name="rms_mix_bwd")

    grad_x = d_h0[N_META:length][None]

    small = ["g_mix", "b_glu", "b_gate", "b_dw", "g_conv_ln", "b_conv_ln", "b_conv_out", "g_q_lora", "g_kv_lora",
             "g_ffn", "g_final", "w_dw", "meta_tokens"]
    sgrads = dict(g_mix=d_g_mix, b_glu=d_b_glu, b_gate=d_b_gate, b_dw=d_b_dw, g_conv_ln=d_g_ln, b_conv_ln=d_b_ln,
                  b_conv_out=d_b_co, g_q_lora=d_g_q, g_kv_lora=d_g_kv, g_ffn=d_g_ffn, g_final=d_g_final,
                  w_dw=d_w_dw[:CONV_WIDTH], meta_tokens=d_h0[:N_META])
    sizes = [int(np.prod(sgrads[nm].shape)) for nm in small]
    packed = jnp.concatenate([sgrads[nm].reshape(-1) for nm in small]).reshape(-1, 128)
    sm_arrs, sm_ss, sm_rs, tok_sm = copies_start([packed, lax.empty((N_DEV,) + packed.shape, F32)], devices_plan,
                                                 N_DEV - 1, packed, name="gather_small_start")

    grads, delta, new_m, new_v, delta_2d = {}, {}, {}, {}, {}

    def adamw_2d(nm, w2, g2, m2, v2, back):
        g_, d_, m_, v_ = adamw(w2, g2, m2, v2, name="adamw_" + nm)
        delta_2d[nm] = d_
        grads[nm], delta[nm], new_m[nm], new_v[nm] = back(g_), back(d_), back(m_), back(v_)

    reduced = dict(zip(grp_b + grp_c, share_halves(red_b + red_c, [axes[nm] for nm in grp_b + grp_c], tok_sm,
                                                   name="rs_share_halves_bc")))
    for nm in grp_b + grp_c + grp_a:
        if nm == grp_a[0]:
            red_a = rs_end(st_a, delta_2d[grp_c[-1]], "a")
            reduced.update(zip(grp_a, share_halves(red_a, [axes[nm] for nm in grp_a], tok_sm,
                                                   name="rs_share_halves_a")))
        shp = weights[nm].shape
        if axes[nm] == 1:
            adamw_2d(nm, shard[nm], reduced[nm], jnp.swapaxes(m_in[nm][0], 0, 1), jnp.swapaxes(v_in[nm][0], 0, 1),
                     lambda a: jnp.swapaxes(a, 0, 1)[None])
        else:
            two = (shp[-2], shp[-1])
            adamw_2d(nm, weights[nm].reshape(two), reduced[nm], m_in[nm].reshape(two), v_in[nm].reshape(two),
                     lambda a, shp=shp: a.reshape(shp))

    sm_arrs = copies_wait(sm_arrs, devices_plan, sm_ss, sm_rs, delta_2d[grp_a[0]], name="gather_small_wait")
    me = (4 * xi + 2 * yi + ci).astype(jnp.int32)
    parts = lax.dynamic_update_slice(sm_arrs[1], sm_arrs[0][None], (me, 0, 0))
    summed = sum_devices(parts, name="sum_small_grads").reshape(-1)
    offs = np.concatenate([[0], np.cumsum(sizes)])
    sfull = {nm: summed[int(offs[i]):int(offs[i + 1])].reshape(sgrads[nm].shape) for i, nm in enumerate(small)}
    for nm in small:
        gfull = sfull[nm]
        if nm == "w_dw":
            cb = C // N_CHIPS
            grads[nm] = lax.dynamic_slice(gfull, (0, chip * cb), (CONV_WIDTH, cb))[None]
        elif nm == "meta_tokens":
            cb = D // N_CHIPS
            grads[nm] = lax.dynamic_slice(gfull, (0, chip * cb), (N_META, cb))
        else:
            grads[nm] = gfull.reshape(weights[nm].shape)

    rep = [nm for nm in small if nm not in ("w_dw", "meta_tokens")]

    def pack(d):
        return jnp.concatenate([d[nm].reshape(-1) for nm in rep]).reshape(-1, 128)

    _, pd, pm, pv = adamw(pack(weights), pack(grads), pack(m_in), pack(v_in), name="adamw_small")
    rsz = [int(np.prod(weights[nm].shape)) for nm in rep]
    roff = np.concatenate([[0], np.cumsum(rsz)])
    for i, nm in enumerate(rep):
        sl = slice(int(roff[i]), int(roff[i + 1]))
        delta[nm] = pd.reshape(-1)[sl].reshape(weights[nm].shape)
        new_m[nm] = pm.reshape(-1)[sl].reshape(weights[nm].shape)
        new_v[nm] = pv.reshape(-1)[sl].reshape(weights[nm].shape)
    for nm in ["w_dw", "meta_tokens"]:
        shp = weights[nm].shape
        two = (shp[-2], shp[-1])
        adamw_2d(nm, weights[nm].reshape(two), grads[nm].reshape(two), m_in[nm].reshape(two), v_in[nm].reshape(two),
                 lambda a, shp=shp: a.reshape(shp))

    return (loss, grad_x, *[grads[nm] for nm in names], *[delta[nm] for nm in names],
            *[new_m[nm] for nm in names], *[new_v[nm] for nm in names])
```

```python
import functools

import numpy as np
import jax
import jax.numpy as jnp
from jax import lax
from jax.experimental import pallas as pl
from jax.experimental.pallas import tpu as pltpu

F32 = jnp.float32
BF = jnp.bfloat16
MESH = pl.DeviceIdType.MESH

N_META = 16
CONV_WIDTH = 31
CONV_PAD = 32
QK_NOPE = 128
QK_ROPE = 64
V_HEAD = 128
HEAD_PAD = 256
ROPE_THETA = 10000.0
EPS = 1e-6
ADAM_LR = 0.001
ADAM_B1 = 0.9
ADAM_B2 = 0.999
ADAM_EPS = 1e-08
ADAM_WD = 0.01
ADAM_STEP = 10
VMEM_LIMIT = 56 * 1024 * 1024
N_CHIPS = 4
N_DEV = 8


def _cparams(sem):
    return pltpu.CompilerParams(dimension_semantics=sem, vmem_limit_bytes=VMEM_LIMIT)


def mm(a, b, *, mode, tm, tn, tk, out_dtype, name, res=None, out_blocks=None, deps=(), n_out=None):
    b3 = b.ndim == 3
    if mode == "nn":
        M, K = a.shape
        N = b.shape[0] * b.shape[2] if b3 else b.shape[1]
        a_spec = pl.BlockSpec((tm, tk), lambda i, j, k: (i, k))
        if b3:
            per = b.shape[2] // tn
            b_spec = pl.BlockSpec((None, tk, tn), lambda i, j, k: (j // per, k, j % per))
        else:
            b_spec = pl.BlockSpec((tk, tn), lambda i, j, k: (k, j))
        dims = (((1,), (0,)), ((), ()))
    elif mode == "nt":
        M, K = a.shape
        N = n_out if n_out is not None else (b.shape[1] if b3 else b.shape[0])
        a_spec = pl.BlockSpec((tm, tk), lambda i, j, k: (i, k))
        if b3:
            per = b.shape[2] // tk
            b_spec = pl.BlockSpec((None, tn, tk), lambda i, j, k: (k // per, j, k % per))
        else:
            b_spec = pl.BlockSpec((tn, tk), lambda i, j, k: (j, k))
        dims = (((1,), (1,)), ((), ()))
    else:
        K, M = a.shape
        N = b.shape[1]
        a_spec = pl.BlockSpec((tk, tm), lambda i, j, k: (k, i))
        b_spec = pl.BlockSpec((tk, tn), lambda i, j, k: (k, j))
        dims = (((0,), (0,)), ((), ()))
    gm, gn, gk = M // tm, N // tn, K // tk
    assert gm * tm == M and gn * tn == N and gk * tk == K, (name, a.shape, b.shape, tm, tn, tk)
    if out_blocks is None:
        o_spec = pl.BlockSpec((tm, tn), lambda i, j, k: (i, j))
        o_shape = (M, N)
    else:
        nbw = N // out_blocks
        per_o = nbw // tn
        assert per_o * tn == nbw
        o_spec = pl.BlockSpec((None, tm, tn), lambda i, j, k: (j // per_o, i, j % per_o))
        o_shape = (out_blocks, M, nbw)
    has_res = res is not None

    def body(*refs):
        a_ref, b_ref = refs[0], refs[1]
        r_ref = refs[2] if has_res else None
        o_ref = refs[2 + has_res + len(deps)]
        acc = refs[-1]
        p = lax.dot_general(a_ref[...], b_ref[...], dims, preferred_element_type=F32)

        def finish(v):
            if has_res:
                v = v + r_ref[...]
            o_ref[...] = v.astype(o_ref.dtype)

        if gk == 1:
            finish(p)
        else:
            k = pl.program_id(2)

            @pl.when(k == 0)
            def _():
                acc[...] = p

            @pl.when(k > 0)
            def _():
                acc[...] += p

            @pl.when(k == gk - 1)
            def _():
                finish(acc[...])

    in_specs = [a_spec, b_spec]
    args = [a, b]
    if has_res:
        in_specs.append(pl.BlockSpec((tm, tn), lambda i, j, k: (i, j)))
        args.append(res)
    in_specs += [pl.BlockSpec(memory_space=pl.ANY)] * len(deps)
    args += list(deps)
    return pl.pallas_call(
        body, name=name, grid=(gm, gn, gk), in_specs=in_specs, out_specs=o_spec,
        out_shape=jax.ShapeDtypeStruct(o_shape, out_dtype),
        scratch_shapes=[pltpu.VMEM((tm, tn), F32)] if gk > 1 else [],
        compiler_params=_cparams(("parallel", "parallel", "arbitrary")),
    )(*args)


def mm_tn_into(a, b, buf, row0, *, tm, tn, name, deps=()):
    K, M = a.shape
    N = b.shape[1]
    gm, gn = M // tm, N // tn
    assert gm * tm == M and gn * tn == N and buf.shape[1] == N and row0 + M <= buf.shape[0]
    steps = gm * gn

    def body(a_ref, b_ref, buf_ref, *rest):
        o_ref, tile, sem = rest[len(deps):]
        i, j = pl.program_id(0), pl.program_id(1)
        s = i * gn + j
        slot = s % 2

        def out_copy(sl):
            return pltpu.make_async_copy(tile.at[sl], o_ref.at[pl.ds(row0 + i * tm, tm), pl.ds(j * tn, tn)],
                                         sem.at[sl])

        @pl.when(s >= 2)
        def _():
            out_copy(slot).wait()

        tile[slot] = lax.dot_general(a_ref[...], b_ref[...], (((0,), (0,)), ((), ())),
                                     preferred_element_type=F32).astype(tile.dtype)
        out_copy(slot).start()

        @pl.when(s == steps - 1)
        def _():
            out_copy(slot).wait()
            if steps >= 2:
                out_copy(1 - slot).wait()

    anyspec = pl.BlockSpec(memory_space=pl.ANY)
    return pl.pallas_call(
        body, name=name, grid=(gm, gn),
        in_specs=[pl.BlockSpec((K, tm), lambda i, j: (0, i)), pl.BlockSpec((K, tn), lambda i, j: (0, j)), anyspec]
        + [anyspec] * len(deps),
        out_specs=anyspec, out_shape=jax.ShapeDtypeStruct(buf.shape, buf.dtype),
        input_output_aliases={2: 0},
        scratch_shapes=[pltpu.VMEM((2, tm, tn), buf.dtype), pltpu.SemaphoreType.DMA((2,))],
        compiler_params=_cparams(("arbitrary", "arbitrary")),
    )(a, b, buf, *deps)


def rowwise(body, row_ins, full_ins, row_outs, acc_outs, *, tr, name, deps=()):
    T = row_ins[0].shape[0]
    assert T % tr == 0, (name, T, tr)
    n_ri, n_fi, n_ro = len(row_ins), len(full_ins), len(row_outs)
    n_in = n_ri + n_fi + len(deps)

    def kern(*refs):
        body(pl.program_id(0), refs[:n_ri], refs[n_ri:n_ri + n_fi], refs[n_in:n_in + n_ro], refs[n_in + n_ro:])

    in_specs = [pl.BlockSpec((tr, a.shape[1]), lambda i: (i, 0)) for a in row_ins]
    in_specs += [pl.BlockSpec(a.shape, lambda i: (0, 0)) for a in full_ins]
    in_specs += [pl.BlockSpec(memory_space=pl.ANY)] * len(deps)
    out_specs = [pl.BlockSpec((tr, c), lambda i: (i, 0)) for c, _ in row_outs]
    out_specs += [pl.BlockSpec(s, lambda i: (0, 0)) for s in acc_outs]
    out_shape = [jax.ShapeDtypeStruct((T, c), d) for c, d in row_outs]
    out_shape += [jax.ShapeDtypeStruct(s, F32) for s in acc_outs]
    return pl.pallas_call(
        kern, name=name, grid=(T // tr,), in_specs=in_specs, out_specs=out_specs, out_shape=out_shape,
        compiler_params=_cparams(("arbitrary",)),
    )(*row_ins, *full_ins, *deps)


def _acc(step, ref, val):
    @pl.when(step == 0)
    def _():
        ref[...] = val

    @pl.when(step > 0)
    def _():
        ref[...] += val


def _colsum(x):
    return jnp.sum(x, axis=0, keepdims=True)


def _sigmoid(x):
    return 1.0 / (1.0 + jnp.exp(-x))


def _rot_half(v):
    lane = lax.broadcasted_iota(jnp.int32, v.shape, 1)
    lo = -pltpu.roll(v, 96, 1)
    hi = pltpu.roll(v, 32, 1)
    return jnp.where(lane < 32, lo, jnp.where(lane < 64, hi, 0.0))


def rms_fwd(h, g, *, tr, name, deps=()):
    def body(step, ri, fi, ro, ao):
        x = ri[0][...]
        r = lax.rsqrt(jnp.mean(x * x, axis=-1, keepdims=True) + EPS)
        ro[0][...] = ((x * r) * fi[0][...]).astype(BF)

    return rowwise(body, [h], [g], [(h.shape[1], BF)], [], tr=tr, name=name, deps=deps)[0]


def rms_bwd(dy, h, g, dres, *, tr, name, deps=()):
    D = h.shape[1]

    def body(step, ri, fi, ro, ao):
        d, x, dr = ri[0][...].astype(F32), ri[1][...], ri[2][...]
        r = lax.rsqrt(jnp.mean(x * x, axis=-1, keepdims=True) + EPS)
        n = x * r
        _acc(step, ao[0], _colsum(d * n))
        dn = d * fi[0][...]
        dh = r * (dn - n * jnp.mean(dn * n, axis=-1, keepdims=True)) + dr
        ro[0][...] = dh
        ro[1][...] = dh.astype(BF)

    return rowwise(body, [dy, h, dres], [g], [(D, F32), (D, BF)], [(1, D)], tr=tr, name=name, deps=deps)


def glu_fwd(z, b, *, tr, name):
    C = z.shape[1] // 2

    def body(step, ri, fi, ro, ao):
        zz = ri[0][...].astype(F32) + fi[0][...]
        ro[0][...] = zz[:, :C] * _sigmoid(zz[:, C:])

    return rowwise(body, [z], [b], [(C, F32)], [], tr=tr, name=name)[0]


def glu_bwd(z, b, dc, *, tr, name):
    C = z.shape[1] // 2

    def body(step, ri, fi, ro, ao):
        zz = ri[0][...].astype(F32) + fi[0][...]
        d = ri[1][...]
        za, sg = zz[:, :C], _sigmoid(zz[:, C:])
        dza = d * sg
        dzb = d * za * sg * (1.0 - sg)
        ro[0][:, :C] = dza.astype(BF)
        ro[0][:, C:] = dzb.astype(BF)
        _acc(step, ao[0], _colsum(dza))
        _acc(step, ao[1], _colsum(dzb))

    dz, da, db = rowwise(body, [z, dc], [b], [(2 * C, BF)], [(1, C), (1, C)], tr=tr, name=name)
    return dz, jnp.concatenate([da, db], axis=1)


def ln_silu_fwd(c1, g, b, *, tr, name):
    def body(step, ri, fi, ro, ao):
        x = ri[0][...]
        mu = jnp.mean(x, axis=-1, keepdims=True)
        xc = x - mu
        rstd = lax.rsqrt(jnp.mean(xc * xc, axis=-1, keepdims=True) + EPS)
        y = (xc * rstd) * fi[0][...] + fi[1][...]
        ro[0][...] = (y * _sigmoid(y)).astype(BF)

    return rowwise(body, [c1], [g, b], [(c1.shape[1], BF)], [], tr=tr, name=name)[0]


def ln_silu_bwd(c1, dc3, g, b, *, tr, name):
    C = c1.shape[1]

    def body(step, ri, fi, ro, ao):
        x, d3 = ri[0][...], ri[1][...].astype(F32)
        mu = jnp.mean(x, axis=-1, keepdims=True)
        xc = x - mu
        rstd = lax.rsqrt(jnp.mean(xc * xc, axis=-1, keepdims=True) + EPS)
        n = xc * rstd
        y = n * fi[0][...] + fi[1][...]
        sg = _sigmoid(y)
        dy = d3 * (sg * (1.0 + y * (1.0 - sg)))
        _acc(step, ao[0], _colsum(dy * n))
        _acc(step, ao[1], _colsum(dy))
        dn = dy * fi[0][...]
        dx = rstd * (dn - jnp.mean(dn, axis=-1, keepdims=True) - n * jnp.mean(dn * n, axis=-1, keepdims=True))
        ro[0][...] = dx
        _acc(step, ao[2], _colsum(dx))

    return rowwise(body, [c1, dc3], [g, b], [(C, F32)], [(1, C), (1, C), (1, C)], tr=tr, name=name)


def lora_norm_fwd(zs, gq, gkv, cos, sin, *, tr, name, deps=()):
    R = gq.shape[1]

    def body(step, ri, fi, ro, ao):
        z = ri[0][...]
        for o, gi in ((0, 0), (1, 1)):
            x = z[:, o * R:(o + 1) * R]
            r = lax.rsqrt(jnp.mean(x * x, axis=-1, keepdims=True) + EPS)
            ro[o][...] = ((x * r) * fi[gi][...]).astype(BF)
        kr = z[:, 2 * R:2 * R + 128]
        ro[2][...] = kr * ri[1][...] + _rot_half(kr) * ri[2][...]

    return rowwise(body, [zs, cos, sin], [gq, gkv], [(R, BF), (R, BF), (128, F32)], [], tr=tr, name=name, deps=deps)


def lora_norm_bwd(zs, dcq, dckv, dkr, cos, sin, gq, gkv, *, tr, name):
    R = gq.shape[1]

    def body(step, ri, fi, ro, ao):
        z = ri[0][...]
        for o in (0, 1):
            x = z[:, o * R:(o + 1) * R]
            d = ri[1 + o][...]
            r = lax.rsqrt(jnp.mean(x * x, axis=-1, keepdims=True) + EPS)
            n = x * r
            _acc(step, ao[o], _colsum(d * n))
            dn = d * fi[o][...]
            ro[0][:, o * R:(o + 1) * R] = (r * (dn - n * jnp.mean(dn * n, axis=-1, keepdims=True))).astype(BF)
        dk = ri[3][...]
        ro[0][:, 2 * R:2 * R + 128] = (dk * ri[4][...] - _rot_half(dk * ri[5][...])).astype(BF)

    return rowwise(body, [zs, dcq, dckv, dkr, cos, sin], [gq, gkv], [(2 * R + 128, BF)], [(1, R), (1, R)],
                   tr=tr, name=name)


def mix_fwd(zg, yc, ya, bg, bco, *, tr, name):
    D = yc.shape[1]

    def body(step, ri, fi, ro, ao):
        g = _sigmoid(ri[0][...].astype(F32) + fi[0][...])
        ro[0][...] = (g[:, :D] * (ri[1][...].astype(F32) + fi[1][...]) + g[:, D:] * ri[2][...].astype(F32)).astype(BF)

    return rowwise(body, [zg, yc, ya], [bg, bco], [(D, BF)], [], tr=tr, name=name)[0]


def mix_bwd(zg, yc, ya, dmix, bg, bco, *, tr, name):
    D = yc.shape[1]

    def body(step, ri, fi, ro, ao):
        g = _sigmoid(ri[0][...].astype(F32) + fi[0][...])
        gc, ga = g[:, :D], g[:, D:]
        ycv = ri[1][...].astype(F32) + fi[1][...]
        yav = ri[2][...].astype(F32)
        dm = ri[3][...].astype(F32)
        dyc = dm * gc
        ro[0][...] = dyc.astype(BF)
        ro[1][...] = (dm * ga).astype(BF)
        dzc = dm * ycv * gc * (1.0 - gc)
        dza = dm * yav * ga * (1.0 - ga)
        ro[2][:, :D] = dzc.astype(BF)
        ro[2][:, D:] = dza.astype(BF)
        _acc(step, ao[0], _colsum(dzc))
        _acc(step, ao[1], _colsum(dza))
        _acc(step, ao[2], _colsum(dyc))

    dyc, dya, dzg, dbc, dba, dbo = rowwise(body, [zg, yc, ya, dmix], [bg, bco], [(D, BF), (D, BF), (2 * D, BF)],
                                           [(1, D), (1, D), (1, D)], tr=tr, name=name)
    return dyc, dya, dzg, jnp.concatenate([dbc, dba], axis=1), dbo


def ffn_fwd(hn, wg, wu, *, tm, name):
    T, D = hn.shape
    nb, _, FB = wg.shape
    assert T % tm == 0

    def body(h_ref, g_ref, u_ref, a_ref, b_ref, f_ref):
        h = h_ref[...]
        a = jnp.dot(h, g_ref[...], preferred_element_type=F32)
        b = jnp.dot(h, u_ref[...], preferred_element_type=F32)
        a_ref[...] = a
        b_ref[...] = b
        f_ref[...] = (a * _sigmoid(a) * b).astype(BF)

    wspec = pl.BlockSpec((None, D, FB), lambda j, i: (j, 0, 0))
    ospec = pl.BlockSpec((tm, FB), lambda j, i: (i, j))
    return pl.pallas_call(
        body, name=name, grid=(nb, T // tm),
        in_specs=[pl.BlockSpec((tm, D), lambda j, i: (i, 0)), wspec, wspec], out_specs=[ospec, ospec, ospec],
        out_shape=[jax.ShapeDtypeStruct((T, nb * FB), F32), jax.ShapeDtypeStruct((T, nb * FB), F32),
                   jax.ShapeDtypeStruct((T, nb * FB), BF)],
        compiler_params=_cparams(("parallel", "parallel")),
    )(hn, wg, wu)


def ffn_dact(dh, wd, a, b, *, tm, name):
    T, D = dh.shape
    FF = wd.shape[0]
    FB = FF // N_CHIPS

    def body(dh_ref, w_ref, a_ref, b_ref, da_ref, db_ref):
        d = lax.dot_general(dh_ref[...], w_ref[...], (((1,), (1,)), ((), ())), preferred_element_type=F32)
        x, u = a_ref[...], b_ref[...]
        sg = _sigmoid(x)
        da_ref[...] = (d * u * (sg * (1.0 + x * (1.0 - sg)))).astype(BF)
        db_ref[...] = (d * (x * sg)).astype(BF)

    blk = pl.BlockSpec((tm, FB), lambda j, i: (i, j))
    return pl.pallas_call(
        body, name=name, grid=(N_CHIPS, T // tm),
        in_specs=[pl.BlockSpec((tm, D), lambda j, i: (i, 0)), pl.BlockSpec((FB, D), lambda j, i: (j, 0)), blk, blk],
        out_specs=[blk, blk],
        out_shape=[jax.ShapeDtypeStruct((T, FF), BF), jax.ShapeDtypeStruct((T, FF), BF)],
        compiler_params=_cparams(("parallel", "parallel")),
    )(dh, wd, a, b)


def ffn_dhn(da, db, wg, wu, *, tm, tn, name, deps=()):
    T, FF = da.shape
    nb, D, FB = wg.shape
    dims = (((1,), (1,)), ((), ()))

    def body(da_ref, db_ref, g_ref, u_ref, *rest):
        o_ref = rest[len(deps)]
        acc = None
        for k in range(nb):
            cols = slice(k * FB, (k + 1) * FB)
            p = lax.dot_general(da_ref[:, cols], g_ref[k], dims, preferred_element_type=F32)
            p = p + lax.dot_general(db_ref[:, cols], u_ref[k], dims, preferred_element_type=F32)
            acc = p if acc is None else acc + p
        o_ref[...] = acc.astype(BF)

    aspec = pl.BlockSpec((tm, FF), lambda i, j: (i, 0))
    wspec = pl.BlockSpec((nb, tn, FB), lambda i, j: (0, j, 0))
    return pl.pallas_call(
        body, name=name, grid=(T // tm, D // tn),
        in_specs=[aspec, aspec, wspec, wspec] + [pl.BlockSpec(memory_space=pl.ANY)] * len(deps),
        out_specs=pl.BlockSpec((tm, tn), lambda i, j: (i, j)),
        out_shape=jax.ShapeDtypeStruct((T, D), BF),
        compiler_params=_cparams(("parallel", "parallel")),
    )(da, db, wg, wu, *deps)


def final_loss(h2, tgt, g, *, seq, tr, name):
    D = h2.shape[1]

    def body(step, ri, fi, ro, ao):
        x = ri[0][...]
        r = lax.rsqrt(jnp.mean(x * x, axis=-1, keepdims=True) + EPS)
        n = x * r
        gg = fi[0][...]
        row = lax.broadcasted_iota(jnp.int32, (tr, 1), 0) + step * tr
        live = jnp.logical_and(row >= N_META, row < N_META + seq)
        e = jnp.where(live, n * gg - ri[1][...], 0.0)
        _acc(step, ao[0], jnp.broadcast_to(0.5 * jnp.sum(jnp.mean(e * e, axis=-1, keepdims=True)), (1, 128)))
        dy = e * (1.0 / D)
        _acc(step, ao[1], _colsum(dy * n))
        dn = dy * gg
        dh = r * (dn - n * jnp.mean(dn * n, axis=-1, keepdims=True))
        ro[0][...] = dh
        ro[1][...] = dh.astype(BF)

    return rowwise(body, [h2, tgt], [g], [(D, F32), (D, BF)], [(1, 128), (1, D)], tr=tr, name=name)


def _shifted_copies(s_ref, T, rc):
    for r in range(1, 8):
        for t0 in range(0, T, rc):
            s_ref[r, t0:t0 + rc, :] = s_ref[0, t0 + r:t0 + r + rc, :]
        s_ref[r, T:T + CONV_PAD - 8, :] = s_ref[0, T + r:T + r + CONV_PAD - 8, :]


def _shifted(s_ref, start, rows):
    return s_ref[start % 8, start - start % 8:start - start % 8 + rows, :]


def dwconv_fwd(x, w, b, *, tc, rc, name, deps=()):
    T, C = x.shape
    nchunk = T // rc
    assert nchunk * rc == T and C % tc == 0

    def body(x_ref, w_ref, b_ref, *rest):
        y_ref, xs_ref = rest[len(deps):]
        xs_ref[0, 0:CONV_PAD, :] = jnp.zeros((CONV_PAD, tc), F32)
        xs_ref[0, CONV_PAD:CONV_PAD + T, :] = x_ref[...]
        _shifted_copies(xs_ref, T, rc)
        off = CONV_PAD - (CONV_WIDTH - 1)
        for ci in range(nchunk):
            t0 = ci * rc
            acc = jnp.broadcast_to(b_ref[...], (rc, tc))
            for j in range(CONV_WIDTH):
                acc = acc + w_ref[j:j + 1, :] * _shifted(xs_ref, t0 + off + j, rc)
            y_ref[t0:t0 + rc, :] = acc

    return pl.pallas_call(
        body, name=name, grid=(C // tc,),
        in_specs=[pl.BlockSpec((T, tc), lambda j: (0, j)), pl.BlockSpec((32, tc), lambda j: (0, j)),
                  pl.BlockSpec((1, tc), lambda j: (0, j))] + [pl.BlockSpec(memory_space=pl.ANY)] * len(deps),
        out_specs=pl.BlockSpec((T, tc), lambda j: (0, j)),
        out_shape=jax.ShapeDtypeStruct((T, C), F32),
        scratch_shapes=[pltpu.VMEM((8, T + CONV_PAD, tc), F32)],
        compiler_params=_cparams(("parallel",)),
    )(x, w, b, *deps)


def dwconv_bwd(x, dy, w, *, tc, rc, name):
    T, C = x.shape
    nchunk = T // rc
    assert nchunk * rc == T and C % tc == 0 and rc % 8 == 0

    def body(x_ref, dy_ref, w_ref, dx_ref, dw_ref, xs_ref, ds_ref):
        off = CONV_PAD - (CONV_WIDTH - 1)
        xs_ref[0, 0:CONV_PAD, :] = jnp.zeros((CONV_PAD, tc), F32)
        xs_ref[0, CONV_PAD:CONV_PAD + T, :] = x_ref[...]
        _shifted_copies(xs_ref, T, rc)
        ds_ref[0, 0:T, :] = dy_ref[...]
        ds_ref[0, T:T + CONV_PAD, :] = jnp.zeros((CONV_PAD, tc), F32)
        _shifted_copies(ds_ref, T, rc)
        for ci in range(nchunk):
            t0 = ci * rc
            acc = jnp.zeros((rc, tc), F32)
            for j in range(CONV_WIDTH):
                acc = acc + w_ref[j:j + 1, :] * _shifted(ds_ref, t0 + (CONV_WIDTH - 1) - j, rc)
            dx_ref[t0:t0 + rc, :] = acc
        dw_ref[...] = jnp.zeros((32, tc), F32)
        for j in range(CONV_WIDTH):
            acc = jnp.zeros((8, tc), F32)
            for ci in range(nchunk):
                t0 = ci * rc
                pr = dy_ref[t0:t0 + rc, :] * _shifted(xs_ref, t0 + off + j, rc)
                acc = acc + jnp.sum(pr.reshape(rc // 8, 8, tc), axis=0)
            dw_ref[j:j + 1, :] = jnp.sum(acc, axis=0, keepdims=True)

    return pl.pallas_call(
        body, name=name, grid=(C // tc,),
        in_specs=[pl.BlockSpec((T, tc), lambda j: (0, j)), pl.BlockSpec((T, tc), lambda j: (0, j)),
                  pl.BlockSpec((32, tc), lambda j: (0, j))],
        out_specs=[pl.BlockSpec((T, tc), lambda j: (0, j)), pl.BlockSpec((32, tc), lambda j: (0, j))],
        out_shape=[jax.ShapeDtypeStruct((T, C), F32), jax.ShapeDtypeStruct((32, C), F32)],
        scratch_shapes=[pltpu.VMEM((8, T + CONV_PAD, tc), F32), pltpu.VMEM((8, T + CONV_PAD, tc), F32)],
        compiler_params=_cparams(("parallel",)),
    )(x, dy, w)


def q_proj(cq, wq, cos, sin, *, scale, name):
    T, R = cq.shape
    H = wq.shape[1] // HEAD_PAD

    def body(a_ref, b_ref, c_ref, s_ref, o_ref):
        p = jnp.dot(a_ref[...], b_ref[...], preferred_element_type=F32)
        o_ref[:, :QK_NOPE] = (p[:, :QK_NOPE] * scale).astype(BF)
        x = p[:, QK_NOPE:]
        o_ref[:, QK_NOPE:] = ((x * c_ref[...] + _rot_half(x) * s_ref[...]) * scale).astype(BF)

    return pl.pallas_call(
        body, name=name, grid=(H,),
        in_specs=[pl.BlockSpec((T, R), lambda h: (0, 0)), pl.BlockSpec((R, HEAD_PAD), lambda h: (0, h)),
                  pl.BlockSpec((T, 128), lambda h: (0, 0)), pl.BlockSpec((T, 128), lambda h: (0, 0))],
        out_specs=pl.BlockSpec((T, HEAD_PAD), lambda h: (0, h)),
        out_shape=jax.ShapeDtypeStruct((T, H * HEAD_PAD), BF),
        compiler_params=_cparams(("parallel",)),
    )(cq, wq, cos, sin)


def k_proj(ckv, wk, kr, *, name):
    T, R = ckv.shape
    H = wk.shape[1] // QK_NOPE

    def body(a_ref, b_ref, kr_ref, o_ref):
        p = jnp.dot(a_ref[...], b_ref[...], preferred_element_type=F32)
        o_ref[:, :QK_NOPE] = p.astype(BF)
        o_ref[:, QK_NOPE:] = kr_ref[...].astype(BF)

    return pl.pallas_call(
        body, name=name, grid=(H,),
        in_specs=[pl.BlockSpec((T, R), lambda h: (0, 0)), pl.BlockSpec((R, QK_NOPE), lambda h: (0, h)),
                  pl.BlockSpec((T, 128), lambda h: (0, 0))],
        out_specs=pl.BlockSpec((T, HEAD_PAD), lambda h: (0, h)),
        out_shape=jax.ShapeDtypeStruct((T, H * HEAD_PAD), BF),
        compiler_params=_cparams(("parallel",)),
    )(ckv, wk, kr)


def _causal_scores(q, k_ref, qi, CH, split):
    nt = (((1,), (1,)), ((), ()))
    pieces = []
    if not split:
        rows = slice(0, (qi + 1) * CH)
        s = lax.dot_general(q, k_ref[rows, :], nt, preferred_element_type=F32)
        keep = lax.broadcasted_iota(jnp.int32, s.shape, 1) <= lax.broadcasted_iota(jnp.int32, s.shape, 0) + qi * CH
        return [(rows, jnp.where(keep, s, -1e30))]
    if qi > 0:
        rows = slice(0, qi * CH)
        pieces.append((rows, lax.dot_general(q, k_ref[rows, :], nt, preferred_element_type=F32)))
    rows = slice(qi * CH, (qi + 1) * CH)
    s = lax.dot_general(q, k_ref[rows, :], nt, preferred_element_type=F32)
    keep = lax.broadcasted_iota(jnp.int32, s.shape, 1) <= lax.broadcasted_iota(jnp.int32, s.shape, 0)
    pieces.append((rows, jnp.where(keep, s, -1e30)))
    return pieces


def attn_fwd(q, k, v, *, nch, name):
    T = q.shape[0]
    H = q.shape[1] // HEAD_PAD
    CH = T // nch
    assert CH * nch == T and CH % 16 == 0

    def body(q_ref, k_ref, v_ref, o_ref, lse_ref):
        for qi in range(nch):
            L = (qi + 1) * CH
            pieces = _causal_scores(q_ref[qi * CH:L, :], k_ref, qi, CH, True)
            m = functools.reduce(jnp.maximum, [jnp.max(s, axis=1, keepdims=True) for _, s in pieces])
            l, o = 0.0, 0.0
            for rows, s in pieces:
                p = jnp.exp(s - m)
                l = l + jnp.sum(p, axis=1, keepdims=True)
                o = o + jnp.dot(p.astype(BF), v_ref[rows, :], preferred_element_type=F32)
            o_ref[qi * CH:L, :] = (o / l).astype(BF)
            lse_ref[qi * CH:L, :] = jnp.broadcast_to(m + jnp.log(l), (CH, V_HEAD))

    return pl.pallas_call(
        body, name=name, grid=(H,),
        in_specs=[pl.BlockSpec((T, HEAD_PAD), lambda h: (0, h)), pl.BlockSpec((T, HEAD_PAD), lambda h: (0, h)),
                  pl.BlockSpec((T, V_HEAD), lambda h: (0, h))],
        out_specs=[pl.BlockSpec((T, V_HEAD), lambda h: (0, h)), pl.BlockSpec((T, V_HEAD), lambda h: (0, h))],
        out_shape=[jax.ShapeDtypeStruct((T, H * V_HEAD), BF), jax.ShapeDtypeStruct((T, H * V_HEAD), F32)],
        compiler_params=_cparams(("parallel",)),
    )(q, k, v)


def attn_bwd(q, k, v, o, do, lse, cos, sin, *, scale, nch, name):
    T = q.shape[0]
    H = q.shape[1] // HEAD_PAD
    CH = T // nch
    tn_dims = (((0,), (0,)), ((), ()))
    nt_dims = (((1,), (1,)), ((), ()))

    def body(q_ref, k_ref, v_ref, o_ref, do_ref, lse_ref, c_ref, s_ref,
             dq_ref, dk_ref, dv_ref, dkr_ref, dk_acc, dv_acc):
        h = pl.program_id(0)
        dk_acc[...] = jnp.zeros_like(dk_acc)
        dv_acc[...] = jnp.zeros_like(dv_acc)
        for qi in range(nch):
            L = (qi + 1) * CH
            rows = slice(qi * CH, L)
            qc, doc = q_ref[rows, :], do_ref[rows, :]
            lse_c = lse_ref[rows, 0:1]
            dsum = jnp.sum(doc.astype(F32) * o_ref[rows, :].astype(F32), axis=1, keepdims=True)
            dq = 0.0
            for krows, s in _causal_scores(qc, k_ref, qi, CH, False):
                p = jnp.exp(s - lse_c)
                dp = lax.dot_general(doc, v_ref[krows, :], nt_dims, preferred_element_type=F32)
                ds = (p * (dp - dsum)).astype(BF)
                dq = dq + jnp.dot(ds, k_ref[krows, :], preferred_element_type=F32)
                dk_acc[krows, :] += lax.dot_general(ds, qc, tn_dims, preferred_element_type=F32)
                dv_acc[krows, :] += lax.dot_general(p.astype(BF), doc, tn_dims, preferred_element_type=F32)
            dq_ref[rows, :QK_NOPE] = (dq[:, :QK_NOPE] * scale).astype(BF)
            dyr = dq[:, QK_NOPE:]
            dq_ref[rows, QK_NOPE:] = ((dyr * c_ref[rows, :] - _rot_half(dyr * s_ref[rows, :])) * scale).astype(BF)
        dk_ref[:, :QK_NOPE] = dk_acc[:, :QK_NOPE].astype(BF)
        dk_ref[:, QK_NOPE:] = jnp.zeros((T, HEAD_PAD - QK_NOPE), BF)
        dv_ref[...] = dv_acc[...].astype(BF)

        @pl.when(h == 0)
        def _():
            dkr_ref[...] = dk_acc[:, QK_NOPE:]

        @pl.when(h > 0)
        def _():
            dkr_ref[...] += dk_acc[:, QK_NOPE:]

    hp = pl.BlockSpec((T, HEAD_PAD), lambda h: (0, h))
    hv = pl.BlockSpec((T, V_HEAD), lambda h: (0, h))
    tab = pl.BlockSpec((T, 128), lambda h: (0, 0))
    return pl.pallas_call(
        body, name=name, grid=(H,),
        in_specs=[hp, hp, hv, hv, hv, hv, tab, tab],
        out_specs=[hp, hp, hv, tab],
        out_shape=[jax.ShapeDtypeStruct((T, H * HEAD_PAD), BF), jax.ShapeDtypeStruct((T, H * HEAD_PAD), BF),
                   jax.ShapeDtypeStruct((T, H * V_HEAD), BF), jax.ShapeDtypeStruct((T, 128), F32)],
        scratch_shapes=[pltpu.VMEM((T, HEAD_PAD), F32), pltpu.VMEM((T, V_HEAD), F32)],
        compiler_params=_cparams(("arbitrary",)),
    )(q, k, v, o, do, lse, cos, sin)


def _place():
    x, y, c = lax.axis_index("x"), lax.axis_index("y"), lax.axis_index("c")
    return x, y, c


def _other_chips(x, y):
    return [(1 - x, y), (x, 1 - y), (1 - x, 1 - y)]


HBM_SPEC = pl.BlockSpec(memory_space=pltpu.HBM)
SEM_SPEC = pl.BlockSpec(memory_space=pltpu.SEMAPHORE)
DATAFLOW = pltpu.SideEffectType.DATAFLOW_SIDE_EFFECTING


def _hbm(a):
    return pltpu.with_memory_space_constraint(a, pltpu.HBM)


def _half(ref, lead, c, axis):
    rows, cols = ref.shape[-2], ref.shape[-1]
    if axis == 0:
        return ref.at[(*lead, pl.ds(c * (rows // 2), rows // 2))]
    return ref.at[(*lead, slice(None), pl.ds(c * (cols // 2), cols // 2))]


def copies_start(arrs, plan, n_sems, after, *, name):
    n = len(arrs)

    def body(*refs):
        outs = refs[n + 1:2 * n + 1]
        send_sems, recv_sems, token = refs[2 * n + 1:]
        for cp in plan(outs, send_sems, recv_sems, False):
            cp.start()
        token[...] = jnp.zeros_like(token)

    res = pl.pallas_call(
        body, name=name,
        out_shape=tuple(pltpu.HBM(a.shape, a.dtype) for a in arrs)
        + (pltpu.SemaphoreType.DMA((n_sems,)), pltpu.SemaphoreType.DMA((n_sems,)), jax.ShapeDtypeStruct((8, 128), F32)),
        in_specs=(HBM_SPEC,) * n + (pl.BlockSpec(memory_space=pl.ANY),),
        out_specs=(HBM_SPEC,) * n + (SEM_SPEC, SEM_SPEC, pl.BlockSpec(memory_space=pltpu.VMEM)),
        input_output_aliases={i: i for i in range(n)},
        compiler_params=pltpu.CompilerParams(has_side_effects=DATAFLOW),
    )(*[_hbm(a) for a in arrs], after)
    return list(res[:n]), res[n], res[n + 1], res[n + 2]


def copies_wait(arrs, plan, send_sems, recv_sems, after, *, name):
    n = len(arrs)
    afters = list(after) if isinstance(after, (list, tuple)) else [after]

    def body(*refs):
        for cp in plan(refs[:n], refs[n], refs[n + 1], True):
            cp.wait_send()
            cp.wait_recv()

    res = pl.pallas_call(
        body, name=name,
        out_shape=tuple(pltpu.HBM(a.shape, a.dtype) for a in arrs),
        in_specs=(HBM_SPEC,) * n + (SEM_SPEC, SEM_SPEC) + (pl.BlockSpec(memory_space=pl.ANY),) * len(afters),
        out_specs=(HBM_SPEC,) * n,
        input_output_aliases={i: i for i in range(n)},
        compiler_params=pltpu.CompilerParams(has_side_effects=DATAFLOW),
    )(*arrs, send_sems, recv_sems, *afters)
    return list(res)


def gather_plan(axes):
    def plan(bufs, send_sems, recv_sems, receiving):
        x, y, c = _place()
        me = 2 * x + y
        out = []
        for i, b in enumerate(bufs):
            for j, (cx, cy) in enumerate(_other_chips(x, y)):
                src = _half(b, (me,), c, axes[i])
                dst = _half(b, (2 * cx + cy,), c, axes[i]) if receiving else src
                out.append(pltpu.make_async_remote_copy(
                    src_ref=src, dst_ref=dst, send_sem=send_sems.at[3 * i + j], recv_sem=recv_sems.at[3 * i + j],
                    device_id=(cx, cy, c), device_id_type=MESH))
        return out
    return plan


def _half_part(ref, lead, c, axis, part):
    rows, cols = ref.shape[-2], ref.shape[-1]
    span = rows // 2 if axis == 0 else rows
    p1 = -(-(span // 2) // 16) * 16
    lo, n = (0, p1) if part == 0 else (p1, span - p1)
    if axis == 0:
        return ref.at[(*lead, pl.ds(c * span + lo, n))]
    return ref.at[(*lead, pl.ds(lo, n), pl.ds(c * (cols // 2), cols // 2))]


def near_plan(axes):
    def plan(bufs, send_sems, recv_sems, receiving):
        x, y, c = _place()
        me = 2 * x + y
        out = []
        for i, b in enumerate(bufs):
            for j, (cx, cy) in enumerate(_other_chips(x, y)[:2]):
                src = _half(b, (me,), c, axes[i])
                dst = _half(b, (2 * cx + cy,), c, axes[i]) if receiving else src
                out.append(pltpu.make_async_remote_copy(
                    src_ref=src, dst_ref=dst, send_sem=send_sems.at[2 * i + j], recv_sem=recv_sems.at[2 * i + j],
                    device_id=(cx, cy, c), device_id_type=MESH))
        return out
    return plan


def relay_plan(axes):
    def plan(bufs, send_sems, recv_sems, receiving):
        x, y, c = _place()
        (xn, yn, dg) = [2 * cx + cy for cx, cy in _other_chips(x, y)]
        to = [(x, 1 - y, c), (1 - x, y, c), (x, y, 1 - c), (x, y, 1 - c)]
        out = []
        for i, b in enumerate(bufs):
            if receiving:
                blks = [_half_part(b, (dg,), c, axes[i], 0), _half_part(b, (dg,), c, axes[i], 1),
                        _half(b, (xn,), 1 - c, axes[i]), _half(b, (yn,), 1 - c, axes[i])]
            else:
                blks = [_half_part(b, (xn,), c, axes[i], 0), _half_part(b, (yn,), c, axes[i], 1),
                        _half(b, (xn,), c, axes[i]), _half(b, (yn,), c, axes[i])]
            for k in range(4):
                out.append(pltpu.make_async_remote_copy(
                    src_ref=blks[k], dst_ref=blks[k], send_sem=send_sems.at[4 * i + k],
                    recv_sem=recv_sems.at[4 * i + k], device_id=to[k], device_id_type=MESH))
        return out
    return plan


def far_forward_plan(axes):
    def plan(bufs, send_sems, recv_sems, receiving):
        x, y, c = _place()
        dg = 2 * (1 - x) + (1 - y)
        out = []
        for i, b in enumerate(bufs):
            blk = _half(b, (dg,), (1 - c) if receiving else c, axes[i])
            out.append(pltpu.make_async_remote_copy(
                src_ref=blk, dst_ref=blk, send_sem=send_sems.at[i], recv_sem=recv_sems.at[i],
                device_id=(x, y, 1 - c), device_id_type=MESH))
        return out
    return plan


def forward_plan(axes):
    def plan(bufs, send_sems, recv_sems, receiving):
        x, y, c = _place()
        out = []
        for i, b in enumerate(bufs):
            for j, (cx, cy) in enumerate(_other_chips(x, y)):
                blk = _half(b, (2 * cx + cy,), (1 - c) if receiving else c, axes[i])
                out.append(pltpu.make_async_remote_copy(
                    src_ref=blk, dst_ref=blk, send_sem=send_sems.at[3 * i + j], recv_sem=recv_sems.at[3 * i + j],
                    device_id=(x, y, 1 - c), device_id_type=MESH))
        return out
    return plan


def scatter_plan(n):
    def plan(arrs, send_sems, recv_sems, receiving):
        x, y, c = _place()
        out = []
        for i in range(n):
            for j, (cx, cy) in enumerate(_other_chips(x, y)):
                out.append(pltpu.make_async_remote_copy(
                    src_ref=arrs[i].at[2 * cx + cy], dst_ref=arrs[n + i].at[j], send_sem=send_sems.at[3 * i + j],
                    recv_sem=recv_sems.at[3 * i + j], device_id=(cx, cy, c), device_id_type=MESH))
        return out
    return plan


def exchange_plan(axes):
    n = len(axes)

    def plan(arrs, send_sems, recv_sems, receiving):
        x, y, c = _place()
        out = []
        for i in range(n):
            for k in range(N_CHIPS):
                out.append(pltpu.make_async_remote_copy(
                    src_ref=_half(arrs[i], (k,), 1 - c, axes[i]), dst_ref=arrs[n + i].at[k],
                    send_sem=send_sems.at[N_CHIPS * i + k], recv_sem=recv_sems.at[N_CHIPS * i + k],
                    device_id=(x, y, 1 - c), device_id_type=MESH))
        return out
    return plan


def devices_plan(arrs, send_sems, recv_sems, receiving):
    x, y, c = _place()
    out = []
    for k in range(1, N_DEV):
        px, py, pc = x ^ ((k >> 2) & 1), y ^ ((k >> 1) & 1), c ^ (k & 1)
        slot = (4 * px + 2 * py + pc) if receiving else (4 * x + 2 * y + c)
        out.append(pltpu.make_async_remote_copy(
            src_ref=arrs[0], dst_ref=arrs[1].at[slot], send_sem=send_sems.at[k - 1], recv_sem=recv_sems.at[k - 1],
            device_id=(px, py, pc), device_id_type=MESH))
    return out


def share_halves(bufs, axes, after, *, name):
    n = len(bufs)

    def body(*refs):
        outs = refs[n + 1:2 * n + 1]
        send_sems, recv_sems = refs[2 * n + 1:]
        x, y, c = _place()
        sib = (x, y, 1 - c)
        cps = []
        for i in range(n):
            mine = _half(outs[i], (), c, axes[i])
            cp = pltpu.make_async_remote_copy(src_ref=mine, dst_ref=mine, send_sem=send_sems.at[i],
                                              recv_sem=recv_sems.at[i], device_id=sib, device_id_type=MESH)
            cp.start()
            cps.append(cp)
        for i in range(n):
            theirs = _half(outs[i], (), 1 - c, axes[i])
            pltpu.make_async_remote_copy(src_ref=theirs, dst_ref=theirs, send_sem=send_sems.at[i],
                                         recv_sem=recv_sems.at[i], device_id=sib, device_id_type=MESH).wait_recv()
        for cp in cps:
            cp.wait_send()

    anyspec = pl.BlockSpec(memory_space=pl.ANY)
    return pl.pallas_call(
        body, name=name,
        out_shape=[jax.ShapeDtypeStruct(b.shape, b.dtype) for b in bufs],
        in_specs=[anyspec] * (n + 1), out_specs=[anyspec] * n,
        input_output_aliases={i: i for i in range(n)},
        scratch_shapes=[pltpu.SemaphoreType.DMA((n,)), pltpu.SemaphoreType.DMA((n,))],
    )(*bufs, after)


def _row_tile(r, cols, itemsize, target_bytes=2 * 1024 * 1024):
    best = None
    for t in range(16, r + 1, 16):
        if r % t == 0 and t * cols * itemsize <= target_bytes:
            best = t
    return best if best is not None else r


def pair_sum(g, recv, half, axis, *, name):
    _, hr, hc = recv.shape
    tr = _row_tile(hr, hc, 4)
    nb = hr // tr
    if axis == 0:
        mine = pl.BlockSpec((None, tr, hc), lambda k, r, hf: (k, hf[0] * nb + r, 0))
    else:
        mine = pl.BlockSpec((None, tr, hc), lambda k, r, hf: (k, r, hf[0]))

    def body(half_ref, a_ref, b_ref, o_ref):
        o_ref[...] = (a_ref[...].astype(F32) + b_ref[...].astype(F32)).astype(BF)

    return pl.pallas_call(
        body, name=name,
        grid_spec=pltpu.PrefetchScalarGridSpec(
            num_scalar_prefetch=1, grid=(N_CHIPS, nb),
            in_specs=[mine, pl.BlockSpec((None, tr, hc), lambda k, r, hf: (k, r, 0))],
            out_specs=pl.BlockSpec((None, tr, hc), lambda k, r, hf: (k, r, 0))),
        out_shape=jax.ShapeDtypeStruct((N_CHIPS, hr, hc), BF),
        compiler_params=_cparams(("parallel", "parallel")),
    )(half, g, recv)


def chip_sum(s, recv, place, axis, *, name):
    _, hr, hc = s.shape
    tr = _row_tile(hr, hc, 4)
    nb = hr // tr
    if axis == 0:
        o_spec = pl.BlockSpec((tr, hc), lambda r, pc: (pc[1] * nb + r, 0))
        o_shape = (2 * hr, hc)
    else:
        o_spec = pl.BlockSpec((tr, hc), lambda r, pc: (r, pc[1]))
        o_shape = (hr, 2 * hc)

    def body(place_ref, a_ref, b_ref, o_ref):
        acc = a_ref[...].astype(F32)
        for j in range(3):
            acc = acc + b_ref[j].astype(F32)
        o_ref[...] = acc

    return pl.pallas_call(
        body, name=name,
        grid_spec=pltpu.PrefetchScalarGridSpec(
            num_scalar_prefetch=1, grid=(nb,),
            in_specs=[pl.BlockSpec((None, tr, hc), lambda r, pc: (pc[0], r, 0)),
                      pl.BlockSpec((3, tr, hc), lambda r, pc: (0, r, 0))],
            out_specs=o_spec),
        out_shape=jax.ShapeDtypeStruct(o_shape, F32),
        compiler_params=_cparams(("parallel",)),
    )(place, s, recv)


def sum_devices(v, *, name):
    _, R, C = v.shape

    def body(v_ref, o_ref):
        acc = v_ref[0]
        for k in range(1, N_DEV):
            acc = acc + v_ref[k]
        o_ref[...] = acc

    return pl.pallas_call(body, name=name, out_shape=jax.ShapeDtypeStruct((R, C), F32))(v)


def adamw(w, g, m, v, *, name):
    R, C = w.shape
    tr = _row_tile(R, C, 4, target_bytes=1024 * 1024)
    c1 = 1.0 / (1.0 - ADAM_B1 ** ADAM_STEP)
    c2 = 1.0 / (1.0 - ADAM_B2 ** ADAM_STEP)

    def body(w_ref, g_ref, m_ref, v_ref, go_ref, d_ref, nm_ref, nv_ref):
        gg = g_ref[...]
        go_ref[...] = gg
        nm = ADAM_B1 * m_ref[...] + (1.0 - ADAM_B1) * gg
        nv = ADAM_B2 * v_ref[...] + (1.0 - ADAM_B2) * (gg * gg)
        nm_ref[...] = nm
        nv_ref[...] = nv
        d_ref[...] = -ADAM_LR * ((nm * c1) / (jnp.sqrt(nv * c2) + ADAM_EPS) + ADAM_WD * w_ref[...])

    spec = pl.BlockSpec((tr, C), lambda i: (i, 0))
    return pl.pallas_call(
        body, name=name, grid=(R // tr,), in_specs=[spec] * 4, out_specs=[spec] * 4,
        out_shape=[jax.ShapeDtypeStruct((R, C), F32)] * 4, compiler_params=_cparams(("parallel",)),
    )(w, g, m, v)


def _rope_tables(T):
    pos = np.arange(T, dtype=np.float32)
    inv_freq = (ROPE_THETA ** (-np.arange(0, QK_ROPE, 2, dtype=np.float32) / QK_ROPE)).astype(np.float32)
    ang = pos[:, None] * inv_freq[None, :]
    z = np.zeros((T, 64), np.float32)
    cos = np.concatenate([np.cos(ang), np.cos(ang), z], axis=1).astype(np.float32)
    sin = np.concatenate([np.sin(ang), np.sin(ang), z], axis=1).astype(np.float32)
    return jnp.asarray(cos), jnp.asarray(sin)


def _padded_rows(length):
    T = -(-length // 16) * 16
    while True:
        for n in (4, 5, 6, 7, 8):
            if T % (16 * n) == 0:
                return T, n
        T += 16


def _pick(n, cands):
    for c in cands:
        if n % c == 0:
            return c
    return n


def kernel(x, meta_tokens, g_mix, w_in, b_glu, b_gate, w_dw, b_dw, g_conv_ln, b_conv_ln, w_conv_out, b_conv_out, g_q_lora, w_uq, g_kv_lora, w_uk, w_uv, w_attn_out, w_out, g_ffn, w_ffn_gate, w_ffn_up, w_ffn_down, g_final, loss_target, m_meta_tokens, m_g_mix, m_w_in, m_b_glu, m_b_gate, m_w_dw, m_b_dw, m_g_conv_ln, m_b_conv_ln, m_w_conv_out, m_b_conv_out, m_g_q_lora, m_w_uq, m_g_kv_lora, m_w_uk, m_w_uv, m_w_attn_out, m_w_out, m_g_ffn, m_w_ffn_gate, m_w_ffn_up, m_w_ffn_down, m_g_final, v_meta_tokens, v_g_mix, v_w_in, v_b_glu, v_b_gate, v_w_dw, v_b_dw, v_g_conv_ln, v_b_conv_ln, v_w_conv_out, v_b_conv_out, v_g_q_lora, v_w_uq, v_g_kv_lora, v_w_uk, v_w_uv, v_w_attn_out, v_w_out, v_g_ffn, v_w_ffn_gate, v_w_ffn_up, v_w_ffn_down, v_g_final):
    weights = dict(meta_tokens=meta_tokens, g_mix=g_mix, w_in=w_in, b_glu=b_glu, b_gate=b_gate, w_dw=w_dw, b_dw=b_dw,
                   g_conv_ln=g_conv_ln, b_conv_ln=b_conv_ln, w_conv_out=w_conv_out, b_conv_out=b_conv_out,
                   g_q_lora=g_q_lora, w_uq=w_uq, g_kv_lora=g_kv_lora, w_uk=w_uk, w_uv=w_uv, w_attn_out=w_attn_out,
                   w_out=w_out, g_ffn=g_ffn, w_ffn_gate=w_ffn_gate, w_ffn_up=w_ffn_up, w_ffn_down=w_ffn_down,
                   g_final=g_final)
    m_in = dict(meta_tokens=m_meta_tokens, g_mix=m_g_mix, w_in=m_w_in, b_glu=m_b_glu, b_gate=m_b_gate, w_dw=m_w_dw,
                b_dw=m_b_dw, g_conv_ln=m_g_conv_ln, b_conv_ln=m_b_conv_ln, w_conv_out=m_w_conv_out,
                b_conv_out=m_b_conv_out, g_q_lora=m_g_q_lora, w_uq=m_w_uq, g_kv_lora=m_g_kv_lora, w_uk=m_w_uk,
                w_uv=m_w_uv, w_attn_out=m_w_attn_out, w_out=m_w_out, g_ffn=m_g_ffn, w_ffn_gate=m_w_ffn_gate,
                w_ffn_up=m_w_ffn_up, w_ffn_down=m_w_ffn_down, g_final=m_g_final)
    v_in = dict(meta_tokens=v_meta_tokens, g_mix=v_g_mix, w_in=v_w_in, b_glu=v_b_glu, b_gate=v_b_gate, w_dw=v_w_dw,
                b_dw=v_b_dw, g_conv_ln=v_g_conv_ln, b_conv_ln=v_b_conv_ln, w_conv_out=v_w_conv_out,
                b_conv_out=v_b_conv_out, g_q_lora=v_g_q_lora, w_uq=v_w_uq, g_kv_lora=v_g_kv_lora, w_uk=v_w_uk,
                w_uv=v_w_uv, w_attn_out=v_w_attn_out, w_out=v_w_out, g_ffn=v_g_ffn, w_ffn_gate=v_w_ffn_gate,
                w_ffn_up=v_w_ffn_up, w_ffn_down=v_w_ffn_down, g_final=v_g_final)
    names = list(weights)

    seq, D = x.shape[1], x.shape[2]
    C = w_conv_out.shape[2]
    R = g_q_lora.shape[1]
    H = w_uv.shape[2] * N_CHIPS // V_HEAD
    FB = w_ffn_gate.shape[2]
    FF = FB * N_CHIPS
    length = N_META + seq
    T, nch = _padded_rows(length)
    unit = T // nch
    scale = (QK_NOPE + QK_ROPE) ** -0.5
    assert C == D and w_in.shape[2] * N_CHIPS == 2 * C + 2 * R + QK_ROPE + 2 * D

    xi, yi, ci = _place()
    chip = (2 * xi + yi).astype(jnp.int32)
    half_arr = ci.astype(jnp.int32).reshape(1)

    tr = unit
    tmh = T // 2 if (T // 2) % 16 == 0 else unit
    tmq = unit
    tmw = _pick(D, (1024, 512))
    rc = unit // 2 if unit % 16 == 0 else unit
    tcv = 128

    def tn_of(n, pref=512):
        return _pick(n, (pref, 384, 256, 128))

    grp_a = ["w_in"]
    grp_b = ["w_conv_out", "w_uq", "w_uk", "w_uv", "w_attn_out", "w_out"]
    grp_c = ["w_ffn_gate", "w_ffn_up", "w_ffn_down"]
    big = grp_a + grp_b + grp_c
    axes = {nm: 0 for nm in big}
    axes["w_in"] = 1
    shard = {nm: weights[nm][0] for nm in big}
    shard["w_in"] = jnp.swapaxes(w_in[0], 0, 1)

    def own_blocks(srcs, dtype=BF):
        bufs = []
        for w in srcs:
            wb = w.astype(dtype)
            bufs.append(lax.dynamic_update_slice(lax.empty((N_CHIPS,) + wb.shape, dtype), wb[None], (chip, 0, 0)))
        return bufs

    def gather_begin(nms, bufs, after, tag):
        ax = [axes[nm] for nm in nms]
        bufs, ssem, rsem, tok = copies_start(bufs, gather_plan(ax), 3 * len(nms), after, name=f"gather_{tag}_start")
        return (nms, ax, bufs, ssem, rsem), tok

    def gather_middle(state, after, tag):
        nms, ax, bufs, ssem, rsem = state
        bufs = copies_wait(bufs, gather_plan(ax), ssem, rsem, after, name=f"gather_{tag}_wait")
        bufs, ssem, rsem, tok = copies_start(bufs, forward_plan(ax), 3 * len(nms), half_arr,
                                             name=f"gather_{tag}_forward_start")
        return (nms, ax, bufs, ssem, rsem), tok

    def gather_end(state, after, tag):
        nms, ax, bufs, ssem, rsem = state
        return dict(zip(nms, copies_wait(bufs, forward_plan(ax), ssem, rsem, after, name=f"gather_{tag}_forward_wait")))

    tiny = jnp.concatenate([meta_tokens, jnp.pad(w_dw[0], ((0, 1), (0, 0)))], axis=0)
    ax_a = [axes[nm] for nm in grp_a] + [0]
    bufs_a, ss_a, rs_a, tok_ga = copies_start(own_blocks([shard[nm] for nm in grp_a]) + own_blocks([tiny], F32),
                                              near_plan(ax_a), 2 * len(ax_a), half_arr, name="gather_a_near_start")
    tok_ga, held = lax.optimization_barrier((tok_ga, (x, loss_target, [shard[nm] for nm in grp_b + grp_c])))
    x_l, tgt_l, shard_bc = held
    h0 = jnp.pad(x_l[0], ((N_META, T - length), (0, 0)))
    tgt = jnp.pad(tgt_l[0], ((N_META, T - length), (0, 0)))
    bufs_b, bufs_c = own_blocks(shard_bc[:len(grp_b)]), own_blocks(shard_bc[len(grp_b):])
    cos, sin = _rope_tables(T)

    bufs_a = copies_wait(bufs_a, near_plan(ax_a), ss_a, rs_a, [h0, tgt] + bufs_b + bufs_c, name="gather_a_near_wait")
    bufs_a, ss_a, rs_a, _ = copies_start(bufs_a, relay_plan(ax_a), 4 * len(ax_a), half_arr,
                                         name="gather_a_relay_start")
    bufs_a = copies_wait(bufs_a, relay_plan(ax_a), ss_a, rs_a, half_arr, name="gather_a_relay_wait")
    bufs_a, ss_a, rs_a, _ = copies_start(bufs_a, far_forward_plan(ax_a), len(ax_a), half_arr,
                                         name="gather_a_far_start")
    bufs_a = copies_wait(bufs_a, far_forward_plan(ax_a), ss_a, rs_a, half_arr, name="gather_a_far_wait")
    G = dict(zip(grp_a, bufs_a))
    meta_full = jnp.concatenate([bufs_a[-1][k, :N_META] for k in range(N_CHIPS)], axis=1)
    w_dw_full = jnp.concatenate([bufs_a[-1][k, N_META:] for k in range(N_CHIPS)], axis=1)
    h0 = lax.dynamic_update_slice(h0, meta_full, (0, 0))
    st_gb, tok_gb = gather_begin(grp_b, bufs_b, G["w_in"], "b")
    ax_c = [axes[nm] for nm in grp_c]
    bufs_c, ss_c, rs_c, tok_gc = copies_start(bufs_c, near_plan(ax_c), 2 * len(grp_c), tok_gb,
                                              name="gather_c_near_start")

    WT_in = G["w_in"].reshape(N_CHIPS * G["w_in"].shape[1], D)
    o1, o2 = 2 * C, 2 * C + 2 * R + QK_ROPE
    WT_small = jnp.pad(WT_in[o1:o2], ((0, 128 - QK_ROPE), (0, 0)))
    WT_gate = WT_in[o2:]

    u = rms_fwd(h0, g_mix, tr=tr, name="rms_mix", deps=[tok_gb, tok_gc])
    z_glu = mm(u, WT_in, mode="nt", tm=T, tn=tn_of(2 * C), tk=D, out_dtype=BF, name="mm_z_glu", n_out=o1)
    z_small = mm(u, WT_small, mode="nt", tm=T, tn=tn_of(2 * R + 128, 384), tk=D, out_dtype=F32, name="mm_z_small")
    z_gate = mm(u, WT_gate, mode="nt", tm=T, tn=tn_of(2 * D), tk=D, out_dtype=BF, name="mm_z_gate")

    c0 = glu_fwd(z_glu, b_glu, tr=tr, name="glu_fwd")
    c1 = dwconv_fwd(c0, w_dw_full, b_dw, tc=tcv, rc=rc, name="dwconv_fwd")
    st_gb, tok_gb = gather_middle(st_gb, c1, "b")
    c3 = ln_silu_fwd(c1, g_conv_ln, b_conv_ln, tr=tr, name="ln_silu_fwd")
    cq, ckv, kr = lora_norm_fwd(z_small, g_q_lora, g_kv_lora, cos, sin, tr=tr, name="lora_norm_fwd", deps=[tok_gb])

    def cols_full(a):
        return jnp.concatenate([a[k] for k in range(N_CHIPS)], axis=1)

    G.update(gather_end(st_gb, cq, "b"))
    W_co = G["w_conv_out"].reshape(C, D)
    W_uq = cols_full(G["w_uq"]).reshape(R, H, QK_NOPE + QK_ROPE)
    Wq_p = jnp.pad(W_uq, ((0, 0), (0, 0), (0, HEAD_PAD - QK_NOPE - QK_ROPE))).reshape(R, H * HEAD_PAD)
    W_uk = cols_full(G["w_uk"])
    W_uv = cols_full(G["w_uv"])
    W_ao = G["w_attn_out"].reshape(H * V_HEAD, D)
    W_o = G["w_out"].reshape(D, D)

    y_conv = mm(c3, W_co, mode="nn", tm=T, tn=tn_of(D), tk=C, out_dtype=BF, name="mm_y_conv")

    q = q_proj(cq, Wq_p, cos, sin, scale=scale, name="q_proj")
    kk = k_proj(ckv, W_uk, kr, name="k_proj")
    vv = mm(ckv, W_uv, mode="nn", tm=T, tn=tn_of(H * V_HEAD), tk=R, out_dtype=BF, name="mm_v")
    o_attn, lse = attn_fwd(q, kk, vv, nch=nch, name="attn_fwd")
    bufs_c = copies_wait(bufs_c, near_plan(ax_c), ss_c, rs_c, o_attn, name="gather_c_near_wait")
    bufs_c, ss_c, rs_c, tok_gc = copies_start(bufs_c, relay_plan(ax_c), 4 * len(grp_c), half_arr,
                                              name="gather_c_relay_start")
    y_attn = mm(o_attn, W_ao, mode="nn", tm=T, tn=tn_of(D), tk=H * V_HEAD, out_dtype=BF, name="mm_y_attn",
                deps=[tok_gc])

    mix = mix_fwd(z_gate, y_conv, y_attn, b_gate, b_conv_out, tr=tr, name="mix_fwd")
    bufs_c = copies_wait(bufs_c, relay_plan(ax_c), ss_c, rs_c, mix, name="gather_c_relay_wait")
    bufs_c, ss_c, rs_c, tok_gc = copies_start(bufs_c, far_forward_plan(ax_c), len(grp_c), half_arr,
                                              name="gather_c_far_start")
    h1 = mm(mix, W_o, mode="nn", tm=T, tn=tn_of(D), tk=D, out_dtype=F32, name="mm_h1", res=h0, deps=[tok_gc])

    hn = rms_fwd(h1, g_ffn, tr=tr, name="rms_ffn")

    G.update(zip(grp_c, copies_wait(bufs_c, far_forward_plan(ax_c), ss_c, rs_c, hn, name="gather_c_far_wait")))
    W_fg, W_fu = G["w_ffn_gate"], G["w_ffn_up"]
    W_fd = G["w_ffn_down"].reshape(FF, D)

    fa, fb, f = ffn_fwd(hn, W_fg, W_fu, tm=tmq, name="ffn_fwd")
    h2 = mm(f, W_fd, mode="nn", tm=tmh, tn=tn_of(D, 256), tk=FF, out_dtype=F32, name="mm_h2", res=h1)

    place_arr = jnp.stack([chip, ci.astype(jnp.int32)])

    def col_blocks(a):
        r, cfull = a.shape
        return a.reshape(r, N_CHIPS, cfull // N_CHIPS).transpose(1, 0, 2)

    def rs_begin(nms, gls, after, tag):
        ax = [axes[nm] for nm in nms]
        lands = []
        for g, a in zip(gls, ax):
            _, r, cfull = g.shape
            lands.append(lax.empty((N_CHIPS, r // 2, cfull) if a == 0 else (N_CHIPS, r, cfull // 2), BF))
        arrs, ssem, rsem, tok = copies_start(list(gls) + lands, exchange_plan(ax), N_CHIPS * len(nms), after,
                                             name=f"rs_{tag}_exchange_start")
        return (nms, ax, arrs, ssem, rsem), tok

    def rs_middle(state, after, tag):
        nms, ax, arrs, ssem, rsem = state
        n = len(nms)
        arrs = copies_wait(arrs, exchange_plan(ax), ssem, rsem, after, name=f"rs_{tag}_exchange_wait")
        pair = [pair_sum(g, r, half_arr, a, name="rs_pair_sum_" + nm)
                for nm, g, r, a in zip(nms, arrs[:n], arrs[n:], ax)]
        lands = [lax.empty((3,) + p.shape[1:], BF) for p in pair]
        arrs, ssem, rsem, tok = copies_start(pair + lands, scatter_plan(n), 3 * n, half_arr,
                                             name=f"rs_{tag}_scatter_start")
        return (nms, ax, arrs, ssem, rsem), tok

    def rs_end(state, after, tag):
        nms, ax, arrs, ssem, rsem = state
        n = len(nms)
        arrs = copies_wait(arrs, scatter_plan(n), ssem, rsem, after, name=f"rs_{tag}_scatter_wait")
        return [chip_sum(s, r, place_arr, a, name="rs_chip_sum_" + nm)
                for nm, s, r, a in zip(nms, arrs[:n], arrs[n:], ax)]

    d_h2, d_h2b, loss_p, d_g_final = final_loss(h2, tgt, g_final.reshape(1, D), seq=seq, tr=tr, name="final_loss")
    loss = lax.psum(loss_p[0, 0], ("x", "y", "c"))

    dW_fd = mm(f, d_h2b, mode="tn", tm=FB, tn=tn_of(D), tk=T, out_dtype=BF, name="mm_dw_ffn_down")
    d_a, d_b = ffn_dact(d_h2b, W_fd, fa, fb, tm=tmq, name="ffn_dact")
    dW_fg = mm(hn, d_a, mode="tn", tm=tmw, tn=FB, tk=T, out_dtype=BF, name="mm_dw_ffn_gate", out_blocks=N_CHIPS)
    dW_fu = mm(hn, d_b, mode="tn", tm=tmw, tn=FB, tk=T, out_dtype=BF, name="mm_dw_ffn_up", out_blocks=N_CHIPS)
    st_c, tok_rc = rs_begin(grp_c, [dW_fg, dW_fu, dW_fd.reshape(N_CHIPS, FB, D)], half_arr, "c")
    d_hn = ffn_dhn(d_a, d_b, W_fg, W_fu, tm=tmq, tn=_pick(D, (256,)), name="ffn_dhn", deps=[tok_rc])
    st_c, tok_rc = rs_middle(st_c, d_hn, "c")
    d_h1, d_h1b, d_g_ffn = rms_bwd(d_hn, h1, g_ffn, d_h2, tr=tr, name="rms_ffn_bwd", deps=[tok_rc])

    d_mix = mm(d_h1b, W_o, mode="nt", tm=T, tn=tn_of(D), tk=D, out_dtype=BF, name="mm_d_mix")
    dW_o = mm(mix, d_h1b, mode="tn", tm=tmw, tn=tn_of(D), tk=T, out_dtype=BF, name="mm_dw_out")
    d_yc, d_ya, d_zgate, d_b_gate, d_b_co = mix_bwd(z_gate, y_conv, y_attn, d_mix, b_gate, b_conv_out, tr=tr,
                                                    name="mix_bwd")

    dW_co = mm(c3, d_yc, mode="tn", tm=tmw, tn=tn_of(D), tk=T, out_dtype=BF, name="mm_dw_conv_out")
    d_c3 = mm(d_yc, W_co, mode="nt", tm=T, tn=tn_of(C), tk=D, out_dtype=BF, name="mm_d_c3")
    d_c1, d_g_ln, d_b_ln, d_b_dw = ln_silu_bwd(c1, d_c3, g_conv_ln, b_conv_ln, tr=tr, name="ln_silu_bwd")
    d_c0, d_w_dw = dwconv_bwd(c0, d_c1, w_dw_full, tc=tcv, rc=rc, name="dwconv_bwd")
    d_zglu, d_b_glu = glu_bwd(z_glu, b_glu, d_c0, tr=tr, name="glu_bwd")

    dW_ao = mm(o_attn, d_ya, mode="tn", tm=_pick(H * V_HEAD, (1024, 512)), tn=tn_of(D), tk=T, out_dtype=BF, name="mm_dw_attn_out")
    d_o = mm(d_ya, W_ao, mode="nt", tm=T, tn=tn_of(H * V_HEAD), tk=D, out_dtype=BF, name="mm_d_o")
    d_q2, d_k2, d_v, d_kr = attn_bwd(q, kk, vv, o_attn, d_o, lse, cos, sin, scale=scale, nch=nch, name="attn_bwd")

    dWq_p = mm(cq, d_q2, mode="tn", tm=R, tn=tn_of(H * HEAD_PAD), tk=T, out_dtype=BF, name="mm_dw_uq")
    d_cq = mm(d_q2, Wq_p, mode="nt", tm=T, tn=R, tk=_pick(H * HEAD_PAD, (1024,)), out_dtype=F32, name="mm_d_cq")
    Wk_p = jnp.pad(W_uk.reshape(R, H, QK_NOPE), ((0, 0), (0, 0), (0, HEAD_PAD - QK_NOPE))).reshape(R, H * HEAD_PAD)
    dWk_p = mm(ckv, d_k2, mode="tn", tm=R, tn=tn_of(H * HEAD_PAD), tk=T, out_dtype=BF, name="mm_dw_uk")
    dW_uv = mm(ckv, d_v, mode="tn", tm=R, tn=tn_of(H * V_HEAD), tk=T, out_dtype=BF, name="mm_dw_uv")
    red_c = rs_end(st_c, [dWq_p, dWk_p, dW_uv, dW_co, dW_ao, dW_o], "c")
    dW_uq = dWq_p.reshape(R, H, HEAD_PAD)[:, :, :QK_NOPE + QK_ROPE].reshape(R, H * (QK_NOPE + QK_ROPE))
    dW_uk = dWk_p.reshape(R, H, HEAD_PAD)[:, :, :QK_NOPE].reshape(R, H * QK_NOPE)
    st_b, tok_rb = rs_begin(grp_b, [dW_co.reshape(N_CHIPS, C // N_CHIPS, D), col_blocks(dW_uq), col_blocks(dW_uk),
                                    col_blocks(dW_uv), dW_ao.reshape(N_CHIPS, H * V_HEAD // N_CHIPS, D),
                                    dW_o.reshape(N_CHIPS, D // N_CHIPS, D)], red_c[-1], "b")
    d_ckv = mm(d_k2, Wk_p, mode="nt", tm=T, tn=R, tk=_pick(H * HEAD_PAD, (1024,)), out_dtype=F32, name="mm_d_ckv_k",
               deps=[tok_rb])
    d_ckv = mm(d_v, W_uv, mode="nt", tm=T, tn=R, tk=_pick(H * V_HEAD, (1024,)), out_dtype=F32, name="mm_d_ckv_v",
               res=d_ckv)
    d_zsmall, d_g_q, d_g_kv = lora_norm_bwd(z_small, d_cq, d_ckv, d_kr, cos, sin, g_q_lora, g_kv_lora, tr=tr,
                                            name="lora_norm_bwd")
    st_b, tok_rb = rs_middle(st_b, d_zsmall, "b")

    tmi = _pick(2 * C, (1024, 512))
    dWT_in = lax.empty((WT_in.shape[0], D), BF)
    dWT_in = mm_tn_into(d_zglu, u, dWT_in, 0, tm=tmi, tn=tn_of(D), name="mm_dw_glu", deps=[tok_rb])
    dWT_in = mm_tn_into(d_zsmall, u, dWT_in, o1, tm=2 * R + 128, tn=tn_of(D), name="mm_dw_small")
    dWT_in = mm_tn_into(d_zgate, u, dWT_in, o2, tm=tmi, tn=tn_of(D), name="mm_dw_gate")
    red_b = rs_end(st_b, dWT_in, "b")
    st_a, tok_ra = rs_begin(grp_a, [dWT_in.reshape(N_CHIPS, dWT_in.shape[0] // N_CHIPS, D)], red_b[-1], "a")
    d_u = mm(d_zglu, WT_in, mode="nn", tm=T, tn=tn_of(D), tk=_pick(2 * C, (2048, 1024)), out_dtype=F32, name="mm_d_u_glu",
             deps=[tok_ra])
    st_a, tok_ra = rs_middle(st_a, d_u, "a")
    d_u = mm(d_zsmall, WT_small, mode="nn", tm=T, tn=tn_of(D), tk=2 * R + 128, out_dtype=F32, name="mm_d_u_small",
             res=d_u, deps=[tok_ra])
    d_u = mm(d_zgate, WT_gate, mode="nn", tm=T, tn=tn_of(D), tk=_pick(2 * D, (2048, 1024)), out_dtype=F32,
             name="mm_d_u_gate", res=d_u)
    d_h0, _, d_g_mix = rms_bwd(d_u, h0, g_mix, d_h1, tr=tr, name="rms_mix_bwd")

    grad_x = d_h0[N_META:length][None]

    small = ["g_mix", "b_glu", "b_gate", "b_dw", "g_conv_ln", "b_conv_ln", "b_conv_out", "g_q_lora", "g_kv_lora",
             "g_ffn", "g_final", "w_dw", "meta_tokens"]
    sgrads = dict(g_mix=d_g_mix, b_glu=d_b_glu, b_gate=d_b_gate, b_dw=d_b_dw, g_conv_ln=d_g_ln, b_conv_ln=d_b_ln,
                  b_conv_out=d_b_co, g_q_lora=d_g_q, g_kv_lora=d_g_kv, g_ffn=d_g_ffn, g_final=d_g_final,
                  w_dw=d_w_dw[:CONV_WIDTH], meta_tokens=d_h0[:N_META])
    sizes = [int(np.prod(sgrads[nm].shape)) for nm in small]
    packed = jnp.concatenate([sgrads[nm].reshape(-1) for nm in small]).reshape(-1, 128)
    sm_arrs, sm_ss, sm_rs, tok_sm = copies_start([packed, lax.empty((N_DEV,) + packed.shape, F32)], devices_plan,
                                                 N_DEV - 1, packed, name="gather_small_start")

    grads, delta, new_m, new_v, delta_2d = {}, {}, {}, {}, {}

    def adamw_2d(nm, w2, g2, m2, v2, back):
        g_, d_, m_, v_ = adamw(w2, g2, m2, v2, name="adamw_" + nm)
        delta_2d[nm] = d_
        grads[nm], delta[nm], new_m[nm], new_v[nm] = back(g_), back(d_), back(m_), back(v_)

    reduced = dict(zip(grp_b + grp_c, share_halves(red_b + red_c, [axes[nm] for nm in grp_b + grp_c], tok_sm,
                                                   name="rs_share_halves_bc")))
    for nm in grp_b + grp_c + grp_a:
        if nm == grp_a[0]:
            red_a = rs_end(st_a, delta_2d[grp_c[-1]], "a")
            reduced.update(zip(grp_a, share_halves(red_a, [axes[nm] for nm in grp_a], tok_sm,
                                                   name="rs_share_halves_a")))
        shp = weights[nm].shape
        if axes[nm] == 1:
            adamw_2d(nm, shard[nm], reduced[nm], jnp.swapaxes(m_in[nm][0], 0, 1), jnp.swapaxes(v_in[nm][0], 0, 1),
                     lambda a: jnp.swapaxes(a, 0, 1)[None])
        else:
            two = (shp[-2], shp[-1])
            adamw_2d(nm, weights[nm].reshape(two), reduced[nm], m_in[nm].reshape(two), v_in[nm].reshape(two),
                     lambda a, shp=shp: a.reshape(shp))

    sm_arrs = copies_wait(sm_arrs, devices_plan, sm_ss, sm_rs, delta_2d[grp_a[0]], name="gather_small_wait")
    me = (4 * xi + 2 * yi + ci).astype(jnp.int32)
    parts = lax.dynamic_update_slice(sm_arrs[1], sm_arrs[0][None], (me, 0, 0))
    summed = sum_devices(parts, name="sum_small_grads").reshape(-1)
    offs = np.concatenate([[0], np.cumsum(sizes)])
    sfull = {nm: summed[int(offs[i]):int(offs[i + 1])].reshape(sgrads[nm].shape) for i, nm in enumerate(small)}
    for nm in small:
        gfull = sfull[nm]
        if nm == "w_dw":
            cb = C // N_CHIPS
            grads[nm] = lax.dynamic_slice(gfull, (0, chip * cb), (CONV_WIDTH, cb))[None]
        elif nm == "meta_tokens":
            cb = D // N_CHIPS
            grads[nm] = lax.dynamic_slice(gfull, (0, chip * cb), (N_META, cb))
        else:
            grads[nm] = gfull.reshape(weights[nm].shape)

    rep = [nm for nm in small if nm not in ("w_dw", "meta_tokens")]

    def pack(d):
        return jnp.concatenate([d[nm].reshape(-1) for nm in rep]).reshape(-1, 128)

    _, pd, pm, pv = adamw(pack(weights), pack(grads), pack(m_in), pack(v_in), name="adamw_small")
    rsz = [int(np.prod(weights[nm].shape)) for nm in rep]
    roff = np.concatenate([[0], np.cumsum(rsz)])
    for i, nm in enumerate(rep):
        sl = slice(int(roff[i]), int(roff[i + 1]))
        delta[nm] = pd.reshape(-1)[sl].reshape(weights[nm].shape)
        new_m[nm] = pm.reshape(-1)[sl].reshape(weights[nm].shape)
        new_v[nm] = pv.reshape(-1)[sl].reshape(weights[nm].shape)
    for nm in ["w_dw", "meta_tokens"]:
        shp = weights[nm].shape
        two = (shp[-2], shp[-1])
        adamw_2d(nm, weights[nm].reshape(two), grads[nm].reshape(two), m_in[nm].reshape(two), v_in[nm].reshape(two),
                 lambda a, shp=shp: a.reshape(shp))

    return (loss, grad_x, *[grads[nm] for nm in names], *[delta[nm] for nm in names],
            *[new_m[nm] for nm in names], *[new_v[nm] for nm in names])
```

```python
import functools

import numpy as np
import jax
import jax.numpy as jnp
from jax import lax
from jax.experimental import pallas as pl
from jax.experimental.pallas import tpu as pltpu

F32 = jnp.float32
BF = jnp.bfloat16
MESH = pl.DeviceIdType.MESH

N_META = 16
CONV_WIDTH = 31
CONV_PAD = 32
QK_NOPE = 128
QK_ROPE = 64
V_HEAD = 128
HEAD_PAD = 256
ROPE_THETA = 10000.0
EPS = 1e-6
ADAM_LR = 0.001
ADAM_B1 = 0.9
ADAM_B2 = 0.999
ADAM_EPS = 1e-08
ADAM_WD = 0.01
ADAM_STEP = 10
VMEM_LIMIT = 56 * 1024 * 1024
N_CHIPS = 4
N_DEV = 8


def _cparams(sem):
    return pltpu.CompilerParams(dimension_semantics=sem, vmem_limit_bytes=VMEM_LIMIT)


def mm(a, b, *, mode, tm, tn, tk, out_dtype, name, res=None, out_blocks=None, deps=(), n_out=None):
    b3 = b.ndim == 3
    if mode == "nn":
        M, K = a.shape
        N = b.shape[0] * b.shape[2] if b3 else b.shape[1]
        a_spec = pl.BlockSpec((tm, tk), lambda i, j, k: (i, k))
        if b3:
            per = b.shape[2] // tn
            b_spec = pl.BlockSpec((None, tk, tn), lambda i, j, k: (j // per, k, j % per))
        else:
            b_spec = pl.BlockSpec((tk, tn), lambda i, j, k: (k, j))
        dims = (((1,), (0,)), ((), ()))
    elif mode == "nt":
        M, K = a.shape
        N = n_out if n_out is not None else (b.shape[1] if b3 else b.shape[0])
        a_spec = pl.BlockSpec((tm, tk), lambda i, j, k: (i, k))
        if b3:
            per = b.shape[2] // tk
            b_spec = pl.BlockSpec((None, tn, tk), lambda i, j, k: (k // per, j, k % per))
        else:
            b_spec = pl.BlockSpec((tn, tk), lambda i, j, k: (j, k))
        dims = (((1,), (1,)), ((), ()))
    else:
        K, M = a.shape
        N = b.shape[1]
        a_spec = pl.BlockSpec((tk, tm), lambda i, j, k: (k, i))
        b_spec = pl.BlockSpec((tk, tn), lambda i, j, k: (k, j))
        dims = (((0,), (0,)), ((), ()))
    gm, gn, gk = M // tm, N // tn, K // tk
    assert gm * tm == M and gn * tn == N and gk * tk == K, (name, a.shape, b.shape, tm, tn, tk)
    if out_blocks is None:
        o_spec = pl.BlockSpec((tm, tn), lambda i, j, k: (i, j))
        o_shape = (M, N)
    else:
        nbw = N // out_blocks
        per_o = nbw // tn
        assert per_o * tn == nbw
        o_spec = pl.BlockSpec((None, tm, tn), lambda i, j, k: (j // per_o, i, j % per_o))
        o_shape = (out_blocks, M, nbw)
    has_res = res is not None

    def body(*refs):
        a_ref, b_ref = refs[0], refs[1]
        r_ref = refs[2] if has_res else None
        o_ref = refs[2 + has_res + len(deps)]
        acc = refs[-1]
        p = lax.dot_general(a_ref[...], b_ref[...], dims, preferred_element_type=F32)

        def finish(v):
            if has_res:
                v = v + r_ref[...]
            o_ref[...] = v.astype(o_ref.dtype)

        if gk == 1:
            finish(p)
        else:
            k = pl.program_id(2)

            @pl.when(k == 0)
            def _():
                acc[...] = p

            @pl.when(k > 0)
            def _():
                acc[...] += p

            @pl.when(k == gk - 1)
            def _():
                finish(acc[...])

    in_specs = [a_spec, b_spec]
    args = [a, b]
    if has_res:
        in_specs.append(pl.BlockSpec((tm, tn), lambda i, j, k: (i, j)))
        args.append(res)
    in_specs += [pl.BlockSpec(memory_space=pl.ANY)] * len(deps)
    args += list(deps)
    return pl.pallas_call(
        body, name=name, grid=(gm, gn, gk), in_specs=in_specs, out_specs=o_spec,
        out_shape=jax.ShapeDtypeStruct(o_shape, out_dtype),
        scratch_shapes=[pltpu.VMEM((tm, tn), F32)] if gk > 1 else [],
        compiler_params=_cparams(("parallel", "parallel", "arbitrary")),
    )(*args)


def mm_tn_into(a, b, buf, row0, *, tm, tn, name, deps=()):
    K, M = a.shape
    N = b.shape[1]
    gm, gn = M // tm, N // tn
    assert gm * tm == M and gn * tn == N and buf.shape[1] == N and row0 + M <= buf.shape[0]
    steps = gm * gn

    def body(a_ref, b_ref, buf_ref, *rest):
        o_ref, tile, sem = rest[len(deps):]
        i, j = pl.program_id(0), pl.program_id(1)
        s = i * gn + j
        slot = s % 2

        def out_copy(sl):
            return pltpu.make_async_copy(tile.at[sl], o_ref.at[pl.ds(row0 + i * tm, tm), pl.ds(j * tn, tn)],
                                         sem.at[sl])

        @pl.when(s >= 2)
        def _():
            out_copy(slot).wait()

        tile[slot] = lax.dot_general(a_ref[...], b_ref[...], (((0,), (0,)), ((), ())),
                                     preferred_element_type=F32).astype(tile.dtype)
        out_copy(slot).start()

        @pl.when(s == steps - 1)
        def _():
            out_copy(slot).wait()
            if steps >= 2:
                out_copy(1 - slot).wait()

    anyspec = pl.BlockSpec(memory_space=pl.ANY)
    return pl.pallas_call(
        body, name=name, grid=(gm, gn),
        in_specs=[pl.BlockSpec((K, tm), lambda i, j: (0, i)), pl.BlockSpec((K, tn), lambda i, j: (0, j)), anyspec]
        + [anyspec] * len(deps),
        out_specs=anyspec, out_shape=jax.ShapeDtypeStruct(buf.shape, buf.dtype),
        input_output_aliases={2: 0},
        scratch_shapes=[pltpu.VMEM((2, tm, tn), buf.dtype), pltpu.SemaphoreType.DMA((2,))],
        compiler_params=_cparams(("arbitrary", "arbitrary")),
    )(a, b, buf, *deps)


def rowwise(body, row_ins, full_ins, row_outs, acc_outs, *, tr, name, deps=()):
    T = row_ins[0].shape[0]
    assert T % tr == 0, (name, T, tr)
    n_ri, n_fi, n_ro = len(row_ins), len(full_ins), len(row_outs)
    n_in = n_ri + n_fi + len(deps)

    def kern(*refs):
        body(pl.program_id(0), refs[:n_ri], refs[n_ri:n_ri + n_fi], refs[n_in:n_in + n_ro], refs[n_in + n_ro:])

    in_specs = [pl.BlockSpec((tr, a.shape[1]), lambda i: (i, 0)) for a in row_ins]
    in_specs += [pl.BlockSpec(a.shape, lambda i: (0, 0)) for a in full_ins]
    in_specs += [pl.BlockSpec(memory_space=pl.ANY)] * len(deps)
    out_specs = [pl.BlockSpec((tr, c), lambda i: (i, 0)) for c, _ in row_outs]
    out_specs += [pl.BlockSpec(s, lambda i: (0, 0)) for s in acc_outs]
    out_shape = [jax.ShapeDtypeStruct((T, c), d) for c, d in row_outs]
    out_shape += [jax.ShapeDtypeStruct(s, F32) for s in acc_outs]
    return pl.pallas_call(
        kern, name=name, grid=(T // tr,), in_specs=in_specs, out_specs=out_specs, out_shape=out_shape,
        compiler_params=_cparams(("arbitrary",)),
    )(*row_ins, *full_ins, *deps)


def _acc(step, ref, val):
    @pl.when(step == 0)
    def _():
        ref[...] = val

    @pl.when(step > 0)
    def _():
        ref[...] += val


def _colsum(x):
    return jnp.sum(x, axis=0, keepdims=True)


def _sigmoid(x):
    return 1.0 / (1.0 + jnp.exp(-x))


def _rot_half(v):
    lane = lax.broadcasted_iota(jnp.int32, v.shape, 1)
    lo = -pltpu.roll(v, 96, 1)
    hi = pltpu.roll(v, 32, 1)
    return jnp.where(lane < 32, lo, jnp.where(lane < 64, hi, 0.0))


def rms_fwd(h, g, *, tr, name, deps=()):
    def body(step, ri, fi, ro, ao):
        x = ri[0][...]
        r = lax.rsqrt(jnp.mean(x * x, axis=-1, keepdims=True) + EPS)
        ro[0][...] = ((x * r) * fi[0][...]).astype(BF)

    return rowwise(body, [h], [g], [(h.shape[1], BF)], [], tr=tr, name=name, deps=deps)[0]


def rms_bwd(dy, h, g, dres, *, tr, name, deps=()):
    D = h.shape[1]

    def body(step, ri, fi, ro, ao):
        d, x, dr = ri[0][...].astype(F32), ri[1][...], ri[2][...]
        r = lax.rsqrt(jnp.mean(x * x, axis=-1, keepdims=True) + EPS)
        n = x * r
        _acc(step, ao[0], _colsum(d * n))
        dn = d * fi[0][...]
        dh = r * (dn - n * jnp.mean(dn * n, axis=-1, keepdims=True)) + dr
        ro[0][...] = dh
        ro[1][...] = dh.astype(BF)

    return rowwise(body, [dy, h, dres], [g], [(D, F32), (D, BF)], [(1, D)], tr=tr, name=name, deps=deps)


def glu_fwd(z, b, *, tr, name, deps=()):
    C = z.shape[1] // 2

    def body(step, ri, fi, ro, ao):
        zz = ri[0][...].astype(F32) + fi[0][...]
        ro[0][...] = zz[:, :C] * _sigmoid(zz[:, C:])

    return rowwise(body, [z], [b], [(C, F32)], [], tr=tr, name=name, deps=deps)[0]


def glu_bwd(z, b, dc, *, tr, name):
    C = z.shape[1] // 2

    def body(step, ri, fi, ro, ao):
        zz = ri[0][...].astype(F32) + fi[0][...]
        d = ri[1][...]
        za, sg = zz[:, :C], _sigmoid(zz[:, C:])
        dza = d * sg
        dzb = d * za * sg * (1.0 - sg)
        ro[0][:, :C] = dza.astype(BF)
        ro[0][:, C:] = dzb.astype(BF)
        _acc(step, ao[0], _colsum(dza))
        _acc(step, ao[1], _colsum(dzb))

    dz, da, db = rowwise(body, [z, dc], [b], [(2 * C, BF)], [(1, C), (1, C)], tr=tr, name=name)
    return dz, jnp.concatenate([da, db], axis=1)


def ln_silu_fwd(c1, g, b, *, tr, name):
    def body(step, ri, fi, ro, ao):
        x = ri[0][...]
        mu = jnp.mean(x, axis=-1, keepdims=True)
        xc = x - mu
        rstd = lax.rsqrt(jnp.mean(xc * xc, axis=-1, keepdims=True) + EPS)
        y = (xc * rstd) * fi[0][...] + fi[1][...]
        ro[0][...] = (y * _sigmoid(y)).astype(BF)

    return rowwise(body, [c1], [g, b], [(c1.shape[1], BF)], [], tr=tr, name=name)[0]


def ln_silu_bwd(c1, dc3, g, b, *, tr, name):
    C = c1.shape[1]

    def body(step, ri, fi, ro, ao):
        x, d3 = ri[0][...], ri[1][...].astype(F32)
        mu = jnp.mean(x, axis=-1, keepdims=True)
        xc = x - mu
        rstd = lax.rsqrt(jnp.mean(xc * xc, axis=-1, keepdims=True) + EPS)
        n = xc * rstd
        y = n * fi[0][...] + fi[1][...]
        sg = _sigmoid(y)
        dy = d3 * (sg * (1.0 + y * (1.0 - sg)))
        _acc(step, ao[0], _colsum(dy * n))
        _acc(step, ao[1], _colsum(dy))
        dn = dy * fi[0][...]
        dx = rstd * (dn - jnp.mean(dn, axis=-1, keepdims=True) - n * jnp.mean(dn * n, axis=-1, keepdims=True))
        ro[0][...] = dx
        _acc(step, ao[2], _colsum(dx))

    return rowwise(body, [c1, dc3], [g, b], [(C, F32)], [(1, C), (1, C), (1, C)], tr=tr, name=name)


def lora_norm_fwd(zs, gq, gkv, cos, sin, *, tr, name, deps=()):
    R = gq.shape[1]

    def body(step, ri, fi, ro, ao):
        z = ri[0][...]
        for o, gi in ((0, 0), (1, 1)):
            x = z[:, o * R:(o + 1) * R]
            r = lax.rsqrt(jnp.mean(x * x, axis=-1, keepdims=True) + EPS)
            ro[o][...] = ((x * r) * fi[gi][...]).astype(BF)
        kr = z[:, 2 * R:2 * R + 128]
        ro[2][...] = kr * ri[1][...] + _rot_half(kr) * ri[2][...]

    return rowwise(body, [zs, cos, sin], [gq, gkv], [(R, BF), (R, BF), (128, F32)], [], tr=tr, name=name, deps=deps)


def lora_norm_bwd(zs, dcq, dckv, dkr, cos, sin, gq, gkv, *, tr, name):
    R = gq.shape[1]

    def body(step, ri, fi, ro, ao):
        z = ri[0][...]
        for o in (0, 1):
            x = z[:, o * R:(o + 1) * R]
            d = ri[1 + o][...]
            r = lax.rsqrt(jnp.mean(x * x, axis=-1, keepdims=True) + EPS)
            n = x * r
            _acc(step, ao[o], _colsum(d * n))
            dn = d * fi[o][...]
            ro[0][:, o * R:(o + 1) * R] = (r * (dn - n * jnp.mean(dn * n, axis=-1, keepdims=True))).astype(BF)
        dk = ri[3][...]
        ro[0][:, 2 * R:2 * R + 128] = (dk * ri[4][...] - _rot_half(dk * ri[5][...])).astype(BF)

    return rowwise(body, [zs, dcq, dckv, dkr, cos, sin], [gq, gkv], [(2 * R + 128, BF)], [(1, R), (1, R)],
                   tr=tr, name=name)


def mix_fwd(zg, yc, ya, bg, bco, *, tr, name):
    D = yc.shape[1]

    def body(step, ri, fi, ro, ao):
        g = _sigmoid(ri[0][...].astype(F32) + fi[0][...])
        ro[0][...] = (g[:, :D] * (ri[1][...].astype(F32) + fi[1][...]) + g[:, D:] * ri[2][...].astype(F32)).astype(BF)

    return rowwise(body, [zg, yc, ya], [bg, bco], [(D, BF)], [], tr=tr, name=name)[0]


def mix_bwd(zg, yc, ya, dmix, bg, bco, *, tr, name):
    D = yc.shape[1]

    def body(step, ri, fi, ro, ao):
        g = _sigmoid(ri[0][...].astype(F32) + fi[0][...])
        gc, ga = g[:, :D], g[:, D:]
        ycv = ri[1][...].astype(F32) + fi[1][...]
        yav = ri[2][...].astype(F32)
        dm = ri[3][...].astype(F32)
        dyc = dm * gc
        ro[0][...] = dyc.astype(BF)
        ro[1][...] = (dm * ga).astype(BF)
        dzc = dm * ycv * gc * (1.0 - gc)
        dza = dm * yav * ga * (1.0 - ga)
        ro[2][:, :D] = dzc.astype(BF)
        ro[2][:, D:] = dza.astype(BF)
        _acc(step, ao[0], _colsum(dzc))
        _acc(step, ao[1], _colsum(dza))
        _acc(step, ao[2], _colsum(dyc))

    dyc, dya, dzg, dbc, dba, dbo = rowwise(body, [zg, yc, ya, dmix], [bg, bco], [(D, BF), (D, BF), (2 * D, BF)],
                                           [(1, D), (1, D), (1, D)], tr=tr, name=name)
    return dyc, dya, dzg, jnp.concatenate([dbc, dba], axis=1), dbo


def ffn_fwd(hn, wg, wu, *, tm, name):
    T, D = hn.shape
    nb, _, FB = wg.shape
    assert T % tm == 0

    def body(h_ref, g_ref, u_ref, a_ref, b_ref, f_ref):
        h = h_ref[...]
        a = jnp.dot(h, g_ref[...], preferred_element_type=F32)
        b = jnp.dot(h, u_ref[...], preferred_element_type=F32)
        a_ref[...] = a
        b_ref[...] = b
        f_ref[...] = (a * _sigmoid(a) * b).astype(BF)

    wspec = pl.BlockSpec((None, D, FB), lambda j, i: (j, 0, 0))
    ospec = pl.BlockSpec((tm, FB), lambda j, i: (i, j))
    return pl.pallas_call(
        body, name=name, grid=(nb, T // tm),
        in_specs=[pl.BlockSpec((tm, D), lambda j, i: (i, 0)), wspec, wspec], out_specs=[ospec, ospec, ospec],
        out_shape=[jax.ShapeDtypeStruct((T, nb * FB), F32), jax.ShapeDtypeStruct((T, nb * FB), F32),
                   jax.ShapeDtypeStruct((T, nb * FB), BF)],
        compiler_params=_cparams(("parallel", "parallel")),
    )(hn, wg, wu)


def ffn_dact(dh, wd, a, b, *, tm, name):
    T, D = dh.shape
    FF = wd.shape[0]
    FB = FF // N_CHIPS

    def body(dh_ref, w_ref, a_ref, b_ref, da_ref, db_ref):
        d = lax.dot_general(dh_ref[...], w_ref[...], (((1,), (1,)), ((), ())), preferred_element_type=F32)
        x, u = a_ref[...], b_ref[...]
        sg = _sigmoid(x)
        da_ref[...] = (d * u * (sg * (1.0 + x * (1.0 - sg)))).astype(BF)
        db_ref[...] = (d * (x * sg)).astype(BF)

    blk = pl.BlockSpec((tm, FB), lambda j, i: (i, j))
    return pl.pallas_call(
        body, name=name, grid=(N_CHIPS, T // tm),
        in_specs=[pl.BlockSpec((tm, D), lambda j, i: (i, 0)), pl.BlockSpec((FB, D), lambda j, i: (j, 0)), blk, blk],
        out_specs=[blk, blk],
        out_shape=[jax.ShapeDtypeStruct((T, FF), BF), jax.ShapeDtypeStruct((T, FF), BF)],
        compiler_params=_cparams(("parallel", "parallel")),
    )(dh, wd, a, b)


def ffn_dhn(da, db, wg, wu, *, tm, tn, name, deps=()):
    T, FF = da.shape
    nb, D, FB = wg.shape
    dims = (((1,), (1,)), ((), ()))

    def body(da_ref, db_ref, g_ref, u_ref, *rest):
        o_ref = rest[len(deps)]
        acc = None
        for k in range(nb):
            cols = slice(k * FB, (k + 1) * FB)
            p = lax.dot_general(da_ref[:, cols], g_ref[k], dims, preferred_element_type=F32)
            p = p + lax.dot_general(db_ref[:, cols], u_ref[k], dims, preferred_element_type=F32)
            acc = p if acc is None else acc + p
        o_ref[...] = acc.astype(BF)

    aspec = pl.BlockSpec((tm, FF), lambda i, j: (i, 0))
    wspec = pl.BlockSpec((nb, tn, FB), lambda i, j: (0, j, 0))
    return pl.pallas_call(
        body, name=name, grid=(T // tm, D // tn),
        in_specs=[aspec, aspec, wspec, wspec] + [pl.BlockSpec(memory_space=pl.ANY)] * len(deps),
        out_specs=pl.BlockSpec((tm, tn), lambda i, j: (i, j)),
        out_shape=jax.ShapeDtypeStruct((T, D), BF),
        compiler_params=_cparams(("parallel", "parallel")),
    )(da, db, wg, wu, *deps)


def final_loss(h2, tgt, g, *, seq, tr, name):
    D = h2.shape[1]

    def body(step, ri, fi, ro, ao):
        x = ri[0][...]
        r = lax.rsqrt(jnp.mean(x * x, axis=-1, keepdims=True) + EPS)
        n = x * r
        gg = fi[0][...]
        row = lax.broadcasted_iota(jnp.int32, (tr, 1), 0) + step * tr
        live = jnp.logical_and(row >= N_META, row < N_META + seq)
        e = jnp.where(live, n * gg - ri[1][...], 0.0)
        _acc(step, ao[0], jnp.broadcast_to(0.5 * jnp.sum(jnp.mean(e * e, axis=-1, keepdims=True)), (1, 128)))
        dy = e * (1.0 / D)
        _acc(step, ao[1], _colsum(dy * n))
        dn = dy * gg
        dh = r * (dn - n * jnp.mean(dn * n, axis=-1, keepdims=True))
        ro[0][...] = dh
        ro[1][...] = dh.astype(BF)

    return rowwise(body, [h2, tgt], [g], [(D, F32), (D, BF)], [(1, 128), (1, D)], tr=tr, name=name)


def _shifted_copies(s_ref, T, rc):
    for r in range(1, 8):
        for t0 in range(0, T, rc):
            s_ref[r, t0:t0 + rc, :] = s_ref[0, t0 + r:t0 + r + rc, :]
        s_ref[r, T:T + CONV_PAD - 8, :] = s_ref[0, T + r:T + r + CONV_PAD - 8, :]


def _shifted(s_ref, start, rows):
    return s_ref[start % 8, start - start % 8:start - start % 8 + rows, :]


def dwconv_fwd(x, w, b, *, tc, rc, name, deps=()):
    T, C = x.shape
    nchunk = T // rc
    assert nchunk * rc == T and C % tc == 0

    def body(x_ref, w_ref, b_ref, *rest):
        y_ref, xs_ref = rest[len(deps):]
        xs_ref[0, 0:CONV_PAD, :] = jnp.zeros((CONV_PAD, tc), F32)
        xs_ref[0, CONV_PAD:CONV_PAD + T, :] = x_ref[...]
        _shifted_copies(xs_ref, T, rc)
        off = CONV_PAD - (CONV_WIDTH - 1)
        for ci in range(nchunk):
            t0 = ci * rc
            acc = jnp.broadcast_to(b_ref[...], (rc, tc))
            for j in range(CONV_WIDTH):
                acc = acc + w_ref[j:j + 1, :] * _shifted(xs_ref, t0 + off + j, rc)
            y_ref[t0:t0 + rc, :] = acc

    return pl.pallas_call(
        body, name=name, grid=(C // tc,),
        in_specs=[pl.BlockSpec((T, tc), lambda j: (0, j)), pl.BlockSpec((32, tc), lambda j: (0, j)),
                  pl.BlockSpec((1, tc), lambda j: (0, j))] + [pl.BlockSpec(memory_space=pl.ANY)] * len(deps),
        out_specs=pl.BlockSpec((T, tc), lambda j: (0, j)),
        out_shape=jax.ShapeDtypeStruct((T, C), F32),
        scratch_shapes=[pltpu.VMEM((8, T + CONV_PAD, tc), F32)],
        compiler_params=_cparams(("parallel",)),
    )(x, w, b, *deps)


def dwconv_bwd(x, dy, w, *, tc, rc, name):
    T, C = x.shape
    nchunk = T // rc
    assert nchunk * rc == T and C % tc == 0 and rc % 8 == 0

    def body(x_ref, dy_ref, w_ref, dx_ref, dw_ref, xs_ref, ds_ref):
        off = CONV_PAD - (CONV_WIDTH - 1)
        xs_ref[0, 0:CONV_PAD, :] = jnp.zeros((CONV_PAD, tc), F32)
        xs_ref[0, CONV_PAD:CONV_PAD + T, :] = x_ref[...]
        _shifted_copies(xs_ref, T, rc)
        ds_ref[0, 0:T, :] = dy_ref[...]
        ds_ref[0, T:T + CONV_PAD, :] = jnp.zeros((CONV_PAD, tc), F32)
        _shifted_copies(ds_ref, T, rc)
        for ci in range(nchunk):
            t0 = ci * rc
            acc = jnp.zeros((rc, tc), F32)
            for j in range(CONV_WIDTH):
                acc = acc + w_ref[j:j + 1, :] * _shifted(ds_ref, t0 + (CONV_WIDTH - 1) - j, rc)
            dx_ref[t0:t0 + rc, :] = acc
        dw_ref[...] = jnp.zeros((32, tc), F32)
        for j in range(CONV_WIDTH):
            acc = jnp.zeros((8, tc), F32)
            for ci in range(nchunk):
                t0 = ci * rc
                pr = dy_ref[t0:t0 + rc, :] * _shifted(xs_ref, t0 + off + j, rc)
                acc = acc + jnp.sum(pr.reshape(rc // 8, 8, tc), axis=0)
            dw_ref[j:j + 1, :] = jnp.sum(acc, axis=0, keepdims=True)

    return pl.pallas_call(
        body, name=name, grid=(C // tc,),
        in_specs=[pl.BlockSpec((T, tc), lambda j: (0, j)), pl.BlockSpec((T, tc), lambda j: (0, j)),
                  pl.BlockSpec((32, tc), lambda j: (0, j))],
        out_specs=[pl.BlockSpec((T, tc), lambda j: (0, j)), pl.BlockSpec((32, tc), lambda j: (0, j))],
        out_shape=[jax.ShapeDtypeStruct((T, C), F32), jax.ShapeDtypeStruct((32, C), F32)],
        scratch_shapes=[pltpu.VMEM((8, T + CONV_PAD, tc), F32), pltpu.VMEM((8, T + CONV_PAD, tc), F32)],
        compiler_params=_cparams(("parallel",)),
    )(x, dy, w)


def q_proj(cq, wq, cos, sin, *, scale, name):
    T, R = cq.shape
    H = wq.shape[1] // HEAD_PAD

    def body(a_ref, b_ref, c_ref, s_ref, o_ref):
        p = jnp.dot(a_ref[...], b_ref[...], preferred_element_type=F32)
        o_ref[:, :QK_NOPE] = (p[:, :QK_NOPE] * scale).astype(BF)
        x = p[:, QK_NOPE:]
        o_ref[:, QK_NOPE:] = ((x * c_ref[...] + _rot_half(x) * s_ref[...]) * scale).astype(BF)

    return pl.pallas_call(
        body, name=name, grid=(H,),
        in_specs=[pl.BlockSpec((T, R), lambda h: (0, 0)), pl.BlockSpec((R, HEAD_PAD), lambda h: (0, h)),
                  pl.BlockSpec((T, 128), lambda h: (0, 0)), pl.BlockSpec((T, 128), lambda h: (0, 0))],
        out_specs=pl.BlockSpec((T, HEAD_PAD), lambda h: (0, h)),
        out_shape=jax.ShapeDtypeStruct((T, H * HEAD_PAD), BF),
        compiler_params=_cparams(("parallel",)),
    )(cq, wq, cos, sin)


def k_proj(ckv, wk, kr, *, name):
    T, R = ckv.shape
    H = wk.shape[1] // QK_NOPE

    def body(a_ref, b_ref, kr_ref, o_ref):
        p = jnp.dot(a_ref[...], b_ref[...], preferred_element_type=F32)
        o_ref[:, :QK_NOPE] = p.astype(BF)
        o_ref[:, QK_NOPE:] = kr_ref[...].astype(BF)

    return pl.pallas_call(
        body, name=name, grid=(H,),
        in_specs=[pl.BlockSpec((T, R), lambda h: (0, 0)), pl.BlockSpec((R, QK_NOPE), lambda h: (0, h)),
                  pl.BlockSpec((T, 128), lambda h: (0, 0))],
        out_specs=pl.BlockSpec((T, HEAD_PAD), lambda h: (0, h)),
        out_shape=jax.ShapeDtypeStruct((T, H * HEAD_PAD), BF),
        compiler_params=_cparams(("parallel",)),
    )(ckv, wk, kr)


def _causal_scores(q, k_ref, qi, CH, split):
    nt = (((1,), (1,)), ((), ()))
    pieces = []
    if not split:
        rows = slice(0, (qi + 1) * CH)
        s = lax.dot_general(q, k_ref[rows, :], nt, preferred_element_type=F32)
        keep = lax.broadcasted_iota(jnp.int32, s.shape, 1) <= lax.broadcasted_iota(jnp.int32, s.shape, 0) + qi * CH
        return [(rows, jnp.where(keep, s, -1e30))]
    if qi > 0:
        rows = slice(0, qi * CH)
        pieces.append((rows, lax.dot_general(q, k_ref[rows, :], nt, preferred_element_type=F32)))
    rows = slice(qi * CH, (qi + 1) * CH)
    s = lax.dot_general(q, k_ref[rows, :], nt, preferred_element_type=F32)
    keep = lax.broadcasted_iota(jnp.int32, s.shape, 1) <= lax.broadcasted_iota(jnp.int32, s.shape, 0)
    pieces.append((rows, jnp.where(keep, s, -1e30)))
    return pieces


def attn_fwd(q, k, v, *, nch, name):
    T = q.shape[0]
    H = q.shape[1] // HEAD_PAD
    CH = T // nch
    assert CH * nch == T and CH % 16 == 0

    def body(q_ref, k_ref, v_ref, o_ref, lse_ref):
        for qi in range(nch):
            L = (qi + 1) * CH
            pieces = _causal_scores(q_ref[qi * CH:L, :], k_ref, qi, CH, True)
            m = functools.reduce(jnp.maximum, [jnp.max(s, axis=1, keepdims=True) for _, s in pieces])
            l, o = 0.0, 0.0
            for rows, s in pieces:
                p = jnp.exp(s - m)
                l = l + jnp.sum(p, axis=1, keepdims=True)
                o = o + jnp.dot(p.astype(BF), v_ref[rows, :], preferred_element_type=F32)
            o_ref[qi * CH:L, :] = (o / l).astype(BF)
            lse_ref[qi * CH:L, :] = jnp.broadcast_to(m + jnp.log(l), (CH, V_HEAD))

    return pl.pallas_call(
        body, name=name, grid=(H,),
        in_specs=[pl.BlockSpec((T, HEAD_PAD), lambda h: (0, h)), pl.BlockSpec((T, HEAD_PAD), lambda h: (0, h)),
                  pl.BlockSpec((T, V_HEAD), lambda h: (0, h))],
        out_specs=[pl.BlockSpec((T, V_HEAD), lambda h: (0, h)), pl.BlockSpec((T, V_HEAD), lambda h: (0, h))],
        out_shape=[jax.ShapeDtypeStruct((T, H * V_HEAD), BF), jax.ShapeDtypeStruct((T, H * V_HEAD), F32)],
        compiler_params=_cparams(("parallel",)),
    )(q, k, v)


def attn_bwd(q, k, v, o, do, lse, cos, sin, *, scale, nch, name):
    T = q.shape[0]
    H = q.shape[1] // HEAD_PAD
    CH = T // nch
    tn_dims = (((0,), (0,)), ((), ()))
    nt_dims = (((1,), (1,)), ((), ()))

    def body(q_ref, k_ref, v_ref, o_ref, do_ref, lse_ref, c_ref, s_ref,
             dq_ref, dk_ref, dv_ref, dkr_ref, dk_acc, dv_acc):
        h = pl.program_id(0)
        dk_acc[...] = jnp.zeros_like(dk_acc)
        dv_acc[...] = jnp.zeros_like(dv_acc)
        for qi in range(nch):
            L = (qi + 1) * CH
            rows = slice(qi * CH, L)
            qc, doc = q_ref[rows, :], do_ref[rows, :]
            lse_c = lse_ref[rows, 0:1]
            dsum = jnp.sum(doc.astype(F32) * o_ref[rows, :].astype(F32), axis=1, keepdims=True)
            dq = 0.0
            for krows, s in _causal_scores(qc, k_ref, qi, CH, False):
                p = jnp.exp(s - lse_c)
                dp = lax.dot_general(doc, v_ref[krows, :], nt_dims, preferred_element_type=F32)
                ds = (p * (dp - dsum)).astype(BF)
                dq = dq + jnp.dot(ds, k_ref[krows, :], preferred_element_type=F32)
                dk_acc[krows, :] += lax.dot_general(ds, qc, tn_dims, preferred_element_type=F32)
                dv_acc[krows, :] += lax.dot_general(p.astype(BF), doc, tn_dims, preferred_element_type=F32)
            dq_ref[rows, :QK_NOPE] = (dq[:, :QK_NOPE] * scale).astype(BF)
            dyr = dq[:, QK_NOPE:]
            dq_ref[rows, QK_NOPE:] = ((dyr * c_ref[rows, :] - _rot_half(dyr * s_ref[rows, :])) * scale).astype(BF)
        dk_ref[:, :QK_NOPE] = dk_acc[:, :QK_NOPE].astype(BF)
        dk_ref[:, QK_NOPE:] = jnp.zeros((T, HEAD_PAD - QK_NOPE), BF)
        dv_ref[...] = dv_acc[...].astype(BF)

        @pl.when(h == 0)
        def _():
            dkr_ref[...] = dk_acc[:, QK_NOPE:]

        @pl.when(h > 0)
        def _():
            dkr_ref[...] += dk_acc[:, QK_NOPE:]

    hp = pl.BlockSpec((T, HEAD_PAD), lambda h: (0, h))
    hv = pl.BlockSpec((T, V_HEAD), lambda h: (0, h))
    tab = pl.BlockSpec((T, 128), lambda h: (0, 0))
    return pl.pallas_call(
        body, name=name, grid=(H,),
        in_specs=[hp, hp, hv, hv, hv, hv, tab, tab],
        out_specs=[hp, hp, hv, tab],
        out_shape=[jax.ShapeDtypeStruct((T, H * HEAD_PAD), BF), jax.ShapeDtypeStruct((T, H * HEAD_PAD), BF),
                   jax.ShapeDtypeStruct((T, H * V_HEAD), BF), jax.ShapeDtypeStruct((T, 128), F32)],
        scratch_shapes=[pltpu.VMEM((T, HEAD_PAD), F32), pltpu.VMEM((T, V_HEAD), F32)],
        compiler_params=_cparams(("arbitrary",)),
    )(q, k, v, o, do, lse, cos, sin)


def _place():
    x, y, c = lax.axis_index("x"), lax.axis_index("y"), lax.axis_index("c")
    return x, y, c


def _other_chips(x, y):
    return [(1 - x, y), (x, 1 - y), (1 - x, 1 - y)]


HBM_SPEC = pl.BlockSpec(memory_space=pltpu.HBM)
SEM_SPEC = pl.BlockSpec(memory_space=pltpu.SEMAPHORE)
DATAFLOW = pltpu.SideEffectType.DATAFLOW_SIDE_EFFECTING


def _hbm(a):
    return pltpu.with_memory_space_constraint(a, pltpu.HBM)


def _half(ref, lead, c, axis):
    rows, cols = ref.shape[-2], ref.shape[-1]
    if axis == 0:
        return ref.at[(*lead, pl.ds(c * (rows // 2), rows // 2))]
    return ref.at[(*lead, slice(None), pl.ds(c * (cols // 2), cols // 2))]


def copies_start(arrs, plan, n_sems, after, *, name):
    n = len(arrs)

    def body(*refs):
        outs = refs[n + 1:2 * n + 1]
        send_sems, recv_sems, token = refs[2 * n + 1:]
        for cp in plan(outs, send_sems, recv_sems, False):
            cp.start()
        token[...] = jnp.zeros_like(token)

    res = pl.pallas_call(
        body, name=name,
        out_shape=tuple(pltpu.HBM(a.shape, a.dtype) for a in arrs)
        + (pltpu.SemaphoreType.DMA((n_sems,)), pltpu.SemaphoreType.DMA((n_sems,)), jax.ShapeDtypeStruct((8, 128), F32)),
        in_specs=(HBM_SPEC,) * n + (pl.BlockSpec(memory_space=pl.ANY),),
        out_specs=(HBM_SPEC,) * n + (SEM_SPEC, SEM_SPEC, pl.BlockSpec(memory_space=pltpu.VMEM)),
        input_output_aliases={i: i for i in range(n)},
        compiler_params=pltpu.CompilerParams(has_side_effects=DATAFLOW),
    )(*[_hbm(a) for a in arrs], after)
    return list(res[:n]), res[n], res[n + 1], res[n + 2]


def copies_wait(arrs, plan, send_sems, recv_sems, after, *, name):
    n = len(arrs)
    afters = list(after) if isinstance(after, (list, tuple)) else [after]

    def body(*refs):
        for cp in plan(refs[:n], refs[n], refs[n + 1], True):
            cp.wait_send()
            cp.wait_recv()

    res = pl.pallas_call(
        body, name=name,
        out_shape=tuple(pltpu.HBM(a.shape, a.dtype) for a in arrs),
        in_specs=(HBM_SPEC,) * n + (SEM_SPEC, SEM_SPEC) + (pl.BlockSpec(memory_space=pl.ANY),) * len(afters),
        out_specs=(HBM_SPEC,) * n,
        input_output_aliases={i: i for i in range(n)},
        compiler_params=pltpu.CompilerParams(has_side_effects=DATAFLOW),
    )(*arrs, send_sems, recv_sems, *afters)
    return list(res)


def _half_part(ref, lead, c, axis, part):
    rows, cols = ref.shape[-2], ref.shape[-1]
    span = rows // 2 if axis == 0 else rows
    p1 = -(-(span // 2) // 16) * 16
    lo, n = (0, p1) if part == 0 else (p1, span - p1)
    if axis == 0:
        return ref.at[(*lead, pl.ds(c * span + lo, n))]
    return ref.at[(*lead, pl.ds(lo, n), pl.ds(c * (cols // 2), cols // 2))]


def near_plan(axes):
    def plan(bufs, send_sems, recv_sems, receiving):
        x, y, c = _place()
        me = 2 * x + y
        out = []
        for i, b in enumerate(bufs):
            for j, (cx, cy) in enumerate(_other_chips(x, y)[:2]):
                src = _half(b, (me,), c, axes[i])
                dst = _half(b, (2 * cx + cy,), c, axes[i]) if receiving else src
                out.append(pltpu.make_async_remote_copy(
                    src_ref=src, dst_ref=dst, send_sem=send_sems.at[2 * i + j], recv_sem=recv_sems.at[2 * i + j],
                    device_id=(cx, cy, c), device_id_type=MESH))
        return out
    return plan


def relay_plan(axes):
    def plan(bufs, send_sems, recv_sems, receiving):
        x, y, c = _place()
        (xn, yn, dg) = [2 * cx + cy for cx, cy in _other_chips(x, y)]
        to = [(x, 1 - y, c), (1 - x, y, c), (x, y, 1 - c), (x, y, 1 - c)]
        out = []
        for i, b in enumerate(bufs):
            if receiving:
                blks = [_half_part(b, (dg,), c, axes[i], 0), _half_part(b, (dg,), c, axes[i], 1),
                        _half(b, (xn,), 1 - c, axes[i]), _half(b, (yn,), 1 - c, axes[i])]
            else:
                blks = [_half_part(b, (xn,), c, axes[i], 0), _half_part(b, (yn,), c, axes[i], 1),
                        _half(b, (xn,), c, axes[i]), _half(b, (yn,), c, axes[i])]
            for k in range(4):
                out.append(pltpu.make_async_remote_copy(
                    src_ref=blks[k], dst_ref=blks[k], send_sem=send_sems.at[4 * i + k],
                    recv_sem=recv_sems.at[4 * i + k], device_id=to[k], device_id_type=MESH))
        return out
    return plan


def far_forward_plan(axes):
    def plan(bufs, send_sems, recv_sems, receiving):
        x, y, c = _place()
        dg = 2 * (1 - x) + (1 - y)
        out = []
        for i, b in enumerate(bufs):
            blk = _half(b, (dg,), (1 - c) if receiving else c, axes[i])
            out.append(pltpu.make_async_remote_copy(
                src_ref=blk, dst_ref=blk, send_sem=send_sems.at[i], recv_sem=recv_sems.at[i],
                device_id=(x, y, 1 - c), device_id_type=MESH))
        return out
    return plan


def share_plan(axes):
    def plan(bufs, send_sems, recv_sems, receiving):
        x, y, c = _place()
        out = []
        for i, b in enumerate(bufs):
            blk = _half(b, (), (1 - c) if receiving else c, axes[i])
            out.append(pltpu.make_async_remote_copy(
                src_ref=blk, dst_ref=blk, send_sem=send_sems.at[i], recv_sem=recv_sems.at[i],
                device_id=(x, y, 1 - c), device_id_type=MESH))
        return out
    return plan


def scatter_plan(n):
    def plan(arrs, send_sems, recv_sems, receiving):
        x, y, c = _place()
        out = []
        for i in range(n):
            for j, (cx, cy) in enumerate(_other_chips(x, y)):
                out.append(pltpu.make_async_remote_copy(
                    src_ref=arrs[i].at[2 * cx + cy], dst_ref=arrs[n + i].at[j], send_sem=send_sems.at[3 * i + j],
                    recv_sem=recv_sems.at[3 * i + j], device_id=(cx, cy, c), device_id_type=MESH))
        return out
    return plan


def exchange_plan(axes):
    n = len(axes)

    def plan(arrs, send_sems, recv_sems, receiving):
        x, y, c = _place()
        out = []
        for i in range(n):
            for k in range(N_CHIPS):
                out.append(pltpu.make_async_remote_copy(
                    src_ref=_half(arrs[i], (k,), 1 - c, axes[i]), dst_ref=arrs[n + i].at[k],
                    send_sem=send_sems.at[N_CHIPS * i + k], recv_sem=recv_sems.at[N_CHIPS * i + k],
                    device_id=(x, y, 1 - c), device_id_type=MESH))
        return out
    return plan


def devices_plan(arrs, send_sems, recv_sems, receiving):
    x, y, c = _place()
    out = []
    for k in range(1, N_DEV):
        px, py, pc = x ^ ((k >> 2) & 1), y ^ ((k >> 1) & 1), c ^ (k & 1)
        slot = (4 * px + 2 * py + pc) if receiving else (4 * x + 2 * y + c)
        out.append(pltpu.make_async_remote_copy(
            src_ref=arrs[0], dst_ref=arrs[1].at[slot], send_sem=send_sems.at[k - 1], recv_sem=recv_sems.at[k - 1],
            device_id=(px, py, pc), device_id_type=MESH))
    return out


def _row_tile(r, cols, itemsize, target_bytes=2 * 1024 * 1024):
    best = None
    for t in range(16, r + 1, 16):
        if r % t == 0 and t * cols * itemsize <= target_bytes:
            best = t
    return best if best is not None else r


def pair_sum(g, recv, half, axis, *, name):
    _, hr, hc = recv.shape
    tr = _row_tile(hr, hc, 4)
    nb = hr // tr
    if axis == 0:
        mine = pl.BlockSpec((None, tr, hc), lambda k, r, hf: (k, hf[0] * nb + r, 0))
    else:
        mine = pl.BlockSpec((None, tr, hc), lambda k, r, hf: (k, r, hf[0]))

    def body(half_ref, a_ref, b_ref, o_ref):
        o_ref[...] = (a_ref[...].astype(F32) + b_ref[...].astype(F32)).astype(BF)

    return pl.pallas_call(
        body, name=name,
        grid_spec=pltpu.PrefetchScalarGridSpec(
            num_scalar_prefetch=1, grid=(N_CHIPS, nb),
            in_specs=[mine, pl.BlockSpec((None, tr, hc), lambda k, r, hf: (k, r, 0))],
            out_specs=pl.BlockSpec((None, tr, hc), lambda k, r, hf: (k, r, 0))),
        out_shape=jax.ShapeDtypeStruct((N_CHIPS, hr, hc), BF),
        compiler_params=_cparams(("parallel", "parallel")),
    )(half, g, recv)


def chip_sum(s, recv, place, axis, *, name):
    _, hr, hc = s.shape
    tr = _row_tile(hr, hc, 4)
    nb = hr // tr
    if axis == 0:
        o_spec = pl.BlockSpec((tr, hc), lambda r, pc: (pc[1] * nb + r, 0))
        o_shape = (2 * hr, hc)
    else:
        o_spec = pl.BlockSpec((tr, hc), lambda r, pc: (r, pc[1]))
        o_shape = (hr, 2 * hc)

    def body(place_ref, a_ref, b_ref, o_ref):
        acc = a_ref[...].astype(F32)
        for j in range(3):
            acc = acc + b_ref[j].astype(F32)
        o_ref[...] = acc

    return pl.pallas_call(
        body, name=name,
        grid_spec=pltpu.PrefetchScalarGridSpec(
            num_scalar_prefetch=1, grid=(nb,),
            in_specs=[pl.BlockSpec((None, tr, hc), lambda r, pc: (pc[0], r, 0)),
                      pl.BlockSpec((3, tr, hc), lambda r, pc: (0, r, 0))],
            out_specs=o_spec),
        out_shape=jax.ShapeDtypeStruct(o_shape, F32),
        compiler_params=_cparams(("parallel",)),
    )(place, s, recv)


def sum_devices(v, *, name):
    _, R, C = v.shape

    def body(v_ref, o_ref):
        acc = v_ref[0]
        for k in range(1, N_DEV):
            acc = acc + v_ref[k]
        o_ref[...] = acc

    return pl.pallas_call(body, name=name, out_shape=jax.ShapeDtypeStruct((R, C), F32))(v)


def adamw(w, g, m, v, *, name):
    R, C = w.shape
    tr = _row_tile(R, C, 4, target_bytes=1024 * 1024)
    c1 = 1.0 / (1.0 - ADAM_B1 ** ADAM_STEP)
    c2 = 1.0 / (1.0 - ADAM_B2 ** ADAM_STEP)

    def body(w_ref, g_ref, m_ref, v_ref, go_ref, d_ref, nm_ref, nv_ref):
        gg = g_ref[...]
        go_ref[...] = gg
        nm = ADAM_B1 * m_ref[...] + (1.0 - ADAM_B1) * gg
        nv = ADAM_B2 * v_ref[...] + (1.0 - ADAM_B2) * (gg * gg)
        nm_ref[...] = nm
        nv_ref[...] = nv
        d_ref[...] = -ADAM_LR * ((nm * c1) / (jnp.sqrt(nv * c2) + ADAM_EPS) + ADAM_WD * w_ref[...])

    spec = pl.BlockSpec((tr, C), lambda i: (i, 0))
    return pl.pallas_call(
        body, name=name, grid=(R // tr,), in_specs=[spec] * 4, out_specs=[spec] * 4,
        out_shape=[jax.ShapeDtypeStruct((R, C), F32)] * 4, compiler_params=_cparams(("parallel",)),
    )(w, g, m, v)


def _rope_tables(T):
    pos = np.arange(T, dtype=np.float32)
    inv_freq = (ROPE_THETA ** (-np.arange(0, QK_ROPE, 2, dtype=np.float32) / QK_ROPE)).astype(np.float32)
    ang = pos[:, None] * inv_freq[None, :]
    z = np.zeros((T, 64), np.float32)
    cos = np.concatenate([np.cos(ang), np.cos(ang), z], axis=1).astype(np.float32)
    sin = np.concatenate([np.sin(ang), np.sin(ang), z], axis=1).astype(np.float32)
    return jnp.asarray(cos), jnp.asarray(sin)


def _padded_rows(length):
    T = -(-length // 16) * 16
    while True:
        for n in (4, 5, 6, 7, 8):
            if T % (16 * n) == 0:
                return T, n
        T += 16


def _pick(n, cands):
    for c in cands:
        if n % c == 0:
            return c
    return n


def kernel(x, meta_tokens, g_mix, w_in, b_glu, b_gate, w_dw, b_dw, g_conv_ln, b_conv_ln, w_conv_out, b_conv_out, g_q_lora, w_uq, g_kv_lora, w_uk, w_uv, w_attn_out, w_out, g_ffn, w_ffn_gate, w_ffn_up, w_ffn_down, g_final, loss_target, m_meta_tokens, m_g_mix, m_w_in, m_b_glu, m_b_gate, m_w_dw, m_b_dw, m_g_conv_ln, m_b_conv_ln, m_w_conv_out, m_b_conv_out, m_g_q_lora, m_w_uq, m_g_kv_lora, m_w_uk, m_w_uv, m_w_attn_out, m_w_out, m_g_ffn, m_w_ffn_gate, m_w_ffn_up, m_w_ffn_down, m_g_final, v_meta_tokens, v_g_mix, v_w_in, v_b_glu, v_b_gate, v_w_dw, v_b_dw, v_g_conv_ln, v_b_conv_ln, v_w_conv_out, v_b_conv_out, v_g_q_lora, v_w_uq, v_g_kv_lora, v_w_uk, v_w_uv, v_w_attn_out, v_w_out, v_g_ffn, v_w_ffn_gate, v_w_ffn_up, v_w_ffn_down, v_g_final):
    weights = dict(meta_tokens=meta_tokens, g_mix=g_mix, w_in=w_in, b_glu=b_glu, b_gate=b_gate, w_dw=w_dw, b_dw=b_dw,
                   g_conv_ln=g_conv_ln, b_conv_ln=b_conv_ln, w_conv_out=w_conv_out, b_conv_out=b_conv_out,
                   g_q_lora=g_q_lora, w_uq=w_uq, g_kv_lora=g_kv_lora, w_uk=w_uk, w_uv=w_uv, w_attn_out=w_attn_out,
                   w_out=w_out, g_ffn=g_ffn, w_ffn_gate=w_ffn_gate, w_ffn_up=w_ffn_up, w_ffn_down=w_ffn_down,
                   g_final=g_final)
    m_in = dict(meta_tokens=m_meta_tokens, g_mix=m_g_mix, w_in=m_w_in, b_glu=m_b_glu, b_gate=m_b_gate, w_dw=m_w_dw,
                b_dw=m_b_dw, g_conv_ln=m_g_conv_ln, b_conv_ln=m_b_conv_ln, w_conv_out=m_w_conv_out,
                b_conv_out=m_b_conv_out, g_q_lora=m_g_q_lora, w_uq=m_w_uq, g_kv_lora=m_g_kv_lora, w_uk=m_w_uk,
                w_uv=m_w_uv, w_attn_out=m_w_attn_out, w_out=m_w_out, g_ffn=m_g_ffn, w_ffn_gate=m_w_ffn_gate,
                w_ffn_up=m_w_ffn_up, w_ffn_down=m_w_ffn_down, g_final=m_g_final)
    v_in = dict(meta_tokens=v_meta_tokens, g_mix=v_g_mix, w_in=v_w_in, b_glu=v_b_glu, b_gate=v_b_gate, w_dw=v_w_dw,
                b_dw=v_b_dw, g_conv_ln=v_g_conv_ln, b_conv_ln=v_b_conv_ln, w_conv_out=v_w_conv_out,
                b_conv_out=v_b_conv_out, g_q_lora=v_g_q_lora, w_uq=v_w_uq, g_kv_lora=v_g_kv_lora, w_uk=v_w_uk,
                w_uv=v_w_uv, w_attn_out=v_w_attn_out, w_out=v_w_out, g_ffn=v_g_ffn, w_ffn_gate=v_w_ffn_gate,
                w_ffn_up=v_w_ffn_up, w_ffn_down=v_w_ffn_down, g_final=v_g_final)
    names = list(weights)

    seq, D = x.shape[1], x.shape[2]
    C = w_conv_out.shape[2]
    R = g_q_lora.shape[1]
    H = w_uv.shape[2] * N_CHIPS // V_HEAD
    FB = w_ffn_gate.shape[2]
    FF = FB * N_CHIPS
    length = N_META + seq
    T, nch = _padded_rows(length)
    unit = T // nch
    scale = (QK_NOPE + QK_ROPE) ** -0.5
    assert C == D and w_in.shape[2] * N_CHIPS == 2 * C + 2 * R + QK_ROPE + 2 * D

    xi, yi, ci = _place()
    chip = (2 * xi + yi).astype(jnp.int32)
    half_arr = ci.astype(jnp.int32).reshape(1)

    tr = unit
    tmh = T // 2 if (T // 2) % 16 == 0 else unit
    tmq = unit
    tmw = _pick(D, (1024, 512))
    rc = unit // 2 if unit % 16 == 0 else unit
    tcv = 128

    def tn_of(n, pref=512):
        return _pick(n, (pref, 384, 256, 128))

    grp_a = ["w_in"]
    grp_b = ["w_conv_out", "w_uq", "w_uk", "w_uv", "w_attn_out", "w_out"]
    grp_c = ["w_ffn_gate", "w_ffn_up", "w_ffn_down"]
    big = grp_a + grp_b + grp_c
    axes = {nm: 0 for nm in big}
    axes["w_in"] = 1
    shard = {nm: weights[nm][0] for nm in big}
    shard["w_in"] = jnp.swapaxes(w_in[0], 0, 1)

    def own_blocks(srcs, dtype=BF):
        bufs = []
        for w in srcs:
            wb = w.astype(dtype)
            bufs.append(lax.dynamic_update_slice(lax.empty((N_CHIPS,) + wb.shape, dtype), wb[None], (chip, 0, 0)))
        return bufs

    tiny = jnp.concatenate([meta_tokens, jnp.pad(w_dw[0], ((0, 1), (0, 0)))], axis=0)
    ax_a = [axes[nm] for nm in grp_a] + [0]
    bufs_a, ss_a, rs_a, tok_ga = copies_start(own_blocks([shard[nm] for nm in grp_a]) + own_blocks([tiny], F32),
                                              near_plan(ax_a), 2 * len(ax_a), half_arr, name="gather_a_near_start")
    tok_ga, held = lax.optimization_barrier((tok_ga, (x, loss_target, [shard[nm] for nm in grp_b + grp_c])))
    x_l, tgt_l, shard_bc = held
    h0 = jnp.pad(x_l[0], ((N_META, T - length), (0, 0)))
    tgt = jnp.pad(tgt_l[0], ((N_META, T - length), (0, 0)))
    bufs_b, bufs_c = own_blocks(shard_bc[:len(grp_b)]), own_blocks(shard_bc[len(grp_b):])
    cos, sin = _rope_tables(T)

    bufs_a = copies_wait(bufs_a, near_plan(ax_a), ss_a, rs_a, [h0, tgt] + bufs_b + bufs_c, name="gather_a_near_wait")
    bufs_a, ss_a, rs_a, _ = copies_start(bufs_a, relay_plan(ax_a), 4 * len(ax_a), half_arr,
                                         name="gather_a_relay_start")
    bufs_a = copies_wait(bufs_a, relay_plan(ax_a), ss_a, rs_a, half_arr, name="gather_a_relay_wait")
    bufs_a, ss_a, rs_a, _ = copies_start(bufs_a, far_forward_plan(ax_a), len(ax_a), half_arr,
                                         name="gather_a_far_start")
    bufs_a = copies_wait(bufs_a, far_forward_plan(ax_a), ss_a, rs_a, half_arr, name="gather_a_far_wait")
    G = dict(zip(grp_a, bufs_a))
    meta_full = jnp.concatenate([bufs_a[-1][k, :N_META] for k in range(N_CHIPS)], axis=1)
    w_dw_full = jnp.concatenate([bufs_a[-1][k, N_META:] for k in range(N_CHIPS)], axis=1)
    h0 = lax.dynamic_update_slice(h0, meta_full, (0, 0))
    ax_b = [axes[nm] for nm in grp_b]
    bufs_b, ss_b, rs_b, tok_gb = copies_start(bufs_b, near_plan(ax_b), 2 * len(grp_b), G["w_in"],
                                              name="gather_b_near_start")
    ax_c = [axes[nm] for nm in grp_c]
    bufs_c, ss_c, rs_c, tok_gc = copies_start(bufs_c, near_plan(ax_c), 2 * len(grp_c), tok_gb,
                                              name="gather_c_near_start")

    WT_in = G["w_in"].reshape(N_CHIPS * G["w_in"].shape[1], D)
    o1, o2 = 2 * C, 2 * C + 2 * R + QK_ROPE
    WT_small = jnp.pad(WT_in[o1:o2], ((0, 128 - QK_ROPE), (0, 0)))
    WT_gate = WT_in[o2:]

    u = rms_fwd(h0, g_mix, tr=tr, name="rms_mix", deps=[tok_gb, tok_gc])
    z_glu = mm(u, WT_in, mode="nt", tm=T, tn=tn_of(2 * C), tk=D, out_dtype=BF, name="mm_z_glu", n_out=o1)
    z_small = mm(u, WT_small, mode="nt", tm=T, tn=tn_of(2 * R + 128, 384), tk=D, out_dtype=F32, name="mm_z_small")
    z_gate = mm(u, WT_gate, mode="nt", tm=T, tn=tn_of(2 * D), tk=D, out_dtype=BF, name="mm_z_gate")

    bufs_b = copies_wait(bufs_b, near_plan(ax_b), ss_b, rs_b, z_glu, name="gather_b_near_wait")
    bufs_b, ss_b, rs_b, tok_gb = copies_start(bufs_b, relay_plan(ax_b), 4 * len(grp_b), half_arr,
                                              name="gather_b_relay_start")
    c0 = glu_fwd(z_glu, b_glu, tr=tr, name="glu_fwd", deps=[tok_gb])
    c1 = dwconv_fwd(c0, w_dw_full, b_dw, tc=tcv, rc=rc, name="dwconv_fwd")
    bufs_b = copies_wait(bufs_b, relay_plan(ax_b), ss_b, rs_b, c1, name="gather_b_relay_wait")
    bufs_b, ss_b, rs_b, tok_gb = copies_start(bufs_b, far_forward_plan(ax_b), len(grp_b), half_arr,
                                              name="gather_b_far_start")
    c3 = ln_silu_fwd(c1, g_conv_ln, b_conv_ln, tr=tr, name="ln_silu_fwd")
    cq, ckv, kr = lora_norm_fwd(z_small, g_q_lora, g_kv_lora, cos, sin, tr=tr, name="lora_norm_fwd", deps=[tok_gb])

    def cols_full(a):
        return jnp.concatenate([a[k] for k in range(N_CHIPS)], axis=1)

    G.update(zip(grp_b, copies_wait(bufs_b, far_forward_plan(ax_b), ss_b, rs_b, cq, name="gather_b_far_wait")))
    W_co = G["w_conv_out"].reshape(C, D)
    W_uq = cols_full(G["w_uq"]).reshape(R, H, QK_NOPE + QK_ROPE)
    Wq_p = jnp.pad(W_uq, ((0, 0), (0, 0), (0, HEAD_PAD - QK_NOPE - QK_ROPE))).reshape(R, H * HEAD_PAD)
    W_uk = cols_full(G["w_uk"])
    W_uv = cols_full(G["w_uv"])
    W_ao = G["w_attn_out"].reshape(H * V_HEAD, D)
    W_o = G["w_out"].reshape(D, D)

    y_conv = mm(c3, W_co, mode="nn", tm=T, tn=tn_of(D), tk=C, out_dtype=BF, name="mm_y_conv")

    q = q_proj(cq, Wq_p, cos, sin, scale=scale, name="q_proj")
    kk = k_proj(ckv, W_uk, kr, name="k_proj")
    vv = mm(ckv, W_uv, mode="nn", tm=T, tn=tn_of(H * V_HEAD), tk=R, out_dtype=BF, name="mm_v")
    o_attn, lse = attn_fwd(q, kk, vv, nch=nch, name="attn_fwd")
    bufs_c = copies_wait(bufs_c, near_plan(ax_c), ss_c, rs_c, o_attn, name="gather_c_near_wait")
    bufs_c, ss_c, rs_c, tok_gc = copies_start(bufs_c, relay_plan(ax_c), 4 * len(grp_c), half_arr,
                                              name="gather_c_relay_start")
    y_attn = mm(o_attn, W_ao, mode="nn", tm=T, tn=tn_of(D), tk=H * V_HEAD, out_dtype=BF, name="mm_y_attn",
                deps=[tok_gc])

    mix = mix_fwd(z_gate, y_conv, y_attn, b_gate, b_conv_out, tr=tr, name="mix_fwd")
    bufs_c = copies_wait(bufs_c, relay_plan(ax_c), ss_c, rs_c, mix, name="gather_c_relay_wait")
    bufs_c, ss_c, rs_c, tok_gc = copies_start(bufs_c, far_forward_plan(ax_c), len(grp_c), half_arr,
                                              name="gather_c_far_start")
    h1 = mm(mix, W_o, mode="nn", tm=T, tn=tn_of(D), tk=D, out_dtype=F32, name="mm_h1", res=h0, deps=[tok_gc])

    hn = rms_fwd(h1, g_ffn, tr=tr, name="rms_ffn")

    G.update(zip(grp_c, copies_wait(bufs_c, far_forward_plan(ax_c), ss_c, rs_c, hn, name="gather_c_far_wait")))
    W_fg, W_fu = G["w_ffn_gate"], G["w_ffn_up"]
    W_fd = G["w_ffn_down"].reshape(FF, D)

    fa, fb, f = ffn_fwd(hn, W_fg, W_fu, tm=tmq, name="ffn_fwd")
    h2 = mm(f, W_fd, mode="nn", tm=tmh, tn=tn_of(D, 256), tk=FF, out_dtype=F32, name="mm_h2", res=h1)

    place_arr = jnp.stack([chip, ci.astype(jnp.int32)])

    def col_blocks(a):
        r, cfull = a.shape
        return a.reshape(r, N_CHIPS, cfull // N_CHIPS).transpose(1, 0, 2)

    def rs_begin(nms, gls, after, tag):
        ax = [axes[nm] for nm in nms]
        lands = []
        for g, a in zip(gls, ax):
            _, r, cfull = g.shape
            lands.append(lax.empty((N_CHIPS, r // 2, cfull) if a == 0 else (N_CHIPS, r, cfull // 2), BF))
        arrs, ssem, rsem, tok = copies_start(list(gls) + lands, exchange_plan(ax), N_CHIPS * len(nms), after,
                                             name=f"rs_{tag}_exchange_start")
        return (nms, ax, arrs, ssem, rsem), tok

    def rs_middle(state, after, tag):
        nms, ax, arrs, ssem, rsem = state
        n = len(nms)
        arrs = copies_wait(arrs, exchange_plan(ax), ssem, rsem, after, name=f"rs_{tag}_exchange_wait")
        pair = [pair_sum(g, r, half_arr, a, name="rs_pair_sum_" + nm)
                for nm, g, r, a in zip(nms, arrs[:n], arrs[n:], ax)]
        lands = [lax.empty((3,) + p.shape[1:], BF) for p in pair]
        arrs, ssem, rsem, tok = copies_start(pair + lands, scatter_plan(n), 3 * n, half_arr,
                                             name=f"rs_{tag}_scatter_start")
        return (nms, ax, arrs, ssem, rsem), tok

    def rs_end(state, after, tag):
        nms, ax, arrs, ssem, rsem = state
        n = len(nms)
        arrs = copies_wait(arrs, scatter_plan(n), ssem, rsem, after, name=f"rs_{tag}_scatter_wait")
        reds = [chip_sum(s, r, place_arr, a, name="rs_chip_sum_" + nm)
                for nm, s, r, a in zip(nms, arrs[:n], arrs[n:], ax)]
        reds, ssem, rsem, _ = copies_start(reds, share_plan(ax), n, half_arr, name=f"rs_{tag}_share_start")
        return (nms, ax, reds, ssem, rsem)

    def rs_shared(state, after, tag):
        nms, ax, reds, ssem, rsem = state
        return dict(zip(nms, copies_wait(reds, share_plan(ax), ssem, rsem, after, name=f"rs_{tag}_share_wait")))

    d_h2, d_h2b, loss_p, d_g_final = final_loss(h2, tgt, g_final.reshape(1, D), seq=seq, tr=tr, name="final_loss")
    loss = lax.psum(loss_p[0, 0], ("x", "y", "c"))

    dW_fd = mm(f, d_h2b, mode="tn", tm=FB, tn=tn_of(D), tk=T, out_dtype=BF, name="mm_dw_ffn_down")
    d_a, d_b = ffn_dact(d_h2b, W_fd, fa, fb, tm=tmq, name="ffn_dact")
    dW_fg = mm(hn, d_a, mode="tn", tm=tmw, tn=FB, tk=T, out_dtype=BF, name="mm_dw_ffn_gate", out_blocks=N_CHIPS)
    dW_fu = mm(hn, d_b, mode="tn", tm=tmw, tn=FB, tk=T, out_dtype=BF, name="mm_dw_ffn_up", out_blocks=N_CHIPS)
    st_c, tok_rc = rs_begin(grp_c, [dW_fg, dW_fu, dW_fd.reshape(N_CHIPS, FB, D)], half_arr, "c")
    d_hn = ffn_dhn(d_a, d_b, W_fg, W_fu, tm=tmq, tn=_pick(D, (256,)), name="ffn_dhn", deps=[tok_rc])
    st_c, tok_rc = rs_middle(st_c, d_hn, "c")
    d_h1, d_h1b, d_g_ffn = rms_bwd(d_hn, h1, g_ffn, d_h2, tr=tr, name="rms_ffn_bwd", deps=[tok_rc])

    d_mix = mm(d_h1b, W_o, mode="nt", tm=T, tn=tn_of(D), tk=D, out_dtype=BF, name="mm_d_mix")
    dW_o = mm(mix, d_h1b, mode="tn", tm=tmw, tn=tn_of(D), tk=T, out_dtype=BF, name="mm_dw_out")
    d_yc, d_ya, d_zgate, d_b_gate, d_b_co = mix_bwd(z_gate, y_conv, y_attn, d_mix, b_gate, b_conv_out, tr=tr,
                                                    name="mix_bwd")

    dW_co = mm(c3, d_yc, mode="tn", tm=tmw, tn=tn_of(D), tk=T, out_dtype=BF, name="mm_dw_conv_out")
    d_c3 = mm(d_yc, W_co, mode="nt", tm=T, tn=tn_of(C), tk=D, out_dtype=BF, name="mm_d_c3")
    d_c1, d_g_ln, d_b_ln, d_b_dw = ln_silu_bwd(c1, d_c3, g_conv_ln, b_conv_ln, tr=tr, name="ln_silu_bwd")
    d_c0, d_w_dw = dwconv_bwd(c0, d_c1, w_dw_full, tc=tcv, rc=rc, name="dwconv_bwd")
    d_zglu, d_b_glu = glu_bwd(z_glu, b_glu, d_c0, tr=tr, name="glu_bwd")

    dW_ao = mm(o_attn, d_ya, mode="tn", tm=_pick(H * V_HEAD, (1024, 512)), tn=tn_of(D), tk=T, out_dtype=BF, name="mm_dw_attn_out")
    d_o = mm(d_ya, W_ao, mode="nt", tm=T, tn=tn_of(H * V_HEAD), tk=D, out_dtype=BF, name="mm_d_o")
    d_q2, d_k2, d_v, d_kr = attn_bwd(q, kk, vv, o_attn, d_o, lse, cos, sin, scale=scale, nch=nch, name="attn_bwd")

    dWq_p = mm(cq, d_q2, mode="tn", tm=R, tn=tn_of(H * HEAD_PAD), tk=T, out_dtype=BF, name="mm_dw_uq")
    d_cq = mm(d_q2, Wq_p, mode="nt", tm=T, tn=R, tk=_pick(H * HEAD_PAD, (1024,)), out_dtype=F32, name="mm_d_cq")
    Wk_p = jnp.pad(W_uk.reshape(R, H, QK_NOPE), ((0, 0), (0, 0), (0, HEAD_PAD - QK_NOPE))).reshape(R, H * HEAD_PAD)
    dWk_p = mm(ckv, d_k2, mode="tn", tm=R, tn=tn_of(H * HEAD_PAD), tk=T, out_dtype=BF, name="mm_dw_uk")
    dW_uv = mm(ckv, d_v, mode="tn", tm=R, tn=tn_of(H * V_HEAD), tk=T, out_dtype=BF, name="mm_dw_uv")
    sh_c = rs_end(st_c, [dWq_p, dWk_p, dW_uv, dW_co, dW_ao, dW_o], "c")
    dW_uq = dWq_p.reshape(R, H, HEAD_PAD)[:, :, :QK_NOPE + QK_ROPE].reshape(R, H * (QK_NOPE + QK_ROPE))
    dW_uk = dWk_p.reshape(R, H, HEAD_PAD)[:, :, :QK_NOPE].reshape(R, H * QK_NOPE)
    st_b, tok_rb = rs_begin(grp_b, [dW_co.reshape(N_CHIPS, C // N_CHIPS, D), col_blocks(dW_uq), col_blocks(dW_uk),
                                    col_blocks(dW_uv), dW_ao.reshape(N_CHIPS, H * V_HEAD // N_CHIPS, D),
                                    dW_o.reshape(N_CHIPS, D // N_CHIPS, D)], sh_c[2][-1], "b")
    d_ckv = mm(d_k2, Wk_p, mode="nt", tm=T, tn=R, tk=_pick(H * HEAD_PAD, (1024,)), out_dtype=F32, name="mm_d_ckv_k",
               deps=[tok_rb])
    d_ckv = mm(d_v, W_uv, mode="nt", tm=T, tn=R, tk=_pick(H * V_HEAD, (1024,)), out_dtype=F32, name="mm_d_ckv_v",
               res=d_ckv)
    d_zsmall, d_g_q, d_g_kv = lora_norm_bwd(z_small, d_cq, d_ckv, d_kr, cos, sin, g_q_lora, g_kv_lora, tr=tr,
                                            name="lora_norm_bwd")
    st_b, tok_rb = rs_middle(st_b, d_zsmall, "b")

    tmi = _pick(2 * C, (1024, 512))
    dWT_in = lax.empty((WT_in.shape[0], D), BF)
    dWT_in = mm_tn_into(d_zglu, u, dWT_in, 0, tm=tmi, tn=tn_of(D), name="mm_dw_glu", deps=[tok_rb])
    dWT_in = mm_tn_into(d_zsmall, u, dWT_in, o1, tm=2 * R + 128, tn=tn_of(D), name="mm_dw_small")
    dWT_in = mm_tn_into(d_zgate, u, dWT_in, o2, tm=tmi, tn=tn_of(D), name="mm_dw_gate")
    sh_b = rs_end(st_b, dWT_in, "b")
    st_a, tok_ra = rs_begin(grp_a, [dWT_in.reshape(N_CHIPS, dWT_in.shape[0] // N_CHIPS, D)], sh_b[2][-1], "a")
    d_u = mm(d_zglu, WT_in, mode="nn", tm=T, tn=tn_of(D), tk=_pick(2 * C, (2048, 1024)), out_dtype=F32, name="mm_d_u_glu",
             deps=[tok_ra])
    st_a, tok_ra = rs_middle(st_a, d_u, "a")
    d_u = mm(d_zsmall, WT_small, mode="nn", tm=T, tn=tn_of(D), tk=2 * R + 128, out_dtype=F32, name="mm_d_u_small",
             res=d_u, deps=[tok_ra])
    d_u = mm(d_zgate, WT_gate, mode="nn", tm=T, tn=tn_of(D), tk=_pick(2 * D, (2048, 1024)), out_dtype=F32,
             name="mm_d_u_gate", res=d_u)
    d_h0, _, d_g_mix = rms_bwd(d_u, h0, g_mix, d_h1, tr=tr, name="rms_mix_bwd")

    grad_x = d_h0[N_META:length][None]

    small = ["g_mix", "b_glu", "b_gate", "b_dw", "g_conv_ln", "b_conv_ln", "b_conv_out", "g_q_lora", "g_kv_lora",
             "g_ffn", "g_final", "w_dw", "meta_tokens"]
    sgrads = dict(g_mix=d_g_mix, b_glu=d_b_glu, b_gate=d_b_gate, b_dw=d_b_dw, g_conv_ln=d_g_ln, b_conv_ln=d_b_ln,
                  b_conv_out=d_b_co, g_q_lora=d_g_q, g_kv_lora=d_g_kv, g_ffn=d_g_ffn, g_final=d_g_final,
                  w_dw=d_w_dw[:CONV_WIDTH], meta_tokens=d_h0[:N_META])
    sizes = [int(np.prod(sgrads[nm].shape)) for nm in small]
    packed = jnp.concatenate([sgrads[nm].reshape(-1) for nm in small]).reshape(-1, 128)
    sm_arrs, sm_ss, sm_rs, tok_sm = copies_start([packed, lax.empty((N_DEV,) + packed.shape, F32)], devices_plan,
                                                 N_DEV - 1, packed, name="gather_small_start")

    grads, delta, new_m, new_v, delta_2d = {}, {}, {}, {}, {}

    def adamw_2d(nm, w2, g2, m2, v2, back):
        g_, d_, m_, v_ = adamw(w2, g2, m2, v2, name="adamw_" + nm)
        delta_2d[nm] = d_
        grads[nm], delta[nm], new_m[nm], new_v[nm] = back(g_), back(d_), back(m_), back(v_)

    reduced = rs_shared(sh_c, tok_sm, "c")
    reduced.update(rs_shared(sh_b, tok_sm, "b"))
    for nm in grp_c + grp_b + grp_a:
        if nm == grp_b[0]:
            sh_a = rs_end(st_a, delta_2d[grp_c[-1]], "a")
        if nm == grp_a[0]:
            reduced.update(rs_shared(sh_a, delta_2d[grp_b[-1]], "a"))
        shp = weights[nm].shape
        if axes[nm] == 1:
            adamw_2d(nm, shard[nm], reduced[nm], jnp.swapaxes(m_in[nm][0], 0, 1), jnp.swapaxes(v_in[nm][0], 0, 1),
                     lambda a: jnp.swapaxes(a, 0, 1)[None])
        else:
            two = (shp[-2], shp[-1])
            adamw_2d(nm, weights[nm].reshape(two), reduced[nm], m_in[nm].reshape(two), v_in[nm].reshape(two),
                     lambda a, shp=shp: a.reshape(shp))

    sm_arrs = copies_wait(sm_arrs, devices_plan, sm_ss, sm_rs, delta_2d[grp_a[0]], name="gather_small_wait")
    me = (4 * xi + 2 * yi + ci).astype(jnp.int32)
    parts = lax.dynamic_update_slice(sm_arrs[1], sm_arrs[0][None], (me, 0, 0))
    summed = sum_devices(parts, name="sum_small_grads").reshape(-1)
    offs = np.concatenate([[0], np.cumsum(sizes)])
    sfull = {nm: summed[int(offs[i]):int(offs[i + 1])].reshape(sgrads[nm].shape) for i, nm in enumerate(small)}
    for nm in small:
        gfull = sfull[nm]
        if nm == "w_dw":
            cb = C // N_CHIPS
            grads[nm] = lax.dynamic_slice(gfull, (0, chip * cb), (CONV_WIDTH, cb))[None]
        elif nm == "meta_tokens":
            cb = D // N_CHIPS
            grads[nm] = lax.dynamic_slice(gfull, (0, chip * cb), (N_META, cb))
        else:
            grads[nm] = gfull.reshape(weights[nm].shape)

    rep = [nm for nm in small if nm not in ("w_dw", "meta_tokens")]

    def pack(d):
        return jnp.concatenate([d[nm].reshape(-1) for nm in rep]).reshape(-1, 128)

    _, pd, pm, pv = adamw(pack(weights), pack(grads), pack(m_in), pack(v_in), name="adamw_small")
    rsz = [int(np.prod(weights[nm].shape)) for nm in rep]
    roff = np.concatenate([[0], np.cumsum(rsz)])
    for i, nm in enumerate(rep):
        sl = slice(int(roff[i]), int(roff[i + 1]))
        delta[nm] = pd.reshape(-1)[sl].reshape(weights[nm].shape)
        new_m[nm] = pm.reshape(-1)[sl].reshape(weights[nm].shape)
        new_v[nm] = pv.reshape(-1)[sl].reshape(weights[nm].shape)
    for nm in ["w_dw", "meta_tokens"]:
        shp = weights[nm].shape
        two = (shp[-2], shp[-1])
        adamw_2d(nm, weights[nm].reshape(two), grads[nm].reshape(two), m_in[nm].reshape(two), v_in[nm].reshape(two),
                 lambda a, shp=shp: a.reshape(shp))

    return (loss, grad_x, *[grads[nm] for nm in names], *[delta[nm] for nm in names],
            *[new_m[nm] for nm in names], *[new_v[nm] for nm in names])
```

```python
import functools

import numpy as np
import jax
import jax.numpy as jnp
from jax import lax
from jax.experimental import pallas as pl
from jax.experimental.pallas import tpu as pltpu

F32 = jnp.float32
BF = jnp.bfloat16
MESH = pl.DeviceIdType.MESH

N_META = 16
CONV_WIDTH = 31
CONV_PAD = 32
QK_NOPE = 128
QK_ROPE = 64
V_HEAD = 128
HEAD_PAD = 256
ROPE_THETA = 10000.0
EPS = 1e-6
ADAM_LR = 0.001
ADAM_B1 = 0.9
ADAM_B2 = 0.999
ADAM_EPS = 1e-08
ADAM_WD = 0.01
ADAM_STEP = 10
VMEM_LIMIT = 56 * 1024 * 1024
N_CHIPS = 4
N_DEV = 8


def _cparams(sem):
    return pltpu.CompilerParams(dimension_semantics=sem, vmem_limit_bytes=VMEM_LIMIT)


def mm(a, b, *, mode, tm, tn, tk, out_dtype, name, res=None, out_blocks=None, deps=(), n_out=None):
    b3 = b.ndim == 3
    if mode == "nn":
        M, K = a.shape
        N = b.shape[0] * b.shape[2] if b3 else b.shape[1]
        a_spec = pl.BlockSpec((tm, tk), lambda i, j, k: (i, k))
        if b3:
            per = b.shape[2] // tn
            b_spec = pl.BlockSpec((None, tk, tn), lambda i, j, k: (j // per, k, j % per))
        else:
            b_spec = pl.BlockSpec((tk, tn), lambda i, j, k: (k, j))
        dims = (((1,), (0,)), ((), ()))
    elif mode == "nt":
        M, K = a.shape
        N = n_out if n_out is not None else (b.shape[1] if b3 else b.shape[0])
        a_spec = pl.BlockSpec((tm, tk), lambda i, j, k: (i, k))
        if b3:
            per = b.shape[2] // tk
            b_spec = pl.BlockSpec((None, tn, tk), lambda i, j, k: (k // per, j, k % per))
        else:
            b_spec = pl.BlockSpec((tn, tk), lambda i, j, k: (j, k))
        dims = (((1,), (1,)), ((), ()))
    else:
        K, M = a.shape
        N = b.shape[1]
        a_spec = pl.BlockSpec((tk, tm), lambda i, j, k: (k, i))
        b_spec = pl.BlockSpec((tk, tn), lambda i, j, k: (k, j))
        dims = (((0,), (0,)), ((), ()))
    gm, gn, gk = M // tm, N // tn, K // tk
    assert gm * tm == M and gn * tn == N and gk * tk == K, (name, a.shape, b.shape, tm, tn, tk)
    if out_blocks is None:
        o_spec = pl.BlockSpec((tm, tn), lambda i, j, k: (i, j))
        o_shape = (M, N)
    else:
        nbw = N // out_blocks
        per_o = nbw // tn
        assert per_o * tn == nbw
        o_spec = pl.BlockSpec((None, tm, tn), lambda i, j, k: (j // per_o, i, j % per_o))
        o_shape = (out_blocks, M, nbw)
    has_res = res is not None

    def body(*refs):
        a_ref, b_ref = refs[0], refs[1]
        r_ref = refs[2] if has_res else None
        o_ref = refs[2 + has_res + len(deps)]
        acc = refs[-1]
        p = lax.dot_general(a_ref[...], b_ref[...], dims, preferred_element_type=F32)

        def finish(v):
            if has_res:
                v = v + r_ref[...]
            o_ref[...] = v.astype(o_ref.dtype)

        if gk == 1:
            finish(p)
        else:
            k = pl.program_id(2)

            @pl.when(k == 0)
            def _():
                acc[...] = p

            @pl.when(k > 0)
            def _():
                acc[...] += p

            @pl.when(k == gk - 1)
            def _():
                finish(acc[...])

    in_specs = [a_spec, b_spec]
    args = [a, b]
    if has_res:
        in_specs.append(pl.BlockSpec((tm, tn), lambda i, j, k: (i, j)))
        args.append(res)
    in_specs += [pl.BlockSpec(memory_space=pl.ANY)] * len(deps)
    args += list(deps)
    return pl.pallas_call(
        body, name=name, grid=(gm, gn, gk), in_specs=in_specs, out_specs=o_spec,
        out_shape=jax.ShapeDtypeStruct(o_shape, out_dtype),
        scratch_shapes=[pltpu.VMEM((tm, tn), F32)] if gk > 1 else [],
        compiler_params=_cparams(("parallel", "parallel", "arbitrary")),
    )(*args)


def mm_tn_into(a, b, buf, row0, *, tm, tn, name, deps=()):
    K, M = a.shape
    N = b.shape[1]
    gm, gn = M // tm, N // tn
    assert gm * tm == M and gn * tn == N and buf.shape[1] == N and row0 + M <= buf.shape[0]
    steps = gm * gn

    def body(a_ref, b_ref, buf_ref, *rest):
        o_ref, tile, sem = rest[len(deps):]
        i, j = pl.program_id(0), pl.program_id(1)
        s = i * gn + j
        slot = s % 2

        def out_copy(sl):
            return pltpu.make_async_copy(tile.at[sl], o_ref.at[pl.ds(row0 + i * tm, tm), pl.ds(j * tn, tn)],
                                         sem.at[sl])

        @pl.when(s >= 2)
        def _():
            out_copy(slot).wait()

        tile[slot] = lax.dot_general(a_ref[...], b_ref[...], (((0,), (0,)), ((), ())),
                                     preferred_element_type=F32).astype(tile.dtype)
        out_copy(slot).start()

        @pl.when(s == steps - 1)
        def _():
            out_copy(slot).wait()
            if steps >= 2:
                out_copy(1 - slot).wait()

    anyspec = pl.BlockSpec(memory_space=pl.ANY)
    return pl.pallas_call(
        body, name=name, grid=(gm, gn),
        in_specs=[pl.BlockSpec((K, tm), lambda i, j: (0, i)), pl.BlockSpec((K, tn), lambda i, j: (0, j)), anyspec]
        + [anyspec] * len(deps),
        out_specs=anyspec, out_shape=jax.ShapeDtypeStruct(buf.shape, buf.dtype),
        input_output_aliases={2: 0},
        scratch_shapes=[pltpu.VMEM((2, tm, tn), buf.dtype), pltpu.SemaphoreType.DMA((2,))],
        compiler_params=_cparams(("arbitrary", "arbitrary")),
    )(a, b, buf, *deps)


def rowwise(body, row_ins, full_ins, row_outs, acc_outs, *, tr, name, deps=()):
    T = row_ins[0].shape[0]
    assert T % tr == 0, (name, T, tr)
    n_ri, n_fi, n_ro = len(row_ins), len(full_ins), len(row_outs)
    n_in = n_ri + n_fi + len(deps)

    def kern(*refs):
        body(pl.program_id(0), refs[:n_ri], refs[n_ri:n_ri + n_fi], refs[n_in:n_in + n_ro], refs[n_in + n_ro:])

    in_specs = [pl.BlockSpec((tr, a.shape[1]), lambda i: (i, 0)) for a in row_ins]
    in_specs += [pl.BlockSpec(a.shape, lambda i: (0, 0)) for a in full_ins]
    in_specs += [pl.BlockSpec(memory_space=pl.ANY)] * len(deps)
    out_specs = [pl.BlockSpec((tr, c), lambda i: (i, 0)) for c, _ in row_outs]
    out_specs += [pl.BlockSpec(s, lambda i: (0, 0)) for s in acc_outs]
    out_shape = [jax.ShapeDtypeStruct((T, c), d) for c, d in row_outs]
    out_shape += [jax.ShapeDtypeStruct(s, F32) for s in acc_outs]
    return pl.pallas_call(
        kern, name=name, grid=(T // tr,), in_specs=in_specs, out_specs=out_specs, out_shape=out_shape,
        compiler_params=_cparams(("arbitrary",)),
    )(*row_ins, *full_ins, *deps)


def _acc(step, ref, val):
    @pl.when(step == 0)
    def _():
        ref[...] = val

    @pl.when(step > 0)
    def _():
        ref[...] += val


def _colsum(x):
    return jnp.sum(x, axis=0, keepdims=True)


def _sigmoid(x):
    return 1.0 / (1.0 + jnp.exp(-x))


def _rot_half(v):
    lane = lax.broadcasted_iota(jnp.int32, v.shape, 1)
    lo = -pltpu.roll(v, 96, 1)
    hi = pltpu.roll(v, 32, 1)
    return jnp.where(lane < 32, lo, jnp.where(lane < 64, hi, 0.0))


def rms_fwd(h, g, *, tr, name, deps=()):
    def body(step, ri, fi, ro, ao):
        x = ri[0][...]
        r = lax.rsqrt(jnp.mean(x * x, axis=-1, keepdims=True) + EPS)
        ro[0][...] = ((x * r) * fi[0][...]).astype(BF)

    return rowwise(body, [h], [g], [(h.shape[1], BF)], [], tr=tr, name=name, deps=deps)[0]


def rms_bwd(dy, h, g, dres, *, tr, name, deps=()):
    D = h.shape[1]

    def body(step, ri, fi, ro, ao):
        d, x, dr = ri[0][...].astype(F32), ri[1][...], ri[2][...]
        r = lax.rsqrt(jnp.mean(x * x, axis=-1, keepdims=True) + EPS)
        n = x * r
        _acc(step, ao[0], _colsum(d * n))
        dn = d * fi[0][...]
        dh = r * (dn - n * jnp.mean(dn * n, axis=-1, keepdims=True)) + dr
        ro[0][...] = dh
        ro[1][...] = dh.astype(BF)

    return rowwise(body, [dy, h, dres], [g], [(D, F32), (D, BF)], [(1, D)], tr=tr, name=name, deps=deps)


def glu_fwd(z, b, *, tr, name, deps=()):
    C = z.shape[1] // 2

    def body(step, ri, fi, ro, ao):
        zz = ri[0][...].astype(F32) + fi[0][...]
        ro[0][...] = zz[:, :C] * _sigmoid(zz[:, C:])

    return rowwise(body, [z], [b], [(C, F32)], [], tr=tr, name=name, deps=deps)[0]


def glu_bwd(z, b, dc, *, tr, name):
    C = z.shape[1] // 2

    def body(step, ri, fi, ro, ao):
        zz = ri[0][...].astype(F32) + fi[0][...]
        d = ri[1][...]
        za, sg = zz[:, :C], _sigmoid(zz[:, C:])
        dza = d * sg
        dzb = d * za * sg * (1.0 - sg)
        ro[0][:, :C] = dza.astype(BF)
        ro[0][:, C:] = dzb.astype(BF)
        _acc(step, ao[0], _colsum(dza))
        _acc(step, ao[1], _colsum(dzb))

    dz, da, db = rowwise(body, [z, dc], [b], [(2 * C, BF)], [(1, C), (1, C)], tr=tr, name=name)
    return dz, jnp.concatenate([da, db], axis=1)


def ln_silu_fwd(c1, g, b, *, tr, name):
    def body(step, ri, fi, ro, ao):
        x = ri[0][...]
        mu = jnp.mean(x, axis=-1, keepdims=True)
        xc = x - mu
        rstd = lax.rsqrt(jnp.mean(xc * xc, axis=-1, keepdims=True) + EPS)
        y = (xc * rstd) * fi[0][...] + fi[1][...]
        ro[0][...] = (y * _sigmoid(y)).astype(BF)

    return rowwise(body, [c1], [g, b], [(c1.shape[1], BF)], [], tr=tr, name=name)[0]


def ln_silu_bwd(c1, dc3, g, b, *, tr, name):
    C = c1.shape[1]

    def body(step, ri, fi, ro, ao):
        x, d3 = ri[0][...], ri[1][...].astype(F32)
        mu = jnp.mean(x, axis=-1, keepdims=True)
        xc = x - mu
        rstd = lax.rsqrt(jnp.mean(xc * xc, axis=-1, keepdims=True) + EPS)
        n = xc * rstd
        y = n * fi[0][...] + fi[1][...]
        sg = _sigmoid(y)
        dy = d3 * (sg * (1.0 + y * (1.0 - sg)))
        _acc(step, ao[0], _colsum(dy * n))
        _acc(step, ao[1], _colsum(dy))
        dn = dy * fi[0][...]
        dx = rstd * (dn - jnp.mean(dn, axis=-1, keepdims=True) - n * jnp.mean(dn * n, axis=-1, keepdims=True))
        ro[0][...] = dx
        _acc(step, ao[2], _colsum(dx))

    return rowwise(body, [c1, dc3], [g, b], [(C, F32)], [(1, C), (1, C), (1, C)], tr=tr, name=name)


def lora_norm_fwd(zs, gq, gkv, cos, sin, *, tr, name, deps=()):
    R = gq.shape[1]

    def body(step, ri, fi, ro, ao):
        z = ri[0][...]
        for o, gi in ((0, 0), (1, 1)):
            x = z[:, o * R:(o + 1) * R]
            r = lax.rsqrt(jnp.mean(x * x, axis=-1, keepdims=True) + EPS)
            ro[o][...] = ((x * r) * fi[gi][...]).astype(BF)
        kr = z[:, 2 * R:2 * R + 128]
        ro[2][...] = kr * ri[1][...] + _rot_half(kr) * ri[2][...]

    return rowwise(body, [zs, cos, sin], [gq, gkv], [(R, BF), (R, BF), (128, F32)], [], tr=tr, name=name, deps=deps)


def lora_norm_bwd(zs, dcq, dckv, dkr, cos, sin, gq, gkv, *, tr, name):
    R = gq.shape[1]

    def body(step, ri, fi, ro, ao):
        z = ri[0][...]
        for o in (0, 1):
            x = z[:, o * R:(o + 1) * R]
            d = ri[1 + o][...]
            r = lax.rsqrt(jnp.mean(x * x, axis=-1, keepdims=True) + EPS)
            n = x * r
            _acc(step, ao[o], _colsum(d * n))
            dn = d * fi[o][...]
            ro[0][:, o * R:(o + 1) * R] = (r * (dn - n * jnp.mean(dn * n, axis=-1, keepdims=True))).astype(BF)
        dk = ri[3][...]
        ro[0][:, 2 * R:2 * R + 128] = (dk * ri[4][...] - _rot_half(dk * ri[5][...])).astype(BF)

    return rowwise(body, [zs, dcq, dckv, dkr, cos, sin], [gq, gkv], [(2 * R + 128, BF)], [(1, R), (1, R)],
                   tr=tr, name=name)


def mix_fwd(zg, yc, ya, bg, bco, *, tr, name):
    D = yc.shape[1]

    def body(step, ri, fi, ro, ao):
        g = _sigmoid(ri[0][...].astype(F32) + fi[0][...])
        ro[0][...] = (g[:, :D] * (ri[1][...].astype(F32) + fi[1][...]) + g[:, D:] * ri[2][...].astype(F32)).astype(BF)

    return rowwise(body, [zg, yc, ya], [bg, bco], [(D, BF)], [], tr=tr, name=name)[0]


def mix_bwd(zg, yc, ya, dmix, bg, bco, *, tr, name):
    D = yc.shape[1]

    def body(step, ri, fi, ro, ao):
        g = _sigmoid(ri[0][...].astype(F32) + fi[0][...])
        gc, ga = g[:, :D], g[:, D:]
        ycv = ri[1][...].astype(F32) + fi[1][...]
        yav = ri[2][...].astype(F32)
        dm = ri[3][...].astype(F32)
        dyc = dm * gc
        ro[0][...] = dyc.astype(BF)
        ro[1][...] = (dm * ga).astype(BF)
        dzc = dm * ycv * gc * (1.0 - gc)
        dza = dm * yav * ga * (1.0 - ga)
        ro[2][:, :D] = dzc.astype(BF)
        ro[2][:, D:] = dza.astype(BF)
        _acc(step, ao[0], _colsum(dzc))
        _acc(step, ao[1], _colsum(dza))
        _acc(step, ao[2], _colsum(dyc))

    dyc, dya, dzg, dbc, dba, dbo = rowwise(body, [zg, yc, ya, dmix], [bg, bco], [(D, BF), (D, BF), (2 * D, BF)],
                                           [(1, D), (1, D), (1, D)], tr=tr, name=name)
    return dyc, dya, dzg, jnp.concatenate([dbc, dba], axis=1), dbo


def ffn_fwd(hn, wg, wu, *, tm, name):
    T, D = hn.shape
    nb, _, FB = wg.shape
    assert T % tm == 0

    def body(h_ref, g_ref, u_ref, a_ref, b_ref, f_ref):
        h = h_ref[...]
        a = jnp.dot(h, g_ref[...], preferred_element_type=F32)
        b = jnp.dot(h, u_ref[...], preferred_element_type=F32)
        a_ref[...] = a
        b_ref[...] = b
        f_ref[...] = (a * _sigmoid(a) * b).astype(BF)

    wspec = pl.BlockSpec((None, D, FB), lambda j, i: (j, 0, 0))
    ospec = pl.BlockSpec((tm, FB), lambda j, i: (i, j))
    return pl.pallas_call(
        body, name=name, grid=(nb, T // tm),
        in_specs=[pl.BlockSpec((tm, D), lambda j, i: (i, 0)), wspec, wspec], out_specs=[ospec, ospec, ospec],
        out_shape=[jax.ShapeDtypeStruct((T, nb * FB), F32), jax.ShapeDtypeStruct((T, nb * FB), F32),
                   jax.ShapeDtypeStruct((T, nb * FB), BF)],
        compiler_params=_cparams(("parallel", "parallel")),
    )(hn, wg, wu)


def ffn_dact(dh, wd, a, b, *, tm, name):
    T, D = dh.shape
    FF = wd.shape[0]
    FB = FF // N_CHIPS

    def body(dh_ref, w_ref, a_ref, b_ref, da_ref, db_ref):
        d = lax.dot_general(dh_ref[...], w_ref[...], (((1,), (1,)), ((), ())), preferred_element_type=F32)
        x, u = a_ref[...], b_ref[...]
        sg = _sigmoid(x)
        da_ref[...] = (d * u * (sg * (1.0 + x * (1.0 - sg)))).astype(BF)
        db_ref[...] = (d * (x * sg)).astype(BF)

    blk = pl.BlockSpec((tm, FB), lambda j, i: (i, j))
    return pl.pallas_call(
        body, name=name, grid=(N_CHIPS, T // tm),
        in_specs=[pl.BlockSpec((tm, D), lambda j, i: (i, 0)), pl.BlockSpec((FB, D), lambda j, i: (j, 0)), blk, blk],
        out_specs=[blk, blk],
        out_shape=[jax.ShapeDtypeStruct((T, FF), BF), jax.ShapeDtypeStruct((T, FF), BF)],
        compiler_params=_cparams(("parallel", "parallel")),
    )(dh, wd, a, b)


def ffn_dhn(da, db, wg, wu, *, tm, tn, name, deps=()):
    T, FF = da.shape
    nb, D, FB = wg.shape
    dims = (((1,), (1,)), ((), ()))

    def body(da_ref, db_ref, g_ref, u_ref, *rest):
        o_ref = rest[len(deps)]
        acc = None
        for k in range(nb):
            cols = slice(k * FB, (k + 1) * FB)
            p = lax.dot_general(da_ref[:, cols], g_ref[k], dims, preferred_element_type=F32)
            p = p + lax.dot_general(db_ref[:, cols], u_ref[k], dims, preferred_element_type=F32)
            acc = p if acc is None else acc + p
        o_ref[...] = acc.astype(BF)

    aspec = pl.BlockSpec((tm, FF), lambda i, j: (i, 0))
    wspec = pl.BlockSpec((nb, tn, FB), lambda i, j: (0, j, 0))
    return pl.pallas_call(
        body, name=name, grid=(T // tm, D // tn),
        in_specs=[aspec, aspec, wspec, wspec] + [pl.BlockSpec(memory_space=pl.ANY)] * len(deps),
        out_specs=pl.BlockSpec((tm, tn), lambda i, j: (i, j)),
        out_shape=jax.ShapeDtypeStruct((T, D), BF),
        compiler_params=_cparams(("parallel", "parallel")),
    )(da, db, wg, wu, *deps)


def final_loss(h2, tgt, g, *, seq, tr, name):
    D = h2.shape[1]

    def body(step, ri, fi, ro, ao):
        x = ri[0][...]
        r = lax.rsqrt(jnp.mean(x * x, axis=-1, keepdims=True) + EPS)
        n = x * r
        gg = fi[0][...]
        row = lax.broadcasted_iota(jnp.int32, (tr, 1), 0) + step * tr
        live = jnp.logical_and(row >= N_META, row < N_META + seq)
        e = jnp.where(live, n * gg - ri[1][...], 0.0)
        _acc(step, ao[0], jnp.broadcast_to(0.5 * jnp.sum(jnp.mean(e * e, axis=-1, keepdims=True)), (1, 128)))
        dy = e * (1.0 / D)
        _acc(step, ao[1], _colsum(dy * n))
        dn = dy * gg
        dh = r * (dn - n * jnp.mean(dn * n, axis=-1, keepdims=True))
        ro[0][...] = dh
        ro[1][...] = dh.astype(BF)

    return rowwise(body, [h2, tgt], [g], [(D, F32), (D, BF)], [(1, 128), (1, D)], tr=tr, name=name)


def _shifted_copies(s_ref, T, rc):
    for r in range(1, 8):
        for t0 in range(0, T, rc):
            s_ref[r, t0:t0 + rc, :] = s_ref[0, t0 + r:t0 + r + rc, :]
        s_ref[r, T:T + CONV_PAD - 8, :] = s_ref[0, T + r:T + r + CONV_PAD - 8, :]


def _shifted(s_ref, start, rows):
    return s_ref[start % 8, start - start % 8:start - start % 8 + rows, :]


def dwconv_fwd(x, w, b, *, tc, rc, name, deps=()):
    T, C = x.shape
    nchunk = T // rc
    assert nchunk * rc == T and C % tc == 0

    def body(x_ref, w_ref, b_ref, *rest):
        y_ref, xs_ref = rest[len(deps):]
        xs_ref[0, 0:CONV_PAD, :] = jnp.zeros((CONV_PAD, tc), F32)
        xs_ref[0, CONV_PAD:CONV_PAD + T, :] = x_ref[...]
        _shifted_copies(xs_ref, T, rc)
        off = CONV_PAD - (CONV_WIDTH - 1)
        for ci in range(nchunk):
            t0 = ci * rc
            acc = jnp.broadcast_to(b_ref[...], (rc, tc))
            for j in range(CONV_WIDTH):
                acc = acc + w_ref[j:j + 1, :] * _shifted(xs_ref, t0 + off + j, rc)
            y_ref[t0:t0 + rc, :] = acc

    return pl.pallas_call(
        body, name=name, grid=(C // tc,),
        in_specs=[pl.BlockSpec((T, tc), lambda j: (0, j)), pl.BlockSpec((32, tc), lambda j: (0, j)),
                  pl.BlockSpec((1, tc), lambda j: (0, j))] + [pl.BlockSpec(memory_space=pl.ANY)] * len(deps),
        out_specs=pl.BlockSpec((T, tc), lambda j: (0, j)),
        out_shape=jax.ShapeDtypeStruct((T, C), F32),
        scratch_shapes=[pltpu.VMEM((8, T + CONV_PAD, tc), F32)],
        compiler_params=_cparams(("parallel",)),
    )(x, w, b, *deps)


def dwconv_bwd(x, dy, w, *, tc, rc, name):
    T, C = x.shape
    nchunk = T // rc
    assert nchunk * rc == T and C % tc == 0 and rc % 8 == 0

    def body(x_ref, dy_ref, w_ref, dx_ref, dw_ref, xs_ref, ds_ref):
        off = CONV_PAD - (CONV_WIDTH - 1)
        xs_ref[0, 0:CONV_PAD, :] = jnp.zeros((CONV_PAD, tc), F32)
        xs_ref[0, CONV_PAD:CONV_PAD + T, :] = x_ref[...]
        _shifted_copies(xs_ref, T, rc)
        ds_ref[0, 0:T, :] = dy_ref[...]
        ds_ref[0, T:T + CONV_PAD, :] = jnp.zeros((CONV_PAD, tc), F32)
        _shifted_copies(ds_ref, T, rc)
        for ci in range(nchunk):
            t0 = ci * rc
            acc = jnp.zeros((rc, tc), F32)
            for j in range(CONV_WIDTH):
                acc = acc + w_ref[j:j + 1, :] * _shifted(ds_ref, t0 + (CONV_WIDTH - 1) - j, rc)
            dx_ref[t0:t0 + rc, :] = acc
        dw_ref[...] = jnp.zeros((32, tc), F32)
        for j in range(CONV_WIDTH):
            acc = jnp.zeros((8, tc), F32)
            for ci in range(nchunk):
                t0 = ci * rc
                pr = dy_ref[t0:t0 + rc, :] * _shifted(xs_ref, t0 + off + j, rc)
                acc = acc + jnp.sum(pr.reshape(rc // 8, 8, tc), axis=0)
            dw_ref[j:j + 1, :] = jnp.sum(acc, axis=0, keepdims=True)

    return pl.pallas_call(
        body, name=name, grid=(C // tc,),
        in_specs=[pl.BlockSpec((T, tc), lambda j: (0, j)), pl.BlockSpec((T, tc), lambda j: (0, j)),
                  pl.BlockSpec((32, tc), lambda j: (0, j))],
        out_specs=[pl.BlockSpec((T, tc), lambda j: (0, j)), pl.BlockSpec((32, tc), lambda j: (0, j))],
        out_shape=[jax.ShapeDtypeStruct((T, C), F32), jax.ShapeDtypeStruct((32, C), F32)],
        scratch_shapes=[pltpu.VMEM((8, T + CONV_PAD, tc), F32), pltpu.VMEM((8, T + CONV_PAD, tc), F32)],
        compiler_params=_cparams(("parallel",)),
    )(x, dy, w)


def q_proj(cq, wq, cos, sin, *, scale, name):
    T, R = cq.shape
    H = wq.shape[1] // HEAD_PAD

    def body(a_ref, b_ref, c_ref, s_ref, o_ref):
        p = jnp.dot(a_ref[...], b_ref[...], preferred_element_type=F32)
        o_ref[:, :QK_NOPE] = (p[:, :QK_NOPE] * scale).astype(BF)
        x = p[:, QK_NOPE:]
        o_ref[:, QK_NOPE:] = ((x * c_ref[...] + _rot_half(x) * s_ref[...]) * scale).astype(BF)

    return pl.pallas_call(
        body, name=name, grid=(H,),
        in_specs=[pl.BlockSpec((T, R), lambda h: (0, 0)), pl.BlockSpec((R, HEAD_PAD), lambda h: (0, h)),
                  pl.BlockSpec((T, 128), lambda h: (0, 0)), pl.BlockSpec((T, 128), lambda h: (0, 0))],
        out_specs=pl.BlockSpec((T, HEAD_PAD), lambda h: (0, h)),
        out_shape=jax.ShapeDtypeStruct((T, H * HEAD_PAD), BF),
        compiler_params=_cparams(("parallel",)),
    )(cq, wq, cos, sin)


def k_proj(ckv, wk, kr, *, name):
    T, R = ckv.shape
    H = wk.shape[1] // QK_NOPE

    def body(a_ref, b_ref, kr_ref, o_ref):
        p = jnp.dot(a_ref[...], b_ref[...], preferred_element_type=F32)
        o_ref[:, :QK_NOPE] = p.astype(BF)
        o_ref[:, QK_NOPE:] = kr_ref[...].astype(BF)

    return pl.pallas_call(
        body, name=name, grid=(H,),
        in_specs=[pl.BlockSpec((T, R), lambda h: (0, 0)), pl.BlockSpec((R, QK_NOPE), lambda h: (0, h)),
                  pl.BlockSpec((T, 128), lambda h: (0, 0))],
        out_specs=pl.BlockSpec((T, HEAD_PAD), lambda h: (0, h)),
        out_shape=jax.ShapeDtypeStruct((T, H * HEAD_PAD), BF),
        compiler_params=_cparams(("parallel",)),
    )(ckv, wk, kr)


def _causal_scores(q, k_ref, qi, CH, split):
    nt = (((1,), (1,)), ((), ()))
    pieces = []
    if not split:
        rows = slice(0, (qi + 1) * CH)
        s = lax.dot_general(q, k_ref[rows, :], nt, preferred_element_type=F32)
        keep = lax.broadcasted_iota(jnp.int32, s.shape, 1) <= lax.broadcasted_iota(jnp.int32, s.shape, 0) + qi * CH
        return [(rows, jnp.where(keep, s, -1e30))]
    if qi > 0:
        rows = slice(0, qi * CH)
        pieces.append((rows, lax.dot_general(q, k_ref[rows, :], nt, preferred_element_type=F32)))
    rows = slice(qi * CH, (qi + 1) * CH)
    s = lax.dot_general(q, k_ref[rows, :], nt, preferred_element_type=F32)
    keep = lax.broadcasted_iota(jnp.int32, s.shape, 1) <= lax.broadcasted_iota(jnp.int32, s.shape, 0)
    pieces.append((rows, jnp.where(keep, s, -1e30)))
    return pieces


def attn_fwd(q, k, v, *, nch, name):
    T = q.shape[0]
    H = q.shape[1] // HEAD_PAD
    CH = T // nch
    assert CH * nch == T and CH % 16 == 0

    def body(q_ref, k_ref, v_ref, o_ref, lse_ref):
        for qi in range(nch):
            L = (qi + 1) * CH
            pieces = _causal_scores(q_ref[qi * CH:L, :], k_ref, qi, CH, True)
            m = functools.reduce(jnp.maximum, [jnp.max(s, axis=1, keepdims=True) for _, s in pieces])
            l, o = 0.0, 0.0
            for rows, s in pieces:
                p = jnp.exp(s - m)
                l = l + jnp.sum(p, axis=1, keepdims=True)
                o = o + jnp.dot(p.astype(BF), v_ref[rows, :], preferred_element_type=F32)
            o_ref[qi * CH:L, :] = (o / l).astype(BF)
            lse_ref[qi * CH:L, :] = jnp.broadcast_to(m + jnp.log(l), (CH, V_HEAD))

    return pl.pallas_call(
        body, name=name, grid=(H,),
        in_specs=[pl.BlockSpec((T, HEAD_PAD), lambda h: (0, h)), pl.BlockSpec((T, HEAD_PAD), lambda h: (0, h)),
                  pl.BlockSpec((T, V_HEAD), lambda h: (0, h))],
        out_specs=[pl.BlockSpec((T, V_HEAD), lambda h: (0, h)), pl.BlockSpec((T, V_HEAD), lambda h: (0, h))],
        out_shape=[jax.ShapeDtypeStruct((T, H * V_HEAD), BF), jax.ShapeDtypeStruct((T, H * V_HEAD), F32)],
        compiler_params=_cparams(("parallel",)),
    )(q, k, v)


def attn_bwd(q, k, v, o, do, lse, cos, sin, *, scale, nch, name):
    T = q.shape[0]
    H = q.shape[1] // HEAD_PAD
    CH = T // nch
    tn_dims = (((0,), (0,)), ((), ()))
    nt_dims = (((1,), (1,)), ((), ()))

    def body(q_ref, k_ref, v_ref, o_ref, do_ref, lse_ref, c_ref, s_ref,
             dq_ref, dk_ref, dv_ref, dkr_ref, dk_acc, dv_acc):
        h = pl.program_id(0)
        dk_acc[...] = jnp.zeros_like(dk_acc)
        dv_acc[...] = jnp.zeros_like(dv_acc)
        for qi in range(nch):
            L = (qi + 1) * CH
            rows = slice(qi * CH, L)
            qc, doc = q_ref[rows, :], do_ref[rows, :]
            lse_c = lse_ref[rows, 0:1]
            dsum = jnp.sum(doc.astype(F32) * o_ref[rows, :].astype(F32), axis=1, keepdims=True)
            dq = 0.0
            for krows, s in _causal_scores(qc, k_ref, qi, CH, False):
                p = jnp.exp(s - lse_c)
                dp = lax.dot_general(doc, v_ref[krows, :], nt_dims, preferred_element_type=F32)
                ds = (p * (dp - dsum)).astype(BF)
                dq = dq + jnp.dot(ds, k_ref[krows, :], preferred_element_type=F32)
                dk_acc[krows, :] += lax.dot_general(ds, qc, tn_dims, preferred_element_type=F32)
                dv_acc[krows, :] += lax.dot_general(p.astype(BF), doc, tn_dims, preferred_element_type=F32)
            dq_ref[rows, :QK_NOPE] = (dq[:, :QK_NOPE] * scale).astype(BF)
            dyr = dq[:, QK_NOPE:]
            dq_ref[rows, QK_NOPE:] = ((dyr * c_ref[rows, :] - _rot_half(dyr * s_ref[rows, :])) * scale).astype(BF)
        dk_ref[:, :QK_NOPE] = dk_acc[:, :QK_NOPE].astype(BF)
        dk_ref[:, QK_NOPE:] = jnp.zeros((T, HEAD_PAD - QK_NOPE), BF)
        dv_ref[...] = dv_acc[...].astype(BF)

        @pl.when(h == 0)
        def _():
            dkr_ref[...] = dk_acc[:, QK_NOPE:]

        @pl.when(h > 0)
        def _():
            dkr_ref[...] += dk_acc[:, QK_NOPE:]

    hp = pl.BlockSpec((T, HEAD_PAD), lambda h: (0, h))
    hv = pl.BlockSpec((T, V_HEAD), lambda h: (0, h))
    tab = pl.BlockSpec((T, 128), lambda h: (0, 0))
    return pl.pallas_call(
        body, name=name, grid=(H,),
        in_specs=[hp, hp, hv, hv, hv, hv, tab, tab],
        out_specs=[hp, hp, hv, tab],
        out_shape=[jax.ShapeDtypeStruct((T, H * HEAD_PAD), BF), jax.ShapeDtypeStruct((T, H * HEAD_PAD), BF),
                   jax.ShapeDtypeStruct((T, H * V_HEAD), BF), jax.ShapeDtypeStruct((T, 128), F32)],
        scratch_shapes=[pltpu.VMEM((T, HEAD_PAD), F32), pltpu.VMEM((T, V_HEAD), F32)],
        compiler_params=_cparams(("arbitrary",)),
    )(q, k, v, o, do, lse, cos, sin)


def _place():
    x, y, c = lax.axis_index("x"), lax.axis_index("y"), lax.axis_index("c")
    return x, y, c


def _other_chips(x, y):
    return [(1 - x, y), (x, 1 - y), (1 - x, 1 - y)]


HBM_SPEC = pl.BlockSpec(memory_space=pltpu.HBM)
SEM_SPEC = pl.BlockSpec(memory_space=pltpu.SEMAPHORE)
DATAFLOW = pltpu.SideEffectType.DATAFLOW_SIDE_EFFECTING


def _hbm(a):
    return pltpu.with_memory_space_constraint(a, pltpu.HBM)


def _half(ref, lead, c, axis):
    rows, cols = ref.shape[-2], ref.shape[-1]
    if axis == 0:
        return ref.at[(*lead, pl.ds(c * (rows // 2), rows // 2))]
    return ref.at[(*lead, slice(None), pl.ds(c * (cols // 2), cols // 2))]


def copies_start(arrs, plan, n_sems, after, *, name):
    n = len(arrs)

    def body(*refs):
        outs = refs[n + 1:2 * n + 1]
        send_sems, recv_sems, token = refs[2 * n + 1:]
        for cp in plan(outs, send_sems, recv_sems, False):
            cp.start()
        token[...] = jnp.zeros_like(token)

    res = pl.pallas_call(
        body, name=name,
        out_shape=tuple(pltpu.HBM(a.shape, a.dtype) for a in arrs)
        + (pltpu.SemaphoreType.DMA((n_sems,)), pltpu.SemaphoreType.DMA((n_sems,)), jax.ShapeDtypeStruct((8, 128), F32)),
        in_specs=(HBM_SPEC,) * n + (pl.BlockSpec(memory_space=pl.ANY),),
        out_specs=(HBM_SPEC,) * n + (SEM_SPEC, SEM_SPEC, pl.BlockSpec(memory_space=pltpu.VMEM)),
        input_output_aliases={i: i for i in range(n)},
        compiler_params=pltpu.CompilerParams(has_side_effects=DATAFLOW),
    )(*[_hbm(a) for a in arrs], after)
    return list(res[:n]), res[n], res[n + 1], res[n + 2]


def copies_wait(arrs, plan, send_sems, recv_sems, after, *, name):
    n = len(arrs)
    afters = list(after) if isinstance(after, (list, tuple)) else [after]

    def body(*refs):
        for cp in plan(refs[:n], refs[n], refs[n + 1], True):
            cp.wait_send()
            cp.wait_recv()

    res = pl.pallas_call(
        body, name=name,
        out_shape=tuple(pltpu.HBM(a.shape, a.dtype) for a in arrs),
        in_specs=(HBM_SPEC,) * n + (SEM_SPEC, SEM_SPEC) + (pl.BlockSpec(memory_space=pl.ANY),) * len(afters),
        out_specs=(HBM_SPEC,) * n,
        input_output_aliases={i: i for i in range(n)},
        compiler_params=pltpu.CompilerParams(has_side_effects=DATAFLOW),
    )(*arrs, send_sems, recv_sems, *afters)
    return list(res)


def _half_part(ref, lead, c, axis, part):
    rows, cols = ref.shape[-2], ref.shape[-1]
    span = rows // 2 if axis == 0 else rows
    p1 = -(-(span // 2) // 16) * 16
    lo, n = (0, p1) if part == 0 else (p1, span - p1)
    if axis == 0:
        return ref.at[(*lead, pl.ds(c * span + lo, n))]
    return ref.at[(*lead, pl.ds(lo, n), pl.ds(c * (cols // 2), cols // 2))]


def near_plan(axes):
    def plan(bufs, send_sems, recv_sems, receiving):
        x, y, c = _place()
        me = 2 * x + y
        out = []
        for i, b in enumerate(bufs):
            for j, (cx, cy) in enumerate(_other_chips(x, y)[:2]):
                src = _half(b, (me,), c, axes[i])
                dst = _half(b, (2 * cx + cy,), c, axes[i]) if receiving else src
                out.append(pltpu.make_async_remote_copy(
                    src_ref=src, dst_ref=dst, send_sem=send_sems.at[2 * i + j], recv_sem=recv_sems.at[2 * i + j],
                    device_id=(cx, cy, c), device_id_type=MESH))
        return out
    return plan


def relay_plan(axes):
    def plan(bufs, send_sems, recv_sems, receiving):
        x, y, c = _place()
        (xn, yn, dg) = [2 * cx + cy for cx, cy in _other_chips(x, y)]
        to = [(x, 1 - y, c), (1 - x, y, c), (x, y, 1 - c), (x, y, 1 - c)]
        out = []
        for i, b in enumerate(bufs):
            if receiving:
                blks = [_half_part(b, (dg,), c, axes[i], 0), _half_part(b, (dg,), c, axes[i], 1),
                        _half(b, (xn,), 1 - c, axes[i]), _half(b, (yn,), 1 - c, axes[i])]
            else:
                blks = [_half_part(b, (xn,), c, axes[i], 0), _half_part(b, (yn,), c, axes[i], 1),
                        _half(b, (xn,), c, axes[i]), _half(b, (yn,), c, axes[i])]
            for k in range(4):
                out.append(pltpu.make_async_remote_copy(
                    src_ref=blks[k], dst_ref=blks[k], send_sem=send_sems.at[4 * i + k],
                    recv_sem=recv_sems.at[4 * i + k], device_id=to[k], device_id_type=MESH))
        return out
    return plan


def far_forward_plan(axes):
    def plan(bufs, send_sems, recv_sems, receiving):
        x, y, c = _place()
        dg = 2 * (1 - x) + (1 - y)
        out = []
        for i, b in enumerate(bufs):
            blk = _half(b, (dg,), (1 - c) if receiving else c, axes[i])
            out.append(pltpu.make_async_remote_copy(
                src_ref=blk, dst_ref=blk, send_sem=send_sems.at[i], recv_sem=recv_sems.at[i],
                device_id=(x, y, 1 - c), device_id_type=MESH))
        return out
    return plan


def share_plan(axes):
    def plan(bufs, send_sems, recv_sems, receiving):
        x, y, c = _place()
        out = []
        for i, b in enumerate(bufs):
            blk = _half(b, (), (1 - c) if receiving else c, axes[i])
            out.append(pltpu.make_async_remote_copy(
                src_ref=blk, dst_ref=blk, send_sem=send_sems.at[i], recv_sem=recv_sems.at[i],
                device_id=(x, y, 1 - c), device_id_type=MESH))
        return out
    return plan


def scatter_plan(n):
    def plan(arrs, send_sems, recv_sems, receiving):
        x, y, c = _place()
        out = []
        for i in range(n):
            for j, (cx, cy) in enumerate(_other_chips(x, y)):
                out.append(pltpu.make_async_remote_copy(
                    src_ref=arrs[i].at[2 * cx + cy], dst_ref=arrs[n + i].at[j], send_sem=send_sems.at[3 * i + j],
                    recv_sem=recv_sems.at[3 * i + j], device_id=(cx, cy, c), device_id_type=MESH))
        return out
    return plan


def exchange_plan(axes):
    n = len(axes)

    def plan(arrs, send_sems, recv_sems, receiving):
        x, y, c = _place()
        out = []
        for i in range(n):
            for k in range(N_CHIPS):
                out.append(pltpu.make_async_remote_copy(
                    src_ref=_half(arrs[i], (k,), 1 - c, axes[i]), dst_ref=arrs[n + i].at[k],
                    send_sem=send_sems.at[N_CHIPS * i + k], recv_sem=recv_sems.at[N_CHIPS * i + k],
                    device_id=(x, y, 1 - c), device_id_type=MESH))
        return out
    return plan


def devices_plan(arrs, send_sems, recv_sems, receiving):
    x, y, c = _place()
    out = []
    for k in range(1, N_DEV):
        px, py, pc = x ^ ((k >> 2) & 1), y ^ ((k >> 1) & 1), c ^ (k & 1)
        slot = (4 * px + 2 * py + pc) if receiving else (4 * x + 2 * y + c)
        out.append(pltpu.make_async_remote_copy(
            src_ref=arrs[0], dst_ref=arrs[1].at[slot], send_sem=send_sems.at[k - 1], recv_sem=recv_sems.at[k - 1],
            device_id=(px, py, pc), device_id_type=MESH))
    return out


def _row_tile(r, cols, itemsize, target_bytes=2 * 1024 * 1024):
    best = None
    for t in range(16, r + 1, 16):
        if r % t == 0 and t * cols * itemsize <= target_bytes:
            best = t
    return best if best is not None else r


def pair_sum(g, recv, half, axis, *, name):
    _, hr, hc = recv.shape
    tr = _row_tile(hr, hc, 4)
    nb = hr // tr
    if axis == 0:
        mine = pl.BlockSpec((None, tr, hc), lambda k, r, hf: (k, hf[0] * nb + r, 0))
    else:
        mine = pl.BlockSpec((None, tr, hc), lambda k, r, hf: (k, r, hf[0]))

    def body(half_ref, a_ref, b_ref, o_ref):
        o_ref[...] = (a_ref[...].astype(F32) + b_ref[...].astype(F32)).astype(BF)

    return pl.pallas_call(
        body, name=name,
        grid_spec=pltpu.PrefetchScalarGridSpec(
            num_scalar_prefetch=1, grid=(N_CHIPS, nb),
            in_specs=[mine, pl.BlockSpec((None, tr, hc), lambda k, r, hf: (k, r, 0))],
            out_specs=pl.BlockSpec((None, tr, hc), lambda k, r, hf: (k, r, 0))),
        out_shape=jax.ShapeDtypeStruct((N_CHIPS, hr, hc), BF),
        compiler_params=_cparams(("parallel", "parallel")),
    )(half, g, recv)


def chip_sum(s, recv, place, axis, *, name):
    _, hr, hc = s.shape
    tr = _row_tile(hr, hc, 4)
    nb = hr // tr
    if axis == 0:
        o_spec = pl.BlockSpec((tr, hc), lambda r, pc: (pc[1] * nb + r, 0))
        o_shape = (2 * hr, hc)
    else:
        o_spec = pl.BlockSpec((tr, hc), lambda r, pc: (r, pc[1]))
        o_shape = (hr, 2 * hc)

    def body(place_ref, a_ref, b_ref, o_ref):
        acc = a_ref[...].astype(F32)
        for j in range(3):
            acc = acc + b_ref[j].astype(F32)
        o_ref[...] = acc

    return pl.pallas_call(
        body, name=name,
        grid_spec=pltpu.PrefetchScalarGridSpec(
            num_scalar_prefetch=1, grid=(nb,),
            in_specs=[pl.BlockSpec((None, tr, hc), lambda r, pc: (pc[0], r, 0)),
                      pl.BlockSpec((3, tr, hc), lambda r, pc: (0, r, 0))],
            out_specs=o_spec),
        out_shape=jax.ShapeDtypeStruct(o_shape, F32),
        compiler_params=_cparams(("parallel",)),
    )(place, s, recv)


def sum_devices(v, *, name):
    _, R, C = v.shape

    def body(v_ref, o_ref):
        acc = v_ref[0]
        for k in range(1, N_DEV):
            acc = acc + v_ref[k]
        o_ref[...] = acc

    return pl.pallas_call(body, name=name, out_shape=jax.ShapeDtypeStruct((R, C), F32))(v)


def adamw(w, g, m, v, *, name):
    R, C = w.shape
    tr = _row_tile(R, C, 4, target_bytes=1024 * 1024)
    c1 = 1.0 / (1.0 - ADAM_B1 ** ADAM_STEP)
    c2 = 1.0 / (1.0 - ADAM_B2 ** ADAM_STEP)

    def body(w_ref, g_ref, m_ref, v_ref, go_ref, d_ref, nm_ref, nv_ref):
        gg = g_ref[...]
        go_ref[...] = gg
        nm = ADAM_B1 * m_ref[...] + (1.0 - ADAM_B1) * gg
        nv = ADAM_B2 * v_ref[...] + (1.0 - ADAM_B2) * (gg * gg)
        nm_ref[...] = nm
        nv_ref[...] = nv
        d_ref[...] = -ADAM_LR * ((nm * c1) / (jnp.sqrt(nv * c2) + ADAM_EPS) + ADAM_WD * w_ref[...])

    spec = pl.BlockSpec((tr, C), lambda i: (i, 0))
    return pl.pallas_call(
        body, name=name, grid=(R // tr,), in_specs=[spec] * 4, out_specs=[spec] * 4,
        out_shape=[jax.ShapeDtypeStruct((R, C), F32)] * 4, compiler_params=_cparams(("parallel",)),
    )(w, g, m, v)


def _rope_tables(T):
    pos = np.arange(T, dtype=np.float32)
    inv_freq = (ROPE_THETA ** (-np.arange(0, QK_ROPE, 2, dtype=np.float32) / QK_ROPE)).astype(np.float32)
    ang = pos[:, None] * inv_freq[None, :]
    z = np.zeros((T, 64), np.float32)
    cos = np.concatenate([np.cos(ang), np.cos(ang), z], axis=1).astype(np.float32)
    sin = np.concatenate([np.sin(ang), np.sin(ang), z], axis=1).astype(np.float32)
    return jnp.asarray(cos), jnp.asarray(sin)


def _padded_rows(length):
    T = -(-length // 16) * 16
    while True:
        for n in (4, 5, 6, 7, 8):
            if T % (16 * n) == 0:
                return T, n
        T += 16


def _pick(n, cands):
    for c in cands:
        if n % c == 0:
            return c
    return n


def kernel(x, meta_tokens, g_mix, w_in, b_glu, b_gate, w_dw, b_dw, g_conv_ln, b_conv_ln, w_conv_out, b_conv_out, g_q_lora, w_uq, g_kv_lora, w_uk, w_uv, w_attn_out, w_out, g_ffn, w_ffn_gate, w_ffn_up, w_ffn_down, g_final, loss_target, m_meta_tokens, m_g_mix, m_w_in, m_b_glu, m_b_gate, m_w_dw, m_b_dw, m_g_conv_ln, m_b_conv_ln, m_w_conv_out, m_b_conv_out, m_g_q_lora, m_w_uq, m_g_kv_lora, m_w_uk, m_w_uv, m_w_attn_out, m_w_out, m_g_ffn, m_w_ffn_gate, m_w_ffn_up, m_w_ffn_down, m_g_final, v_meta_tokens, v_g_mix, v_w_in, v_b_glu, v_b_gate, v_w_dw, v_b_dw, v_g_conv_ln, v_b_conv_ln, v_w_conv_out, v_b_conv_out, v_g_q_lora, v_w_uq, v_g_kv_lora, v_w_uk, v_w_uv, v_w_attn_out, v_w_out, v_g_ffn, v_w_ffn_gate, v_w_ffn_up, v_w_ffn_down, v_g_final):
    weights = dict(meta_tokens=meta_tokens, g_mix=g_mix, w_in=w_in, b_glu=b_glu, b_gate=b_gate, w_dw=w_dw, b_dw=b_dw,
                   g_conv_ln=g_conv_ln, b_conv_ln=b_conv_ln, w_conv_out=w_conv_out, b_conv_out=b_conv_out,
                   g_q_lora=g_q_lora, w_uq=w_uq, g_kv_lora=g_kv_lora, w_uk=w_uk, w_uv=w_uv, w_attn_out=w_attn_out,
                   w_out=w_out, g_ffn=g_ffn, w_ffn_gate=w_ffn_gate, w_ffn_up=w_ffn_up, w_ffn_down=w_ffn_down,
                   g_final=g_final)
    m_in = dict(meta_tokens=m_meta_tokens, g_mix=m_g_mix, w_in=m_w_in, b_glu=m_b_glu, b_gate=m_b_gate, w_dw=m_w_dw,
                b_dw=m_b_dw, g_conv_ln=m_g_conv_ln, b_conv_ln=m_b_conv_ln, w_conv_out=m_w_conv_out,
                b_conv_out=m_b_conv_out, g_q_lora=m_g_q_lora, w_uq=m_w_uq, g_kv_lora=m_g_kv_lora, w_uk=m_w_uk,
                w_uv=m_w_uv, w_attn_out=m_w_attn_out, w_out=m_w_out, g_ffn=m_g_ffn, w_ffn_gate=m_w_ffn_gate,
                w_ffn_up=m_w_ffn_up, w_ffn_down=m_w_ffn_down, g_final=m_g_final)
    v_in = dict(meta_tokens=v_meta_tokens, g_mix=v_g_mix, w_in=v_w_in, b_glu=v_b_glu, b_gate=v_b_gate, w_dw=v_w_dw,
                b_dw=v_b_dw, g_conv_ln=v_g_conv_ln, b_conv_ln=v_b_conv_ln, w_conv_out=v_w_conv_out,
                b_conv_out=v_b_conv_out, g_q_lora=v_g_q_lora, w_uq=v_w_uq, g_kv_lora=v_g_kv_lora, w_uk=v_w_uk,
                w_uv=v_w_uv, w_attn_out=v_w_attn_out, w_out=v_w_out, g_ffn=v_g_ffn, w_ffn_gate=v_w_ffn_gate,
                w_ffn_up=v_w_ffn_up, w_ffn_down=v_w_ffn_down, g_final=v_g_final)
    names = list(weights)

    seq, D = x.shape[1], x.shape[2]
    C = w_conv_out.shape[2]
    R = g_q_lora.shape[1]
    H = w_uv.shape[2] * N_CHIPS // V_HEAD
    FB = w_ffn_gate.shape[2]
    FF = FB * N_CHIPS
    length = N_META + seq
    T, nch = _padded_rows(length)
    unit = T // nch
    scale = (QK_NOPE + QK_ROPE) ** -0.5
    assert C == D and w_in.shape[2] * N_CHIPS == 2 * C + 2 * R + QK_ROPE + 2 * D

    xi, yi, ci = _place()
    chip = (2 * xi + yi).astype(jnp.int32)
    half_arr = ci.astype(jnp.int32).reshape(1)

    tr = unit
    tmh = T // 2 if (T // 2) % 16 == 0 else unit
    tmq = unit
    tmw = _pick(D, (1024, 512))
    rc = unit // 2 if unit % 16 == 0 else unit
    tcv = 128

    def tn_of(n, pref=512):
        return _pick(n, (pref, 384, 256, 128))

    grp_a = ["w_in"]
    grp_b = ["w_conv_out", "w_uq", "w_uk", "w_uv", "w_attn_out", "w_out"]
    grp_c = ["w_ffn_gate", "w_ffn_up", "w_ffn_down"]
    big = grp_a + grp_b + grp_c
    axes = {nm: 0 for nm in big}
    axes["w_in"] = 1
    shard = {nm: weights[nm][0] for nm in big}
    shard["w_in"] = jnp.swapaxes(w_in[0], 0, 1)

    def own_blocks(srcs, dtype=BF):
        bufs = []
        for w in srcs:
            wb = w.astype(dtype)
            bufs.append(lax.dynamic_update_slice(lax.empty((N_CHIPS,) + wb.shape, dtype), wb[None], (chip, 0, 0)))
        return bufs

    tiny = jnp.concatenate([meta_tokens, jnp.pad(w_dw[0], ((0, 1), (0, 0)))], axis=0)
    ax_a = [axes[nm] for nm in grp_a] + [0]
    bufs_a, ss_a, rs_a, tok_ga = copies_start(own_blocks([shard[nm] for nm in grp_a]) + own_blocks([tiny], F32),
                                              near_plan(ax_a), 2 * len(ax_a), half_arr, name="gather_a_near_start")
    tok_ga, held = lax.optimization_barrier((tok_ga, (x, loss_target, [shard[nm] for nm in grp_b + grp_c])))
    x_l, tgt_l, shard_bc = held
    h0 = jnp.pad(x_l[0], ((N_META, T - length), (0, 0)))
    tgt = jnp.pad(tgt_l[0], ((N_META, T - length), (0, 0)))
    bufs_b, bufs_c = own_blocks(shard_bc[:len(grp_b)]), own_blocks(shard_bc[len(grp_b):])
    cos, sin = _rope_tables(T)

    bufs_a = copies_wait(bufs_a, near_plan(ax_a), ss_a, rs_a, [h0, tgt] + bufs_b + bufs_c, name="gather_a_near_wait")
    bufs_a, ss_a, rs_a, _ = copies_start(bufs_a, relay_plan(ax_a), 4 * len(ax_a), half_arr,
                                         name="gather_a_relay_start")
    bufs_a = copies_wait(bufs_a, relay_plan(ax_a), ss_a, rs_a, half_arr, name="gather_a_relay_wait")
    bufs_a, ss_a, rs_a, _ = copies_start(bufs_a, far_forward_plan(ax_a), len(ax_a), half_arr,
                                         name="gather_a_far_start")
    bufs_a = copies_wait(bufs_a, far_forward_plan(ax_a), ss_a, rs_a, half_arr, name="gather_a_far_wait")
    G = dict(zip(grp_a, bufs_a))
    meta_full = jnp.concatenate([bufs_a[-1][k, :N_META] for k in range(N_CHIPS)], axis=1)
    w_dw_full = jnp.concatenate([bufs_a[-1][k, N_META:] for k in range(N_CHIPS)], axis=1)
    h0 = lax.dynamic_update_slice(h0, meta_full, (0, 0))
    ax_b = [axes[nm] for nm in grp_b]
    bufs_b, ss_b, rs_b, tok_gb = copies_start(bufs_b, near_plan(ax_b), 2 * len(grp_b), G["w_in"],
                                              name="gather_b_near_start")

    WT_in = G["w_in"].reshape(N_CHIPS * G["w_in"].shape[1], D)
    o1, o2 = 2 * C, 2 * C + 2 * R + QK_ROPE
    WT_small = jnp.pad(WT_in[o1:o2], ((0, 128 - QK_ROPE), (0, 0)))
    WT_gate = WT_in[o2:]

    u = rms_fwd(h0, g_mix, tr=tr, name="rms_mix", deps=[tok_gb])
    z_glu = mm(u, WT_in, mode="nt", tm=T, tn=tn_of(2 * C), tk=D, out_dtype=BF, name="mm_z_glu", n_out=o1)
    z_small = mm(u, WT_small, mode="nt", tm=T, tn=tn_of(2 * R + 128, 384), tk=D, out_dtype=F32, name="mm_z_small")
    z_gate = mm(u, WT_gate, mode="nt", tm=T, tn=tn_of(2 * D), tk=D, out_dtype=BF, name="mm_z_gate")

    bufs_b = copies_wait(bufs_b, near_plan(ax_b), ss_b, rs_b, z_glu, name="gather_b_near_wait")
    bufs_b, ss_b, rs_b, tok_gb = copies_start(bufs_b, relay_plan(ax_b), 4 * len(grp_b), half_arr,
                                              name="gather_b_relay_start")
    ax_c = [axes[nm] for nm in grp_c]
    bufs_c, ss_c, rs_c, tok_gc = copies_start(bufs_c, near_plan(ax_c), 2 * len(grp_c), tok_gb,
                                              name="gather_c_near_start")
    c0 = glu_fwd(z_glu, b_glu, tr=tr, name="glu_fwd", deps=[tok_gc])
    c1 = dwconv_fwd(c0, w_dw_full, b_dw, tc=tcv, rc=rc, name="dwconv_fwd")
    bufs_b = copies_wait(bufs_b, relay_plan(ax_b), ss_b, rs_b, c1, name="gather_b_relay_wait")
    bufs_b, ss_b, rs_b, tok_gb = copies_start(bufs_b, far_forward_plan(ax_b), len(grp_b), half_arr,
                                              name="gather_b_far_start")
    c3 = ln_silu_fwd(c1, g_conv_ln, b_conv_ln, tr=tr, name="ln_silu_fwd")
    cq, ckv, kr = lora_norm_fwd(z_small, g_q_lora, g_kv_lora, cos, sin, tr=tr, name="lora_norm_fwd", deps=[tok_gb])

    def cols_full(a):
        return jnp.concatenate([a[k] for k in range(N_CHIPS)], axis=1)

    G.update(zip(grp_b, copies_wait(bufs_b, far_forward_plan(ax_b), ss_b, rs_b, cq, name="gather_b_far_wait")))
    W_co = G["w_conv_out"].reshape(C, D)
    W_uq = cols_full(G["w_uq"]).reshape(R, H, QK_NOPE + QK_ROPE)
    Wq_p = jnp.pad(W_uq, ((0, 0), (0, 0), (0, HEAD_PAD - QK_NOPE - QK_ROPE))).reshape(R, H * HEAD_PAD)
    W_uk = cols_full(G["w_uk"])
    W_uv = cols_full(G["w_uv"])
    W_ao = G["w_attn_out"].reshape(H * V_HEAD, D)
    W_o = G["w_out"].reshape(D, D)

    y_conv = mm(c3, W_co, mode="nn", tm=T, tn=tn_of(D), tk=C, out_dtype=BF, name="mm_y_conv")

    q = q_proj(cq, Wq_p, cos, sin, scale=scale, name="q_proj")
    kk = k_proj(ckv, W_uk, kr, name="k_proj")
    vv = mm(ckv, W_uv, mode="nn", tm=T, tn=tn_of(H * V_HEAD), tk=R, out_dtype=BF, name="mm_v")
    o_attn, lse = attn_fwd(q, kk, vv, nch=nch, name="attn_fwd")
    bufs_c = copies_wait(bufs_c, near_plan(ax_c), ss_c, rs_c, o_attn, name="gather_c_near_wait")
    bufs_c, ss_c, rs_c, tok_gc = copies_start(bufs_c, relay_plan(ax_c), 4 * len(grp_c), half_arr,
                                              name="gather_c_relay_start")
    y_attn = mm(o_attn, W_ao, mode="nn", tm=T, tn=tn_of(D), tk=H * V_HEAD, out_dtype=BF, name="mm_y_attn",
                deps=[tok_gc])

    mix = mix_fwd(z_gate, y_conv, y_attn, b_gate, b_conv_out, tr=tr, name="mix_fwd")
    bufs_c = copies_wait(bufs_c, relay_plan(ax_c), ss_c, rs_c, mix, name="gather_c_relay_wait")
    bufs_c, ss_c, rs_c, tok_gc = copies_start(bufs_c, far_forward_plan(ax_c), len(grp_c), half_arr,
                                              name="gather_c_far_start")
    h1 = mm(mix, W_o, mode="nn", tm=T, tn=tn_of(D), tk=D, out_dtype=F32, name="mm_h1", res=h0, deps=[tok_gc])

    hn = rms_fwd(h1, g_ffn, tr=tr, name="rms_ffn")

    G.update(zip(grp_c, copies_wait(bufs_c, far_forward_plan(ax_c), ss_c, rs_c, hn, name="gather_c_far_wait")))
    W_fg, W_fu = G["w_ffn_gate"], G["w_ffn_up"]
    W_fd = G["w_ffn_down"].reshape(FF, D)

    fa, fb, f = ffn_fwd(hn, W_fg, W_fu, tm=tmq, name="ffn_fwd")
    h2 = mm(f, W_fd, mode="nn", tm=tmh, tn=tn_of(D, 256), tk=FF, out_dtype=F32, name="mm_h2", res=h1)

    place_arr = jnp.stack([chip, ci.astype(jnp.int32)])

    def col_blocks(a):
        r, cfull = a.shape
        return a.reshape(r, N_CHIPS, cfull // N_CHIPS).transpose(1, 0, 2)

    def rs_begin(nms, gls, after, tag):
        ax = [axes[nm] for nm in nms]
        lands = []
        for g, a in zip(gls, ax):
            _, r, cfull = g.shape
            lands.append(lax.empty((N_CHIPS, r // 2, cfull) if a == 0 else (N_CHIPS, r, cfull // 2), BF))
        arrs, ssem, rsem, tok = copies_start(list(gls) + lands, exchange_plan(ax), N_CHIPS * len(nms), after,
                                             name=f"rs_{tag}_exchange_start")
        return (nms, ax, arrs, ssem, rsem), tok

    def rs_middle(state, after, tag):
        nms, ax, arrs, ssem, rsem = state
        n = len(nms)
        arrs = copies_wait(arrs, exchange_plan(ax), ssem, rsem, after, name=f"rs_{tag}_exchange_wait")
        pair = [pair_sum(g, r, half_arr, a, name="rs_pair_sum_" + nm)
                for nm, g, r, a in zip(nms, arrs[:n], arrs[n:], ax)]
        lands = [lax.empty((3,) + p.shape[1:], BF) for p in pair]
        arrs, ssem, rsem, tok = copies_start(pair + lands, scatter_plan(n), 3 * n, half_arr,
                                             name=f"rs_{tag}_scatter_start")
        return (nms, ax, arrs, ssem, rsem), tok

    def rs_end(state, after, tag):
        nms, ax, arrs, ssem, rsem = state
        n = len(nms)
        arrs = copies_wait(arrs, scatter_plan(n), ssem, rsem, after, name=f"rs_{tag}_scatter_wait")
        return [chip_sum(s, r, place_arr, a, name="rs_chip_sum_" + nm)
                for nm, s, r, a in zip(nms, arrs[:n], arrs[n:], ax)]

    def rs_shared(nms, reds, after, tag):
        ax = [axes[nm] for nm in nms]
        reds, ssem, rsem, _ = copies_start(reds, share_plan(ax), len(nms), after, name=f"rs_{tag}_share_start")
        return dict(zip(nms, copies_wait(reds, share_plan(ax), ssem, rsem, half_arr, name=f"rs_{tag}_share_wait")))

    d_h2, d_h2b, loss_p, d_g_final = final_loss(h2, tgt, g_final.reshape(1, D), seq=seq, tr=tr, name="final_loss")
    loss = lax.psum(loss_p[0, 0], ("x", "y", "c"))

    dW_fd = mm(f, d_h2b, mode="tn", tm=FB, tn=tn_of(D), tk=T, out_dtype=BF, name="mm_dw_ffn_down")
    d_a, d_b = ffn_dact(d_h2b, W_fd, fa, fb, tm=tmq, name="ffn_dact")
    dW_fg = mm(hn, d_a, mode="tn", tm=tmw, tn=FB, tk=T, out_dtype=BF, name="mm_dw_ffn_gate", out_blocks=N_CHIPS)
    dW_fu = mm(hn, d_b, mode="tn", tm=tmw, tn=FB, tk=T, out_dtype=BF, name="mm_dw_ffn_up", out_blocks=N_CHIPS)
    st_c, tok_rc = rs_begin(grp_c, [dW_fg, dW_fu, dW_fd.reshape(N_CHIPS, FB, D)], half_arr, "c")
    d_hn = ffn_dhn(d_a, d_b, W_fg, W_fu, tm=tmq, tn=_pick(D, (256,)), name="ffn_dhn", deps=[tok_rc])
    st_c, tok_rc = rs_middle(st_c, d_hn, "c")
    d_h1, d_h1b, d_g_ffn = rms_bwd(d_hn, h1, g_ffn, d_h2, tr=tr, name="rms_ffn_bwd", deps=[tok_rc])

    d_mix = mm(d_h1b, W_o, mode="nt", tm=T, tn=tn_of(D), tk=D, out_dtype=BF, name="mm_d_mix")
    dW_o = mm(mix, d_h1b, mode="tn", tm=tmw, tn=tn_of(D), tk=T, out_dtype=BF, name="mm_dw_out")
    d_yc, d_ya, d_zgate, d_b_gate, d_b_co = mix_bwd(z_gate, y_conv, y_attn, d_mix, b_gate, b_conv_out, tr=tr,
                                                    name="mix_bwd")

    dW_co = mm(c3, d_yc, mode="tn", tm=tmw, tn=tn_of(D), tk=T, out_dtype=BF, name="mm_dw_conv_out")
    d_c3 = mm(d_yc, W_co, mode="nt", tm=T, tn=tn_of(C), tk=D, out_dtype=BF, name="mm_d_c3")
    d_c1, d_g_ln, d_b_ln, d_b_dw = ln_silu_bwd(c1, d_c3, g_conv_ln, b_conv_ln, tr=tr, name="ln_silu_bwd")
    d_c0, d_w_dw = dwconv_bwd(c0, d_c1, w_dw_full, tc=tcv, rc=rc, name="dwconv_bwd")
    d_zglu, d_b_glu = glu_bwd(z_glu, b_glu, d_c0, tr=tr, name="glu_bwd")

    dW_ao = mm(o_attn, d_ya, mode="tn", tm=_pick(H * V_HEAD, (1024, 512)), tn=tn_of(D), tk=T, out_dtype=BF, name="mm_dw_attn_out")
    d_o = mm(d_ya, W_ao, mode="nt", tm=T, tn=tn_of(H * V_HEAD), tk=D, out_dtype=BF, name="mm_d_o")
    d_q2, d_k2, d_v, d_kr = attn_bwd(q, kk, vv, o_attn, d_o, lse, cos, sin, scale=scale, nch=nch, name="attn_bwd")

    dWq_p = mm(cq, d_q2, mode="tn", tm=R, tn=tn_of(H * HEAD_PAD), tk=T, out_dtype=BF, name="mm_dw_uq")
    d_cq = mm(d_q2, Wq_p, mode="nt", tm=T, tn=R, tk=_pick(H * HEAD_PAD, (1024,)), out_dtype=F32, name="mm_d_cq")
    Wk_p = jnp.pad(W_uk.reshape(R, H, QK_NOPE), ((0, 0), (0, 0), (0, HEAD_PAD - QK_NOPE))).reshape(R, H * HEAD_PAD)
    dWk_p = mm(ckv, d_k2, mode="tn", tm=R, tn=tn_of(H * HEAD_PAD), tk=T, out_dtype=BF, name="mm_dw_uk")
    dW_uv = mm(ckv, d_v, mode="tn", tm=R, tn=tn_of(H * V_HEAD), tk=T, out_dtype=BF, name="mm_dw_uv")
    sh_c = rs_end(st_c, [dWq_p, dWk_p, dW_uv, dW_co, dW_ao, dW_o], "c")
    dW_uq = dWq_p.reshape(R, H, HEAD_PAD)[:, :, :QK_NOPE + QK_ROPE].reshape(R, H * (QK_NOPE + QK_ROPE))
    dW_uk = dWk_p.reshape(R, H, HEAD_PAD)[:, :, :QK_NOPE].reshape(R, H * QK_NOPE)
    st_b, tok_rb = rs_begin(grp_b, [dW_co.reshape(N_CHIPS, C // N_CHIPS, D), col_blocks(dW_uq), col_blocks(dW_uk),
                                    col_blocks(dW_uv), dW_ao.reshape(N_CHIPS, H * V_HEAD // N_CHIPS, D),
                                    dW_o.reshape(N_CHIPS, D // N_CHIPS, D)], sh_c[-1], "b")
    d_ckv = mm(d_k2, Wk_p, mode="nt", tm=T, tn=R, tk=_pick(H * HEAD_PAD, (1024,)), out_dtype=F32, name="mm_d_ckv_k",
               deps=[tok_rb])
    d_ckv = mm(d_v, W_uv, mode="nt", tm=T, tn=R, tk=_pick(H * V_HEAD, (1024,)), out_dtype=F32, name="mm_d_ckv_v",
               res=d_ckv)
    d_zsmall, d_g_q, d_g_kv = lora_norm_bwd(z_small, d_cq, d_ckv, d_kr, cos, sin, g_q_lora, g_kv_lora, tr=tr,
                                            name="lora_norm_bwd")
    st_b, tok_rb = rs_middle(st_b, d_zsmall, "b")

    tmi = _pick(2 * C, (1024, 512))
    dWT_in = lax.empty((WT_in.shape[0], D), BF)
    dWT_in = mm_tn_into(d_zglu, u, dWT_in, 0, tm=tmi, tn=tn_of(D), name="mm_dw_glu", deps=[tok_rb])
    dWT_in = mm_tn_into(d_zsmall, u, dWT_in, o1, tm=2 * R + 128, tn=tn_of(D), name="mm_dw_small")
    dWT_in = mm_tn_into(d_zgate, u, dWT_in, o2, tm=tmi, tn=tn_of(D), name="mm_dw_gate")
    sh_b = rs_end(st_b, dWT_in, "b")
    st_a, tok_ra = rs_begin(grp_a, [dWT_in.reshape(N_CHIPS, dWT_in.shape[0] // N_CHIPS, D)], sh_b[-1], "a")
    d_u = mm(d_zglu, WT_in, mode="nn", tm=T, tn=tn_of(D), tk=_pick(2 * C, (2048, 1024)), out_dtype=F32, name="mm_d_u_glu",
             deps=[tok_ra])
    st_a, tok_ra = rs_middle(st_a, d_u, "a")
    d_u = mm(d_zsmall, WT_small, mode="nn", tm=T, tn=tn_of(D), tk=2 * R + 128, out_dtype=F32, name="mm_d_u_small",
             res=d_u, deps=[tok_ra])
    d_u = mm(d_zgate, WT_gate, mode="nn", tm=T, tn=tn_of(D), tk=_pick(2 * D, (2048, 1024)), out_dtype=F32,
             name="mm_d_u_gate", res=d_u)
    d_h0, _, d_g_mix = rms_bwd(d_u, h0, g_mix, d_h1, tr=tr, name="rms_mix_bwd")

    grad_x = d_h0[N_META:length][None]

    small = ["g_mix", "b_glu", "b_gate", "b_dw", "g_conv_ln", "b_conv_ln", "b_conv_out", "g_q_lora", "g_kv_lora",
             "g_ffn", "g_final", "w_dw", "meta_tokens"]
    sgrads = dict(g_mix=d_g_mix, b_glu=d_b_glu, b_gate=d_b_gate, b_dw=d_b_dw, g_conv_ln=d_g_ln, b_conv_ln=d_b_ln,
                  b_conv_out=d_b_co, g_q_lora=d_g_q, g_kv_lora=d_g_kv, g_ffn=d_g_ffn, g_final=d_g_final,
                  w_dw=d_w_dw[:CONV_WIDTH], meta_tokens=d_h0[:N_META])
    sizes = [int(np.prod(sgrads[nm].shape)) for nm in small]
    packed = jnp.concatenate([sgrads[nm].reshape(-1) for nm in small]).reshape(-1, 128)
    sm_arrs, sm_ss, sm_rs, tok_sm = copies_start([packed, lax.empty((N_DEV,) + packed.shape, F32)], devices_plan,
                                                 N_DEV - 1, packed, name="gather_small_start")

    grads, delta, new_m, new_v, delta_2d = {}, {}, {}, {}, {}

    def adamw_2d(nm, w2, g2, m2, v2, back):
        g_, d_, m_, v_ = adamw(w2, g2, m2, v2, name="adamw_" + nm)
        delta_2d[nm] = d_
        grads[nm], delta[nm], new_m[nm], new_v[nm] = back(g_), back(d_), back(m_), back(v_)

    reduced = rs_shared(grp_b + grp_c, sh_b + sh_c, tok_sm, "bc")
    for nm in grp_c + grp_b + grp_a:
        if nm == grp_a[0]:
            reduced.update(rs_shared(grp_a, rs_end(st_a, [delta_2d[k] for k in grp_c + grp_b], "a"), half_arr, "a"))
        shp = weights[nm].shape
        if axes[nm] == 1:
            adamw_2d(nm, shard[nm], reduced[nm], jnp.swapaxes(m_in[nm][0], 0, 1), jnp.swapaxes(v_in[nm][0], 0, 1),
                     lambda a: jnp.swapaxes(a, 0, 1)[None])
        else:
            two = (shp[-2], shp[-1])
            adamw_2d(nm, weights[nm].reshape(two), reduced[nm], m_in[nm].reshape(two), v_in[nm].reshape(two),
                     lambda a, shp=shp: a.reshape(shp))

    sm_arrs = copies_wait(sm_arrs, devices_plan, sm_ss, sm_rs, delta_2d[grp_a[0]], name="gather_small_wait")
    me = (4 * xi + 2 * yi + ci).astype(jnp.int32)
    parts = lax.dynamic_update_slice(sm_arrs[1], sm_arrs[0][None], (me, 0, 0))
    summed = sum_devices(parts, name="sum_small_grads").reshape(-1)
    offs = np.concatenate([[0], np.cumsum(sizes)])
    sfull = {nm: summed[int(offs[i]):int(offs[i + 1])].reshape(sgrads[nm].shape) for i, nm in enumerate(small)}
    for nm in small:
        gfull = sfull[nm]
        if nm == "w_dw":
            cb = C // N_CHIPS
            grads[nm] = lax.dynamic_slice(gfull, (0, chip * cb), (CONV_WIDTH, cb))[None]
        elif nm == "meta_tokens":
            cb = D // N_CHIPS
            grads[nm] = lax.dynamic_slice(gfull, (0, chip * cb), (N_META, cb))
        else:
            grads[nm] = gfull.reshape(weights[nm].shape)

    rep = [nm for nm in small if nm not in ("w_dw", "meta_tokens")]

    def pack(d):
        return jnp.concatenate([d[nm].reshape(-1) for nm in rep]).reshape(-1, 128)

    _, pd, pm, pv = adamw(pack(weights), pack(grads), pack(m_in), pack(v_in), name="adamw_small")
    rsz = [int(np.prod(weights[nm].shape)) for nm in rep]
    roff = np.concatenate([[0], np.cumsum(rsz)])
    for i, nm in enumerate(rep):
        sl = slice(int(roff[i]), int(roff[i + 1]))
        delta[nm] = pd.reshape(-1)[sl].reshape(weights[nm].shape)
        new_m[nm] = pm.reshape(-1)[sl].reshape(weights[nm].shape)
        new_v[nm] = pv.reshape(-1)[sl].reshape(weights[nm].shape)
    for nm in ["w_dw", "meta_tokens"]:
        shp = weights[nm].shape
        two = (shp[-2], shp[-1])
        adamw_2d(nm, weights[nm].reshape(two), grads[nm].reshape(two), m_in[nm].reshape(two), v_in[nm].reshape(two),
                 lambda a, shp=shp: a.reshape(shp))

    return (loss, grad_x, *[grads[nm] for nm in names], *[delta[nm] for nm in names],
            *[new_m[nm] for nm in names], *[new_v[nm] for nm in names])
```

```python
import functools

import numpy as np
import jax
import jax.numpy as jnp
from jax import lax
from jax.experimental import pallas as pl
from jax.experimental.pallas import tpu as pltpu

F32 = jnp.float32
BF = jnp.bfloat16
MESH = pl.DeviceIdType.MESH

N_META = 16
CONV_WIDTH = 31
CONV_PAD = 32
QK_NOPE = 128
QK_ROPE = 64
V_HEAD = 128
HEAD_PAD = 256
ROPE_THETA = 10000.0
EPS = 1e-6
ADAM_LR = 0.001
ADAM_B1 = 0.9
ADAM_B2 = 0.999
ADAM_EPS = 1e-08
ADAM_WD = 0.01
ADAM_STEP = 10
VMEM_LIMIT = 56 * 1024 * 1024
N_CHIPS = 4
N_DEV = 8


def _cparams(sem):
    return pltpu.CompilerParams(dimension_semantics=sem, vmem_limit_bytes=VMEM_LIMIT)


def mm(a, b, *, mode, tm, tn, tk, out_dtype, name, res=None, out_blocks=None, deps=(), n_out=None):
    b3 = b.ndim == 3
    if mode == "nn":
        M, K = a.shape
        N = b.shape[0] * b.shape[2] if b3 else b.shape[1]
        a_spec = pl.BlockSpec((tm, tk), lambda i, j, k: (i, k))
        if b3:
            per = b.shape[2] // tn
            b_spec = pl.BlockSpec((None, tk, tn), lambda i, j, k: (j // per, k, j % per))
        else:
            b_spec = pl.BlockSpec((tk, tn), lambda i, j, k: (k, j))
        dims = (((1,), (0,)), ((), ()))
    elif mode == "nt":
        M, K = a.shape
        N = n_out if n_out is not None else (b.shape[1] if b3 else b.shape[0])
        a_spec = pl.BlockSpec((tm, tk), lambda i, j, k: (i, k))
        if b3:
            per = b.shape[2] // tk
            b_spec = pl.BlockSpec((None, tn, tk), lambda i, j, k: (k // per, j, k % per))
        else:
            b_spec = pl.BlockSpec((tn, tk), lambda i, j, k: (j, k))
        dims = (((1,), (1,)), ((), ()))
    else:
        K, M = a.shape
        N = b.shape[1]
        a_spec = pl.BlockSpec((tk, tm), lambda i, j, k: (k, i))
        b_spec = pl.BlockSpec((tk, tn), lambda i, j, k: (k, j))
        dims = (((0,), (0,)), ((), ()))
    gm, gn, gk = M // tm, N // tn, K // tk
    assert gm * tm == M and gn * tn == N and gk * tk == K, (name, a.shape, b.shape, tm, tn, tk)
    if out_blocks is None:
        o_spec = pl.BlockSpec((tm, tn), lambda i, j, k: (i, j))
        o_shape = (M, N)
    else:
        nbw = N // out_blocks
        per_o = nbw // tn
        assert per_o * tn == nbw
        o_spec = pl.BlockSpec((None, tm, tn), lambda i, j, k: (j // per_o, i, j % per_o))
        o_shape = (out_blocks, M, nbw)
    has_res = res is not None

    def body(*refs):
        a_ref, b_ref = refs[0], refs[1]
        r_ref = refs[2] if has_res else None
        o_ref = refs[2 + has_res + len(deps)]
        acc = refs[-1]
        p = lax.dot_general(a_ref[...], b_ref[...], dims, preferred_element_type=F32)

        def finish(v):
            if has_res:
                v = v + r_ref[...]
            o_ref[...] = v.astype(o_ref.dtype)

        if gk == 1:
            finish(p)
        else:
            k = pl.program_id(2)

            @pl.when(k == 0)
            def _():
                acc[...] = p

            @pl.when(k > 0)
            def _():
                acc[...] += p

            @pl.when(k == gk - 1)
            def _():
                finish(acc[...])

    in_specs = [a_spec, b_spec]
    args = [a, b]
    if has_res:
        in_specs.append(pl.BlockSpec((tm, tn), lambda i, j, k: (i, j)))
        args.append(res)
    in_specs += [pl.BlockSpec(memory_space=pl.ANY)] * len(deps)
    args += list(deps)
    return pl.pallas_call(
        body, name=name, grid=(gm, gn, gk), in_specs=in_specs, out_specs=o_spec,
        out_shape=jax.ShapeDtypeStruct(o_shape, out_dtype),
        scratch_shapes=[pltpu.VMEM((tm, tn), F32)] if gk > 1 else [],
        compiler_params=_cparams(("parallel", "parallel", "arbitrary")),
    )(*args)


def mm_tn_into(a, b, buf, row0, *, tm, tn, name, deps=()):
    K, M = a.shape
    N = b.shape[1]
    gm, gn = M // tm, N // tn
    assert gm * tm == M and gn * tn == N and buf.shape[1] == N and row0 + M <= buf.shape[0]
    steps = gm * gn

    def body(a_ref, b_ref, buf_ref, *rest):
        o_ref, tile, sem = rest[len(deps):]
        i, j = pl.program_id(0), pl.program_id(1)
        s = i * gn + j
        slot = s % 2

        def out_copy(sl):
            return pltpu.make_async_copy(tile.at[sl], o_ref.at[pl.ds(row0 + i * tm, tm), pl.ds(j * tn, tn)],
                                         sem.at[sl])

        @pl.when(s >= 2)
        def _():
            out_copy(slot).wait()

        tile[slot] = lax.dot_general(a_ref[...], b_ref[...], (((0,), (0,)), ((), ())),
                                     preferred_element_type=F32).astype(tile.dtype)
        out_copy(slot).start()

        @pl.when(s == steps - 1)
        def _():
            out_copy(slot).wait()
            if steps >= 2:
                out_copy(1 - slot).wait()

    anyspec = pl.BlockSpec(memory_space=pl.ANY)
    return pl.pallas_call(
        body, name=name, grid=(gm, gn),
        in_specs=[pl.BlockSpec((K, tm), lambda i, j: (0, i)), pl.BlockSpec((K, tn), lambda i, j: (0, j)), anyspec]
        + [anyspec] * len(deps),
        out_specs=anyspec, out_shape=jax.ShapeDtypeStruct(buf.shape, buf.dtype),
        input_output_aliases={2: 0},
        scratch_shapes=[pltpu.VMEM((2, tm, tn), buf.dtype), pltpu.SemaphoreType.DMA((2,))],
        compiler_params=_cparams(("arbitrary", "arbitrary")),
    )(a, b, buf, *deps)


def rowwise(body, row_ins, full_ins, row_outs, acc_outs, *, tr, name, deps=()):
    T = row_ins[0].shape[0]
    assert T % tr == 0, (name, T, tr)
    n_ri, n_fi, n_ro = len(row_ins), len(full_ins), len(row_outs)
    n_in = n_ri + n_fi + len(deps)

    def kern(*refs):
        body(pl.program_id(0), refs[:n_ri], refs[n_ri:n_ri + n_fi], refs[n_in:n_in + n_ro], refs[n_in + n_ro:])

    in_specs = [pl.BlockSpec((tr, a.shape[1]), lambda i: (i, 0)) for a in row_ins]
    in_specs += [pl.BlockSpec(a.shape, lambda i: (0, 0)) for a in full_ins]
    in_specs += [pl.BlockSpec(memory_space=pl.ANY)] * len(deps)
    out_specs = [pl.BlockSpec((tr, c), lambda i: (i, 0)) for c, _ in row_outs]
    out_specs += [pl.BlockSpec(s, lambda i: (0, 0)) for s in acc_outs]
    out_shape = [jax.ShapeDtypeStruct((T, c), d) for c, d in row_outs]
    out_shape += [jax.ShapeDtypeStruct(s, F32) for s in acc_outs]
    return pl.pallas_call(
        kern, name=name, grid=(T // tr,), in_specs=in_specs, out_specs=out_specs, out_shape=out_shape,
        compiler_params=_cparams(("arbitrary",)),
    )(*row_ins, *full_ins, *deps)


def _acc(step, ref, val):
    @pl.when(step == 0)
    def _():
        ref[...] = val

    @pl.when(step > 0)
    def _():
        ref[...] += val


def _colsum(x):
    return jnp.sum(x, axis=0, keepdims=True)


def _sigmoid(x):
    return 1.0 / (1.0 + jnp.exp(-x))


def _rot_half(v):
    lane = lax.broadcasted_iota(jnp.int32, v.shape, 1)
    lo = -pltpu.roll(v, 96, 1)
    hi = pltpu.roll(v, 32, 1)
    return jnp.where(lane < 32, lo, jnp.where(lane < 64, hi, 0.0))


def rms_fwd(h, g, *, tr, name, deps=()):
    def body(step, ri, fi, ro, ao):
        x = ri[0][...]
        r = lax.rsqrt(jnp.mean(x * x, axis=-1, keepdims=True) + EPS)
        ro[0][...] = ((x * r) * fi[0][...]).astype(BF)

    return rowwise(body, [h], [g], [(h.shape[1], BF)], [], tr=tr, name=name, deps=deps)[0]


def rms_bwd(dy, h, g, dres, *, tr, name, deps=()):
    D = h.shape[1]

    def body(step, ri, fi, ro, ao):
        d, x, dr = ri[0][...].astype(F32), ri[1][...], ri[2][...]
        r = lax.rsqrt(jnp.mean(x * x, axis=-1, keepdims=True) + EPS)
        n = x * r
        _acc(step, ao[0], _colsum(d * n))
        dn = d * fi[0][...]
        dh = r * (dn - n * jnp.mean(dn * n, axis=-1, keepdims=True)) + dr
        ro[0][...] = dh
        ro[1][...] = dh.astype(BF)

    return rowwise(body, [dy, h, dres], [g], [(D, F32), (D, BF)], [(1, D)], tr=tr, name=name, deps=deps)


def glu_fwd(z, b, *, tr, name, deps=()):
    C = z.shape[1] // 2

    def body(step, ri, fi, ro, ao):
        zz = ri[0][...].astype(F32) + fi[0][...]
        ro[0][...] = zz[:, :C] * _sigmoid(zz[:, C:])

    return rowwise(body, [z], [b], [(C, F32)], [], tr=tr, name=name, deps=deps)[0]


def glu_bwd(z, b, dc, *, tr, name):
    C = z.shape[1] // 2

    def body(step, ri, fi, ro, ao):
        zz = ri[0][...].astype(F32) + fi[0][...]
        d = ri[1][...]
        za, sg = zz[:, :C], _sigmoid(zz[:, C:])
        dza = d * sg
        dzb = d * za * sg * (1.0 - sg)
        ro[0][:, :C] = dza.astype(BF)
        ro[0][:, C:] = dzb.astype(BF)
        _acc(step, ao[0], _colsum(dza))
        _acc(step, ao[1], _colsum(dzb))

    dz, da, db = rowwise(body, [z, dc], [b], [(2 * C, BF)], [(1, C), (1, C)], tr=tr, name=name)
    return dz, jnp.concatenate([da, db], axis=1)


def ln_silu_fwd(c1, g, b, *, tr, name):
    def body(step, ri, fi, ro, ao):
        x = ri[0][...]
        mu = jnp.mean(x, axis=-1, keepdims=True)
        xc = x - mu
        rstd = lax.rsqrt(jnp.mean(xc * xc, axis=-1, keepdims=True) + EPS)
        y = (xc * rstd) * fi[0][...] + fi[1][...]
        ro[0][...] = (y * _sigmoid(y)).astype(BF)

    return rowwise(body, [c1], [g, b], [(c1.shape[1], BF)], [], tr=tr, name=name)[0]


def ln_silu_bwd(c1, dc3, g, b, *, tr, name):
    C = c1.shape[1]

    def body(step, ri, fi, ro, ao):
        x, d3 = ri[0][...], ri[1][...].astype(F32)
        mu = jnp.mean(x, axis=-1, keepdims=True)
        xc = x - mu
        rstd = lax.rsqrt(jnp.mean(xc * xc, axis=-1, keepdims=True) + EPS)
        n = xc * rstd
        y = n * fi[0][...] + fi[1][...]
        sg = _sigmoid(y)
        dy = d3 * (sg * (1.0 + y * (1.0 - sg)))
        _acc(step, ao[0], _colsum(dy * n))
        _acc(step, ao[1], _colsum(dy))
        dn = dy * fi[0][...]
        dx = rstd * (dn - jnp.mean(dn, axis=-1, keepdims=True) - n * jnp.mean(dn * n, axis=-1, keepdims=True))
        ro[0][...] = dx
        _acc(step, ao[2], _colsum(dx))

    return rowwise(body, [c1, dc3], [g, b], [(C, F32)], [(1, C), (1, C), (1, C)], tr=tr, name=name)


def lora_norm_fwd(zs, gq, gkv, cos, sin, *, tr, name, deps=()):
    R = gq.shape[1]

    def body(step, ri, fi, ro, ao):
        z = ri[0][...]
        for o, gi in ((0, 0), (1, 1)):
            x = z[:, o * R:(o + 1) * R]
            r = lax.rsqrt(jnp.mean(x * x, axis=-1, keepdims=True) + EPS)
            ro[o][...] = ((x * r) * fi[gi][...]).astype(BF)
        kr = z[:, 2 * R:2 * R + 128]
        ro[2][...] = kr * ri[1][...] + _rot_half(kr) * ri[2][...]

    return rowwise(body, [zs, cos, sin], [gq, gkv], [(R, BF), (R, BF), (128, F32)], [], tr=tr, name=name, deps=deps)


def lora_norm_bwd(zs, dcq, dckv, dkr, cos, sin, gq, gkv, *, tr, name):
    R = gq.shape[1]

    def body(step, ri, fi, ro, ao):
        z = ri[0][...]
        for o in (0, 1):
            x = z[:, o * R:(o + 1) * R]
            d = ri[1 + o][...]
            r = lax.rsqrt(jnp.mean(x * x, axis=-1, keepdims=True) + EPS)
            n = x * r
            _acc(step, ao[o], _colsum(d * n))
            dn = d * fi[o][...]
            ro[0][:, o * R:(o + 1) * R] = (r * (dn - n * jnp.mean(dn * n, axis=-1, keepdims=True))).astype(BF)
        dk = ri[3][...]
        ro[0][:, 2 * R:2 * R + 128] = (dk * ri[4][...] - _rot_half(dk * ri[5][...])).astype(BF)

    return rowwise(body, [zs, dcq, dckv, dkr, cos, sin], [gq, gkv], [(2 * R + 128, BF)], [(1, R), (1, R)],
                   tr=tr, name=name)


def mix_fwd(zg, yc, ya, bg, bco, *, tr, name):
    D = yc.shape[1]

    def body(step, ri, fi, ro, ao):
        g = _sigmoid(ri[0][...].astype(F32) + fi[0][...])
        ro[0][...] = (g[:, :D] * (ri[1][...].astype(F32) + fi[1][...]) + g[:, D:] * ri[2][...].astype(F32)).astype(BF)

    return rowwise(body, [zg, yc, ya], [bg, bco], [(D, BF)], [], tr=tr, name=name)[0]


def mix_bwd(zg, yc, ya, dmix, bg, bco, *, tr, name):
    D = yc.shape[1]

    def body(step, ri, fi, ro, ao):
        g = _sigmoid(ri[0][...].astype(F32) + fi[0][...])
        gc, ga = g[:, :D], g[:, D:]
        ycv = ri[1][...].astype(F32) + fi[1][...]
        yav = ri[2][...].astype(F32)
        dm = ri[3][...].astype(F32)
        dyc = dm * gc
        ro[0][...] = dyc.astype(BF)
        ro[1][...] = (dm * ga).astype(BF)
        dzc = dm * ycv * gc * (1.0 - gc)
        dza = dm * yav * ga * (1.0 - ga)
        ro[2][:, :D] = dzc.astype(BF)
        ro[2][:, D:] = dza.astype(BF)
        _acc(step, ao[0], _colsum(dzc))
        _acc(step, ao[1], _colsum(dza))
        _acc(step, ao[2], _colsum(dyc))

    dyc, dya, dzg, dbc, dba, dbo = rowwise(body, [zg, yc, ya, dmix], [bg, bco], [(D, BF), (D, BF), (2 * D, BF)],
                                           [(1, D), (1, D), (1, D)], tr=tr, name=name)
    return dyc, dya, dzg, jnp.concatenate([dbc, dba], axis=1), dbo


def ffn_fwd(hn, wg, wu, *, tm, name):
    T, D = hn.shape
    nb, _, FB = wg.shape
    assert T % tm == 0

    def body(h_ref, g_ref, u_ref, a_ref, b_ref, f_ref):
        h = h_ref[...]
        a = jnp.dot(h, g_ref[...], preferred_element_type=F32)
        b = jnp.dot(h, u_ref[...], preferred_element_type=F32)
        a_ref[...] = a
        b_ref[...] = b
        f_ref[...] = (a * _sigmoid(a) * b).astype(BF)

    wspec = pl.BlockSpec((None, D, FB), lambda j, i: (j, 0, 0))
    ospec = pl.BlockSpec((tm, FB), lambda j, i: (i, j))
    return pl.pallas_call(
        body, name=name, grid=(nb, T // tm),
        in_specs=[pl.BlockSpec((tm, D), lambda j, i: (i, 0)), wspec, wspec], out_specs=[ospec, ospec, ospec],
        out_shape=[jax.ShapeDtypeStruct((T, nb * FB), F32), jax.ShapeDtypeStruct((T, nb * FB), F32),
                   jax.ShapeDtypeStruct((T, nb * FB), BF)],
        compiler_params=_cparams(("parallel", "parallel")),
    )(hn, wg, wu)


def ffn_dact(dh, wd, a, b, *, tm, name):
    T, D = dh.shape
    FF = wd.shape[0]
    FB = FF // N_CHIPS

    def body(dh_ref, w_ref, a_ref, b_ref, da_ref, db_ref):
        d = lax.dot_general(dh_ref[...], w_ref[...], (((1,), (1,)), ((), ())), preferred_element_type=F32)
        x, u = a_ref[...], b_ref[...]
        sg = _sigmoid(x)
        da_ref[...] = (d * u * (sg * (1.0 + x * (1.0 - sg)))).astype(BF)
        db_ref[...] = (d * (x * sg)).astype(BF)

    blk = pl.BlockSpec((tm, FB), lambda j, i: (i, j))
    return pl.pallas_call(
        body, name=name, grid=(N_CHIPS, T // tm),
        in_specs=[pl.BlockSpec((tm, D), lambda j, i: (i, 0)), pl.BlockSpec((FB, D), lambda j, i: (j, 0)), blk, blk],
        out_specs=[blk, blk],
        out_shape=[jax.ShapeDtypeStruct((T, FF), BF), jax.ShapeDtypeStruct((T, FF), BF)],
        compiler_params=_cparams(("parallel", "parallel")),
    )(dh, wd, a, b)


def ffn_dhn(da, db, wg, wu, *, tm, tn, name, deps=()):
    T, FF = da.shape
    nb, D, FB = wg.shape
    dims = (((1,), (1,)), ((), ()))

    def body(da_ref, db_ref, g_ref, u_ref, *rest):
        o_ref = rest[len(deps)]
        acc = None
        for k in range(nb):
            cols = slice(k * FB, (k + 1) * FB)
            p = lax.dot_general(da_ref[:, cols], g_ref[k], dims, preferred_element_type=F32)
            p = p + lax.dot_general(db_ref[:, cols], u_ref[k], dims, preferred_element_type=F32)
            acc = p if acc is None else acc + p
        o_ref[...] = acc.astype(BF)

    aspec = pl.BlockSpec((tm, FF), lambda i, j: (i, 0))
    wspec = pl.BlockSpec((nb, tn, FB), lambda i, j: (0, j, 0))
    return pl.pallas_call(
        body, name=name, grid=(T // tm, D // tn),
        in_specs=[aspec, aspec, wspec, wspec] + [pl.BlockSpec(memory_space=pl.ANY)] * len(deps),
        out_specs=pl.BlockSpec((tm, tn), lambda i, j: (i, j)),
        out_shape=jax.ShapeDtypeStruct((T, D), BF),
        compiler_params=_cparams(("parallel", "parallel")),
    )(da, db, wg, wu, *deps)


def final_loss(h2, tgt, g, *, seq, tr, name):
    D = h2.shape[1]

    def body(step, ri, fi, ro, ao):
        x = ri[0][...]
        r = lax.rsqrt(jnp.mean(x * x, axis=-1, keepdims=True) + EPS)
        n = x * r
        gg = fi[0][...]
        row = lax.broadcasted_iota(jnp.int32, (tr, 1), 0) + step * tr
        live = jnp.logical_and(row >= N_META, row < N_META + seq)
        e = jnp.where(live, n * gg - ri[1][...], 0.0)
        _acc(step, ao[0], jnp.broadcast_to(0.5 * jnp.sum(jnp.mean(e * e, axis=-1, keepdims=True)), (1, 128)))
        dy = e * (1.0 / D)
        _acc(step, ao[1], _colsum(dy * n))
        dn = dy * gg
        dh = r * (dn - n * jnp.mean(dn * n, axis=-1, keepdims=True))
        ro[0][...] = dh
        ro[1][...] = dh.astype(BF)

    return rowwise(body, [h2, tgt], [g], [(D, F32), (D, BF)], [(1, 128), (1, D)], tr=tr, name=name)


def _shifted_copies(s_ref, T, rc):
    for r in range(1, 8):
        for t0 in range(0, T, rc):
            s_ref[r, t0:t0 + rc, :] = s_ref[0, t0 + r:t0 + r + rc, :]
        s_ref[r, T:T + CONV_PAD - 8, :] = s_ref[0, T + r:T + r + CONV_PAD - 8, :]


def _shifted(s_ref, start, rows):
    return s_ref[start % 8, start - start % 8:start - start % 8 + rows, :]


def dwconv_fwd(x, w, b, *, tc, rc, name, deps=()):
    T, C = x.shape
    nchunk = T // rc
    assert nchunk * rc == T and C % tc == 0

    def body(x_ref, w_ref, b_ref, *rest):
        y_ref, xs_ref = rest[len(deps):]
        xs_ref[0, 0:CONV_PAD, :] = jnp.zeros((CONV_PAD, tc), F32)
        xs_ref[0, CONV_PAD:CONV_PAD + T, :] = x_ref[...]
        _shifted_copies(xs_ref, T, rc)
        off = CONV_PAD - (CONV_WIDTH - 1)
        for ci in range(nchunk):
            t0 = ci * rc
            acc = jnp.broadcast_to(b_ref[...], (rc, tc))
            for j in range(CONV_WIDTH):
                acc = acc + w_ref[j:j + 1, :] * _shifted(xs_ref, t0 + off + j, rc)
            y_ref[t0:t0 + rc, :] = acc

    return pl.pallas_call(
        body, name=name, grid=(C // tc,),
        in_specs=[pl.BlockSpec((T, tc), lambda j: (0, j)), pl.BlockSpec((32, tc), lambda j: (0, j)),
                  pl.BlockSpec((1, tc), lambda j: (0, j))] + [pl.BlockSpec(memory_space=pl.ANY)] * len(deps),
        out_specs=pl.BlockSpec((T, tc), lambda j: (0, j)),
        out_shape=jax.ShapeDtypeStruct((T, C), F32),
        scratch_shapes=[pltpu.VMEM((8, T + CONV_PAD, tc), F32)],
        compiler_params=_cparams(("parallel",)),
    )(x, w, b, *deps)


def dwconv_bwd(x, dy, w, *, tc, rc, name):
    T, C = x.shape
    nchunk = T // rc
    assert nchunk * rc == T and C % tc == 0 and rc % 8 == 0

    def body(x_ref, dy_ref, w_ref, dx_ref, dw_ref, xs_ref, ds_ref):
        off = CONV_PAD - (CONV_WIDTH - 1)
        xs_ref[0, 0:CONV_PAD, :] = jnp.zeros((CONV_PAD, tc), F32)
        xs_ref[0, CONV_PAD:CONV_PAD + T, :] = x_ref[...]
        _shifted_copies(xs_ref, T, rc)
        ds_ref[0, 0:T, :] = dy_ref[...]
        ds_ref[0, T:T + CONV_PAD, :] = jnp.zeros((CONV_PAD, tc), F32)
        _shifted_copies(ds_ref, T, rc)
        for ci in range(nchunk):
            t0 = ci * rc
            acc = jnp.zeros((rc, tc), F32)
            for j in range(CONV_WIDTH):
                acc = acc + w_ref[j:j + 1, :] * _shifted(ds_ref, t0 + (CONV_WIDTH - 1) - j, rc)
            dx_ref[t0:t0 + rc, :] = acc
        dw_ref[...] = jnp.zeros((32, tc), F32)
        for j in range(CONV_WIDTH):
            acc = jnp.zeros((8, tc), F32)
            for ci in range(nchunk):
                t0 = ci * rc
                pr = dy_ref[t0:t0 + rc, :] * _shifted(xs_ref, t0 + off + j, rc)
                acc = acc + jnp.sum(pr.reshape(rc // 8, 8, tc), axis=0)
            dw_ref[j:j + 1, :] = jnp.sum(acc, axis=0, keepdims=True)

    return pl.pallas_call(
        body, name=name, grid=(C // tc,),
        in_specs=[pl.BlockSpec((T, tc), lambda j: (0, j)), pl.BlockSpec((T, tc), lambda j: (0, j)),
                  pl.BlockSpec((32, tc), lambda j: (0, j))],
        out_specs=[pl.BlockSpec((T, tc), lambda j: (0, j)), pl.BlockSpec((32, tc), lambda j: (0, j))],
        out_shape=[jax.ShapeDtypeStruct((T, C), F32), jax.ShapeDtypeStruct((32, C), F32)],
        scratch_shapes=[pltpu.VMEM((8, T + CONV_PAD, tc), F32), pltpu.VMEM((8, T + CONV_PAD, tc), F32)],
        compiler_params=_cparams(("parallel",)),
    )(x, dy, w)


def _heads_per_step(H):
    return 4 if H % 4 == 0 else 1


def q_proj(cq, wq, cos, sin, *, scale, name):
    T, R = cq.shape
    H = wq.shape[1] // HEAD_PAD
    hb = _heads_per_step(H)

    def body(a_ref, b_ref, c_ref, s_ref, o_ref):
        p = jnp.dot(a_ref[...], b_ref[...], preferred_element_type=F32)
        for h in range(hb):
            o = h * HEAD_PAD
            o_ref[:, o:o + QK_NOPE] = (p[:, o:o + QK_NOPE] * scale).astype(BF)
            x = p[:, o + QK_NOPE:o + HEAD_PAD]
            o_ref[:, o + QK_NOPE:o + HEAD_PAD] = ((x * c_ref[...] + _rot_half(x) * s_ref[...]) * scale).astype(BF)

    return pl.pallas_call(
        body, name=name, grid=(H // hb,),
        in_specs=[pl.BlockSpec((T, R), lambda h: (0, 0)), pl.BlockSpec((R, hb * HEAD_PAD), lambda h: (0, h)),
                  pl.BlockSpec((T, 128), lambda h: (0, 0)), pl.BlockSpec((T, 128), lambda h: (0, 0))],
        out_specs=pl.BlockSpec((T, hb * HEAD_PAD), lambda h: (0, h)),
        out_shape=jax.ShapeDtypeStruct((T, H * HEAD_PAD), BF),
        compiler_params=_cparams(("parallel",)),
    )(cq, wq, cos, sin)


def k_proj(ckv, wk, kr, *, name):
    T, R = ckv.shape
    H = wk.shape[1] // QK_NOPE
    hb = _heads_per_step(H)

    def body(a_ref, b_ref, kr_ref, o_ref):
        p = jnp.dot(a_ref[...], b_ref[...], preferred_element_type=F32)
        krb = kr_ref[...].astype(BF)
        for h in range(hb):
            o_ref[:, h * HEAD_PAD:h * HEAD_PAD + QK_NOPE] = p[:, h * QK_NOPE:(h + 1) * QK_NOPE].astype(BF)
            o_ref[:, h * HEAD_PAD + QK_NOPE:(h + 1) * HEAD_PAD] = krb

    return pl.pallas_call(
        body, name=name, grid=(H // hb,),
        in_specs=[pl.BlockSpec((T, R), lambda h: (0, 0)), pl.BlockSpec((R, hb * QK_NOPE), lambda h: (0, h)),
                  pl.BlockSpec((T, 128), lambda h: (0, 0))],
        out_specs=pl.BlockSpec((T, hb * HEAD_PAD), lambda h: (0, h)),
        out_shape=jax.ShapeDtypeStruct((T, H * HEAD_PAD), BF),
        compiler_params=_cparams(("parallel",)),
    )(ckv, wk, kr)


def _causal_scores(q, k_ref, qi, CH, split):
    nt = (((1,), (1,)), ((), ()))
    pieces = []
    if not split:
        rows = slice(0, (qi + 1) * CH)
        s = lax.dot_general(q, k_ref[rows, :], nt, preferred_element_type=F32)
        keep = lax.broadcasted_iota(jnp.int32, s.shape, 1) <= lax.broadcasted_iota(jnp.int32, s.shape, 0) + qi * CH
        return [(rows, jnp.where(keep, s, -1e30))]
    if qi > 0:
        rows = slice(0, qi * CH)
        pieces.append((rows, lax.dot_general(q, k_ref[rows, :], nt, preferred_element_type=F32)))
    rows = slice(qi * CH, (qi + 1) * CH)
    s = lax.dot_general(q, k_ref[rows, :], nt, preferred_element_type=F32)
    keep = lax.broadcasted_iota(jnp.int32, s.shape, 1) <= lax.broadcasted_iota(jnp.int32, s.shape, 0)
    pieces.append((rows, jnp.where(keep, s, -1e30)))
    return pieces


def attn_fwd(q, k, v, *, nch, name):
    T = q.shape[0]
    H = q.shape[1] // HEAD_PAD
    CH = T // nch
    assert CH * nch == T and CH % 16 == 0

    def body(q_ref, k_ref, v_ref, o_ref, lse_ref):
        for qi in range(nch):
            L = (qi + 1) * CH
            pieces = _causal_scores(q_ref[qi * CH:L, :], k_ref, qi, CH, True)
            m = functools.reduce(jnp.maximum, [jnp.max(s, axis=1, keepdims=True) for _, s in pieces])
            l, o = 0.0, 0.0
            for rows, s in pieces:
                p = jnp.exp(s - m)
                l = l + jnp.sum(p, axis=1, keepdims=True)
                o = o + jnp.dot(p.astype(BF), v_ref[rows, :], preferred_element_type=F32)
            o_ref[qi * CH:L, :] = (o / l).astype(BF)
            lse_ref[qi * CH:L, :] = jnp.broadcast_to(m + jnp.log(l), (CH, V_HEAD))

    return pl.pallas_call(
        body, name=name, grid=(H,),
        in_specs=[pl.BlockSpec((T, HEAD_PAD), lambda h: (0, h)), pl.BlockSpec((T, HEAD_PAD), lambda h: (0, h)),
                  pl.BlockSpec((T, V_HEAD), lambda h: (0, h))],
        out_specs=[pl.BlockSpec((T, V_HEAD), lambda h: (0, h)), pl.BlockSpec((T, V_HEAD), lambda h: (0, h))],
        out_shape=[jax.ShapeDtypeStruct((T, H * V_HEAD), BF), jax.ShapeDtypeStruct((T, H * V_HEAD), F32)],
        compiler_params=_cparams(("parallel",)),
    )(q, k, v)


def attn_bwd(q, k, v, o, do, lse, cos, sin, *, scale, nch, name):
    T = q.shape[0]
    H = q.shape[1] // HEAD_PAD
    CH = T // nch
    tn_dims = (((0,), (0,)), ((), ()))
    nt_dims = (((1,), (1,)), ((), ()))

    def body(q_ref, k_ref, v_ref, o_ref, do_ref, lse_ref, c_ref, s_ref,
             dq_ref, dk_ref, dv_ref, dkr_ref, dk_acc, dv_acc):
        h = pl.program_id(0)
        dk_acc[...] = jnp.zeros_like(dk_acc)
        dv_acc[...] = jnp.zeros_like(dv_acc)
        for qi in range(nch):
            L = (qi + 1) * CH
            rows = slice(qi * CH, L)
            qc, doc = q_ref[rows, :], do_ref[rows, :]
            lse_c = lse_ref[rows, 0:1]
            dsum = jnp.sum(doc.astype(F32) * o_ref[rows, :].astype(F32), axis=1, keepdims=True)
            dq = 0.0
            for krows, s in _causal_scores(qc, k_ref, qi, CH, False):
                p = jnp.exp(s - lse_c)
                dp = lax.dot_general(doc, v_ref[krows, :], nt_dims, preferred_element_type=F32)
                ds = (p * (dp - dsum)).astype(BF)
                dq = dq + jnp.dot(ds, k_ref[krows, :], preferred_element_type=F32)
                dk_acc[krows, :] += lax.dot_general(ds, qc, tn_dims, preferred_element_type=F32)
                dv_acc[krows, :] += lax.dot_general(p.astype(BF), doc, tn_dims, preferred_element_type=F32)
            dq_ref[rows, :QK_NOPE] = (dq[:, :QK_NOPE] * scale).astype(BF)
            dyr = dq[:, QK_NOPE:]
            dq_ref[rows, QK_NOPE:] = ((dyr * c_ref[rows, :] - _rot_half(dyr * s_ref[rows, :])) * scale).astype(BF)
        dk_ref[:, :QK_NOPE] = dk_acc[:, :QK_NOPE].astype(BF)
        dk_ref[:, QK_NOPE:] = jnp.zeros((T, HEAD_PAD - QK_NOPE), BF)
        dv_ref[...] = dv_acc[...].astype(BF)

        @pl.when(h == 0)
        def _():
            dkr_ref[...] = dk_acc[:, QK_NOPE:]

        @pl.when(h > 0)
        def _():
            dkr_ref[...] += dk_acc[:, QK_NOPE:]

    hp = pl.BlockSpec((T, HEAD_PAD), lambda h: (0, h))
    hv = pl.BlockSpec((T, V_HEAD), lambda h: (0, h))
    tab = pl.BlockSpec((T, 128), lambda h: (0, 0))
    return pl.pallas_call(
        body, name=name, grid=(H,),
        in_specs=[hp, hp, hv, hv, hv, hv, tab, tab],
        out_specs=[hp, hp, hv, tab],
        out_shape=[jax.ShapeDtypeStruct((T, H * HEAD_PAD), BF), jax.ShapeDtypeStruct((T, H * HEAD_PAD), BF),
                   jax.ShapeDtypeStruct((T, H * V_HEAD), BF), jax.ShapeDtypeStruct((T, 128), F32)],
        scratch_shapes=[pltpu.VMEM((T, HEAD_PAD), F32), pltpu.VMEM((T, V_HEAD), F32)],
        compiler_params=_cparams(("arbitrary",)),
    )(q, k, v, o, do, lse, cos, sin)


def _place():
    x, y, c = lax.axis_index("x"), lax.axis_index("y"), lax.axis_index("c")
    return x, y, c


def _other_chips(x, y):
    return [(1 - x, y), (x, 1 - y), (1 - x, 1 - y)]


HBM_SPEC = pl.BlockSpec(memory_space=pltpu.HBM)
SEM_SPEC = pl.BlockSpec(memory_space=pltpu.SEMAPHORE)
DATAFLOW = pltpu.SideEffectType.DATAFLOW_SIDE_EFFECTING


def _hbm(a):
    return pltpu.with_memory_space_constraint(a, pltpu.HBM)


def _half(ref, lead, c, axis):
    rows, cols = ref.shape[-2], ref.shape[-1]
    if axis == 0:
        return ref.at[(*lead, pl.ds(c * (rows // 2), rows // 2))]
    return ref.at[(*lead, slice(None), pl.ds(c * (cols // 2), cols // 2))]


def copies_start(arrs, plan, n_sems, after, *, name):
    n = len(arrs)

    def body(*refs):
        outs = refs[n + 1:2 * n + 1]
        send_sems, recv_sems, token = refs[2 * n + 1:]
        for cp in plan(outs, send_sems, recv_sems, False):
            cp.start()
        token[...] = jnp.zeros_like(token)

    res = pl.pallas_call(
        body, name=name,
        out_shape=tuple(pltpu.HBM(a.shape, a.dtype) for a in arrs)
        + (pltpu.SemaphoreType.DMA((n_sems,)), pltpu.SemaphoreType.DMA((n_sems,)), jax.ShapeDtypeStruct((8, 128), F32)),
        in_specs=(HBM_SPEC,) * n + (pl.BlockSpec(memory_space=pl.ANY),),
        out_specs=(HBM_SPEC,) * n + (SEM_SPEC, SEM_SPEC, pl.BlockSpec(memory_space=pltpu.VMEM)),
        input_output_aliases={i: i for i in range(n)},
        compiler_params=pltpu.CompilerParams(has_side_effects=DATAFLOW),
    )(*[_hbm(a) for a in arrs], after)
    return list(res[:n]), res[n], res[n + 1], res[n + 2]


def copies_wait(arrs, plan, send_sems, recv_sems, after, *, name):
    n = len(arrs)
    afters = list(after) if isinstance(after, (list, tuple)) else [after]

    def body(*refs):
        for cp in plan(refs[:n], refs[n], refs[n + 1], True):
            cp.wait_send()
            cp.wait_recv()

    res = pl.pallas_call(
        body, name=name,
        out_shape=tuple(pltpu.HBM(a.shape, a.dtype) for a in arrs),
        in_specs=(HBM_SPEC,) * n + (SEM_SPEC, SEM_SPEC) + (pl.BlockSpec(memory_space=pl.ANY),) * len(afters),
        out_specs=(HBM_SPEC,) * n,
        input_output_aliases={i: i for i in range(n)},
        compiler_params=pltpu.CompilerParams(has_side_effects=DATAFLOW),
    )(*arrs, send_sems, recv_sems, *afters)
    return list(res)


def _half_part(ref, lead, c, axis, part):
    rows, cols = ref.shape[-2], ref.shape[-1]
    span = rows // 2 if axis == 0 else rows
    p1 = -(-(span // 2) // 16) * 16
    lo, n = (0, p1) if part == 0 else (p1, span - p1)
    if axis == 0:
        return ref.at[(*lead, pl.ds(c * span + lo, n))]
    return ref.at[(*lead, pl.ds(lo, n), pl.ds(c * (cols // 2), cols // 2))]


def near_plan(axes):
    def plan(bufs, send_sems, recv_sems, receiving):
        x, y, c = _place()
        me = 2 * x + y
        out = []
        for i, b in enumerate(bufs):
            for j, (cx, cy) in enumerate(_other_chips(x, y)[:2]):
                src = _half(b, (me,), c, axes[i])
                dst = _half(b, (2 * cx + cy,), c, axes[i]) if receiving else src
                out.append(pltpu.make_async_remote_copy(
                    src_ref=src, dst_ref=dst, send_sem=send_sems.at[2 * i + j], recv_sem=recv_sems.at[2 * i + j],
                    device_id=(cx, cy, c), device_id_type=MESH))
        return out
    return plan


def relay_plan(axes):
    def plan(bufs, send_sems, recv_sems, receiving):
        x, y, c = _place()
        (xn, yn, dg) = [2 * cx + cy for cx, cy in _other_chips(x, y)]
        to = [(x, 1 - y, c), (1 - x, y, c), (x, y, 1 - c), (x, y, 1 - c)]
        out = []
        for i, b in enumerate(bufs):
            if receiving:
                blks = [_half_part(b, (dg,), c, axes[i], 0), _half_part(b, (dg,), c, axes[i], 1),
                        _half(b, (xn,), 1 - c, axes[i]), _half(b, (yn,), 1 - c, axes[i])]
            else:
                blks = [_half_part(b, (xn,), c, axes[i], 0), _half_part(b, (yn,), c, axes[i], 1),
                        _half(b, (xn,), c, axes[i]), _half(b, (yn,), c, axes[i])]
            for k in range(4):
                out.append(pltpu.make_async_remote_copy(
                    src_ref=blks[k], dst_ref=blks[k], send_sem=send_sems.at[4 * i + k],
                    recv_sem=recv_sems.at[4 * i + k], device_id=to[k], device_id_type=MESH))
        return out
    return plan


def far_forward_plan(axes):
    def plan(bufs, send_sems, recv_sems, receiving):
        x, y, c = _place()
        dg = 2 * (1 - x) + (1 - y)
        out = []
        for i, b in enumerate(bufs):
            blk = _half(b, (dg,), (1 - c) if receiving else c, axes[i])
            out.append(pltpu.make_async_remote_copy(
                src_ref=blk, dst_ref=blk, send_sem=send_sems.at[i], recv_sem=recv_sems.at[i],
                device_id=(x, y, 1 - c), device_id_type=MESH))
        return out
    return plan


def share_plan(axes):
    def plan(bufs, send_sems, recv_sems, receiving):
        x, y, c = _place()
        out = []
        for i, b in enumerate(bufs):
            blk = _half(b, (), (1 - c) if receiving else c, axes[i])
            out.append(pltpu.make_async_remote_copy(
                src_ref=blk, dst_ref=blk, send_sem=send_sems.at[i], recv_sem=recv_sems.at[i],
                device_id=(x, y, 1 - c), device_id_type=MESH))
        return out
    return plan


def scatter_plan(n):
    def plan(arrs, send_sems, recv_sems, receiving):
        x, y, c = _place()
        out = []
        for i in range(n):
            for j, (cx, cy) in enumerate(_other_chips(x, y)):
                out.append(pltpu.make_async_remote_copy(
                    src_ref=arrs[i].at[2 * cx + cy], dst_ref=arrs[n + i].at[j], send_sem=send_sems.at[3 * i + j],
                    recv_sem=recv_sems.at[3 * i + j], device_id=(cx, cy, c), device_id_type=MESH))
        return out
    return plan


def exchange_plan(axes):
    n = len(axes)

    def plan(arrs, send_sems, recv_sems, receiving):
        x, y, c = _place()
        out = []
        for i in range(n):
            for k in range(N_CHIPS):
                out.append(pltpu.make_async_remote_copy(
                    src_ref=_half(arrs[i], (k,), 1 - c, axes[i]), dst_ref=arrs[n + i].at[k],
                    send_sem=send_sems.at[N_CHIPS * i + k], recv_sem=recv_sems.at[N_CHIPS * i + k],
                    device_id=(x, y, 1 - c), device_id_type=MESH))
        return out
    return plan


def devices_plan(arrs, send_sems, recv_sems, receiving):
    x, y, c = _place()
    out = []
    for k in range(1, N_DEV):
        px, py, pc = x ^ ((k >> 2) & 1), y ^ ((k >> 1) & 1), c ^ (k & 1)
        slot = (4 * px + 2 * py + pc) if receiving else (4 * x + 2 * y + c)
        out.append(pltpu.make_async_remote_copy(
            src_ref=arrs[0], dst_ref=arrs[1].at[slot], send_sem=send_sems.at[k - 1], recv_sem=recv_sems.at[k - 1],
            device_id=(px, py, pc), device_id_type=MESH))
    return out


def _row_tile(r, cols, itemsize, target_bytes=2 * 1024 * 1024):
    best = None
    for t in range(16, r + 1, 16):
        if r % t == 0 and t * cols * itemsize <= target_bytes:
            best = t
    return best if best is not None else r


def pair_sum(g, recv, half, axis, *, name):
    _, hr, hc = recv.shape
    tr = _row_tile(hr, hc, 4)
    nb = hr // tr
    if axis == 0:
        mine = pl.BlockSpec((None, tr, hc), lambda k, r, hf: (k, hf[0] * nb + r, 0))
    else:
        mine = pl.BlockSpec((None, tr, hc), lambda k, r, hf: (k, r, hf[0]))

    def body(half_ref, a_ref, b_ref, o_ref):
        o_ref[...] = (a_ref[...].astype(F32) + b_ref[...].astype(F32)).astype(BF)

    return pl.pallas_call(
        body, name=name,
        grid_spec=pltpu.PrefetchScalarGridSpec(
            num_scalar_prefetch=1, grid=(N_CHIPS, nb),
            in_specs=[mine, pl.BlockSpec((None, tr, hc), lambda k, r, hf: (k, r, 0))],
            out_specs=pl.BlockSpec((None, tr, hc), lambda k, r, hf: (k, r, 0))),
        out_shape=jax.ShapeDtypeStruct((N_CHIPS, hr, hc), BF),
        compiler_params=_cparams(("parallel", "parallel")),
    )(half, g, recv)


def chip_sum(s, recv, place, axis, *, name):
    _, hr, hc = s.shape
    tr = _row_tile(hr, hc, 4)
    nb = hr // tr
    if axis == 0:
        o_spec = pl.BlockSpec((tr, hc), lambda r, pc: (pc[1] * nb + r, 0))
        o_shape = (2 * hr, hc)
    else:
        o_spec = pl.BlockSpec((tr, hc), lambda r, pc: (r, pc[1]))
        o_shape = (hr, 2 * hc)

    def body(place_ref, a_ref, b_ref, o_ref):
        acc = a_ref[...].astype(F32)
        for j in range(3):
            acc = acc + b_ref[j].astype(F32)
        o_ref[...] = acc

    return pl.pallas_call(
        body, name=name,
        grid_spec=pltpu.PrefetchScalarGridSpec(
            num_scalar_prefetch=1, grid=(nb,),
            in_specs=[pl.BlockSpec((None, tr, hc), lambda r, pc: (pc[0], r, 0)),
                      pl.BlockSpec((3, tr, hc), lambda r, pc: (0, r, 0))],
            out_specs=o_spec),
        out_shape=jax.ShapeDtypeStruct(o_shape, F32),
        compiler_params=_cparams(("parallel",)),
    )(place, s, recv)


def sum_devices(v, *, name):
    _, R, C = v.shape

    def body(v_ref, o_ref):
        acc = v_ref[0]
        for k in range(1, N_DEV):
            acc = acc + v_ref[k]
        o_ref[...] = acc

    return pl.pallas_call(body, name=name, out_shape=jax.ShapeDtypeStruct((R, C), F32))(v)


def adamw(w, g, m, v, *, name):
    R, C = w.shape
    tr = _row_tile(R, C, 4, target_bytes=1024 * 1024)
    c1 = 1.0 / (1.0 - ADAM_B1 ** ADAM_STEP)
    c2 = 1.0 / (1.0 - ADAM_B2 ** ADAM_STEP)

    def body(w_ref, g_ref, m_ref, v_ref, go_ref, d_ref, nm_ref, nv_ref):
        gg = g_ref[...]
        go_ref[...] = gg
        nm = ADAM_B1 * m_ref[...] + (1.0 - ADAM_B1) * gg
        nv = ADAM_B2 * v_ref[...] + (1.0 - ADAM_B2) * (gg * gg)
        nm_ref[...] = nm
        nv_ref[...] = nv
        d_ref[...] = -ADAM_LR * ((nm * c1) / (jnp.sqrt(nv * c2) + ADAM_EPS) + ADAM_WD * w_ref[...])

    spec = pl.BlockSpec((tr, C), lambda i: (i, 0))
    return pl.pallas_call(
        body, name=name, grid=(R // tr,), in_specs=[spec] * 4, out_specs=[spec] * 4,
        out_shape=[jax.ShapeDtypeStruct((R, C), F32)] * 4, compiler_params=_cparams(("parallel",)),
    )(w, g, m, v)


def _rope_tables(T):
    pos = np.arange(T, dtype=np.float32)
    inv_freq = (ROPE_THETA ** (-np.arange(0, QK_ROPE, 2, dtype=np.float32) / QK_ROPE)).astype(np.float32)
    ang = pos[:, None] * inv_freq[None, :]
    z = np.zeros((T, 64), np.float32)
    cos = np.concatenate([np.cos(ang), np.cos(ang), z], axis=1).astype(np.float32)
    sin = np.concatenate([np.sin(ang), np.sin(ang), z], axis=1).astype(np.float32)
    return jnp.asarray(cos), jnp.asarray(sin)


def _padded_rows(length):
    T = -(-length // 16) * 16
    while True:
        for n in (4, 5, 6, 7, 8):
            if T % (16 * n) == 0:
                return T, n
        T += 16


def _pick(n, cands):
    for c in cands:
        if n % c == 0:
            return c
    return n


def kernel(x, meta_tokens, g_mix, w_in, b_glu, b_gate, w_dw, b_dw, g_conv_ln, b_conv_ln, w_conv_out, b_conv_out, g_q_lora, w_uq, g_kv_lora, w_uk, w_uv, w_attn_out, w_out, g_ffn, w_ffn_gate, w_ffn_up, w_ffn_down, g_final, loss_target, m_meta_tokens, m_g_mix, m_w_in, m_b_glu, m_b_gate, m_w_dw, m_b_dw, m_g_conv_ln, m_b_conv_ln, m_w_conv_out, m_b_conv_out, m_g_q_lora, m_w_uq, m_g_kv_lora, m_w_uk, m_w_uv, m_w_attn_out, m_w_out, m_g_ffn, m_w_ffn_gate, m_w_ffn_up, m_w_ffn_down, m_g_final, v_meta_tokens, v_g_mix, v_w_in, v_b_glu, v_b_gate, v_w_dw, v_b_dw, v_g_conv_ln, v_b_conv_ln, v_w_conv_out, v_b_conv_out, v_g_q_lora, v_w_uq, v_g_kv_lora, v_w_uk, v_w_uv, v_w_attn_out, v_w_out, v_g_ffn, v_w_ffn_gate, v_w_ffn_up, v_w_ffn_down, v_g_final):
    weights = dict(meta_tokens=meta_tokens, g_mix=g_mix, w_in=w_in, b_glu=b_glu, b_gate=b_gate, w_dw=w_dw, b_dw=b_dw,
                   g_conv_ln=g_conv_ln, b_conv_ln=b_conv_ln, w_conv_out=w_conv_out, b_conv_out=b_conv_out,
                   g_q_lora=g_q_lora, w_uq=w_uq, g_kv_lora=g_kv_lora, w_uk=w_uk, w_uv=w_uv, w_attn_out=w_attn_out,
                   w_out=w_out, g_ffn=g_ffn, w_ffn_gate=w_ffn_gate, w_ffn_up=w_ffn_up, w_ffn_down=w_ffn_down,
                   g_final=g_final)
    m_in = dict(meta_tokens=m_meta_tokens, g_mix=m_g_mix, w_in=m_w_in, b_glu=m_b_glu, b_gate=m_b_gate, w_dw=m_w_dw,
                b_dw=m_b_dw, g_conv_ln=m_g_conv_ln, b_conv_ln=m_b_conv_ln, w_conv_out=m_w_conv_out,
                b_conv_out=m_b_conv_out, g_q_lora=m_g_q_lora, w_uq=m_w_uq, g_kv_lora=m_g_kv_lora, w_uk=m_w_uk,
                w_uv=m_w_uv, w_attn_out=m_w_attn_out, w_out=m_w_out, g_ffn=m_g_ffn, w_ffn_gate=m_w_ffn_gate,
                w_ffn_up=m_w_ffn_up, w_ffn_down=m_w_ffn_down, g_final=m_g_final)
    v_in = dict(meta_tokens=v_meta_tokens, g_mix=v_g_mix, w_in=v_w_in, b_glu=v_b_glu, b_gate=v_b_gate, w_dw=v_w_dw,
                b_dw=v_b_dw, g_conv_ln=v_g_conv_ln, b_conv_ln=v_b_conv_ln, w_conv_out=v_w_conv_out,
                b_conv_out=v_b_conv_out, g_q_lora=v_g_q_lora, w_uq=v_w_uq, g_kv_lora=v_g_kv_lora, w_uk=v_w_uk,
                w_uv=v_w_uv, w_attn_out=v_w_attn_out, w_out=v_w_out, g_ffn=v_g_ffn, w_ffn_gate=v_w_ffn_gate,
                w_ffn_up=v_w_ffn_up, w_ffn_down=v_w_ffn_down, g_final=v_g_final)
    names = list(weights)

    seq, D = x.shape[1], x.shape[2]
    C = w_conv_out.shape[2]
    R = g_q_lora.shape[1]
    H = w_uv.shape[2] * N_CHIPS // V_HEAD
    FB = w_ffn_gate.shape[2]
    FF = FB * N_CHIPS
    length = N_META + seq
    T, nch = _padded_rows(length)
    unit = T // nch
    scale = (QK_NOPE + QK_ROPE) ** -0.5
    assert C == D and w_in.shape[2] * N_CHIPS == 2 * C + 2 * R + QK_ROPE + 2 * D

    xi, yi, ci = _place()
    chip = (2 * xi + yi).astype(jnp.int32)
    half_arr = ci.astype(jnp.int32).reshape(1)

    tr = unit
    tmh = T // 2 if (T // 2) % 16 == 0 else unit
    tmq = unit
    tmw = _pick(D, (1024, 512))
    rc = unit // 2 if unit % 16 == 0 else unit
    tcv = 128

    def tn_of(n, pref=512):
        return _pick(n, (pref, 384, 256, 128))

    grp_a = ["w_in"]
    grp_b = ["w_conv_out", "w_uq", "w_uk", "w_uv", "w_attn_out", "w_out"]
    grp_c = ["w_ffn_gate", "w_ffn_up", "w_ffn_down"]
    big = grp_a + grp_b + grp_c
    axes = {nm: 0 for nm in big}
    axes["w_in"] = 1
    shard = {nm: weights[nm][0] for nm in big}
    shard["w_in"] = jnp.swapaxes(w_in[0], 0, 1)

    def own_blocks(srcs, dtype=BF):
        bufs = []
        for w in srcs:
            wb = w.astype(dtype)
            bufs.append(lax.dynamic_update_slice(lax.empty((N_CHIPS,) + wb.shape, dtype), wb[None], (chip, 0, 0)))
        return bufs

    tiny = jnp.concatenate([meta_tokens, jnp.pad(w_dw[0], ((0, 1), (0, 0)))], axis=0)
    ax_a = [axes[nm] for nm in grp_a] + [0]
    bufs_a, ss_a, rs_a, tok_ga = copies_start(own_blocks([shard[nm] for nm in grp_a]) + own_blocks([tiny], F32),
                                              near_plan(ax_a), 2 * len(ax_a), half_arr, name="gather_a_near_start")
    tok_ga, held = lax.optimization_barrier((tok_ga, (x, loss_target, [shard[nm] for nm in grp_b + grp_c])))
    x_l, tgt_l, shard_bc = held
    h0 = jnp.pad(x_l[0], ((N_META, T - length), (0, 0)))
    tgt = jnp.pad(tgt_l[0], ((N_META, T - length), (0, 0)))
    bufs_b, bufs_c = own_blocks(shard_bc[:len(grp_b)]), own_blocks(shard_bc[len(grp_b):])
    cos, sin = _rope_tables(T)

    bufs_a = copies_wait(bufs_a, near_plan(ax_a), ss_a, rs_a, [h0, tgt] + bufs_b + bufs_c, name="gather_a_near_wait")
    bufs_a, ss_a, rs_a, _ = copies_start(bufs_a, relay_plan(ax_a), 4 * len(ax_a), half_arr,
                                         name="gather_a_relay_start")
    bufs_a = copies_wait(bufs_a, relay_plan(ax_a), ss_a, rs_a, half_arr, name="gather_a_relay_wait")
    bufs_a, ss_a, rs_a, _ = copies_start(bufs_a, far_forward_plan(ax_a), len(ax_a), half_arr,
                                         name="gather_a_far_start")
    bufs_a = copies_wait(bufs_a, far_forward_plan(ax_a), ss_a, rs_a, half_arr, name="gather_a_far_wait")
    G = dict(zip(grp_a, bufs_a))
    meta_full = jnp.concatenate([bufs_a[-1][k, :N_META] for k in range(N_CHIPS)], axis=1)
    w_dw_full = jnp.concatenate([bufs_a[-1][k, N_META:] for k in range(N_CHIPS)], axis=1)
    h0 = lax.dynamic_update_slice(h0, meta_full, (0, 0))
    ax_b = [axes[nm] for nm in grp_b]
    bufs_b, ss_b, rs_b, tok_gb = copies_start(bufs_b, near_plan(ax_b), 2 * len(grp_b), G["w_in"],
                                              name="gather_b_near_start")

    WT_in = G["w_in"].reshape(N_CHIPS * G["w_in"].shape[1], D)
    o1, o2 = 2 * C, 2 * C + 2 * R + QK_ROPE
    WT_small = jnp.pad(WT_in[o1:o2], ((0, 128 - QK_ROPE), (0, 0)))
    WT_gate = WT_in[o2:]

    u = rms_fwd(h0, g_mix, tr=tr, name="rms_mix", deps=[tok_gb])
    z_glu = mm(u, WT_in, mode="nt", tm=T, tn=tn_of(2 * C, 1024), tk=D, out_dtype=BF, name="mm_z_glu", n_out=o1)
    z_small = mm(u, WT_small, mode="nt", tm=T, tn=2 * R + 128, tk=D, out_dtype=F32, name="mm_z_small")
    z_gate = mm(u, WT_gate, mode="nt", tm=T, tn=tn_of(2 * D, 1024), tk=D, out_dtype=BF, name="mm_z_gate")

    bufs_b = copies_wait(bufs_b, near_plan(ax_b), ss_b, rs_b, z_glu, name="gather_b_near_wait")
    bufs_b, ss_b, rs_b, tok_gb = copies_start(bufs_b, relay_plan(ax_b), 4 * len(grp_b), half_arr,
                                              name="gather_b_relay_start")
    ax_c = [axes[nm] for nm in grp_c]
    bufs_c, ss_c, rs_c, tok_gc = copies_start(bufs_c, near_plan(ax_c), 2 * len(grp_c), tok_gb,
                                              name="gather_c_near_start")
    c0 = glu_fwd(z_glu, b_glu, tr=tr, name="glu_fwd", deps=[tok_gc])
    c1 = dwconv_fwd(c0, w_dw_full, b_dw, tc=tcv, rc=rc, name="dwconv_fwd")
    bufs_b = copies_wait(bufs_b, relay_plan(ax_b), ss_b, rs_b, c1, name="gather_b_relay_wait")
    bufs_b, ss_b, rs_b, tok_gb = copies_start(bufs_b, far_forward_plan(ax_b), len(grp_b), half_arr,
                                              name="gather_b_far_start")
    c3 = ln_silu_fwd(c1, g_conv_ln, b_conv_ln, tr=tr, name="ln_silu_fwd")
    cq, ckv, kr = lora_norm_fwd(z_small, g_q_lora, g_kv_lora, cos, sin, tr=tr, name="lora_norm_fwd", deps=[tok_gb])

    def cols_full(a):
        return jnp.concatenate([a[k] for k in range(N_CHIPS)], axis=1)

    G.update(zip(grp_b, copies_wait(bufs_b, far_forward_plan(ax_b), ss_b, rs_b, cq, name="gather_b_far_wait")))
    W_co = G["w_conv_out"].reshape(C, D)
    W_uq = cols_full(G["w_uq"]).reshape(R, H, QK_NOPE + QK_ROPE)
    Wq_p = jnp.pad(W_uq, ((0, 0), (0, 0), (0, HEAD_PAD - QK_NOPE - QK_ROPE))).reshape(R, H * HEAD_PAD)
    W_uk = cols_full(G["w_uk"])
    W_uv = cols_full(G["w_uv"])
    W_ao = G["w_attn_out"].reshape(H * V_HEAD, D)
    W_o = G["w_out"].reshape(D, D)

    y_conv = mm(c3, W_co, mode="nn", tm=T, tn=tn_of(D, 1024), tk=C, out_dtype=BF, name="mm_y_conv")

    q = q_proj(cq, Wq_p, cos, sin, scale=scale, name="q_proj")
    kk = k_proj(ckv, W_uk, kr, name="k_proj")
    vv = mm(ckv, W_uv, mode="nn", tm=T, tn=tn_of(H * V_HEAD), tk=R, out_dtype=BF, name="mm_v")
    o_attn, lse = attn_fwd(q, kk, vv, nch=nch, name="attn_fwd")
    bufs_c = copies_wait(bufs_c, near_plan(ax_c), ss_c, rs_c, o_attn, name="gather_c_near_wait")
    bufs_c, ss_c, rs_c, tok_gc = copies_start(bufs_c, relay_plan(ax_c), 4 * len(grp_c), half_arr,
                                              name="gather_c_relay_start")
    y_attn = mm(o_attn, W_ao, mode="nn", tm=T, tn=tn_of(D, 1024), tk=H * V_HEAD, out_dtype=BF, name="mm_y_attn",
                deps=[tok_gc])

    mix = mix_fwd(z_gate, y_conv, y_attn, b_gate, b_conv_out, tr=tr, name="mix_fwd")
    bufs_c = copies_wait(bufs_c, relay_plan(ax_c), ss_c, rs_c, mix, name="gather_c_relay_wait")
    bufs_c, ss_c, rs_c, tok_gc = copies_start(bufs_c, far_forward_plan(ax_c), len(grp_c), half_arr,
                                              name="gather_c_far_start")
    h1 = mm(mix, W_o, mode="nn", tm=T, tn=tn_of(D), tk=D, out_dtype=F32, name="mm_h1", res=h0, deps=[tok_gc])

    hn = rms_fwd(h1, g_ffn, tr=tr, name="rms_ffn")

    G.update(zip(grp_c, copies_wait(bufs_c, far_forward_plan(ax_c), ss_c, rs_c, hn, name="gather_c_far_wait")))
    W_fg, W_fu = G["w_ffn_gate"], G["w_ffn_up"]
    W_fd = G["w_ffn_down"].reshape(FF, D)

    fa, fb, f = ffn_fwd(hn, W_fg, W_fu, tm=tmq, name="ffn_fwd")
    h2 = mm(f, W_fd, mode="nn", tm=tmh, tn=tn_of(D), tk=FF, out_dtype=F32, name="mm_h2", res=h1)

    place_arr = jnp.stack([chip, ci.astype(jnp.int32)])

    def col_blocks(a):
        r, cfull = a.shape
        return a.reshape(r, N_CHIPS, cfull // N_CHIPS).transpose(1, 0, 2)

    def rs_begin(nms, gls, after, tag):
        ax = [axes[nm] for nm in nms]
        lands = []
        for g, a in zip(gls, ax):
            _, r, cfull = g.shape
            lands.append(lax.empty((N_CHIPS, r // 2, cfull) if a == 0 else (N_CHIPS, r, cfull // 2), BF))
        arrs, ssem, rsem, tok = copies_start(list(gls) + lands, exchange_plan(ax), N_CHIPS * len(nms), after,
                                             name=f"rs_{tag}_exchange_start")
        return (nms, ax, arrs, ssem, rsem), tok

    def rs_middle(state, after, tag):
        nms, ax, arrs, ssem, rsem = state
        n = len(nms)
        arrs = copies_wait(arrs, exchange_plan(ax), ssem, rsem, after, name=f"rs_{tag}_exchange_wait")
        pair = [pair_sum(g, r, half_arr, a, name="rs_pair_sum_" + nm)
                for nm, g, r, a in zip(nms, arrs[:n], arrs[n:], ax)]
        lands = [lax.empty((3,) + p.shape[1:], BF) for p in pair]
        arrs, ssem, rsem, tok = copies_start(pair + lands, scatter_plan(n), 3 * n, half_arr,
                                             name=f"rs_{tag}_scatter_start")
        return (nms, ax, arrs, ssem, rsem), tok

    def rs_end(state, after, tag):
        nms, ax, arrs, ssem, rsem = state
        n = len(nms)
        arrs = copies_wait(arrs, scatter_plan(n), ssem, rsem, after, name=f"rs_{tag}_scatter_wait")
        return [chip_sum(s, r, place_arr, a, name="rs_chip_sum_" + nm)
                for nm, s, r, a in zip(nms, arrs[:n], arrs[n:], ax)]

    def rs_shared(nms, reds, after, tag):
        ax = [axes[nm] for nm in nms]
        reds, ssem, rsem, _ = copies_start(reds, share_plan(ax), len(nms), after, name=f"rs_{tag}_share_start")
        return dict(zip(nms, copies_wait(reds, share_plan(ax), ssem, rsem, half_arr, name=f"rs_{tag}_share_wait")))

    d_h2, d_h2b, loss_p, d_g_final = final_loss(h2, tgt, g_final.reshape(1, D), seq=seq, tr=tr, name="final_loss")
    loss = lax.psum(loss_p[0, 0], ("x", "y", "c"))

    dW_fd = mm(f, d_h2b, mode="tn", tm=FB, tn=tn_of(D, 1024), tk=T, out_dtype=BF, name="mm_dw_ffn_down")
    d_a, d_b = ffn_dact(d_h2b, W_fd, fa, fb, tm=tmq, name="ffn_dact")
    dW_fg = mm(hn, d_a, mode="tn", tm=tmw, tn=FB, tk=T, out_dtype=BF, name="mm_dw_ffn_gate", out_blocks=N_CHIPS)
    dW_fu = mm(hn, d_b, mode="tn", tm=tmw, tn=FB, tk=T, out_dtype=BF, name="mm_dw_ffn_up", out_blocks=N_CHIPS)
    st_c, tok_rc = rs_begin(grp_c, [dW_fg, dW_fu, dW_fd.reshape(N_CHIPS, FB, D)], half_arr, "c")
    d_hn = ffn_dhn(d_a, d_b, W_fg, W_fu, tm=tmq, tn=_pick(D, (512, 256)), name="ffn_dhn", deps=[tok_rc])
    st_c, tok_rc = rs_middle(st_c, d_hn, "c")
    d_h1, d_h1b, d_g_ffn = rms_bwd(d_hn, h1, g_ffn, d_h2, tr=tr, name="rms_ffn_bwd", deps=[tok_rc])

    d_mix = mm(d_h1b, W_o, mode="nt", tm=T, tn=tn_of(D, 1024), tk=D, out_dtype=BF, name="mm_d_mix")
    dW_o = mm(mix, d_h1b, mode="tn", tm=tmw, tn=tn_of(D, 1024), tk=T, out_dtype=BF, name="mm_dw_out")
    d_yc, d_ya, d_zgate, d_b_gate, d_b_co = mix_bwd(z_gate, y_conv, y_attn, d_mix, b_gate, b_conv_out, tr=tr,
                                                    name="mix_bwd")

    dW_co = mm(c3, d_yc, mode="tn", tm=tmw, tn=tn_of(D, 1024), tk=T, out_dtype=BF, name="mm_dw_conv_out")
    d_c3 = mm(d_yc, W_co, mode="nt", tm=T, tn=tn_of(C, 1024), tk=D, out_dtype=BF, name="mm_d_c3")
    d_c1, d_g_ln, d_b_ln, d_b_dw = ln_silu_bwd(c1, d_c3, g_conv_ln, b_conv_ln, tr=tr, name="ln_silu_bwd")
    d_c0, d_w_dw = dwconv_bwd(c0, d_c1, w_dw_full, tc=tcv, rc=rc, name="dwconv_bwd")
    d_zglu, d_b_glu = glu_bwd(z_glu, b_glu, d_c0, tr=tr, name="glu_bwd")

    dW_ao = mm(o_attn, d_ya, mode="tn", tm=_pick(H * V_HEAD, (1024, 512)), tn=tn_of(D, 1024), tk=T, out_dtype=BF, name="mm_dw_attn_out")
    d_o = mm(d_ya, W_ao, mode="nt", tm=T, tn=tn_of(H * V_HEAD, 1024), tk=D, out_dtype=BF, name="mm_d_o")
    d_q2, d_k2, d_v, d_kr = attn_bwd(q, kk, vv, o_attn, d_o, lse, cos, sin, scale=scale, nch=nch, name="attn_bwd")

    dWq_p = mm(cq, d_q2, mode="tn", tm=R, tn=tn_of(H * HEAD_PAD), tk=T, out_dtype=BF, name="mm_dw_uq")
    d_cq = mm(d_q2, Wq_p, mode="nt", tm=T, tn=R, tk=_pick(H * HEAD_PAD, (1024,)), out_dtype=F32, name="mm_d_cq")
    Wk_p = jnp.pad(W_uk.reshape(R, H, QK_NOPE), ((0, 0), (0, 0), (0, HEAD_PAD - QK_NOPE))).reshape(R, H * HEAD_PAD)
    dWk_p = mm(ckv, d_k2, mode="tn", tm=R, tn=tn_of(H * HEAD_PAD), tk=T, out_dtype=BF, name="mm_dw_uk")
    dW_uv = mm(ckv, d_v, mode="tn", tm=R, tn=tn_of(H * V_HEAD), tk=T, out_dtype=BF, name="mm_dw_uv")
    sh_c = rs_end(st_c, [dWq_p, dWk_p, dW_uv, dW_co, dW_ao, dW_o], "c")
    dW_uq = dWq_p.reshape(R, H, HEAD_PAD)[:, :, :QK_NOPE + QK_ROPE].reshape(R, H * (QK_NOPE + QK_ROPE))
    dW_uk = dWk_p.reshape(R, H, HEAD_PAD)[:, :, :QK_NOPE].reshape(R, H * QK_NOPE)
    st_b, tok_rb = rs_begin(grp_b, [dW_co.reshape(N_CHIPS, C // N_CHIPS, D), col_blocks(dW_uq), col_blocks(dW_uk),
                                    col_blocks(dW_uv), dW_ao.reshape(N_CHIPS, H * V_HEAD // N_CHIPS, D),
                                    dW_o.reshape(N_CHIPS, D // N_CHIPS, D)], sh_c[-1], "b")
    d_ckv = mm(d_k2, Wk_p, mode="nt", tm=T, tn=R, tk=_pick(H * HEAD_PAD, (1024,)), out_dtype=F32, name="mm_d_ckv_k",
               deps=[tok_rb])
    d_ckv = mm(d_v, W_uv, mode="nt", tm=T, tn=R, tk=_pick(H * V_HEAD, (1024,)), out_dtype=F32, name="mm_d_ckv_v",
               res=d_ckv)
    d_zsmall, d_g_q, d_g_kv = lora_norm_bwd(z_small, d_cq, d_ckv, d_kr, cos, sin, g_q_lora, g_kv_lora, tr=tr,
                                            name="lora_norm_bwd")
    st_b, tok_rb = rs_middle(st_b, d_zsmall, "b")

    tmi = _pick(2 * C, (1024, 512))
    dWT_in = lax.empty((WT_in.shape[0], D), BF)
    dWT_in = mm_tn_into(d_zglu, u, dWT_in, 0, tm=tmi, tn=tn_of(D, 1024), name="mm_dw_glu", deps=[tok_rb])
    dWT_in = mm_tn_into(d_zsmall, u, dWT_in, o1, tm=2 * R + 128, tn=tn_of(D), name="mm_dw_small")
    dWT_in = mm_tn_into(d_zgate, u, dWT_in, o2, tm=tmi, tn=tn_of(D, 1024), name="mm_dw_gate")
    sh_b = rs_end(st_b, dWT_in, "b")
    st_a, tok_ra = rs_begin(grp_a, [dWT_in.reshape(N_CHIPS, dWT_in.shape[0] // N_CHIPS, D)], sh_b[-1], "a")
    d_u = mm(d_zglu, WT_in, mode="nn", tm=T, tn=tn_of(D), tk=_pick(2 * C, (2048, 1024)), out_dtype=F32, name="mm_d_u_glu",
             deps=[tok_ra])
    st_a, tok_ra = rs_middle(st_a, d_u, "a")
    d_u = mm(d_zsmall, WT_small, mode="nn", tm=T, tn=tn_of(D), tk=2 * R + 128, out_dtype=F32, name="mm_d_u_small",
             res=d_u, deps=[tok_ra])
    d_u = mm(d_zgate, WT_gate, mode="nn", tm=T, tn=tn_of(D), tk=_pick(2 * D, (2048, 1024)), out_dtype=F32,
             name="mm_d_u_gate", res=d_u)
    d_h0, _, d_g_mix = rms_bwd(d_u, h0, g_mix, d_h1, tr=tr, name="rms_mix_bwd")

    grad_x = d_h0[N_META:length][None]

    small = ["g_mix", "b_glu", "b_gate", "b_dw", "g_conv_ln", "b_conv_ln", "b_conv_out", "g_q_lora", "g_kv_lora",
             "g_ffn", "g_final", "w_dw", "meta_tokens"]
    sgrads = dict(g_mix=d_g_mix, b_glu=d_b_glu, b_gate=d_b_gate, b_dw=d_b_dw, g_conv_ln=d_g_ln, b_conv_ln=d_b_ln,
                  b_conv_out=d_b_co, g_q_lora=d_g_q, g_kv_lora=d_g_kv, g_ffn=d_g_ffn, g_final=d_g_final,
                  w_dw=d_w_dw[:CONV_WIDTH], meta_tokens=d_h0[:N_META])
    sizes = [int(np.prod(sgrads[nm].shape)) for nm in small]
    packed = jnp.concatenate([sgrads[nm].reshape(-1) for nm in small]).reshape(-1, 128)
    sm_arrs, sm_ss, sm_rs, tok_sm = copies_start([packed, lax.empty((N_DEV,) + packed.shape, F32)], devices_plan,
                                                 N_DEV - 1, packed, name="gather_small_start")

    grads, delta, new_m, new_v, delta_2d = {}, {}, {}, {}, {}

    def adamw_2d(nm, w2, g2, m2, v2, back):
        g_, d_, m_, v_ = adamw(w2, g2, m2, v2, name="adamw_" + nm)
        delta_2d[nm] = d_
        grads[nm], delta[nm], new_m[nm], new_v[nm] = back(g_), back(d_), back(m_), back(v_)

    reduced = rs_shared(grp_b + grp_c, sh_b + sh_c, tok_sm, "bc")
    for nm in grp_c + grp_b + grp_a:
        if nm == grp_a[0]:
            reduced.update(rs_shared(grp_a, rs_end(st_a, [delta_2d[k] for k in grp_c + grp_b], "a"), half_arr, "a"))
        shp = weights[nm].shape
        if axes[nm] == 1:
            adamw_2d(nm, shard[nm], reduced[nm], jnp.swapaxes(m_in[nm][0], 0, 1), jnp.swapaxes(v_in[nm][0], 0, 1),
                     lambda a: jnp.swapaxes(a, 0, 1)[None])
        else:
            two = (shp[-2], shp[-1])
            adamw_2d(nm, weights[nm].reshape(two), reduced[nm], m_in[nm].reshape(two), v_in[nm].reshape(two),
                     lambda a, shp=shp: a.reshape(shp))

    sm_arrs = copies_wait(sm_arrs, devices_plan, sm_ss, sm_rs, delta_2d[grp_a[0]], name="gather_small_wait")
    me = (4 * xi + 2 * yi + ci).astype(jnp.int32)
    parts = lax.dynamic_update_slice(sm_arrs[1], sm_arrs[0][None], (me, 0, 0))
    summed = sum_devices(parts, name="sum_small_grads").reshape(-1)
    offs = np.concatenate([[0], np.cumsum(sizes)])
    sfull = {nm: summed[int(offs[i]):int(offs[i + 1])].reshape(sgrads[nm].shape) for i, nm in enumerate(small)}
    for nm in small:
        gfull = sfull[nm]
        if nm == "w_dw":
            cb = C // N_CHIPS
            grads[nm] = lax.dynamic_slice(gfull, (0, chip * cb), (CONV_WIDTH, cb))[None]
        elif nm == "meta_tokens":
            cb = D // N_CHIPS
            grads[nm] = lax.dynamic_slice(gfull, (0, chip * cb), (N_META, cb))
        else:
            grads[nm] = gfull.reshape(weights[nm].shape)

    rep = [nm for nm in small if nm not in ("w_dw", "meta_tokens")]

    def pack(d):
        return jnp.concatenate([d[nm].reshape(-1) for nm in rep]).reshape(-1, 128)

    _, pd, pm, pv = adamw(pack(weights), pack(grads), pack(m_in), pack(v_in), name="adamw_small")
    rsz = [int(np.prod(weights[nm].shape)) for nm in rep]
    roff = np.concatenate([[0], np.cumsum(rsz)])
    for i, nm in enumerate(rep):
        sl = slice(int(roff[i]), int(roff[i + 1]))
        delta[nm] = pd.reshape(-1)[sl].reshape(weights[nm].shape)
        new_m[nm] = pm.reshape(-1)[sl].reshape(weights[nm].shape)
        new_v[nm] = pv.reshape(-1)[sl].reshape(weights[nm].shape)
    for nm in ["w_dw", "meta_tokens"]:
        shp = weights[nm].shape
        two = (shp[-2], shp[-1])
        adamw_2d(nm, weights[nm].reshape(two), grads[nm].reshape(two), m_in[nm].reshape(two), v_in[nm].reshape(two),
                 lambda a, shp=shp: a.reshape(shp))

    return (loss, grad_x, *[grads[nm] for nm in names], *[delta[nm] for nm in names],
            *[new_m[nm] for nm in names], *[new_v[nm] for nm in names])
```

```python
import functools

import numpy as np
import jax
import jax.numpy as jnp
from jax import lax
from jax.experimental import pallas as pl
from jax.experimental.pallas import tpu as pltpu

F32 = jnp.float32
BF = jnp.bfloat16
MESH = pl.DeviceIdType.MESH

N_META = 16
CONV_WIDTH = 31
CONV_PAD = 32
QK_NOPE = 128
QK_ROPE = 64
V_HEAD = 128
HEAD_PAD = 256
ROPE_THETA = 10000.0
EPS = 1e-6
ADAM_LR = 0.001
ADAM_B1 = 0.9
ADAM_B2 = 0.999
ADAM_EPS = 1e-08
ADAM_WD = 0.01
ADAM_STEP = 10
VMEM_LIMIT = 56 * 1024 * 1024
N_CHIPS = 4
N_DEV = 8


def _cparams(sem):
    return pltpu.CompilerParams(dimension_semantics=sem, vmem_limit_bytes=VMEM_LIMIT)


def mm(a, b, *, mode, tm, tn, tk, out_dtype, name, res=None, out_blocks=None, deps=(), n_out=None):
    b3 = b.ndim == 3
    if mode == "nn":
        M, K = a.shape
        N = b.shape[0] * b.shape[2] if b3 else b.shape[1]
        a_spec = pl.BlockSpec((tm, tk), lambda i, j, k: (i, k))
        if b3:
            per = b.shape[2] // tn
            b_spec = pl.BlockSpec((None, tk, tn), lambda i, j, k: (j // per, k, j % per))
        else:
            b_spec = pl.BlockSpec((tk, tn), lambda i, j, k: (k, j))
        dims = (((1,), (0,)), ((), ()))
    elif mode == "nt":
        M, K = a.shape
        N = n_out if n_out is not None else (b.shape[1] if b3 else b.shape[0])
        a_spec = pl.BlockSpec((tm, tk), lambda i, j, k: (i, k))
        if b3:
            per = b.shape[2] // tk
            b_spec = pl.BlockSpec((None, tn, tk), lambda i, j, k: (k // per, j, k % per))
        else:
            b_spec = pl.BlockSpec((tn, tk), lambda i, j, k: (j, k))
        dims = (((1,), (1,)), ((), ()))
    else:
        K, M = a.shape
        N = b.shape[1]
        a_spec = pl.BlockSpec((tk, tm), lambda i, j, k: (k, i))
        b_spec = pl.BlockSpec((tk, tn), lambda i, j, k: (k, j))
        dims = (((0,), (0,)), ((), ()))
    gm, gn, gk = M // tm, N // tn, K // tk
    assert gm * tm == M and gn * tn == N and gk * tk == K, (name, a.shape, b.shape, tm, tn, tk)
    if out_blocks is None:
        o_spec = pl.BlockSpec((tm, tn), lambda i, j, k: (i, j))
        o_shape = (M, N)
    else:
        nbw = N // out_blocks
        per_o = nbw // tn
        assert per_o * tn == nbw
        o_spec = pl.BlockSpec((None, tm, tn), lambda i, j, k: (j // per_o, i, j % per_o))
        o_shape = (out_blocks, M, nbw)
    has_res = res is not None

    def body(*refs):
        a_ref, b_ref = refs[0], refs[1]
        r_ref = refs[2] if has_res else None
        o_ref = refs[2 + has_res + len(deps)]
        acc = refs[-1]
        p = lax.dot_general(a_ref[...], b_ref[...], dims, preferred_element_type=F32)

        def finish(v):
            if has_res:
                v = v + r_ref[...]
            o_ref[...] = v.astype(o_ref.dtype)

        if gk == 1:
            finish(p)
        else:
            k = pl.program_id(2)

            @pl.when(k == 0)
            def _():
                acc[...] = p

            @pl.when(k > 0)
            def _():
                acc[...] += p

            @pl.when(k == gk - 1)
            def _():
                finish(acc[...])

    in_specs = [a_spec, b_spec]
    args = [a, b]
    if has_res:
        in_specs.append(pl.BlockSpec((tm, tn), lambda i, j, k: (i, j)))
        args.append(res)
    in_specs += [pl.BlockSpec(memory_space=pl.ANY)] * len(deps)
    args += list(deps)
    return pl.pallas_call(
        body, name=name, grid=(gm, gn, gk), in_specs=in_specs, out_specs=o_spec,
        out_shape=jax.ShapeDtypeStruct(o_shape, out_dtype),
        scratch_shapes=[pltpu.VMEM((tm, tn), F32)] if gk > 1 else [],
        compiler_params=_cparams(("parallel", "parallel", "arbitrary")),
    )(*args)


def mm_tn_into(a, b, buf, row0, *, tm, tn, name, deps=()):
    K, M = a.shape
    N = b.shape[1]
    gm, gn = M // tm, N // tn
    assert gm * tm == M and gn * tn == N and buf.shape[1] == N and row0 + M <= buf.shape[0]
    steps = gm * gn

    def body(a_ref, b_ref, buf_ref, *rest):
        o_ref, tile, sem = rest[len(deps):]
        i, j = pl.program_id(0), pl.program_id(1)
        s = i * gn + j
        slot = s % 2

        def out_copy(sl):
            return pltpu.make_async_copy(tile.at[sl], o_ref.at[pl.ds(row0 + i * tm, tm), pl.ds(j * tn, tn)],
                                         sem.at[sl])

        @pl.when(s >= 2)
        def _():
            out_copy(slot).wait()

        tile[slot] = lax.dot_general(a_ref[...], b_ref[...], (((0,), (0,)), ((), ())),
                                     preferred_element_type=F32).astype(tile.dtype)
        out_copy(slot).start()

        @pl.when(s == steps - 1)
        def _():
            out_copy(slot).wait()
            if steps >= 2:
                out_copy(1 - slot).wait()

    anyspec = pl.BlockSpec(memory_space=pl.ANY)
    return pl.pallas_call(
        body, name=name, grid=(gm, gn),
        in_specs=[pl.BlockSpec((K, tm), lambda i, j: (0, i)), pl.BlockSpec((K, tn), lambda i, j: (0, j)), anyspec]
        + [anyspec] * len(deps),
        out_specs=anyspec, out_shape=jax.ShapeDtypeStruct(buf.shape, buf.dtype),
        input_output_aliases={2: 0},
        scratch_shapes=[pltpu.VMEM((2, tm, tn), buf.dtype), pltpu.SemaphoreType.DMA((2,))],
        compiler_params=_cparams(("arbitrary", "arbitrary")),
    )(a, b, buf, *deps)


def rowwise(body, row_ins, full_ins, row_outs, acc_outs, *, tr, name, deps=()):
    T = row_ins[0].shape[0]
    assert T % tr == 0, (name, T, tr)
    n_ri, n_fi, n_ro = len(row_ins), len(full_ins), len(row_outs)
    n_in = n_ri + n_fi + len(deps)

    def kern(*refs):
        body(pl.program_id(0), refs[:n_ri], refs[n_ri:n_ri + n_fi], refs[n_in:n_in + n_ro], refs[n_in + n_ro:])

    in_specs = [pl.BlockSpec((tr, a.shape[1]), lambda i: (i, 0)) for a in row_ins]
    in_specs += [pl.BlockSpec(a.shape, lambda i: (0, 0)) for a in full_ins]
    in_specs += [pl.BlockSpec(memory_space=pl.ANY)] * len(deps)
    out_specs = [pl.BlockSpec((tr, c), lambda i: (i, 0)) for c, _ in row_outs]
    out_specs += [pl.BlockSpec(s, lambda i: (0, 0)) for s in acc_outs]
    out_shape = [jax.ShapeDtypeStruct((T, c), d) for c, d in row_outs]
    out_shape += [jax.ShapeDtypeStruct(s, F32) for s in acc_outs]
    return pl.pallas_call(
        kern, name=name, grid=(T // tr,), in_specs=in_specs, out_specs=out_specs, out_shape=out_shape,
        compiler_params=_cparams(("arbitrary",)),
    )(*row_ins, *full_ins, *deps)


def _acc(step, ref, val):
    @pl.when(step == 0)
    def _():
        ref[...] = val

    @pl.when(step > 0)
    def _():
        ref[...] += val


def _colsum(x):
    return jnp.sum(x, axis=0, keepdims=True)


def _sigmoid(x):
    return 1.0 / (1.0 + jnp.exp(-x))


def _rot_half(v):
    lane = lax.broadcasted_iota(jnp.int32, v.shape, 1)
    lo = -pltpu.roll(v, 96, 1)
    hi = pltpu.roll(v, 32, 1)
    return jnp.where(lane < 32, lo, jnp.where(lane < 64, hi, 0.0))


def rms_fwd(h, g, *, tr, name, deps=()):
    def body(step, ri, fi, ro, ao):
        x = ri[0][...]
        r = lax.rsqrt(jnp.mean(x * x, axis=-1, keepdims=True) + EPS)
        ro[0][...] = ((x * r) * fi[0][...]).astype(BF)

    return rowwise(body, [h], [g], [(h.shape[1], BF)], [], tr=tr, name=name, deps=deps)[0]


def rms_bwd(dy, h, g, dres, *, tr, name, deps=()):
    D = h.shape[1]

    def body(step, ri, fi, ro, ao):
        d, x, dr = ri[0][...].astype(F32), ri[1][...], ri[2][...]
        r = lax.rsqrt(jnp.mean(x * x, axis=-1, keepdims=True) + EPS)
        n = x * r
        _acc(step, ao[0], _colsum(d * n))
        dn = d * fi[0][...]
        dh = r * (dn - n * jnp.mean(dn * n, axis=-1, keepdims=True)) + dr
        ro[0][...] = dh
        ro[1][...] = dh.astype(BF)

    return rowwise(body, [dy, h, dres], [g], [(D, F32), (D, BF)], [(1, D)], tr=tr, name=name, deps=deps)


def glu_fwd(z, b, *, tr, name, deps=()):
    C = z.shape[1] // 2

    def body(step, ri, fi, ro, ao):
        zz = ri[0][...].astype(F32) + fi[0][...]
        ro[0][...] = zz[:, :C] * _sigmoid(zz[:, C:])

    return rowwise(body, [z], [b], [(C, F32)], [], tr=tr, name=name, deps=deps)[0]


def glu_bwd(z, b, dc, *, tr, name):
    C = z.shape[1] // 2

    def body(step, ri, fi, ro, ao):
        zz = ri[0][...].astype(F32) + fi[0][...]
        d = ri[1][...]
        za, sg = zz[:, :C], _sigmoid(zz[:, C:])
        dza = d * sg
        dzb = d * za * sg * (1.0 - sg)
        ro[0][:, :C] = dza.astype(BF)
        ro[0][:, C:] = dzb.astype(BF)
        _acc(step, ao[0], _colsum(dza))
        _acc(step, ao[1], _colsum(dzb))

    dz, da, db = rowwise(body, [z, dc], [b], [(2 * C, BF)], [(1, C), (1, C)], tr=tr, name=name)
    return dz, jnp.concatenate([da, db], axis=1)


def ln_silu_fwd(c1, g, b, *, tr, name):
    def body(step, ri, fi, ro, ao):
        x = ri[0][...]
        mu = jnp.mean(x, axis=-1, keepdims=True)
        xc = x - mu
        rstd = lax.rsqrt(jnp.mean(xc * xc, axis=-1, keepdims=True) + EPS)
        y = (xc * rstd) * fi[0][...] + fi[1][...]
        ro[0][...] = (y * _sigmoid(y)).astype(BF)

    return rowwise(body, [c1], [g, b], [(c1.shape[1], BF)], [], tr=tr, name=name)[0]


def ln_silu_bwd(c1, dc3, g, b, *, tr, name):
    C = c1.shape[1]

    def body(step, ri, fi, ro, ao):
        x, d3 = ri[0][...], ri[1][...].astype(F32)
        mu = jnp.mean(x, axis=-1, keepdims=True)
        xc = x - mu
        rstd = lax.rsqrt(jnp.mean(xc * xc, axis=-1, keepdims=True) + EPS)
        n = xc * rstd
        y = n * fi[0][...] + fi[1][...]
        sg = _sigmoid(y)
        dy = d3 * (sg * (1.0 + y * (1.0 - sg)))
        _acc(step, ao[0], _colsum(dy * n))
        _acc(step, ao[1], _colsum(dy))
        dn = dy * fi[0][...]
        dx = rstd * (dn - jnp.mean(dn, axis=-1, keepdims=True) - n * jnp.mean(dn * n, axis=-1, keepdims=True))
        ro[0][...] = dx
        _acc(step, ao[2], _colsum(dx))

    return rowwise(body, [c1, dc3], [g, b], [(C, F32)], [(1, C), (1, C), (1, C)], tr=tr, name=name)


def lora_norm_fwd(zs, gq, gkv, cos, sin, *, tr, name, deps=()):
    R = gq.shape[1]

    def body(step, ri, fi, ro, ao):
        z = ri[0][...]
        for o, gi in ((0, 0), (1, 1)):
            x = z[:, o * R:(o + 1) * R]
            r = lax.rsqrt(jnp.mean(x * x, axis=-1, keepdims=True) + EPS)
            ro[o][...] = ((x * r) * fi[gi][...]).astype(BF)
        kr = z[:, 2 * R:2 * R + 128]
        ro[2][...] = kr * ri[1][...] + _rot_half(kr) * ri[2][...]

    return rowwise(body, [zs, cos, sin], [gq, gkv], [(R, BF), (R, BF), (128, F32)], [], tr=tr, name=name, deps=deps)


def lora_norm_bwd(zs, dcq, dckv, dkr, cos, sin, gq, gkv, *, tr, name):
    R = gq.shape[1]

    def body(step, ri, fi, ro, ao):
        z = ri[0][...]
        for o in (0, 1):
            x = z[:, o * R:(o + 1) * R]
            d = ri[1 + o][...]
            r = lax.rsqrt(jnp.mean(x * x, axis=-1, keepdims=True) + EPS)
            n = x * r
            _acc(step, ao[o], _colsum(d * n))
            dn = d * fi[o][...]
            ro[0][:, o * R:(o + 1) * R] = (r * (dn - n * jnp.mean(dn * n, axis=-1, keepdims=True))).astype(BF)
        dk = ri[3][...]
        ro[0][:, 2 * R:2 * R + 128] = (dk * ri[4][...] - _rot_half(dk * ri[5][...])).astype(BF)

    return rowwise(body, [zs, dcq, dckv, dkr, cos, sin], [gq, gkv], [(2 * R + 128, BF)], [(1, R), (1, R)],
                   tr=tr, name=name)


def mix_fwd(zg, yc, ya, bg, bco, *, tr, name):
    D = yc.shape[1]

    def body(step, ri, fi, ro, ao):
        g = _sigmoid(ri[0][...].astype(F32) + fi[0][...])
        ro[0][...] = (g[:, :D] * (ri[1][...].astype(F32) + fi[1][...]) + g[:, D:] * ri[2][...].astype(F32)).astype(BF)

    return rowwise(body, [zg, yc, ya], [bg, bco], [(D, BF)], [], tr=tr, name=name)[0]


def mix_bwd(zg, yc, ya, dmix, bg, bco, *, tr, name):
    D = yc.shape[1]

    def body(step, ri, fi, ro, ao):
        g = _sigmoid(ri[0][...].astype(F32) + fi[0][...])
        gc, ga = g[:, :D], g[:, D:]
        ycv = ri[1][...].astype(F32) + fi[1][...]
        yav = ri[2][...].astype(F32)
        dm = ri[3][...].astype(F32)
        dyc = dm * gc
        ro[0][...] = dyc.astype(BF)
        ro[1][...] = (dm * ga).astype(BF)
        dzc = dm * ycv * gc * (1.0 - gc)
        dza = dm * yav * ga * (1.0 - ga)
        ro[2][:, :D] = dzc.astype(BF)
        ro[2][:, D:] = dza.astype(BF)
        _acc(step, ao[0], _colsum(dzc))
        _acc(step, ao[1], _colsum(dza))
        _acc(step, ao[2], _colsum(dyc))

    dyc, dya, dzg, dbc, dba, dbo = rowwise(body, [zg, yc, ya, dmix], [bg, bco], [(D, BF), (D, BF), (2 * D, BF)],
                                           [(1, D), (1, D), (1, D)], tr=tr, name=name)
    return dyc, dya, dzg, jnp.concatenate([dbc, dba], axis=1), dbo


def ffn_fwd(hn, wg, wu, *, tm, name):
    T, D = hn.shape
    nb, _, FB = wg.shape
    assert T % tm == 0

    def body(h_ref, g_ref, u_ref, a_ref, b_ref, f_ref):
        h = h_ref[...]
        a = jnp.dot(h, g_ref[...], preferred_element_type=F32)
        b = jnp.dot(h, u_ref[...], preferred_element_type=F32)
        a_ref[...] = a
        b_ref[...] = b
        f_ref[...] = (a * _sigmoid(a) * b).astype(BF)

    wspec = pl.BlockSpec((None, D, FB), lambda j, i: (j, 0, 0))
    ospec = pl.BlockSpec((tm, FB), lambda j, i: (i, j))
    return pl.pallas_call(
        body, name=name, grid=(nb, T // tm),
        in_specs=[pl.BlockSpec((tm, D), lambda j, i: (i, 0)), wspec, wspec], out_specs=[ospec, ospec, ospec],
        out_shape=[jax.ShapeDtypeStruct((T, nb * FB), F32), jax.ShapeDtypeStruct((T, nb * FB), F32),
                   jax.ShapeDtypeStruct((T, nb * FB), BF)],
        compiler_params=_cparams(("parallel", "parallel")),
    )(hn, wg, wu)


def ffn_dact(dh, wd, a, b, *, tm, name):
    T, D = dh.shape
    FF = wd.shape[0]
    FB = FF // N_CHIPS

    def body(dh_ref, w_ref, a_ref, b_ref, da_ref, db_ref):
        d = lax.dot_general(dh_ref[...], w_ref[...], (((1,), (1,)), ((), ())), preferred_element_type=F32)
        x, u = a_ref[...], b_ref[...]
        sg = _sigmoid(x)
        da_ref[...] = (d * u * (sg * (1.0 + x * (1.0 - sg)))).astype(BF)
        db_ref[...] = (d * (x * sg)).astype(BF)

    blk = pl.BlockSpec((tm, FB), lambda j, i: (i, j))
    return pl.pallas_call(
        body, name=name, grid=(N_CHIPS, T // tm),
        in_specs=[pl.BlockSpec((tm, D), lambda j, i: (i, 0)), pl.BlockSpec((FB, D), lambda j, i: (j, 0)), blk, blk],
        out_specs=[blk, blk],
        out_shape=[jax.ShapeDtypeStruct((T, FF), BF), jax.ShapeDtypeStruct((T, FF), BF)],
        compiler_params=_cparams(("parallel", "parallel")),
    )(dh, wd, a, b)


def ffn_dhn(da, db, wg, wu, *, tm, tn, name, deps=()):
    T, FF = da.shape
    nb, D, FB = wg.shape
    dims = (((1,), (1,)), ((), ()))

    def body(da_ref, db_ref, g_ref, u_ref, *rest):
        o_ref = rest[len(deps)]
        acc = None
        for k in range(nb):
            cols = slice(k * FB, (k + 1) * FB)
            p = lax.dot_general(da_ref[:, cols], g_ref[k], dims, preferred_element_type=F32)
            p = p + lax.dot_general(db_ref[:, cols], u_ref[k], dims, preferred_element_type=F32)
            acc = p if acc is None else acc + p
        o_ref[...] = acc.astype(BF)

    aspec = pl.BlockSpec((tm, FF), lambda i, j: (i, 0))
    wspec = pl.BlockSpec((nb, tn, FB), lambda i, j: (0, j, 0))
    return pl.pallas_call(
        body, name=name, grid=(T // tm, D // tn),
        in_specs=[aspec, aspec, wspec, wspec] + [pl.BlockSpec(memory_space=pl.ANY)] * len(deps),
        out_specs=pl.BlockSpec((tm, tn), lambda i, j: (i, j)),
        out_shape=jax.ShapeDtypeStruct((T, D), BF),
        compiler_params=_cparams(("parallel", "parallel")),
    )(da, db, wg, wu, *deps)


def final_loss(h2, tgt, g, *, seq, tr, name):
    D = h2.shape[1]

    def body(step, ri, fi, ro, ao):
        x = ri[0][...]
        r = lax.rsqrt(jnp.mean(x * x, axis=-1, keepdims=True) + EPS)
        n = x * r
        gg = fi[0][...]
        row = lax.broadcasted_iota(jnp.int32, (tr, 1), 0) + step * tr
        live = jnp.logical_and(row >= N_META, row < N_META + seq)
        e = jnp.where(live, n * gg - ri[1][...], 0.0)
        _acc(step, ao[0], jnp.broadcast_to(0.5 * jnp.sum(jnp.mean(e * e, axis=-1, keepdims=True)), (1, 128)))
        dy = e * (1.0 / D)
        _acc(step, ao[1], _colsum(dy * n))
        dn = dy * gg
        dh = r * (dn - n * jnp.mean(dn * n, axis=-1, keepdims=True))
        ro[0][...] = dh
        ro[1][...] = dh.astype(BF)

    return rowwise(body, [h2, tgt], [g], [(D, F32), (D, BF)], [(1, 128), (1, D)], tr=tr, name=name)


def _shifted_copies(s_ref, T, rc):
    for r in range(1, 8):
        for t0 in range(0, T, rc):
            s_ref[r, t0:t0 + rc, :] = s_ref[0, t0 + r:t0 + r + rc, :]
        s_ref[r, T:T + CONV_PAD - 8, :] = s_ref[0, T + r:T + r + CONV_PAD - 8, :]


def _shifted(s_ref, start, rows):
    return s_ref[start % 8, start - start % 8:start - start % 8 + rows, :]


def dwconv_fwd(x, w, b, *, tc, rc, name, deps=()):
    T, C = x.shape
    nchunk = T // rc
    assert nchunk * rc == T and C % tc == 0

    def body(x_ref, w_ref, b_ref, *rest):
        y_ref, xs_ref = rest[len(deps):]
        xs_ref[0, 0:CONV_PAD, :] = jnp.zeros((CONV_PAD, tc), F32)
        xs_ref[0, CONV_PAD:CONV_PAD + T, :] = x_ref[...]
        _shifted_copies(xs_ref, T, rc)
        off = CONV_PAD - (CONV_WIDTH - 1)
        for ci in range(nchunk):
            t0 = ci * rc
            acc = jnp.broadcast_to(b_ref[...], (rc, tc))
            for j in range(CONV_WIDTH):
                acc = acc + w_ref[j:j + 1, :] * _shifted(xs_ref, t0 + off + j, rc)
            y_ref[t0:t0 + rc, :] = acc

    return pl.pallas_call(
        body, name=name, grid=(C // tc,),
        in_specs=[pl.BlockSpec((T, tc), lambda j: (0, j)), pl.BlockSpec((32, tc), lambda j: (0, j)),
                  pl.BlockSpec((1, tc), lambda j: (0, j))] + [pl.BlockSpec(memory_space=pl.ANY)] * len(deps),
        out_specs=pl.BlockSpec((T, tc), lambda j: (0, j)),
        out_shape=jax.ShapeDtypeStruct((T, C), F32),
        scratch_shapes=[pltpu.VMEM((8, T + CONV_PAD, tc), F32)],
        compiler_params=_cparams(("parallel",)),
    )(x, w, b, *deps)


def dwconv_bwd(x, dy, w, *, tc, rc, name):
    T, C = x.shape
    nchunk = T // rc
    assert nchunk * rc == T and C % tc == 0 and rc % 8 == 0

    def body(x_ref, dy_ref, w_ref, dx_ref, dw_ref, xs_ref, ds_ref):
        off = CONV_PAD - (CONV_WIDTH - 1)
        xs_ref[0, 0:CONV_PAD, :] = jnp.zeros((CONV_PAD, tc), F32)
        xs_ref[0, CONV_PAD:CONV_PAD + T, :] = x_ref[...]
        _shifted_copies(xs_ref, T, rc)
        ds_ref[0, 0:T, :] = dy_ref[...]
        ds_ref[0, T:T + CONV_PAD, :] = jnp.zeros((CONV_PAD, tc), F32)
        _shifted_copies(ds_ref, T, rc)
        for ci in range(nchunk):
            t0 = ci * rc
            acc = jnp.zeros((rc, tc), F32)
            for j in range(CONV_WIDTH):
                acc = acc + w_ref[j:j + 1, :] * _shifted(ds_ref, t0 + (CONV_WIDTH - 1) - j, rc)
            dx_ref[t0:t0 + rc, :] = acc
        dw_ref[...] = jnp.zeros((32, tc), F32)
        for j in range(CONV_WIDTH):
            acc = jnp.zeros((8, tc), F32)
            for ci in range(nchunk):
                t0 = ci * rc
                pr = dy_ref[t0:t0 + rc, :] * _shifted(xs_ref, t0 + off + j, rc)
                acc = acc + jnp.sum(pr.reshape(rc // 8, 8, tc), axis=0)
            dw_ref[j:j + 1, :] = jnp.sum(acc, axis=0, keepdims=True)

    return pl.pallas_call(
        body, name=name, grid=(C // tc,),
        in_specs=[pl.BlockSpec((T, tc), lambda j: (0, j)), pl.BlockSpec((T, tc), lambda j: (0, j)),
                  pl.BlockSpec((32, tc), lambda j: (0, j))],
        out_specs=[pl.BlockSpec((T, tc), lambda j: (0, j)), pl.BlockSpec((32, tc), lambda j: (0, j))],
        out_shape=[jax.ShapeDtypeStruct((T, C), F32), jax.ShapeDtypeStruct((32, C), F32)],
        scratch_shapes=[pltpu.VMEM((8, T + CONV_PAD, tc), F32), pltpu.VMEM((8, T + CONV_PAD, tc), F32)],
        compiler_params=_cparams(("parallel",)),
    )(x, dy, w)


def _heads_per_step(H):
    return 4 if H % 4 == 0 else 1


def q_proj(cq, wq, cos, sin, *, scale, name):
    T, R = cq.shape
    H = wq.shape[1] // HEAD_PAD
    hb = _heads_per_step(H)

    def body(a_ref, b_ref, c_ref, s_ref, o_ref):
        p = jnp.dot(a_ref[...], b_ref[...], preferred_element_type=F32)
        for h in range(hb):
            o = h * HEAD_PAD
            o_ref[:, o:o + QK_NOPE] = (p[:, o:o + QK_NOPE] * scale).astype(BF)
            x = p[:, o + QK_NOPE:o + HEAD_PAD]
            o_ref[:, o + QK_NOPE:o + HEAD_PAD] = ((x * c_ref[...] + _rot_half(x) * s_ref[...]) * scale).astype(BF)

    return pl.pallas_call(
        body, name=name, grid=(H // hb,),
        in_specs=[pl.BlockSpec((T, R), lambda h: (0, 0)), pl.BlockSpec((R, hb * HEAD_PAD), lambda h: (0, h)),
                  pl.BlockSpec((T, 128), lambda h: (0, 0)), pl.BlockSpec((T, 128), lambda h: (0, 0))],
        out_specs=pl.BlockSpec((T, hb * HEAD_PAD), lambda h: (0, h)),
        out_shape=jax.ShapeDtypeStruct((T, H * HEAD_PAD), BF),
        compiler_params=_cparams(("parallel",)),
    )(cq, wq, cos, sin)


def k_proj(ckv, wk, kr, *, name):
    T, R = ckv.shape
    H = wk.shape[1] // QK_NOPE
    hb = _heads_per_step(H)

    def body(a_ref, b_ref, kr_ref, o_ref):
        p = jnp.dot(a_ref[...], b_ref[...], preferred_element_type=F32)
        krb = kr_ref[...].astype(BF)
        for h in range(hb):
            o_ref[:, h * HEAD_PAD:h * HEAD_PAD + QK_NOPE] = p[:, h * QK_NOPE:(h + 1) * QK_NOPE].astype(BF)
            o_ref[:, h * HEAD_PAD + QK_NOPE:(h + 1) * HEAD_PAD] = krb

    return pl.pallas_call(
        body, name=name, grid=(H // hb,),
        in_specs=[pl.BlockSpec((T, R), lambda h: (0, 0)), pl.BlockSpec((R, hb * QK_NOPE), lambda h: (0, h)),
                  pl.BlockSpec((T, 128), lambda h: (0, 0))],
        out_specs=pl.BlockSpec((T, hb * HEAD_PAD), lambda h: (0, h)),
        out_shape=jax.ShapeDtypeStruct((T, H * HEAD_PAD), BF),
        compiler_params=_cparams(("parallel",)),
    )(ckv, wk, kr)


def _causal_scores(q, k_ref, qi, CH, split):
    nt = (((1,), (1,)), ((), ()))
    pieces = []
    if not split:
        rows = slice(0, (qi + 1) * CH)
        s = lax.dot_general(q, k_ref[rows, :], nt, preferred_element_type=F32)
        keep = lax.broadcasted_iota(jnp.int32, s.shape, 1) <= lax.broadcasted_iota(jnp.int32, s.shape, 0) + qi * CH
        return [(rows, jnp.where(keep, s, -1e30))]
    if qi > 0:
        rows = slice(0, qi * CH)
        pieces.append((rows, lax.dot_general(q, k_ref[rows, :], nt, preferred_element_type=F32)))
    rows = slice(qi * CH, (qi + 1) * CH)
    s = lax.dot_general(q, k_ref[rows, :], nt, preferred_element_type=F32)
    keep = lax.broadcasted_iota(jnp.int32, s.shape, 1) <= lax.broadcasted_iota(jnp.int32, s.shape, 0)
    pieces.append((rows, jnp.where(keep, s, -1e30)))
    return pieces


def attn_fwd(q, k, v, *, nch, name):
    T = q.shape[0]
    H = q.shape[1] // HEAD_PAD
    CH = T // nch
    assert CH * nch == T and CH % 16 == 0

    def body(q_ref, k_ref, v_ref, o_ref, lse_ref):
        for qi in range(nch):
            L = (qi + 1) * CH
            pieces = _causal_scores(q_ref[qi * CH:L, :], k_ref, qi, CH, True)
            m = functools.reduce(jnp.maximum, [jnp.max(s, axis=1, keepdims=True) for _, s in pieces])
            l, o = 0.0, 0.0
            for rows, s in pieces:
                p = jnp.exp(s - m)
                l = l + jnp.sum(p, axis=1, keepdims=True)
                o = o + jnp.dot(p.astype(BF), v_ref[rows, :], preferred_element_type=F32)
            o_ref[qi * CH:L, :] = (o / l).astype(BF)
            lse_ref[qi * CH:L, :] = jnp.broadcast_to(m + jnp.log(l), (CH, V_HEAD))

    return pl.pallas_call(
        body, name=name, grid=(H,),
        in_specs=[pl.BlockSpec((T, HEAD_PAD), lambda h: (0, h)), pl.BlockSpec((T, HEAD_PAD), lambda h: (0, h)),
                  pl.BlockSpec((T, V_HEAD), lambda h: (0, h))],
        out_specs=[pl.BlockSpec((T, V_HEAD), lambda h: (0, h)), pl.BlockSpec((T, V_HEAD), lambda h: (0, h))],
        out_shape=[jax.ShapeDtypeStruct((T, H * V_HEAD), BF), jax.ShapeDtypeStruct((T, H * V_HEAD), F32)],
        compiler_params=_cparams(("parallel",)),
    )(q, k, v)


def attn_bwd(q, k, v, o, do, lse, cos, sin, *, scale, nch, name):
    T = q.shape[0]
    H = q.shape[1] // HEAD_PAD
    CH = T // nch
    tn_dims = (((0,), (0,)), ((), ()))
    nt_dims = (((1,), (1,)), ((), ()))

    def body(q_ref, k_ref, v_ref, o_ref, do_ref, lse_ref, c_ref, s_ref,
             dq_ref, dk_ref, dv_ref, dkr_ref, dk_acc, dv_acc):
        h = pl.program_id(0)
        dk_acc[...] = jnp.zeros_like(dk_acc)
        dv_acc[...] = jnp.zeros_like(dv_acc)
        for qi in range(nch):
            L = (qi + 1) * CH
            rows = slice(qi * CH, L)
            qc, doc = q_ref[rows, :], do_ref[rows, :]
            lse_c = lse_ref[rows, 0:1]
            dsum = jnp.sum(doc.astype(F32) * o_ref[rows, :].astype(F32), axis=1, keepdims=True)
            dq = 0.0
            for krows, s in _causal_scores(qc, k_ref, qi, CH, False):
                p = jnp.exp(s - lse_c)
                dp = lax.dot_general(doc, v_ref[krows, :], nt_dims, preferred_element_type=F32)
                ds = (p * (dp - dsum)).astype(BF)
                dq = dq + jnp.dot(ds, k_ref[krows, :], preferred_element_type=F32)
                dk_acc[krows, :] += lax.dot_general(ds, qc, tn_dims, preferred_element_type=F32)
                dv_acc[krows, :] += lax.dot_general(p.astype(BF), doc, tn_dims, preferred_element_type=F32)
            dq_ref[rows, :QK_NOPE] = (dq[:, :QK_NOPE] * scale).astype(BF)
            dyr = dq[:, QK_NOPE:]
            dq_ref[rows, QK_NOPE:] = ((dyr * c_ref[rows, :] - _rot_half(dyr * s_ref[rows, :])) * scale).astype(BF)
        dk_ref[:, :QK_NOPE] = dk_acc[:, :QK_NOPE].astype(BF)
        dk_ref[:, QK_NOPE:] = jnp.zeros((T, HEAD_PAD - QK_NOPE), BF)
        dv_ref[...] = dv_acc[...].astype(BF)

        @pl.when(h == 0)
        def _():
            dkr_ref[...] = dk_acc[:, QK_NOPE:]

        @pl.when(h > 0)
        def _():
            dkr_ref[...] += dk_acc[:, QK_NOPE:]

    hp = pl.BlockSpec((T, HEAD_PAD), lambda h: (0, h))
    hv = pl.BlockSpec((T, V_HEAD), lambda h: (0, h))
    tab = pl.BlockSpec((T, 128), lambda h: (0, 0))
    return pl.pallas_call(
        body, name=name, grid=(H,),
        in_specs=[hp, hp, hv, hv, hv, hv, tab, tab],
        out_specs=[hp, hp, hv, tab],
        out_shape=[jax.ShapeDtypeStruct((T, H * HEAD_PAD), BF), jax.ShapeDtypeStruct((T, H * HEAD_PAD), BF),
                   jax.ShapeDtypeStruct((T, H * V_HEAD), BF), jax.ShapeDtypeStruct((T, 128), F32)],
        scratch_shapes=[pltpu.VMEM((T, HEAD_PAD), F32), pltpu.VMEM((T, V_HEAD), F32)],
        compiler_params=_cparams(("arbitrary",)),
    )(q, k, v, o, do, lse, cos, sin)


def _place():
    x, y, c = lax.axis_index("x"), lax.axis_index("y"), lax.axis_index("c")
    return x, y, c


def _other_chips(x, y):
    return [(1 - x, y), (x, 1 - y), (1 - x, 1 - y)]


HBM_SPEC = pl.BlockSpec(memory_space=pltpu.HBM)
SEM_SPEC = pl.BlockSpec(memory_space=pltpu.SEMAPHORE)
DATAFLOW = pltpu.SideEffectType.DATAFLOW_SIDE_EFFECTING


def _hbm(a):
    return pltpu.with_memory_space_constraint(a, pltpu.HBM)


def _half(ref, lead, c, axis):
    rows, cols = ref.shape[-2], ref.shape[-1]
    if axis == 0:
        return ref.at[(*lead, pl.ds(c * (rows // 2), rows // 2))]
    return ref.at[(*lead, slice(None), pl.ds(c * (cols // 2), cols // 2))]


def copies_start(arrs, plan, n_sems, after, *, name):
    n = len(arrs)

    def body(*refs):
        outs = refs[n + 1:2 * n + 1]
        send_sems, recv_sems, token = refs[2 * n + 1:]
        for cp in plan(outs, send_sems, recv_sems, False):
            cp.start()
        token[...] = jnp.zeros_like(token)

    res = pl.pallas_call(
        body, name=name,
        out_shape=tuple(pltpu.HBM(a.shape, a.dtype) for a in arrs)
        + (pltpu.SemaphoreType.DMA((n_sems,)), pltpu.SemaphoreType.DMA((n_sems,)), jax.ShapeDtypeStruct((8, 128), F32)),
        in_specs=(HBM_SPEC,) * n + (pl.BlockSpec(memory_space=pl.ANY),),
        out_specs=(HBM_SPEC,) * n + (SEM_SPEC, SEM_SPEC, pl.BlockSpec(memory_space=pltpu.VMEM)),
        input_output_aliases={i: i for i in range(n)},
        compiler_params=pltpu.CompilerParams(has_side_effects=DATAFLOW),
    )(*[_hbm(a) for a in arrs], after)
    return list(res[:n]), res[n], res[n + 1], res[n + 2]


def copies_wait(arrs, plan, send_sems, recv_sems, after, *, name):
    n = len(arrs)
    afters = list(after) if isinstance(after, (list, tuple)) else [after]

    def body(*refs):
        for cp in plan(refs[:n], refs[n], refs[n + 1], True):
            cp.wait_send()
            cp.wait_recv()

    res = pl.pallas_call(
        body, name=name,
        out_shape=tuple(pltpu.HBM(a.shape, a.dtype) for a in arrs),
        in_specs=(HBM_SPEC,) * n + (SEM_SPEC, SEM_SPEC) + (pl.BlockSpec(memory_space=pl.ANY),) * len(afters),
        out_specs=(HBM_SPEC,) * n,
        input_output_aliases={i: i for i in range(n)},
        compiler_params=pltpu.CompilerParams(has_side_effects=DATAFLOW),
    )(*arrs, send_sems, recv_sems, *afters)
    return list(res)


def _half_part(ref, lead, c, axis, part):
    rows, cols = ref.shape[-2], ref.shape[-1]
    span = rows // 2 if axis == 0 else rows
    p1 = -(-(span // 2) // 16) * 16
    lo, n = (0, p1) if part == 0 else (p1, span - p1)
    if axis == 0:
        return ref.at[(*lead, pl.ds(c * span + lo, n))]
    return ref.at[(*lead, pl.ds(lo, n), pl.ds(c * (cols // 2), cols // 2))]


def near_plan(axes):
    def plan(bufs, send_sems, recv_sems, receiving):
        x, y, c = _place()
        me = 2 * x + y
        out = []
        for i, b in enumerate(bufs):
            for j, (cx, cy) in enumerate(_other_chips(x, y)[:2]):
                src = _half(b, (me,), c, axes[i])
                dst = _half(b, (2 * cx + cy,), c, axes[i]) if receiving else src
                out.append(pltpu.make_async_remote_copy(
                    src_ref=src, dst_ref=dst, send_sem=send_sems.at[2 * i + j], recv_sem=recv_sems.at[2 * i + j],
                    device_id=(cx, cy, c), device_id_type=MESH))
        return out
    return plan


def relay_plan(axes):
    def plan(bufs, send_sems, recv_sems, receiving):
        x, y, c = _place()
        (xn, yn, dg) = [2 * cx + cy for cx, cy in _other_chips(x, y)]
        to = [(x, 1 - y, c), (1 - x, y, c), (x, y, 1 - c), (x, y, 1 - c)]
        out = []
        for i, b in enumerate(bufs):
            if receiving:
                blks = [_half_part(b, (dg,), c, axes[i], 0), _half_part(b, (dg,), c, axes[i], 1),
                        _half(b, (xn,), 1 - c, axes[i]), _half(b, (yn,), 1 - c, axes[i])]
            else:
                blks = [_half_part(b, (xn,), c, axes[i], 0), _half_part(b, (yn,), c, axes[i], 1),
                        _half(b, (xn,), c, axes[i]), _half(b, (yn,), c, axes[i])]
            for k in range(4):
                out.append(pltpu.make_async_remote_copy(
                    src_ref=blks[k], dst_ref=blks[k], send_sem=send_sems.at[4 * i + k],
                    recv_sem=recv_sems.at[4 * i + k], device_id=to[k], device_id_type=MESH))
        return out
    return plan


def far_forward_plan(axes):
    def plan(bufs, send_sems, recv_sems, receiving):
        x, y, c = _place()
        dg = 2 * (1 - x) + (1 - y)
        out = []
        for i, b in enumerate(bufs):
            blk = _half(b, (dg,), (1 - c) if receiving else c, axes[i])
            out.append(pltpu.make_async_remote_copy(
                src_ref=blk, dst_ref=blk, send_sem=send_sems.at[i], recv_sem=recv_sems.at[i],
                device_id=(x, y, 1 - c), device_id_type=MESH))
        return out
    return plan


def share_plan(axes):
    def plan(bufs, send_sems, recv_sems, receiving):
        x, y, c = _place()
        out = []
        for i, b in enumerate(bufs):
            blk = _half(b, (), (1 - c) if receiving else c, axes[i])
            out.append(pltpu.make_async_remote_copy(
                src_ref=blk, dst_ref=blk, send_sem=send_sems.at[i], recv_sem=recv_sems.at[i],
                device_id=(x, y, 1 - c), device_id_type=MESH))
        return out
    return plan


def scatter_plan(n):
    def plan(arrs, send_sems, recv_sems, receiving):
        x, y, c = _place()
        out = []
        for i in range(n):
            for j, (cx, cy) in enumerate(_other_chips(x, y)):
                out.append(pltpu.make_async_remote_copy(
                    src_ref=arrs[i].at[2 * cx + cy], dst_ref=arrs[n + i].at[j], send_sem=send_sems.at[3 * i + j],
                    recv_sem=recv_sems.at[3 * i + j], device_id=(cx, cy, c), device_id_type=MESH))
        return out
    return plan


def exchange_plan(axes):
    n = len(axes)

    def plan(arrs, send_sems, recv_sems, receiving):
        x, y, c = _place()
        out = []
        for i in range(n):
            for k in range(N_CHIPS):
                out.append(pltpu.make_async_remote_copy(
                    src_ref=_half(arrs[i], (k,), 1 - c, axes[i]), dst_ref=arrs[n + i].at[k],
                    send_sem=send_sems.at[N_CHIPS * i + k], recv_sem=recv_sems.at[N_CHIPS * i + k],
                    device_id=(x, y, 1 - c), device_id_type=MESH))
        return out
    return plan


def devices_plan(arrs, send_sems, recv_sems, receiving):
    x, y, c = _place()
    out = []
    for k in range(1, N_DEV):
        px, py, pc = x ^ ((k >> 2) & 1), y ^ ((k >> 1) & 1), c ^ (k & 1)
        slot = (4 * px + 2 * py + pc) if receiving else (4 * x + 2 * y + c)
        out.append(pltpu.make_async_remote_copy(
            src_ref=arrs[0], dst_ref=arrs[1].at[slot], send_sem=send_sems.at[k - 1], recv_sem=recv_sems.at[k - 1],
            device_id=(px, py, pc), device_id_type=MESH))
    return out


def _row_tile(r, cols, itemsize, target_bytes=2 * 1024 * 1024):
    best = None
    for t in range(16, r + 1, 16):
        if r % t == 0 and t * cols * itemsize <= target_bytes:
            best = t
    return best if best is not None else r


def pair_sum(g, recv, half, axis, *, name):
    _, hr, hc = recv.shape
    tr = _row_tile(hr, hc, 4)
    nb = hr // tr
    if axis == 0:
        mine = pl.BlockSpec((None, tr, hc), lambda k, r, hf: (k, hf[0] * nb + r, 0))
    else:
        mine = pl.BlockSpec((None, tr, hc), lambda k, r, hf: (k, r, hf[0]))

    def body(half_ref, a_ref, b_ref, o_ref):
        o_ref[...] = (a_ref[...].astype(F32) + b_ref[...].astype(F32)).astype(BF)

    return pl.pallas_call(
        body, name=name,
        grid_spec=pltpu.PrefetchScalarGridSpec(
            num_scalar_prefetch=1, grid=(N_CHIPS, nb),
            in_specs=[mine, pl.BlockSpec((None, tr, hc), lambda k, r, hf: (k, r, 0))],
            out_specs=pl.BlockSpec((None, tr, hc), lambda k, r, hf: (k, r, 0))),
        out_shape=jax.ShapeDtypeStruct((N_CHIPS, hr, hc), BF),
        compiler_params=_cparams(("parallel", "parallel")),
    )(half, g, recv)


def chip_sum(s, recv, place, axis, *, name):
    _, hr, hc = s.shape
    tr = _row_tile(hr, hc, 4)
    nb = hr // tr
    if axis == 0:
        o_spec = pl.BlockSpec((tr, hc), lambda r, pc: (pc[1] * nb + r, 0))
        o_shape = (2 * hr, hc)
    else:
        o_spec = pl.BlockSpec((tr, hc), lambda r, pc: (r, pc[1]))
        o_shape = (hr, 2 * hc)

    def body(place_ref, a_ref, b_ref, o_ref):
        acc = a_ref[...].astype(F32)
        for j in range(3):
            acc = acc + b_ref[j].astype(F32)
        o_ref[...] = acc

    return pl.pallas_call(
        body, name=name,
        grid_spec=pltpu.PrefetchScalarGridSpec(
            num_scalar_prefetch=1, grid=(nb,),
            in_specs=[pl.BlockSpec((None, tr, hc), lambda r, pc: (pc[0], r, 0)),
                      pl.BlockSpec((3, tr, hc), lambda r, pc: (0, r, 0))],
            out_specs=o_spec),
        out_shape=jax.ShapeDtypeStruct(o_shape, F32),
        compiler_params=_cparams(("parallel",)),
    )(place, s, recv)


def sum_devices(v, *, name):
    _, R, C = v.shape

    def body(v_ref, o_ref):
        acc = v_ref[0]
        for k in range(1, N_DEV):
            acc = acc + v_ref[k]
        o_ref[...] = acc

    return pl.pallas_call(body, name=name, out_shape=jax.ShapeDtypeStruct((R, C), F32))(v)


def adamw(w, g, m, v, *, name):
    R, C = w.shape
    tr = _row_tile(R, C, 4)
    c1 = 1.0 / (1.0 - ADAM_B1 ** ADAM_STEP)
    c2 = 1.0 / (1.0 - ADAM_B2 ** ADAM_STEP)

    def body(w_ref, g_ref, m_ref, v_ref, go_ref, d_ref, nm_ref, nv_ref):
        gg = g_ref[...]
        go_ref[...] = gg
        nm = ADAM_B1 * m_ref[...] + (1.0 - ADAM_B1) * gg
        nv = ADAM_B2 * v_ref[...] + (1.0 - ADAM_B2) * (gg * gg)
        nm_ref[...] = nm
        nv_ref[...] = nv
        d_ref[...] = -ADAM_LR * ((nm * c1) / (jnp.sqrt(nv * c2) + ADAM_EPS) + ADAM_WD * w_ref[...])

    spec = pl.BlockSpec((tr, C), lambda i: (i, 0))
    return pl.pallas_call(
        body, name=name, grid=(R // tr,), in_specs=[spec] * 4, out_specs=[spec] * 4,
        out_shape=[jax.ShapeDtypeStruct((R, C), F32)] * 4, compiler_params=_cparams(("parallel",)),
    )(w, g, m, v)


def _rope_tables(T):
    pos = np.arange(T, dtype=np.float32)
    inv_freq = (ROPE_THETA ** (-np.arange(0, QK_ROPE, 2, dtype=np.float32) / QK_ROPE)).astype(np.float32)
    ang = pos[:, None] * inv_freq[None, :]
    z = np.zeros((T, 64), np.float32)
    cos = np.concatenate([np.cos(ang), np.cos(ang), z], axis=1).astype(np.float32)
    sin = np.concatenate([np.sin(ang), np.sin(ang), z], axis=1).astype(np.float32)
    return jnp.asarray(cos), jnp.asarray(sin)


def _padded_rows(length):
    T = -(-length // 16) * 16
    while True:
        for n in (4, 5, 6, 7, 8):
            if T % (16 * n) == 0:
                return T, n
        T += 16


def _pick(n, cands):
    for c in cands:
        if n % c == 0:
            return c
    return n


def kernel(x, meta_tokens, g_mix, w_in, b_glu, b_gate, w_dw, b_dw, g_conv_ln, b_conv_ln, w_conv_out, b_conv_out, g_q_lora, w_uq, g_kv_lora, w_uk, w_uv, w_attn_out, w_out, g_ffn, w_ffn_gate, w_ffn_up, w_ffn_down, g_final, loss_target, m_meta_tokens, m_g_mix, m_w_in, m_b_glu, m_b_gate, m_w_dw, m_b_dw, m_g_conv_ln, m_b_conv_ln, m_w_conv_out, m_b_conv_out, m_g_q_lora, m_w_uq, m_g_kv_lora, m_w_uk, m_w_uv, m_w_attn_out, m_w_out, m_g_ffn, m_w_ffn_gate, m_w_ffn_up, m_w_ffn_down, m_g_final, v_meta_tokens, v_g_mix, v_w_in, v_b_glu, v_b_gate, v_w_dw, v_b_dw, v_g_conv_ln, v_b_conv_ln, v_w_conv_out, v_b_conv_out, v_g_q_lora, v_w_uq, v_g_kv_lora, v_w_uk, v_w_uv, v_w_attn_out, v_w_out, v_g_ffn, v_w_ffn_gate, v_w_ffn_up, v_w_ffn_down, v_g_final):
    weights = dict(meta_tokens=meta_tokens, g_mix=g_mix, w_in=w_in, b_glu=b_glu, b_gate=b_gate, w_dw=w_dw, b_dw=b_dw,
                   g_conv_ln=g_conv_ln, b_conv_ln=b_conv_ln, w_conv_out=w_conv_out, b_conv_out=b_conv_out,
                   g_q_lora=g_q_lora, w_uq=w_uq, g_kv_lora=g_kv_lora, w_uk=w_uk, w_uv=w_uv, w_attn_out=w_attn_out,
                   w_out=w_out, g_ffn=g_ffn, w_ffn_gate=w_ffn_gate, w_ffn_up=w_ffn_up, w_ffn_down=w_ffn_down,
                   g_final=g_final)
    m_in = dict(meta_tokens=m_meta_tokens, g_mix=m_g_mix, w_in=m_w_in, b_glu=m_b_glu, b_gate=m_b_gate, w_dw=m_w_dw,
                b_dw=m_b_dw, g_conv_ln=m_g_conv_ln, b_conv_ln=m_b_conv_ln, w_conv_out=m_w_conv_out,
                b_conv_out=m_b_conv_out, g_q_lora=m_g_q_lora, w_uq=m_w_uq, g_kv_lora=m_g_kv_lora, w_uk=m_w_uk,
                w_uv=m_w_uv, w_attn_out=m_w_attn_out, w_out=m_w_out, g_ffn=m_g_ffn, w_ffn_gate=m_w_ffn_gate,
                w_ffn_up=m_w_ffn_up, w_ffn_down=m_w_ffn_down, g_final=m_g_final)
    v_in = dict(meta_tokens=v_meta_tokens, g_mix=v_g_mix, w_in=v_w_in, b_glu=v_b_glu, b_gate=v_b_gate, w_dw=v_w_dw,
                b_dw=v_b_dw, g_conv_ln=v_g_conv_ln, b_conv_ln=v_b_conv_ln, w_conv_out=v_w_conv_out,
                b_conv_out=v_b_conv_out, g_q_lora=v_g_q_lora, w_uq=v_w_uq, g_kv_lora=v_g_kv_lora, w_uk=v_w_uk,
                w_uv=v_w_uv, w_attn_out=v_w_attn_out, w_out=v_w_out, g_ffn=v_g_ffn, w_ffn_gate=v_w_ffn_gate,
                w_ffn_up=v_w_ffn_up, w_ffn_down=v_w_ffn_down, g_final=v_g_final)
    names = list(weights)

    seq, D = x.shape[1], x.shape[2]
    C = w_conv_out.shape[2]
    R = g_q_lora.shape[1]
    H = w_uv.shape[2] * N_CHIPS // V_HEAD
    FB = w_ffn_gate.shape[2]
    FF = FB * N_CHIPS
    length = N_META + seq
    T, nch = _padded_rows(length)
    unit = T // nch
    scale = (QK_NOPE + QK_ROPE) ** -0.5
    assert C == D and w_in.shape[2] * N_CHIPS == 2 * C + 2 * R + QK_ROPE + 2 * D

    xi, yi, ci = _place()
    chip = (2 * xi + yi).astype(jnp.int32)
    half_arr = ci.astype(jnp.int32).reshape(1)

    tr = unit
    tmh = T // 2 if (T // 2) % 16 == 0 else unit
    tmq = unit
    tmw = _pick(D, (1024, 512))
    rc = unit // 2 if unit % 16 == 0 else unit
    tcv = 128

    def tn_of(n, pref=512):
        return _pick(n, (pref, 384, 256, 128))

    grp_a = ["w_in"]
    grp_b = ["w_conv_out", "w_uq", "w_uk", "w_uv", "w_attn_out", "w_out"]
    grp_c = ["w_ffn_gate", "w_ffn_up", "w_ffn_down"]
    big = grp_a + grp_b + grp_c
    axes = {nm: 0 for nm in big}
    axes["w_in"] = 1
    shard = {nm: weights[nm][0] for nm in big}
    shard["w_in"] = jnp.swapaxes(w_in[0], 0, 1)

    def own_blocks(srcs, dtype=BF):
        bufs = []
        for w in srcs:
            wb = w.astype(dtype)
            bufs.append(lax.dynamic_update_slice(lax.empty((N_CHIPS,) + wb.shape, dtype), wb[None], (chip, 0, 0)))
        return bufs

    tiny = jnp.concatenate([meta_tokens, jnp.pad(w_dw[0], ((0, 1), (0, 0)))], axis=0)
    ax_a = [axes[nm] for nm in grp_a] + [0]
    bufs_a, ss_a, rs_a, tok_ga = copies_start(own_blocks([shard[nm] for nm in grp_a]) + own_blocks([tiny], F32),
                                              near_plan(ax_a), 2 * len(ax_a), half_arr, name="gather_a_near_start")
    tok_ga, held = lax.optimization_barrier((tok_ga, (x, loss_target, [shard[nm] for nm in grp_b + grp_c])))
    x_l, tgt_l, shard_bc = held
    h0 = jnp.pad(x_l[0], ((N_META, T - length), (0, 0)))
    tgt = jnp.pad(tgt_l[0], ((N_META, T - length), (0, 0)))
    bufs_b, bufs_c = own_blocks(shard_bc[:len(grp_b)]), own_blocks(shard_bc[len(grp_b):])
    cos, sin = _rope_tables(T)

    bufs_a = copies_wait(bufs_a, near_plan(ax_a), ss_a, rs_a, [h0, tgt] + bufs_b + bufs_c, name="gather_a_near_wait")
    bufs_a, ss_a, rs_a, _ = copies_start(bufs_a, relay_plan(ax_a), 4 * len(ax_a), half_arr,
                                         name="gather_a_relay_start")
    bufs_a = copies_wait(bufs_a, relay_plan(ax_a), ss_a, rs_a, half_arr, name="gather_a_relay_wait")
    bufs_a, ss_a, rs_a, _ = copies_start(bufs_a, far_forward_plan(ax_a), len(ax_a), half_arr,
                                         name="gather_a_far_start")
    bufs_a = copies_wait(bufs_a, far_forward_plan(ax_a), ss_a, rs_a, half_arr, name="gather_a_far_wait")
    G = dict(zip(grp_a, bufs_a))
    meta_full = jnp.concatenate([bufs_a[-1][k, :N_META] for k in range(N_CHIPS)], axis=1)
    w_dw_full = jnp.concatenate([bufs_a[-1][k, N_META:] for k in range(N_CHIPS)], axis=1)
    h0 = lax.dynamic_update_slice(h0, meta_full, (0, 0))
    ax_b = [axes[nm] for nm in grp_b]
    bufs_b, ss_b, rs_b, tok_gb = copies_start(bufs_b, near_plan(ax_b), 2 * len(grp_b), G["w_in"],
                                              name="gather_b_near_start")

    WT_in = G["w_in"].reshape(N_CHIPS * G["w_in"].shape[1], D)
    o1, o2 = 2 * C, 2 * C + 2 * R + QK_ROPE
    WT_small = jnp.pad(WT_in[o1:o2], ((0, 128 - QK_ROPE), (0, 0)))
    WT_gate = WT_in[o2:]

    u = rms_fwd(h0, g_mix, tr=tr, name="rms_mix", deps=[tok_gb])
    z_glu = mm(u, WT_in, mode="nt", tm=T, tn=tn_of(2 * C, 1024), tk=D, out_dtype=BF, name="mm_z_glu", n_out=o1)
    z_small = mm(u, WT_small, mode="nt", tm=T, tn=2 * R + 128, tk=D, out_dtype=F32, name="mm_z_small")
    z_gate = mm(u, WT_gate, mode="nt", tm=T, tn=tn_of(2 * D, 1024), tk=D, out_dtype=BF, name="mm_z_gate")

    bufs_b = copies_wait(bufs_b, near_plan(ax_b), ss_b, rs_b, z_glu, name="gather_b_near_wait")
    bufs_b, ss_b, rs_b, tok_gb = copies_start(bufs_b, relay_plan(ax_b), 4 * len(grp_b), half_arr,
                                              name="gather_b_relay_start")
    ax_c = [axes[nm] for nm in grp_c]
    bufs_c, ss_c, rs_c, tok_gc = copies_start(bufs_c, near_plan(ax_c), 2 * len(grp_c), tok_gb,
                                              name="gather_c_near_start")
    c0 = glu_fwd(z_glu, b_glu, tr=tr, name="glu_fwd", deps=[tok_gc])
    c1 = dwconv_fwd(c0, w_dw_full, b_dw, tc=tcv, rc=rc, name="dwconv_fwd")
    bufs_b = copies_wait(bufs_b, relay_plan(ax_b), ss_b, rs_b, c1, name="gather_b_relay_wait")
    bufs_b, ss_b, rs_b, tok_gb = copies_start(bufs_b, far_forward_plan(ax_b), len(grp_b), half_arr,
                                              name="gather_b_far_start")
    c3 = ln_silu_fwd(c1, g_conv_ln, b_conv_ln, tr=tr, name="ln_silu_fwd")
    cq, ckv, kr = lora_norm_fwd(z_small, g_q_lora, g_kv_lora, cos, sin, tr=tr, name="lora_norm_fwd", deps=[tok_gb])

    def cols_full(a):
        return jnp.concatenate([a[k] for k in range(N_CHIPS)], axis=1)

    G.update(zip(grp_b, copies_wait(bufs_b, far_forward_plan(ax_b), ss_b, rs_b, cq, name="gather_b_far_wait")))
    W_co = G["w_conv_out"].reshape(C, D)
    W_uq = cols_full(G["w_uq"]).reshape(R, H, QK_NOPE + QK_ROPE)
    Wq_p = jnp.pad(W_uq, ((0, 0), (0, 0), (0, HEAD_PAD - QK_NOPE - QK_ROPE))).reshape(R, H * HEAD_PAD)
    W_uk = cols_full(G["w_uk"])
    W_uv = cols_full(G["w_uv"])
    W_ao = G["w_attn_out"].reshape(H * V_HEAD, D)
    W_o = G["w_out"].reshape(D, D)

    y_conv = mm(c3, W_co, mode="nn", tm=T, tn=tn_of(D, 1024), tk=C, out_dtype=BF, name="mm_y_conv")

    q = q_proj(cq, Wq_p, cos, sin, scale=scale, name="q_proj")
    kk = k_proj(ckv, W_uk, kr, name="k_proj")
    vv = mm(ckv, W_uv, mode="nn", tm=T, tn=tn_of(H * V_HEAD), tk=R, out_dtype=BF, name="mm_v")
    o_attn, lse = attn_fwd(q, kk, vv, nch=nch, name="attn_fwd")
    bufs_c = copies_wait(bufs_c, near_plan(ax_c), ss_c, rs_c, o_attn, name="gather_c_near_wait")
    bufs_c, ss_c, rs_c, tok_gc = copies_start(bufs_c, relay_plan(ax_c), 4 * len(grp_c), half_arr,
                                              name="gather_c_relay_start")
    y_attn = mm(o_attn, W_ao, mode="nn", tm=T, tn=tn_of(D, 1024), tk=H * V_HEAD, out_dtype=BF, name="mm_y_attn",
                deps=[tok_gc])

    mix = mix_fwd(z_gate, y_conv, y_attn, b_gate, b_conv_out, tr=tr, name="mix_fwd")
    bufs_c = copies_wait(bufs_c, relay_plan(ax_c), ss_c, rs_c, mix, name="gather_c_relay_wait")
    bufs_c, ss_c, rs_c, tok_gc = copies_start(bufs_c, far_forward_plan(ax_c), len(grp_c), half_arr,
                                              name="gather_c_far_start")
    h1 = mm(mix, W_o, mode="nn", tm=T, tn=tn_of(D), tk=D, out_dtype=F32, name="mm_h1", res=h0, deps=[tok_gc])

    hn = rms_fwd(h1, g_ffn, tr=tr, name="rms_ffn")

    G.update(zip(grp_c, copies_wait(bufs_c, far_forward_plan(ax_c), ss_c, rs_c, hn, name="gather_c_far_wait")))
    W_fg, W_fu = G["w_ffn_gate"], G["w_ffn_up"]
    W_fd = G["w_ffn_down"].reshape(FF, D)

    fa, fb, f = ffn_fwd(hn, W_fg, W_fu, tm=tmq, name="ffn_fwd")
    h2 = mm(f, W_fd, mode="nn", tm=tmh, tn=tn_of(D), tk=FF, out_dtype=F32, name="mm_h2", res=h1)

    place_arr = jnp.stack([chip, ci.astype(jnp.int32)])

    def col_blocks(a):
        r, cfull = a.shape
        return a.reshape(r, N_CHIPS, cfull // N_CHIPS).transpose(1, 0, 2)

    def rs_begin(nms, gls, after, tag):
        ax = [axes[nm] for nm in nms]
        lands = []
        for g, a in zip(gls, ax):
            _, r, cfull = g.shape
            lands.append(lax.empty((N_CHIPS, r // 2, cfull) if a == 0 else (N_CHIPS, r, cfull // 2), BF))
        arrs, ssem, rsem, tok = copies_start(list(gls) + lands, exchange_plan(ax), N_CHIPS * len(nms), after,
                                             name=f"rs_{tag}_exchange_start")
        return (nms, ax, arrs, ssem, rsem), tok

    def rs_middle(state, after, tag):
        nms, ax, arrs, ssem, rsem = state
        n = len(nms)
        arrs = copies_wait(arrs, exchange_plan(ax), ssem, rsem, after, name=f"rs_{tag}_exchange_wait")
        pair = [pair_sum(g, r, half_arr, a, name="rs_pair_sum_" + nm)
                for nm, g, r, a in zip(nms, arrs[:n], arrs[n:], ax)]
        lands = [lax.empty((3,) + p.shape[1:], BF) for p in pair]
        arrs, ssem, rsem, tok = copies_start(pair + lands, scatter_plan(n), 3 * n, half_arr,
                                             name=f"rs_{tag}_scatter_start")
        return (nms, ax, arrs, ssem, rsem), tok

    def rs_end(state, after, tag):
        nms, ax, arrs, ssem, rsem = state
        n = len(nms)
        arrs = copies_wait(arrs, scatter_plan(n), ssem, rsem, after, name=f"rs_{tag}_scatter_wait")
        return [chip_sum(s, r, place_arr, a, name="rs_chip_sum_" + nm)
                for nm, s, r, a in zip(nms, arrs[:n], arrs[n:], ax)]

    def rs_shared(nms, reds, after, tag):
        ax = [axes[nm] for nm in nms]
        reds, ssem, rsem, _ = copies_start(reds, share_plan(ax), len(nms), after, name=f"rs_{tag}_share_start")
        return dict(zip(nms, copies_wait(reds, share_plan(ax), ssem, rsem, half_arr, name=f"rs_{tag}_share_wait")))

    d_h2, d_h2b, loss_p, d_g_final = final_loss(h2, tgt, g_final.reshape(1, D), seq=seq, tr=tr, name="final_loss")
    loss = lax.psum(loss_p[0, 0], ("x", "y", "c"))

    dW_fd = mm(f, d_h2b, mode="tn", tm=FB, tn=tn_of(D, 1024), tk=T, out_dtype=BF, name="mm_dw_ffn_down")
    d_a, d_b = ffn_dact(d_h2b, W_fd, fa, fb, tm=tmq, name="ffn_dact")
    dW_fg = mm(hn, d_a, mode="tn", tm=tmw, tn=FB, tk=T, out_dtype=BF, name="mm_dw_ffn_gate", out_blocks=N_CHIPS)
    dW_fu = mm(hn, d_b, mode="tn", tm=tmw, tn=FB, tk=T, out_dtype=BF, name="mm_dw_ffn_up", out_blocks=N_CHIPS)
    st_c, tok_rc = rs_begin(grp_c, [dW_fg, dW_fu, dW_fd.reshape(N_CHIPS, FB, D)], half_arr, "c")
    d_hn = ffn_dhn(d_a, d_b, W_fg, W_fu, tm=tmq, tn=_pick(D, (512, 256)), name="ffn_dhn", deps=[tok_rc])
    st_c, tok_rc = rs_middle(st_c, d_hn, "c")
    d_h1, d_h1b, d_g_ffn = rms_bwd(d_hn, h1, g_ffn, d_h2, tr=tr, name="rms_ffn_bwd", deps=[tok_rc])

    d_mix = mm(d_h1b, W_o, mode="nt", tm=T, tn=tn_of(D, 1024), tk=D, out_dtype=BF, name="mm_d_mix")
    dW_o = mm(mix, d_h1b, mode="tn", tm=tmw, tn=tn_of(D, 1024), tk=T, out_dtype=BF, name="mm_dw_out")
    d_yc, d_ya, d_zgate, d_b_gate, d_b_co = mix_bwd(z_gate, y_conv, y_attn, d_mix, b_gate, b_conv_out, tr=tr,
                                                    name="mix_bwd")

    dW_co = mm(c3, d_yc, mode="tn", tm=tmw, tn=tn_of(D, 1024), tk=T, out_dtype=BF, name="mm_dw_conv_out")
    d_c3 = mm(d_yc, W_co, mode="nt", tm=T, tn=tn_of(C, 1024), tk=D, out_dtype=BF, name="mm_d_c3")
    d_c1, d_g_ln, d_b_ln, d_b_dw = ln_silu_bwd(c1, d_c3, g_conv_ln, b_conv_ln, tr=tr, name="ln_silu_bwd")
    d_c0, d_w_dw = dwconv_bwd(c0, d_c1, w_dw_full, tc=tcv, rc=rc, name="dwconv_bwd")
    d_zglu, d_b_glu = glu_bwd(z_glu, b_glu, d_c0, tr=tr, name="glu_bwd")

    dW_ao = mm(o_attn, d_ya, mode="tn", tm=_pick(H * V_HEAD, (1024, 512)), tn=tn_of(D, 1024), tk=T, out_dtype=BF, name="mm_dw_attn_out")
    d_o = mm(d_ya, W_ao, mode="nt", tm=T, tn=tn_of(H * V_HEAD, 1024), tk=D, out_dtype=BF, name="mm_d_o")
    d_q2, d_k2, d_v, d_kr = attn_bwd(q, kk, vv, o_attn, d_o, lse, cos, sin, scale=scale, nch=nch, name="attn_bwd")

    dWq_p = mm(cq, d_q2, mode="tn", tm=R, tn=tn_of(H * HEAD_PAD), tk=T, out_dtype=BF, name="mm_dw_uq")
    d_cq = mm(d_q2, Wq_p, mode="nt", tm=T, tn=R, tk=_pick(H * HEAD_PAD, (1024,)), out_dtype=F32, name="mm_d_cq")
    Wk_p = jnp.pad(W_uk.reshape(R, H, QK_NOPE), ((0, 0), (0, 0), (0, HEAD_PAD - QK_NOPE))).reshape(R, H * HEAD_PAD)
    dWk_p = mm(ckv, d_k2, mode="tn", tm=R, tn=tn_of(H * HEAD_PAD), tk=T, out_dtype=BF, name="mm_dw_uk")
    dW_uv = mm(ckv, d_v, mode="tn", tm=R, tn=tn_of(H * V_HEAD), tk=T, out_dtype=BF, name="mm_dw_uv")
    sh_c = rs_end(st_c, [dWq_p, dWk_p, dW_uv, dW_co, dW_ao, dW_o], "c")
    dW_uq = dWq_p.reshape(R, H, HEAD_PAD)[:, :, :QK_NOPE + QK_ROPE].reshape(R, H * (QK_NOPE + QK_ROPE))
    dW_uk = dWk_p.reshape(R, H, HEAD_PAD)[:, :, :QK_NOPE].reshape(R, H * QK_NOPE)
    st_b, tok_rb = rs_begin(grp_b, [dW_co.reshape(N_CHIPS, C // N_CHIPS, D), col_blocks(dW_uq), col_blocks(dW_uk),
                                    col_blocks(dW_uv), dW_ao.reshape(N_CHIPS, H * V_HEAD // N_CHIPS, D),
                                    dW_o.reshape(N_CHIPS, D // N_CHIPS, D)], sh_c[-1], "b")
    d_ckv = mm(d_k2, Wk_p, mode="nt", tm=T, tn=R, tk=_pick(H * HEAD_PAD, (1024,)), out_dtype=F32, name="mm_d_ckv_k",
               deps=[tok_rb])
    d_ckv = mm(d_v, W_uv, mode="nt", tm=T, tn=R, tk=_pick(H * V_HEAD, (1024,)), out_dtype=F32, name="mm_d_ckv_v",
               res=d_ckv)
    d_zsmall, d_g_q, d_g_kv = lora_norm_bwd(z_small, d_cq, d_ckv, d_kr, cos, sin, g_q_lora, g_kv_lora, tr=tr,
                                            name="lora_norm_bwd")
    st_b, tok_rb = rs_middle(st_b, d_zsmall, "b")

    tmi = _pick(2 * C, (1024, 512))
    dWT_in = lax.empty((WT_in.shape[0], D), BF)
    dWT_in = mm_tn_into(d_zglu, u, dWT_in, 0, tm=tmi, tn=tn_of(D, 1024), name="mm_dw_glu", deps=[tok_rb])
    dWT_in = mm_tn_into(d_zsmall, u, dWT_in, o1, tm=2 * R + 128, tn=tn_of(D), name="mm_dw_small")
    dWT_in = mm_tn_into(d_zgate, u, dWT_in, o2, tm=tmi, tn=tn_of(D, 1024), name="mm_dw_gate")
    sh_b = rs_end(st_b, dWT_in, "b")
    sh_c, ss_sc, rs_sc, tok_sc = copies_start(sh_c, share_plan(ax_c), len(grp_c), sh_b[-1], name="rs_c_share_start")
    st_a, tok_ra = rs_begin(grp_a, [dWT_in.reshape(N_CHIPS, dWT_in.shape[0] // N_CHIPS, D)], tok_sc, "a")
    d_u = mm(d_zglu, WT_in, mode="nn", tm=T, tn=tn_of(D, 256), tk=2 * C, out_dtype=F32, name="mm_d_u_glu",
             deps=[tok_ra])
    st_a, tok_ra = rs_middle(st_a, d_u, "a")
    d_u = mm(d_zsmall, WT_small, mode="nn", tm=T, tn=tn_of(D), tk=2 * R + 128, out_dtype=F32, name="mm_d_u_small",
             res=d_u, deps=[tok_ra])
    d_u = mm(d_zgate, WT_gate, mode="nn", tm=T, tn=tn_of(D, 256), tk=2 * D, out_dtype=F32,
             name="mm_d_u_gate", res=d_u)
    d_h0, _, d_g_mix = rms_bwd(d_u, h0, g_mix, d_h1, tr=tr, name="rms_mix_bwd")

    grad_x = d_h0[N_META:length][None]

    small = ["g_mix", "b_glu", "b_gate", "b_dw", "g_conv_ln", "b_conv_ln", "b_conv_out", "g_q_lora", "g_kv_lora",
             "g_ffn", "g_final", "w_dw", "meta_tokens"]
    sgrads = dict(g_mix=d_g_mix, b_glu=d_b_glu, b_gate=d_b_gate, b_dw=d_b_dw, g_conv_ln=d_g_ln, b_conv_ln=d_b_ln,
                  b_conv_out=d_b_co, g_q_lora=d_g_q, g_kv_lora=d_g_kv, g_ffn=d_g_ffn, g_final=d_g_final,
                  w_dw=d_w_dw[:CONV_WIDTH], meta_tokens=d_h0[:N_META])
    sizes = [int(np.prod(sgrads[nm].shape)) for nm in small]
    packed = jnp.concatenate([sgrads[nm].reshape(-1) for nm in small]).reshape(-1, 128)
    sm_arrs, sm_ss, sm_rs, tok_sm = copies_start([packed, lax.empty((N_DEV,) + packed.shape, F32)], devices_plan,
                                                 N_DEV - 1, packed, name="gather_small_start")

    grads, delta, new_m, new_v, delta_2d = {}, {}, {}, {}, {}

    def adamw_2d(nm, w2, g2, m2, v2, back):
        g_, d_, m_, v_ = adamw(w2, g2, m2, v2, name="adamw_" + nm)
        delta_2d[nm] = d_
        grads[nm], delta[nm], new_m[nm], new_v[nm] = back(g_), back(d_), back(m_), back(v_)

    reduced = dict(zip(grp_c, copies_wait(sh_c, share_plan(ax_c), ss_sc, rs_sc, tok_sm, name="rs_c_share_wait")))
    reduced.update(rs_shared(grp_b, sh_b, tok_sm, "b"))
    for nm in grp_c + grp_b + grp_a:
        if nm == grp_a[0]:
            reduced.update(rs_shared(grp_a, rs_end(st_a, [delta_2d[k] for k in grp_c + grp_b], "a"), half_arr, "a"))
        shp = weights[nm].shape
        if axes[nm] == 1:
            adamw_2d(nm, shard[nm], reduced[nm], jnp.swapaxes(m_in[nm][0], 0, 1), jnp.swapaxes(v_in[nm][0], 0, 1),
                     lambda a: jnp.swapaxes(a, 0, 1)[None])
        else:
            two = (shp[-2], shp[-1])
            adamw_2d(nm, weights[nm].reshape(two), reduced[nm], m_in[nm].reshape(two), v_in[nm].reshape(two),
                     lambda a, shp=shp: a.reshape(shp))

    sm_arrs = copies_wait(sm_arrs, devices_plan, sm_ss, sm_rs, delta_2d[grp_a[0]], name="gather_small_wait")
    me = (4 * xi + 2 * yi + ci).astype(jnp.int32)
    parts = lax.dynamic_update_slice(sm_arrs[1], sm_arrs[0][None], (me, 0, 0))
    summed = sum_devices(parts, name="sum_small_grads").reshape(-1)
    offs = np.concatenate([[0], np.cumsum(sizes)])
    sfull = {nm: summed[int(offs[i]):int(offs[i + 1])].reshape(sgrads[nm].shape) for i, nm in enumerate(small)}
    for nm in small:
        gfull = sfull[nm]
        if nm == "w_dw":
            cb = C // N_CHIPS
            grads[nm] = lax.dynamic_slice(gfull, (0, chip * cb), (CONV_WIDTH, cb))[None]
        elif nm == "meta_tokens":
            cb = D // N_CHIPS
            grads[nm] = lax.dynamic_slice(gfull, (0, chip * cb), (N_META, cb))
        else:
            grads[nm] = gfull.reshape(weights[nm].shape)

    rep = [nm for nm in small if nm not in ("w_dw", "meta_tokens")]

    def pack(d):
        return jnp.concatenate([d[nm].reshape(-1) for nm in rep]).reshape(-1, 128)

    _, pd, pm, pv = adamw(pack(weights), pack(grads), pack(m_in), pack(v_in), name="adamw_small")
    rsz = [int(np.prod(weights[nm].shape)) for nm in rep]
    roff = np.concatenate([[0], np.cumsum(rsz)])
    for i, nm in enumerate(rep):
        sl = slice(int(roff[i]), int(roff[i + 1]))
        delta[nm] = pd.reshape(-1)[sl].reshape(weights[nm].shape)
        new_m[nm] = pm.reshape(-1)[sl].reshape(weights[nm].shape)
        new_v[nm] = pv.reshape(-1)[sl].reshape(weights[nm].shape)
    for nm in ["w_dw", "meta_tokens"]:
        shp = weights[nm].shape
        two = (shp[-2], shp[-1])
        adamw_2d(nm, weights[nm].reshape(two), grads[nm].reshape(two), m_in[nm].reshape(two), v_in[nm].reshape(two),
                 lambda a, shp=shp: a.reshape(shp))

    return (loss, grad_x, *[grads[nm] for nm in names], *[delta[nm] for nm in names],
            *[new_m[nm] for nm in names], *[new_v[nm] for nm in names])
```

```python
import functools

import numpy as np
import jax
import jax.numpy as jnp
from jax import lax
from jax.experimental import pallas as pl
from jax.experimental.pallas import tpu as pltpu

F32 = jnp.float32
BF = jnp.bfloat16
MESH = pl.DeviceIdType.MESH

N_META = 16
CONV_WIDTH = 31
CONV_PAD = 32
QK_NOPE = 128
QK_ROPE = 64
V_HEAD = 128
HEAD_PAD = 256
ROPE_THETA = 10000.0
EPS = 1e-6
ADAM_LR = 0.001
ADAM_B1 = 0.9
ADAM_B2 = 0.999
ADAM_EPS = 1e-08
ADAM_WD = 0.01
ADAM_STEP = 10
VMEM_LIMIT = 56 * 1024 * 1024
N_CHIPS = 4
N_DEV = 8


def _cparams(sem):
    return pltpu.CompilerParams(dimension_semantics=sem, vmem_limit_bytes=VMEM_LIMIT)


def mm(a, b, *, mode, tm, tn, tk, out_dtype, name, res=None, out_blocks=None, deps=(), n_out=None):
    b3 = b.ndim == 3
    if mode == "nn":
        M, K = a.shape
        N = b.shape[0] * b.shape[2] if b3 else b.shape[1]
        a_spec = pl.BlockSpec((tm, tk), lambda i, j, k: (i, k))
        if b3:
            per = b.shape[2] // tn
            b_spec = pl.BlockSpec((None, tk, tn), lambda i, j, k: (j // per, k, j % per))
        else:
            b_spec = pl.BlockSpec((tk, tn), lambda i, j, k: (k, j))
        dims = (((1,), (0,)), ((), ()))
    elif mode == "nt":
        M, K = a.shape
        N = n_out if n_out is not None else (b.shape[1] if b3 else b.shape[0])
        a_spec = pl.BlockSpec((tm, tk), lambda i, j, k: (i, k))
        if b3:
            per = b.shape[2] // tk
            b_spec = pl.BlockSpec((None, tn, tk), lambda i, j, k: (k // per, j, k % per))
        else:
            b_spec = pl.BlockSpec((tn, tk), lambda i, j, k: (j, k))
        dims = (((1,), (1,)), ((), ()))
    else:
        K, M = a.shape
        N = b.shape[1]
        a_spec = pl.BlockSpec((tk, tm), lambda i, j, k: (k, i))
        b_spec = pl.BlockSpec((tk, tn), lambda i, j, k: (k, j))
        dims = (((0,), (0,)), ((), ()))
    gm, gn, gk = M // tm, N // tn, K // tk
    assert gm * tm == M and gn * tn == N and gk * tk == K, (name, a.shape, b.shape, tm, tn, tk)
    if out_blocks is None:
        o_spec = pl.BlockSpec((tm, tn), lambda i, j, k: (i, j))
        o_shape = (M, N)
    else:
        nbw = N // out_blocks
        per_o = nbw // tn
        assert per_o * tn == nbw
        o_spec = pl.BlockSpec((None, tm, tn), lambda i, j, k: (j // per_o, i, j % per_o))
        o_shape = (out_blocks, M, nbw)
    has_res = res is not None

    def body(*refs):
        a_ref, b_ref = refs[0], refs[1]
        r_ref = refs[2] if has_res else None
        o_ref = refs[2 + has_res + len(deps)]
        acc = refs[-1]
        p = lax.dot_general(a_ref[...], b_ref[...], dims, preferred_element_type=F32)

        def finish(v):
            if has_res:
                v = v + r_ref[...]
            o_ref[...] = v.astype(o_ref.dtype)

        if gk == 1:
            finish(p)
        else:
            k = pl.program_id(2)

            @pl.when(k == 0)
            def _():
                acc[...] = p

            @pl.when(k > 0)
            def _():
                acc[...] += p

            @pl.when(k == gk - 1)
            def _():
                finish(acc[...])

    in_specs = [a_spec, b_spec]
    args = [a, b]
    if has_res:
        in_specs.append(pl.BlockSpec((tm, tn), lambda i, j, k: (i, j)))
        args.append(res)
    in_specs += [pl.BlockSpec(memory_space=pl.ANY)] * len(deps)
    args += list(deps)
    return pl.pallas_call(
        body, name=name, grid=(gm, gn, gk), in_specs=in_specs, out_specs=o_spec,
        out_shape=jax.ShapeDtypeStruct(o_shape, out_dtype),
        scratch_shapes=[pltpu.VMEM((tm, tn), F32)] if gk > 1 else [],
        compiler_params=_cparams(("parallel", "parallel", "arbitrary")),
    )(*args)


def mm_tn_into(a, b, buf, row0, *, tm, tn, name, deps=()):
    K, M = a.shape
    N = b.shape[1]
    gm, gn = M // tm, N // tn
    assert gm * tm == M and gn * tn == N and buf.shape[1] == N and row0 + M <= buf.shape[0]
    steps = gm * gn

    def body(a_ref, b_ref, buf_ref, *rest):
        o_ref, tile, sem = rest[len(deps):]
        i, j = pl.program_id(0), pl.program_id(1)
        s = i * gn + j
        slot = s % 2

        def out_copy(sl):
            return pltpu.make_async_copy(tile.at[sl], o_ref.at[pl.ds(row0 + i * tm, tm), pl.ds(j * tn, tn)],
                                         sem.at[sl])

        @pl.when(s >= 2)
        def _():
            out_copy(slot).wait()

        tile[slot] = lax.dot_general(a_ref[...], b_ref[...], (((0,), (0,)), ((), ())),
                                     preferred_element_type=F32).astype(tile.dtype)
        out_copy(slot).start()

        @pl.when(s == steps - 1)
        def _():
            out_copy(slot).wait()
            if steps >= 2:
                out_copy(1 - slot).wait()

    anyspec = pl.BlockSpec(memory_space=pl.ANY)
    return pl.pallas_call(
        body, name=name, grid=(gm, gn),
        in_specs=[pl.BlockSpec((K, tm), lambda i, j: (0, i)), pl.BlockSpec((K, tn), lambda i, j: (0, j)), anyspec]
        + [anyspec] * len(deps),
        out_specs=anyspec, out_shape=jax.ShapeDtypeStruct(buf.shape, buf.dtype),
        input_output_aliases={2: 0},
        scratch_shapes=[pltpu.VMEM((2, tm, tn), buf.dtype), pltpu.SemaphoreType.DMA((2,))],
        compiler_params=_cparams(("arbitrary", "arbitrary")),
    )(a, b, buf, *deps)


def rowwise(body, row_ins, full_ins, row_outs, acc_outs, *, tr, name, deps=()):
    T = row_ins[0].shape[0]
    assert T % tr == 0, (name, T, tr)
    n_ri, n_fi, n_ro = len(row_ins), len(full_ins), len(row_outs)
    n_in = n_ri + n_fi + len(deps)

    def kern(*refs):
        body(pl.program_id(0), refs[:n_ri], refs[n_ri:n_ri + n_fi], refs[n_in:n_in + n_ro], refs[n_in + n_ro:])

    in_specs = [pl.BlockSpec((tr, a.shape[1]), lambda i: (i, 0)) for a in row_ins]
    in_specs += [pl.BlockSpec(a.shape, lambda i: (0, 0)) for a in full_ins]
    in_specs += [pl.BlockSpec(memory_space=pl.ANY)] * len(deps)
    out_specs = [pl.BlockSpec((tr, c), lambda i: (i, 0)) for c, _ in row_outs]
    out_specs += [pl.BlockSpec(s, lambda i: (0, 0)) for s in acc_outs]
    out_shape = [jax.ShapeDtypeStruct((T, c), d) for c, d in row_outs]
    out_shape += [jax.ShapeDtypeStruct(s, F32) for s in acc_outs]
    return pl.pallas_call(
        kern, name=name, grid=(T // tr,), in_specs=in_specs, out_specs=out_specs, out_shape=out_shape,
        compiler_params=_cparams(("arbitrary",)),
    )(*row_ins, *full_ins, *deps)


def _acc(step, ref, val):
    @pl.when(step == 0)
    def _():
        ref[...] = val

    @pl.when(step > 0)
    def _():
        ref[...] += val


def _colsum(x):
    return jnp.sum(x, axis=0, keepdims=True)


def _sigmoid(x):
    return 1.0 / (1.0 + jnp.exp(-x))


def _rot_half(v):
    lane = lax.broadcasted_iota(jnp.int32, v.shape, 1)
    lo = -pltpu.roll(v, 96, 1)
    hi = pltpu.roll(v, 32, 1)
    return jnp.where(lane < 32, lo, jnp.where(lane < 64, hi, 0.0))


def rms_fwd(h, g, *, tr, name, deps=()):
    def body(step, ri, fi, ro, ao):
        x = ri[0][...]
        r = lax.rsqrt(jnp.mean(x * x, axis=-1, keepdims=True) + EPS)
        ro[0][...] = ((x * r) * fi[0][...]).astype(BF)

    return rowwise(body, [h], [g], [(h.shape[1], BF)], [], tr=tr, name=name, deps=deps)[0]


def rms_bwd(dy, h, g, dres, *, tr, name, deps=()):
    D = h.shape[1]

    def body(step, ri, fi, ro, ao):
        d, x, dr = ri[0][...].astype(F32), ri[1][...], ri[2][...]
        r = lax.rsqrt(jnp.mean(x * x, axis=-1, keepdims=True) + EPS)
        n = x * r
        _acc(step, ao[0], _colsum(d * n))
        dn = d * fi[0][...]
        dh = r * (dn - n * jnp.mean(dn * n, axis=-1, keepdims=True)) + dr
        ro[0][...] = dh
        ro[1][...] = dh.astype(BF)

    return rowwise(body, [dy, h, dres], [g], [(D, F32), (D, BF)], [(1, D)], tr=tr, name=name, deps=deps)


def glu_fwd(z, b, *, tr, name, deps=()):
    C = z.shape[1] // 2

    def body(step, ri, fi, ro, ao):
        zz = ri[0][...].astype(F32) + fi[0][...]
        ro[0][...] = zz[:, :C] * _sigmoid(zz[:, C:])

    return rowwise(body, [z], [b], [(C, F32)], [], tr=tr, name=name, deps=deps)[0]


def glu_bwd(z, b, dc, *, tr, name):
    C = z.shape[1] // 2

    def body(step, ri, fi, ro, ao):
        zz = ri[0][...].astype(F32) + fi[0][...]
        d = ri[1][...]
        za, sg = zz[:, :C], _sigmoid(zz[:, C:])
        dza = d * sg
        dzb = d * za * sg * (1.0 - sg)
        ro[0][:, :C] = dza.astype(BF)
        ro[0][:, C:] = dzb.astype(BF)
        _acc(step, ao[0], _colsum(dza))
        _acc(step, ao[1], _colsum(dzb))

    dz, da, db = rowwise(body, [z, dc], [b], [(2 * C, BF)], [(1, C), (1, C)], tr=tr, name=name)
    return dz, jnp.concatenate([da, db], axis=1)


def ln_silu_fwd(c1, g, b, *, tr, name):
    def body(step, ri, fi, ro, ao):
        x = ri[0][...]
        mu = jnp.mean(x, axis=-1, keepdims=True)
        xc = x - mu
        rstd = lax.rsqrt(jnp.mean(xc * xc, axis=-1, keepdims=True) + EPS)
        y = (xc * rstd) * fi[0][...] + fi[1][...]
        ro[0][...] = (y * _sigmoid(y)).astype(BF)

    return rowwise(body, [c1], [g, b], [(c1.shape[1], BF)], [], tr=tr, name=name)[0]


def ln_silu_bwd(c1, dc3, g, b, *, tr, name):
    C = c1.shape[1]

    def body(step, ri, fi, ro, ao):
        x, d3 = ri[0][...], ri[1][...].astype(F32)
        mu = jnp.mean(x, axis=-1, keepdims=True)
        xc = x - mu
        rstd = lax.rsqrt(jnp.mean(xc * xc, axis=-1, keepdims=True) + EPS)
        n = xc * rstd
        y = n * fi[0][...] + fi[1][...]
        sg = _sigmoid(y)
        dy = d3 * (sg * (1.0 + y * (1.0 - sg)))
        _acc(step, ao[0], _colsum(dy * n))
        _acc(step, ao[1], _colsum(dy))
        dn = dy * fi[0][...]
        dx = rstd * (dn - jnp.mean(dn, axis=-1, keepdims=True) - n * jnp.mean(dn * n, axis=-1, keepdims=True))
        ro[0][...] = dx
        _acc(step, ao[2], _colsum(dx))

    return rowwise(body, [c1, dc3], [g, b], [(C, F32)], [(1, C), (1, C), (1, C)], tr=tr, name=name)


def lora_norm_fwd(zs, gq, gkv, cos, sin, *, tr, name, deps=()):
    R = gq.shape[1]

    def body(step, ri, fi, ro, ao):
        z = ri[0][...]
        for o, gi in ((0, 0), (1, 1)):
            x = z[:, o * R:(o + 1) * R]
            r = lax.rsqrt(jnp.mean(x * x, axis=-1, keepdims=True) + EPS)
            ro[o][...] = ((x * r) * fi[gi][...]).astype(BF)
        kr = z[:, 2 * R:2 * R + 128]
        ro[2][...] = kr * ri[1][...] + _rot_half(kr) * ri[2][...]

    return rowwise(body, [zs, cos, sin], [gq, gkv], [(R, BF), (R, BF), (128, F32)], [], tr=tr, name=name, deps=deps)


def lora_norm_bwd(zs, dcq, dckv, dkr, cos, sin, gq, gkv, *, tr, name):
    R = gq.shape[1]

    def body(step, ri, fi, ro, ao):
        z = ri[0][...]
        for o in (0, 1):
            x = z[:, o * R:(o + 1) * R]
            d = ri[1 + o][...]
            r = lax.rsqrt(jnp.mean(x * x, axis=-1, keepdims=True) + EPS)
            n = x * r
            _acc(step, ao[o], _colsum(d * n))
            dn = d * fi[o][...]
            ro[0][:, o * R:(o + 1) * R] = (r * (dn - n * jnp.mean(dn * n, axis=-1, keepdims=True))).astype(BF)
        dk = ri[3][...]
        ro[0][:, 2 * R:2 * R + 128] = (dk * ri[4][...] - _rot_half(dk * ri[5][...])).astype(BF)

    return rowwise(body, [zs, dcq, dckv, dkr, cos, sin], [gq, gkv], [(2 * R + 128, BF)], [(1, R), (1, R)],
                   tr=tr, name=name)


def mix_fwd(zg, yc, ya, bg, bco, *, tr, name):
    D = yc.shape[1]

    def body(step, ri, fi, ro, ao):
        g = _sigmoid(ri[0][...].astype(F32) + fi[0][...])
        ro[0][...] = (g[:, :D] * (ri[1][...].astype(F32) + fi[1][...]) + g[:, D:] * ri[2][...].astype(F32)).astype(BF)

    return rowwise(body, [zg, yc, ya], [bg, bco], [(D, BF)], [], tr=tr, name=name)[0]


def mix_bwd(zg, yc, ya, dmix, bg, bco, *, tr, name):
    D = yc.shape[1]

    def body(step, ri, fi, ro, ao):
        g = _sigmoid(ri[0][...].astype(F32) + fi[0][...])
        gc, ga = g[:, :D], g[:, D:]
        ycv = ri[1][...].astype(F32) + fi[1][...]
        yav = ri[2][...].astype(F32)
        dm = ri[3][...].astype(F32)
        dyc = dm * gc
        ro[0][...] = dyc.astype(BF)
        ro[1][...] = (dm * ga).astype(BF)
        dzc = dm * ycv * gc * (1.0 - gc)
        dza = dm * yav * ga * (1.0 - ga)
        ro[2][:, :D] = dzc.astype(BF)
        ro[2][:, D:] = dza.astype(BF)
        _acc(step, ao[0], _colsum(dzc))
        _acc(step, ao[1], _colsum(dza))
        _acc(step, ao[2], _colsum(dyc))

    dyc, dya, dzg, dbc, dba, dbo = rowwise(body, [zg, yc, ya, dmix], [bg, bco], [(D, BF), (D, BF), (2 * D, BF)],
                                           [(1, D), (1, D), (1, D)], tr=tr, name=name)
    return dyc, dya, dzg, jnp.concatenate([dbc, dba], axis=1), dbo


def ffn_fwd(hn, wg, wu, *, tm, name):
    T, D = hn.shape
    nb, _, FB = wg.shape
    assert T % tm == 0

    def body(h_ref, g_ref, u_ref, a_ref, b_ref, f_ref):
        h = h_ref[...]
        a = jnp.dot(h, g_ref[...], preferred_element_type=F32)
        b = jnp.dot(h, u_ref[...], preferred_element_type=F32)
        a_ref[...] = a
        b_ref[...] = b
        f_ref[...] = (a * _sigmoid(a) * b).astype(BF)

    wspec = pl.BlockSpec((None, D, FB), lambda j, i: (j, 0, 0))
    ospec = pl.BlockSpec((tm, FB), lambda j, i: (i, j))
    return pl.pallas_call(
        body, name=name, grid=(nb, T // tm),
        in_specs=[pl.BlockSpec((tm, D), lambda j, i: (i, 0)), wspec, wspec], out_specs=[ospec, ospec, ospec],
        out_shape=[jax.ShapeDtypeStruct((T, nb * FB), F32), jax.ShapeDtypeStruct((T, nb * FB), F32),
                   jax.ShapeDtypeStruct((T, nb * FB), BF)],
        compiler_params=_cparams(("parallel", "parallel")),
    )(hn, wg, wu)


def ffn_dact(dh, wd, a, b, *, tm, name):
    T, D = dh.shape
    FF = wd.shape[0]
    FB = FF // N_CHIPS

    def body(dh_ref, w_ref, a_ref, b_ref, da_ref, db_ref):
        d = lax.dot_general(dh_ref[...], w_ref[...], (((1,), (1,)), ((), ())), preferred_element_type=F32)
        x, u = a_ref[...], b_ref[...]
        sg = _sigmoid(x)
        da_ref[...] = (d * u * (sg * (1.0 + x * (1.0 - sg)))).astype(BF)
        db_ref[...] = (d * (x * sg)).astype(BF)

    blk = pl.BlockSpec((tm, FB), lambda j, i: (i, j))
    return pl.pallas_call(
        body, name=name, grid=(N_CHIPS, T // tm),
        in_specs=[pl.BlockSpec((tm, D), lambda j, i: (i, 0)), pl.BlockSpec((FB, D), lambda j, i: (j, 0)), blk, blk],
        out_specs=[blk, blk],
        out_shape=[jax.ShapeDtypeStruct((T, FF), BF), jax.ShapeDtypeStruct((T, FF), BF)],
        compiler_params=_cparams(("parallel", "parallel")),
    )(dh, wd, a, b)


def ffn_dhn(da, db, wg, wu, *, tm, tn, name, deps=()):
    T, FF = da.shape
    nb, D, FB = wg.shape
    dims = (((1,), (1,)), ((), ()))

    def body(da_ref, db_ref, g_ref, u_ref, *rest):
        o_ref = rest[len(deps)]
        acc = None
        for k in range(nb):
            cols = slice(k * FB, (k + 1) * FB)
            p = lax.dot_general(da_ref[:, cols], g_ref[k], dims, preferred_element_type=F32)
            p = p + lax.dot_general(db_ref[:, cols], u_ref[k], dims, preferred_element_type=F32)
            acc = p if acc is None else acc + p
        o_ref[...] = acc.astype(BF)

    aspec = pl.BlockSpec((tm, FF), lambda i, j: (i, 0))
    wspec = pl.BlockSpec((nb, tn, FB), lambda i, j: (0, j, 0))
    return pl.pallas_call(
        body, name=name, grid=(T // tm, D // tn),
        in_specs=[aspec, aspec, wspec, wspec] + [pl.BlockSpec(memory_space=pl.ANY)] * len(deps),
        out_specs=pl.BlockSpec((tm, tn), lambda i, j: (i, j)),
        out_shape=jax.ShapeDtypeStruct((T, D), BF),
        compiler_params=_cparams(("parallel", "parallel")),
    )(da, db, wg, wu, *deps)


def final_loss(h2, tgt, g, *, seq, tr, name):
    D = h2.shape[1]

    def body(step, ri, fi, ro, ao):
        x = ri[0][...]
        r = lax.rsqrt(jnp.mean(x * x, axis=-1, keepdims=True) + EPS)
        n = x * r
        gg = fi[0][...]
        row = lax.broadcasted_iota(jnp.int32, (tr, 1), 0) + step * tr
        live = jnp.logical_and(row >= N_META, row < N_META + seq)
        e = jnp.where(live, n * gg - ri[1][...], 0.0)
        _acc(step, ao[0], jnp.broadcast_to(0.5 * jnp.sum(jnp.mean(e * e, axis=-1, keepdims=True)), (1, 128)))
        dy = e * (1.0 / D)
        _acc(step, ao[1], _colsum(dy * n))
        dn = dy * gg
        dh = r * (dn - n * jnp.mean(dn * n, axis=-1, keepdims=True))
        ro[0][...] = dh
        ro[1][...] = dh.astype(BF)

    return rowwise(body, [h2, tgt], [g], [(D, F32), (D, BF)], [(1, 128), (1, D)], tr=tr, name=name)


def _shifted_copies(s_ref, T, rc):
    for r in range(1, 8):
        for t0 in range(0, T, rc):
            s_ref[r, t0:t0 + rc, :] = s_ref[0, t0 + r:t0 + r + rc, :]
        s_ref[r, T:T + CONV_PAD - 8, :] = s_ref[0, T + r:T + r + CONV_PAD - 8, :]


def _shifted(s_ref, start, rows):
    return s_ref[start % 8, start - start % 8:start - start % 8 + rows, :]


def dwconv_fwd(x, w, b, *, tc, rc, name, deps=()):
    T, C = x.shape
    nchunk = T // rc
    assert nchunk * rc == T and C % tc == 0

    def body(x_ref, w_ref, b_ref, *rest):
        y_ref, xs_ref = rest[len(deps):]
        xs_ref[0, 0:CONV_PAD, :] = jnp.zeros((CONV_PAD, tc), F32)
        xs_ref[0, CONV_PAD:CONV_PAD + T, :] = x_ref[...]
        _shifted_copies(xs_ref, T, rc)
        off = CONV_PAD - (CONV_WIDTH - 1)
        for ci in range(nchunk):
            t0 = ci * rc
            acc = jnp.broadcast_to(b_ref[...], (rc, tc))
            for j in range(CONV_WIDTH):
                acc = acc + w_ref[j:j + 1, :] * _shifted(xs_ref, t0 + off + j, rc)
            y_ref[t0:t0 + rc, :] = acc

    return pl.pallas_call(
        body, name=name, grid=(C // tc,),
        in_specs=[pl.BlockSpec((T, tc), lambda j: (0, j)), pl.BlockSpec((32, tc), lambda j: (0, j)),
                  pl.BlockSpec((1, tc), lambda j: (0, j))] + [pl.BlockSpec(memory_space=pl.ANY)] * len(deps),
        out_specs=pl.BlockSpec((T, tc), lambda j: (0, j)),
        out_shape=jax.ShapeDtypeStruct((T, C), F32),
        scratch_shapes=[pltpu.VMEM((8, T + CONV_PAD, tc), F32)],
        compiler_params=_cparams(("parallel",)),
    )(x, w, b, *deps)


def dwconv_bwd(x, dy, w, *, tc, rc, name):
    T, C = x.shape
    nchunk = T // rc
    assert nchunk * rc == T and C % tc == 0 and rc % 8 == 0

    def body(x_ref, dy_ref, w_ref, dx_ref, dw_ref, xs_ref, ds_ref):
        off = CONV_PAD - (CONV_WIDTH - 1)
        xs_ref[0, 0:CONV_PAD, :] = jnp.zeros((CONV_PAD, tc), F32)
        xs_ref[0, CONV_PAD:CONV_PAD + T, :] = x_ref[...]
        _shifted_copies(xs_ref, T, rc)
        ds_ref[0, 0:T, :] = dy_ref[...]
        ds_ref[0, T:T + CONV_PAD, :] = jnp.zeros((CONV_PAD, tc), F32)
        _shifted_copies(ds_ref, T, rc)
        for ci in range(nchunk):
            t0 = ci * rc
            acc = jnp.zeros((rc, tc), F32)
            for j in range(CONV_WIDTH):
                acc = acc + w_ref[j:j + 1, :] * _shifted(ds_ref, t0 + (CONV_WIDTH - 1) - j, rc)
            dx_ref[t0:t0 + rc, :] = acc
        rd = rc // 2 if rc % 16 == 0 else rc
        accs = [jnp.zeros((8, tc), F32) for _ in range(CONV_WIDTH)]
        for t0 in range(0, T, rd):
            dyc = dy_ref[t0:t0 + rd, :]
            for j in range(CONV_WIDTH):
                pr = dyc * _shifted(xs_ref, t0 + off + j, rd)
                accs[j] = accs[j] + jnp.sum(pr.reshape(rd // 8, 8, tc), axis=0)
        dw_ref[...] = jnp.zeros((32, tc), F32)
        for j in range(CONV_WIDTH):
            dw_ref[j:j + 1, :] = jnp.sum(accs[j], axis=0, keepdims=True)

    return pl.pallas_call(
        body, name=name, grid=(C // tc,),
        in_specs=[pl.BlockSpec((T, tc), lambda j: (0, j)), pl.BlockSpec((T, tc), lambda j: (0, j)),
                  pl.BlockSpec((32, tc), lambda j: (0, j))],
        out_specs=[pl.BlockSpec((T, tc), lambda j: (0, j)), pl.BlockSpec((32, tc), lambda j: (0, j))],
        out_shape=[jax.ShapeDtypeStruct((T, C), F32), jax.ShapeDtypeStruct((32, C), F32)],
        scratch_shapes=[pltpu.VMEM((8, T + CONV_PAD, tc), F32), pltpu.VMEM((8, T + CONV_PAD, tc), F32)],
        compiler_params=_cparams(("parallel",)),
    )(x, dy, w)


def _heads_per_step(H):
    return 4 if H % 4 == 0 else 1


def q_proj(cq, wq, cos, sin, *, scale, name):
    T, R = cq.shape
    H = wq.shape[1] // HEAD_PAD
    hb = _heads_per_step(H)

    def body(a_ref, b_ref, c_ref, s_ref, o_ref):
        p = jnp.dot(a_ref[...], b_ref[...], preferred_element_type=F32)
        for h in range(hb):
            o = h * HEAD_PAD
            o_ref[:, o:o + QK_NOPE] = (p[:, o:o + QK_NOPE] * scale).astype(BF)
            x = p[:, o + QK_NOPE:o + HEAD_PAD]
            o_ref[:, o + QK_NOPE:o + HEAD_PAD] = ((x * c_ref[...] + _rot_half(x) * s_ref[...]) * scale).astype(BF)

    return pl.pallas_call(
        body, name=name, grid=(H // hb,),
        in_specs=[pl.BlockSpec((T, R), lambda h: (0, 0)), pl.BlockSpec((R, hb * HEAD_PAD), lambda h: (0, h)),
                  pl.BlockSpec((T, 128), lambda h: (0, 0)), pl.BlockSpec((T, 128), lambda h: (0, 0))],
        out_specs=pl.BlockSpec((T, hb * HEAD_PAD), lambda h: (0, h)),
        out_shape=jax.ShapeDtypeStruct((T, H * HEAD_PAD), BF),
        compiler_params=_cparams(("parallel",)),
    )(cq, wq, cos, sin)


def k_proj(ckv, wk, kr, *, name):
    T, R = ckv.shape
    H = wk.shape[1] // QK_NOPE
    hb = _heads_per_step(H)

    def body(a_ref, b_ref, kr_ref, o_ref):
        p = jnp.dot(a_ref[...], b_ref[...], preferred_element_type=F32)
        krb = kr_ref[...].astype(BF)
        for h in range(hb):
            o_ref[:, h * HEAD_PAD:h * HEAD_PAD + QK_NOPE] = p[:, h * QK_NOPE:(h + 1) * QK_NOPE].astype(BF)
            o_ref[:, h * HEAD_PAD + QK_NOPE:(h + 1) * HEAD_PAD] = krb

    return pl.pallas_call(
        body, name=name, grid=(H // hb,),
        in_specs=[pl.BlockSpec((T, R), lambda h: (0, 0)), pl.BlockSpec((R, hb * QK_NOPE), lambda h: (0, h)),
                  pl.BlockSpec((T, 128), lambda h: (0, 0))],
        out_specs=pl.BlockSpec((T, hb * HEAD_PAD), lambda h: (0, h)),
        out_shape=jax.ShapeDtypeStruct((T, H * HEAD_PAD), BF),
        compiler_params=_cparams(("parallel",)),
    )(ckv, wk, kr)


def _causal_scores(q, k_ref, qi, CH, split):
    nt = (((1,), (1,)), ((), ()))
    pieces = []
    if not split:
        rows = slice(0, (qi + 1) * CH)
        s = lax.dot_general(q, k_ref[rows, :], nt, preferred_element_type=F32)
        keep = lax.broadcasted_iota(jnp.int32, s.shape, 1) <= lax.broadcasted_iota(jnp.int32, s.shape, 0) + qi * CH
        return [(rows, jnp.where(keep, s, -1e30))]
    if qi > 0:
        rows = slice(0, qi * CH)
        pieces.append((rows, lax.dot_general(q, k_ref[rows, :], nt, preferred_element_type=F32)))
    rows = slice(qi * CH, (qi + 1) * CH)
    s = lax.dot_general(q, k_ref[rows, :], nt, preferred_element_type=F32)
    keep = lax.broadcasted_iota(jnp.int32, s.shape, 1) <= lax.broadcasted_iota(jnp.int32, s.shape, 0)
    pieces.append((rows, jnp.where(keep, s, -1e30)))
    return pieces


def attn_fwd(q, k, v, *, nch, name):
    T = q.shape[0]
    H = q.shape[1] // HEAD_PAD
    CH = T // nch
    assert CH * nch == T and CH % 16 == 0

    def body(q_ref, k_ref, v_ref, o_ref, lse_ref):
        for qi in range(nch):
            L = (qi + 1) * CH
            pieces = _causal_scores(q_ref[qi * CH:L, :], k_ref, qi, CH, True)
            m = functools.reduce(jnp.maximum, [jnp.max(s, axis=1, keepdims=True) for _, s in pieces])
            l, o = 0.0, 0.0
            for rows, s in pieces:
                p = jnp.exp(s - m)
                l = l + jnp.sum(p, axis=1, keepdims=True)
                o = o + jnp.dot(p.astype(BF), v_ref[rows, :], preferred_element_type=F32)
            o_ref[qi * CH:L, :] = (o / l).astype(BF)
            lse_ref[qi * CH:L, :] = jnp.broadcast_to(m + jnp.log(l), (CH, V_HEAD))

    return pl.pallas_call(
        body, name=name, grid=(H,),
        in_specs=[pl.BlockSpec((T, HEAD_PAD), lambda h: (0, h)), pl.BlockSpec((T, HEAD_PAD), lambda h: (0, h)),
                  pl.BlockSpec((T, V_HEAD), lambda h: (0, h))],
        out_specs=[pl.BlockSpec((T, V_HEAD), lambda h: (0, h)), pl.BlockSpec((T, V_HEAD), lambda h: (0, h))],
        out_shape=[jax.ShapeDtypeStruct((T, H * V_HEAD), BF), jax.ShapeDtypeStruct((T, H * V_HEAD), F32)],
        compiler_params=_cparams(("parallel",)),
    )(q, k, v)


def attn_bwd(q, k, v, o, do, lse, cos, sin, *, scale, nch, name):
    T = q.shape[0]
    H = q.shape[1] // HEAD_PAD
    CH = T // nch
    tn_dims = (((0,), (0,)), ((), ()))
    nt_dims = (((1,), (1,)), ((), ()))

    def body(q_ref, k_ref, v_ref, o_ref, do_ref, lse_ref, c_ref, s_ref,
             dq_ref, dk_ref, dv_ref, dkr_ref, dk_acc, dv_acc):
        h = pl.program_id(0)
        dk_acc[...] = jnp.zeros_like(dk_acc)
        dv_acc[...] = jnp.zeros_like(dv_acc)
        for qi in range(nch):
            L = (qi + 1) * CH
            rows = slice(qi * CH, L)
            qc, doc = q_ref[rows, :], do_ref[rows, :]
            lse_c = lse_ref[rows, 0:1]
            dsum = jnp.sum(doc.astype(F32) * o_ref[rows, :].astype(F32), axis=1, keepdims=True)
            dq = 0.0
            for krows, s in _causal_scores(qc, k_ref, qi, CH, False):
                p = jnp.exp(s - lse_c)
                dp = lax.dot_general(doc, v_ref[krows, :], nt_dims, preferred_element_type=F32)
                ds = (p * (dp - dsum)).astype(BF)
                dq = dq + jnp.dot(ds, k_ref[krows, :], preferred_element_type=F32)
                dk_acc[krows, :] += lax.dot_general(ds, qc, tn_dims, preferred_element_type=F32)
                dv_acc[krows, :] += lax.dot_general(p.astype(BF), doc, tn_dims, preferred_element_type=F32)
            dq_ref[rows, :QK_NOPE] = (dq[:, :QK_NOPE] * scale).astype(BF)
            dyr = dq[:, QK_NOPE:]
            dq_ref[rows, QK_NOPE:] = ((dyr * c_ref[rows, :] - _rot_half(dyr * s_ref[rows, :])) * scale).astype(BF)
        dk_ref[:, :QK_NOPE] = dk_acc[:, :QK_NOPE].astype(BF)
        dk_ref[:, QK_NOPE:] = jnp.zeros((T, HEAD_PAD - QK_NOPE), BF)
        dv_ref[...] = dv_acc[...].astype(BF)

        @pl.when(h == 0)
        def _():
            dkr_ref[...] = dk_acc[:, QK_NOPE:]

        @pl.when(h > 0)
        def _():
            dkr_ref[...] += dk_acc[:, QK_NOPE:]

    hp = pl.BlockSpec((T, HEAD_PAD), lambda h: (0, h))
    hv = pl.BlockSpec((T, V_HEAD), lambda h: (0, h))
    tab = pl.BlockSpec((T, 128), lambda h: (0, 0))
    return pl.pallas_call(
        body, name=name, grid=(H,),
        in_specs=[hp, hp, hv, hv, hv, hv, tab, tab],
        out_specs=[hp, hp, hv, tab],
        out_shape=[jax.ShapeDtypeStruct((T, H * HEAD_PAD), BF), jax.ShapeDtypeStruct((T, H * HEAD_PAD), BF),
                   jax.ShapeDtypeStruct((T, H * V_HEAD), BF), jax.ShapeDtypeStruct((T, 128), F32)],
        scratch_shapes=[pltpu.VMEM((T, HEAD_PAD), F32), pltpu.VMEM((T, V_HEAD), F32)],
        compiler_params=_cparams(("arbitrary",)),
    )(q, k, v, o, do, lse, cos, sin)


def _place():
    x, y, c = lax.axis_index("x"), lax.axis_index("y"), lax.axis_index("c")
    return x, y, c


def _other_chips(x, y):
    return [(1 - x, y), (x, 1 - y), (1 - x, 1 - y)]


HBM_SPEC = pl.BlockSpec(memory_space=pltpu.HBM)
SEM_SPEC = pl.BlockSpec(memory_space=pltpu.SEMAPHORE)
DATAFLOW = pltpu.SideEffectType.DATAFLOW_SIDE_EFFECTING


def _hbm(a):
    return pltpu.with_memory_space_constraint(a, pltpu.HBM)


def _half(ref, lead, c, axis):
    rows, cols = ref.shape[-2], ref.shape[-1]
    if axis == 0:
        return ref.at[(*lead, pl.ds(c * (rows // 2), rows // 2))]
    return ref.at[(*lead, slice(None), pl.ds(c * (cols // 2), cols // 2))]


def copies_start(arrs, plan, n_sems, after, *, name):
    n = len(arrs)

    def body(*refs):
        outs = refs[n + 1:2 * n + 1]
        send_sems, recv_sems, token = refs[2 * n + 1:]
        for cp in plan(outs, send_sems, recv_sems, False):
            cp.start()
        token[...] = jnp.zeros_like(token)

    res = pl.pallas_call(
        body, name=name,
        out_shape=tuple(pltpu.HBM(a.shape, a.dtype) for a in arrs)
        + (pltpu.SemaphoreType.DMA((n_sems,)), pltpu.SemaphoreType.DMA((n_sems,)), jax.ShapeDtypeStruct((8, 128), F32)),
        in_specs=(HBM_SPEC,) * n + (pl.BlockSpec(memory_space=pl.ANY),),
        out_specs=(HBM_SPEC,) * n + (SEM_SPEC, SEM_SPEC, pl.BlockSpec(memory_space=pltpu.VMEM)),
        input_output_aliases={i: i for i in range(n)},
        compiler_params=pltpu.CompilerParams(has_side_effects=DATAFLOW),
    )(*[_hbm(a) for a in arrs], after)
    return list(res[:n]), res[n], res[n + 1], res[n + 2]


def copies_wait(arrs, plan, send_sems, recv_sems, after, *, name):
    n = len(arrs)
    afters = list(after) if isinstance(after, (list, tuple)) else [after]

    def body(*refs):
        for cp in plan(refs[:n], refs[n], refs[n + 1], True):
            cp.wait_send()
            cp.wait_recv()

    res = pl.pallas_call(
        body, name=name,
        out_shape=tuple(pltpu.HBM(a.shape, a.dtype) for a in arrs),
        in_specs=(HBM_SPEC,) * n + (SEM_SPEC, SEM_SPEC) + (pl.BlockSpec(memory_space=pl.ANY),) * len(afters),
        out_specs=(HBM_SPEC,) * n,
        input_output_aliases={i: i for i in range(n)},
        compiler_params=pltpu.CompilerParams(has_side_effects=DATAFLOW),
    )(*arrs, send_sems, recv_sems, *afters)
    return list(res)


def _half_part(ref, lead, c, axis, part):
    rows, cols = ref.shape[-2], ref.shape[-1]
    span = rows // 2 if axis == 0 else rows
    p1 = -(-(span // 2) // 16) * 16
    lo, n = (0, p1) if part == 0 else (p1, span - p1)
    if axis == 0:
        return ref.at[(*lead, pl.ds(c * span + lo, n))]
    return ref.at[(*lead, pl.ds(lo, n), pl.ds(c * (cols // 2), cols // 2))]


def near_plan(axes):
    def plan(bufs, send_sems, recv_sems, receiving):
        x, y, c = _place()
        me = 2 * x + y
        out = []
        for i, b in enumerate(bufs):
            for j, (cx, cy) in enumerate(_other_chips(x, y)[:2]):
                src = _half(b, (me,), c, axes[i])
                dst = _half(b, (2 * cx + cy,), c, axes[i]) if receiving else src
                out.append(pltpu.make_async_remote_copy(
                    src_ref=src, dst_ref=dst, send_sem=send_sems.at[2 * i + j], recv_sem=recv_sems.at[2 * i + j],
                    device_id=(cx, cy, c), device_id_type=MESH))
        return out
    return plan


def relay_plan(axes):
    def plan(bufs, send_sems, recv_sems, receiving):
        x, y, c = _place()
        (xn, yn, dg) = [2 * cx + cy for cx, cy in _other_chips(x, y)]
        to = [(x, 1 - y, c), (1 - x, y, c), (x, y, 1 - c), (x, y, 1 - c)]
        out = []
        for i, b in enumerate(bufs):
            if receiving:
                blks = [_half_part(b, (dg,), c, axes[i], 0), _half_part(b, (dg,), c, axes[i], 1),
                        _half(b, (xn,), 1 - c, axes[i]), _half(b, (yn,), 1 - c, axes[i])]
            else:
                blks = [_half_part(b, (xn,), c, axes[i], 0), _half_part(b, (yn,), c, axes[i], 1),
                        _half(b, (xn,), c, axes[i]), _half(b, (yn,), c, axes[i])]
            for k in range(4):
                out.append(pltpu.make_async_remote_copy(
                    src_ref=blks[k], dst_ref=blks[k], send_sem=send_sems.at[4 * i + k],
                    recv_sem=recv_sems.at[4 * i + k], device_id=to[k], device_id_type=MESH))
        return out
    return plan


def far_forward_plan(axes):
    def plan(bufs, send_sems, recv_sems, receiving):
        x, y, c = _place()
        dg = 2 * (1 - x) + (1 - y)
        out = []
        for i, b in enumerate(bufs):
            blk = _half(b, (dg,), (1 - c) if receiving else c, axes[i])
            out.append(pltpu.make_async_remote_copy(
                src_ref=blk, dst_ref=blk, send_sem=send_sems.at[i], recv_sem=recv_sems.at[i],
                device_id=(x, y, 1 - c), device_id_type=MESH))
        return out
    return plan


def share_plan(axes):
    def plan(bufs, send_sems, recv_sems, receiving):
        x, y, c = _place()
        out = []
        for i, b in enumerate(bufs):
            blk = _half(b, (), (1 - c) if receiving else c, axes[i])
            out.append(pltpu.make_async_remote_copy(
                src_ref=blk, dst_ref=blk, send_sem=send_sems.at[i], recv_sem=recv_sems.at[i],
                device_id=(x, y, 1 - c), device_id_type=MESH))
        return out
    return plan


def scatter_plan(n):
    def plan(arrs, send_sems, recv_sems, receiving):
        x, y, c = _place()
        out = []
        for i in range(n):
            for j, (cx, cy) in enumerate(_other_chips(x, y)):
                out.append(pltpu.make_async_remote_copy(
                    src_ref=arrs[i].at[2 * cx + cy], dst_ref=arrs[n + i].at[j], send_sem=send_sems.at[3 * i + j],
                    recv_sem=recv_sems.at[3 * i + j], device_id=(cx, cy, c), device_id_type=MESH))
        return out
    return plan


def exchange_plan(axes):
    n = len(axes)

    def plan(arrs, send_sems, recv_sems, receiving):
        x, y, c = _place()
        out = []
        for i in range(n):
            for k in range(N_CHIPS):
                out.append(pltpu.make_async_remote_copy(
                    src_ref=_half(arrs[i], (k,), 1 - c, axes[i]), dst_ref=arrs[n + i].at[k],
                    send_sem=send_sems.at[N_CHIPS * i + k], recv_sem=recv_sems.at[N_CHIPS * i + k],
                    device_id=(x, y, 1 - c), device_id_type=MESH))
        return out
    return plan


def devices_plan(arrs, send_sems, recv_sems, receiving):
    x, y, c = _place()
    out = []
    for k in range(1, N_DEV):
        px, py, pc = x ^ ((k >> 2) & 1), y ^ ((k >> 1) & 1), c ^ (k & 1)
        slot = (4 * px + 2 * py + pc) if receiving else (4 * x + 2 * y + c)
        out.append(pltpu.make_async_remote_copy(
            src_ref=arrs[0], dst_ref=arrs[1].at[slot], send_sem=send_sems.at[k - 1], recv_sem=recv_sems.at[k - 1],
            device_id=(px, py, pc), device_id_type=MESH))
    return out


def _row_tile(r, cols, itemsize, target_bytes=2 * 1024 * 1024):
    best = None
    for t in range(16, r + 1, 16):
        if r % t == 0 and t * cols * itemsize <= target_bytes:
            best = t
    return best if best is not None else r


def pair_sum(g, recv, half, axis, *, name):
    _, hr, hc = recv.shape
    tr = _row_tile(hr, hc, 4)
    nb = hr // tr
    if axis == 0:
        mine = pl.BlockSpec((None, tr, hc), lambda k, r, hf: (k, hf[0] * nb + r, 0))
    else:
        mine = pl.BlockSpec((None, tr, hc), lambda k, r, hf: (k, r, hf[0]))

    def body(half_ref, a_ref, b_ref, o_ref):
        o_ref[...] = (a_ref[...].astype(F32) + b_ref[...].astype(F32)).astype(BF)

    return pl.pallas_call(
        body, name=name,
        grid_spec=pltpu.PrefetchScalarGridSpec(
            num_scalar_prefetch=1, grid=(N_CHIPS, nb),
            in_specs=[mine, pl.BlockSpec((None, tr, hc), lambda k, r, hf: (k, r, 0))],
            out_specs=pl.BlockSpec((None, tr, hc), lambda k, r, hf: (k, r, 0))),
        out_shape=jax.ShapeDtypeStruct((N_CHIPS, hr, hc), BF),
        compiler_params=_cparams(("parallel", "parallel")),
    )(half, g, recv)


def chip_sum(s, recv, place, axis, *, name):
    _, hr, hc = s.shape
    tr = _row_tile(hr, hc, 4)
    nb = hr // tr
    if axis == 0:
        o_spec = pl.BlockSpec((tr, hc), lambda r, pc: (pc[1] * nb + r, 0))
        o_shape = (2 * hr, hc)
    else:
        o_spec = pl.BlockSpec((tr, hc), lambda r, pc: (r, pc[1]))
        o_shape = (hr, 2 * hc)

    def body(place_ref, a_ref, b_ref, o_ref):
        acc = a_ref[...].astype(F32)
        for j in range(3):
            acc = acc + b_ref[j].astype(F32)
        o_ref[...] = acc

    return pl.pallas_call(
        body, name=name,
        grid_spec=pltpu.PrefetchScalarGridSpec(
            num_scalar_prefetch=1, grid=(nb,),
            in_specs=[pl.BlockSpec((None, tr, hc), lambda r, pc: (pc[0], r, 0)),
                      pl.BlockSpec((3, tr, hc), lambda r, pc: (0, r, 0))],
            out_specs=o_spec),
        out_shape=jax.ShapeDtypeStruct(o_shape, F32),
        compiler_params=_cparams(("parallel",)),
    )(place, s, recv)


def sum_devices(v, *, name):
    _, R, C = v.shape

    def body(v_ref, o_ref):
        acc = v_ref[0]
        for k in range(1, N_DEV):
            acc = acc + v_ref[k]
        o_ref[...] = acc

    return pl.pallas_call(body, name=name, out_shape=jax.ShapeDtypeStruct((R, C), F32))(v)


def adamw(w, g, m, v, *, name):
    R, C = w.shape
    tr = _row_tile(R, C, 4)
    c1 = 1.0 / (1.0 - ADAM_B1 ** ADAM_STEP)
    c2 = 1.0 / (1.0 - ADAM_B2 ** ADAM_STEP)

    def body(w_ref, g_ref, m_ref, v_ref, go_ref, d_ref, nm_ref, nv_ref):
        gg = g_ref[...]
        go_ref[...] = gg
        nm = ADAM_B1 * m_ref[...] + (1.0 - ADAM_B1) * gg
        nv = ADAM_B2 * v_ref[...] + (1.0 - ADAM_B2) * (gg * gg)
        nm_ref[...] = nm
        nv_ref[...] = nv
        d_ref[...] = -ADAM_LR * ((nm * c1) / (jnp.sqrt(nv * c2) + ADAM_EPS) + ADAM_WD * w_ref[...])

    spec = pl.BlockSpec((tr, C), lambda i: (i, 0))
    return pl.pallas_call(
        body, name=name, grid=(R // tr,), in_specs=[spec] * 4, out_specs=[spec] * 4,
        out_shape=[jax.ShapeDtypeStruct((R, C), F32)] * 4, compiler_params=_cparams(("parallel",)),
    )(w, g, m, v)


def _rope_tables(T):
    pos = np.arange(T, dtype=np.float32)
    inv_freq = (ROPE_THETA ** (-np.arange(0, QK_ROPE, 2, dtype=np.float32) / QK_ROPE)).astype(np.float32)
    ang = pos[:, None] * inv_freq[None, :]
    z = np.zeros((T, 64), np.float32)
    cos = np.concatenate([np.cos(ang), np.cos(ang), z], axis=1).astype(np.float32)
    sin = np.concatenate([np.sin(ang), np.sin(ang), z], axis=1).astype(np.float32)
    return jnp.asarray(cos), jnp.asarray(sin)


def _padded_rows(length):
    T = -(-length // 16) * 16
    while True:
        for n in (4, 5, 6, 7, 8):
            if T % (16 * n) == 0:
                return T, n
        T += 16


def _pick(n, cands):
    for c in cands:
        if n % c == 0:
            return c
    return n


def kernel(x, meta_tokens, g_mix, w_in, b_glu, b_gate, w_dw, b_dw, g_conv_ln, b_conv_ln, w_conv_out, b_conv_out, g_q_lora, w_uq, g_kv_lora, w_uk, w_uv, w_attn_out, w_out, g_ffn, w_ffn_gate, w_ffn_up, w_ffn_down, g_final, loss_target, m_meta_tokens, m_g_mix, m_w_in, m_b_glu, m_b_gate, m_w_dw, m_b_dw, m_g_conv_ln, m_b_conv_ln, m_w_conv_out, m_b_conv_out, m_g_q_lora, m_w_uq, m_g_kv_lora, m_w_uk, m_w_uv, m_w_attn_out, m_w_out, m_g_ffn, m_w_ffn_gate, m_w_ffn_up, m_w_ffn_down, m_g_final, v_meta_tokens, v_g_mix, v_w_in, v_b_glu, v_b_gate, v_w_dw, v_b_dw, v_g_conv_ln, v_b_conv_ln, v_w_conv_out, v_b_conv_out, v_g_q_lora, v_w_uq, v_g_kv_lora, v_w_uk, v_w_uv, v_w_attn_out, v_w_out, v_g_ffn, v_w_ffn_gate, v_w_ffn_up, v_w_ffn_down, v_g_final):
    weights = dict(meta_tokens=meta_tokens, g_mix=g_mix, w_in=w_in, b_glu=b_glu, b_gate=b_gate, w_dw=w_dw, b_dw=b_dw,
                   g_conv_ln=g_conv_ln, b_conv_ln=b_conv_ln, w_conv_out=w_conv_out, b_conv_out=b_conv_out,
                   g_q_lora=g_q_lora, w_uq=w_uq, g_kv_lora=g_kv_lora, w_uk=w_uk, w_uv=w_uv, w_attn_out=w_attn_out,
                   w_out=w_out, g_ffn=g_ffn, w_ffn_gate=w_ffn_gate, w_ffn_up=w_ffn_up, w_ffn_down=w_ffn_down,
                   g_final=g_final)
    m_in = dict(meta_tokens=m_meta_tokens, g_mix=m_g_mix, w_in=m_w_in, b_glu=m_b_glu, b_gate=m_b_gate, w_dw=m_w_dw,
                b_dw=m_b_dw, g_conv_ln=m_g_conv_ln, b_conv_ln=m_b_conv_ln, w_conv_out=m_w_conv_out,
                b_conv_out=m_b_conv_out, g_q_lora=m_g_q_lora, w_uq=m_w_uq, g_kv_lora=m_g_kv_lora, w_uk=m_w_uk,
                w_uv=m_w_uv, w_attn_out=m_w_attn_out, w_out=m_w_out, g_ffn=m_g_ffn, w_ffn_gate=m_w_ffn_gate,
                w_ffn_up=m_w_ffn_up, w_ffn_down=m_w_ffn_down, g_final=m_g_final)
    v_in = dict(meta_tokens=v_meta_tokens, g_mix=v_g_mix, w_in=v_w_in, b_glu=v_b_glu, b_gate=v_b_gate, w_dw=v_w_dw,
                b_dw=v_b_dw, g_conv_ln=v_g_conv_ln, b_conv_ln=v_b_conv_ln, w_conv_out=v_w_conv_out,
                b_conv_out=v_b_conv_out, g_q_lora=v_g_q_lora, w_uq=v_w_uq, g_kv_lora=v_g_kv_lora, w_uk=v_w_uk,
                w_uv=v_w_uv, w_attn_out=v_w_attn_out, w_out=v_w_out, g_ffn=v_g_ffn, w_ffn_gate=v_w_ffn_gate,
                w_ffn_up=v_w_ffn_up, w_ffn_down=v_w_ffn_down, g_final=v_g_final)
    names = list(weights)

    seq, D = x.shape[1], x.shape[2]
    C = w_conv_out.shape[2]
    R = g_q_lora.shape[1]
    H = w_uv.shape[2] * N_CHIPS // V_HEAD
    FB = w_ffn_gate.shape[2]
    FF = FB * N_CHIPS
    length = N_META + seq
    T, nch = _padded_rows(length)
    unit = T // nch
    scale = (QK_NOPE + QK_ROPE) ** -0.5
    assert C == D and w_in.shape[2] * N_CHIPS == 2 * C + 2 * R + QK_ROPE + 2 * D

    xi, yi, ci = _place()
    chip = (2 * xi + yi).astype(jnp.int32)
    half_arr = ci.astype(jnp.int32).reshape(1)

    tr = unit
    tmh = T // 2 if (T // 2) % 16 == 0 else unit
    tmq = unit
    tmw = _pick(D, (1024, 512))
    rc = unit // 2 if unit % 16 == 0 else unit
    tcv = 128

    def tn_of(n, pref=512):
        return _pick(n, (pref, 384, 256, 128))

    grp_a = ["w_in"]
    grp_b = ["w_conv_out", "w_uq", "w_uk", "w_uv", "w_attn_out", "w_out"]
    grp_c = ["w_ffn_gate", "w_ffn_up", "w_ffn_down"]
    big = grp_a + grp_b + grp_c
    axes = {nm: 0 for nm in big}
    axes["w_in"] = 1
    shard = {nm: weights[nm][0] for nm in big}
    shard["w_in"] = jnp.swapaxes(w_in[0], 0, 1)

    def own_blocks(srcs, dtype=BF):
        bufs = []
        for w in srcs:
            wb = w.astype(dtype)
            bufs.append(lax.dynamic_update_slice(lax.empty((N_CHIPS,) + wb.shape, dtype), wb[None], (chip, 0, 0)))
        return bufs

    tiny = jnp.concatenate([meta_tokens, jnp.pad(w_dw[0], ((0, 1), (0, 0)))], axis=0)
    ax_a = [axes[nm] for nm in grp_a] + [0]
    bufs_a, ss_a, rs_a, tok_ga = copies_start(own_blocks([shard[nm] for nm in grp_a]) + own_blocks([tiny], F32),
                                              near_plan(ax_a), 2 * len(ax_a), half_arr, name="gather_a_near_start")
    tok_ga, held = lax.optimization_barrier((tok_ga, (x, loss_target, [shard[nm] for nm in grp_b + grp_c])))
    x_l, tgt_l, shard_bc = held
    h0 = jnp.pad(x_l[0], ((N_META, T - length), (0, 0)))
    tgt = jnp.pad(tgt_l[0], ((N_META, T - length), (0, 0)))
    bufs_b, bufs_c = own_blocks(shard_bc[:len(grp_b)]), own_blocks(shard_bc[len(grp_b):])
    cos, sin = _rope_tables(T)

    bufs_a = copies_wait(bufs_a, near_plan(ax_a), ss_a, rs_a, [h0, tgt] + bufs_b + bufs_c, name="gather_a_near_wait")
    bufs_a, ss_a, rs_a, _ = copies_start(bufs_a, relay_plan(ax_a), 4 * len(ax_a), half_arr,
                                         name="gather_a_relay_start")
    bufs_a = copies_wait(bufs_a, relay_plan(ax_a), ss_a, rs_a, half_arr, name="gather_a_relay_wait")
    bufs_a, ss_a, rs_a, _ = copies_start(bufs_a, far_forward_plan(ax_a), len(ax_a), half_arr,
                                         name="gather_a_far_start")
    bufs_a = copies_wait(bufs_a, far_forward_plan(ax_a), ss_a, rs_a, half_arr, name="gather_a_far_wait")
    G = dict(zip(grp_a, bufs_a))
    meta_full = jnp.concatenate([bufs_a[-1][k, :N_META] for k in range(N_CHIPS)], axis=1)
    w_dw_full = jnp.concatenate([bufs_a[-1][k, N_META:] for k in range(N_CHIPS)], axis=1)
    h0 = lax.dynamic_update_slice(h0, meta_full, (0, 0))
    ax_b = [axes[nm] for nm in grp_b]
    bufs_b, ss_b, rs_b, tok_gb = copies_start(bufs_b, near_plan(ax_b), 2 * len(grp_b), G["w_in"],
                                              name="gather_b_near_start")

    WT_in = G["w_in"].reshape(N_CHIPS * G["w_in"].shape[1], D)
    o1, o2 = 2 * C, 2 * C + 2 * R + QK_ROPE
    WT_small = jnp.pad(WT_in[o1:o2], ((0, 128 - QK_ROPE), (0, 0)))
    WT_gate = WT_in[o2:]

    u = rms_fwd(h0, g_mix, tr=tr, name="rms_mix", deps=[tok_gb])
    z_glu = mm(u, WT_in, mode="nt", tm=T, tn=tn_of(2 * C, 1024), tk=D, out_dtype=BF, name="mm_z_glu", n_out=o1)
    z_small = mm(u, WT_small, mode="nt", tm=T, tn=2 * R + 128, tk=D, out_dtype=F32, name="mm_z_small")
    z_gate = mm(u, WT_gate, mode="nt", tm=T, tn=tn_of(2 * D, 1024), tk=D, out_dtype=BF, name="mm_z_gate")

    bufs_b = copies_wait(bufs_b, near_plan(ax_b), ss_b, rs_b, z_glu, name="gather_b_near_wait")
    bufs_b, ss_b, rs_b, tok_gb = copies_start(bufs_b, relay_plan(ax_b), 4 * len(grp_b), half_arr,
                                              name="gather_b_relay_start")
    ax_c = [axes[nm] for nm in grp_c]
    bufs_c, ss_c, rs_c, tok_gc = copies_start(bufs_c, near_plan(ax_c), 2 * len(grp_c), tok_gb,
                                              name="gather_c_near_start")
    c0 = glu_fwd(z_glu, b_glu, tr=tr, name="glu_fwd", deps=[tok_gc])
    c1 = dwconv_fwd(c0, w_dw_full, b_dw, tc=tcv, rc=rc, name="dwconv_fwd")
    bufs_b = copies_wait(bufs_b, relay_plan(ax_b), ss_b, rs_b, c1, name="gather_b_relay_wait")
    bufs_b, ss_b, rs_b, tok_gb = copies_start(bufs_b, far_forward_plan(ax_b), len(grp_b), half_arr,
                                              name="gather_b_far_start")
    c3 = ln_silu_fwd(c1, g_conv_ln, b_conv_ln, tr=tr, name="ln_silu_fwd")
    cq, ckv, kr = lora_norm_fwd(z_small, g_q_lora, g_kv_lora, cos, sin, tr=tr, name="lora_norm_fwd", deps=[tok_gb])

    def cols_full(a):
        return jnp.concatenate([a[k] for k in range(N_CHIPS)], axis=1)

    G.update(zip(grp_b, copies_wait(bufs_b, far_forward_plan(ax_b), ss_b, rs_b, cq, name="gather_b_far_wait")))
    W_co = G["w_conv_out"].reshape(C, D)
    W_uq = cols_full(G["w_uq"]).reshape(R, H, QK_NOPE + QK_ROPE)
    Wq_p = jnp.pad(W_uq, ((0, 0), (0, 0), (0, HEAD_PAD - QK_NOPE - QK_ROPE))).reshape(R, H * HEAD_PAD)
    W_uk = cols_full(G["w_uk"])
    W_uv = cols_full(G["w_uv"])
    W_ao = G["w_attn_out"].reshape(H * V_HEAD, D)
    W_o = G["w_out"].reshape(D, D)

    y_conv = mm(c3, W_co, mode="nn", tm=T, tn=tn_of(D, 1024), tk=C, out_dtype=BF, name="mm_y_conv")

    q = q_proj(cq, Wq_p, cos, sin, scale=scale, name="q_proj")
    kk = k_proj(ckv, W_uk, kr, name="k_proj")
    vv = mm(ckv, W_uv, mode="nn", tm=T, tn=tn_of(H * V_HEAD), tk=R, out_dtype=BF, name="mm_v")
    o_attn, lse = attn_fwd(q, kk, vv, nch=nch, name="attn_fwd")
    bufs_c = copies_wait(bufs_c, near_plan(ax_c), ss_c, rs_c, o_attn, name="gather_c_near_wait")
    bufs_c, ss_c, rs_c, tok_gc = copies_start(bufs_c, relay_plan(ax_c), 4 * len(grp_c), half_arr,
                                              name="gather_c_relay_start")
    y_attn = mm(o_attn, W_ao, mode="nn", tm=T, tn=tn_of(D, 1024), tk=H * V_HEAD, out_dtype=BF, name="mm_y_attn",
                deps=[tok_gc])

    mix = mix_fwd(z_gate, y_conv, y_attn, b_gate, b_conv_out, tr=tr, name="mix_fwd")
    bufs_c = copies_wait(bufs_c, relay_plan(ax_c), ss_c, rs_c, mix, name="gather_c_relay_wait")
    bufs_c, ss_c, rs_c, tok_gc = copies_start(bufs_c, far_forward_plan(ax_c), len(grp_c), half_arr,
                                              name="gather_c_far_start")
    h1 = mm(mix, W_o, mode="nn", tm=T, tn=tn_of(D), tk=D, out_dtype=F32, name="mm_h1", res=h0, deps=[tok_gc])

    hn = rms_fwd(h1, g_ffn, tr=tr, name="rms_ffn")

    G.update(zip(grp_c, copies_wait(bufs_c, far_forward_plan(ax_c), ss_c, rs_c, hn, name="gather_c_far_wait")))
    W_fg, W_fu = G["w_ffn_gate"], G["w_ffn_up"]
    W_fd = G["w_ffn_down"].reshape(FF, D)

    fa, fb, f = ffn_fwd(hn, W_fg, W_fu, tm=tmq, name="ffn_fwd")
    h2 = mm(f, W_fd, mode="nn", tm=tmh, tn=tn_of(D), tk=FF, out_dtype=F32, name="mm_h2", res=h1)

    place_arr = jnp.stack([chip, ci.astype(jnp.int32)])

    def col_blocks(a):
        r, cfull = a.shape
        return a.reshape(r, N_CHIPS, cfull // N_CHIPS).transpose(1, 0, 2)

    def rs_begin(nms, gls, after, tag):
        ax = [axes[nm] for nm in nms]
        lands = []
        for g, a in zip(gls, ax):
            _, r, cfull = g.shape
            lands.append(lax.empty((N_CHIPS, r // 2, cfull) if a == 0 else (N_CHIPS, r, cfull // 2), BF))
        arrs, ssem, rsem, tok = copies_start(list(gls) + lands, exchange_plan(ax), N_CHIPS * len(nms), after,
                                             name=f"rs_{tag}_exchange_start")
        return (nms, ax, arrs, ssem, rsem), tok

    def rs_middle(state, after, tag):
        nms, ax, arrs, ssem, rsem = state
        n = len(nms)
        arrs = copies_wait(arrs, exchange_plan(ax), ssem, rsem, after, name=f"rs_{tag}_exchange_wait")
        pair = [pair_sum(g, r, half_arr, a, name="rs_pair_sum_" + nm)
                for nm, g, r, a in zip(nms, arrs[:n], arrs[n:], ax)]
        lands = [lax.empty((3,) + p.shape[1:], BF) for p in pair]
        arrs, ssem, rsem, tok = copies_start(pair + lands, scatter_plan(n), 3 * n, half_arr,
                                             name=f"rs_{tag}_scatter_start")
        return (nms, ax, arrs, ssem, rsem), tok

    def rs_end(state, after, tag):
        nms, ax, arrs, ssem, rsem = state
        n = len(nms)
        arrs = copies_wait(arrs, scatter_plan(n), ssem, rsem, after, name=f"rs_{tag}_scatter_wait")
        return [chip_sum(s, r, place_arr, a, name="rs_chip_sum_" + nm)
                for nm, s, r, a in zip(nms, arrs[:n], arrs[n:], ax)]

    def rs_shared(nms, reds, after, tag):
        ax = [axes[nm] for nm in nms]
        reds, ssem, rsem, _ = copies_start(reds, share_plan(ax), len(nms), after, name=f"rs_{tag}_share_start")
        return dict(zip(nms, copies_wait(reds, share_plan(ax), ssem, rsem, half_arr, name=f"rs_{tag}_share_wait")))

    d_h2, d_h2b, loss_p, d_g_final = final_loss(h2, tgt, g_final.reshape(1, D), seq=seq, tr=tr, name="final_loss")
    loss = lax.psum(loss_p[0, 0], ("x", "y", "c"))

    dW_fd = mm(f, d_h2b, mode="tn", tm=FB, tn=tn_of(D, 1024), tk=T, out_dtype=BF, name="mm_dw_ffn_down")
    d_a, d_b = ffn_dact(d_h2b, W_fd, fa, fb, tm=tmq, name="ffn_dact")
    dW_fg = mm(hn, d_a, mode="tn", tm=tmw, tn=FB, tk=T, out_dtype=BF, name="mm_dw_ffn_gate", out_blocks=N_CHIPS)
    dW_fu = mm(hn, d_b, mode="tn", tm=tmw, tn=FB, tk=T, out_dtype=BF, name="mm_dw_ffn_up", out_blocks=N_CHIPS)
    st_c, tok_rc = rs_begin(grp_c, [dW_fg, dW_fu, dW_fd.reshape(N_CHIPS, FB, D)], half_arr, "c")
    d_hn = ffn_dhn(d_a, d_b, W_fg, W_fu, tm=tmq, tn=_pick(D, (512, 256)), name="ffn_dhn", deps=[tok_rc])
    st_c, tok_rc = rs_middle(st_c, d_hn, "c")
    d_h1, d_h1b, d_g_ffn = rms_bwd(d_hn, h1, g_ffn, d_h2, tr=tr, name="rms_ffn_bwd", deps=[tok_rc])

    d_mix = mm(d_h1b, W_o, mode="nt", tm=T, tn=tn_of(D, 1024), tk=D, out_dtype=BF, name="mm_d_mix")
    dW_o = mm(mix, d_h1b, mode="tn", tm=tmw, tn=tn_of(D, 1024), tk=T, out_dtype=BF, name="mm_dw_out")
    d_yc, d_ya, d_zgate, d_b_gate, d_b_co = mix_bwd(z_gate, y_conv, y_attn, d_mix, b_gate, b_conv_out, tr=tr,
                                                    name="mix_bwd")

    dW_co = mm(c3, d_yc, mode="tn", tm=tmw, tn=tn_of(D, 1024), tk=T, out_dtype=BF, name="mm_dw_conv_out")
    d_c3 = mm(d_yc, W_co, mode="nt", tm=T, tn=tn_of(C, 1024), tk=D, out_dtype=BF, name="mm_d_c3")
    d_c1, d_g_ln, d_b_ln, d_b_dw = ln_silu_bwd(c1, d_c3, g_conv_ln, b_conv_ln, tr=tr, name="ln_silu_bwd")
    d_c0, d_w_dw = dwconv_bwd(c0, d_c1, w_dw_full, tc=tcv, rc=rc, name="dwconv_bwd")
    d_zglu, d_b_glu = glu_bwd(z_glu, b_glu, d_c0, tr=tr, name="glu_bwd")

    dW_ao = mm(o_attn, d_ya, mode="tn", tm=_pick(H * V_HEAD, (1024, 512)), tn=tn_of(D, 1024), tk=T, out_dtype=BF, name="mm_dw_attn_out")
    d_o = mm(d_ya, W_ao, mode="nt", tm=T, tn=tn_of(H * V_HEAD, 1024), tk=D, out_dtype=BF, name="mm_d_o")
    d_q2, d_k2, d_v, d_kr = attn_bwd(q, kk, vv, o_attn, d_o, lse, cos, sin, scale=scale, nch=nch, name="attn_bwd")

    dWq_p = mm(cq, d_q2, mode="tn", tm=R, tn=tn_of(H * HEAD_PAD), tk=T, out_dtype=BF, name="mm_dw_uq")
    d_cq = mm(d_q2, Wq_p, mode="nt", tm=T, tn=R, tk=_pick(H * HEAD_PAD, (1024,)), out_dtype=F32, name="mm_d_cq")
    Wk_p = jnp.pad(W_uk.reshape(R, H, QK_NOPE), ((0, 0), (0, 0), (0, HEAD_PAD - QK_NOPE))).reshape(R, H * HEAD_PAD)
    dWk_p = mm(ckv, d_k2, mode="tn", tm=R, tn=tn_of(H * HEAD_PAD), tk=T, out_dtype=BF, name="mm_dw_uk")
    dW_uv = mm(ckv, d_v, mode="tn", tm=R, tn=tn_of(H * V_HEAD), tk=T, out_dtype=BF, name="mm_dw_uv")
    sh_c = rs_end(st_c, [dWq_p, dWk_p, dW_uv, dW_co, dW_ao, dW_o], "c")
    dW_uq = dWq_p.reshape(R, H, HEAD_PAD)[:, :, :QK_NOPE + QK_ROPE].reshape(R, H * (QK_NOPE + QK_ROPE))
    dW_uk = dWk_p.reshape(R, H, HEAD_PAD)[:, :, :QK_NOPE].reshape(R, H * QK_NOPE)
    st_b, tok_rb = rs_begin(grp_b, [dW_co.reshape(N_CHIPS, C // N_CHIPS, D), col_blocks(dW_uq), col_blocks(dW_uk),
                                    col_blocks(dW_uv), dW_ao.reshape(N_CHIPS, H * V_HEAD // N_CHIPS, D),
                                    dW_o.reshape(N_CHIPS, D // N_CHIPS, D)], sh_c[-1], "b")
    d_ckv = mm(d_k2, Wk_p, mode="nt", tm=T, tn=R, tk=_pick(H * HEAD_PAD, (1024,)), out_dtype=F32, name="mm_d_ckv_k",
               deps=[tok_rb])
    d_ckv = mm(d_v, W_uv, mode="nt", tm=T, tn=R, tk=_pick(H * V_HEAD, (1024,)), out_dtype=F32, name="mm_d_ckv_v",
               res=d_ckv)
    d_zsmall, d_g_q, d_g_kv = lora_norm_bwd(z_small, d_cq, d_ckv, d_kr, cos, sin, g_q_lora, g_kv_lora, tr=tr,
                                            name="lora_norm_bwd")
    st_b, tok_rb = rs_middle(st_b, d_zsmall, "b")

    tmi = _pick(2 * C, (1024, 512))
    dWT_in = lax.empty((WT_in.shape[0], D), BF)
    dWT_in = mm_tn_into(d_zglu, u, dWT_in, 0, tm=tmi, tn=tn_of(D, 1024), name="mm_dw_glu", deps=[tok_rb])
    dWT_in = mm_tn_into(d_zsmall, u, dWT_in, o1, tm=2 * R + 128, tn=tn_of(D), name="mm_dw_small")
    dWT_in = mm_tn_into(d_zgate, u, dWT_in, o2, tm=tmi, tn=tn_of(D, 1024), name="mm_dw_gate")
    sh_b = rs_end(st_b, dWT_in, "b")
    sh_c, ss_sc, rs_sc, tok_sc = copies_start(sh_c, share_plan(ax_c), len(grp_c), sh_b[-1], name="rs_c_share_start")
    st_a, tok_ra = rs_begin(grp_a, [dWT_in.reshape(N_CHIPS, dWT_in.shape[0] // N_CHIPS, D)], tok_sc, "a")
    d_u = mm(d_zglu, WT_in, mode="nn", tm=T, tn=tn_of(D, 256), tk=2 * C, out_dtype=F32, name="mm_d_u_glu",
             deps=[tok_ra])
    st_a, tok_ra = rs_middle(st_a, d_u, "a")
    d_u = mm(d_zsmall, WT_small, mode="nn", tm=T, tn=tn_of(D), tk=2 * R + 128, out_dtype=F32, name="mm_d_u_small",
             res=d_u, deps=[tok_ra])
    d_u = mm(d_zgate, WT_gate, mode="nn", tm=T, tn=tn_of(D, 256), tk=2 * D, out_dtype=F32,
             name="mm_d_u_gate", res=d_u)
    d_h0, _, d_g_mix = rms_bwd(d_u, h0, g_mix, d_h1, tr=tr, name="rms_mix_bwd")

    grad_x = d_h0[N_META:length][None]

    small = ["g_mix", "b_glu", "b_gate", "b_dw", "g_conv_ln", "b_conv_ln", "b_conv_out", "g_q_lora", "g_kv_lora",
             "g_ffn", "g_final", "w_dw", "meta_tokens"]
    sgrads = dict(g_mix=d_g_mix, b_glu=d_b_glu, b_gate=d_b_gate, b_dw=d_b_dw, g_conv_ln=d_g_ln, b_conv_ln=d_b_ln,
                  b_conv_out=d_b_co, g_q_lora=d_g_q, g_kv_lora=d_g_kv, g_ffn=d_g_ffn, g_final=d_g_final,
                  w_dw=d_w_dw[:CONV_WIDTH], meta_tokens=d_h0[:N_META])
    sizes = [int(np.prod(sgrads[nm].shape)) for nm in small]
    packed = jnp.concatenate([sgrads[nm].reshape(-1) for nm in small]).reshape(-1, 128)
    sm_arrs, sm_ss, sm_rs, tok_sm = copies_start([packed, lax.empty((N_DEV,) + packed.shape, F32)], devices_plan,
                                                 N_DEV - 1, packed, name="gather_small_start")

    grads, delta, new_m, new_v, delta_2d = {}, {}, {}, {}, {}

    def adamw_2d(nm, w2, g2, m2, v2, back):
        g_, d_, m_, v_ = adamw(w2, g2, m2, v2, name="adamw_" + nm)
        delta_2d[nm] = d_
        grads[nm], delta[nm], new_m[nm], new_v[nm] = back(g_), back(d_), back(m_), back(v_)

    reduced = dict(zip(grp_c, copies_wait(sh_c, share_plan(ax_c), ss_sc, rs_sc, tok_sm, name="rs_c_share_wait")))
    reduced.update(rs_shared(grp_b, sh_b, tok_sm, "b"))
    for nm in grp_c + grp_b + grp_a:
        if nm == grp_a[0]:
            reduced.update(rs_shared(grp_a, rs_end(st_a, [delta_2d[k] for k in grp_c + grp_b], "a"), half_arr, "a"))
        shp = weights[nm].shape
        if axes[nm] == 1:
            adamw_2d(nm, shard[nm], reduced[nm], jnp.swapaxes(m_in[nm][0], 0, 1), jnp.swapaxes(v_in[nm][0], 0, 1),
                     lambda a: jnp.swapaxes(a, 0, 1)[None])
        else:
            two = (shp[-2], shp[-1])
            adamw_2d(nm, weights[nm].reshape(two), reduced[nm], m_in[nm].reshape(two), v_in[nm].reshape(two),
                     lambda a, shp=shp: a.reshape(shp))

    sm_arrs = copies_wait(sm_arrs, devices_plan, sm_ss, sm_rs, delta_2d[grp_a[0]], name="gather_small_wait")
    me = (4 * xi + 2 * yi + ci).astype(jnp.int32)
    parts = lax.dynamic_update_slice(sm_arrs[1], sm_arrs[0][None], (me, 0, 0))
    summed = sum_devices(parts, name="sum_small_grads").reshape(-1)
    offs = np.concatenate([[0], np.cumsum(sizes)])
    sfull = {nm: summed[int(offs[i]):int(offs[i + 1])].reshape(sgrads[nm].shape) for i, nm in enumerate(small)}
    for nm in small:
        gfull = sfull[nm]
        if nm == "w_dw":
            cb = C // N_CHIPS
            grads[nm] = lax.dynamic_slice(gfull, (0, chip * cb), (CONV_WIDTH, cb))[None]
        elif nm == "meta_tokens":
            cb = D // N_CHIPS
            grads[nm] = lax.dynamic_slice(gfull, (0, chip * cb), (N_META, cb))
        else:
            grads[nm] = gfull.reshape(weights[nm].shape)

    rep = [nm for nm in small if nm not in ("w_dw", "meta_tokens")]

    def pack(d):
        return jnp.concatenate([d[nm].reshape(-1) for nm in rep]).reshape(-1, 128)

    _, pd, pm, pv = adamw(pack(weights), pack(grads), pack(m_in), pack(v_in), name="adamw_small")
    rsz = [int(np.prod(weights[nm].shape)) for nm in rep]
    roff = np.concatenate([[0], np.cumsum(rsz)])
    for i, nm in enumerate(rep):
        sl = slice(int(roff[i]), int(roff[i + 1]))
        delta[nm] = pd.reshape(-1)[sl].reshape(weights[nm].shape)
        new_m[nm] = pm.reshape(-1)[sl].reshape(weights[nm].shape)
        new_v[nm] = pv.reshape(-1)[sl].reshape(weights[nm].shape)
    for nm in ["w_dw", "meta_tokens"]:
        shp = weights[nm].shape
        two = (shp[-2], shp[-1])
        adamw_2d(nm, weights[nm].reshape(two), grads[nm].reshape(two), m_in[nm].reshape(two), v_in[nm].reshape(two),
                 lambda a, shp=shp: a.reshape(shp))

    return (loss, grad_x, *[grads[nm] for nm in names], *[delta[nm] for nm in names],
            *[new_m[nm] for nm in names], *[new_v[nm] for nm in names])
```

```python
import functools

import numpy as np
import jax
import jax.numpy as jnp
from jax import lax
from jax.experimental import pallas as pl
from jax.experimental.pallas import tpu as pltpu

F32 = jnp.float32
BF = jnp.bfloat16
MESH = pl.DeviceIdType.MESH

N_META = 16
CONV_WIDTH = 31
CONV_PAD = 32
QK_NOPE = 128
QK_ROPE = 64
V_HEAD = 128
HEAD_PAD = 256
ROPE_THETA = 10000.0
EPS = 1e-6
ADAM_LR = 0.001
ADAM_B1 = 0.9
ADAM_B2 = 0.999
ADAM_EPS = 1e-08
ADAM_WD = 0.01
ADAM_STEP = 10
VMEM_LIMIT = 56 * 1024 * 1024
N_CHIPS = 4
N_DEV = 8


def _cparams(sem):
    return pltpu.CompilerParams(dimension_semantics=sem, vmem_limit_bytes=VMEM_LIMIT)


def mm(a, b, *, mode, tm, tn, tk, out_dtype, name, res=None, out_blocks=None, deps=(), n_out=None):
    b3 = b.ndim == 3
    if mode == "nn":
        M, K = a.shape
        N = b.shape[0] * b.shape[2] if b3 else b.shape[1]
        a_spec = pl.BlockSpec((tm, tk), lambda i, j, k: (i, k))
        if b3:
            per = b.shape[2] // tn
            b_spec = pl.BlockSpec((None, tk, tn), lambda i, j, k: (j // per, k, j % per))
        else:
            b_spec = pl.BlockSpec((tk, tn), lambda i, j, k: (k, j))
        dims = (((1,), (0,)), ((), ()))
    elif mode == "nt":
        M, K = a.shape
        N = n_out if n_out is not None else (b.shape[1] if b3 else b.shape[0])
        a_spec = pl.BlockSpec((tm, tk), lambda i, j, k: (i, k))
        if b3:
            per = b.shape[2] // tk
            b_spec = pl.BlockSpec((None, tn, tk), lambda i, j, k: (k // per, j, k % per))
        else:
            b_spec = pl.BlockSpec((tn, tk), lambda i, j, k: (j, k))
        dims = (((1,), (1,)), ((), ()))
    else:
        K, M = a.shape
        N = b.shape[1]
        a_spec = pl.BlockSpec((tk, tm), lambda i, j, k: (k, i))
        b_spec = pl.BlockSpec((tk, tn), lambda i, j, k: (k, j))
        dims = (((0,), (0,)), ((), ()))
    gm, gn, gk = M // tm, N // tn, K // tk
    assert gm * tm == M and gn * tn == N and gk * tk == K, (name, a.shape, b.shape, tm, tn, tk)
    if out_blocks is None:
        o_spec = pl.BlockSpec((tm, tn), lambda i, j, k: (i, j))
        o_shape = (M, N)
    else:
        nbw = N // out_blocks
        per_o = nbw // tn
        assert per_o * tn == nbw
        o_spec = pl.BlockSpec((None, tm, tn), lambda i, j, k: (j // per_o, i, j % per_o))
        o_shape = (out_blocks, M, nbw)
    has_res = res is not None

    def body(*refs):
        a_ref, b_ref = refs[0], refs[1]
        r_ref = refs[2] if has_res else None
        o_ref = refs[2 + has_res + len(deps)]
        acc = refs[-1]
        p = lax.dot_general(a_ref[...], b_ref[...], dims, preferred_element_type=F32)

        def finish(v):
            if has_res:
                v = v + r_ref[...]
            o_ref[...] = v.astype(o_ref.dtype)

        if gk == 1:
            finish(p)
        else:
            k = pl.program_id(2)

            @pl.when(k == 0)
            def _():
                acc[...] = p

            @pl.when(k > 0)
            def _():
                acc[...] += p

            @pl.when(k == gk - 1)
            def _():
                finish(acc[...])

    in_specs = [a_spec, b_spec]
    args = [a, b]
    if has_res:
        in_specs.append(pl.BlockSpec((tm, tn), lambda i, j, k: (i, j)))
        args.append(res)
    in_specs += [pl.BlockSpec(memory_space=pl.ANY)] * len(deps)
    args += list(deps)
    return pl.pallas_call(
        body, name=name, grid=(gm, gn, gk), in_specs=in_specs, out_specs=o_spec,
        out_shape=jax.ShapeDtypeStruct(o_shape, out_dtype),
        scratch_shapes=[pltpu.VMEM((tm, tn), F32)] if gk > 1 else [],
        compiler_params=_cparams(("parallel", "parallel", "arbitrary")),
    )(*args)


def mm_tn_into(a, b, buf, row0, *, tm, tn, name, deps=()):
    K, M = a.shape
    N = b.shape[1]
    gm, gn = M // tm, N // tn
    assert gm * tm == M and gn * tn == N and buf.shape[1] == N and row0 + M <= buf.shape[0]
    steps = gm * gn

    def body(a_ref, b_ref, buf_ref, *rest):
        o_ref, tile, sem = rest[len(deps):]
        i, j = pl.program_id(0), pl.program_id(1)
        s = i * gn + j
        slot = s % 2

        def out_copy(sl):
            return pltpu.make_async_copy(tile.at[sl], o_ref.at[pl.ds(row0 + i * tm, tm), pl.ds(j * tn, tn)],
                                         sem.at[sl])

        @pl.when(s >= 2)
        def _():
            out_copy(slot).wait()

        tile[slot] = lax.dot_general(a_ref[...], b_ref[...], (((0,), (0,)), ((), ())),
                                     preferred_element_type=F32).astype(tile.dtype)
        out_copy(slot).start()

        @pl.when(s == steps - 1)
        def _():
            out_copy(slot).wait()
            if steps >= 2:
                out_copy(1 - slot).wait()

    anyspec = pl.BlockSpec(memory_space=pl.ANY)
    return pl.pallas_call(
        body, name=name, grid=(gm, gn),
        in_specs=[pl.BlockSpec((K, tm), lambda i, j: (0, i)), pl.BlockSpec((K, tn), lambda i, j: (0, j)), anyspec]
        + [anyspec] * len(deps),
        out_specs=anyspec, out_shape=jax.ShapeDtypeStruct(buf.shape, buf.dtype),
        input_output_aliases={2: 0},
        scratch_shapes=[pltpu.VMEM((2, tm, tn), buf.dtype), pltpu.SemaphoreType.DMA((2,))],
        compiler_params=_cparams(("arbitrary", "arbitrary")),
    )(a, b, buf, *deps)


def rowwise(body, row_ins, full_ins, row_outs, acc_outs, *, tr, name, deps=()):
    T = row_ins[0].shape[0]
    assert T % tr == 0, (name, T, tr)
    n_ri, n_fi, n_ro = len(row_ins), len(full_ins), len(row_outs)
    n_in = n_ri + n_fi + len(deps)

    def kern(*refs):
        body(pl.program_id(0), refs[:n_ri], refs[n_ri:n_ri + n_fi], refs[n_in:n_in + n_ro], refs[n_in + n_ro:])

    in_specs = [pl.BlockSpec((tr, a.shape[1]), lambda i: (i, 0)) for a in row_ins]
    in_specs += [pl.BlockSpec(a.shape, lambda i: (0, 0)) for a in full_ins]
    in_specs += [pl.BlockSpec(memory_space=pl.ANY)] * len(deps)
    out_specs = [pl.BlockSpec((tr, c), lambda i: (i, 0)) for c, _ in row_outs]
    out_specs += [pl.BlockSpec(s, lambda i: (0, 0)) for s in acc_outs]
    out_shape = [jax.ShapeDtypeStruct((T, c), d) for c, d in row_outs]
    out_shape += [jax.ShapeDtypeStruct(s, F32) for s in acc_outs]
    return pl.pallas_call(
        kern, name=name, grid=(T // tr,), in_specs=in_specs, out_specs=out_specs, out_shape=out_shape,
        compiler_params=_cparams(("arbitrary",)),
    )(*row_ins, *full_ins, *deps)


def _acc(step, ref, val):
    @pl.when(step == 0)
    def _():
        ref[...] = val

    @pl.when(step > 0)
    def _():
        ref[...] += val


def _colsum(x):
    return jnp.sum(x, axis=0, keepdims=True)


def _sigmoid(x):
    return 1.0 / (1.0 + jnp.exp(-x))


def _rot_half(v):
    lane = lax.broadcasted_iota(jnp.int32, v.shape, 1)
    lo = -pltpu.roll(v, 96, 1)
    hi = pltpu.roll(v, 32, 1)
    return jnp.where(lane < 32, lo, jnp.where(lane < 64, hi, 0.0))


def rms_fwd(h, g, *, tr, name, deps=()):
    def body(step, ri, fi, ro, ao):
        x = ri[0][...]
        r = lax.rsqrt(jnp.mean(x * x, axis=-1, keepdims=True) + EPS)
        ro[0][...] = ((x * r) * fi[0][...]).astype(BF)

    return rowwise(body, [h], [g], [(h.shape[1], BF)], [], tr=tr, name=name, deps=deps)[0]


def rms_bwd(dy, h, g, dres, *, tr, name, deps=()):
    D = h.shape[1]

    def body(step, ri, fi, ro, ao):
        d, x, dr = ri[0][...].astype(F32), ri[1][...], ri[2][...]
        r = lax.rsqrt(jnp.mean(x * x, axis=-1, keepdims=True) + EPS)
        n = x * r
        _acc(step, ao[0], _colsum(d * n))
        dn = d * fi[0][...]
        dh = r * (dn - n * jnp.mean(dn * n, axis=-1, keepdims=True)) + dr
        ro[0][...] = dh
        ro[1][...] = dh.astype(BF)

    return rowwise(body, [dy, h, dres], [g], [(D, F32), (D, BF)], [(1, D)], tr=tr, name=name, deps=deps)


def glu_fwd(z, b, *, tr, name, deps=()):
    C = z.shape[1] // 2

    def body(step, ri, fi, ro, ao):
        zz = ri[0][...].astype(F32) + fi[0][...]
        ro[0][...] = zz[:, :C] * _sigmoid(zz[:, C:])

    return rowwise(body, [z], [b], [(C, F32)], [], tr=tr, name=name, deps=deps)[0]


def glu_proj(u, wt, b, C, *, tn, name):
    T, D = u.shape
    nb = C // tn
    nt = (((1,), (1,)), ((), ()))

    def body(a_ref, w1_ref, w2_ref, b1_ref, b2_ref, z_ref, c_ref):
        a = a_ref[...]
        za = lax.dot_general(a, w1_ref[...], nt, preferred_element_type=F32)
        zb = lax.dot_general(a, w2_ref[...], nt, preferred_element_type=F32)
        z_ref[0] = za.astype(BF)
        z_ref[1] = zb.astype(BF)
        c_ref[...] = (za + b1_ref[...]) * _sigmoid(zb + b2_ref[...])

    return pl.pallas_call(
        body, name=name, grid=(nb,),
        in_specs=[pl.BlockSpec((T, D), lambda j: (0, 0)), pl.BlockSpec((tn, D), lambda j: (j, 0)),
                  pl.BlockSpec((tn, D), lambda j: (nb + j, 0)), pl.BlockSpec((1, tn), lambda j: (0, j)),
                  pl.BlockSpec((1, tn), lambda j: (0, nb + j))],
        out_specs=[pl.BlockSpec((2, T, tn), lambda j: (0, 0, j)), pl.BlockSpec((T, tn), lambda j: (0, j))],
        out_shape=[jax.ShapeDtypeStruct((2, T, C), BF), jax.ShapeDtypeStruct((T, C), F32)],
        compiler_params=_cparams(("parallel",)),
    )(u, wt, wt, b, b)


def glu_bwd(z, b, dc, *, tr, name):
    C = z.shape[2]

    def body(step, ri, fi, ro, ao):
        zz = jnp.concatenate([ri[0][...].astype(F32), ri[2][...].astype(F32)], axis=1) + fi[0][...]
        d = ri[1][...]
        za, sg = zz[:, :C], _sigmoid(zz[:, C:])
        dza = d * sg
        dzb = d * za * sg * (1.0 - sg)
        ro[0][:, :C] = dza.astype(BF)
        ro[0][:, C:] = dzb.astype(BF)
        _acc(step, ao[0], _colsum(dza))
        _acc(step, ao[1], _colsum(dzb))

    dz, da, db = rowwise(body, [z[0], dc, z[1]], [b], [(2 * C, BF)], [(1, C), (1, C)], tr=tr, name=name)
    return dz, jnp.concatenate([da, db], axis=1)


def ln_silu_fwd(c1, g, b, *, tr, name):
    def body(step, ri, fi, ro, ao):
        x = ri[0][...]
        mu = jnp.mean(x, axis=-1, keepdims=True)
        xc = x - mu
        rstd = lax.rsqrt(jnp.mean(xc * xc, axis=-1, keepdims=True) + EPS)
        y = (xc * rstd) * fi[0][...] + fi[1][...]
        ro[0][...] = (y * _sigmoid(y)).astype(BF)

    return rowwise(body, [c1], [g, b], [(c1.shape[1], BF)], [], tr=tr, name=name)[0]


def ln_silu_bwd(c1, dc3, g, b, *, tr, name):
    C = c1.shape[1]

    def body(step, ri, fi, ro, ao):
        x, d3 = ri[0][...], ri[1][...].astype(F32)
        mu = jnp.mean(x, axis=-1, keepdims=True)
        xc = x - mu
        rstd = lax.rsqrt(jnp.mean(xc * xc, axis=-1, keepdims=True) + EPS)
        n = xc * rstd
        y = n * fi[0][...] + fi[1][...]
        sg = _sigmoid(y)
        dy = d3 * (sg * (1.0 + y * (1.0 - sg)))
        _acc(step, ao[0], _colsum(dy * n))
        _acc(step, ao[1], _colsum(dy))
        dn = dy * fi[0][...]
        dx = rstd * (dn - jnp.mean(dn, axis=-1, keepdims=True) - n * jnp.mean(dn * n, axis=-1, keepdims=True))
        ro[0][...] = dx
        _acc(step, ao[2], _colsum(dx))

    return rowwise(body, [c1, dc3], [g, b], [(C, F32)], [(1, C), (1, C), (1, C)], tr=tr, name=name)


def lora_norm_fwd(zs, gq, gkv, cos, sin, *, tr, name, deps=()):
    R = gq.shape[1]

    def body(step, ri, fi, ro, ao):
        z = ri[0][...]
        for o, gi in ((0, 0), (1, 1)):
            x = z[:, o * R:(o + 1) * R]
            r = lax.rsqrt(jnp.mean(x * x, axis=-1, keepdims=True) + EPS)
            ro[o][...] = ((x * r) * fi[gi][...]).astype(BF)
        kr = z[:, 2 * R:2 * R + 128]
        ro[2][...] = kr * ri[1][...] + _rot_half(kr) * ri[2][...]

    return rowwise(body, [zs, cos, sin], [gq, gkv], [(R, BF), (R, BF), (128, F32)], [], tr=tr, name=name, deps=deps)


def lora_norm_bwd(zs, dcq, dckv, dkr, cos, sin, gq, gkv, *, tr, name):
    R = gq.shape[1]

    def body(step, ri, fi, ro, ao):
        z = ri[0][...]
        for o in (0, 1):
            x = z[:, o * R:(o + 1) * R]
            d = ri[1 + o][...]
            r = lax.rsqrt(jnp.mean(x * x, axis=-1, keepdims=True) + EPS)
            n = x * r
            _acc(step, ao[o], _colsum(d * n))
            dn = d * fi[o][...]
            ro[0][:, o * R:(o + 1) * R] = (r * (dn - n * jnp.mean(dn * n, axis=-1, keepdims=True))).astype(BF)
        dk = ri[3][...]
        ro[0][:, 2 * R:2 * R + 128] = (dk * ri[4][...] - _rot_half(dk * ri[5][...])).astype(BF)

    return rowwise(body, [zs, dcq, dckv, dkr, cos, sin], [gq, gkv], [(2 * R + 128, BF)], [(1, R), (1, R)],
                   tr=tr, name=name)


def mix_fwd(zg, yc, ya, bg, bco, *, tr, name):
    D = yc.shape[1]

    def body(step, ri, fi, ro, ao):
        g = _sigmoid(ri[0][...].astype(F32) + fi[0][...])
        ro[0][...] = (g[:, :D] * (ri[1][...].astype(F32) + fi[1][...]) + g[:, D:] * ri[2][...].astype(F32)).astype(BF)

    return rowwise(body, [zg, yc, ya], [bg, bco], [(D, BF)], [], tr=tr, name=name)[0]


def mix_bwd(zg, yc, ya, dmix, bg, bco, *, tr, name):
    D = yc.shape[1]

    def body(step, ri, fi, ro, ao):
        g = _sigmoid(ri[0][...].astype(F32) + fi[0][...])
        gc, ga = g[:, :D], g[:, D:]
        ycv = ri[1][...].astype(F32) + fi[1][...]
        yav = ri[2][...].astype(F32)
        dm = ri[3][...].astype(F32)
        dyc = dm * gc
        ro[0][...] = dyc.astype(BF)
        ro[1][...] = (dm * ga).astype(BF)
        dzc = dm * ycv * gc * (1.0 - gc)
        dza = dm * yav * ga * (1.0 - ga)
        ro[2][:, :D] = dzc.astype(BF)
        ro[2][:, D:] = dza.astype(BF)
        _acc(step, ao[0], _colsum(dzc))
        _acc(step, ao[1], _colsum(dza))
        _acc(step, ao[2], _colsum(dyc))

    dyc, dya, dzg, dbc, dba, dbo = rowwise(body, [zg, yc, ya, dmix], [bg, bco], [(D, BF), (D, BF), (2 * D, BF)],
                                           [(1, D), (1, D), (1, D)], tr=tr, name=name)
    return dyc, dya, dzg, jnp.concatenate([dbc, dba], axis=1), dbo


def ffn_fwd(hn, wg, wu, *, tm, name):
    T, D = hn.shape
    nb, _, FB = wg.shape
    assert T % tm == 0

    def body(h_ref, g_ref, u_ref, a_ref, b_ref, f_ref):
        h = h_ref[...]
        a = jnp.dot(h, g_ref[...], preferred_element_type=F32)
        b = jnp.dot(h, u_ref[...], preferred_element_type=F32)
        a_ref[...] = a
        b_ref[...] = b
        f_ref[...] = (a * _sigmoid(a) * b).astype(BF)

    wspec = pl.BlockSpec((None, D, FB), lambda j, i: (j, 0, 0))
    ospec = pl.BlockSpec((tm, FB), lambda j, i: (i, j))
    return pl.pallas_call(
        body, name=name, grid=(nb, T // tm),
        in_specs=[pl.BlockSpec((tm, D), lambda j, i: (i, 0)), wspec, wspec], out_specs=[ospec, ospec, ospec],
        out_shape=[jax.ShapeDtypeStruct((T, nb * FB), F32), jax.ShapeDtypeStruct((T, nb * FB), F32),
                   jax.ShapeDtypeStruct((T, nb * FB), BF)],
        compiler_params=_cparams(("parallel", "parallel")),
    )(hn, wg, wu)


def ffn_dact(dh, wd, a, b, *, tm, name):
    T, D = dh.shape
    FF = wd.shape[0]
    FB = FF // N_CHIPS

    def body(dh_ref, w_ref, a_ref, b_ref, da_ref, db_ref):
        d = lax.dot_general(dh_ref[...], w_ref[...], (((1,), (1,)), ((), ())), preferred_element_type=F32)
        x, u = a_ref[...], b_ref[...]
        sg = _sigmoid(x)
        da_ref[...] = (d * u * (sg * (1.0 + x * (1.0 - sg)))).astype(BF)
        db_ref[...] = (d * (x * sg)).astype(BF)

    blk = pl.BlockSpec((tm, FB), lambda j, i: (i, j))
    return pl.pallas_call(
        body, name=name, grid=(N_CHIPS, T // tm),
        in_specs=[pl.BlockSpec((tm, D), lambda j, i: (i, 0)), pl.BlockSpec((FB, D), lambda j, i: (j, 0)), blk, blk],
        out_specs=[blk, blk],
        out_shape=[jax.ShapeDtypeStruct((T, FF), BF), jax.ShapeDtypeStruct((T, FF), BF)],
        compiler_params=_cparams(("parallel", "parallel")),
    )(dh, wd, a, b)


def ffn_dhn(da, db, wg, wu, *, tm, tn, name, deps=()):
    T, FF = da.shape
    nb, D, FB = wg.shape
    dims = (((1,), (1,)), ((), ()))

    def body(da_ref, db_ref, g_ref, u_ref, *rest):
        o_ref = rest[len(deps)]
        acc = None
        for k in range(nb):
            cols = slice(k * FB, (k + 1) * FB)
            p = lax.dot_general(da_ref[:, cols], g_ref[k], dims, preferred_element_type=F32)
            p = p + lax.dot_general(db_ref[:, cols], u_ref[k], dims, preferred_element_type=F32)
            acc = p if acc is None else acc + p
        o_ref[...] = acc.astype(BF)

    aspec = pl.BlockSpec((tm, FF), lambda i, j: (i, 0))
    wspec = pl.BlockSpec((nb, tn, FB), lambda i, j: (0, j, 0))
    return pl.pallas_call(
        body, name=name, grid=(T // tm, D // tn),
        in_specs=[aspec, aspec, wspec, wspec] + [pl.BlockSpec(memory_space=pl.ANY)] * len(deps),
        out_specs=pl.BlockSpec((tm, tn), lambda i, j: (i, j)),
        out_shape=jax.ShapeDtypeStruct((T, D), BF),
        compiler_params=_cparams(("parallel", "parallel")),
    )(da, db, wg, wu, *deps)


def final_loss(h2, tgt, g, *, seq, tr, name):
    D = h2.shape[1]

    def body(step, ri, fi, ro, ao):
        x = ri[0][...]
        r = lax.rsqrt(jnp.mean(x * x, axis=-1, keepdims=True) + EPS)
        n = x * r
        gg = fi[0][...]
        row = lax.broadcasted_iota(jnp.int32, (tr, 1), 0) + step * tr
        live = jnp.logical_and(row >= N_META, row < N_META + seq)
        e = jnp.where(live, n * gg - ri[1][...], 0.0)
        _acc(step, ao[0], jnp.broadcast_to(0.5 * jnp.sum(jnp.mean(e * e, axis=-1, keepdims=True)), (1, 128)))
        dy = e * (1.0 / D)
        _acc(step, ao[1], _colsum(dy * n))
        dn = dy * gg
        dh = r * (dn - n * jnp.mean(dn * n, axis=-1, keepdims=True))
        ro[0][...] = dh
        ro[1][...] = dh.astype(BF)

    return rowwise(body, [h2, tgt], [g], [(D, F32), (D, BF)], [(1, 128), (1, D)], tr=tr, name=name)


def _shifted_copies(s_ref, T, rc):
    for r in range(1, 8):
        for t0 in range(0, T, rc):
            s_ref[r, t0:t0 + rc, :] = s_ref[0, t0 + r:t0 + r + rc, :]
        s_ref[r, T:T + CONV_PAD - 8, :] = s_ref[0, T + r:T + r + CONV_PAD - 8, :]


def _shifted(s_ref, start, rows):
    return s_ref[start % 8, start - start % 8:start - start % 8 + rows, :]


def dwconv_fwd(x, w, b, *, tc, rc, name, deps=()):
    T, C = x.shape
    nchunk = T // rc
    assert nchunk * rc == T and C % tc == 0

    def body(x_ref, w_ref, b_ref, *rest):
        y_ref, xs_ref = rest[len(deps):]
        xs_ref[0, 0:CONV_PAD, :] = jnp.zeros((CONV_PAD, tc), F32)
        xs_ref[0, CONV_PAD:CONV_PAD + T, :] = x_ref[...]
        _shifted_copies(xs_ref, T, rc)
        off = CONV_PAD - (CONV_WIDTH - 1)
        for ci in range(nchunk):
            t0 = ci * rc
            acc = jnp.broadcast_to(b_ref[...], (rc, tc))
            for j in range(CONV_WIDTH):
                acc = acc + w_ref[j:j + 1, :] * _shifted(xs_ref, t0 + off + j, rc)
            y_ref[t0:t0 + rc, :] = acc

    return pl.pallas_call(
        body, name=name, grid=(C // tc,),
        in_specs=[pl.BlockSpec((T, tc), lambda j: (0, j)), pl.BlockSpec((32, tc), lambda j: (0, j)),
                  pl.BlockSpec((1, tc), lambda j: (0, j))] + [pl.BlockSpec(memory_space=pl.ANY)] * len(deps),
        out_specs=pl.BlockSpec((T, tc), lambda j: (0, j)),
        out_shape=jax.ShapeDtypeStruct((T, C), F32),
        scratch_shapes=[pltpu.VMEM((8, T + CONV_PAD, tc), F32)],
        compiler_params=_cparams(("parallel",)),
    )(x, w, b, *deps)


def dwconv_bwd(x, dy, w, *, tc, rc, name):
    T, C = x.shape
    nchunk = T // rc
    assert nchunk * rc == T and C % tc == 0 and rc % 8 == 0

    def body(x_ref, dy_ref, w_ref, dx_ref, dw_ref, xs_ref, ds_ref):
        off = CONV_PAD - (CONV_WIDTH - 1)
        xs_ref[0, 0:CONV_PAD, :] = jnp.zeros((CONV_PAD, tc), F32)
        xs_ref[0, CONV_PAD:CONV_PAD + T, :] = x_ref[...]
        _shifted_copies(xs_ref, T, rc)
        ds_ref[0, 0:T, :] = dy_ref[...]
        ds_ref[0, T:T + CONV_PAD, :] = jnp.zeros((CONV_PAD, tc), F32)
        _shifted_copies(ds_ref, T, rc)
        for ci in range(nchunk):
            t0 = ci * rc
            acc = jnp.zeros((rc, tc), F32)
            for j in range(CONV_WIDTH):
                acc = acc + w_ref[j:j + 1, :] * _shifted(ds_ref, t0 + (CONV_WIDTH - 1) - j, rc)
            dx_ref[t0:t0 + rc, :] = acc
        rd = rc // 2 if rc % 16 == 0 else rc
        accs = [jnp.zeros((8, tc), F32) for _ in range(CONV_WIDTH)]
        for t0 in range(0, T, rd):
            dyc = dy_ref[t0:t0 + rd, :]
            for j in range(CONV_WIDTH):
                pr = dyc * _shifted(xs_ref, t0 + off + j, rd)
                accs[j] = accs[j] + jnp.sum(pr.reshape(rd // 8, 8, tc), axis=0)
        dw_ref[...] = jnp.zeros((32, tc), F32)
        for j in range(CONV_WIDTH):
            dw_ref[j:j + 1, :] = jnp.sum(accs[j], axis=0, keepdims=True)

    return pl.pallas_call(
        body, name=name, grid=(C // tc,),
        in_specs=[pl.BlockSpec((T, tc), lambda j: (0, j)), pl.BlockSpec((T, tc), lambda j: (0, j)),
                  pl.BlockSpec((32, tc), lambda j: (0, j))],
        out_specs=[pl.BlockSpec((T, tc), lambda j: (0, j)), pl.BlockSpec((32, tc), lambda j: (0, j))],
        out_shape=[jax.ShapeDtypeStruct((T, C), F32), jax.ShapeDtypeStruct((32, C), F32)],
        scratch_shapes=[pltpu.VMEM((8, T + CONV_PAD, tc), F32), pltpu.VMEM((8, T + CONV_PAD, tc), F32)],
        compiler_params=_cparams(("parallel",)),
    )(x, dy, w)


def _heads_per_step(H):
    return 4 if H % 4 == 0 else 1


def q_proj(cq, wq, cos, sin, *, scale, name):
    T, R = cq.shape
    H = wq.shape[1] // HEAD_PAD
    hb = _heads_per_step(H)

    def body(a_ref, b_ref, c_ref, s_ref, o_ref):
        p = jnp.dot(a_ref[...], b_ref[...], preferred_element_type=F32)
        for h in range(hb):
            o = h * HEAD_PAD
            o_ref[:, o:o + QK_NOPE] = (p[:, o:o + QK_NOPE] * scale).astype(BF)
            x = p[:, o + QK_NOPE:o + HEAD_PAD]
            o_ref[:, o + QK_NOPE:o + HEAD_PAD] = ((x * c_ref[...] + _rot_half(x) * s_ref[...]) * scale).astype(BF)

    return pl.pallas_call(
        body, name=name, grid=(H // hb,),
        in_specs=[pl.BlockSpec((T, R), lambda h: (0, 0)), pl.BlockSpec((R, hb * HEAD_PAD), lambda h: (0, h)),
                  pl.BlockSpec((T, 128), lambda h: (0, 0)), pl.BlockSpec((T, 128), lambda h: (0, 0))],
        out_specs=pl.BlockSpec((T, hb * HEAD_PAD), lambda h: (0, h)),
        out_shape=jax.ShapeDtypeStruct((T, H * HEAD_PAD), BF),
        compiler_params=_cparams(("parallel",)),
    )(cq, wq, cos, sin)


def k_proj(ckv, wk, kr, *, name):
    T, R = ckv.shape
    H = wk.shape[1] // QK_NOPE
    hb = _heads_per_step(H)

    def body(a_ref, b_ref, kr_ref, o_ref):
        p = jnp.dot(a_ref[...], b_ref[...], preferred_element_type=F32)
        krb = kr_ref[...].astype(BF)
        for h in range(hb):
            o_ref[:, h * HEAD_PAD:h * HEAD_PAD + QK_NOPE] = p[:, h * QK_NOPE:(h + 1) * QK_NOPE].astype(BF)
            o_ref[:, h * HEAD_PAD + QK_NOPE:(h + 1) * HEAD_PAD] = krb

    return pl.pallas_call(
        body, name=name, grid=(H // hb,),
        in_specs=[pl.BlockSpec((T, R), lambda h: (0, 0)), pl.BlockSpec((R, hb * QK_NOPE), lambda h: (0, h)),
                  pl.BlockSpec((T, 128), lambda h: (0, 0))],
        out_specs=pl.BlockSpec((T, hb * HEAD_PAD), lambda h: (0, h)),
        out_shape=jax.ShapeDtypeStruct((T, H * HEAD_PAD), BF),
        compiler_params=_cparams(("parallel",)),
    )(ckv, wk, kr)


def _causal_scores(q, k_ref, qi, CH, split):
    nt = (((1,), (1,)), ((), ()))
    pieces = []
    if not split:
        rows = slice(0, (qi + 1) * CH)
        s = lax.dot_general(q, k_ref[rows, :], nt, preferred_element_type=F32)
        keep = lax.broadcasted_iota(jnp.int32, s.shape, 1) <= lax.broadcasted_iota(jnp.int32, s.shape, 0) + qi * CH
        return [(rows, jnp.where(keep, s, -1e30))]
    if qi > 0:
        rows = slice(0, qi * CH)
        pieces.append((rows, lax.dot_general(q, k_ref[rows, :], nt, preferred_element_type=F32)))
    rows = slice(qi * CH, (qi + 1) * CH)
    s = lax.dot_general(q, k_ref[rows, :], nt, preferred_element_type=F32)
    keep = lax.broadcasted_iota(jnp.int32, s.shape, 1) <= lax.broadcasted_iota(jnp.int32, s.shape, 0)
    pieces.append((rows, jnp.where(keep, s, -1e30)))
    return pieces


def attn_fwd(q, k, v, *, nch, name):
    T = q.shape[0]
    H = q.shape[1] // HEAD_PAD
    CH = T // nch
    assert CH * nch == T and CH % 16 == 0

    def body(q_ref, k_ref, v_ref, o_ref, lse_ref):
        for qi in range(nch):
            L = (qi + 1) * CH
            pieces = _causal_scores(q_ref[qi * CH:L, :], k_ref, qi, CH, True)
            m = functools.reduce(jnp.maximum, [jnp.max(s, axis=1, keepdims=True) for _, s in pieces])
            l, o = 0.0, 0.0
            for rows, s in pieces:
                p = jnp.exp(s - m)
                l = l + jnp.sum(p, axis=1, keepdims=True)
                o = o + jnp.dot(p.astype(BF), v_ref[rows, :], preferred_element_type=F32)
            o_ref[qi * CH:L, :] = (o / l).astype(BF)
            lse_ref[qi * CH:L, :] = jnp.broadcast_to(m + jnp.log(l), (CH, V_HEAD))

    return pl.pallas_call(
        body, name=name, grid=(H,),
        in_specs=[pl.BlockSpec((T, HEAD_PAD), lambda h: (0, h)), pl.BlockSpec((T, HEAD_PAD), lambda h: (0, h)),
                  pl.BlockSpec((T, V_HEAD), lambda h: (0, h))],
        out_specs=[pl.BlockSpec((T, V_HEAD), lambda h: (0, h)), pl.BlockSpec((T, V_HEAD), lambda h: (0, h))],
        out_shape=[jax.ShapeDtypeStruct((T, H * V_HEAD), BF), jax.ShapeDtypeStruct((T, H * V_HEAD), F32)],
        compiler_params=_cparams(("parallel",)),
    )(q, k, v)


def attn_bwd(q, k, v, o, do, lse, cos, sin, *, scale, nch, name):
    T = q.shape[0]
    H = q.shape[1] // HEAD_PAD
    CH = T // nch
    tn_dims = (((0,), (0,)), ((), ()))
    nt_dims = (((1,), (1,)), ((), ()))

    def body(q_ref, k_ref, v_ref, o_ref, do_ref, lse_ref, c_ref, s_ref,
             dq_ref, dk_ref, dv_ref, dkr_ref, dk_acc, dv_acc):
        h = pl.program_id(0)
        dk_acc[...] = jnp.zeros_like(dk_acc)
        dv_acc[...] = jnp.zeros_like(dv_acc)
        for qi in range(nch):
            L = (qi + 1) * CH
            rows = slice(qi * CH, L)
            qc, doc = q_ref[rows, :], do_ref[rows, :]
            lse_c = lse_ref[rows, 0:1]
            dsum = jnp.sum(doc.astype(F32) * o_ref[rows, :].astype(F32), axis=1, keepdims=True)
            dq = 0.0
            for krows, s in _causal_scores(qc, k_ref, qi, CH, False):
                p = jnp.exp(s - lse_c)
                dp = lax.dot_general(doc, v_ref[krows, :], nt_dims, preferred_element_type=F32)
                ds = (p * (dp - dsum)).astype(BF)
                dq = dq + jnp.dot(ds, k_ref[krows, :], preferred_element_type=F32)
                dk_acc[krows, :] += lax.dot_general(ds, qc, tn_dims, preferred_element_type=F32)
                dv_acc[krows, :] += lax.dot_general(p.astype(BF), doc, tn_dims, preferred_element_type=F32)
            dq_ref[rows, :QK_NOPE] = (dq[:, :QK_NOPE] * scale).astype(BF)
            dyr = dq[:, QK_NOPE:]
            dq_ref[rows, QK_NOPE:] = ((dyr * c_ref[rows, :] - _rot_half(dyr * s_ref[rows, :])) * scale).astype(BF)
        dk_ref[:, :QK_NOPE] = dk_acc[:, :QK_NOPE].astype(BF)
        dk_ref[:, QK_NOPE:] = jnp.zeros((T, HEAD_PAD - QK_NOPE), BF)
        dv_ref[...] = dv_acc[...].astype(BF)

        @pl.when(h == 0)
        def _():
            dkr_ref[...] = dk_acc[:, QK_NOPE:]

        @pl.when(h > 0)
        def _():
            dkr_ref[...] += dk_acc[:, QK_NOPE:]

    hp = pl.BlockSpec((T, HEAD_PAD), lambda h: (0, h))
    hv = pl.BlockSpec((T, V_HEAD), lambda h: (0, h))
    tab = pl.BlockSpec((T, 128), lambda h: (0, 0))
    return pl.pallas_call(
        body, name=name, grid=(H,),
        in_specs=[hp, hp, hv, hv, hv, hv, tab, tab],
        out_specs=[hp, hp, hv, tab],
        out_shape=[jax.ShapeDtypeStruct((T, H * HEAD_PAD), BF), jax.ShapeDtypeStruct((T, H * HEAD_PAD), BF),
                   jax.ShapeDtypeStruct((T, H * V_HEAD), BF), jax.ShapeDtypeStruct((T, 128), F32)],
        scratch_shapes=[pltpu.VMEM((T, HEAD_PAD), F32), pltpu.VMEM((T, V_HEAD), F32)],
        compiler_params=_cparams(("arbitrary",)),
    )(q, k, v, o, do, lse, cos, sin)


def _place():
    x, y, c = lax.axis_index("x"), lax.axis_index("y"), lax.axis_index("c")
    return x, y, c


def _other_chips(x, y):
    return [(1 - x, y), (x, 1 - y), (1 - x, 1 - y)]


HBM_SPEC = pl.BlockSpec(memory_space=pltpu.HBM)
SEM_SPEC = pl.BlockSpec(memory_space=pltpu.SEMAPHORE)
DATAFLOW = pltpu.SideEffectType.DATAFLOW_SIDE_EFFECTING


def _hbm(a):
    return pltpu.with_memory_space_constraint(a, pltpu.HBM)


def _half(ref, lead, c, axis):
    rows, cols = ref.shape[-2], ref.shape[-1]
    if axis == 0:
        return ref.at[(*lead, pl.ds(c * (rows // 2), rows // 2))]
    return ref.at[(*lead, slice(None), pl.ds(c * (cols // 2), cols // 2))]


def copies_start(arrs, plan, n_sems, after, *, name):
    n = len(arrs)

    def body(*refs):
        outs = refs[n + 1:2 * n + 1]
        send_sems, recv_sems, token = refs[2 * n + 1:]
        for cp in plan(outs, send_sems, recv_sems, False):
            cp.start()
        token[...] = jnp.zeros_like(token)

    res = pl.pallas_call(
        body, name=name,
        out_shape=tuple(pltpu.HBM(a.shape, a.dtype) for a in arrs)
        + (pltpu.SemaphoreType.DMA((n_sems,)), pltpu.SemaphoreType.DMA((n_sems,)), jax.ShapeDtypeStruct((8, 128), F32)),
        in_specs=(HBM_SPEC,) * n + (pl.BlockSpec(memory_space=pl.ANY),),
        out_specs=(HBM_SPEC,) * n + (SEM_SPEC, SEM_SPEC, pl.BlockSpec(memory_space=pltpu.VMEM)),
        input_output_aliases={i: i for i in range(n)},
        compiler_params=pltpu.CompilerParams(has_side_effects=DATAFLOW),
    )(*[_hbm(a) for a in arrs], after)
    return list(res[:n]), res[n], res[n + 1], res[n + 2]


def copies_wait(arrs, plan, send_sems, recv_sems, after, *, name):
    n = len(arrs)
    afters = list(after) if isinstance(after, (list, tuple)) else [after]

    def body(*refs):
        for cp in plan(refs[:n], refs[n], refs[n + 1], True):
            cp.wait_send()
            cp.wait_recv()

    res = pl.pallas_call(
        body, name=name,
        out_shape=tuple(pltpu.HBM(a.shape, a.dtype) for a in arrs),
        in_specs=(HBM_SPEC,) * n + (SEM_SPEC, SEM_SPEC) + (pl.BlockSpec(memory_space=pl.ANY),) * len(afters),
        out_specs=(HBM_SPEC,) * n,
        input_output_aliases={i: i for i in range(n)},
        compiler_params=pltpu.CompilerParams(has_side_effects=DATAFLOW),
    )(*arrs, send_sems, recv_sems, *afters)
    return list(res)


def _half_part(ref, lead, c, axis, part):
    rows, cols = ref.shape[-2], ref.shape[-1]
    span = rows // 2 if axis == 0 else rows
    p1 = -(-(span // 2) // 16) * 16
    lo, n = (0, p1) if part == 0 else (p1, span - p1)
    if axis == 0:
        return ref.at[(*lead, pl.ds(c * span + lo, n))]
    return ref.at[(*lead, pl.ds(lo, n), pl.ds(c * (cols // 2), cols // 2))]


def near_plan(axes):
    def plan(bufs, send_sems, recv_sems, receiving):
        x, y, c = _place()
        me = 2 * x + y
        out = []
        for i, b in enumerate(bufs):
            for j, (cx, cy) in enumerate(_other_chips(x, y)[:2]):
                src = _half(b, (me,), c, axes[i])
                dst = _half(b, (2 * cx + cy,), c, axes[i]) if receiving else src
                out.append(pltpu.make_async_remote_copy(
                    src_ref=src, dst_ref=dst, send_sem=send_sems.at[2 * i + j], recv_sem=recv_sems.at[2 * i + j],
                    device_id=(cx, cy, c), device_id_type=MESH))
        return out
    return plan


def relay_plan(axes):
    def plan(bufs, send_sems, recv_sems, receiving):
        x, y, c = _place()
        (xn, yn, dg) = [2 * cx + cy for cx, cy in _other_chips(x, y)]
        to = [(x, 1 - y, c), (1 - x, y, c), (x, y, 1 - c), (x, y, 1 - c)]
        out = []
        for i, b in enumerate(bufs):
            if receiving:
                blks = [_half_part(b, (dg,), c, axes[i], 0), _half_part(b, (dg,), c, axes[i], 1),
                        _half(b, (xn,), 1 - c, axes[i]), _half(b, (yn,), 1 - c, axes[i])]
            else:
                blks = [_half_part(b, (xn,), c, axes[i], 0), _half_part(b, (yn,), c, axes[i], 1),
                        _half(b, (xn,), c, axes[i]), _half(b, (yn,), c, axes[i])]
            for k in range(4):
                out.append(pltpu.make_async_remote_copy(
                    src_ref=blks[k], dst_ref=blks[k], send_sem=send_sems.at[4 * i + k],
                    recv_sem=recv_sems.at[4 * i + k], device_id=to[k], device_id_type=MESH))
        return out
    return plan


def far_forward_plan(axes):
    def plan(bufs, send_sems, recv_sems, receiving):
        x, y, c = _place()
        dg = 2 * (1 - x) + (1 - y)
        out = []
        for i, b in enumerate(bufs):
            blk = _half(b, (dg,), (1 - c) if receiving else c, axes[i])
            out.append(pltpu.make_async_remote_copy(
                src_ref=blk, dst_ref=blk, send_sem=send_sems.at[i], recv_sem=recv_sems.at[i],
                device_id=(x, y, 1 - c), device_id_type=MESH))
        return out
    return plan


def share_plan(axes):
    def plan(bufs, send_sems, recv_sems, receiving):
        x, y, c = _place()
        out = []
        for i, b in enumerate(bufs):
            blk = _half(b, (), (1 - c) if receiving else c, axes[i])
            out.append(pltpu.make_async_remote_copy(
                src_ref=blk, dst_ref=blk, send_sem=send_sems.at[i], recv_sem=recv_sems.at[i],
                device_id=(x, y, 1 - c), device_id_type=MESH))
        return out
    return plan


def scatter_plan(n):
    def plan(arrs, send_sems, recv_sems, receiving):
        x, y, c = _place()
        out = []
        for i in range(n):
            for j, (cx, cy) in enumerate(_other_chips(x, y)):
                out.append(pltpu.make_async_remote_copy(
                    src_ref=arrs[i].at[2 * cx + cy], dst_ref=arrs[n + i].at[j], send_sem=send_sems.at[3 * i + j],
                    recv_sem=recv_sems.at[3 * i + j], device_id=(cx, cy, c), device_id_type=MESH))
        return out
    return plan


def exchange_plan(axes):
    n = len(axes)

    def plan(arrs, send_sems, recv_sems, receiving):
        x, y, c = _place()
        out = []
        for i in range(n):
            for k in range(N_CHIPS):
                out.append(pltpu.make_async_remote_copy(
                    src_ref=_half(arrs[i], (k,), 1 - c, axes[i]), dst_ref=arrs[n + i].at[k],
                    send_sem=send_sems.at[N_CHIPS * i + k], recv_sem=recv_sems.at[N_CHIPS * i + k],
                    device_id=(x, y, 1 - c), device_id_type=MESH))
        return out
    return plan


def devices_plan(arrs, send_sems, recv_sems, receiving):
    x, y, c = _place()
    out = []
    for k in range(1, N_DEV):
        px, py, pc = x ^ ((k >> 2) & 1), y ^ ((k >> 1) & 1), c ^ (k & 1)
        slot = (4 * px + 2 * py + pc) if receiving else (4 * x + 2 * y + c)
        out.append(pltpu.make_async_remote_copy(
            src_ref=arrs[0], dst_ref=arrs[1].at[slot], send_sem=send_sems.at[k - 1], recv_sem=recv_sems.at[k - 1],
            device_id=(px, py, pc), device_id_type=MESH))
    return out


def _row_tile(r, cols, itemsize, target_bytes=2 * 1024 * 1024):
    best = None
    for t in range(16, r + 1, 16):
        if r % t == 0 and t * cols * itemsize <= target_bytes:
            best = t
    return best if best is not None else r


def pair_sum(g, recv, half, axis, *, name):
    _, hr, hc = recv.shape
    tr = _row_tile(hr, hc, 4)
    nb = hr // tr
    if axis == 0:
        mine = pl.BlockSpec((None, tr, hc), lambda k, r, hf: (k, hf[0] * nb + r, 0))
    else:
        mine = pl.BlockSpec((None, tr, hc), lambda k, r, hf: (k, r, hf[0]))

    def body(half_ref, a_ref, b_ref, o_ref):
        o_ref[...] = (a_ref[...].astype(F32) + b_ref[...].astype(F32)).astype(BF)

    return pl.pallas_call(
        body, name=name,
        grid_spec=pltpu.PrefetchScalarGridSpec(
            num_scalar_prefetch=1, grid=(N_CHIPS, nb),
            in_specs=[mine, pl.BlockSpec((None, tr, hc), lambda k, r, hf: (k, r, 0))],
            out_specs=pl.BlockSpec((None, tr, hc), lambda k, r, hf: (k, r, 0))),
        out_shape=jax.ShapeDtypeStruct((N_CHIPS, hr, hc), BF),
        compiler_params=_cparams(("parallel", "parallel")),
    )(half, g, recv)


def chip_sum(s, recv, place, axis, *, name):
    _, hr, hc = s.shape
    tr = _row_tile(hr, hc, 4)
    nb = hr // tr
    if axis == 0:
        o_spec = pl.BlockSpec((tr, hc), lambda r, pc: (pc[1] * nb + r, 0))
        o_shape = (2 * hr, hc)
    else:
        o_spec = pl.BlockSpec((tr, hc), lambda r, pc: (r, pc[1]))
        o_shape = (hr, 2 * hc)

    def body(place_ref, a_ref, b_ref, o_ref):
        acc = a_ref[...].astype(F32)
        for j in range(3):
            acc = acc + b_ref[j].astype(F32)
        o_ref[...] = acc

    return pl.pallas_call(
        body, name=name,
        grid_spec=pltpu.PrefetchScalarGridSpec(
            num_scalar_prefetch=1, grid=(nb,),
            in_specs=[pl.BlockSpec((None, tr, hc), lambda r, pc: (pc[0], r, 0)),
                      pl.BlockSpec((3, tr, hc), lambda r, pc: (0, r, 0))],
            out_specs=o_spec),
        out_shape=jax.ShapeDtypeStruct(o_shape, F32),
        compiler_params=_cparams(("parallel",)),
    )(place, s, recv)


def sum_devices(v, *, name):
    _, R, C = v.shape

    def body(v_ref, o_ref):
        acc = v_ref[0]
        for k in range(1, N_DEV):
            acc = acc + v_ref[k]
        o_ref[...] = acc

    return pl.pallas_call(body, name=name, out_shape=jax.ShapeDtypeStruct((R, C), F32))(v)


def adamw(w, g, m, v, *, name):
    R, C = w.shape
    tr = _row_tile(R, C, 4)
    c1 = 1.0 / (1.0 - ADAM_B1 ** ADAM_STEP)
    c2 = 1.0 / (1.0 - ADAM_B2 ** ADAM_STEP)

    def body(w_ref, g_ref, m_ref, v_ref, go_ref, d_ref, nm_ref, nv_ref):
        gg = g_ref[...]
        go_ref[...] = gg
        nm = ADAM_B1 * m_ref[...] + (1.0 - ADAM_B1) * gg
        nv = ADAM_B2 * v_ref[...] + (1.0 - ADAM_B2) * (gg * gg)
        nm_ref[...] = nm
        nv_ref[...] = nv
        d_ref[...] = -ADAM_LR * ((nm * c1) / (jnp.sqrt(nv * c2) + ADAM_EPS) + ADAM_WD * w_ref[...])

    spec = pl.BlockSpec((tr, C), lambda i: (i, 0))
    return pl.pallas_call(
        body, name=name, grid=(R // tr,), in_specs=[spec] * 4, out_specs=[spec] * 4,
        out_shape=[jax.ShapeDtypeStruct((R, C), F32)] * 4, compiler_params=_cparams(("parallel",)),
    )(w, g, m, v)


def _rope_tables(T):
    pos = np.arange(T, dtype=np.float32)
    inv_freq = (ROPE_THETA ** (-np.arange(0, QK_ROPE, 2, dtype=np.float32) / QK_ROPE)).astype(np.float32)
    ang = pos[:, None] * inv_freq[None, :]
    z = np.zeros((T, 64), np.float32)
    cos = np.concatenate([np.cos(ang), np.cos(ang), z], axis=1).astype(np.float32)
    sin = np.concatenate([np.sin(ang), np.sin(ang), z], axis=1).astype(np.float32)
    return jnp.asarray(cos), jnp.asarray(sin)


def _padded_rows(length):
    T = -(-length // 16) * 16
    while True:
        for n in (4, 5, 6, 7, 8):
            if T % (16 * n) == 0:
                return T, n
        T += 16


def _pick(n, cands):
    for c in cands:
        if n % c == 0:
            return c
    return n


def kernel(x, meta_tokens, g_mix, w_in, b_glu, b_gate, w_dw, b_dw, g_conv_ln, b_conv_ln, w_conv_out, b_conv_out, g_q_lora, w_uq, g_kv_lora, w_uk, w_uv, w_attn_out, w_out, g_ffn, w_ffn_gate, w_ffn_up, w_ffn_down, g_final, loss_target, m_meta_tokens, m_g_mix, m_w_in, m_b_glu, m_b_gate, m_w_dw, m_b_dw, m_g_conv_ln, m_b_conv_ln, m_w_conv_out, m_b_conv_out, m_g_q_lora, m_w_uq, m_g_kv_lora, m_w_uk, m_w_uv, m_w_attn_out, m_w_out, m_g_ffn, m_w_ffn_gate, m_w_ffn_up, m_w_ffn_down, m_g_final, v_meta_tokens, v_g_mix, v_w_in, v_b_glu, v_b_gate, v_w_dw, v_b_dw, v_g_conv_ln, v_b_conv_ln, v_w_conv_out, v_b_conv_out, v_g_q_lora, v_w_uq, v_g_kv_lora, v_w_uk, v_w_uv, v_w_attn_out, v_w_out, v_g_ffn, v_w_ffn_gate, v_w_ffn_up, v_w_ffn_down, v_g_final):
    weights = dict(meta_tokens=meta_tokens, g_mix=g_mix, w_in=w_in, b_glu=b_glu, b_gate=b_gate, w_dw=w_dw, b_dw=b_dw,
                   g_conv_ln=g_conv_ln, b_conv_ln=b_conv_ln, w_conv_out=w_conv_out, b_conv_out=b_conv_out,
                   g_q_lora=g_q_lora, w_uq=w_uq, g_kv_lora=g_kv_lora, w_uk=w_uk, w_uv=w_uv, w_attn_out=w_attn_out,
                   w_out=w_out, g_ffn=g_ffn, w_ffn_gate=w_ffn_gate, w_ffn_up=w_ffn_up, w_ffn_down=w_ffn_down,
                   g_final=g_final)
    m_in = dict(meta_tokens=m_meta_tokens, g_mix=m_g_mix, w_in=m_w_in, b_glu=m_b_glu, b_gate=m_b_gate, w_dw=m_w_dw,
                b_dw=m_b_dw, g_conv_ln=m_g_conv_ln, b_conv_ln=m_b_conv_ln, w_conv_out=m_w_conv_out,
                b_conv_out=m_b_conv_out, g_q_lora=m_g_q_lora, w_uq=m_w_uq, g_kv_lora=m_g_kv_lora, w_uk=m_w_uk,
                w_uv=m_w_uv, w_attn_out=m_w_attn_out, w_out=m_w_out, g_ffn=m_g_ffn, w_ffn_gate=m_w_ffn_gate,
                w_ffn_up=m_w_ffn_up, w_ffn_down=m_w_ffn_down, g_final=m_g_final)
    v_in = dict(meta_tokens=v_meta_tokens, g_mix=v_g_mix, w_in=v_w_in, b_glu=v_b_glu, b_gate=v_b_gate, w_dw=v_w_dw,
                b_dw=v_b_dw, g_conv_ln=v_g_conv_ln, b_conv_ln=v_b_conv_ln, w_conv_out=v_w_conv_out,
                b_conv_out=v_b_conv_out, g_q_lora=v_g_q_lora, w_uq=v_w_uq, g_kv_lora=v_g_kv_lora, w_uk=v_w_uk,
                w_uv=v_w_uv, w_attn_out=v_w_attn_out, w_out=v_w_out, g_ffn=v_g_ffn, w_ffn_gate=v_w_ffn_gate,
                w_ffn_up=v_w_ffn_up, w_ffn_down=v_w_ffn_down, g_final=v_g_final)
    names = list(weights)

    seq, D = x.shape[1], x.shape[2]
    C = w_conv_out.shape[2]
    R = g_q_lora.shape[1]
    H = w_uv.shape[2] * N_CHIPS // V_HEAD
    FB = w_ffn_gate.shape[2]
    FF = FB * N_CHIPS
    length = N_META + seq
    T, nch = _padded_rows(length)
    unit = T // nch
    scale = (QK_NOPE + QK_ROPE) ** -0.5
    assert C == D and w_in.shape[2] * N_CHIPS == 2 * C + 2 * R + QK_ROPE + 2 * D

    xi, yi, ci = _place()
    chip = (2 * xi + yi).astype(jnp.int32)
    half_arr = ci.astype(jnp.int32).reshape(1)

    tr = unit
    tmh = T // 2 if (T // 2) % 16 == 0 else unit
    tmq = unit
    tmw = _pick(D, (1024, 512))
    rc = unit // 2 if unit % 16 == 0 else unit
    tcv = 128

    def tn_of(n, pref=512):
        return _pick(n, (pref, 384, 256, 128))

    grp_a = ["w_in"]
    grp_b = ["w_conv_out", "w_uq", "w_uk", "w_uv", "w_attn_out", "w_out"]
    grp_c = ["w_ffn_gate", "w_ffn_up", "w_ffn_down"]
    big = grp_a + grp_b + grp_c
    axes = {nm: 0 for nm in big}
    axes["w_in"] = 1
    shard = {nm: weights[nm][0] for nm in big}
    shard["w_in"] = jnp.swapaxes(w_in[0], 0, 1)

    def own_blocks(srcs, dtype=BF):
        bufs = []
        for w in srcs:
            wb = w.astype(dtype)
            bufs.append(lax.dynamic_update_slice(lax.empty((N_CHIPS,) + wb.shape, dtype), wb[None], (chip, 0, 0)))
        return bufs

    tiny = jnp.concatenate([meta_tokens, jnp.pad(w_dw[0], ((0, 1), (0, 0)))], axis=0)
    ax_a = [axes[nm] for nm in grp_a] + [0]
    bufs_a, ss_a, rs_a, tok_ga = copies_start(own_blocks([shard[nm] for nm in grp_a]) + own_blocks([tiny], F32),
                                              near_plan(ax_a), 2 * len(ax_a), half_arr, name="gather_a_near_start")
    tok_ga, held = lax.optimization_barrier((tok_ga, (x, loss_target, [shard[nm] for nm in grp_b + grp_c])))
    x_l, tgt_l, shard_bc = held
    h0 = jnp.pad(x_l[0], ((N_META, T - length), (0, 0)))
    tgt = jnp.pad(tgt_l[0], ((N_META, T - length), (0, 0)))
    bufs_b, bufs_c = own_blocks(shard_bc[:len(grp_b)]), own_blocks(shard_bc[len(grp_b):])
    cos, sin = _rope_tables(T)

    bufs_a = copies_wait(bufs_a, near_plan(ax_a), ss_a, rs_a, [h0, tgt] + bufs_b + bufs_c, name="gather_a_near_wait")
    bufs_a, ss_a, rs_a, _ = copies_start(bufs_a, relay_plan(ax_a), 4 * len(ax_a), half_arr,
                                         name="gather_a_relay_start")
    bufs_a = copies_wait(bufs_a, relay_plan(ax_a), ss_a, rs_a, half_arr, name="gather_a_relay_wait")
    bufs_a, ss_a, rs_a, _ = copies_start(bufs_a, far_forward_plan(ax_a), len(ax_a), half_arr,
                                         name="gather_a_far_start")
    bufs_a = copies_wait(bufs_a, far_forward_plan(ax_a), ss_a, rs_a, half_arr, name="gather_a_far_wait")
    G = dict(zip(grp_a, bufs_a))
    meta_full = jnp.concatenate([bufs_a[-1][k, :N_META] for k in range(N_CHIPS)], axis=1)
    w_dw_full = jnp.concatenate([bufs_a[-1][k, N_META:] for k in range(N_CHIPS)], axis=1)
    h0 = lax.dynamic_update_slice(h0, meta_full, (0, 0))
    ax_b = [axes[nm] for nm in grp_b]
    bufs_b, ss_b, rs_b, tok_gb = copies_start(bufs_b, near_plan(ax_b), 2 * len(grp_b), G["w_in"],
                                              name="gather_b_near_start")

    WT_in = G["w_in"].reshape(N_CHIPS * G["w_in"].shape[1], D)
    o1, o2 = 2 * C, 2 * C + 2 * R + QK_ROPE
    WT_small = jnp.pad(WT_in[o1:o2], ((0, 128 - QK_ROPE), (0, 0)))
    WT_gate = WT_in[o2:]

    u = rms_fwd(h0, g_mix, tr=tr, name="rms_mix", deps=[tok_gb])
    z_glu, c0 = glu_proj(u, WT_in, b_glu, C, tn=_pick(C, (256, 128)), name="glu_proj")
    z_small = mm(u, WT_small, mode="nt", tm=T, tn=2 * R + 128, tk=D, out_dtype=F32, name="mm_z_small")
    z_gate = mm(u, WT_gate, mode="nt", tm=T, tn=tn_of(2 * D, 1024), tk=D, out_dtype=BF, name="mm_z_gate")

    bufs_b = copies_wait(bufs_b, near_plan(ax_b), ss_b, rs_b, z_glu, name="gather_b_near_wait")
    bufs_b, ss_b, rs_b, tok_gb = copies_start(bufs_b, relay_plan(ax_b), 4 * len(grp_b), half_arr,
                                              name="gather_b_relay_start")
    ax_c = [axes[nm] for nm in grp_c]
    bufs_c, ss_c, rs_c, tok_gc = copies_start(bufs_c, near_plan(ax_c), 2 * len(grp_c), tok_gb,
                                              name="gather_c_near_start")
    c1 = dwconv_fwd(c0, w_dw_full, b_dw, tc=tcv, rc=rc, name="dwconv_fwd", deps=[tok_gc])
    bufs_b = copies_wait(bufs_b, relay_plan(ax_b), ss_b, rs_b, c1, name="gather_b_relay_wait")
    bufs_b, ss_b, rs_b, tok_gb = copies_start(bufs_b, far_forward_plan(ax_b), len(grp_b), half_arr,
                                              name="gather_b_far_start")
    c3 = ln_silu_fwd(c1, g_conv_ln, b_conv_ln, tr=tr, name="ln_silu_fwd")
    cq, ckv, kr = lora_norm_fwd(z_small, g_q_lora, g_kv_lora, cos, sin, tr=tr, name="lora_norm_fwd", deps=[tok_gb])

    def cols_full(a):
        return jnp.concatenate([a[k] for k in range(N_CHIPS)], axis=1)

    G.update(zip(grp_b, copies_wait(bufs_b, far_forward_plan(ax_b), ss_b, rs_b, cq, name="gather_b_far_wait")))
    W_co = G["w_conv_out"].reshape(C, D)
    W_uq = cols_full(G["w_uq"]).reshape(R, H, QK_NOPE + QK_ROPE)
    Wq_p = jnp.pad(W_uq, ((0, 0), (0, 0), (0, HEAD_PAD - QK_NOPE - QK_ROPE))).reshape(R, H * HEAD_PAD)
    W_uk = cols_full(G["w_uk"])
    W_uv = cols_full(G["w_uv"])
    W_ao = G["w_attn_out"].reshape(H * V_HEAD, D)
    W_o = G["w_out"].reshape(D, D)

    y_conv = mm(c3, W_co, mode="nn", tm=T, tn=tn_of(D, 1024), tk=C, out_dtype=BF, name="mm_y_conv")

    q = q_proj(cq, Wq_p, cos, sin, scale=scale, name="q_proj")
    kk = k_proj(ckv, W_uk, kr, name="k_proj")
    vv = mm(ckv, W_uv, mode="nn", tm=T, tn=tn_of(H * V_HEAD), tk=R, out_dtype=BF, name="mm_v")
    o_attn, lse = attn_fwd(q, kk, vv, nch=nch, name="attn_fwd")
    bufs_c = copies_wait(bufs_c, near_plan(ax_c), ss_c, rs_c, o_attn, name="gather_c_near_wait")
    bufs_c, ss_c, rs_c, tok_gc = copies_start(bufs_c, relay_plan(ax_c), 4 * len(grp_c), half_arr,
                                              name="gather_c_relay_start")
    y_attn = mm(o_attn, W_ao, mode="nn", tm=T, tn=tn_of(D, 1024), tk=H * V_HEAD, out_dtype=BF, name="mm_y_attn",
                deps=[tok_gc])

    mix = mix_fwd(z_gate, y_conv, y_attn, b_gate, b_conv_out, tr=tr, name="mix_fwd")
    bufs_c = copies_wait(bufs_c, relay_plan(ax_c), ss_c, rs_c, mix, name="gather_c_relay_wait")
    bufs_c, ss_c, rs_c, tok_gc = copies_start(bufs_c, far_forward_plan(ax_c), len(grp_c), half_arr,
                                              name="gather_c_far_start")
    h1 = mm(mix, W_o, mode="nn", tm=T, tn=tn_of(D), tk=D, out_dtype=F32, name="mm_h1", res=h0, deps=[tok_gc])

    hn = rms_fwd(h1, g_ffn, tr=tr, name="rms_ffn")

    G.update(zip(grp_c, copies_wait(bufs_c, far_forward_plan(ax_c), ss_c, rs_c, hn, name="gather_c_far_wait")))
    W_fg, W_fu = G["w_ffn_gate"], G["w_ffn_up"]
    W_fd = G["w_ffn_down"].reshape(FF, D)

    fa, fb, f = ffn_fwd(hn, W_fg, W_fu, tm=tmq, name="ffn_fwd")
    h2 = mm(f, W_fd, mode="nn", tm=tmh, tn=tn_of(D), tk=FF, out_dtype=F32, name="mm_h2", res=h1)

    place_arr = jnp.stack([chip, ci.astype(jnp.int32)])

    def col_blocks(a):
        r, cfull = a.shape
        return a.reshape(r, N_CHIPS, cfull // N_CHIPS).transpose(1, 0, 2)

    def rs_begin(nms, gls, after, tag):
        ax = [axes[nm] for nm in nms]
        lands = []
        for g, a in zip(gls, ax):
            _, r, cfull = g.shape
            lands.append(lax.empty((N_CHIPS, r // 2, cfull) if a == 0 else (N_CHIPS, r, cfull // 2), BF))
        arrs, ssem, rsem, tok = copies_start(list(gls) + lands, exchange_plan(ax), N_CHIPS * len(nms), after,
                                             name=f"rs_{tag}_exchange_start")
        return (nms, ax, arrs, ssem, rsem), tok

    def rs_middle(state, after, tag):
        nms, ax, arrs, ssem, rsem = state
        n = len(nms)
        arrs = copies_wait(arrs, exchange_plan(ax), ssem, rsem, after, name=f"rs_{tag}_exchange_wait")
        pair = [pair_sum(g, r, half_arr, a, name="rs_pair_sum_" + nm)
                for nm, g, r, a in zip(nms, arrs[:n], arrs[n:], ax)]
        lands = [lax.empty((3,) + p.shape[1:], BF) for p in pair]
        arrs, ssem, rsem, tok = copies_start(pair + lands, scatter_plan(n), 3 * n, half_arr,
                                             name=f"rs_{tag}_scatter_start")
        return (nms, ax, arrs, ssem, rsem), tok

    def rs_end(state, after, tag):
        nms, ax, arrs, ssem, rsem = state
        n = len(nms)
        arrs = copies_wait(arrs, scatter_plan(n), ssem, rsem, after, name=f"rs_{tag}_scatter_wait")
        return [chip_sum(s, r, place_arr, a, name="rs_chip_sum_" + nm)
                for nm, s, r, a in zip(nms, arrs[:n], arrs[n:], ax)]

    def rs_shared(nms, reds, after, tag):
        ax = [axes[nm] for nm in nms]
        reds, ssem, rsem, _ = copies_start(reds, share_plan(ax), len(nms), after, name=f"rs_{tag}_share_start")
        return dict(zip(nms, copies_wait(reds, share_plan(ax), ssem, rsem, half_arr, name=f"rs_{tag}_share_wait")))

    d_h2, d_h2b, loss_p, d_g_final = final_loss(h2, tgt, g_final.reshape(1, D), seq=seq, tr=tr, name="final_loss")
    loss = lax.psum(loss_p[0, 0], ("x", "y", "c"))

    dW_fd = mm(f, d_h2b, mode="tn", tm=FB, tn=tn_of(D, 1024), tk=T, out_dtype=BF, name="mm_dw_ffn_down")
    d_a, d_b = ffn_dact(d_h2b, W_fd, fa, fb, tm=tmq, name="ffn_dact")
    dW_fg = mm(hn, d_a, mode="tn", tm=tmw, tn=FB, tk=T, out_dtype=BF, name="mm_dw_ffn_gate", out_blocks=N_CHIPS)
    dW_fu = mm(hn, d_b, mode="tn", tm=tmw, tn=FB, tk=T, out_dtype=BF, name="mm_dw_ffn_up", out_blocks=N_CHIPS)
    st_c, tok_rc = rs_begin(grp_c, [dW_fg, dW_fu, dW_fd.reshape(N_CHIPS, FB, D)], half_arr, "c")
    d_hn = ffn_dhn(d_a, d_b, W_fg, W_fu, tm=tmq, tn=_pick(D, (512, 256)), name="ffn_dhn", deps=[tok_rc])
    st_c, tok_rc = rs_middle(st_c, d_hn, "c")
    d_h1, d_h1b, d_g_ffn = rms_bwd(d_hn, h1, g_ffn, d_h2, tr=tr, name="rms_ffn_bwd", deps=[tok_rc])

    d_mix = mm(d_h1b, W_o, mode="nt", tm=T, tn=tn_of(D, 1024), tk=D, out_dtype=BF, name="mm_d_mix")
    dW_o = mm(mix, d_h1b, mode="tn", tm=tmw, tn=tn_of(D, 1024), tk=T, out_dtype=BF, name="mm_dw_out")
    d_yc, d_ya, d_zgate, d_b_gate, d_b_co = mix_bwd(z_gate, y_conv, y_attn, d_mix, b_gate, b_conv_out, tr=tr,
                                                    name="mix_bwd")

    dW_co = mm(c3, d_yc, mode="tn", tm=tmw, tn=tn_of(D, 1024), tk=T, out_dtype=BF, name="mm_dw_conv_out")
    d_c3 = mm(d_yc, W_co, mode="nt", tm=T, tn=tn_of(C, 1024), tk=D, out_dtype=BF, name="mm_d_c3")
    d_c1, d_g_ln, d_b_ln, d_b_dw = ln_silu_bwd(c1, d_c3, g_conv_ln, b_conv_ln, tr=tr, name="ln_silu_bwd")
    d_c0, d_w_dw = dwconv_bwd(c0, d_c1, w_dw_full, tc=tcv, rc=rc, name="dwconv_bwd")
    d_zglu, d_b_glu = glu_bwd(z_glu, b_glu, d_c0, tr=tr, name="glu_bwd")

    dW_ao = mm(o_attn, d_ya, mode="tn", tm=_pick(H * V_HEAD, (1024, 512)), tn=tn_of(D, 1024), tk=T, out_dtype=BF, name="mm_dw_attn_out")
    d_o = mm(d_ya, W_ao, mode="nt", tm=T, tn=tn_of(H * V_HEAD, 1024), tk=D, out_dtype=BF, name="mm_d_o")
    d_q2, d_k2, d_v, d_kr = attn_bwd(q, kk, vv, o_attn, d_o, lse, cos, sin, scale=scale, nch=nch, name="attn_bwd")

    dWq_p = mm(cq, d_q2, mode="tn", tm=R, tn=tn_of(H * HEAD_PAD), tk=T, out_dtype=BF, name="mm_dw_uq")
    d_cq = mm(d_q2, Wq_p, mode="nt", tm=T, tn=R, tk=_pick(H * HEAD_PAD, (1024,)), out_dtype=F32, name="mm_d_cq")
    Wk_p = jnp.pad(W_uk.reshape(R, H, QK_NOPE), ((0, 0), (0, 0), (0, HEAD_PAD - QK_NOPE))).reshape(R, H * HEAD_PAD)
    dWk_p = mm(ckv, d_k2, mode="tn", tm=R, tn=tn_of(H * HEAD_PAD), tk=T, out_dtype=BF, name="mm_dw_uk")
    dW_uv = mm(ckv, d_v, mode="tn", tm=R, tn=tn_of(H * V_HEAD), tk=T, out_dtype=BF, name="mm_dw_uv")
    sh_c = rs_end(st_c, [dWq_p, dWk_p, dW_uv, dW_co, dW_ao, dW_o], "c")
    dW_uq = dWq_p.reshape(R, H, HEAD_PAD)[:, :, :QK_NOPE + QK_ROPE].reshape(R, H * (QK_NOPE + QK_ROPE))
    dW_uk = dWk_p.reshape(R, H, HEAD_PAD)[:, :, :QK_NOPE].reshape(R, H * QK_NOPE)
    st_b, tok_rb = rs_begin(grp_b, [dW_co.reshape(N_CHIPS, C // N_CHIPS, D), col_blocks(dW_uq), col_blocks(dW_uk),
                                    col_blocks(dW_uv), dW_ao.reshape(N_CHIPS, H * V_HEAD // N_CHIPS, D),
                                    dW_o.reshape(N_CHIPS, D // N_CHIPS, D)], sh_c[-1], "b")
    d_ckv = mm(d_k2, Wk_p, mode="nt", tm=T, tn=R, tk=_pick(H * HEAD_PAD, (1024,)), out_dtype=F32, name="mm_d_ckv_k",
               deps=[tok_rb])
    d_ckv = mm(d_v, W_uv, mode="nt", tm=T, tn=R, tk=_pick(H * V_HEAD, (1024,)), out_dtype=F32, name="mm_d_ckv_v",
               res=d_ckv)
    d_zsmall, d_g_q, d_g_kv = lora_norm_bwd(z_small, d_cq, d_ckv, d_kr, cos, sin, g_q_lora, g_kv_lora, tr=tr,
                                            name="lora_norm_bwd")
    st_b, tok_rb = rs_middle(st_b, d_zsmall, "b")

    tmi = _pick(2 * C, (1024, 512))
    dWT_in = lax.empty((WT_in.shape[0], D), BF)
    dWT_in = mm_tn_into(d_zglu, u, dWT_in, 0, tm=tmi, tn=tn_of(D, 1024), name="mm_dw_glu", deps=[tok_rb])
    dWT_in = mm_tn_into(d_zsmall, u, dWT_in, o1, tm=2 * R + 128, tn=tn_of(D), name="mm_dw_small")
    dWT_in = mm_tn_into(d_zgate, u, dWT_in, o2, tm=tmi, tn=tn_of(D, 1024), name="mm_dw_gate")
    sh_b = rs_end(st_b, dWT_in, "b")
    sh_c, ss_sc, rs_sc, tok_sc = copies_start(sh_c, share_plan(ax_c), len(grp_c), sh_b[-1], name="rs_c_share_start")
    st_a, tok_ra = rs_begin(grp_a, [dWT_in.reshape(N_CHIPS, dWT_in.shape[0] // N_CHIPS, D)], tok_sc, "a")
    d_u = mm(d_zglu, WT_in, mode="nn", tm=T, tn=tn_of(D, 256), tk=2 * C, out_dtype=F32, name="mm_d_u_glu",
             deps=[tok_ra])
    st_a, tok_ra = rs_middle(st_a, d_u, "a")
    d_u = mm(d_zsmall, WT_small, mode="nn", tm=T, tn=tn_of(D), tk=2 * R + 128, out_dtype=F32, name="mm_d_u_small",
             res=d_u, deps=[tok_ra])
    d_u = mm(d_zgate, WT_gate, mode="nn", tm=T, tn=tn_of(D, 256), tk=2 * D, out_dtype=F32,
             name="mm_d_u_gate", res=d_u)
    d_h0, _, d_g_mix = rms_bwd(d_u, h0, g_mix, d_h1, tr=tr, name="rms_mix_bwd")

    grad_x = d_h0[N_META:length][None]

    small = ["g_mix", "b_glu", "b_gate", "b_dw", "g_conv_ln", "b_conv_ln", "b_conv_out", "g_q_lora", "g_kv_lora",
             "g_ffn", "g_final", "w_dw", "meta_tokens"]
    sgrads = dict(g_mix=d_g_mix, b_glu=d_b_glu, b_gate=d_b_gate, b_dw=d_b_dw, g_conv_ln=d_g_ln, b_conv_ln=d_b_ln,
                  b_conv_out=d_b_co, g_q_lora=d_g_q, g_kv_lora=d_g_kv, g_ffn=d_g_ffn, g_final=d_g_final,
                  w_dw=d_w_dw[:CONV_WIDTH], meta_tokens=d_h0[:N_META])
    sizes = [int(np.prod(sgrads[nm].shape)) for nm in small]
    packed = jnp.concatenate([sgrads[nm].reshape(-1) for nm in small]).reshape(-1, 128)
    sm_arrs, sm_ss, sm_rs, tok_sm = copies_start([packed, lax.empty((N_DEV,) + packed.shape, F32)], devices_plan,
                                                 N_DEV - 1, packed, name="gather_small_start")

    grads, delta, new_m, new_v, delta_2d = {}, {}, {}, {}, {}

    def adamw_2d(nm, w2, g2, m2, v2, back):
        g_, d_, m_, v_ = adamw(w2, g2, m2, v2, name="adamw_" + nm)
        delta_2d[nm] = d_
        grads[nm], delta[nm], new_m[nm], new_v[nm] = back(g_), back(d_), back(m_), back(v_)

    reduced = dict(zip(grp_c, copies_wait(sh_c, share_plan(ax_c), ss_sc, rs_sc, tok_sm, name="rs_c_share_wait")))
    reduced.update(rs_shared(grp_b, sh_b, tok_sm, "b"))
    for nm in grp_c + grp_b + grp_a:
        if nm == grp_a[0]:
            reduced.update(rs_shared(grp_a, rs_end(st_a, [delta_2d[k] for k in grp_c + grp_b], "a"), half_arr, "a"))
        shp = weights[nm].shape
        if axes[nm] == 1:
            adamw_2d(nm, shard[nm], reduced[nm], jnp.swapaxes(m_in[nm][0], 0, 1), jnp.swapaxes(v_in[nm][0], 0, 1),
                     lambda a: jnp.swapaxes(a, 0, 1)[None])
        else:
            two = (shp[-2], shp[-1])
            adamw_2d(nm, weights[nm].reshape(two), reduced[nm], m_in[nm].reshape(two), v_in[nm].reshape(two),
                     lambda a, shp=shp: a.reshape(shp))

    sm_arrs = copies_wait(sm_arrs, devices_plan, sm_ss, sm_rs, delta_2d[grp_a[0]], name="gather_small_wait")
    me = (4 * xi + 2 * yi + ci).astype(jnp.int32)
    parts = lax.dynamic_update_slice(sm_arrs[1], sm_arrs[0][None], (me, 0, 0))
    summed = sum_devices(parts, name="sum_small_grads").reshape(-1)
    offs = np.concatenate([[0], np.cumsum(sizes)])
    sfull = {nm: summed[int(offs[i]):int(offs[i + 1])].reshape(sgrads[nm].shape) for i, nm in enumerate(small)}
    for nm in small:
        gfull = sfull[nm]
        if nm == "w_dw":
            cb = C // N_CHIPS
            grads[nm] = lax.dynamic_slice(gfull, (0, chip * cb), (CONV_WIDTH, cb))[None]
        elif nm == "meta_tokens":
            cb = D // N_CHIPS
            grads[nm] = lax.dynamic_slice(gfull, (0, chip * cb), (N_META, cb))
        else:
            grads[nm] = gfull.reshape(weights[nm].shape)

    rep = [nm for nm in small if nm not in ("w_dw", "meta_tokens")]

    def pack(d):
        return jnp.concatenate([d[nm].reshape(-1) for nm in rep]).reshape(-1, 128)

    _, pd, pm, pv = adamw(pack(weights), pack(grads), pack(m_in), pack(v_in), name="adamw_small")
    rsz = [int(np.prod(weights[nm].shape)) for nm in rep]
    roff = np.concatenate([[0], np.cumsum(rsz)])
    for i, nm in enumerate(rep):
        sl = slice(int(roff[i]), int(roff[i + 1]))
        delta[nm] = pd.reshape(-1)[sl].reshape(weights[nm].shape)
        new_m[nm] = pm.reshape(-1)[sl].reshape(weights[nm].shape)
        new_v[nm] = pv.reshape(-1)[sl].reshape(weights[nm].shape)
    for nm in ["w_dw", "meta_tokens"]:
        shp = weights[nm].shape
        two = (shp[-2], shp[-1])
        adamw_2d(nm, weights[nm].reshape(two), grads[nm].reshape(two), m_in[nm].reshape(two), v_in[nm].reshape(two),
                 lambda a, shp=shp: a.reshape(shp))

    return (loss, grad_x, *[grads[nm] for nm in names], *[delta[nm] for nm in names],
            *[new_m[nm] for nm in names], *[new_v[nm] for nm in names])
```
